```python
import jax, jax.numpy as jnp
from jax import lax
import numpy as np

D_MODEL = 1024
BATCH = 8
SEQ = 8192
DEPTH = 1

SSD_HEADS = 16
SSD_HEAD_DIM = 64
SSD_WIDTH = SSD_HEADS * SSD_HEAD_DIM
SSD_GROUPS = 2
SSD_STATE = 128
SSD_CONV = 4
SSD_XBC_WIDTH = SSD_WIDTH + 2 * SSD_GROUPS * SSD_STATE
RET_HEADS = 8
RET_QK_DIM = 64
RET_V_DIM = 128
RET_QK_WIDTH = RET_HEADS * RET_QK_DIM
RET_V_WIDTH = RET_HEADS * RET_V_DIM
ROPE_BASE = 10000.0
CHUNK = 128
MIX_WIDTH = SSD_WIDTH + RET_V_WIDTH
IN_WIDTH = SSD_WIDTH + SSD_XBC_WIDTH + SSD_HEADS + 2 * RET_QK_WIDTH + 2 * RET_V_WIDTH
D_FF = 2816
FFN_CONV = 3
EPS = 1e-6

kernel_name = "hymba_ssd_retention_convffn_block"


def rmsnorm(x, w):
    xf = x.astype(jnp.float32)
    y = xf * lax.rsqrt(jnp.mean(xf * xf, axis=-1, keepdims=True) + EPS)
    return (y * w.astype(jnp.float32)).astype(x.dtype)


def causal_dwconv(x, w, b):
    k = w.shape[0]
    y = lax.conv_general_dilated(
        x, w[:, None, :].astype(x.dtype), window_strides=(1,), padding=[(k - 1, 0)],
        dimension_numbers=("NWC", "WIO", "NWC"), feature_group_count=x.shape[-1])
    return y + b.astype(x.dtype)


def rotary(t):
    s, d = t.shape[1], t.shape[-1]
    inv = ROPE_BASE ** (-jnp.arange(0, d, 2, dtype=jnp.float32) / d)
    ang = jnp.arange(s, dtype=jnp.float32)[:, None] * inv[None, :]
    cos = jnp.cos(ang)[None, :, None, :]
    sin = jnp.sin(ang)[None, :, None, :]
    tf = t.astype(jnp.float32)
    t1, t2 = jnp.split(tf, 2, axis=-1)
    return jnp.concatenate([t1 * cos - t2 * sin, t2 * cos + t1 * sin], axis=-1).astype(t.dtype)


def ssd_mixer(z, xbc, dt_raw, conv_w, conv_b, dt_bias, a_log, d_skip, norm_w):
    b, s, _ = z.shape
    nc = s // CHUNK
    hpg = SSD_HEADS // SSD_GROUPS
    xbc = jax.nn.silu(causal_dwconv(xbc, conv_w, conv_b))
    xs, bm, cm = jnp.split(xbc, [SSD_WIDTH, SSD_WIDTH + SSD_GROUPS * SSD_STATE], axis=-1)
    xs = xs.reshape(b, nc, CHUNK, SSD_GROUPS, hpg, SSD_HEAD_DIM)
    bm = bm.reshape(b, nc, CHUNK, SSD_GROUPS, SSD_STATE)
    cm = cm.reshape(b, nc, CHUNK, SSD_GROUPS, SSD_STATE)
    dt = jax.nn.softplus(dt_raw.astype(jnp.float32) + dt_bias.astype(jnp.float32))
    a = -jnp.exp(a_log.astype(jnp.float32))
    dt_c = dt.reshape(b, nc, CHUNK, SSD_GROUPS, hpg)
    da = jnp.transpose((dt * a).reshape(b, nc, CHUNK, SSD_GROUPS, hpg), (0, 1, 3, 4, 2))
    acs = jnp.cumsum(da, axis=-1)
    xdt = xs * dt_c[..., None].astype(xs.dtype)
    causal = jnp.tril(jnp.ones((CHUNK, CHUNK), dtype=bool))
    seg = acs[..., :, None] - acs[..., None, :]
    lmat = jnp.exp(jnp.where(causal, seg, -jnp.inf)).astype(xs.dtype)
    cb = jnp.einsum("bclgn,bcsgn->bcgls", cm, bm)
    y_diag = jnp.einsum("bcgkls,bcsgkp->bclgkp", cb[:, :, :, None] * lmat, xdt)
    decay_states = jnp.exp(acs[..., -1:] - acs).astype(xs.dtype)
    states = jnp.einsum("bclgn,bcgkl,bclgkp->bcgkpn", bm, decay_states, xdt)
    chunk_decay = jnp.exp(acs[..., -1]).astype(xs.dtype)

    def step(h, inp):
        st, dec = inp
        return (h * dec[..., None, None] + st).astype(h.dtype), h

    h0 = jnp.zeros_like(states[:, 0])
    _, prev = lax.scan(step, h0, (jnp.moveaxis(states, 1, 0), jnp.moveaxis(chunk_decay, 1, 0)))
    prev = jnp.moveaxis(prev, 0, 1)
    y_off = jnp.einsum("bclgn,bcgkpn,bcgkl->bclgkp", cm, prev, jnp.exp(acs).astype(xs.dtype))
    y = y_diag + y_off + xs * d_skip.reshape(SSD_GROUPS, hpg, 1).astype(xs.dtype)
    y = y.reshape(b, s, SSD_WIDTH)
    return rmsnorm(y * jax.nn.silu(z), norm_w)


def retention_mixer(q, k, v, g, norm_w):
    b, s, _ = q.shape
    nc = s // CHUNK
    q = rotary(q.reshape(b, s, RET_HEADS, RET_QK_DIM))
    k = rotary(k.reshape(b, s, RET_HEADS, RET_QK_DIM)) * (RET_QK_DIM ** -0.5)
    v = v.reshape(b, s, RET_HEADS, RET_V_DIM)
    log_gamma = jnp.log1p(-jnp.exp2(-5.0 - jnp.arange(RET_HEADS, dtype=jnp.float32)))
    pos = jnp.arange(CHUNK, dtype=jnp.float32)
    rel = pos[:, None] - pos[None, :]
    dmask = jnp.where(rel >= 0, jnp.exp(log_gamma[:, None, None] * jnp.maximum(rel, 0.0)), 0.0).astype(q.dtype)
    qc = q.reshape(b, nc, CHUNK, RET_HEADS, RET_QK_DIM)
    kc = k.reshape(b, nc, CHUNK, RET_HEADS, RET_QK_DIM)
    vc = v.reshape(b, nc, CHUNK, RET_HEADS, RET_V_DIM)
    scores = jnp.einsum("bclhd,bcshd->bchls", qc, kc) * dmask
    inner = jnp.einsum("bchls,bcshe->bclhe", scores, vc)
    k_decay = jnp.exp(log_gamma[:, None] * (CHUNK - 1 - pos)[None, :]).astype(q.dtype)
    chunk_kv = jnp.einsum("bclhd,hl,bclhe->bchde", kc, k_decay, vc)
    chunk_gamma = jnp.exp(log_gamma * CHUNK).astype(q.dtype)

    def step(r, kv):
        return (r * chunk_gamma[:, None, None] + kv).astype(r.dtype), r

    r0 = jnp.zeros_like(chunk_kv[:, 0])
    _, prev = lax.scan(step, r0, jnp.moveaxis(chunk_kv, 1, 0))
    prev = jnp.moveaxis(prev, 0, 1)
    q_decay = jnp.exp(log_gamma[:, None] * (pos + 1.0)[None, :]).astype(q.dtype)
    cross = jnp.einsum("bclhd,hl,bchde->bclhe", qc, q_decay, prev)
    o = (inner + cross).reshape(b, s, RET_HEADS, RET_V_DIM)
    of = o.astype(jnp.float32)
    of = of * lax.rsqrt(jnp.mean(of * of, axis=-1, keepdims=True) + EPS)
    o = (of.reshape(b, s, RET_V_WIDTH) * norm_w.astype(jnp.float32)).astype(q.dtype)
    return o * jax.nn.silu(g)


def _fwd_setup_inputs(seed: int = 0) -> dict:
    key = jax.random.key(seed)
    ks = jax.random.split(key, 24)
    f32 = jnp.float32

    def nrm(k, shape, scale):
        return jax.random.normal(k, shape, f32) * scale

    def gain(k, shape):
        return 1.0 + 0.02 * jax.random.normal(k, shape, f32)

    dt0 = jnp.exp(jax.random.uniform(ks[6], (DEPTH, SSD_HEADS), f32, np.log(1e-3), np.log(1e-1)))
    return {
        "x": jax.random.normal(ks[0], (BATCH, SEQ, D_MODEL), f32),
        "pre_mix_norm_w": gain(ks[1], (DEPTH, D_MODEL)),
        "w_in": nrm(ks[2], (DEPTH, D_MODEL, IN_WIDTH), D_MODEL ** -0.5),
        "ssd_conv_w": nrm(ks[3], (DEPTH, SSD_CONV, SSD_XBC_WIDTH), SSD_CONV ** -0.5),
        "ssd_conv_b": nrm(ks[4], (DEPTH, SSD_XBC_WIDTH), 0.01),
        "ssd_dt_bias": dt0 + jnp.log(-jnp.expm1(-dt0)),
        "ssd_a_log": jnp.log(jax.random.uniform(ks[7], (DEPTH, SSD_HEADS), f32, 1.0, 16.0)),
        "ssd_d": gain(ks[8], (DEPTH, SSD_HEADS)),
        "ssd_norm_w": gain(ks[9], (DEPTH, SSD_WIDTH)),
        "ret_norm_w": gain(ks[10], (DEPTH, RET_V_WIDTH)),
        "w_out": nrm(ks[11], (DEPTH, MIX_WIDTH, D_MODEL), MIX_WIDTH ** -0.5),
        "post_mix_norm_w": gain(ks[12], (DEPTH, D_MODEL)),
        "pre_ffn_norm_w": gain(ks[13], (DEPTH, D_MODEL)),
        "w_up": nrm(ks[14], (DEPTH, D_MODEL, 2 * D_FF), D_MODEL ** -0.5),
        "ffn_conv_w": nrm(ks[15], (DEPTH, FFN_CONV, D_FF), FFN_CONV ** -0.5),
        "ffn_conv_b": nrm(ks[16], (DEPTH, D_FF), 0.01),
        "w_down": nrm(ks[17], (DEPTH, D_FF, D_MODEL), D_FF ** -0.5),
        "post_ffn_norm_w": gain(ks[18], (DEPTH, D_MODEL)),
    }


def _fwd_reference(x, pre_mix_norm_w, w_in, ssd_conv_w, ssd_conv_b, ssd_dt_bias, ssd_a_log, ssd_d,
              ssd_norm_w, ret_norm_w, w_out, post_mix_norm_w, pre_ffn_norm_w, w_up,
              ffn_conv_w, ffn_conv_b, w_down, post_ffn_norm_w):
    splits = np.cumsum([SSD_WIDTH, SSD_XBC_WIDTH, SSD_HEADS, RET_QK_WIDTH, RET_QK_WIDTH, RET_V_WIDTH]).tolist()
    for l in range(DEPTH):
        h = rmsnorm(x, pre_mix_norm_w[l])
        proj = h @ w_in[l]
        z, xbc, dt_raw, q, k, v, g = jnp.split(proj, splits, axis=-1)
        y_ssd = ssd_mixer(z, xbc, dt_raw, ssd_conv_w[l], ssd_conv_b[l], ssd_dt_bias[l],
                          ssd_a_log[l], ssd_d[l], ssd_norm_w[l])
        y_ret = retention_mixer(q, k, v, g, ret_norm_w[l])
        y = jnp.concatenate([y_ssd, y_ret], axis=-1) @ w_out[l]
        x = x + rmsnorm(y, post_mix_norm_w[l])
        h = rmsnorm(x, pre_ffn_norm_w[l])
        gate, val = jnp.split(h @ w_up[l], 2, axis=-1)
        gate = causal_dwconv(gate, ffn_conv_w[l], ffn_conv_b[l])
        f = (jax.nn.silu(gate) * val) @ w_down[l]
        x = x + rmsnorm(f, post_ffn_norm_w[l])
    return x


import jax as _jax
import jax.numpy as _jnp

TWIN_FORMAT = 'train_step'
FWD_PARAMS = ['x', 'pre_mix_norm_w', 'w_in', 'ssd_conv_w', 'ssd_conv_b', 'ssd_dt_bias', 'ssd_a_log', 'ssd_d', 'ssd_norm_w', 'ret_norm_w', 'w_out', 'post_mix_norm_w', 'pre_ffn_norm_w', 'w_up', 'ffn_conv_w', 'ffn_conv_b', 'w_down', 'post_ffn_norm_w']
TWIN_WEIGHTS = ['pre_mix_norm_w', 'w_in', 'ssd_conv_w', 'ssd_conv_b', 'ssd_dt_bias', 'ssd_a_log', 'ssd_d', 'ssd_norm_w', 'ret_norm_w', 'w_out', 'post_mix_norm_w', 'pre_ffn_norm_w', 'w_up', 'ffn_conv_w', 'ffn_conv_b', 'w_down', 'post_ffn_norm_w']
TWIN_DIFF_INPUT = 'x'
TWIN_INPUTS = ['x', 'pre_mix_norm_w', 'w_in', 'ssd_conv_w', 'ssd_conv_b', 'ssd_dt_bias', 'ssd_a_log', 'ssd_d', 'ssd_norm_w', 'ret_norm_w', 'w_out', 'post_mix_norm_w', 'pre_ffn_norm_w', 'w_up', 'ffn_conv_w', 'ffn_conv_b', 'w_down', 'post_ffn_norm_w', 'loss_target', 'm_pre_mix_norm_w', 'm_w_in', 'm_ssd_conv_w', 'm_ssd_conv_b', 'm_ssd_dt_bias', 'm_ssd_a_log', 'm_ssd_d', 'm_ssd_norm_w', 'm_ret_norm_w', 'm_w_out', 'm_post_mix_norm_w', 'm_pre_ffn_norm_w', 'm_w_up', 'm_ffn_conv_w', 'm_ffn_conv_b', 'm_w_down', 'm_post_ffn_norm_w', 'v_pre_mix_norm_w', 'v_w_in', 'v_ssd_conv_w', 'v_ssd_conv_b', 'v_ssd_dt_bias', 'v_ssd_a_log', 'v_ssd_d', 'v_ssd_norm_w', 'v_ret_norm_w', 'v_w_out', 'v_post_mix_norm_w', 'v_pre_ffn_norm_w', 'v_w_up', 'v_ffn_conv_w', 'v_ffn_conv_b', 'v_w_down', 'v_post_ffn_norm_w']
TWIN_OUTPUTS = ['loss', 'grad_x', 'grad_pre_mix_norm_w', 'grad_w_in', 'grad_ssd_conv_w', 'grad_ssd_conv_b', 'grad_ssd_dt_bias', 'grad_ssd_a_log', 'grad_ssd_d', 'grad_ssd_norm_w', 'grad_ret_norm_w', 'grad_w_out', 'grad_post_mix_norm_w', 'grad_pre_ffn_norm_w', 'grad_w_up', 'grad_ffn_conv_w', 'grad_ffn_conv_b', 'grad_w_down', 'grad_post_ffn_norm_w', 'delta_pre_mix_norm_w', 'delta_w_in', 'delta_ssd_conv_w', 'delta_ssd_conv_b', 'delta_ssd_dt_bias', 'delta_ssd_a_log', 'delta_ssd_d', 'delta_ssd_norm_w', 'delta_ret_norm_w', 'delta_w_out', 'delta_post_mix_norm_w', 'delta_pre_ffn_norm_w', 'delta_w_up', 'delta_ffn_conv_w', 'delta_ffn_conv_b', 'delta_w_down', 'delta_post_ffn_norm_w', 'new_m_pre_mix_norm_w', 'new_m_w_in', 'new_m_ssd_conv_w', 'new_m_ssd_conv_b', 'new_m_ssd_dt_bias', 'new_m_ssd_a_log', 'new_m_ssd_d', 'new_m_ssd_norm_w', 'new_m_ret_norm_w', 'new_m_w_out', 'new_m_post_mix_norm_w', 'new_m_pre_ffn_norm_w', 'new_m_w_up', 'new_m_ffn_conv_w', 'new_m_ffn_conv_b', 'new_m_w_down', 'new_m_post_ffn_norm_w', 'new_v_pre_mix_norm_w', 'new_v_w_in', 'new_v_ssd_conv_w', 'new_v_ssd_conv_b', 'new_v_ssd_dt_bias', 'new_v_ssd_a_log', 'new_v_ssd_d', 'new_v_ssd_norm_w', 'new_v_ret_norm_w', 'new_v_w_out', 'new_v_post_mix_norm_w', 'new_v_pre_ffn_norm_w', 'new_v_w_up', 'new_v_ffn_conv_w', 'new_v_ffn_conv_b', 'new_v_w_down', 'new_v_post_ffn_norm_w']
TWIN_LEAF_KINDS = {'loss': 'loss', 'grad_x': 'grad_x', 'grad_pre_mix_norm_w': 'grad_w', 'grad_w_in': 'grad_w', 'grad_ssd_conv_w': 'grad_w', 'grad_ssd_conv_b': 'grad_w', 'grad_ssd_dt_bias': 'grad_w', 'grad_ssd_a_log': 'grad_w', 'grad_ssd_d': 'grad_w', 'grad_ssd_norm_w': 'grad_w', 'grad_ret_norm_w': 'grad_w', 'grad_w_out': 'grad_w', 'grad_post_mix_norm_w': 'grad_w', 'grad_pre_ffn_norm_w': 'grad_w', 'grad_w_up': 'grad_w', 'grad_ffn_conv_w': 'grad_w', 'grad_ffn_conv_b': 'grad_w', 'grad_w_down': 'grad_w', 'grad_post_ffn_norm_w': 'grad_w', 'delta_pre_mix_norm_w': 'delta_w', 'delta_w_in': 'delta_w', 'delta_ssd_conv_w': 'delta_w', 'delta_ssd_conv_b': 'delta_w', 'delta_ssd_dt_bias': 'delta_w', 'delta_ssd_a_log': 'delta_w', 'delta_ssd_d': 'delta_w', 'delta_ssd_norm_w': 'delta_w', 'delta_ret_norm_w': 'delta_w', 'delta_w_out': 'delta_w', 'delta_post_mix_norm_w': 'delta_w', 'delta_pre_ffn_norm_w': 'delta_w', 'delta_w_up': 'delta_w', 'delta_ffn_conv_w': 'delta_w', 'delta_ffn_conv_b': 'delta_w', 'delta_w_down': 'delta_w', 'delta_post_ffn_norm_w': 'delta_w', 'new_m_pre_mix_norm_w': 'new_m', 'new_m_w_in': 'new_m', 'new_m_ssd_conv_w': 'new_m', 'new_m_ssd_conv_b': 'new_m', 'new_m_ssd_dt_bias': 'new_m', 'new_m_ssd_a_log': 'new_m', 'new_m_ssd_d': 'new_m', 'new_m_ssd_norm_w': 'new_m', 'new_m_ret_norm_w': 'new_m', 'new_m_w_out': 'new_m', 'new_m_post_mix_norm_w': 'new_m', 'new_m_pre_ffn_norm_w': 'new_m', 'new_m_w_up': 'new_m', 'new_m_ffn_conv_w': 'new_m', 'new_m_ffn_conv_b': 'new_m', 'new_m_w_down': 'new_m', 'new_m_post_ffn_norm_w': 'new_m', 'new_v_pre_mix_norm_w': 'new_v', 'new_v_w_in': 'new_v', 'new_v_ssd_conv_w': 'new_v', 'new_v_ssd_conv_b': 'new_v', 'new_v_ssd_dt_bias': 'new_v', 'new_v_ssd_a_log': 'new_v', 'new_v_ssd_d': 'new_v', 'new_v_ssd_norm_w': 'new_v', 'new_v_ret_norm_w': 'new_v', 'new_v_w_out': 'new_v', 'new_v_post_mix_norm_w': 'new_v', 'new_v_pre_ffn_norm_w': 'new_v', 'new_v_w_up': 'new_v', 'new_v_ffn_conv_w': 'new_v', 'new_v_ffn_conv_b': 'new_v', 'new_v_w_down': 'new_v', 'new_v_post_ffn_norm_w': 'new_v'}


def _forward(args):
    return _fwd_reference(*[args[k] for k in FWD_PARAMS])


def _output_shape():
    def fwd():
        inp = _fwd_setup_inputs(0)
        return _fwd_reference(*[inp[k] for k in FWD_PARAMS])
    out = _jax.eval_shape(fwd)
    return out.shape, out.dtype

N_MICROBATCH = 1
ADAM_LR = 0.001
ADAM_B1 = 0.9
ADAM_B2 = 0.999
ADAM_EPS = 1e-08
ADAM_WD = 0.01
ADAM_STEP = 10
PER_EXAMPLE_BATCH_AXIS = {'x': 0, 'loss_target': 0}
SHARED_INPUTS = []
_WEIGHT_DTYPES = {'pre_mix_norm_w': _jnp.float32, 'w_in': _jnp.float32, 'ssd_conv_w': _jnp.float32, 'ssd_conv_b': _jnp.float32, 'ssd_dt_bias': _jnp.float32, 'ssd_a_log': _jnp.float32, 'ssd_d': _jnp.float32, 'ssd_norm_w': _jnp.float32, 'ret_norm_w': _jnp.float32, 'w_out': _jnp.float32, 'post_mix_norm_w': _jnp.float32, 'pre_ffn_norm_w': _jnp.float32, 'w_up': _jnp.float32, 'ffn_conv_w': _jnp.float32, 'ffn_conv_b': _jnp.float32, 'w_down': _jnp.float32, 'post_ffn_norm_w': _jnp.float32}
MOMENT_SCALE = {'pre_mix_norm_w': 1.209642e+00, 'w_in': 4.916491e-01, 'ssd_conv_w': 6.747695e-01, 'ssd_conv_b': 1.745548e+00, 'ssd_dt_bias': 3.763696e+00, 'ssd_a_log': 6.766857e+00, 'ssd_d': 4.475251e+00, 'ssd_norm_w': 1.144450e+00, 'ret_norm_w': 3.714801e-01, 'w_out': 1.122515e+00, 'post_mix_norm_w': 6.434369e+01, 'pre_ffn_norm_w': 1.146682e+00, 'w_up': 4.737860e-01, 'ffn_conv_w': 5.284224e-01, 'ffn_conv_b': 1.555426e+00, 'w_down': 1.050314e+00, 'post_ffn_norm_w': 6.413129e+01}


def _to_microbatches(a, axis):
    t = _jnp.moveaxis(a, axis, 0)
    t = t.reshape((N_MICROBATCH, t.shape[0] // N_MICROBATCH) + t.shape[1:])
    return _jnp.moveaxis(t, 1, axis + 1)


def setup_inputs(seed: int = 0) -> dict:
    inp = _fwd_setup_inputs(seed)
    key = _jax.random.fold_in(_jax.random.key(seed), 7919)
    shape, _ = _output_shape()
    out = dict(inp)
    out["loss_target"] = _jax.random.normal(_jax.random.fold_in(key, 0), shape, _jnp.float32)
    for i, name in enumerate(TWIN_WEIGHTS):
        w = inp[name].astype(_jnp.float32)
        if MOMENT_SCALE is None:
            s = _jnp.sqrt(_jnp.mean(_jnp.square(w)) + 1e-30)
        else:
            s = MOMENT_SCALE[name]
        km, kv = _jax.random.split(_jax.random.fold_in(key, i + 1))
        out[name] = w
        out["m_" + name] = s * _jax.random.normal(km, w.shape, _jnp.float32)
        out["v_" + name] = (s * s) * _jax.random.uniform(kv, w.shape, _jnp.float32, 0.5, 1.5)
    if N_MICROBATCH > 1:
        for name, axis in PER_EXAMPLE_BATCH_AXIS.items():
            out[name] = _to_microbatches(out[name], axis)
    return {'x': out['x'], 'pre_mix_norm_w': out['pre_mix_norm_w'], 'w_in': out['w_in'], 'ssd_conv_w': out['ssd_conv_w'], 'ssd_conv_b': out['ssd_conv_b'], 'ssd_dt_bias': out['ssd_dt_bias'], 'ssd_a_log': out['ssd_a_log'], 'ssd_d': out['ssd_d'], 'ssd_norm_w': out['ssd_norm_w'], 'ret_norm_w': out['ret_norm_w'], 'w_out': out['w_out'], 'post_mix_norm_w': out['post_mix_norm_w'], 'pre_ffn_norm_w': out['pre_ffn_norm_w'], 'w_up': out['w_up'], 'ffn_conv_w': out['ffn_conv_w'], 'ffn_conv_b': out['ffn_conv_b'], 'w_down': out['w_down'], 'post_ffn_norm_w': out['post_ffn_norm_w'], 'loss_target': out['loss_target'], 'm_pre_mix_norm_w': out['m_pre_mix_norm_w'], 'm_w_in': out['m_w_in'], 'm_ssd_conv_w': out['m_ssd_conv_w'], 'm_ssd_conv_b': out['m_ssd_conv_b'], 'm_ssd_dt_bias': out['m_ssd_dt_bias'], 'm_ssd_a_log': out['m_ssd_a_log'], 'm_ssd_d': out['m_ssd_d'], 'm_ssd_norm_w': out['m_ssd_norm_w'], 'm_ret_norm_w': out['m_ret_norm_w'], 'm_w_out': out['m_w_out'], 'm_post_mix_norm_w': out['m_post_mix_norm_w'], 'm_pre_ffn_norm_w': out['m_pre_ffn_norm_w'], 'm_w_up': out['m_w_up'], 'm_ffn_conv_w': out['m_ffn_conv_w'], 'm_ffn_conv_b': out['m_ffn_conv_b'], 'm_w_down': out['m_w_down'], 'm_post_ffn_norm_w': out['m_post_ffn_norm_w'], 'v_pre_mix_norm_w': out['v_pre_mix_norm_w'], 'v_w_in': out['v_w_in'], 'v_ssd_conv_w': out['v_ssd_conv_w'], 'v_ssd_conv_b': out['v_ssd_conv_b'], 'v_ssd_dt_bias': out['v_ssd_dt_bias'], 'v_ssd_a_log': out['v_ssd_a_log'], 'v_ssd_d': out['v_ssd_d'], 'v_ssd_norm_w': out['v_ssd_norm_w'], 'v_ret_norm_w': out['v_ret_norm_w'], 'v_w_out': out['v_w_out'], 'v_post_mix_norm_w': out['v_post_mix_norm_w'], 'v_pre_ffn_norm_w': out['v_pre_ffn_norm_w'], 'v_w_up': out['v_w_up'], 'v_ffn_conv_w': out['v_ffn_conv_w'], 'v_ffn_conv_b': out['v_ffn_conv_b'], 'v_w_down': out['v_w_down'], 'v_post_ffn_norm_w': out['v_post_ffn_norm_w']}


def _loss(weights, diff, rest, loss_target):
    with _jax.named_scope("forward"):
        args = {**rest, TWIN_DIFF_INPUT: diff, **{k: w.astype(_WEIGHT_DTYPES[k]) for k, w in weights.items()}}
        y = _forward(args)
    with _jax.named_scope("loss_head"):
        err = _jnp.square(y.astype(_jnp.float32) - loss_target)
        return 0.5 * _jnp.sum(_jnp.mean(err, axis=-1)) if err.ndim else 0.5 * err


def _adamw(w, g, m, v):
    m = ADAM_B1 * m + (1.0 - ADAM_B1) * g
    v = ADAM_B2 * v + (1.0 - ADAM_B2) * _jnp.square(g)
    m_hat = m / (1.0 - ADAM_B1 ** ADAM_STEP)
    v_hat = v / (1.0 - ADAM_B2 ** ADAM_STEP)
    delta = -ADAM_LR * (m_hat / (_jnp.sqrt(v_hat) + ADAM_EPS) + ADAM_WD * w)
    return delta, m, v


def reference(x, pre_mix_norm_w, w_in, ssd_conv_w, ssd_conv_b, ssd_dt_bias, ssd_a_log, ssd_d, ssd_norm_w, ret_norm_w, w_out, post_mix_norm_w, pre_ffn_norm_w, w_up, ffn_conv_w, ffn_conv_b, w_down, post_ffn_norm_w, loss_target, m_pre_mix_norm_w, m_w_in, m_ssd_conv_w, m_ssd_conv_b, m_ssd_dt_bias, m_ssd_a_log, m_ssd_d, m_ssd_norm_w, m_ret_norm_w, m_w_out, m_post_mix_norm_w, m_pre_ffn_norm_w, m_w_up, m_ffn_conv_w, m_ffn_conv_b, m_w_down, m_post_ffn_norm_w, v_pre_mix_norm_w, v_w_in, v_ssd_conv_w, v_ssd_conv_b, v_ssd_dt_bias, v_ssd_a_log, v_ssd_d, v_ssd_norm_w, v_ret_norm_w, v_w_out, v_post_mix_norm_w, v_pre_ffn_norm_w, v_w_up, v_ffn_conv_w, v_ffn_conv_b, v_w_down, v_post_ffn_norm_w):
    given = dict(x=x, pre_mix_norm_w=pre_mix_norm_w, w_in=w_in, ssd_conv_w=ssd_conv_w, ssd_conv_b=ssd_conv_b, ssd_dt_bias=ssd_dt_bias, ssd_a_log=ssd_a_log, ssd_d=ssd_d, ssd_norm_w=ssd_norm_w, ret_norm_w=ret_norm_w, w_out=w_out, post_mix_norm_w=post_mix_norm_w, pre_ffn_norm_w=pre_ffn_norm_w, w_up=w_up, ffn_conv_w=ffn_conv_w, ffn_conv_b=ffn_conv_b, w_down=w_down, post_ffn_norm_w=post_ffn_norm_w, loss_target=loss_target, m_pre_mix_norm_w=m_pre_mix_norm_w, m_w_in=m_w_in, m_ssd_conv_w=m_ssd_conv_w, m_ssd_conv_b=m_ssd_conv_b, m_ssd_dt_bias=m_ssd_dt_bias, m_ssd_a_log=m_ssd_a_log, m_ssd_d=m_ssd_d, m_ssd_norm_w=m_ssd_norm_w, m_ret_norm_w=m_ret_norm_w, m_w_out=m_w_out, m_post_mix_norm_w=m_post_mix_norm_w, m_pre_ffn_norm_w=m_pre_ffn_norm_w, m_w_up=m_w_up, m_ffn_conv_w=m_ffn_conv_w, m_ffn_conv_b=m_ffn_conv_b, m_w_down=m_w_down, m_post_ffn_norm_w=m_post_ffn_norm_w, v_pre_mix_norm_w=v_pre_mix_norm_w, v_w_in=v_w_in, v_ssd_conv_w=v_ssd_conv_w, v_ssd_conv_b=v_ssd_conv_b, v_ssd_dt_bias=v_ssd_dt_bias, v_ssd_a_log=v_ssd_a_log, v_ssd_d=v_ssd_d, v_ssd_norm_w=v_ssd_norm_w, v_ret_norm_w=v_ret_norm_w, v_w_out=v_w_out, v_post_mix_norm_w=v_post_mix_norm_w, v_pre_ffn_norm_w=v_pre_ffn_norm_w, v_w_up=v_w_up, v_ffn_conv_w=v_ffn_conv_w, v_ffn_conv_b=v_ffn_conv_b, v_w_down=v_w_down, v_post_ffn_norm_w=v_post_ffn_norm_w)
    weights = {n: given[n] for n in TWIN_WEIGHTS}
    shared = {n: given[n] for n in SHARED_INPUTS}
    per_example = {n: given[n] for n in ['x']}
    grad_fn = _jax.value_and_grad(_loss, argnums=(0, 1))

    def one_microbatch(ex, loss_target):
        ex = dict(ex)
        diff = ex.pop(TWIN_DIFF_INPUT)
        return grad_fn(weights, diff, {**shared, **ex}, loss_target)

    if N_MICROBATCH == 1:
        loss, (grad_w, grad_x) = one_microbatch(per_example, given["loss_target"])
    else:
        def body(carry, xs):
            loss_sum, grad_sum = carry
            l_k, (gw_k, gx_k) = one_microbatch(xs[0], xs[1])
            with _jax.named_scope("update"):
                return (loss_sum + l_k, _jax.tree.map(_jnp.add, grad_sum, gw_k)), gx_k

        init = (_jnp.zeros((), _jnp.float32), _jax.tree.map(_jnp.zeros_like, weights))
        (loss, grad_w), grad_x = _jax.lax.scan(body, init, (per_example, given["loss_target"]))
    with _jax.named_scope("update"):
        delta_w, new_m, new_v = {}, {}, {}
        for n in TWIN_WEIGHTS:
            delta_w[n], new_m[n], new_v[n] = _adamw(weights[n], grad_w[n], given["m_" + n], given["v_" + n])
    return (loss, grad_x, *[grad_w[n] for n in TWIN_WEIGHTS], *[delta_w[n] for n in TWIN_WEIGHTS],
            *[new_m[n] for n in TWIN_WEIGHTS], *[new_v[n] for n in TWIN_WEIGHTS])
```

```python
import functools
import math

import numpy as np
import jax
import jax.numpy as jnp
from jax import lax
from jax.experimental import pallas as pl
from jax.experimental.pallas import tpu as pltpu

F32 = jnp.float32
BF = jnp.bfloat16
HI = lax.Precision.HIGHEST
S = jax.ShapeDtypeStruct

D_MODEL = 1024
SSD_HEADS = 16
SSD_HEAD_DIM = 64
SSD_GROUPS = 2
SSD_STATE = 128
SSD_WIDTH = 1024
SSD_XBC = 1536
SSD_CONV = 4
RET_HEADS = 8
RET_QK = 64
RET_V = 128
RET_QK_W = 512
RET_V_W = 1024
ROPE_BASE = 10000.0
CH = 128
D_FF = 2816
FFN_CONV = 3
EPS = 1e-6
IN_WIDTH = 5648
N_DEV = 8

ADAM_LR = 0.001
ADAM_B1 = 0.9
ADAM_B2 = 0.999
ADAM_EPS = 1e-08
ADAM_WD = 0.01
ADAM_STEP = 10

LANES = 128
HALO = 16
VMEM_LIMIT = 48 * 1024 * 1024

P_Z, P_XBC, P_Q, P_K, P_V, P_G, P_DT, P_END = 0, 1024, 2560, 3072, 3584, 4608, 5632, 5760
O_Z, O_XBC, O_DT, O_Q, O_K, O_V, O_G = 0, 1024, 2560, 2576, 3088, 3600, 4624


def _cparams(*sem):
    return pltpu.CompilerParams(dimension_semantics=sem, vmem_limit_bytes=VMEM_LIMIT)


def _dot(a, b):
    return jnp.dot(a.astype(BF), b.astype(BF), preferred_element_type=F32)


def _dot_nt(a, b):
    return lax.dot_general(a.astype(BF), b.astype(BF), (((1,), (1,)), ((), ())), preferred_element_type=F32)


def _dot_tn(a, b):
    return lax.dot_general(a.astype(BF), b.astype(BF), (((0,), (0,)), ((), ())), preferred_element_type=F32)


def _dot_hi(a, b):
    return jnp.dot(a, b, preferred_element_type=F32, precision=HI)


def _dot_tn_hi(a, b):
    return lax.dot_general(a, b, (((0,), (0,)), ((), ())), preferred_element_type=F32, precision=HI)


def _sigmoid(x):
    return jax.nn.sigmoid(x)


def _dsilu(x, s):
    return s * (1.0 + x * (1.0 - s))


def _softplus(x):
    return jnp.maximum(x, 0.0) + jnp.log1p(jnp.exp(-jnp.abs(x)))


def _rstd(x):
    return lax.rsqrt(jnp.mean(x * x, axis=-1, keepdims=True) + EPS)


def _rms_bwd(dy, x, r, w):
    gn = dy * w
    dx = r * gn - x * (r * r * r) * jnp.mean(gn * x, axis=-1, keepdims=True)
    dw = jnp.sum(dy * x * r, axis=0, keepdims=True)
    return dx, dw


def _rows_before(ext, s, head, n):
    if s == 0:
        return ext[head:head + n]
    return pltpu.roll(ext, s, 0)[head:head + n]


def _rows_after(ext, s, n):
    if s == 0:
        return ext[0:n]
    return pltpu.roll(ext, ext.shape[0] - s, 0)[0:n]


def _row_spec(tm, width):
    return pl.BlockSpec((tm, width), lambda i: (i, 0))


def _const_spec(shape):
    return pl.BlockSpec(shape, lambda i: (0,) * len(shape))


_VMEM_WHOLE = pl.BlockSpec(memory_space=pltpu.VMEM)


def _fwd_in(x, w0, wp, tm=256):
    T = x.shape[0]

    def body(x_ref, w0_ref, wp_ref, h_ref, z_ref, xbc_ref, q_ref, k_ref, v_ref, g_ref, dt_ref):
        xf = x_ref[...]
        h = (xf * _rstd(xf) * w0_ref[...]).astype(BF)
        h_ref[...] = h
        for ref, lo, hi in ((z_ref, P_Z, P_XBC), (xbc_ref, P_XBC, P_Q), (q_ref, P_Q, P_K), (k_ref, P_K, P_V),
                            (v_ref, P_V, P_G), (g_ref, P_G, P_DT), (dt_ref, P_DT, P_END)):
            ref[...] = jnp.dot(h, wp_ref[:, lo:hi], preferred_element_type=F32).astype(ref.dtype)

    widths = (D_MODEL, SSD_WIDTH, SSD_XBC, RET_QK_W, RET_QK_W, RET_V_W, RET_V_W)
    return pl.pallas_call(
        body, name="fwd_in", grid=(T // tm,),
        in_specs=[_row_spec(tm, D_MODEL), _const_spec((1, D_MODEL)), _VMEM_WHOLE],
        out_specs=[_row_spec(tm, w) for w in widths] + [_row_spec(tm, LANES)],
        out_shape=[S((T, w), BF) for w in widths] + [S((T, LANES), F32)],
        compiler_params=_cparams("parallel"),
    )(x, w0, wp)


def _fwd_mid(ys, yr, x, wout, wpm, wpf, wup, tm=256):
    T = x.shape[0]

    def body(ys_ref, yr_ref, x_ref, wout_ref, wpm_ref, wpf_ref, wup_ref, y_ref, x1_ref, h2_ref, graw_ref, val_ref):
        y = (jnp.dot(ys_ref[...], wout_ref[0:SSD_WIDTH, :], preferred_element_type=F32)
             + jnp.dot(yr_ref[...], wout_ref[SSD_WIDTH:, :], preferred_element_type=F32))
        y_ref[...] = y
        x1 = x_ref[...] + y * _rstd(y) * wpm_ref[...]
        x1_ref[...] = x1
        h2 = (x1 * _rstd(x1) * wpf_ref[...]).astype(BF)
        h2_ref[...] = h2
        graw_ref[...] = jnp.dot(h2, wup_ref[:, 0:D_FF], preferred_element_type=F32).astype(BF)
        val_ref[...] = jnp.dot(h2, wup_ref[:, D_FF:], preferred_element_type=F32).astype(BF)

    return pl.pallas_call(
        body, name="fwd_mid", grid=(T // tm,),
        in_specs=[_row_spec(tm, SSD_WIDTH), _row_spec(tm, RET_V_W), _row_spec(tm, D_MODEL), _VMEM_WHOLE,
                  _const_spec((1, D_MODEL)), _const_spec((1, D_MODEL)), _VMEM_WHOLE],
        out_specs=[_row_spec(tm, D_MODEL), _row_spec(tm, D_MODEL), _row_spec(tm, D_MODEL), _row_spec(tm, D_FF),
                   _row_spec(tm, D_FF)],
        out_shape=[S((T, D_MODEL), F32), S((T, D_MODEL), F32), S((T, D_MODEL), BF), S((T, D_FF), BF), S((T, D_FF), BF)],
        compiler_params=_cparams("parallel"),
    )(ys, yr, x, wout, wpm, wpf, wup)


def _ffn_tail(graw, val, x1, tgt, convw, convb, wdown, wpff, tm=256):
    T = x1.shape[0]

    def body(graw_ref, val_ref, x1_ref, tgt_ref, cw_ref, cb_ref, wd_ref, wpff_ref,
             a_ref, df_ref, dval_ref, dgate_ref, dx2_ref, loss_ref, dwpff_ref, dcb_ref, carry):
        i = pl.program_id(0)

        @pl.when(i == 0)
        def _():
            carry[...] = jnp.zeros_like(carry)
            loss_ref[...] = jnp.zeros_like(loss_ref)
            dwpff_ref[...] = jnp.zeros_like(dwpff_ref)
            dcb_ref[...] = jnp.zeros_like(dcb_ref)

        g = graw_ref[...].astype(F32)
        ext = jnp.concatenate([carry[...], g], axis=0)
        carry[...] = g[tm - 8:tm]
        gate = cb_ref[...] + sum(cw_ref[j:j + 1, :] * _rows_before(ext, FFN_CONV - 1 - j, 8, tm) for j in range(FFN_CONV))
        sg = _sigmoid(gate)
        silu = gate * sg
        v = val_ref[...].astype(F32)
        a = (silu * v).astype(BF)
        a_ref[...] = a
        f = jnp.dot(a, wd_ref[...], preferred_element_type=F32)
        r = _rstd(f)
        w = wpff_ref[...]
        e = x1_ref[...] + f * r * w - tgt_ref[...]
        loss_ref[...] += jnp.sum(e * e) * (0.5 / D_MODEL)
        dx2 = e * (1.0 / D_MODEL)
        dx2_ref[...] = dx2
        df, dw = _rms_bwd(dx2, f, r, w)
        dwpff_ref[...] += dw
        dfb = df.astype(BF)
        df_ref[...] = dfb
        da = _dot_nt(dfb, wd_ref[...])
        dval_ref[...] = (da * silu).astype(BF)
        dgate = da * v * _dsilu(gate, sg)
        dcb_ref[...] += jnp.sum(dgate, axis=0, keepdims=True)
        dgate_ref[...] = dgate.astype(BF)

    return pl.pallas_call(
        body, name="ffn_tail", grid=(T // tm,),
        in_specs=[_row_spec(tm, D_FF), _row_spec(tm, D_FF), _row_spec(tm, D_MODEL), _row_spec(tm, D_MODEL),
                  _const_spec((8, D_FF)), _const_spec((1, D_FF)), _VMEM_WHOLE, _const_spec((1, D_MODEL))],
        out_specs=[_row_spec(tm, D_FF), _row_spec(tm, D_MODEL), _row_spec(tm, D_FF), _row_spec(tm, D_FF),
                   _row_spec(tm, D_MODEL), _const_spec((8, LANES)), _const_spec((1, D_MODEL)), _const_spec((1, D_FF))],
        out_shape=[S((T, D_FF), BF), S((T, D_MODEL), BF), S((T, D_FF), BF), S((T, D_FF), BF), S((T, D_MODEL), F32),
                   S((8, LANES), F32), S((1, D_MODEL), F32), S((1, D_FF), F32)],
        scratch_shapes=[pltpu.VMEM((8, D_FF), F32)],
        compiler_params=_cparams("arbitrary"),
    )(graw, val, x1, tgt, convw, convb, wdown, wpff)


def _ffn_bwd(dgate, dval, graw, x1, dx2, y, convw, wup, wpf, wpm, wout, tm=256):
    T = x1.shape[0]
    nt = T // tm
    rev = lambda i: (nt - 1 - i, 0)
    rspec = lambda w: pl.BlockSpec((tm, w), rev)

    def body(dgate_ref, dval_ref, graw_ref, x1_ref, dx2_ref, y_ref, cw_ref, wup_ref, wpf_ref, wpm_ref, wout_ref,
             dgraw_ref, dx1_ref, dy_ref, dys_ref, dyr_ref, dcw_ref, dwpf_ref, dwpm_ref, carry):
        i = pl.program_id(0)

        @pl.when(i == 0)
        def _():
            carry[...] = jnp.zeros_like(carry)
            dcw_ref[...] = jnp.zeros_like(dcw_ref)
            dwpf_ref[...] = jnp.zeros_like(dwpf_ref)
            dwpm_ref[...] = jnp.zeros_like(dwpm_ref)

        dg = dgate_ref[...].astype(F32)
        ext = jnp.concatenate([dg, carry[...]], axis=0)
        carry[...] = dg[0:8]
        g = graw_ref[...].astype(F32)
        dgraw = jnp.zeros((tm, D_FF), F32)
        for j in range(FFN_CONV):
            sj = _rows_after(ext, FFN_CONV - 1 - j, tm)
            dgraw = dgraw + cw_ref[j:j + 1, :] * sj
            dcw_ref[j:j + 1, :] += jnp.sum(sj * g, axis=0, keepdims=True)
        dgrawb = dgraw.astype(BF)
        dgraw_ref[...] = dgrawb
        dh2 = _dot_nt(dgrawb, wup_ref[:, 0:D_FF]) + _dot_nt(dval_ref[...], wup_ref[:, D_FF:])
        x1 = x1_ref[...]
        dxa, dw = _rms_bwd(dh2, x1, _rstd(x1), wpf_ref[...])
        dwpf_ref[...] += dw
        dx1 = dx2_ref[...] + dxa
        dx1_ref[...] = dx1
        yv = y_ref[...]
        dy, dw = _rms_bwd(dx1, yv, _rstd(yv), wpm_ref[...])
        dwpm_ref[...] += dw
        dyb = dy.astype(BF)
        dy_ref[...] = dyb
        dys_ref[...] = _dot_nt(dyb, wout_ref[0:SSD_WIDTH, :]).astype(BF)
        dyr_ref[...] = _dot_nt(dyb, wout_ref[SSD_WIDTH:, :]).astype(BF)

    return pl.pallas_call(
        body, name="ffn_bwd", grid=(nt,),
        in_specs=[rspec(D_FF), rspec(D_FF), rspec(D_FF), rspec(D_MODEL), rspec(D_MODEL), rspec(D_MODEL),
                  _const_spec((8, D_FF)), _VMEM_WHOLE, _const_spec((1, D_MODEL)), _const_spec((1, D_MODEL)), _VMEM_WHOLE],
        out_specs=[rspec(D_FF), rspec(D_MODEL), rspec(D_MODEL), rspec(SSD_WIDTH), rspec(RET_V_W),
                   _const_spec((8, D_FF)), _const_spec((1, D_MODEL)), _const_spec((1, D_MODEL))],
        out_shape=[S((T, D_FF), BF), S((T, D_MODEL), F32), S((T, D_MODEL), BF), S((T, SSD_WIDTH), BF), S((T, RET_V_W), BF),
                   S((8, D_FF), F32), S((1, D_MODEL), F32), S((1, D_MODEL), F32)],
        scratch_shapes=[pltpu.VMEM((8, D_FF), F32)],
        compiler_params=_cparams("arbitrary"),
    )(dgate, dval, graw, x1, dx2, y, convw, wup, wpf, wpm, wout)


def _in_bwd(dz, dxbc, dq, dk, dv, dg, ddt, x, dx1, w0, wp, tm=256):
    T = x.shape[0]

    def body(dz_ref, dxbc_ref, dq_ref, dk_ref, dv_ref, dg_ref, ddt_ref, x_ref, dx1_ref, w0_ref, wp_ref, gx_ref, dw0_ref):
        @pl.when(pl.program_id(0) == 0)
        def _():
            dw0_ref[...] = jnp.zeros_like(dw0_ref)

        dh = jnp.zeros((tm, D_MODEL), F32)
        for ref, lo, hi in ((dz_ref, P_Z, P_XBC), (dxbc_ref, P_XBC, P_Q), (dq_ref, P_Q, P_K), (dk_ref, P_K, P_V),
                            (dv_ref, P_V, P_G), (dg_ref, P_G, P_DT), (ddt_ref, P_DT, P_END)):
            dh = dh + _dot_nt(ref[...], wp_ref[:, lo:hi])
        xf = x_ref[...]
        dx, dw = _rms_bwd(dh, xf, _rstd(xf), w0_ref[...])
        dw0_ref[...] += dw
        gx_ref[...] = dx1_ref[...] + dx

    widths = (SSD_WIDTH, SSD_XBC, RET_QK_W, RET_QK_W, RET_V_W, RET_V_W, LANES)
    return pl.pallas_call(
        body, name="in_bwd", grid=(T // tm,),
        in_specs=[_row_spec(tm, w) for w in widths] + [_row_spec(tm, D_MODEL), _row_spec(tm, D_MODEL),
                                                       _const_spec((1, D_MODEL)), _VMEM_WHOLE],
        out_specs=[_row_spec(tm, D_MODEL), _const_spec((1, D_MODEL))],
        out_shape=[S((T, D_MODEL), F32), S((1, D_MODEL), F32)],
        compiler_params=_cparams("arbitrary"),
    )(dz, dxbc, dq, dk, dv, dg, ddt, x, dx1, w0, wp)


def _matmul_tn(a, b, name, tk=512):
    T, M = a.shape
    N = b.shape[1]
    tn = N
    while M * tn * 4 > (4 << 20) and tn % 256 == 0:
        tn //= 2
    nk = T // tk

    def body(a_ref, b_ref, o_ref):
        @pl.when(pl.program_id(1) == 0)
        def _():
            o_ref[...] = jnp.zeros_like(o_ref)

        o_ref[...] += _dot_tn(a_ref[...], b_ref[...])

    return pl.pallas_call(
        body, name=name, grid=(N // tn, nk),
        in_specs=[pl.BlockSpec((tk, M), lambda n, k: (k, 0)), pl.BlockSpec((tk, tn), lambda n, k: (k, n))],
        out_specs=pl.BlockSpec((M, tn), lambda n, k: (0, n)),
        out_shape=S((M, N), F32),
        compiler_params=_cparams("parallel", "arbitrary"),
    )(a, b)


def _tri(lower):
    r = lax.broadcasted_iota(jnp.int32, (CH, CH), 0)
    c = lax.broadcasted_iota(jnp.int32, (CH, CH), 1)
    return ((c <= r) if lower else (r <= c)).astype(F32)


def _onehot_row(h):
    return (lax.broadcasted_iota(jnp.int32, (1, LANES), 1) == h).astype(F32)


def _onehot_col(h):
    return (lax.broadcasted_iota(jnp.int32, (LANES, 1), 0) == h).astype(F32)


def _ssd_pre(xc_ref, xh_ref, dtr_ref, cw_ref, cb_ref, dtb_ref, alog_ref, first):
    xc = xc_ref[...].astype(F32)
    xh = jnp.where(first, 0.0, xh_ref[...].astype(F32))
    ext = jnp.concatenate([xh, xc], axis=0)
    u = cb_ref[...] + sum(cw_ref[j:j + 1, :] * _rows_before(ext, SSD_CONV - 1 - j, HALO, CH) for j in range(SSD_CONV))
    sg = _sigmoid(u)
    act = u * sg
    dt = _softplus(dtr_ref[...] + dtb_ref[...])
    a = -jnp.exp(alog_ref[...])
    da = dt * a
    cs = _dot_hi(_tri(True), da)
    cst = _dot_tn_hi(da, _tri(False))
    return xc, u, sg, act, dt, a, cs, cst


def _ssd_head(h, act, dt, cs, cst, cb_g, sprev):
    causal = lax.broadcasted_iota(jnp.int32, (CH, CH), 0) >= lax.broadcasted_iota(jnp.int32, (CH, CH), 1)
    g = h // (SSD_HEADS // SSD_GROUPS)
    c_col = cs[:, h:h + 1]
    c_row = cst[h:h + 1, :]
    c_last = cs[CH - 1:CH, h:h + 1]
    lmat = jnp.exp(jnp.where(causal, c_col - c_row, -1e30))
    xs_h = act[:, h * SSD_HEAD_DIM:(h + 1) * SSD_HEAD_DIM]
    dt_col = dt[:, h:h + 1]
    xdt = xs_h * dt_col
    e_col = jnp.exp(c_col)
    decay = jnp.exp(c_last - c_col)
    wdec = xdt * decay
    cm_g = act[:, SSD_WIDTH + SSD_GROUPS * SSD_STATE + g * SSD_STATE:SSD_WIDTH + SSD_GROUPS * SSD_STATE + (g + 1) * SSD_STATE]
    yoff = _dot_nt(cm_g, sprev) * e_col
    ydiag = _dot(cb_g * lmat, xdt)
    return lmat, xs_h, dt_col, xdt, e_col, decay, wdec, c_last, ydiag, yoff


def _ssd_specs(T):
    nc = T // CH
    return nc, [
        _row_spec(CH, SSD_XBC),
        pl.BlockSpec((HALO, SSD_XBC), lambda i: (jnp.maximum(i * (CH // HALO) - 1, 0), 0)),
        _row_spec(CH, LANES),
        _row_spec(CH, SSD_WIDTH),
    ]


def _groups(act):
    bm = [act[:, SSD_WIDTH + g * SSD_STATE:SSD_WIDTH + (g + 1) * SSD_STATE] for g in range(SSD_GROUPS)]
    o = SSD_WIDTH + SSD_GROUPS * SSD_STATE
    cm = [act[:, o + g * SSD_STATE:o + (g + 1) * SSD_STATE] for g in range(SSD_GROUPS)]
    return bm, cm


def _ssd_fwd(xbc, dtr, z, convw, convb, dtb, alog, dsk, nw):
    T = xbc.shape[0]
    nc, specs = _ssd_specs(T)

    def body(xc_ref, xh_ref, dtr_ref, z_ref, cw_ref, cb_ref, dtb_ref, alog_ref, dsk_ref, nw_ref,
             out_ref, st_ref, state, ybuf):
        i = pl.program_id(0)

        @pl.when(i == 0)
        def _():
            state[...] = jnp.zeros_like(state)

        xc, u, sg, act, dt, a, cs, cst = _ssd_pre(xc_ref, xh_ref, dtr_ref, cw_ref, cb_ref, dtb_ref, alog_ref, i == 0)
        bm, cm = _groups(act)
        cb = [_dot_nt(cm[g], bm[g]) for g in range(SSD_GROUPS)]
        st_ref[0] = state[...]
        for h in range(SSD_HEADS):
            g = h // (SSD_HEADS // SSD_GROUPS)
            sprev = state[h]
            lmat, xs_h, dt_col, xdt, e_col, decay, wdec, c_last, ydiag, yoff = _ssd_head(h, act, dt, cs, cst, cb[g], sprev)
            state[h] = sprev * jnp.exp(c_last) + _dot_tn(wdec, bm[g])
            ybuf[:, h * SSD_HEAD_DIM:(h + 1) * SSD_HEAD_DIM] = ydiag + yoff + xs_h * dsk_ref[:, h:h + 1]
        zf = z_ref[...].astype(F32)
        gated = ybuf[...] * (zf * _sigmoid(zf))
        out_ref[...] = (gated * _rstd(gated) * nw_ref[...]).astype(BF)

    return pl.pallas_call(
        body, name="ssd_fwd", grid=(nc,),
        in_specs=specs + [_const_spec((8, SSD_XBC)), _const_spec((1, SSD_XBC)), _const_spec((1, LANES)),
                          _const_spec((1, LANES)), _const_spec((1, LANES)), _const_spec((1, SSD_WIDTH))],
        out_specs=[_row_spec(CH, SSD_WIDTH),
                   pl.BlockSpec((1, SSD_HEADS, SSD_HEAD_DIM, SSD_STATE), lambda i: (i, 0, 0, 0))],
        out_shape=[S((T, SSD_WIDTH), BF), S((nc, SSD_HEADS, SSD_HEAD_DIM, SSD_STATE), F32)],
        scratch_shapes=[pltpu.VMEM((SSD_HEADS, SSD_HEAD_DIM, SSD_STATE), F32), pltpu.VMEM((CH, SSD_WIDTH), F32)],
        compiler_params=_cparams("arbitrary"),
    )(xbc, xbc, dtr, z, convw, convb, dtb, alog, dsk, nw)


def _ssd_bwd(dout, xbc, dtr, z, states, convw, convb, dtb, alog, dsk, nw):
    T = xbc.shape[0]
    nc = T // CH
    rev = lambda i: (nc - 1 - i, 0)
    rspec = lambda w: pl.BlockSpec((CH, w), rev)
    halo_spec = pl.BlockSpec((HALO, SSD_XBC), lambda i: (jnp.maximum((nc - 1 - i) * (CH // HALO) - 1, 0), 0))
    NB = SSD_WIDTH
    NC_ = SSD_WIDTH + SSD_GROUPS * SSD_STATE

    def body(do_ref, xc_ref, xh_ref, dtr_ref, z_ref, st_ref, cw_ref, cb_ref, dtb_ref, alog_ref, dsk_ref, nw_ref,
             dz_ref, dxbc_ref, ddt_ref, dcw_ref, dcb_ref, ddtb_ref, dalog_ref, ddsk_ref, dnw_ref,
             dstate, ducarry, ybuf, yoffbuf, lbuf, dact):
        i = pl.program_id(0)

        @pl.when(i == 0)
        def _():
            dstate[...] = jnp.zeros_like(dstate)
            ducarry[...] = jnp.zeros_like(ducarry)
            for ref in (dcw_ref, dcb_ref, ddtb_ref, dalog_ref, ddsk_ref, dnw_ref):
                ref[...] = jnp.zeros_like(ref)

        xc, u, sg, act, dt, a, cs, cst = _ssd_pre(xc_ref, xh_ref, dtr_ref, cw_ref, cb_ref, dtb_ref, alog_ref, i == nc - 1)
        bm, cm = _groups(act)
        cb = [_dot_nt(cm[g], bm[g]) for g in range(SSD_GROUPS)]
        for h in range(SSD_HEADS):
            g = h // (SSD_HEADS // SSD_GROUPS)
            lmat, xs_h, dt_col, xdt, e_col, decay, wdec, c_last, ydiag, yoff = _ssd_head(h, act, dt, cs, cst, cb[g], st_ref[0, h])
            lbuf[h] = lmat
            sl = slice(h * SSD_HEAD_DIM, (h + 1) * SSD_HEAD_DIM)
            yoffbuf[:, sl] = yoff
            ybuf[:, sl] = ydiag + yoff + xs_h * dsk_ref[:, h:h + 1]
        yv = ybuf[...]
        zf = z_ref[...].astype(F32)
        sz = _sigmoid(zf)
        gated = yv * (zf * sz)
        dgated, dnw = _rms_bwd(do_ref[...].astype(F32), gated, _rstd(gated), nw_ref[...])
        dnw_ref[...] += dnw
        dz_ref[...] = (dgated * yv * _dsilu(zf, sz)).astype(BF)
        ybuf[...] = dgated * (zf * sz)
        dcs = jnp.zeros((CH, LANES), F32)
        dcst = jnp.zeros((LANES, CH), F32)
        dlast = jnp.zeros((1, LANES), F32)
        ddt = jnp.zeros((CH, LANES), F32)
        ddsk = jnp.zeros((1, LANES), F32)
        dcb = [jnp.zeros((CH, CH), F32) for _ in range(SSD_GROUPS)]
        dbm = [jnp.zeros((CH, SSD_STATE), F32) for _ in range(SSD_GROUPS)]
        dcm = [jnp.zeros((CH, SSD_STATE), F32) for _ in range(SSD_GROUPS)]
        for h in range(SSD_HEADS):
            g = h // (SSD_HEADS // SSD_GROUPS)
            sl = slice(h * SSD_HEAD_DIM, (h + 1) * SSD_HEAD_DIM)
            oh = _onehot_row(h)
            sprev = st_ref[0, h]
            dsn = dstate[h]
            c_col = cs[:, h:h + 1]
            c_last = cs[CH - 1:CH, h:h + 1]
            xs_h = act[:, sl]
            dt_col = dt[:, h:h + 1]
            xdt = xs_h * dt_col
            e_col = jnp.exp(c_col)
            decay = jnp.exp(c_last - c_col)
            wdec = xdt * decay
            e_last = jnp.exp(c_last)
            lmat = lbuf[h]
            mmat = cb[g] * lmat
            dy_h = ybuf[:, sl]
            dm = _dot_nt(dy_h, xdt)
            dxdt = _dot_tn(mmat, dy_h)
            dseg = dm * mmat
            dcb[g] = dcb[g] + dm * lmat
            col = jnp.sum(dseg, axis=1, keepdims=True)
            dcst = dcst - _onehot_col(h) * jnp.sum(dseg, axis=0, keepdims=True)
            col = col + jnp.sum(dy_h * yoffbuf[:, sl], axis=1, keepdims=True)
            dq = dy_h * e_col
            dcm[g] = dcm[g] + _dot(dq, sprev)
            dsp = _dot_tn(dq, cm[g])
            dsp = dsp + dsn * e_last
            dl = jnp.sum(dsn * sprev) * e_last
            dw = _dot_nt(bm[g], dsn)
            dbm[g] = dbm[g] + _dot(wdec, dsn)
            dxdt = dxdt + dw * decay
            vv = jnp.sum(dw * wdec, axis=1, keepdims=True)
            col = col - vv
            dl = dl + jnp.sum(vv)
            dlast = dlast + dl * oh
            dcs = dcs + col * oh
            dstate[h] = dsp
            dact[:, sl] = dy_h * dsk_ref[:, h:h + 1] + dxdt * dt_col
            ddt = ddt + jnp.sum(dxdt * xs_h, axis=1, keepdims=True) * oh
            ddsk = ddsk + jnp.sum(dy_h * xs_h) * oh
        for g in range(SSD_GROUPS):
            dact[:, NB + g * SSD_STATE:NB + (g + 1) * SSD_STATE] = dbm[g] + _dot_tn(dcb[g], cm[g])
            dact[:, NC_ + g * SSD_STATE:NC_ + (g + 1) * SSD_STATE] = dcm[g] + _dot(dcb[g], bm[g])
        ddsk_ref[...] += ddsk
        rows = lax.broadcasted_iota(jnp.int32, (CH, LANES), 0)
        dcs = dcs + _dot_tn_hi(dcst, jnp.eye(LANES, dtype=F32)) + jnp.where(rows == CH - 1, dlast, 0.0)
        dda = _dot_hi(_tri(False), dcs)
        dalog_ref[...] += jnp.sum(dda * dt, axis=0, keepdims=True) * a
        ddt = ddt + dda * a
        ddtr = ddt * _sigmoid(dtr_ref[...] + dtb_ref[...])
        ddtb_ref[...] += jnp.sum(ddtr, axis=0, keepdims=True)
        ddt_ref[...] = ddtr.astype(BF)
        du = dact[...] * _dsilu(u, sg)
        dcb_ref[...] += jnp.sum(du, axis=0, keepdims=True)
        ext = jnp.concatenate([du, ducarry[...]], axis=0)
        ducarry[...] = du[0:8]
        dx = jnp.zeros((CH, SSD_XBC), F32)
        for j in range(SSD_CONV):
            sj = _rows_after(ext, SSD_CONV - 1 - j, CH)
            dx = dx + cw_ref[j:j + 1, :] * sj
            dcw_ref[j:j + 1, :] += jnp.sum(sj * xc, axis=0, keepdims=True)
        dxbc_ref[...] = dx.astype(BF)

    return pl.pallas_call(
        body, name="ssd_bwd", grid=(nc,),
        in_specs=[rspec(SSD_WIDTH), rspec(SSD_XBC), halo_spec, rspec(LANES), rspec(SSD_WIDTH),
                  pl.BlockSpec((1, SSD_HEADS, SSD_HEAD_DIM, SSD_STATE), lambda i: (nc - 1 - i, 0, 0, 0)),
                  _const_spec((8, SSD_XBC)), _const_spec((1, SSD_XBC)), _const_spec((1, LANES)),
                  _const_spec((1, LANES)), _const_spec((1, LANES)), _const_spec((1, SSD_WIDTH))],
        out_specs=[rspec(SSD_WIDTH), rspec(SSD_XBC), rspec(LANES),
                   _const_spec((8, SSD_XBC)), _const_spec((1, SSD_XBC)), _const_spec((1, LANES)),
                   _const_spec((1, LANES)), _const_spec((1, LANES)), _const_spec((1, SSD_WIDTH))],
        out_shape=[S((T, SSD_WIDTH), BF), S((T, SSD_XBC), BF), S((T, LANES), BF),
                   S((8, SSD_XBC), F32), S((1, SSD_XBC), F32), S((1, LANES), F32),
                   S((1, LANES), F32), S((1, LANES), F32), S((1, SSD_WIDTH), F32)],
        scratch_shapes=[pltpu.VMEM((SSD_HEADS, SSD_HEAD_DIM, SSD_STATE), F32), pltpu.VMEM((8, SSD_XBC), F32),
                        pltpu.VMEM((CH, SSD_WIDTH), F32), pltpu.VMEM((CH, SSD_WIDTH), F32),
                        pltpu.VMEM((SSD_HEADS, CH, CH), F32), pltpu.VMEM((CH, SSD_XBC), F32)],
        compiler_params=_cparams("arbitrary"),
    )(dout, xbc, xbc, dtr, z, states, convw, convb, dtb, alog, dsk, nw)


def _log_gamma(h):
    return float(np.log1p(-np.exp2(np.float32(-5.0 - h)), dtype=np.float32))


def _swap_halves(t):
    n = t.shape[1]
    lane = lax.broadcasted_iota(jnp.int32, t.shape, 1)
    return jnp.where((lane & (RET_QK - 1)) < RET_QK // 2, pltpu.roll(t, n - RET_QK // 2, 1), pltpu.roll(t, RET_QK // 2, 1))


def _rot(t, cos, sin):
    return t * cos + _swap_halves(t) * sin


def _rot_t(d, cos, sin):
    return d * cos + _swap_halves(d * sin)


def _ret_consts(h):
    lg = _log_gamma(h)
    r = lax.broadcasted_iota(jnp.int32, (CH, CH), 0)
    c = lax.broadcasted_iota(jnp.int32, (CH, CH), 1)
    rel = (r - c).astype(F32)
    dmask = jnp.where(rel >= 0, jnp.exp(lg * jnp.maximum(rel, 0.0)), 0.0)
    pos = lax.broadcasted_iota(jnp.int32, (CH, 1), 0).astype(F32)
    kdec = jnp.exp(lg * (CH - 1.0 - pos))
    qdec = jnp.exp(lg * (pos + 1.0))
    return dmask, kdec, qdec, math.exp(lg * CH)


def _ret_fwd(q, k, v, g, cos, sin, nw):
    T = q.shape[0]
    nc = T // CH

    def body(q_ref, k_ref, v_ref, g_ref, cos_ref, sin_ref, nw_ref, out_ref, st_ref, state):
        i = pl.program_id(0)

        @pl.when(i == 0)
        def _():
            state[...] = jnp.zeros_like(state)

        cosf = jnp.tile(cos_ref[...], (1, RET_QK_W // LANES))
        sinf = jnp.tile(sin_ref[...], (1, RET_QK_W // LANES))
        qr = _rot(q_ref[...].astype(F32), cosf, sinf)
        kr = _rot(k_ref[...].astype(F32), cosf, sinf) * (RET_QK ** -0.5)
        st_ref[0] = state[...]
        for h in range(RET_HEADS):
            dmask, kdec, qdec, gam = _ret_consts(h)
            q_h = qr[:, h * RET_QK:(h + 1) * RET_QK]
            k_h = kr[:, h * RET_QK:(h + 1) * RET_QK]
            sl = slice(h * RET_V, (h + 1) * RET_V)
            v_h = v_ref[:, sl]
            rprev = state[h]
            scores = _dot_nt(q_h, k_h) * dmask
            o = _dot(scores, v_h) + _dot(q_h * qdec, rprev)
            state[h] = rprev * gam + _dot_tn(k_h * kdec, v_h)
            gf = g_ref[:, sl].astype(F32)
            out_ref[:, sl] = (o * _rstd(o) * nw_ref[:, sl] * (gf * _sigmoid(gf))).astype(BF)

    return pl.pallas_call(
        body, name="ret_fwd", grid=(nc,),
        in_specs=[_row_spec(CH, RET_QK_W), _row_spec(CH, RET_QK_W), _row_spec(CH, RET_V_W), _row_spec(CH, RET_V_W),
                  _row_spec(CH, LANES), _row_spec(CH, LANES), _const_spec((1, RET_V_W))],
        out_specs=[_row_spec(CH, RET_V_W), pl.BlockSpec((1, RET_HEADS, RET_QK, RET_V), lambda i: (i, 0, 0, 0))],
        out_shape=[S((T, RET_V_W), BF), S((nc, RET_HEADS, RET_QK, RET_V), F32)],
        scratch_shapes=[pltpu.VMEM((RET_HEADS, RET_QK, RET_V), F32)],
        compiler_params=_cparams("arbitrary"),
    )(q, k, v, g, cos, sin, nw)


def _ret_bwd(dout, q, k, v, g, states, cos, sin, nw):
    T = q.shape[0]
    nc = T // CH
    rev = lambda i: (nc - 1 - i, 0)
    rspec = lambda w: pl.BlockSpec((CH, w), rev)

    def body(do_ref, q_ref, k_ref, v_ref, g_ref, st_ref, cos_ref, sin_ref, nw_ref,
             dq_ref, dk_ref, dv_ref, dg_ref, dnw_ref, dstate, dqbuf, dkbuf):
        i = pl.program_id(0)

        @pl.when(i == 0)
        def _():
            dstate[...] = jnp.zeros_like(dstate)
            dnw_ref[...] = jnp.zeros_like(dnw_ref)

        cosf = jnp.tile(cos_ref[...], (1, RET_QK_W // LANES))
        sinf = jnp.tile(sin_ref[...], (1, RET_QK_W // LANES))
        qr = _rot(q_ref[...].astype(F32), cosf, sinf)
        kr = _rot(k_ref[...].astype(F32), cosf, sinf) * (RET_QK ** -0.5)
        for h in range(RET_HEADS):
            dmask, kdec, qdec, gam = _ret_consts(h)
            q_h = qr[:, h * RET_QK:(h + 1) * RET_QK]
            k_h = kr[:, h * RET_QK:(h + 1) * RET_QK]
            sl = slice(h * RET_V, (h + 1) * RET_V)
            v_h = v_ref[:, sl]
            rprev = st_ref[0, h]
            gnext = dstate[h]
            qd = q_h * qdec
            kd = k_h * kdec
            scores = _dot_nt(q_h, k_h) * dmask
            o = _dot(scores, v_h) + _dot(qd, rprev)
            rr = _rstd(o)
            of = o * rr
            gf = g_ref[:, sl].astype(F32)
            sgg = _sigmoid(gf)
            d_h = do_ref[:, sl].astype(F32)
            nw_h = nw_ref[:, sl]
            dg_ref[:, sl] = (d_h * of * nw_h * _dsilu(gf, sgg)).astype(BF)
            dt_ = d_h * (gf * sgg)
            dnw_ref[:, sl] += jnp.sum(dt_ * of, axis=0, keepdims=True)
            dof = dt_ * nw_h
            do = rr * dof - o * (rr * rr * rr) * jnp.mean(dof * o, axis=-1, keepdims=True)
            dsc = _dot_nt(do, v_h) * dmask
            dv_ref[:, sl] = (_dot_tn(scores, do) + _dot(kd, gnext)).astype(BF)
            dqbuf[:, h * RET_QK:(h + 1) * RET_QK] = _dot(dsc, k_h) + _dot_nt(do, rprev) * qdec
            dkbuf[:, h * RET_QK:(h + 1) * RET_QK] = _dot_tn(dsc, q_h) + _dot_nt(v_h, gnext) * kdec
            dstate[h] = gnext * gam + _dot_tn(qd, do)
        dq_ref[...] = _rot_t(dqbuf[...], cosf, sinf).astype(BF)
        dk_ref[...] = (_rot_t(dkbuf[...], cosf, sinf) * (RET_QK ** -0.5)).astype(BF)

    return pl.pallas_call(
        body, name="ret_bwd", grid=(nc,),
        in_specs=[rspec(RET_V_W), rspec(RET_QK_W), rspec(RET_QK_W), rspec(RET_V_W), rspec(RET_V_W),
                  pl.BlockSpec((1, RET_HEADS, RET_QK, RET_V), lambda i: (nc - 1 - i, 0, 0, 0)),
                  rspec(LANES), rspec(LANES), _const_spec((1, RET_V_W))],
        out_specs=[rspec(RET_QK_W), rspec(RET_QK_W), rspec(RET_V_W), rspec(RET_V_W), _const_spec((1, RET_V_W))],
        out_shape=[S((T, RET_QK_W), BF), S((T, RET_QK_W), BF), S((T, RET_V_W), BF), S((T, RET_V_W), BF),
                   S((1, RET_V_W), F32)],
        scratch_shapes=[pltpu.VMEM((RET_HEADS, RET_QK, RET_V), F32), pltpu.VMEM((CH, RET_QK_W), F32),
                        pltpu.VMEM((CH, RET_QK_W), F32)],
        compiler_params=_cparams("arbitrary"),
    )(dout, q, k, v, g, states, cos, sin, nw)


def _exchange(buf, name, same):
    slab = buf.shape if same else buf.shape[1:]

    def body(buf_ref, out_ref, send_sems, recv_sems, local_sem):
        x, y, c = lax.axis_index("x"), lax.axis_index("y"), lax.axis_index("c")
        me = 4 * x + 2 * y + c

        def src(d):
            return buf_ref if same else buf_ref.at[d]

        def peer(k):
            px = 1 - x if k & 4 else x
            py = 1 - y if k & 2 else y
            pc = 1 - c if k & 1 else c
            return (px, py, pc), 4 * px + 2 * py + pc

        local = pltpu.make_async_copy(src(me), out_ref.at[me], local_sem)
        local.start()
        sends = []
        for k in range(1, N_DEV):
            pid, p = peer(k)
            cp = pltpu.make_async_remote_copy(src_ref=src(p), dst_ref=out_ref.at[me], send_sem=send_sems.at[k - 1],
                                              recv_sem=recv_sems.at[k - 1], device_id=pid, device_id_type=pl.DeviceIdType.MESH)
            cp.start()
            sends.append(cp)
        for k in range(1, N_DEV):
            pid, p = peer(k)
            pltpu.make_async_remote_copy(src_ref=src(p), dst_ref=out_ref.at[p], send_sem=send_sems.at[k - 1],
                                         recv_sem=recv_sems.at[k - 1], device_id=pid,
                                         device_id_type=pl.DeviceIdType.MESH).wait_recv()
        for cp in sends:
            cp.wait_send()
        local.wait()

    return pl.pallas_call(
        body, name=name,
        in_specs=[pl.BlockSpec(memory_space=pl.ANY)], out_specs=pl.BlockSpec(memory_space=pl.ANY),
        out_shape=S((N_DEV,) + tuple(slab), buf.dtype),
        scratch_shapes=[pltpu.SemaphoreType.DMA((N_DEV - 1,)), pltpu.SemaphoreType.DMA((N_DEV - 1,)), pltpu.SemaphoreType.DMA],
    )(buf)


def _sum_slabs(recv, name):
    n, R, _ = recv.shape

    def body(r_ref, o_ref):
        g = r_ref[0].astype(F32)
        for s in range(1, n):
            g = g + r_ref[s].astype(F32)
        o_ref[...] = g

    return pl.pallas_call(body, name=name, out_shape=S((R, LANES), F32))(recv)


def _adamw(recv, w, m, v, name, tr):
    n, R, _ = recv.shape
    c1 = 1.0 - ADAM_B1 ** ADAM_STEP
    c2 = 1.0 - ADAM_B2 ** ADAM_STEP

    def body(r_ref, w_ref, m_ref, v_ref, g_out, d_out, m_out, v_out):
        g = r_ref[0].astype(F32)
        for s in range(1, n):
            g = g + r_ref[s].astype(F32)
        mm = ADAM_B1 * m_ref[...] + (1.0 - ADAM_B1) * g
        vv = ADAM_B2 * v_ref[...] + (1.0 - ADAM_B2) * (g * g)
        g_out[...] = g
        m_out[...] = mm
        v_out[...] = vv
        d_out[...] = -ADAM_LR * ((mm / c1) / (jnp.sqrt(vv / c2) + ADAM_EPS) + ADAM_WD * w_ref[...])

    spec = pl.BlockSpec((tr, LANES), lambda i: (i, 0))
    return pl.pallas_call(
        body, name=name, grid=(R // tr,),
        in_specs=[pl.BlockSpec((n, tr, LANES), lambda i: (0, i, 0)), spec, spec, spec],
        out_specs=[spec] * 4, out_shape=[S((R, LANES), F32)] * 4,
        compiler_params=_cparams("parallel"),
    )(recv, w, m, v)


def _pack(parts, rows):
    cols = []
    for p in parts:
        f = p.reshape(-1)
        cols.append(jnp.pad(f, (0, (-f.shape[0]) % LANES)))
    flat = jnp.concatenate(cols)
    return jnp.pad(flat, (0, rows * LANES - flat.shape[0])).reshape(rows, LANES)


def _unpack(buf, shapes):
    flat = buf.reshape(-1)
    out, o = [], 0
    for shp in shapes:
        n = int(np.prod(shp))
        out.append(flat[o:o + n].reshape(shp))
        o += n + (-n) % LANES
    return out


BIG_ROWS = 16384
SMALL_ROWS = 200
CONV_ROWS = 16


def kernel(x, pre_mix_norm_w, w_in, ssd_conv_w, ssd_conv_b, ssd_dt_bias, ssd_a_log, ssd_d, ssd_norm_w, ret_norm_w, w_out, post_mix_norm_w, pre_ffn_norm_w, w_up, ffn_conv_w, ffn_conv_b, w_down, post_ffn_norm_w, loss_target, m_pre_mix_norm_w, m_w_in, m_ssd_conv_w, m_ssd_conv_b, m_ssd_dt_bias, m_ssd_a_log, m_ssd_d, m_ssd_norm_w, m_ret_norm_w, m_w_out, m_post_mix_norm_w, m_pre_ffn_norm_w, m_w_up, m_ffn_conv_w, m_ffn_conv_b, m_w_down, m_post_ffn_norm_w, v_pre_mix_norm_w, v_w_in, v_ssd_conv_w, v_ssd_conv_b, v_ssd_dt_bias, v_ssd_a_log, v_ssd_d, v_ssd_norm_w, v_ret_norm_w, v_w_out, v_post_mix_norm_w, v_pre_ffn_norm_w, v_w_up, v_ffn_conv_w, v_ffn_conv_b, v_w_down, v_post_ffn_norm_w):
    T = x.shape[1]
    xi, tgt = x[0], loss_target[0]
    me = 4 * lax.axis_index("x") + 2 * lax.axis_index("y") + lax.axis_index("c")
    n_in, n_up = w_in.shape[2], w_up.shape[2]
    n_out, n_down = w_out.shape[1], w_down.shape[1]
    n_sc, n_fc = ssd_conv_w.shape[2], ffn_conv_w.shape[2]
    big_shapes = [(D_MODEL, n_in), (n_out, D_MODEL), (D_MODEL, n_up), (n_down, D_MODEL)]

    gw = _exchange(_pack([w_in, w_out, w_up, w_down], BIG_ROWS).astype(BF), "gather_w", True)
    gconv = _exchange(_pack([ssd_conv_w, ffn_conv_w], CONV_ROWS), "gather_conv", True)
    per_dev = [_unpack(gw[d], big_shapes) for d in range(N_DEV)]
    win = jnp.concatenate([p[0] for p in per_dev], axis=1)
    wout = jnp.concatenate([p[1] for p in per_dev], axis=0)
    wup = jnp.concatenate([p[2] for p in per_dev], axis=1)
    wdown = jnp.concatenate([p[3] for p in per_dev], axis=0)
    convs = [_unpack(gconv[d], [(SSD_CONV, n_sc), (FFN_CONV, n_fc)]) for d in range(N_DEV)]
    scw = jnp.pad(jnp.concatenate([c[0] for c in convs], axis=1), ((0, 8 - SSD_CONV), (0, 0)))
    fcw = jnp.pad(jnp.concatenate([c[1] for c in convs], axis=1), ((0, 8 - FFN_CONV), (0, 0)))
    wp = jnp.concatenate([win[:, O_Z:O_XBC], win[:, O_XBC:O_DT], win[:, O_Q:O_K], win[:, O_K:O_V], win[:, O_V:O_G],
                          win[:, O_G:], win[:, O_DT:O_Q], jnp.zeros((D_MODEL, P_END - P_DT - SSD_HEADS), win.dtype)], axis=1)

    pad_h = lambda p: jnp.pad(p, ((0, 0), (0, LANES - SSD_HEADS)))
    dtb, alog, dsk = pad_h(ssd_dt_bias), pad_h(ssd_a_log), pad_h(ssd_d)
    inv = ROPE_BASE ** (-jnp.arange(0, RET_QK, 2, dtype=F32) / RET_QK)
    ang = jnp.arange(T, dtype=F32)[:, None] * inv[None, :]
    cs_, sn_ = jnp.cos(ang), jnp.sin(ang)
    cos = jnp.concatenate([cs_, cs_, cs_, cs_], axis=1)
    sin = jnp.concatenate([-sn_, sn_, -sn_, sn_], axis=1)

    h, z, xbc, q, k, v, g, dtr = _fwd_in(xi, pre_mix_norm_w, wp)
    ys, sst = _ssd_fwd(xbc, dtr, z, scw, ssd_conv_b, dtb, alog, dsk, ssd_norm_w)
    yr, rst = _ret_fwd(q, k, v, g, cos, sin, ret_norm_w)
    y, x1, h2, graw, val = _fwd_mid(ys, yr, xi, wout, post_mix_norm_w, pre_ffn_norm_w, wup)
    a, dfb, dval, dgate, dx2, lossb, d_pff, d_fcb = _ffn_tail(graw, val, x1, tgt, fcw, ffn_conv_b, wdown, post_ffn_norm_w)
    dgraw, dx1, dyb, dys, dyr, d_fcw, d_pf, d_pm = _ffn_bwd(dgate, dval, graw, x1, dx2, y, fcw, wup, pre_ffn_norm_w,
                                                         post_mix_norm_w, wout)
    dz, dxbc, ddt, d_scw, d_scb, d_dtb, d_alog, d_dsk, d_snw = _ssd_bwd(dys, xbc, dtr, z, sst, scw, ssd_conv_b, dtb, alog,
                                                                      dsk, ssd_norm_w)
    dq, dk, dv, dg, d_rnw = _ret_bwd(dyr, q, k, v, g, rst, cos, sin, ret_norm_w)
    gx, d_w0 = _in_bwd(dz, dxbc, dq, dk, dv, dg, ddt, xi, dx1, pre_mix_norm_w, wp)
    gin = jnp.concatenate([_matmul_tn(h, dz, "dw_z"), _matmul_tn(h, dxbc, "dw_xbc"),
                           _matmul_tn(h, ddt, "dw_dt")[:, :SSD_HEADS], _matmul_tn(h, dq, "dw_q"), _matmul_tn(h, dk, "dw_k"),
                           _matmul_tn(h, dv, "dw_v"), _matmul_tn(h, dg, "dw_g")], axis=1)
    gout = jnp.concatenate([_matmul_tn(ys, dyb, "dw_out_s"), _matmul_tn(yr, dyb, "dw_out_r")], axis=0)
    gup = jnp.concatenate([_matmul_tn(h2, dgraw, "dw_up_g"), _matmul_tn(h2, dval, "dw_up_v")], axis=1)
    gdown = _matmul_tn(a, dfb, "dw_down")
    send = jnp.stack([_pack([gin[:, d * n_in:(d + 1) * n_in], gout[d * n_out:(d + 1) * n_out],
                             gup[:, d * n_up:(d + 1) * n_up], gdown[d * n_down:(d + 1) * n_down]], BIG_ROWS)
                      for d in range(N_DEV)]).astype(BF)
    recv = _exchange(send, "scatter_g", False)
    gb, db, mb, vb = _adamw(recv, _pack([w_in, w_out, w_up, w_down], BIG_ROWS), _pack([m_w_in, m_w_out, m_w_up, m_w_down], BIG_ROWS),
                            _pack([v_w_in, v_w_out, v_w_up, v_w_down], BIG_ROWS), "adamw_big", 1024)
    big = [[t.reshape((1,) + s) for t, s in zip(_unpack(b, big_shapes), big_shapes)] for b in (gb, db, mb, vb)]

    small_full = [d_w0, d_scw[:SSD_CONV], d_scb, d_dtb[:, :SSD_HEADS], d_alog[:, :SSD_HEADS], d_dsk[:, :SSD_HEADS], d_snw, d_rnw,
                  d_pm, d_pf, d_fcw[:FFN_CONV], d_fcb, d_pff]
    full_shapes = [t.shape for t in small_full]
    gs = _sum_slabs(_exchange(_pack(small_full, SMALL_ROWS), "gather_small", True), "sum_small")
    gfull = _unpack(gs, full_shapes)
    gfull[1] = lax.dynamic_slice_in_dim(gfull[1], me * n_sc, n_sc, axis=1)
    gfull[10] = lax.dynamic_slice_in_dim(gfull[10], me * n_fc, n_fc, axis=1)
    ws = [pre_mix_norm_w, ssd_conv_w, ssd_conv_b, ssd_dt_bias, ssd_a_log, ssd_d, ssd_norm_w, ret_norm_w, post_mix_norm_w,
          pre_ffn_norm_w, ffn_conv_w, ffn_conv_b, post_ffn_norm_w]
    ms = [m_pre_mix_norm_w, m_ssd_conv_w, m_ssd_conv_b, m_ssd_dt_bias, m_ssd_a_log, m_ssd_d, m_ssd_norm_w, m_ret_norm_w,
          m_post_mix_norm_w, m_pre_ffn_norm_w, m_ffn_conv_w, m_ffn_conv_b, m_post_ffn_norm_w]
    vs = [v_pre_mix_norm_w, v_ssd_conv_w, v_ssd_conv_b, v_ssd_dt_bias, v_ssd_a_log, v_ssd_d, v_ssd_norm_w, v_ret_norm_w,
          v_post_mix_norm_w, v_pre_ffn_norm_w, v_ffn_conv_w, v_ffn_conv_b, v_post_ffn_norm_w]
    out_shapes = [t.shape for t in ws]
    small = _adamw(_pack(gfull, SMALL_ROWS)[None], _pack(ws, SMALL_ROWS), _pack(ms, SMALL_ROWS), _pack(vs, SMALL_ROWS),
                   "adamw_small", SMALL_ROWS)
    small = [_unpack(b, out_shapes) for b in small]

    order = {"pre_mix_norm_w": ("s", 0), "w_in": ("b", 0), "ssd_conv_w": ("s", 1), "ssd_conv_b": ("s", 2),
             "ssd_dt_bias": ("s", 3), "ssd_a_log": ("s", 4), "ssd_d": ("s", 5), "ssd_norm_w": ("s", 6), "ret_norm_w": ("s", 7),
             "w_out": ("b", 1), "post_mix_norm_w": ("s", 8), "pre_ffn_norm_w": ("s", 9), "w_up": ("b", 2),
             "ffn_conv_w": ("s", 10), "ffn_conv_b": ("s", 11), "w_down": ("b", 3), "post_ffn_norm_w": ("s", 12)}
    loss = lax.psum(lossb[0, 0], ("x", "y", "c"))
    outs = [loss, gx[None]]
    for kind in range(4):
        for name, (grp, idx) in order.items():
            outs.append(big[kind][idx] if grp == "b" else small[kind][idx])
    return tuple(outs)
```

```python
import functools
import math

import numpy as np
import jax
import jax.numpy as jnp
from jax import lax
from jax.experimental import pallas as pl
from jax.experimental.pallas import tpu as pltpu

F32 = jnp.float32
BF = jnp.bfloat16
HI = lax.Precision.HIGHEST
S = jax.ShapeDtypeStruct

D_MODEL = 1024
SSD_HEADS = 16
SSD_HEAD_DIM = 64
SSD_GROUPS = 2
SSD_STATE = 128
SSD_WIDTH = 1024
SSD_XBC = 1536
SSD_CONV = 4
RET_HEADS = 8
RET_QK = 64
RET_V = 128
RET_QK_W = 512
RET_V_W = 1024
ROPE_BASE = 10000.0
CH = 128
D_FF = 2816
FFN_CONV = 3
EPS = 1e-6
IN_WIDTH = 5648
N_DEV = 8

ADAM_LR = 0.001
ADAM_B1 = 0.9
ADAM_B2 = 0.999
ADAM_EPS = 1e-08
ADAM_WD = 0.01
ADAM_STEP = 10

LANES = 128
HALO = 16
VMEM_LIMIT = 48 * 1024 * 1024

P_Z, P_XBC, P_Q, P_K, P_V, P_G, P_DT, P_END = 0, 1024, 2560, 3072, 3584, 4608, 5632, 5760
O_Z, O_XBC, O_DT, O_Q, O_K, O_V, O_G = 0, 1024, 2560, 2576, 3088, 3600, 4624


def _cparams(*sem):
    return pltpu.CompilerParams(dimension_semantics=sem, vmem_limit_bytes=VMEM_LIMIT)


def _dot(a, b):
    return jnp.dot(a.astype(BF), b.astype(BF), preferred_element_type=F32)


def _dot_nt(a, b):
    return lax.dot_general(a.astype(BF), b.astype(BF), (((1,), (1,)), ((), ())), preferred_element_type=F32)


def _dot_tn(a, b):
    return lax.dot_general(a.astype(BF), b.astype(BF), (((0,), (0,)), ((), ())), preferred_element_type=F32)


def _dot_hi(a, b):
    return jnp.dot(a, b, preferred_element_type=F32, precision=HI)


def _dot_tn_hi(a, b):
    return lax.dot_general(a, b, (((0,), (0,)), ((), ())), preferred_element_type=F32, precision=HI)


def _sigmoid(x):
    return jax.nn.sigmoid(x)


def _dsilu(x, s):
    return s * (1.0 + x * (1.0 - s))


def _softplus(x):
    return jnp.maximum(x, 0.0) + jnp.log1p(jnp.exp(-jnp.abs(x)))


def _rstd(x):
    return lax.rsqrt(jnp.mean(x * x, axis=-1, keepdims=True) + EPS)


def _rms_bwd(dy, x, r, w):
    gn = dy * w
    dx = r * gn - x * (r * r * r) * jnp.mean(gn * x, axis=-1, keepdims=True)
    dw = jnp.sum(dy * x * r, axis=0, keepdims=True)
    return dx, dw


def _rows_before(ext, s, head, n):
    if s == 0:
        return ext[head:head + n]
    return pltpu.roll(ext, s, 0)[head:head + n]


def _rows_after(ext, s, n):
    if s == 0:
        return ext[0:n]
    return pltpu.roll(ext, ext.shape[0] - s, 0)[0:n]


def _row_spec(tm, width):
    return pl.BlockSpec((tm, width), lambda i: (i, 0))


def _const_spec(shape):
    return pl.BlockSpec(shape, lambda i: (0,) * len(shape))


_VMEM_WHOLE = pl.BlockSpec(memory_space=pltpu.VMEM)


def _fwd_in(x, w0, wp, tm=256):
    T = x.shape[0]

    def body(x_ref, w0_ref, wp_ref, h_ref, z_ref, xbc_ref, q_ref, k_ref, v_ref, g_ref, dt_ref):
        xf = x_ref[...]
        h = (xf * _rstd(xf) * w0_ref[...]).astype(BF)
        h_ref[...] = h
        for ref, lo, hi in ((z_ref, P_Z, P_XBC), (xbc_ref, P_XBC, P_Q), (q_ref, P_Q, P_K), (k_ref, P_K, P_V),
                            (v_ref, P_V, P_G), (g_ref, P_G, P_DT), (dt_ref, P_DT, P_END)):
            ref[...] = jnp.dot(h, wp_ref[:, lo:hi], preferred_element_type=F32).astype(ref.dtype)

    widths = (D_MODEL, SSD_WIDTH, SSD_XBC, RET_QK_W, RET_QK_W, RET_V_W, RET_V_W)
    return pl.pallas_call(
        body, name="fwd_in", grid=(T // tm,),
        in_specs=[_row_spec(tm, D_MODEL), _const_spec((1, D_MODEL)), _VMEM_WHOLE],
        out_specs=[_row_spec(tm, w) for w in widths] + [_row_spec(tm, LANES)],
        out_shape=[S((T, w), BF) for w in widths] + [S((T, LANES), F32)],
        compiler_params=_cparams("parallel"),
    )(x, w0, wp)


def _fwd_mid(ys, yr, x, wout, wpm, wpf, wup, tm=256):
    T = x.shape[0]

    def body(ys_ref, yr_ref, x_ref, wout_ref, wpm_ref, wpf_ref, wup_ref, y_ref, x1_ref, h2_ref, graw_ref, val_ref):
        y = (jnp.dot(ys_ref[...], wout_ref[0:SSD_WIDTH, :], preferred_element_type=F32)
             + jnp.dot(yr_ref[...], wout_ref[SSD_WIDTH:, :], preferred_element_type=F32))
        y_ref[...] = y
        x1 = x_ref[...] + y * _rstd(y) * wpm_ref[...]
        x1_ref[...] = x1
        h2 = (x1 * _rstd(x1) * wpf_ref[...]).astype(BF)
        h2_ref[...] = h2
        graw_ref[...] = jnp.dot(h2, wup_ref[:, 0:D_FF], preferred_element_type=F32).astype(BF)
        val_ref[...] = jnp.dot(h2, wup_ref[:, D_FF:], preferred_element_type=F32).astype(BF)

    return pl.pallas_call(
        body, name="fwd_mid", grid=(T // tm,),
        in_specs=[_row_spec(tm, SSD_WIDTH), _row_spec(tm, RET_V_W), _row_spec(tm, D_MODEL), _VMEM_WHOLE,
                  _const_spec((1, D_MODEL)), _const_spec((1, D_MODEL)), _VMEM_WHOLE],
        out_specs=[_row_spec(tm, D_MODEL), _row_spec(tm, D_MODEL), _row_spec(tm, D_MODEL), _row_spec(tm, D_FF),
                   _row_spec(tm, D_FF)],
        out_shape=[S((T, D_MODEL), F32), S((T, D_MODEL), F32), S((T, D_MODEL), BF), S((T, D_FF), BF), S((T, D_FF), BF)],
        compiler_params=_cparams("parallel"),
    )(ys, yr, x, wout, wpm, wpf, wup)


def _ffn_tail(graw, val, x1, tgt, convw, convb, wdown, wpff, tm=256):
    T = x1.shape[0]

    def body(graw_ref, val_ref, x1_ref, tgt_ref, cw_ref, cb_ref, wd_ref, wpff_ref,
             a_ref, df_ref, dval_ref, dgate_ref, dx2_ref, loss_ref, dwpff_ref, dcb_ref, carry):
        i = pl.program_id(0)

        @pl.when(i == 0)
        def _():
            carry[...] = jnp.zeros_like(carry)
            loss_ref[...] = jnp.zeros_like(loss_ref)
            dwpff_ref[...] = jnp.zeros_like(dwpff_ref)
            dcb_ref[...] = jnp.zeros_like(dcb_ref)

        g = graw_ref[...].astype(F32)
        ext = jnp.concatenate([carry[...], g], axis=0)
        carry[...] = g[tm - 8:tm]
        gate = cb_ref[...] + sum(cw_ref[j:j + 1, :] * _rows_before(ext, FFN_CONV - 1 - j, 8, tm) for j in range(FFN_CONV))
        sg = _sigmoid(gate)
        silu = gate * sg
        v = val_ref[...].astype(F32)
        a = (silu * v).astype(BF)
        a_ref[...] = a
        f = jnp.dot(a, wd_ref[...], preferred_element_type=F32)
        r = _rstd(f)
        w = wpff_ref[...]
        e = x1_ref[...] + f * r * w - tgt_ref[...]
        loss_ref[...] += jnp.sum(e * e) * (0.5 / D_MODEL)
        dx2 = e * (1.0 / D_MODEL)
        dx2_ref[...] = dx2
        df, dw = _rms_bwd(dx2, f, r, w)
        dwpff_ref[...] += dw
        dfb = df.astype(BF)
        df_ref[...] = dfb
        da = _dot_nt(dfb, wd_ref[...])
        dval_ref[...] = (da * silu).astype(BF)
        dgate = da * v * _dsilu(gate, sg)
        dcb_ref[...] += jnp.sum(dgate, axis=0, keepdims=True)
        dgate_ref[...] = dgate.astype(BF)

    return pl.pallas_call(
        body, name="ffn_tail", grid=(T // tm,),
        in_specs=[_row_spec(tm, D_FF), _row_spec(tm, D_FF), _row_spec(tm, D_MODEL), _row_spec(tm, D_MODEL),
                  _const_spec((8, D_FF)), _const_spec((1, D_FF)), _VMEM_WHOLE, _const_spec((1, D_MODEL))],
        out_specs=[_row_spec(tm, D_FF), _row_spec(tm, D_MODEL), _row_spec(tm, D_FF), _row_spec(tm, D_FF),
                   _row_spec(tm, D_MODEL), _const_spec((8, LANES)), _const_spec((1, D_MODEL)), _const_spec((1, D_FF))],
        out_shape=[S((T, D_FF), BF), S((T, D_MODEL), BF), S((T, D_FF), BF), S((T, D_FF), BF), S((T, D_MODEL), F32),
                   S((8, LANES), F32), S((1, D_MODEL), F32), S((1, D_FF), F32)],
        scratch_shapes=[pltpu.VMEM((8, D_FF), F32)],
        compiler_params=_cparams("arbitrary"),
    )(graw, val, x1, tgt, convw, convb, wdown, wpff)


def _ffn_bwd(dgate, dval, graw, x1, dx2, y, convw, wup, wpf, wpm, wout, tm=256):
    T = x1.shape[0]
    nt = T // tm
    rev = lambda i: (nt - 1 - i, 0)
    rspec = lambda w: pl.BlockSpec((tm, w), rev)

    def body(dgate_ref, dval_ref, graw_ref, x1_ref, dx2_ref, y_ref, cw_ref, wup_ref, wpf_ref, wpm_ref, wout_ref,
             dgraw_ref, dx1_ref, dy_ref, dys_ref, dyr_ref, dcw_ref, dwpf_ref, dwpm_ref, carry):
        i = pl.program_id(0)

        @pl.when(i == 0)
        def _():
            carry[...] = jnp.zeros_like(carry)
            dcw_ref[...] = jnp.zeros_like(dcw_ref)
            dwpf_ref[...] = jnp.zeros_like(dwpf_ref)
            dwpm_ref[...] = jnp.zeros_like(dwpm_ref)

        dg = dgate_ref[...].astype(F32)
        ext = jnp.concatenate([dg, carry[...]], axis=0)
        carry[...] = dg[0:8]
        g = graw_ref[...].astype(F32)
        dgraw = jnp.zeros((tm, D_FF), F32)
        for j in range(FFN_CONV):
            sj = _rows_after(ext, FFN_CONV - 1 - j, tm)
            dgraw = dgraw + cw_ref[j:j + 1, :] * sj
            dcw_ref[j:j + 1, :] += jnp.sum(sj * g, axis=0, keepdims=True)
        dgrawb = dgraw.astype(BF)
        dgraw_ref[...] = dgrawb
        dh2 = _dot_nt(dgrawb, wup_ref[:, 0:D_FF]) + _dot_nt(dval_ref[...], wup_ref[:, D_FF:])
        x1 = x1_ref[...]
        dxa, dw = _rms_bwd(dh2, x1, _rstd(x1), wpf_ref[...])
        dwpf_ref[...] += dw
        dx1 = dx2_ref[...] + dxa
        dx1_ref[...] = dx1
        yv = y_ref[...]
        dy, dw = _rms_bwd(dx1, yv, _rstd(yv), wpm_ref[...])
        dwpm_ref[...] += dw
        dyb = dy.astype(BF)
        dy_ref[...] = dyb
        dys_ref[...] = _dot_nt(dyb, wout_ref[0:SSD_WIDTH, :]).astype(BF)
        dyr_ref[...] = _dot_nt(dyb, wout_ref[SSD_WIDTH:, :]).astype(BF)

    return pl.pallas_call(
        body, name="ffn_bwd", grid=(nt,),
        in_specs=[rspec(D_FF), rspec(D_FF), rspec(D_FF), rspec(D_MODEL), rspec(D_MODEL), rspec(D_MODEL),
                  _const_spec((8, D_FF)), _VMEM_WHOLE, _const_spec((1, D_MODEL)), _const_spec((1, D_MODEL)), _VMEM_WHOLE],
        out_specs=[rspec(D_FF), rspec(D_MODEL), rspec(D_MODEL), rspec(SSD_WIDTH), rspec(RET_V_W),
                   _const_spec((8, D_FF)), _const_spec((1, D_MODEL)), _const_spec((1, D_MODEL))],
        out_shape=[S((T, D_FF), BF), S((T, D_MODEL), F32), S((T, D_MODEL), BF), S((T, SSD_WIDTH), BF), S((T, RET_V_W), BF),
                   S((8, D_FF), F32), S((1, D_MODEL), F32), S((1, D_MODEL), F32)],
        scratch_shapes=[pltpu.VMEM((8, D_FF), F32)],
        compiler_params=_cparams("arbitrary"),
    )(dgate, dval, graw, x1, dx2, y, convw, wup, wpf, wpm, wout)


def _in_bwd(dz, dxbc, dq, dk, dv, dg, ddt, x, dx1, w0, wp, tm=256):
    T = x.shape[0]

    def body(dz_ref, dxbc_ref, dq_ref, dk_ref, dv_ref, dg_ref, ddt_ref, x_ref, dx1_ref, w0_ref, wp_ref, gx_ref, dw0_ref):
        @pl.when(pl.program_id(0) == 0)
        def _():
            dw0_ref[...] = jnp.zeros_like(dw0_ref)

        dh = jnp.zeros((tm, D_MODEL), F32)
        for ref, lo, hi in ((dz_ref, P_Z, P_XBC), (dxbc_ref, P_XBC, P_Q), (dq_ref, P_Q, P_K), (dk_ref, P_K, P_V),
                            (dv_ref, P_V, P_G), (dg_ref, P_G, P_DT), (ddt_ref, P_DT, P_END)):
            dh = dh + _dot_nt(ref[...], wp_ref[:, lo:hi])
        xf = x_ref[...]
        dx, dw = _rms_bwd(dh, xf, _rstd(xf), w0_ref[...])
        dw0_ref[...] += dw
        gx_ref[...] = dx1_ref[...] + dx

    widths = (SSD_WIDTH, SSD_XBC, RET_QK_W, RET_QK_W, RET_V_W, RET_V_W, LANES)
    return pl.pallas_call(
        body, name="in_bwd", grid=(T // tm,),
        in_specs=[_row_spec(tm, w) for w in widths] + [_row_spec(tm, D_MODEL), _row_spec(tm, D_MODEL),
                                                       _const_spec((1, D_MODEL)), _VMEM_WHOLE],
        out_specs=[_row_spec(tm, D_MODEL), _const_spec((1, D_MODEL))],
        out_shape=[S((T, D_MODEL), F32), S((1, D_MODEL), F32)],
        compiler_params=_cparams("arbitrary"),
    )(dz, dxbc, dq, dk, dv, dg, ddt, x, dx1, w0, wp)


def _matmul_tn(a, b, name, tk=512):
    T, M = a.shape
    N = b.shape[1]
    tn = N
    while M * tn * 4 > (4 << 20) and tn % 256 == 0:
        tn //= 2
    nk = T // tk

    def body(a_ref, b_ref, o_ref):
        @pl.when(pl.program_id(1) == 0)
        def _():
            o_ref[...] = jnp.zeros_like(o_ref)

        o_ref[...] += _dot_tn(a_ref[...], b_ref[...])

    return pl.pallas_call(
        body, name=name, grid=(N // tn, nk),
        in_specs=[pl.BlockSpec((tk, M), lambda n, k: (k, 0)), pl.BlockSpec((tk, tn), lambda n, k: (k, n))],
        out_specs=pl.BlockSpec((M, tn), lambda n, k: (0, n)),
        out_shape=S((M, N), F32),
        compiler_params=_cparams("parallel", "arbitrary"),
    )(a, b)


def _tri(lower):
    r = lax.broadcasted_iota(jnp.int32, (CH, CH), 0)
    c = lax.broadcasted_iota(jnp.int32, (CH, CH), 1)
    return ((c <= r) if lower else (r <= c)).astype(F32)


def _onehot_row(h):
    return (lax.broadcasted_iota(jnp.int32, (1, LANES), 1) == h).astype(F32)


def _onehot_col(h):
    return (lax.broadcasted_iota(jnp.int32, (LANES, 1), 0) == h).astype(F32)


def _ssd_pre(xc_ref, xh_ref, dtr_ref, cw_ref, cb_ref, dtb_ref, alog_ref, first):
    xc = xc_ref[...].astype(F32)
    xh = jnp.where(first, 0.0, xh_ref[...].astype(F32))
    ext = jnp.concatenate([xh, xc], axis=0)
    u = cb_ref[...] + sum(cw_ref[j:j + 1, :] * _rows_before(ext, SSD_CONV - 1 - j, HALO, CH) for j in range(SSD_CONV))
    sg = _sigmoid(u)
    act = u * sg
    dt = _softplus(dtr_ref[...] + dtb_ref[...])
    a = -jnp.exp(alog_ref[...])
    da = dt * a
    cs = _dot_hi(_tri(True), da)
    cst = _dot_tn_hi(da, _tri(False))
    return xc, u, sg, act, dt, a, cs, cst


def _ssd_head(h, act, dt, cs, cst, cb_g, sprev):
    causal = lax.broadcasted_iota(jnp.int32, (CH, CH), 0) >= lax.broadcasted_iota(jnp.int32, (CH, CH), 1)
    g = h // (SSD_HEADS // SSD_GROUPS)
    c_col = cs[:, h:h + 1]
    c_row = cst[h:h + 1, :]
    c_last = cs[CH - 1:CH, h:h + 1]
    lmat = jnp.exp(jnp.where(causal, c_col - c_row, -1e30))
    xs_h = act[:, h * SSD_HEAD_DIM:(h + 1) * SSD_HEAD_DIM]
    dt_col = dt[:, h:h + 1]
    xdt = xs_h * dt_col
    e_col = jnp.exp(c_col)
    decay = jnp.exp(c_last - c_col)
    wdec = xdt * decay
    cm_g = act[:, SSD_WIDTH + SSD_GROUPS * SSD_STATE + g * SSD_STATE:SSD_WIDTH + SSD_GROUPS * SSD_STATE + (g + 1) * SSD_STATE]
    yoff = _dot_nt(cm_g, sprev) * e_col
    ydiag = _dot(cb_g * lmat, xdt)
    return lmat, xs_h, dt_col, xdt, e_col, decay, wdec, c_last, ydiag, yoff


def _ssd_specs(T):
    nc = T // CH
    return nc, [
        _row_spec(CH, SSD_XBC),
        pl.BlockSpec((HALO, SSD_XBC), lambda i: (jnp.maximum(i * (CH // HALO) - 1, 0), 0)),
        _row_spec(CH, LANES),
        _row_spec(CH, SSD_WIDTH),
    ]


def _groups(act):
    bm = [act[:, SSD_WIDTH + g * SSD_STATE:SSD_WIDTH + (g + 1) * SSD_STATE] for g in range(SSD_GROUPS)]
    o = SSD_WIDTH + SSD_GROUPS * SSD_STATE
    cm = [act[:, o + g * SSD_STATE:o + (g + 1) * SSD_STATE] for g in range(SSD_GROUPS)]
    return bm, cm


def _ssd_fwd(xbc, dtr, z, convw, convb, dtb, alog, dsk, nw):
    T = xbc.shape[0]
    nc, specs = _ssd_specs(T)

    def body(xc_ref, xh_ref, dtr_ref, z_ref, cw_ref, cb_ref, dtb_ref, alog_ref, dsk_ref, nw_ref,
             out_ref, st_ref, state, ybuf):
        i = pl.program_id(0)

        @pl.when(i == 0)
        def _():
            state[...] = jnp.zeros_like(state)

        xc, u, sg, act, dt, a, cs, cst = _ssd_pre(xc_ref, xh_ref, dtr_ref, cw_ref, cb_ref, dtb_ref, alog_ref, i == 0)
        bm, cm = _groups(act)
        cb = [_dot_nt(cm[g], bm[g]) for g in range(SSD_GROUPS)]
        st_ref[0] = state[...]
        for h in range(SSD_HEADS):
            g = h // (SSD_HEADS // SSD_GROUPS)
            sprev = state[h]
            lmat, xs_h, dt_col, xdt, e_col, decay, wdec, c_last, ydiag, yoff = _ssd_head(h, act, dt, cs, cst, cb[g], sprev)
            state[h] = sprev * jnp.exp(c_last) + _dot_tn(wdec, bm[g])
            ybuf[:, h * SSD_HEAD_DIM:(h + 1) * SSD_HEAD_DIM] = ydiag + yoff + xs_h * dsk_ref[:, h:h + 1]
        zf = z_ref[...].astype(F32)
        gated = ybuf[...] * (zf * _sigmoid(zf))
        out_ref[...] = (gated * _rstd(gated) * nw_ref[...]).astype(BF)

    return pl.pallas_call(
        body, name="ssd_fwd", grid=(nc,),
        in_specs=specs + [_const_spec((8, SSD_XBC)), _const_spec((1, SSD_XBC)), _const_spec((1, LANES)),
                          _const_spec((1, LANES)), _const_spec((1, LANES)), _const_spec((1, SSD_WIDTH))],
        out_specs=[_row_spec(CH, SSD_WIDTH),
                   pl.BlockSpec((1, SSD_HEADS, SSD_HEAD_DIM, SSD_STATE), lambda i: (i, 0, 0, 0))],
        out_shape=[S((T, SSD_WIDTH), BF), S((nc, SSD_HEADS, SSD_HEAD_DIM, SSD_STATE), F32)],
        scratch_shapes=[pltpu.VMEM((SSD_HEADS, SSD_HEAD_DIM, SSD_STATE), F32), pltpu.VMEM((CH, SSD_WIDTH), F32)],
        compiler_params=_cparams("arbitrary"),
    )(xbc, xbc, dtr, z, convw, convb, dtb, alog, dsk, nw)


def _ssd_bwd(dout, xbc, dtr, z, states, convw, convb, dtb, alog, dsk, nw):
    T = xbc.shape[0]
    nc = T // CH
    rev = lambda i: (nc - 1 - i, 0)
    rspec = lambda w: pl.BlockSpec((CH, w), rev)
    halo_spec = pl.BlockSpec((HALO, SSD_XBC), lambda i: (jnp.maximum((nc - 1 - i) * (CH // HALO) - 1, 0), 0))
    NB = SSD_WIDTH
    NC_ = SSD_WIDTH + SSD_GROUPS * SSD_STATE

    def body(do_ref, xc_ref, xh_ref, dtr_ref, z_ref, st_ref, cw_ref, cb_ref, dtb_ref, alog_ref, dsk_ref, nw_ref,
             dz_ref, dxbc_ref, ddt_ref, dcw_ref, dcb_ref, ddtb_ref, dalog_ref, ddsk_ref, dnw_ref,
             dstate, ducarry, ybuf, yoffbuf, lbuf, dact):
        i = pl.program_id(0)

        @pl.when(i == 0)
        def _():
            dstate[...] = jnp.zeros_like(dstate)
            ducarry[...] = jnp.zeros_like(ducarry)
            for ref in (dcw_ref, dcb_ref, ddtb_ref, dalog_ref, ddsk_ref, dnw_ref):
                ref[...] = jnp.zeros_like(ref)

        xc, u, sg, act, dt, a, cs, cst = _ssd_pre(xc_ref, xh_ref, dtr_ref, cw_ref, cb_ref, dtb_ref, alog_ref, i == nc - 1)
        bm, cm = _groups(act)
        cb = [_dot_nt(cm[g], bm[g]) for g in range(SSD_GROUPS)]
        for h in range(SSD_HEADS):
            g = h // (SSD_HEADS // SSD_GROUPS)
            lmat, xs_h, dt_col, xdt, e_col, decay, wdec, c_last, ydiag, yoff = _ssd_head(h, act, dt, cs, cst, cb[g], st_ref[0, h])
            lbuf[h] = lmat
            sl = slice(h * SSD_HEAD_DIM, (h + 1) * SSD_HEAD_DIM)
            yoffbuf[:, sl] = yoff
            ybuf[:, sl] = ydiag + yoff + xs_h * dsk_ref[:, h:h + 1]
        yv = ybuf[...]
        zf = z_ref[...].astype(F32)
        sz = _sigmoid(zf)
        gated = yv * (zf * sz)
        dgated, dnw = _rms_bwd(do_ref[...].astype(F32), gated, _rstd(gated), nw_ref[...])
        dnw_ref[...] += dnw
        dz_ref[...] = (dgated * yv * _dsilu(zf, sz)).astype(BF)
        ybuf[...] = dgated * (zf * sz)
        dcs = jnp.zeros((CH, LANES), F32)
        dcst = jnp.zeros((LANES, CH), F32)
        dlast = jnp.zeros((1, LANES), F32)
        ddt = jnp.zeros((CH, LANES), F32)
        ddsk = jnp.zeros((1, LANES), F32)
        dcb = [jnp.zeros((CH, CH), F32) for _ in range(SSD_GROUPS)]
        dbm = [jnp.zeros((CH, SSD_STATE), F32) for _ in range(SSD_GROUPS)]
        dcm = [jnp.zeros((CH, SSD_STATE), F32) for _ in range(SSD_GROUPS)]
        for h in range(SSD_HEADS):
            g = h // (SSD_HEADS // SSD_GROUPS)
            sl = slice(h * SSD_HEAD_DIM, (h + 1) * SSD_HEAD_DIM)
            oh = _onehot_row(h)
            sprev = st_ref[0, h]
            dsn = dstate[h]
            c_col = cs[:, h:h + 1]
            c_last = cs[CH - 1:CH, h:h + 1]
            xs_h = act[:, sl]
            dt_col = dt[:, h:h + 1]
            xdt = xs_h * dt_col
            e_col = jnp.exp(c_col)
            decay = jnp.exp(c_last - c_col)
            wdec = xdt * decay
            e_last = jnp.exp(c_last)
            lmat = lbuf[h]
            mmat = cb[g] * lmat
            dy_h = ybuf[:, sl]
            dm = _dot_nt(dy_h, xdt)
            dxdt = _dot_tn(mmat, dy_h)
            dseg = dm * mmat
            dcb[g] = dcb[g] + dm * lmat
            col = jnp.sum(dseg, axis=1, keepdims=True)
            dcst = dcst - _onehot_col(h) * jnp.sum(dseg, axis=0, keepdims=True)
            col = col + jnp.sum(dy_h * yoffbuf[:, sl], axis=1, keepdims=True)
            dq = dy_h * e_col
            dcm[g] = dcm[g] + _dot(dq, sprev)
            dsp = _dot_tn(dq, cm[g])
            dsp = dsp + dsn * e_last
            dl = jnp.sum(dsn * sprev) * e_last
            dw = _dot_nt(bm[g], dsn)
            dbm[g] = dbm[g] + _dot(wdec, dsn)
            dxdt = dxdt + dw * decay
            vv = jnp.sum(dw * wdec, axis=1, keepdims=True)
            col = col - vv
            dl = dl + jnp.sum(vv)
            dlast = dlast + dl * oh
            dcs = dcs + col * oh
            dstate[h] = dsp
            dact[:, sl] = dy_h * dsk_ref[:, h:h + 1] + dxdt * dt_col
            ddt = ddt + jnp.sum(dxdt * xs_h, axis=1, keepdims=True) * oh
            ddsk = ddsk + jnp.sum(dy_h * xs_h) * oh
        for g in range(SSD_GROUPS):
            dact[:, NB + g * SSD_STATE:NB + (g + 1) * SSD_STATE] = dbm[g] + _dot_tn(dcb[g], cm[g])
            dact[:, NC_ + g * SSD_STATE:NC_ + (g + 1) * SSD_STATE] = dcm[g] + _dot(dcb[g], bm[g])
        ddsk_ref[...] += ddsk
        rows = lax.broadcasted_iota(jnp.int32, (CH, LANES), 0)
        dcs = dcs + _dot_tn_hi(dcst, jnp.eye(LANES, dtype=F32)) + jnp.where(rows == CH - 1, dlast, 0.0)
        dda = _dot_hi(_tri(False), dcs)
        dalog_ref[...] += jnp.sum(dda * dt, axis=0, keepdims=True) * a
        ddt = ddt + dda * a
        ddtr = ddt * _sigmoid(dtr_ref[...] + dtb_ref[...])
        ddtb_ref[...] += jnp.sum(ddtr, axis=0, keepdims=True)
        ddt_ref[...] = ddtr.astype(BF)
        du = dact[...] * _dsilu(u, sg)
        dcb_ref[...] += jnp.sum(du, axis=0, keepdims=True)
        ext = jnp.concatenate([du, ducarry[...]], axis=0)
        ducarry[...] = du[0:8]
        dx = jnp.zeros((CH, SSD_XBC), F32)
        for j in range(SSD_CONV):
            sj = _rows_after(ext, SSD_CONV - 1 - j, CH)
            dx = dx + cw_ref[j:j + 1, :] * sj
            dcw_ref[j:j + 1, :] += jnp.sum(sj * xc, axis=0, keepdims=True)
        dxbc_ref[...] = dx.astype(BF)

    return pl.pallas_call(
        body, name="ssd_bwd", grid=(nc,),
        in_specs=[rspec(SSD_WIDTH), rspec(SSD_XBC), halo_spec, rspec(LANES), rspec(SSD_WIDTH),
                  pl.BlockSpec((1, SSD_HEADS, SSD_HEAD_DIM, SSD_STATE), lambda i: (nc - 1 - i, 0, 0, 0)),
                  _const_spec((8, SSD_XBC)), _const_spec((1, SSD_XBC)), _const_spec((1, LANES)),
                  _const_spec((1, LANES)), _const_spec((1, LANES)), _const_spec((1, SSD_WIDTH))],
        out_specs=[rspec(SSD_WIDTH), rspec(SSD_XBC), rspec(LANES),
                   _const_spec((8, SSD_XBC)), _const_spec((1, SSD_XBC)), _const_spec((1, LANES)),
                   _const_spec((1, LANES)), _const_spec((1, LANES)), _const_spec((1, SSD_WIDTH))],
        out_shape=[S((T, SSD_WIDTH), BF), S((T, SSD_XBC), BF), S((T, LANES), BF),
                   S((8, SSD_XBC), F32), S((1, SSD_XBC), F32), S((1, LANES), F32),
                   S((1, LANES), F32), S((1, LANES), F32), S((1, SSD_WIDTH), F32)],
        scratch_shapes=[pltpu.VMEM((SSD_HEADS, SSD_HEAD_DIM, SSD_STATE), F32), pltpu.VMEM((8, SSD_XBC), F32),
                        pltpu.VMEM((CH, SSD_WIDTH), F32), pltpu.VMEM((CH, SSD_WIDTH), F32),
                        pltpu.VMEM((SSD_HEADS, CH, CH), F32), pltpu.VMEM((CH, SSD_XBC), F32)],
        compiler_params=_cparams("arbitrary"),
    )(dout, xbc, xbc, dtr, z, states, convw, convb, dtb, alog, dsk, nw)


def _log_gamma(h):
    return float(np.log1p(-np.exp2(np.float32(-5.0 - h)), dtype=np.float32))


def _swap_halves(t):
    n = t.shape[1]
    lane = lax.broadcasted_iota(jnp.int32, t.shape, 1)
    return jnp.where((lane & (RET_QK - 1)) < RET_QK // 2, pltpu.roll(t, n - RET_QK // 2, 1), pltpu.roll(t, RET_QK // 2, 1))


def _rot(t, cos, sin):
    return t * cos + _swap_halves(t) * sin


def _rot_t(d, cos, sin):
    return d * cos + _swap_halves(d * sin)


def _ret_consts(h):
    lg = _log_gamma(h)
    r = lax.broadcasted_iota(jnp.int32, (CH, CH), 0)
    c = lax.broadcasted_iota(jnp.int32, (CH, CH), 1)
    rel = (r - c).astype(F32)
    dmask = jnp.where(rel >= 0, jnp.exp(lg * jnp.maximum(rel, 0.0)), 0.0)
    pos = lax.broadcasted_iota(jnp.int32, (CH, 1), 0).astype(F32)
    kdec = jnp.exp(lg * (CH - 1.0 - pos))
    qdec = jnp.exp(lg * (pos + 1.0))
    return dmask, kdec, qdec, math.exp(lg * CH)


def _ret_fwd(q, k, v, g, cos, sin, nw):
    T = q.shape[0]
    nc = T // CH

    def body(q_ref, k_ref, v_ref, g_ref, cos_ref, sin_ref, nw_ref, out_ref, st_ref, state):
        i = pl.program_id(0)

        @pl.when(i == 0)
        def _():
            state[...] = jnp.zeros_like(state)

        cosf = jnp.tile(cos_ref[...], (1, RET_QK_W // LANES))
        sinf = jnp.tile(sin_ref[...], (1, RET_QK_W // LANES))
        qr = _rot(q_ref[...].astype(F32), cosf, sinf)
        kr = _rot(k_ref[...].astype(F32), cosf, sinf) * (RET_QK ** -0.5)
        st_ref[0] = state[...]
        for h in range(RET_HEADS):
            dmask, kdec, qdec, gam = _ret_consts(h)
            q_h = qr[:, h * RET_QK:(h + 1) * RET_QK]
            k_h = kr[:, h * RET_QK:(h + 1) * RET_QK]
            sl = slice(h * RET_V, (h + 1) * RET_V)
            v_h = v_ref[:, sl]
            rprev = state[h]
            scores = _dot_nt(q_h, k_h) * dmask
            o = _dot(scores, v_h) + _dot(q_h * qdec, rprev)
            state[h] = rprev * gam + _dot_tn(k_h * kdec, v_h)
            gf = g_ref[:, sl].astype(F32)
            out_ref[:, sl] = (o * _rstd(o) * nw_ref[:, sl] * (gf * _sigmoid(gf))).astype(BF)

    return pl.pallas_call(
        body, name="ret_fwd", grid=(nc,),
        in_specs=[_row_spec(CH, RET_QK_W), _row_spec(CH, RET_QK_W), _row_spec(CH, RET_V_W), _row_spec(CH, RET_V_W),
                  _row_spec(CH, LANES), _row_spec(CH, LANES), _const_spec((1, RET_V_W))],
        out_specs=[_row_spec(CH, RET_V_W), pl.BlockSpec((1, RET_HEADS, RET_QK, RET_V), lambda i: (i, 0, 0, 0))],
        out_shape=[S((T, RET_V_W), BF), S((nc, RET_HEADS, RET_QK, RET_V), F32)],
        scratch_shapes=[pltpu.VMEM((RET_HEADS, RET_QK, RET_V), F32)],
        compiler_params=_cparams("arbitrary"),
    )(q, k, v, g, cos, sin, nw)


def _ret_bwd(dout, q, k, v, g, states, cos, sin, nw):
    T = q.shape[0]
    nc = T // CH
    rev = lambda i: (nc - 1 - i, 0)
    rspec = lambda w: pl.BlockSpec((CH, w), rev)

    def body(do_ref, q_ref, k_ref, v_ref, g_ref, st_ref, cos_ref, sin_ref, nw_ref,
             dq_ref, dk_ref, dv_ref, dg_ref, dnw_ref, dstate, dqbuf, dkbuf):
        i = pl.program_id(0)

        @pl.when(i == 0)
        def _():
            dstate[...] = jnp.zeros_like(dstate)
            dnw_ref[...] = jnp.zeros_like(dnw_ref)

        cosf = jnp.tile(cos_ref[...], (1, RET_QK_W // LANES))
        sinf = jnp.tile(sin_ref[...], (1, RET_QK_W // LANES))
        qr = _rot(q_ref[...].astype(F32), cosf, sinf)
        kr = _rot(k_ref[...].astype(F32), cosf, sinf) * (RET_QK ** -0.5)
        for h in range(RET_HEADS):
            dmask, kdec, qdec, gam = _ret_consts(h)
            q_h = qr[:, h * RET_QK:(h + 1) * RET_QK]
            k_h = kr[:, h * RET_QK:(h + 1) * RET_QK]
            sl = slice(h * RET_V, (h + 1) * RET_V)
            v_h = v_ref[:, sl]
            rprev = st_ref[0, h]
            gnext = dstate[h]
            qd = q_h * qdec
            kd = k_h * kdec
            scores = _dot_nt(q_h, k_h) * dmask
            o = _dot(scores, v_h) + _dot(qd, rprev)
            rr = _rstd(o)
            of = o * rr
            gf = g_ref[:, sl].astype(F32)
            sgg = _sigmoid(gf)
            d_h = do_ref[:, sl].astype(F32)
            nw_h = nw_ref[:, sl]
            dg_ref[:, sl] = (d_h * of * nw_h * _dsilu(gf, sgg)).astype(BF)
            dt_ = d_h * (gf * sgg)
            dnw_ref[:, sl] += jnp.sum(dt_ * of, axis=0, keepdims=True)
            dof = dt_ * nw_h
            do = rr * dof - o * (rr * rr * rr) * jnp.mean(dof * o, axis=-1, keepdims=True)
            dsc = _dot_nt(do, v_h) * dmask
            dv_ref[:, sl] = (_dot_tn(scores, do) + _dot(kd, gnext)).astype(BF)
            dqbuf[:, h * RET_QK:(h + 1) * RET_QK] = _dot(dsc, k_h) + _dot_nt(do, rprev) * qdec
            dkbuf[:, h * RET_QK:(h + 1) * RET_QK] = _dot_tn(dsc, q_h) + _dot_nt(v_h, gnext) * kdec
            dstate[h] = gnext * gam + _dot_tn(qd, do)
        dq_ref[...] = _rot_t(dqbuf[...], cosf, sinf).astype(BF)
        dk_ref[...] = (_rot_t(dkbuf[...], cosf, sinf) * (RET_QK ** -0.5)).astype(BF)

    return pl.pallas_call(
        body, name="ret_bwd", grid=(nc,),
        in_specs=[rspec(RET_V_W), rspec(RET_QK_W), rspec(RET_QK_W), rspec(RET_V_W), rspec(RET_V_W),
                  pl.BlockSpec((1, RET_HEADS, RET_QK, RET_V), lambda i: (nc - 1 - i, 0, 0, 0)),
                  rspec(LANES), rspec(LANES), _const_spec((1, RET_V_W))],
        out_specs=[rspec(RET_QK_W), rspec(RET_QK_W), rspec(RET_V_W), rspec(RET_V_W), _const_spec((1, RET_V_W))],
        out_shape=[S((T, RET_QK_W), BF), S((T, RET_QK_W), BF), S((T, RET_V_W), BF), S((T, RET_V_W), BF),
                   S((1, RET_V_W), F32)],
        scratch_shapes=[pltpu.VMEM((RET_HEADS, RET_QK, RET_V), F32), pltpu.VMEM((CH, RET_QK_W), F32),
                        pltpu.VMEM((CH, RET_QK_W), F32)],
        compiler_params=_cparams("arbitrary"),
    )(dout, q, k, v, g, states, cos, sin, nw)


def _exchange(bufs, name, same):
    nb = len(bufs)
    slabs = [tuple(b.shape if same else b.shape[1:]) for b in bufs]

    def body(*refs):
        buf_refs, out_refs = refs[:nb], refs[nb:2 * nb]
        send_sems, recv_sems, local_sems = refs[2 * nb:]
        x, y, c = lax.axis_index("x"), lax.axis_index("y"), lax.axis_index("c")
        me = 4 * x + 2 * y + c

        def src(b, d):
            return buf_refs[b] if same else buf_refs[b].at[d]

        def remote(b, k, to_me):
            px = 1 - x if k & 4 else x
            py = 1 - y if k & 2 else y
            pc = 1 - c if k & 1 else c
            p = 4 * px + 2 * py + pc
            s = b * (N_DEV - 1) + k - 1
            return pltpu.make_async_remote_copy(
                src_ref=src(b, p), dst_ref=out_refs[b].at[me if to_me else p], send_sem=send_sems.at[s],
                recv_sem=recv_sems.at[s], device_id=(px, py, pc), device_id_type=pl.DeviceIdType.MESH)

        local = [pltpu.make_async_copy(src(b, me), out_refs[b].at[me], local_sems.at[b]) for b in range(nb)]
        for cp in local:
            cp.start()
        sends = [remote(b, k, True) for k in range(1, N_DEV) for b in range(nb)]
        for cp in sends:
            cp.start()
        for k in range(1, N_DEV):
            for b in range(nb):
                remote(b, k, False).wait_recv()
        for cp in sends:
            cp.wait_send()
        for cp in local:
            cp.wait()

    any_spec = pl.BlockSpec(memory_space=pl.ANY)
    return pl.pallas_call(
        body, name=name,
        in_specs=[any_spec] * nb, out_specs=[any_spec] * nb,
        out_shape=[S((N_DEV,) + s, b.dtype) for s, b in zip(slabs, bufs)],
        scratch_shapes=[pltpu.SemaphoreType.DMA((nb * (N_DEV - 1),)), pltpu.SemaphoreType.DMA((nb * (N_DEV - 1),)),
                        pltpu.SemaphoreType.DMA((nb,))],
    )(*bufs)


def _sum_slabs(recv, name):
    n, R, _ = recv.shape

    def body(r_ref, o_ref):
        g = r_ref[0].astype(F32)
        for s in range(1, n):
            g = g + r_ref[s].astype(F32)
        o_ref[...] = g

    return pl.pallas_call(body, name=name, out_shape=S((R, LANES), F32))(recv)


def _adamw(recv, w, m, v, name, tr):
    n, R, C = recv.shape
    c1 = 1.0 - ADAM_B1 ** ADAM_STEP
    c2 = 1.0 - ADAM_B2 ** ADAM_STEP

    def body(r_ref, w_ref, m_ref, v_ref, g_out, d_out, m_out, v_out):
        g = r_ref[0].astype(F32)
        for s in range(1, n):
            g = g + r_ref[s].astype(F32)
        mm = ADAM_B1 * m_ref[...] + (1.0 - ADAM_B1) * g
        vv = ADAM_B2 * v_ref[...] + (1.0 - ADAM_B2) * (g * g)
        g_out[...] = g
        m_out[...] = mm
        v_out[...] = vv
        d_out[...] = -ADAM_LR * ((mm / c1) / (jnp.sqrt(vv / c2) + ADAM_EPS) + ADAM_WD * w_ref[...])

    spec = pl.BlockSpec((tr, C), lambda i: (i, 0))
    return pl.pallas_call(
        body, name=name, grid=(R // tr,),
        in_specs=[pl.BlockSpec((n, tr, C), lambda i: (0, i, 0)), spec, spec, spec],
        out_specs=[spec] * 4, out_shape=[S((R, C), F32)] * 4,
        compiler_params=_cparams("parallel"),
    )(recv, w, m, v)


def _pack(parts, rows):
    cols = []
    for p in parts:
        f = p.reshape(-1)
        cols.append(jnp.pad(f, (0, (-f.shape[0]) % LANES)))
    flat = jnp.concatenate(cols)
    return jnp.pad(flat, (0, rows * LANES - flat.shape[0])).reshape(rows, LANES)


def _unpack(buf, shapes):
    flat = buf.reshape(-1)
    out, o = [], 0
    for shp in shapes:
        n = int(np.prod(shp))
        out.append(flat[o:o + n].reshape(shp))
        o += n + (-n) % LANES
    return out


SMALL_ROWS = 200
CONV_ROWS = 16


def kernel(x, pre_mix_norm_w, w_in, ssd_conv_w, ssd_conv_b, ssd_dt_bias, ssd_a_log, ssd_d, ssd_norm_w, ret_norm_w, w_out, post_mix_norm_w, pre_ffn_norm_w, w_up, ffn_conv_w, ffn_conv_b, w_down, post_ffn_norm_w, loss_target, m_pre_mix_norm_w, m_w_in, m_ssd_conv_w, m_ssd_conv_b, m_ssd_dt_bias, m_ssd_a_log, m_ssd_d, m_ssd_norm_w, m_ret_norm_w, m_w_out, m_post_mix_norm_w, m_pre_ffn_norm_w, m_w_up, m_ffn_conv_w, m_ffn_conv_b, m_w_down, m_post_ffn_norm_w, v_pre_mix_norm_w, v_w_in, v_ssd_conv_w, v_ssd_conv_b, v_ssd_dt_bias, v_ssd_a_log, v_ssd_d, v_ssd_norm_w, v_ret_norm_w, v_w_out, v_post_mix_norm_w, v_pre_ffn_norm_w, v_w_up, v_ffn_conv_w, v_ffn_conv_b, v_w_down, v_post_ffn_norm_w):
    T = x.shape[1]
    xi, tgt = x[0], loss_target[0]
    me = 4 * lax.axis_index("x") + 2 * lax.axis_index("y") + lax.axis_index("c")
    n_in, n_up = w_in.shape[2], w_up.shape[2]
    n_out, n_down = w_out.shape[1], w_down.shape[1]
    n_sc, n_fc = ssd_conv_w.shape[2], ffn_conv_w.shape[2]

    g_in, g_out, g_up, g_down = _exchange([w_in[0].astype(BF), w_out[0].astype(BF), w_up[0].astype(BF), w_down[0].astype(BF)],
                                          "gather_w", True)
    gconv, = _exchange([_pack([ssd_conv_w, ffn_conv_w], CONV_ROWS)], "gather_conv", True)
    win = jnp.transpose(g_in, (1, 0, 2)).reshape(D_MODEL, N_DEV * n_in)
    wout = g_out.reshape(N_DEV * n_out, D_MODEL)
    wup = jnp.transpose(g_up, (1, 0, 2)).reshape(D_MODEL, N_DEV * n_up)
    wdown = g_down.reshape(N_DEV * n_down, D_MODEL)
    convs = [_unpack(gconv[d], [(SSD_CONV, n_sc), (FFN_CONV, n_fc)]) for d in range(N_DEV)]
    scw = jnp.pad(jnp.concatenate([c[0] for c in convs], axis=1), ((0, 8 - SSD_CONV), (0, 0)))
    fcw = jnp.pad(jnp.concatenate([c[1] for c in convs], axis=1), ((0, 8 - FFN_CONV), (0, 0)))
    wp = jnp.concatenate([win[:, O_Z:O_XBC], win[:, O_XBC:O_DT], win[:, O_Q:O_K], win[:, O_K:O_V], win[:, O_V:O_G],
                          win[:, O_G:], win[:, O_DT:O_Q], jnp.zeros((D_MODEL, P_END - P_DT - SSD_HEADS), win.dtype)], axis=1)

    pad_h = lambda p: jnp.pad(p, ((0, 0), (0, LANES - SSD_HEADS)))
    dtb, alog, dsk = pad_h(ssd_dt_bias), pad_h(ssd_a_log), pad_h(ssd_d)
    inv = ROPE_BASE ** (-jnp.arange(0, RET_QK, 2, dtype=F32) / RET_QK)
    ang = jnp.arange(T, dtype=F32)[:, None] * inv[None, :]
    cs_, sn_ = jnp.cos(ang), jnp.sin(ang)
    cos = jnp.concatenate([cs_, cs_, cs_, cs_], axis=1)
    sin = jnp.concatenate([-sn_, sn_, -sn_, sn_], axis=1)

    h, z, xbc, q, k, v, g, dtr = _fwd_in(xi, pre_mix_norm_w, wp)
    ys, sst = _ssd_fwd(xbc, dtr, z, scw, ssd_conv_b, dtb, alog, dsk, ssd_norm_w)
    yr, rst = _ret_fwd(q, k, v, g, cos, sin, ret_norm_w)
    y, x1, h2, graw, val = _fwd_mid(ys, yr, xi, wout, post_mix_norm_w, pre_ffn_norm_w, wup)
    a, dfb, dval, dgate, dx2, lossb, d_pff, d_fcb = _ffn_tail(graw, val, x1, tgt, fcw, ffn_conv_b, wdown, post_ffn_norm_w)
    dgraw, dx1, dyb, dys, dyr, d_fcw, d_pf, d_pm = _ffn_bwd(dgate, dval, graw, x1, dx2, y, fcw, wup, pre_ffn_norm_w,
                                                         post_mix_norm_w, wout)
    dz, dxbc, ddt, d_scw, d_scb, d_dtb, d_alog, d_dsk, d_snw = _ssd_bwd(dys, xbc, dtr, z, sst, scw, ssd_conv_b, dtb, alog,
                                                                      dsk, ssd_norm_w)
    dq, dk, dv, dg, d_rnw = _ret_bwd(dyr, q, k, v, g, rst, cos, sin, ret_norm_w)
    gx, d_w0 = _in_bwd(dz, dxbc, dq, dk, dv, dg, ddt, xi, dx1, pre_mix_norm_w, wp)
    gin = jnp.concatenate([_matmul_tn(h, dz, "dw_z"), _matmul_tn(h, dxbc, "dw_xbc"),
                           _matmul_tn(h, ddt, "dw_dt")[:, :SSD_HEADS], _matmul_tn(h, dq, "dw_q"), _matmul_tn(h, dk, "dw_k"),
                           _matmul_tn(h, dv, "dw_v"), _matmul_tn(h, dg, "dw_g")], axis=1)
    gout = jnp.concatenate([_matmul_tn(ys, dyb, "dw_out_s"), _matmul_tn(yr, dyb, "dw_out_r")], axis=0)
    gup = jnp.concatenate([_matmul_tn(h2, dgraw, "dw_up_g"), _matmul_tn(h2, dval, "dw_up_v")], axis=1)
    gdown = _matmul_tn(a, dfb, "dw_down")
    send = [jnp.transpose(gin.reshape(D_MODEL, N_DEV, n_in), (1, 0, 2)).astype(BF), gout.reshape(N_DEV, n_out, D_MODEL).astype(BF),
            jnp.transpose(gup.reshape(D_MODEL, N_DEV, n_up), (1, 0, 2)).astype(BF), gdown.reshape(N_DEV, n_down, D_MODEL).astype(BF)]
    recv = _exchange(send, "scatter_g", False)
    per_w = [_adamw(r, w[0], m[0], v[0], nm, tr) for r, w, m, v, nm, tr in (
        (recv[0], w_in, m_w_in, v_w_in, "adamw_in", 256), (recv[1], w_out, m_w_out, v_w_out, "adamw_out", n_out),
        (recv[2], w_up, m_w_up, v_w_up, "adamw_up", 256), (recv[3], w_down, m_w_down, v_w_down, "adamw_down", n_down))]
    big = [[per_w[i][kind][None] for i in range(4)] for kind in range(4)]

    small_full = [d_w0, d_scw[:SSD_CONV], d_scb, d_dtb[:, :SSD_HEADS], d_alog[:, :SSD_HEADS], d_dsk[:, :SSD_HEADS], d_snw, d_rnw,
                  d_pm, d_pf, d_fcw[:FFN_CONV], d_fcb, d_pff]
    full_shapes = [t.shape for t in small_full]
    gs = _sum_slabs(_exchange([_pack(small_full, SMALL_ROWS)], "gather_small", True)[0], "sum_small")
    gfull = _unpack(gs, full_shapes)
    gfull[1] = lax.dynamic_slice_in_dim(gfull[1], me * n_sc, n_sc, axis=1)
    gfull[10] = lax.dynamic_slice_in_dim(gfull[10], me * n_fc, n_fc, axis=1)
    ws = [pre_mix_norm_w, ssd_conv_w, ssd_conv_b, ssd_dt_bias, ssd_a_log, ssd_d, ssd_norm_w, ret_norm_w, post_mix_norm_w,
          pre_ffn_norm_w, ffn_conv_w, ffn_conv_b, post_ffn_norm_w]
    ms = [m_pre_mix_norm_w, m_ssd_conv_w, m_ssd_conv_b, m_ssd_dt_bias, m_ssd_a_log, m_ssd_d, m_ssd_norm_w, m_ret_norm_w,
          m_post_mix_norm_w, m_pre_ffn_norm_w, m_ffn_conv_w, m_ffn_conv_b, m_post_ffn_norm_w]
    vs = [v_pre_mix_norm_w, v_ssd_conv_w, v_ssd_conv_b, v_ssd_dt_bias, v_ssd_a_log, v_ssd_d, v_ssd_norm_w, v_ret_norm_w,
          v_post_mix_norm_w, v_pre_ffn_norm_w, v_ffn_conv_w, v_ffn_conv_b, v_post_ffn_norm_w]
    out_shapes = [t.shape for t in ws]
    small = _adamw(_pack(gfull, SMALL_ROWS)[None], _pack(ws, SMALL_ROWS), _pack(ms, SMALL_ROWS), _pack(vs, SMALL_ROWS),
                   "adamw_small", SMALL_ROWS)
    small = [_unpack(b, out_shapes) for b in small]

    order = {"pre_mix_norm_w": ("s", 0), "w_in": ("b", 0), "ssd_conv_w": ("s", 1), "ssd_conv_b": ("s", 2),
             "ssd_dt_bias": ("s", 3), "ssd_a_log": ("s", 4), "ssd_d": ("s", 5), "ssd_norm_w": ("s", 6), "ret_norm_w": ("s", 7),
             "w_out": ("b", 1), "post_mix_norm_w": ("s", 8), "pre_ffn_norm_w": ("s", 9), "w_up": ("b", 2),
             "ffn_conv_w": ("s", 10), "ffn_conv_b": ("s", 11), "w_down": ("b", 3), "post_ffn_norm_w": ("s", 12)}
    loss = lax.psum(lossb[0, 0], ("x", "y", "c"))
    outs = [loss, gx[None]]
    for kind in range(4):
        for name, (grp, idx) in order.items():
            outs.append(big[kind][idx] if grp == "b" else small[kind][idx])
    return tuple(outs)
```

```python
import functools
import math

import numpy as np
import jax
import jax.numpy as jnp
from jax import lax
from jax.experimental import pallas as pl
from jax.experimental.pallas import tpu as pltpu

F32 = jnp.float32
BF = jnp.bfloat16
HI = lax.Precision.HIGHEST
S = jax.ShapeDtypeStruct

D_MODEL = 1024
SSD_HEADS = 16
SSD_HEAD_DIM = 64
SSD_GROUPS = 2
SSD_STATE = 128
SSD_WIDTH = 1024
SSD_XBC = 1536
SSD_CONV = 4
RET_HEADS = 8
RET_QK = 64
RET_V = 128
RET_QK_W = 512
RET_V_W = 1024
ROPE_BASE = 10000.0
CH = 128
D_FF = 2816
FFN_CONV = 3
EPS = 1e-6
IN_WIDTH = 5648
N_DEV = 8

ADAM_LR = 0.001
ADAM_B1 = 0.9
ADAM_B2 = 0.999
ADAM_EPS = 1e-08
ADAM_WD = 0.01
ADAM_STEP = 10

LANES = 128
HALO = 16
VMEM_LIMIT = 48 * 1024 * 1024

P_Z, P_XBC, P_Q, P_K, P_V, P_G, P_DT, P_END = 0, 1024, 2560, 3072, 3584, 4608, 5632, 5760
O_Z, O_XBC, O_DT, O_Q, O_K, O_V, O_G = 0, 1024, 2560, 2576, 3088, 3600, 4624


def _cparams(*sem):
    return pltpu.CompilerParams(dimension_semantics=sem, vmem_limit_bytes=VMEM_LIMIT)


def _dot(a, b):
    return jnp.dot(a.astype(BF), b.astype(BF), preferred_element_type=F32)


def _dot_nt(a, b):
    return lax.dot_general(a.astype(BF), b.astype(BF), (((1,), (1,)), ((), ())), preferred_element_type=F32)


def _dot_tn(a, b):
    return lax.dot_general(a.astype(BF), b.astype(BF), (((0,), (0,)), ((), ())), preferred_element_type=F32)


def _dot_hi(a, b):
    return jnp.dot(a, b, preferred_element_type=F32, precision=HI)


def _dot_tn_hi(a, b):
    return lax.dot_general(a, b, (((0,), (0,)), ((), ())), preferred_element_type=F32, precision=HI)


def _sigmoid(x):
    return jax.nn.sigmoid(x)


def _dsilu(x, s):
    return s * (1.0 + x * (1.0 - s))


def _softplus(x):
    return jnp.maximum(x, 0.0) + jnp.log1p(jnp.exp(-jnp.abs(x)))


def _rstd(x):
    return lax.rsqrt(jnp.mean(x * x, axis=-1, keepdims=True) + EPS)


def _rms_bwd(dy, x, r, w):
    gn = dy * w
    dx = r * gn - x * (r * r * r) * jnp.mean(gn * x, axis=-1, keepdims=True)
    dw = jnp.sum(dy * x * r, axis=0, keepdims=True)
    return dx, dw


def _rows_before(ext, s, head, n):
    if s == 0:
        return ext[head:head + n]
    return pltpu.roll(ext, s, 0)[head:head + n]


def _rows_after(ext, s, n):
    if s == 0:
        return ext[0:n]
    return pltpu.roll(ext, ext.shape[0] - s, 0)[0:n]


def _row_spec(tm, width):
    return pl.BlockSpec((tm, width), lambda i: (i, 0))


def _const_spec(shape):
    return pl.BlockSpec(shape, lambda i: (0,) * len(shape))


_VMEM_WHOLE = pl.BlockSpec(memory_space=pltpu.VMEM)


def _fwd_in(x, w0, wp, tm=256):
    T = x.shape[0]

    def body(x_ref, w0_ref, wp_ref, h_ref, z_ref, xbc_ref, q_ref, k_ref, v_ref, g_ref, dt_ref):
        xf = x_ref[...]
        h = (xf * _rstd(xf) * w0_ref[...]).astype(BF)
        h_ref[...] = h
        for ref, lo, hi in ((z_ref, P_Z, P_XBC), (xbc_ref, P_XBC, P_Q), (q_ref, P_Q, P_K), (k_ref, P_K, P_V),
                            (v_ref, P_V, P_G), (g_ref, P_G, P_DT), (dt_ref, P_DT, P_END)):
            ref[...] = jnp.dot(h, wp_ref[:, lo:hi], preferred_element_type=F32).astype(ref.dtype)

    widths = (D_MODEL, SSD_WIDTH, SSD_XBC, RET_QK_W, RET_QK_W, RET_V_W, RET_V_W)
    return pl.pallas_call(
        body, name="fwd_in", grid=(T // tm,),
        in_specs=[_row_spec(tm, D_MODEL), _const_spec((1, D_MODEL)), _VMEM_WHOLE],
        out_specs=[_row_spec(tm, w) for w in widths] + [_row_spec(tm, LANES)],
        out_shape=[S((T, w), BF) for w in widths] + [S((T, LANES), F32)],
        compiler_params=_cparams("parallel"),
    )(x, w0, wp)


def _fwd_mid(ys, yr, x, wout, wpm, wpf, wup, tm=256):
    T = x.shape[0]

    def body(ys_ref, yr_ref, x_ref, wout_ref, wpm_ref, wpf_ref, wup_ref, y_ref, x1_ref, h2_ref, graw_ref, val_ref):
        y = (jnp.dot(ys_ref[...], wout_ref[0:SSD_WIDTH, :], preferred_element_type=F32)
             + jnp.dot(yr_ref[...], wout_ref[SSD_WIDTH:, :], preferred_element_type=F32))
        y_ref[...] = y
        x1 = x_ref[...] + y * _rstd(y) * wpm_ref[...]
        x1_ref[...] = x1
        h2 = (x1 * _rstd(x1) * wpf_ref[...]).astype(BF)
        h2_ref[...] = h2
        graw_ref[...] = jnp.dot(h2, wup_ref[:, 0:D_FF], preferred_element_type=F32).astype(BF)
        val_ref[...] = jnp.dot(h2, wup_ref[:, D_FF:], preferred_element_type=F32).astype(BF)

    return pl.pallas_call(
        body, name="fwd_mid", grid=(T // tm,),
        in_specs=[_row_spec(tm, SSD_WIDTH), _row_spec(tm, RET_V_W), _row_spec(tm, D_MODEL), _VMEM_WHOLE,
                  _const_spec((1, D_MODEL)), _const_spec((1, D_MODEL)), _VMEM_WHOLE],
        out_specs=[_row_spec(tm, D_MODEL), _row_spec(tm, D_MODEL), _row_spec(tm, D_MODEL), _row_spec(tm, D_FF),
                   _row_spec(tm, D_FF)],
        out_shape=[S((T, D_MODEL), F32), S((T, D_MODEL), F32), S((T, D_MODEL), BF), S((T, D_FF), BF), S((T, D_FF), BF)],
        compiler_params=_cparams("parallel"),
    )(ys, yr, x, wout, wpm, wpf, wup)


def _ffn_tail(graw, val, x1, tgt, convw, convb, wdown, wpff, tm=256):
    T = x1.shape[0]

    def body(graw_ref, val_ref, x1_ref, tgt_ref, cw_ref, cb_ref, wd_ref, wpff_ref,
             a_ref, df_ref, dval_ref, dgate_ref, dx2_ref, loss_ref, dwpff_ref, dcb_ref, carry):
        i = pl.program_id(0)

        @pl.when(i == 0)
        def _():
            carry[...] = jnp.zeros_like(carry)
            loss_ref[...] = jnp.zeros_like(loss_ref)
            dwpff_ref[...] = jnp.zeros_like(dwpff_ref)
            dcb_ref[...] = jnp.zeros_like(dcb_ref)

        g = graw_ref[...].astype(F32)
        ext = jnp.concatenate([carry[...], g], axis=0)
        carry[...] = g[tm - 8:tm]
        gate = cb_ref[...] + sum(cw_ref[j:j + 1, :] * _rows_before(ext, FFN_CONV - 1 - j, 8, tm) for j in range(FFN_CONV))
        sg = _sigmoid(gate)
        silu = gate * sg
        v = val_ref[...].astype(F32)
        a = (silu * v).astype(BF)
        a_ref[...] = a
        f = jnp.dot(a, wd_ref[...], preferred_element_type=F32)
        r = _rstd(f)
        w = wpff_ref[...]
        e = x1_ref[...] + f * r * w - tgt_ref[...]
        loss_ref[...] += jnp.sum(e * e) * (0.5 / D_MODEL)
        dx2 = e * (1.0 / D_MODEL)
        dx2_ref[...] = dx2
        df, dw = _rms_bwd(dx2, f, r, w)
        dwpff_ref[...] += dw
        dfb = df.astype(BF)
        df_ref[...] = dfb
        da = _dot_nt(dfb, wd_ref[...])
        dval_ref[...] = (da * silu).astype(BF)
        dgate = da * v * _dsilu(gate, sg)
        dcb_ref[...] += jnp.sum(dgate, axis=0, keepdims=True)
        dgate_ref[...] = dgate.astype(BF)

    return pl.pallas_call(
        body, name="ffn_tail", grid=(T // tm,),
        in_specs=[_row_spec(tm, D_FF), _row_spec(tm, D_FF), _row_spec(tm, D_MODEL), _row_spec(tm, D_MODEL),
                  _const_spec((8, D_FF)), _const_spec((1, D_FF)), _VMEM_WHOLE, _const_spec((1, D_MODEL))],
        out_specs=[_row_spec(tm, D_FF), _row_spec(tm, D_MODEL), _row_spec(tm, D_FF), _row_spec(tm, D_FF),
                   _row_spec(tm, D_MODEL), _const_spec((8, LANES)), _const_spec((1, D_MODEL)), _const_spec((1, D_FF))],
        out_shape=[S((T, D_FF), BF), S((T, D_MODEL), BF), S((T, D_FF), BF), S((T, D_FF), BF), S((T, D_MODEL), F32),
                   S((8, LANES), F32), S((1, D_MODEL), F32), S((1, D_FF), F32)],
        scratch_shapes=[pltpu.VMEM((8, D_FF), F32)],
        compiler_params=_cparams("arbitrary"),
    )(graw, val, x1, tgt, convw, convb, wdown, wpff)


def _ffn_bwd(dgate, dval, graw, x1, dx2, y, convw, wup, wpf, wpm, wout, tm=256):
    T = x1.shape[0]
    nt = T // tm
    rev = lambda i: (nt - 1 - i, 0)
    rspec = lambda w: pl.BlockSpec((tm, w), rev)

    def body(dgate_ref, dval_ref, graw_ref, x1_ref, dx2_ref, y_ref, cw_ref, wup_ref, wpf_ref, wpm_ref, wout_ref,
             dgraw_ref, dx1_ref, dy_ref, dys_ref, dyr_ref, dcw_ref, dwpf_ref, dwpm_ref, carry):
        i = pl.program_id(0)

        @pl.when(i == 0)
        def _():
            carry[...] = jnp.zeros_like(carry)
            dcw_ref[...] = jnp.zeros_like(dcw_ref)
            dwpf_ref[...] = jnp.zeros_like(dwpf_ref)
            dwpm_ref[...] = jnp.zeros_like(dwpm_ref)

        dg = dgate_ref[...].astype(F32)
        ext = jnp.concatenate([dg, carry[...]], axis=0)
        carry[...] = dg[0:8]
        g = graw_ref[...].astype(F32)
        dgraw = jnp.zeros((tm, D_FF), F32)
        for j in range(FFN_CONV):
            sj = _rows_after(ext, FFN_CONV - 1 - j, tm)
            dgraw = dgraw + cw_ref[j:j + 1, :] * sj
            dcw_ref[j:j + 1, :] += jnp.sum(sj * g, axis=0, keepdims=True)
        dgrawb = dgraw.astype(BF)
        dgraw_ref[...] = dgrawb
        dh2 = _dot_nt(dgrawb, wup_ref[:, 0:D_FF]) + _dot_nt(dval_ref[...], wup_ref[:, D_FF:])
        x1 = x1_ref[...]
        dxa, dw = _rms_bwd(dh2, x1, _rstd(x1), wpf_ref[...])
        dwpf_ref[...] += dw
        dx1 = dx2_ref[...] + dxa
        dx1_ref[...] = dx1
        yv = y_ref[...]
        dy, dw = _rms_bwd(dx1, yv, _rstd(yv), wpm_ref[...])
        dwpm_ref[...] += dw
        dyb = dy.astype(BF)
        dy_ref[...] = dyb
        dys_ref[...] = _dot_nt(dyb, wout_ref[0:SSD_WIDTH, :]).astype(BF)
        dyr_ref[...] = _dot_nt(dyb, wout_ref[SSD_WIDTH:, :]).astype(BF)

    return pl.pallas_call(
        body, name="ffn_bwd", grid=(nt,),
        in_specs=[rspec(D_FF), rspec(D_FF), rspec(D_FF), rspec(D_MODEL), rspec(D_MODEL), rspec(D_MODEL),
                  _const_spec((8, D_FF)), _VMEM_WHOLE, _const_spec((1, D_MODEL)), _const_spec((1, D_MODEL)), _VMEM_WHOLE],
        out_specs=[rspec(D_FF), rspec(D_MODEL), rspec(D_MODEL), rspec(SSD_WIDTH), rspec(RET_V_W),
                   _const_spec((8, D_FF)), _const_spec((1, D_MODEL)), _const_spec((1, D_MODEL))],
        out_shape=[S((T, D_FF), BF), S((T, D_MODEL), F32), S((T, D_MODEL), BF), S((T, SSD_WIDTH), BF), S((T, RET_V_W), BF),
                   S((8, D_FF), F32), S((1, D_MODEL), F32), S((1, D_MODEL), F32)],
        scratch_shapes=[pltpu.VMEM((8, D_FF), F32)],
        compiler_params=_cparams("arbitrary"),
    )(dgate, dval, graw, x1, dx2, y, convw, wup, wpf, wpm, wout)


def _in_bwd(dz, dxbc, dq, dk, dv, dg, ddt, x, dx1, w0, wp, tm=256):
    T = x.shape[0]

    def body(dz_ref, dxbc_ref, dq_ref, dk_ref, dv_ref, dg_ref, ddt_ref, x_ref, dx1_ref, w0_ref, wp_ref, gx_ref, dw0_ref):
        @pl.when(pl.program_id(0) == 0)
        def _():
            dw0_ref[...] = jnp.zeros_like(dw0_ref)

        dh = jnp.zeros((tm, D_MODEL), F32)
        for ref, lo, hi in ((dz_ref, P_Z, P_XBC), (dxbc_ref, P_XBC, P_Q), (dq_ref, P_Q, P_K), (dk_ref, P_K, P_V),
                            (dv_ref, P_V, P_G), (dg_ref, P_G, P_DT), (ddt_ref, P_DT, P_END)):
            dh = dh + _dot_nt(ref[...], wp_ref[:, lo:hi])
        xf = x_ref[...]
        dx, dw = _rms_bwd(dh, xf, _rstd(xf), w0_ref[...])
        dw0_ref[...] += dw
        gx_ref[...] = dx1_ref[...] + dx

    widths = (SSD_WIDTH, SSD_XBC, RET_QK_W, RET_QK_W, RET_V_W, RET_V_W, LANES)
    return pl.pallas_call(
        body, name="in_bwd", grid=(T // tm,),
        in_specs=[_row_spec(tm, w) for w in widths] + [_row_spec(tm, D_MODEL), _row_spec(tm, D_MODEL),
                                                       _const_spec((1, D_MODEL)), _VMEM_WHOLE],
        out_specs=[_row_spec(tm, D_MODEL), _const_spec((1, D_MODEL))],
        out_shape=[S((T, D_MODEL), F32), S((1, D_MODEL), F32)],
        compiler_params=_cparams("arbitrary"),
    )(dz, dxbc, dq, dk, dv, dg, ddt, x, dx1, w0, wp)


def _matmul_tn(a, b, name, tk=512):
    T, M = a.shape
    N = b.shape[1]
    tn = N
    while M * tn * 4 > (4 << 20) and tn % 256 == 0:
        tn //= 2
    nk = T // tk

    def body(a_ref, b_ref, o_ref):
        @pl.when(pl.program_id(1) == 0)
        def _():
            o_ref[...] = jnp.zeros_like(o_ref)

        o_ref[...] += _dot_tn(a_ref[...], b_ref[...])

    return pl.pallas_call(
        body, name=name, grid=(N // tn, nk),
        in_specs=[pl.BlockSpec((tk, M), lambda n, k: (k, 0)), pl.BlockSpec((tk, tn), lambda n, k: (k, n))],
        out_specs=pl.BlockSpec((M, tn), lambda n, k: (0, n)),
        out_shape=S((M, N), F32),
        compiler_params=_cparams("parallel", "arbitrary"),
    )(a, b)


def _tri(lower):
    r = lax.broadcasted_iota(jnp.int32, (CH, CH), 0)
    c = lax.broadcasted_iota(jnp.int32, (CH, CH), 1)
    return ((c <= r) if lower else (r <= c)).astype(F32)


def _onehot_row(h):
    return (lax.broadcasted_iota(jnp.int32, (1, LANES), 1) == h).astype(F32)


def _onehot_col(h):
    return (lax.broadcasted_iota(jnp.int32, (LANES, 1), 0) == h).astype(F32)


def _ssd_pre(xc_ref, xh_ref, dtr_ref, cw_ref, cb_ref, dtb_ref, alog_ref, first):
    xc = xc_ref[...].astype(F32)
    xh = jnp.where(first, 0.0, xh_ref[...].astype(F32))
    ext = jnp.concatenate([xh, xc], axis=0)
    u = cb_ref[...] + sum(cw_ref[j:j + 1, :] * _rows_before(ext, SSD_CONV - 1 - j, HALO, CH) for j in range(SSD_CONV))
    sg = _sigmoid(u)
    act = u * sg
    dt = _softplus(dtr_ref[...] + dtb_ref[...])
    a = -jnp.exp(alog_ref[...])
    da = dt * a
    cs = _dot_hi(_tri(True), da)
    cst = _dot_tn_hi(da, _tri(False))
    return xc, u, sg, act, dt, a, cs, cst


HPG = SSD_HEADS // SSD_GROUPS
GW = HPG * SSD_HEAD_DIM


def _expand_heads(src, buf):
    for h in range(SSD_HEADS):
        buf[:, h * SSD_HEAD_DIM:(h + 1) * SSD_HEAD_DIM] = jnp.broadcast_to(src[:, h:h + 1], (CH, SSD_HEAD_DIM))


def _ssd_expanded(act, dt, cs, dtx, csx):
    _expand_heads(dt, dtx)
    _expand_heads(cs, csx)
    csv = csx[...]
    last = csv[CH - 1:CH, :]
    e_exp = jnp.exp(csv)
    dec_exp = jnp.exp(last - csv)
    el_exp = jnp.exp(last)
    xs = act[:, 0:SSD_WIDTH]
    xdt = xs * dtx[...]
    return xs, xdt, xdt * dec_exp, e_exp, dec_exp, el_exp


def _decay_mats(h, cs, cst, transposed):
    r = lax.broadcasted_iota(jnp.int32, (CH, CH), 0)
    c = lax.broadcasted_iota(jnp.int32, (CH, CH), 1)
    c_col = cs[:, h:h + 1]
    c_row = cst[h:h + 1, :]
    if transposed:
        return jnp.exp(jnp.where(r <= c, c_row - c_col, -1e30))
    return jnp.exp(jnp.where(r >= c, c_col - c_row, -1e30))


def _ssd_specs(T):
    nc = T // CH
    return nc, [
        _row_spec(CH, SSD_XBC),
        pl.BlockSpec((HALO, SSD_XBC), lambda i: (jnp.maximum(i * (CH // HALO) - 1, 0), 0)),
        _row_spec(CH, LANES),
        _row_spec(CH, SSD_WIDTH),
    ]


def _groups(act):
    bm = [act[:, SSD_WIDTH + g * SSD_STATE:SSD_WIDTH + (g + 1) * SSD_STATE] for g in range(SSD_GROUPS)]
    o = SSD_WIDTH + SSD_GROUPS * SSD_STATE
    cm = [act[:, o + g * SSD_STATE:o + (g + 1) * SSD_STATE] for g in range(SSD_GROUPS)]
    return bm, cm


def _ssd_fwd(xbc, dtr, z, convw, convb, dtb, alog, dskx, nw):
    T = xbc.shape[0]
    nc, specs = _ssd_specs(T)

    def body(xc_ref, xh_ref, dtr_ref, z_ref, cw_ref, cb_ref, dtb_ref, alog_ref, dskx_ref, nw_ref,
             out_ref, y_ref, st_ref, state, ybuf, dtx, csx):
        i = pl.program_id(0)

        @pl.when(i == 0)
        def _():
            state[...] = jnp.zeros_like(state)

        xc, u, sg, act, dt, a, cs, cst = _ssd_pre(xc_ref, xh_ref, dtr_ref, cw_ref, cb_ref, dtb_ref, alog_ref, i == 0)
        xs, xdt, w, e_exp, dec_exp, el_exp = _ssd_expanded(act, dt, cs, dtx, csx)
        bm, cm = _groups(act)
        for g in range(SSD_GROUPS):
            gs = slice(g * GW, (g + 1) * GW)
            st = state[g]
            st_ref[0, g] = st
            cb = _dot_nt(cm[g], bm[g])
            ybuf[:, gs] = _dot(cm[g], st) * e_exp[:, gs] + xs[:, gs] * dskx_ref[:, gs]
            state[g] = st * el_exp[:, gs] + _dot_tn(bm[g], w[:, gs])
            for h in range(g * HPG, (g + 1) * HPG):
                sl = slice(h * SSD_HEAD_DIM, (h + 1) * SSD_HEAD_DIM)
                ybuf[:, sl] += _dot(cb * _decay_mats(h, cs, cst, False), xdt[:, sl])
        yv = ybuf[...]
        y_ref[...] = yv.astype(BF)
        zf = z_ref[...].astype(F32)
        gated = yv * (zf * _sigmoid(zf))
        out_ref[...] = (gated * _rstd(gated) * nw_ref[...]).astype(BF)

    st_spec = pl.BlockSpec((1, SSD_GROUPS, SSD_STATE, GW), lambda i: (i, 0, 0, 0))
    return pl.pallas_call(
        body, name="ssd_fwd", grid=(nc,),
        in_specs=specs + [_const_spec((8, SSD_XBC)), _const_spec((1, SSD_XBC)), _const_spec((1, LANES)),
                          _const_spec((1, LANES)), _const_spec((1, SSD_WIDTH)), _const_spec((1, SSD_WIDTH))],
        out_specs=[_row_spec(CH, SSD_WIDTH), _row_spec(CH, SSD_WIDTH), st_spec],
        out_shape=[S((T, SSD_WIDTH), BF), S((T, SSD_WIDTH), BF), S((nc, SSD_GROUPS, SSD_STATE, GW), F32)],
        scratch_shapes=[pltpu.VMEM((SSD_GROUPS, SSD_STATE, GW), F32), pltpu.VMEM((CH, SSD_WIDTH), F32),
                        pltpu.VMEM((CH, SSD_WIDTH), F32), pltpu.VMEM((CH, SSD_WIDTH), F32)],
        compiler_params=_cparams("arbitrary"),
    )(xbc, xbc, dtr, z, convw, convb, dtb, alog, dskx, nw)


def _ssd_bwd(dout, y, xbc, dtr, z, states, convw, convb, dtb, alog, dskx, nw):
    T = xbc.shape[0]
    nc = T // CH
    rev = lambda i: (nc - 1 - i, 0)
    rspec = lambda w: pl.BlockSpec((CH, w), rev)
    halo_spec = pl.BlockSpec((HALO, SSD_XBC), lambda i: (jnp.maximum((nc - 1 - i) * (CH // HALO) - 1, 0), 0))
    NB = SSD_WIDTH
    NC_ = SSD_WIDTH + SSD_GROUPS * SSD_STATE

    def body(do_ref, y_ref, xc_ref, xh_ref, dtr_ref, z_ref, st_ref, cw_ref, cb_ref, dtb_ref, alog_ref, dskx_ref, nw_ref,
             dz_ref, dxbc_ref, ddt_ref, dcw_ref, dcb_ref, ddtb_ref, dalog_ref, ddsk_ref, dnw_ref,
             dstate, ducarry, dtx, csx, dxdtbuf, dact):
        i = pl.program_id(0)

        @pl.when(i == 0)
        def _():
            dstate[...] = jnp.zeros_like(dstate)
            ducarry[...] = jnp.zeros_like(ducarry)
            for ref in (dcw_ref, dcb_ref, ddtb_ref, dalog_ref, ddsk_ref, dnw_ref):
                ref[...] = jnp.zeros_like(ref)

        xc, u, sg, act, dt, a, cs, cst = _ssd_pre(xc_ref, xh_ref, dtr_ref, cw_ref, cb_ref, dtb_ref, alog_ref, i == nc - 1)
        xs, xdt, w, e_exp, dec_exp, el_exp = _ssd_expanded(act, dt, cs, dtx, csx)
        bm, cm = _groups(act)
        yv = y_ref[...].astype(F32)
        zf = z_ref[...].astype(F32)
        sz = _sigmoid(zf)
        gated = yv * (zf * sz)
        dgated, dnw = _rms_bwd(do_ref[...].astype(F32), gated, _rstd(gated), nw_ref[...])
        dnw_ref[...] += dnw
        dz_ref[...] = (dgated * yv * _dsilu(zf, sz)).astype(BF)
        dy = dgated * (zf * sz)
        lane_of = lax.broadcasted_iota(jnp.int32, (SSD_WIDTH, LANES), 0) - SSD_HEAD_DIM * lax.broadcasted_iota(jnp.int32, (SSD_WIDTH, LANES), 1)
        expt = ((lane_of >= 0) & (lane_of < SSD_HEAD_DIM)).astype(F32)
        ddsk_ref[...] += _dot_hi(jnp.sum(dy * xs, axis=0, keepdims=True), expt)
        dcs = jnp.zeros((CH, LANES), F32)
        dcst = jnp.zeros((LANES, CH), F32)
        ddt = jnp.zeros((CH, LANES), F32)
        lastrows = []
        for g in range(SSD_GROUPS):
            gs = slice(g * GW, (g + 1) * GW)
            st = st_ref[0, g]
            dsn = dstate[g]
            cbm = _dot_nt(cm[g], bm[g])
            cbt = _dot_nt(bm[g], cm[g])
            dy_g = dy[:, gs]
            yoff = _dot(cm[g], st) * e_exp[:, gs]
            dq = dy_g * e_exp[:, gs]
            dcm_g = _dot_nt(dq, st)
            dstate[g] = _dot_tn(cm[g], dq) + dsn * el_exp[:, gs]
            dw = _dot(bm[g], dsn)
            w_g = w[:, gs]
            dbm_g = _dot_nt(w_g, dsn)
            dww = dw * w_g
            red = dy_g * yoff - dww
            lastrows.append(jnp.sum(dsn * st, axis=0, keepdims=True) * el_exp[:, gs] + jnp.sum(dww, axis=0, keepdims=True))
            dxdtbuf[:, gs] = dw * dec_exp[:, gs]
            dcb = jnp.zeros((CH, CH), F32)
            for h in range(g * HPG, (g + 1) * HPG):
                sl = slice(h * SSD_HEAD_DIM, (h + 1) * SSD_HEAD_DIM)
                rl = slice((h - g * HPG) * SSD_HEAD_DIM, (h - g * HPG + 1) * SSD_HEAD_DIM)
                oh = _onehot_row(h)
                lmat = _decay_mats(h, cs, cst, False)
                mmat = cbm * lmat
                dy_h = dy[:, sl]
                dm = _dot_nt(dy_h, xdt[:, sl])
                dxdt_h = dxdtbuf[:, sl] + _dot(cbt * _decay_mats(h, cs, cst, True), dy_h)
                dxdtbuf[:, sl] = dxdt_h
                dseg = dm * mmat
                dcb = dcb + dm * lmat
                col = jnp.sum(dseg, axis=1, keepdims=True) + jnp.sum(red[:, rl], axis=1, keepdims=True)
                dcs = dcs + col * oh
                dcst = dcst - _onehot_col(h) * jnp.sum(dseg, axis=0, keepdims=True)
                ddt = ddt + jnp.sum(dxdt_h * xs[:, sl], axis=1, keepdims=True) * oh
            dact[:, NB + g * SSD_STATE:NB + (g + 1) * SSD_STATE] = dbm_g + _dot_tn(dcb, cm[g])
            dact[:, NC_ + g * SSD_STATE:NC_ + (g + 1) * SSD_STATE] = dcm_g + _dot(dcb, bm[g])
        dact[:, 0:SSD_WIDTH] = dy * dskx_ref[...] + dxdtbuf[...] * dtx[...]
        dlast = _dot_hi(jnp.concatenate(lastrows, axis=1), expt)
        rows = lax.broadcasted_iota(jnp.int32, (CH, LANES), 0)
        dcs = dcs + _dot_tn_hi(dcst, jnp.eye(LANES, dtype=F32)) + jnp.where(rows == CH - 1, dlast, 0.0)
        dda = _dot_hi(_tri(False), dcs)
        dalog_ref[...] += jnp.sum(dda * dt, axis=0, keepdims=True) * a
        ddt = ddt + dda * a
        ddtr = ddt * _sigmoid(dtr_ref[...] + dtb_ref[...])
        ddtb_ref[...] += jnp.sum(ddtr, axis=0, keepdims=True)
        ddt_ref[...] = ddtr.astype(BF)
        du = dact[...] * _dsilu(u, sg)
        dcb_ref[...] += jnp.sum(du, axis=0, keepdims=True)
        ext = jnp.concatenate([du, ducarry[...]], axis=0)
        ducarry[...] = du[0:8]
        dx = jnp.zeros((CH, SSD_XBC), F32)
        for j in range(SSD_CONV):
            sj = _rows_after(ext, SSD_CONV - 1 - j, CH)
            dx = dx + cw_ref[j:j + 1, :] * sj
            dcw_ref[j:j + 1, :] += jnp.sum(sj * xc, axis=0, keepdims=True)
        dxbc_ref[...] = dx.astype(BF)

    return pl.pallas_call(
        body, name="ssd_bwd", grid=(nc,),
        in_specs=[rspec(SSD_WIDTH), rspec(SSD_WIDTH), rspec(SSD_XBC), halo_spec, rspec(LANES), rspec(SSD_WIDTH),
                  pl.BlockSpec((1, SSD_GROUPS, SSD_STATE, GW), lambda i: (nc - 1 - i, 0, 0, 0)),
                  _const_spec((8, SSD_XBC)), _const_spec((1, SSD_XBC)), _const_spec((1, LANES)),
                  _const_spec((1, LANES)), _const_spec((1, SSD_WIDTH)), _const_spec((1, SSD_WIDTH))],
        out_specs=[rspec(SSD_WIDTH), rspec(SSD_XBC), rspec(LANES),
                   _const_spec((8, SSD_XBC)), _const_spec((1, SSD_XBC)), _const_spec((1, LANES)),
                   _const_spec((1, LANES)), _const_spec((1, LANES)), _const_spec((1, SSD_WIDTH))],
        out_shape=[S((T, SSD_WIDTH), BF), S((T, SSD_XBC), BF), S((T, LANES), BF),
                   S((8, SSD_XBC), F32), S((1, SSD_XBC), F32), S((1, LANES), F32),
                   S((1, LANES), F32), S((1, LANES), F32), S((1, SSD_WIDTH), F32)],
        scratch_shapes=[pltpu.VMEM((SSD_GROUPS, SSD_STATE, GW), F32), pltpu.VMEM((8, SSD_XBC), F32),
                        pltpu.VMEM((CH, SSD_WIDTH), F32), pltpu.VMEM((CH, SSD_WIDTH), F32),
                        pltpu.VMEM((CH, SSD_WIDTH), F32), pltpu.VMEM((CH, SSD_XBC), F32)],
        compiler_params=_cparams("arbitrary"),
    )(dout, y, xbc, xbc, dtr, z, states, convw, convb, dtb, alog, dskx, nw)


def _log_gamma(h):
    return float(np.log1p(-np.exp2(np.float32(-5.0 - h)), dtype=np.float32))


def _swap_halves(t):
    n = t.shape[1]
    lane = lax.broadcasted_iota(jnp.int32, t.shape, 1)
    return jnp.where((lane & (RET_QK - 1)) < RET_QK // 2, pltpu.roll(t, n - RET_QK // 2, 1), pltpu.roll(t, RET_QK // 2, 1))


def _rot(t, cos, sin):
    return t * cos + _swap_halves(t) * sin


def _rot_t(d, cos, sin):
    return d * cos + _swap_halves(d * sin)


def _ret_consts(h):
    lg = _log_gamma(h)
    r = lax.broadcasted_iota(jnp.int32, (CH, CH), 0)
    c = lax.broadcasted_iota(jnp.int32, (CH, CH), 1)
    rel = (r - c).astype(F32)
    dmask = jnp.where(rel >= 0, jnp.exp(lg * jnp.maximum(rel, 0.0)), 0.0)
    dmask_t = jnp.where(rel <= 0, jnp.exp(lg * jnp.maximum(-rel, 0.0)), 0.0)
    pos = lax.broadcasted_iota(jnp.int32, (CH, 1), 0).astype(F32)
    kdec = jnp.exp(lg * (CH - 1.0 - pos))
    qdec = jnp.exp(lg * (pos + 1.0))
    pos_row = lax.broadcasted_iota(jnp.int32, (1, CH), 1).astype(F32)
    kdec_row = jnp.exp(lg * (CH - 1.0 - pos_row))
    qdec_row = jnp.exp(lg * (pos_row + 1.0))
    return dmask, dmask_t, kdec, qdec, kdec_row, qdec_row, math.exp(lg * CH)


def _ret_fwd(q, k, v, g, cos, sin, nw):
    T = q.shape[0]
    nc = T // CH

    def body(q_ref, k_ref, v_ref, g_ref, cos_ref, sin_ref, nw_ref, out_ref, st_ref, state):
        i = pl.program_id(0)

        @pl.when(i == 0)
        def _():
            state[...] = jnp.zeros_like(state)

        cosf = jnp.tile(cos_ref[...], (1, RET_QK_W // LANES))
        sinf = jnp.tile(sin_ref[...], (1, RET_QK_W // LANES))
        qr = _rot(q_ref[...].astype(F32), cosf, sinf)
        kr = _rot(k_ref[...].astype(F32), cosf, sinf) * (RET_QK ** -0.5)
        krt = kr.T
        st_ref[0] = state[...]
        o_all = []
        for h in range(RET_HEADS):
            dmask, dmask_t, kdec, qdec, kdec_row, qdec_row, gam = _ret_consts(h)
            qs = slice(h * RET_QK, (h + 1) * RET_QK)
            v_h = v_ref[:, h * RET_V:(h + 1) * RET_V]
            rprev = state[h]
            scores = _dot_nt(qr[:, qs], kr[:, qs]) * dmask
            o_all.append(_dot(scores, v_h) + _dot(qr[:, qs] * qdec, rprev))
            state[h] = rprev * gam + _dot(krt[qs, :] * kdec_row, v_h)
        for h in range(RET_HEADS):
            sl = slice(h * RET_V, (h + 1) * RET_V)
            o = o_all[h]
            gf = g_ref[:, sl].astype(F32)
            out_ref[:, sl] = (o * _rstd(o) * nw_ref[:, sl] * (gf * _sigmoid(gf))).astype(BF)

    return pl.pallas_call(
        body, name="ret_fwd", grid=(nc,),
        in_specs=[_row_spec(CH, RET_QK_W), _row_spec(CH, RET_QK_W), _row_spec(CH, RET_V_W), _row_spec(CH, RET_V_W),
                  _row_spec(CH, LANES), _row_spec(CH, LANES), _const_spec((1, RET_V_W))],
        out_specs=[_row_spec(CH, RET_V_W), pl.BlockSpec((1, RET_HEADS, RET_QK, RET_V), lambda i: (i, 0, 0, 0))],
        out_shape=[S((T, RET_V_W), BF), S((nc, RET_HEADS, RET_QK, RET_V), F32)],
        scratch_shapes=[pltpu.VMEM((RET_HEADS, RET_QK, RET_V), F32)],
        compiler_params=_cparams("arbitrary"),
    )(q, k, v, g, cos, sin, nw)


def _ret_bwd(dout, q, k, v, g, states, cos, sin, nw):
    T = q.shape[0]
    nc = T // CH
    rev = lambda i: (nc - 1 - i, 0)
    rspec = lambda w: pl.BlockSpec((CH, w), rev)

    def body(do_ref, q_ref, k_ref, v_ref, g_ref, st_ref, cos_ref, sin_ref, nw_ref,
             dq_ref, dk_ref, dv_ref, dg_ref, dnw_ref, dstate, dqbuf, dkbuf):
        i = pl.program_id(0)

        @pl.when(i == 0)
        def _():
            dstate[...] = jnp.zeros_like(dstate)
            dnw_ref[...] = jnp.zeros_like(dnw_ref)

        cosf = jnp.tile(cos_ref[...], (1, RET_QK_W // LANES))
        sinf = jnp.tile(sin_ref[...], (1, RET_QK_W // LANES))
        qr = _rot(q_ref[...].astype(F32), cosf, sinf)
        kr = _rot(k_ref[...].astype(F32), cosf, sinf) * (RET_QK ** -0.5)
        qrt = qr.T
        heads = range(RET_HEADS)
        qsl = [slice(h * RET_QK, (h + 1) * RET_QK) for h in heads]
        vsl = [slice(h * RET_V, (h + 1) * RET_V) for h in heads]
        scores_t, o_all, do_all = [], [], []
        for h in heads:
            dmask, dmask_t, kdec, qdec, kdec_row, qdec_row, gam = _ret_consts(h)
            q_h, k_h = qr[:, qsl[h]], kr[:, qsl[h]]
            scores = _dot_nt(q_h, k_h) * dmask
            scores_t.append(_dot_nt(k_h, q_h) * dmask_t)
            o_all.append(_dot(scores, v_ref[:, vsl[h]]) + _dot(q_h * qdec, st_ref[0, h]))
        for h in heads:
            o = o_all[h]
            rr = _rstd(o)
            of = o * rr
            gf = g_ref[:, vsl[h]].astype(F32)
            sgg = _sigmoid(gf)
            d_h = do_ref[:, vsl[h]].astype(F32)
            nw_h = nw_ref[:, vsl[h]]
            dg_ref[:, vsl[h]] = (d_h * of * nw_h * _dsilu(gf, sgg)).astype(BF)
            dt_ = d_h * (gf * sgg)
            dnw_ref[:, vsl[h]] += jnp.sum(dt_ * of, axis=0, keepdims=True)
            dof = dt_ * nw_h
            do_all.append(rr * dof - o * (rr * rr * rr) * jnp.mean(dof * o, axis=-1, keepdims=True))
        for h in heads:
            dmask, dmask_t, kdec, qdec, kdec_row, qdec_row, gam = _ret_consts(h)
            q_h, k_h, v_h, do = qr[:, qsl[h]], kr[:, qsl[h]], v_ref[:, vsl[h]], do_all[h]
            gnext = dstate[h]
            dsc = _dot_nt(do, v_h) * dmask
            dsc_t = _dot_nt(v_h, do) * dmask_t
            dv_ref[:, vsl[h]] = (_dot(scores_t[h], do) + _dot(k_h * kdec, gnext)).astype(BF)
            dqbuf[:, qsl[h]] = _dot(dsc, k_h) + _dot_nt(do, st_ref[0, h]) * qdec
            dkbuf[:, qsl[h]] = _dot(dsc_t, q_h) + _dot_nt(v_h, gnext) * kdec
            dstate[h] = gnext * gam + _dot(qrt[qsl[h], :] * qdec_row, do)
        dq_ref[...] = _rot_t(dqbuf[...], cosf, sinf).astype(BF)
        dk_ref[...] = (_rot_t(dkbuf[...], cosf, sinf) * (RET_QK ** -0.5)).astype(BF)

    return pl.pallas_call(
        body, name="ret_bwd", grid=(nc,),
        in_specs=[rspec(RET_V_W), rspec(RET_QK_W), rspec(RET_QK_W), rspec(RET_V_W), rspec(RET_V_W),
                  pl.BlockSpec((1, RET_HEADS, RET_QK, RET_V), lambda i: (nc - 1 - i, 0, 0, 0)),
                  rspec(LANES), rspec(LANES), _const_spec((1, RET_V_W))],
        out_specs=[rspec(RET_QK_W), rspec(RET_QK_W), rspec(RET_V_W), rspec(RET_V_W), _const_spec((1, RET_V_W))],
        out_shape=[S((T, RET_QK_W), BF), S((T, RET_QK_W), BF), S((T, RET_V_W), BF), S((T, RET_V_W), BF),
                   S((1, RET_V_W), F32)],
        scratch_shapes=[pltpu.VMEM((RET_HEADS, RET_QK, RET_V), F32), pltpu.VMEM((CH, RET_QK_W), F32),
                        pltpu.VMEM((CH, RET_QK_W), F32)],
        compiler_params=_cparams("arbitrary"),
    )(dout, q, k, v, g, states, cos, sin, nw)


def _exchange(bufs, name, same):
    nb = len(bufs)
    slabs = [tuple(b.shape if same else b.shape[1:]) for b in bufs]

    def body(*refs):
        buf_refs, out_refs = refs[:nb], refs[nb:2 * nb]
        send_sems, recv_sems, local_sems = refs[2 * nb:]
        x, y, c = lax.axis_index("x"), lax.axis_index("y"), lax.axis_index("c")
        me = 4 * x + 2 * y + c

        def src(b, d):
            return buf_refs[b] if same else buf_refs[b].at[d]

        def remote(b, k, to_me):
            px = 1 - x if k & 4 else x
            py = 1 - y if k & 2 else y
            pc = 1 - c if k & 1 else c
            p = 4 * px + 2 * py + pc
            s = b * (N_DEV - 1) + k - 1
            return pltpu.make_async_remote_copy(
                src_ref=src(b, p), dst_ref=out_refs[b].at[me if to_me else p], send_sem=send_sems.at[s],
                recv_sem=recv_sems.at[s], device_id=(px, py, pc), device_id_type=pl.DeviceIdType.MESH)

        local = [pltpu.make_async_copy(src(b, me), out_refs[b].at[me], local_sems.at[b]) for b in range(nb)]
        for cp in local:
            cp.start()
        sends = [remote(b, k, True) for k in range(1, N_DEV) for b in range(nb)]
        for cp in sends:
            cp.start()
        for k in range(1, N_DEV):
            for b in range(nb):
                remote(b, k, False).wait_recv()
        for cp in sends:
            cp.wait_send()
        for cp in local:
            cp.wait()

    any_spec = pl.BlockSpec(memory_space=pl.ANY)
    return pl.pallas_call(
        body, name=name,
        in_specs=[any_spec] * nb, out_specs=[any_spec] * nb,
        out_shape=[S((N_DEV,) + s, b.dtype) for s, b in zip(slabs, bufs)],
        scratch_shapes=[pltpu.SemaphoreType.DMA((nb * (N_DEV - 1),)), pltpu.SemaphoreType.DMA((nb * (N_DEV - 1),)),
                        pltpu.SemaphoreType.DMA((nb,))],
    )(*bufs)


def _sum_slabs(recv, name):
    n, R, _ = recv.shape

    def body(r_ref, o_ref):
        g = r_ref[0].astype(F32)
        for s in range(1, n):
            g = g + r_ref[s].astype(F32)
        o_ref[...] = g

    return pl.pallas_call(body, name=name, out_shape=S((R, LANES), F32))(recv)


def _adamw(recv, w, m, v, name, tr):
    n, R, C = recv.shape
    c1 = 1.0 - ADAM_B1 ** ADAM_STEP
    c2 = 1.0 - ADAM_B2 ** ADAM_STEP

    def body(r_ref, w_ref, m_ref, v_ref, g_out, d_out, m_out, v_out):
        g = r_ref[0].astype(F32)
        for s in range(1, n):
            g = g + r_ref[s].astype(F32)
        mm = ADAM_B1 * m_ref[...] + (1.0 - ADAM_B1) * g
        vv = ADAM_B2 * v_ref[...] + (1.0 - ADAM_B2) * (g * g)
        g_out[...] = g
        m_out[...] = mm
        v_out[...] = vv
        d_out[...] = -ADAM_LR * ((mm / c1) / (jnp.sqrt(vv / c2) + ADAM_EPS) + ADAM_WD * w_ref[...])

    spec = pl.BlockSpec((tr, C), lambda i: (i, 0))
    return pl.pallas_call(
        body, name=name, grid=(R // tr,),
        in_specs=[pl.BlockSpec((n, tr, C), lambda i: (0, i, 0)), spec, spec, spec],
        out_specs=[spec] * 4, out_shape=[S((R, C), F32)] * 4,
        compiler_params=_cparams("parallel"),
    )(recv, w, m, v)


def _pack(parts, rows):
    cols = []
    for p in parts:
        f = p.reshape(-1)
        cols.append(jnp.pad(f, (0, (-f.shape[0]) % LANES)))
    flat = jnp.concatenate(cols)
    return jnp.pad(flat, (0, rows * LANES - flat.shape[0])).reshape(rows, LANES)


def _unpack(buf, shapes):
    flat = buf.reshape(-1)
    out, o = [], 0
    for shp in shapes:
        n = int(np.prod(shp))
        out.append(flat[o:o + n].reshape(shp))
        o += n + (-n) % LANES
    return out


SMALL_ROWS = 200
CONV_ROWS = 16


def kernel(x, pre_mix_norm_w, w_in, ssd_conv_w, ssd_conv_b, ssd_dt_bias, ssd_a_log, ssd_d, ssd_norm_w, ret_norm_w, w_out, post_mix_norm_w, pre_ffn_norm_w, w_up, ffn_conv_w, ffn_conv_b, w_down, post_ffn_norm_w, loss_target, m_pre_mix_norm_w, m_w_in, m_ssd_conv_w, m_ssd_conv_b, m_ssd_dt_bias, m_ssd_a_log, m_ssd_d, m_ssd_norm_w, m_ret_norm_w, m_w_out, m_post_mix_norm_w, m_pre_ffn_norm_w, m_w_up, m_ffn_conv_w, m_ffn_conv_b, m_w_down, m_post_ffn_norm_w, v_pre_mix_norm_w, v_w_in, v_ssd_conv_w, v_ssd_conv_b, v_ssd_dt_bias, v_ssd_a_log, v_ssd_d, v_ssd_norm_w, v_ret_norm_w, v_w_out, v_post_mix_norm_w, v_pre_ffn_norm_w, v_w_up, v_ffn_conv_w, v_ffn_conv_b, v_w_down, v_post_ffn_norm_w):
    T = x.shape[1]
    xi, tgt = x[0], loss_target[0]
    me = 4 * lax.axis_index("x") + 2 * lax.axis_index("y") + lax.axis_index("c")
    n_in, n_up = w_in.shape[2], w_up.shape[2]
    n_out, n_down = w_out.shape[1], w_down.shape[1]
    n_sc, n_fc = ssd_conv_w.shape[2], ffn_conv_w.shape[2]

    g_in, g_out, g_up, g_down = _exchange([w_in[0].astype(BF), w_out[0].astype(BF), w_up[0].astype(BF), w_down[0].astype(BF)],
                                          "gather_w", True)
    gconv, = _exchange([_pack([ssd_conv_w, ffn_conv_w], CONV_ROWS)], "gather_conv", True)
    win = jnp.transpose(g_in, (1, 0, 2)).reshape(D_MODEL, N_DEV * n_in)
    wout = g_out.reshape(N_DEV * n_out, D_MODEL)
    wup = jnp.transpose(g_up, (1, 0, 2)).reshape(D_MODEL, N_DEV * n_up)
    wdown = g_down.reshape(N_DEV * n_down, D_MODEL)
    convs = [_unpack(gconv[d], [(SSD_CONV, n_sc), (FFN_CONV, n_fc)]) for d in range(N_DEV)]
    scw = jnp.pad(jnp.concatenate([c[0] for c in convs], axis=1), ((0, 8 - SSD_CONV), (0, 0)))
    fcw = jnp.pad(jnp.concatenate([c[1] for c in convs], axis=1), ((0, 8 - FFN_CONV), (0, 0)))
    wp = jnp.concatenate([win[:, O_Z:O_XBC], win[:, O_XBC:O_DT], win[:, O_Q:O_K], win[:, O_K:O_V], win[:, O_V:O_G],
                          win[:, O_G:], win[:, O_DT:O_Q], jnp.zeros((D_MODEL, P_END - P_DT - SSD_HEADS), win.dtype)], axis=1)

    pad_h = lambda p: jnp.pad(p, ((0, 0), (0, LANES - SSD_HEADS)))
    dtb, alog = pad_h(ssd_dt_bias), pad_h(ssd_a_log)
    dskx = jnp.repeat(ssd_d, SSD_HEAD_DIM, axis=1)
    inv = ROPE_BASE ** (-jnp.arange(0, RET_QK, 2, dtype=F32) / RET_QK)
    ang = jnp.arange(T, dtype=F32)[:, None] * inv[None, :]
    cs_, sn_ = jnp.cos(ang), jnp.sin(ang)
    cos = jnp.concatenate([cs_, cs_, cs_, cs_], axis=1)
    sin = jnp.concatenate([-sn_, sn_, -sn_, sn_], axis=1)

    h, z, xbc, q, k, v, g, dtr = _fwd_in(xi, pre_mix_norm_w, wp)
    ys, ypre, sst = _ssd_fwd(xbc, dtr, z, scw, ssd_conv_b, dtb, alog, dskx, ssd_norm_w)
    yr, rst = _ret_fwd(q, k, v, g, cos, sin, ret_norm_w)
    y, x1, h2, graw, val = _fwd_mid(ys, yr, xi, wout, post_mix_norm_w, pre_ffn_norm_w, wup)
    a, dfb, dval, dgate, dx2, lossb, d_pff, d_fcb = _ffn_tail(graw, val, x1, tgt, fcw, ffn_conv_b, wdown, post_ffn_norm_w)
    dgraw, dx1, dyb, dys, dyr, d_fcw, d_pf, d_pm = _ffn_bwd(dgate, dval, graw, x1, dx2, y, fcw, wup, pre_ffn_norm_w,
                                                         post_mix_norm_w, wout)
    dz, dxbc, ddt, d_scw, d_scb, d_dtb, d_alog, d_dsk, d_snw = _ssd_bwd(dys, ypre, xbc, dtr, z, sst, scw, ssd_conv_b, dtb,
                                                                      alog, dskx, ssd_norm_w)
    dq, dk, dv, dg, d_rnw = _ret_bwd(dyr, q, k, v, g, rst, cos, sin, ret_norm_w)
    gx, d_w0 = _in_bwd(dz, dxbc, dq, dk, dv, dg, ddt, xi, dx1, pre_mix_norm_w, wp)
    gin = jnp.concatenate([_matmul_tn(h, dz, "dw_z"), _matmul_tn(h, dxbc, "dw_xbc"),
                           _matmul_tn(h, ddt, "dw_dt")[:, :SSD_HEADS], _matmul_tn(h, dq, "dw_q"), _matmul_tn(h, dk, "dw_k"),
                           _matmul_tn(h, dv, "dw_v"), _matmul_tn(h, dg, "dw_g")], axis=1)
    gout = jnp.concatenate([_matmul_tn(ys, dyb, "dw_out_s"), _matmul_tn(yr, dyb, "dw_out_r")], axis=0)
    gup = jnp.concatenate([_matmul_tn(h2, dgraw, "dw_up_g"), _matmul_tn(h2, dval, "dw_up_v")], axis=1)
    gdown = _matmul_tn(a, dfb, "dw_down")
    send = [jnp.transpose(gin.reshape(D_MODEL, N_DEV, n_in), (1, 0, 2)).astype(BF), gout.reshape(N_DEV, n_out, D_MODEL).astype(BF),
            jnp.transpose(gup.reshape(D_MODEL, N_DEV, n_up), (1, 0, 2)).astype(BF), gdown.reshape(N_DEV, n_down, D_MODEL).astype(BF)]
    recv = _exchange(send, "scatter_g", False)
    per_w = [_adamw(r, w[0], m[0], v[0], nm, tr) for r, w, m, v, nm, tr in (
        (recv[0], w_in, m_w_in, v_w_in, "adamw_in", 256), (recv[1], w_out, m_w_out, v_w_out, "adamw_out", n_out),
        (recv[2], w_up, m_w_up, v_w_up, "adamw_up", 256), (recv[3], w_down, m_w_down, v_w_down, "adamw_down", n_down))]
    big = [[per_w[i][kind][None] for i in range(4)] for kind in range(4)]

    small_full = [d_w0, d_scw[:SSD_CONV], d_scb, d_dtb[:, :SSD_HEADS], d_alog[:, :SSD_HEADS], d_dsk[:, :SSD_HEADS], d_snw, d_rnw,
                  d_pm, d_pf, d_fcw[:FFN_CONV], d_fcb, d_pff]
    full_shapes = [t.shape for t in small_full]
    gs = _sum_slabs(_exchange([_pack(small_full, SMALL_ROWS)], "gather_small", True)[0], "sum_small")
    gfull = _unpack(gs, full_shapes)
    gfull[1] = lax.dynamic_slice_in_dim(gfull[1], me * n_sc, n_sc, axis=1)
    gfull[10] = lax.dynamic_slice_in_dim(gfull[10], me * n_fc, n_fc, axis=1)
    ws = [pre_mix_norm_w, ssd_conv_w, ssd_conv_b, ssd_dt_bias, ssd_a_log, ssd_d, ssd_norm_w, ret_norm_w, post_mix_norm_w,
          pre_ffn_norm_w, ffn_conv_w, ffn_conv_b, post_ffn_norm_w]
    ms = [m_pre_mix_norm_w, m_ssd_conv_w, m_ssd_conv_b, m_ssd_dt_bias, m_ssd_a_log, m_ssd_d, m_ssd_norm_w, m_ret_norm_w,
          m_post_mix_norm_w, m_pre_ffn_norm_w, m_ffn_conv_w, m_ffn_conv_b, m_post_ffn_norm_w]
    vs = [v_pre_mix_norm_w, v_ssd_conv_w, v_ssd_conv_b, v_ssd_dt_bias, v_ssd_a_log, v_ssd_d, v_ssd_norm_w, v_ret_norm_w,
          v_post_mix_norm_w, v_pre_ffn_norm_w, v_ffn_conv_w, v_ffn_conv_b, v_post_ffn_norm_w]
    out_shapes = [t.shape for t in ws]
    small = _adamw(_pack(gfull, SMALL_ROWS)[None], _pack(ws, SMALL_ROWS), _pack(ms, SMALL_ROWS), _pack(vs, SMALL_ROWS),
                   "adamw_small", SMALL_ROWS)
    small = [_unpack(b, out_shapes) for b in small]

    order = {"pre_mix_norm_w": ("s", 0), "w_in": ("b", 0), "ssd_conv_w": ("s", 1), "ssd_conv_b": ("s", 2),
             "ssd_dt_bias": ("s", 3), "ssd_a_log": ("s", 4), "ssd_d": ("s", 5), "ssd_norm_w": ("s", 6), "ret_norm_w": ("s", 7),
             "w_out": ("b", 1), "post_mix_norm_w": ("s", 8), "pre_ffn_norm_w": ("s", 9), "w_up": ("b", 2),
             "ffn_conv_w": ("s", 10), "ffn_conv_b": ("s", 11), "w_down": ("b", 3), "post_ffn_norm_w": ("s", 12)}
    loss = lax.psum(lossb[0, 0], ("x", "y", "c"))
    outs = [loss, gx[None]]
    for kind in range(4):
        for name, (grp, idx) in order.items():
            outs.append(big[kind][idx] if grp == "b" else small[kind][idx])
    return tuple(outs)
```

```python
import functools
import math

import numpy as np
import jax
import jax.numpy as jnp
from jax import lax
from jax.experimental import pallas as pl
from jax.experimental.pallas import tpu as pltpu

F32 = jnp.float32
BF = jnp.bfloat16
HI = lax.Precision.HIGHEST
S = jax.ShapeDtypeStruct

D_MODEL = 1024
SSD_HEADS = 16
SSD_HEAD_DIM = 64
SSD_GROUPS = 2
SSD_STATE = 128
SSD_WIDTH = 1024
SSD_XBC = 1536
SSD_CONV = 4
RET_HEADS = 8
RET_QK = 64
RET_V = 128
RET_QK_W = 512
RET_V_W = 1024
ROPE_BASE = 10000.0
CH = 128
D_FF = 2816
FFN_CONV = 3
EPS = 1e-6
IN_WIDTH = 5648
N_DEV = 8

ADAM_LR = 0.001
ADAM_B1 = 0.9
ADAM_B2 = 0.999
ADAM_EPS = 1e-08
ADAM_WD = 0.01
ADAM_STEP = 10

LANES = 128
HALO = 16
VMEM_LIMIT = 48 * 1024 * 1024

P_Z, P_XBC, P_Q, P_K, P_V, P_G, P_DT, P_END = 0, 1024, 2560, 3072, 3584, 4608, 5632, 5760
O_Z, O_XBC, O_DT, O_Q, O_K, O_V, O_G = 0, 1024, 2560, 2576, 3088, 3600, 4624


def _cparams(*sem):
    return pltpu.CompilerParams(dimension_semantics=sem, vmem_limit_bytes=VMEM_LIMIT)


def _dot(a, b):
    return jnp.dot(a.astype(BF), b.astype(BF), preferred_element_type=F32)


def _dot_nt(a, b):
    return lax.dot_general(a.astype(BF), b.astype(BF), (((1,), (1,)), ((), ())), preferred_element_type=F32)


def _dot_tn(a, b):
    return lax.dot_general(a.astype(BF), b.astype(BF), (((0,), (0,)), ((), ())), preferred_element_type=F32)


def _dot_hi(a, b):
    return jnp.dot(a, b, preferred_element_type=F32, precision=HI)


def _dot_tn_hi(a, b):
    return lax.dot_general(a, b, (((0,), (0,)), ((), ())), preferred_element_type=F32, precision=HI)


def _sigmoid(x):
    return jax.nn.sigmoid(x)


def _dsilu(x, s):
    return s * (1.0 + x * (1.0 - s))


def _softplus(x):
    return jnp.maximum(x, 0.0) + jnp.log1p(jnp.exp(-jnp.abs(x)))


def _rstd(x):
    return lax.rsqrt(jnp.mean(x * x, axis=-1, keepdims=True) + EPS)


def _rms_bwd(dy, x, r, w):
    gn = dy * w
    dx = r * gn - x * (r * r * r) * jnp.mean(gn * x, axis=-1, keepdims=True)
    dw = jnp.sum(dy * x * r, axis=0, keepdims=True)
    return dx, dw


def _rows_before(ext, s, head, n):
    if s == 0:
        return ext[head:head + n]
    return pltpu.roll(ext, s, 0)[head:head + n]


def _rows_after(ext, s, n):
    if s == 0:
        return ext[0:n]
    return pltpu.roll(ext, ext.shape[0] - s, 0)[0:n]


def _row_spec(tm, width):
    return pl.BlockSpec((tm, width), lambda i: (i, 0))


def _const_spec(shape):
    return pl.BlockSpec(shape, lambda i: (0,) * len(shape))


_VMEM_WHOLE = pl.BlockSpec(memory_space=pltpu.VMEM)


def _fwd_in(x, w0, wp, tm=256):
    T = x.shape[0]

    def body(x_ref, w0_ref, wp_ref, h_ref, z_ref, xbc_ref, q_ref, k_ref, v_ref, g_ref, dt_ref):
        xf = x_ref[...]
        h = (xf * _rstd(xf) * w0_ref[...]).astype(BF)
        h_ref[...] = h
        for ref, lo, hi in ((z_ref, P_Z, P_XBC), (xbc_ref, P_XBC, P_Q), (q_ref, P_Q, P_K), (k_ref, P_K, P_V),
                            (v_ref, P_V, P_G), (g_ref, P_G, P_DT), (dt_ref, P_DT, P_END)):
            ref[...] = jnp.dot(h, wp_ref[:, lo:hi], preferred_element_type=F32).astype(ref.dtype)

    widths = (D_MODEL, SSD_WIDTH, SSD_XBC, RET_QK_W, RET_QK_W, RET_V_W, RET_V_W)
    return pl.pallas_call(
        body, name="fwd_in", grid=(T // tm,),
        in_specs=[_row_spec(tm, D_MODEL), _const_spec((1, D_MODEL)), _VMEM_WHOLE],
        out_specs=[_row_spec(tm, w) for w in widths] + [_row_spec(tm, LANES)],
        out_shape=[S((T, w), BF) for w in widths] + [S((T, LANES), F32)],
        compiler_params=_cparams("parallel"),
    )(x, w0, wp)


def _fwd_mid(ys, yr, x, wout, wpm, wpf, wup, tm=256):
    T = x.shape[0]

    def body(ys_ref, yr_ref, x_ref, wout_ref, wpm_ref, wpf_ref, wup_ref, y_ref, x1_ref, h2_ref, graw_ref, val_ref):
        y = (jnp.dot(ys_ref[...], wout_ref[0:SSD_WIDTH, :], preferred_element_type=F32)
             + jnp.dot(yr_ref[...], wout_ref[SSD_WIDTH:, :], preferred_element_type=F32))
        y_ref[...] = y
        x1 = x_ref[...] + y * _rstd(y) * wpm_ref[...]
        x1_ref[...] = x1
        h2 = (x1 * _rstd(x1) * wpf_ref[...]).astype(BF)
        h2_ref[...] = h2
        graw_ref[...] = jnp.dot(h2, wup_ref[:, 0:D_FF], preferred_element_type=F32).astype(BF)
        val_ref[...] = jnp.dot(h2, wup_ref[:, D_FF:], preferred_element_type=F32).astype(BF)

    return pl.pallas_call(
        body, name="fwd_mid", grid=(T // tm,),
        in_specs=[_row_spec(tm, SSD_WIDTH), _row_spec(tm, RET_V_W), _row_spec(tm, D_MODEL), _VMEM_WHOLE,
                  _const_spec((1, D_MODEL)), _const_spec((1, D_MODEL)), _VMEM_WHOLE],
        out_specs=[_row_spec(tm, D_MODEL), _row_spec(tm, D_MODEL), _row_spec(tm, D_MODEL), _row_spec(tm, D_FF),
                   _row_spec(tm, D_FF)],
        out_shape=[S((T, D_MODEL), F32), S((T, D_MODEL), F32), S((T, D_MODEL), BF), S((T, D_FF), BF), S((T, D_FF), BF)],
        compiler_params=_cparams("parallel"),
    )(ys, yr, x, wout, wpm, wpf, wup)


def _ffn_tail(graw, val, x1, tgt, convw, convb, wdown, wpff, tm=256):
    T = x1.shape[0]

    def body(graw_ref, val_ref, x1_ref, tgt_ref, cw_ref, cb_ref, wd_ref, wpff_ref,
             a_ref, df_ref, dval_ref, dgate_ref, dx2_ref, loss_ref, dwpff_ref, dcb_ref, carry):
        i = pl.program_id(0)

        @pl.when(i == 0)
        def _():
            carry[...] = jnp.zeros_like(carry)
            loss_ref[...] = jnp.zeros_like(loss_ref)
            dwpff_ref[...] = jnp.zeros_like(dwpff_ref)
            dcb_ref[...] = jnp.zeros_like(dcb_ref)

        g = graw_ref[...].astype(F32)
        ext = jnp.concatenate([carry[...], g], axis=0)
        carry[...] = g[tm - 8:tm]
        gate = cb_ref[...] + sum(cw_ref[j:j + 1, :] * _rows_before(ext, FFN_CONV - 1 - j, 8, tm) for j in range(FFN_CONV))
        sg = _sigmoid(gate)
        silu = gate * sg
        v = val_ref[...].astype(F32)
        a = (silu * v).astype(BF)
        a_ref[...] = a
        f = jnp.dot(a, wd_ref[...], preferred_element_type=F32)
        r = _rstd(f)
        w = wpff_ref[...]
        e = x1_ref[...] + f * r * w - tgt_ref[...]
        loss_ref[...] += jnp.sum(e * e) * (0.5 / D_MODEL)
        dx2 = e * (1.0 / D_MODEL)
        dx2_ref[...] = dx2
        df, dw = _rms_bwd(dx2, f, r, w)
        dwpff_ref[...] += dw
        dfb = df.astype(BF)
        df_ref[...] = dfb
        da = _dot_nt(dfb, wd_ref[...])
        dval_ref[...] = (da * silu).astype(BF)
        dgate = da * v * _dsilu(gate, sg)
        dcb_ref[...] += jnp.sum(dgate, axis=0, keepdims=True)
        dgate_ref[...] = dgate.astype(BF)

    return pl.pallas_call(
        body, name="ffn_tail", grid=(T // tm,),
        in_specs=[_row_spec(tm, D_FF), _row_spec(tm, D_FF), _row_spec(tm, D_MODEL), _row_spec(tm, D_MODEL),
                  _const_spec((8, D_FF)), _const_spec((1, D_FF)), _VMEM_WHOLE, _const_spec((1, D_MODEL))],
        out_specs=[_row_spec(tm, D_FF), _row_spec(tm, D_MODEL), _row_spec(tm, D_FF), _row_spec(tm, D_FF),
                   _row_spec(tm, D_MODEL), _const_spec((8, LANES)), _const_spec((1, D_MODEL)), _const_spec((1, D_FF))],
        out_shape=[S((T, D_FF), BF), S((T, D_MODEL), BF), S((T, D_FF), BF), S((T, D_FF), BF), S((T, D_MODEL), F32),
                   S((8, LANES), F32), S((1, D_MODEL), F32), S((1, D_FF), F32)],
        scratch_shapes=[pltpu.VMEM((8, D_FF), F32)],
        compiler_params=_cparams("arbitrary"),
    )(graw, val, x1, tgt, convw, convb, wdown, wpff)


def _ffn_bwd(dgate, dval, graw, x1, dx2, y, convw, wup, wpf, wpm, wout, tm=256):
    T = x1.shape[0]
    nt = T // tm
    rev = lambda i: (nt - 1 - i, 0)
    rspec = lambda w: pl.BlockSpec((tm, w), rev)

    def body(dgate_ref, dval_ref, graw_ref, x1_ref, dx2_ref, y_ref, cw_ref, wup_ref, wpf_ref, wpm_ref, wout_ref,
             dgraw_ref, dx1_ref, dy_ref, dys_ref, dyr_ref, dcw_ref, dwpf_ref, dwpm_ref, carry):
        i = pl.program_id(0)

        @pl.when(i == 0)
        def _():
            carry[...] = jnp.zeros_like(carry)
            dcw_ref[...] = jnp.zeros_like(dcw_ref)
            dwpf_ref[...] = jnp.zeros_like(dwpf_ref)
            dwpm_ref[...] = jnp.zeros_like(dwpm_ref)

        dg = dgate_ref[...].astype(F32)
        ext = jnp.concatenate([dg, carry[...]], axis=0)
        carry[...] = dg[0:8]
        g = graw_ref[...].astype(F32)
        dgraw = jnp.zeros((tm, D_FF), F32)
        for j in range(FFN_CONV):
            sj = _rows_after(ext, FFN_CONV - 1 - j, tm)
            dgraw = dgraw + cw_ref[j:j + 1, :] * sj
            dcw_ref[j:j + 1, :] += jnp.sum(sj * g, axis=0, keepdims=True)
        dgrawb = dgraw.astype(BF)
        dgraw_ref[...] = dgrawb
        dh2 = _dot_nt(dgrawb, wup_ref[:, 0:D_FF]) + _dot_nt(dval_ref[...], wup_ref[:, D_FF:])
        x1 = x1_ref[...]
        dxa, dw = _rms_bwd(dh2, x1, _rstd(x1), wpf_ref[...])
        dwpf_ref[...] += dw
        dx1 = dx2_ref[...] + dxa
        dx1_ref[...] = dx1
        yv = y_ref[...]
        dy, dw = _rms_bwd(dx1, yv, _rstd(yv), wpm_ref[...])
        dwpm_ref[...] += dw
        dyb = dy.astype(BF)
        dy_ref[...] = dyb
        dys_ref[...] = _dot_nt(dyb, wout_ref[0:SSD_WIDTH, :]).astype(BF)
        dyr_ref[...] = _dot_nt(dyb, wout_ref[SSD_WIDTH:, :]).astype(BF)

    return pl.pallas_call(
        body, name="ffn_bwd", grid=(nt,),
        in_specs=[rspec(D_FF), rspec(D_FF), rspec(D_FF), rspec(D_MODEL), rspec(D_MODEL), rspec(D_MODEL),
                  _const_spec((8, D_FF)), _VMEM_WHOLE, _const_spec((1, D_MODEL)), _const_spec((1, D_MODEL)), _VMEM_WHOLE],
        out_specs=[rspec(D_FF), rspec(D_MODEL), rspec(D_MODEL), rspec(SSD_WIDTH), rspec(RET_V_W),
                   _const_spec((8, D_FF)), _const_spec((1, D_MODEL)), _const_spec((1, D_MODEL))],
        out_shape=[S((T, D_FF), BF), S((T, D_MODEL), F32), S((T, D_MODEL), BF), S((T, SSD_WIDTH), BF), S((T, RET_V_W), BF),
                   S((8, D_FF), F32), S((1, D_MODEL), F32), S((1, D_MODEL), F32)],
        scratch_shapes=[pltpu.VMEM((8, D_FF), F32)],
        compiler_params=_cparams("arbitrary"),
    )(dgate, dval, graw, x1, dx2, y, convw, wup, wpf, wpm, wout)


def _in_bwd(dz, dxbc, dq, dk, dv, dg, ddt, x, dx1, w0, wp, tm=256):
    T = x.shape[0]

    def body(dz_ref, dxbc_ref, dq_ref, dk_ref, dv_ref, dg_ref, ddt_ref, x_ref, dx1_ref, w0_ref, wp_ref, gx_ref, dw0_ref):
        @pl.when(pl.program_id(0) == 0)
        def _():
            dw0_ref[...] = jnp.zeros_like(dw0_ref)

        dh = jnp.zeros((tm, D_MODEL), F32)
        for ref, lo, hi in ((dz_ref, P_Z, P_XBC), (dxbc_ref, P_XBC, P_Q), (dq_ref, P_Q, P_K), (dk_ref, P_K, P_V),
                            (dv_ref, P_V, P_G), (dg_ref, P_G, P_DT), (ddt_ref, P_DT, P_END)):
            dh = dh + _dot_nt(ref[...], wp_ref[:, lo:hi])
        xf = x_ref[...]
        dx, dw = _rms_bwd(dh, xf, _rstd(xf), w0_ref[...])
        dw0_ref[...] += dw
        gx_ref[...] = dx1_ref[...] + dx

    widths = (SSD_WIDTH, SSD_XBC, RET_QK_W, RET_QK_W, RET_V_W, RET_V_W, LANES)
    return pl.pallas_call(
        body, name="in_bwd", grid=(T // tm,),
        in_specs=[_row_spec(tm, w) for w in widths] + [_row_spec(tm, D_MODEL), _row_spec(tm, D_MODEL),
                                                       _const_spec((1, D_MODEL)), _VMEM_WHOLE],
        out_specs=[_row_spec(tm, D_MODEL), _const_spec((1, D_MODEL))],
        out_shape=[S((T, D_MODEL), F32), S((1, D_MODEL), F32)],
        compiler_params=_cparams("arbitrary"),
    )(dz, dxbc, dq, dk, dv, dg, ddt, x, dx1, w0, wp)


def _matmul_tn(a, b, name, tk=512):
    T, M = a.shape
    N = b.shape[1]
    tn = N
    while M * tn * 4 > (4 << 20) and tn % 256 == 0:
        tn //= 2
    nk = T // tk

    def body(a_ref, b_ref, o_ref):
        @pl.when(pl.program_id(1) == 0)
        def _():
            o_ref[...] = jnp.zeros_like(o_ref)

        o_ref[...] += _dot_tn(a_ref[...], b_ref[...])

    return pl.pallas_call(
        body, name=name, grid=(N // tn, nk),
        in_specs=[pl.BlockSpec((tk, M), lambda n, k: (k, 0)), pl.BlockSpec((tk, tn), lambda n, k: (k, n))],
        out_specs=pl.BlockSpec((M, tn), lambda n, k: (0, n)),
        out_shape=S((M, N), F32),
        compiler_params=_cparams("parallel", "arbitrary"),
    )(a, b)


def _tri(lower):
    r = lax.broadcasted_iota(jnp.int32, (CH, CH), 0)
    c = lax.broadcasted_iota(jnp.int32, (CH, CH), 1)
    return ((c <= r) if lower else (r <= c)).astype(F32)


def _onehot_row(h):
    return (lax.broadcasted_iota(jnp.int32, (1, LANES), 1) == h).astype(F32)


def _onehot_col(h):
    return (lax.broadcasted_iota(jnp.int32, (LANES, 1), 0) == h).astype(F32)


def _ssd_pre(xc_ref, xh_ref, dtr_ref, cw_ref, cb_ref, dtb_ref, alog_ref, first):
    xc = xc_ref[...].astype(F32)
    xh = jnp.where(first, 0.0, xh_ref[...].astype(F32))
    ext = jnp.concatenate([xh, xc], axis=0)
    u = cb_ref[...] + sum(cw_ref[j:j + 1, :] * _rows_before(ext, SSD_CONV - 1 - j, HALO, CH) for j in range(SSD_CONV))
    sg = _sigmoid(u)
    act = u * sg
    dt = _softplus(dtr_ref[...] + dtb_ref[...])
    a = -jnp.exp(alog_ref[...])
    da = dt * a
    cs = _dot_hi(_tri(True), da)
    cst = _dot_tn_hi(da, _tri(False))
    return xc, u, sg, act, dt, a, cs, cst


HPG = SSD_HEADS // SSD_GROUPS
GW = HPG * SSD_HEAD_DIM


def _expand_heads(src, buf):
    for h in range(SSD_HEADS):
        buf[:, h * SSD_HEAD_DIM:(h + 1) * SSD_HEAD_DIM] = jnp.broadcast_to(src[:, h:h + 1], (CH, SSD_HEAD_DIM))


def _ssd_expanded(act, dt, cs, dtx, csx):
    _expand_heads(dt, dtx)
    _expand_heads(cs, csx)
    csv = csx[...]
    last = csv[CH - 1:CH, :]
    e_exp = jnp.exp(csv)
    dec_exp = jnp.exp(last - csv)
    el_exp = jnp.exp(last)
    xs = act[:, 0:SSD_WIDTH]
    xdt = xs * dtx[...]
    return xs, xdt, xdt * dec_exp, e_exp, dec_exp, el_exp


def _decay_mats(h, cs, cst, transposed):
    r = lax.broadcasted_iota(jnp.int32, (CH, CH), 0)
    c = lax.broadcasted_iota(jnp.int32, (CH, CH), 1)
    c_col = cs[:, h:h + 1]
    c_row = cst[h:h + 1, :]
    if transposed:
        return jnp.exp(jnp.where(r <= c, c_row - c_col, -1e30))
    return jnp.exp(jnp.where(r >= c, c_col - c_row, -1e30))


def _ssd_specs(T):
    nc = T // CH
    return nc, [
        _row_spec(CH, SSD_XBC),
        pl.BlockSpec((HALO, SSD_XBC), lambda i: (jnp.maximum(i * (CH // HALO) - 1, 0), 0)),
        _row_spec(CH, LANES),
        _row_spec(CH, SSD_WIDTH),
    ]


def _groups(act):
    bm = [act[:, SSD_WIDTH + g * SSD_STATE:SSD_WIDTH + (g + 1) * SSD_STATE] for g in range(SSD_GROUPS)]
    o = SSD_WIDTH + SSD_GROUPS * SSD_STATE
    cm = [act[:, o + g * SSD_STATE:o + (g + 1) * SSD_STATE] for g in range(SSD_GROUPS)]
    return bm, cm


def _ssd_fwd(xbc, dtr, z, convw, convb, dtb, alog, dskx, nw):
    T = xbc.shape[0]
    nc, specs = _ssd_specs(T)

    def body(xc_ref, xh_ref, dtr_ref, z_ref, cw_ref, cb_ref, dtb_ref, alog_ref, dskx_ref, nw_ref,
             out_ref, y_ref, st_ref, state, ybuf, dtx, csx):
        i = pl.program_id(0)

        @pl.when(i == 0)
        def _():
            state[...] = jnp.zeros_like(state)

        xc, u, sg, act, dt, a, cs, cst = _ssd_pre(xc_ref, xh_ref, dtr_ref, cw_ref, cb_ref, dtb_ref, alog_ref, i == 0)
        xs, xdt, w, e_exp, dec_exp, el_exp = _ssd_expanded(act, dt, cs, dtx, csx)
        bm, cm = _groups(act)
        for g in range(SSD_GROUPS):
            gs = slice(g * GW, (g + 1) * GW)
            st = state[g]
            st_ref[0, g] = st
            cb = _dot_nt(cm[g], bm[g])
            ybuf[:, gs] = _dot(cm[g], st) * e_exp[:, gs] + xs[:, gs] * dskx_ref[:, gs]
            state[g] = st * el_exp[:, gs] + _dot_tn(bm[g], w[:, gs])
            for h in range(g * HPG, (g + 1) * HPG):
                sl = slice(h * SSD_HEAD_DIM, (h + 1) * SSD_HEAD_DIM)
                ybuf[:, sl] += _dot(cb * _decay_mats(h, cs, cst, False), xdt[:, sl])
        yv = ybuf[...]
        y_ref[...] = yv.astype(BF)
        zf = z_ref[...].astype(F32)
        gated = yv * (zf * _sigmoid(zf))
        out_ref[...] = (gated * _rstd(gated) * nw_ref[...]).astype(BF)

    st_spec = pl.BlockSpec((1, SSD_GROUPS, SSD_STATE, GW), lambda i: (i, 0, 0, 0))
    return pl.pallas_call(
        body, name="ssd_fwd", grid=(nc,),
        in_specs=specs + [_const_spec((8, SSD_XBC)), _const_spec((1, SSD_XBC)), _const_spec((1, LANES)),
                          _const_spec((1, LANES)), _const_spec((1, SSD_WIDTH)), _const_spec((1, SSD_WIDTH))],
        out_specs=[_row_spec(CH, SSD_WIDTH), _row_spec(CH, SSD_WIDTH), st_spec],
        out_shape=[S((T, SSD_WIDTH), BF), S((T, SSD_WIDTH), BF), S((nc, SSD_GROUPS, SSD_STATE, GW), F32)],
        scratch_shapes=[pltpu.VMEM((SSD_GROUPS, SSD_STATE, GW), F32), pltpu.VMEM((CH, SSD_WIDTH), F32),
                        pltpu.VMEM((CH, SSD_WIDTH), F32), pltpu.VMEM((CH, SSD_WIDTH), F32)],
        compiler_params=_cparams("arbitrary"),
    )(xbc, xbc, dtr, z, convw, convb, dtb, alog, dskx, nw)


def _ssd_bwd(dout, y, xbc, dtr, z, states, convw, convb, dtb, alog, dskx, nw):
    T = xbc.shape[0]
    nc = T // CH
    rev = lambda i: (nc - 1 - i, 0)
    rspec = lambda w: pl.BlockSpec((CH, w), rev)
    halo_spec = pl.BlockSpec((HALO, SSD_XBC), lambda i: (jnp.maximum((nc - 1 - i) * (CH // HALO) - 1, 0), 0))
    NB = SSD_WIDTH
    NC_ = SSD_WIDTH + SSD_GROUPS * SSD_STATE

    def body(do_ref, y_ref, xc_ref, xh_ref, dtr_ref, z_ref, st_ref, cw_ref, cb_ref, dtb_ref, alog_ref, dskx_ref, nw_ref,
             dz_ref, dxbc_ref, ddt_ref, dcw_ref, dcb_ref, ddtb_ref, dalog_ref, ddsk_ref, dnw_ref,
             dstate, ducarry, dtx, csx, dxdtbuf, dact):
        i = pl.program_id(0)

        @pl.when(i == 0)
        def _():
            dstate[...] = jnp.zeros_like(dstate)
            ducarry[...] = jnp.zeros_like(ducarry)
            for ref in (dcw_ref, dcb_ref, ddtb_ref, dalog_ref, ddsk_ref, dnw_ref):
                ref[...] = jnp.zeros_like(ref)

        xc, u, sg, act, dt, a, cs, cst = _ssd_pre(xc_ref, xh_ref, dtr_ref, cw_ref, cb_ref, dtb_ref, alog_ref, i == nc - 1)
        xs, xdt, w, e_exp, dec_exp, el_exp = _ssd_expanded(act, dt, cs, dtx, csx)
        bm, cm = _groups(act)
        yv = y_ref[...].astype(F32)
        zf = z_ref[...].astype(F32)
        sz = _sigmoid(zf)
        gated = yv * (zf * sz)
        dgated, dnw = _rms_bwd(do_ref[...].astype(F32), gated, _rstd(gated), nw_ref[...])
        dnw_ref[...] += dnw
        dz_ref[...] = (dgated * yv * _dsilu(zf, sz)).astype(BF)
        dy = dgated * (zf * sz)
        lane_of = lax.broadcasted_iota(jnp.int32, (SSD_WIDTH, LANES), 0) - SSD_HEAD_DIM * lax.broadcasted_iota(jnp.int32, (SSD_WIDTH, LANES), 1)
        expt = ((lane_of >= 0) & (lane_of < SSD_HEAD_DIM)).astype(F32)
        ddsk_ref[...] += _dot_hi(jnp.sum(dy * xs, axis=0, keepdims=True), expt)
        dcs = jnp.zeros((CH, LANES), F32)
        dcst = jnp.zeros((LANES, CH), F32)
        ddt = jnp.zeros((CH, LANES), F32)
        lastrows = []
        for g in range(SSD_GROUPS):
            gs = slice(g * GW, (g + 1) * GW)
            st = st_ref[0, g]
            dsn = dstate[g]
            cbm = _dot_nt(cm[g], bm[g])
            cbt = _dot_nt(bm[g], cm[g])
            dy_g = dy[:, gs]
            yoff = _dot(cm[g], st) * e_exp[:, gs]
            dq = dy_g * e_exp[:, gs]
            dcm_g = _dot_nt(dq, st)
            dstate[g] = _dot_tn(cm[g], dq) + dsn * el_exp[:, gs]
            dw = _dot(bm[g], dsn)
            w_g = w[:, gs]
            dbm_g = _dot_nt(w_g, dsn)
            dww = dw * w_g
            red = dy_g * yoff - dww
            lastrows.append(jnp.sum(dsn * st, axis=0, keepdims=True) * el_exp[:, gs] + jnp.sum(dww, axis=0, keepdims=True))
            dxdtbuf[:, gs] = dw * dec_exp[:, gs]
            dcb = jnp.zeros((CH, CH), F32)
            for h in range(g * HPG, (g + 1) * HPG):
                sl = slice(h * SSD_HEAD_DIM, (h + 1) * SSD_HEAD_DIM)
                rl = slice((h - g * HPG) * SSD_HEAD_DIM, (h - g * HPG + 1) * SSD_HEAD_DIM)
                oh = _onehot_row(h)
                lmat = _decay_mats(h, cs, cst, False)
                mmat = cbm * lmat
                dy_h = dy[:, sl]
                dm = _dot_nt(dy_h, xdt[:, sl])
                dxdt_h = dxdtbuf[:, sl] + _dot(cbt * _decay_mats(h, cs, cst, True), dy_h)
                dxdtbuf[:, sl] = dxdt_h
                dseg = dm * mmat
                dcb = dcb + dm * lmat
                col = jnp.sum(dseg, axis=1, keepdims=True) + jnp.sum(red[:, rl], axis=1, keepdims=True)
                dcs = dcs + col * oh
                dcst = dcst - _onehot_col(h) * jnp.sum(dseg, axis=0, keepdims=True)
                ddt = ddt + jnp.sum(dxdt_h * xs[:, sl], axis=1, keepdims=True) * oh
            dact[:, NB + g * SSD_STATE:NB + (g + 1) * SSD_STATE] = dbm_g + _dot_tn(dcb, cm[g])
            dact[:, NC_ + g * SSD_STATE:NC_ + (g + 1) * SSD_STATE] = dcm_g + _dot(dcb, bm[g])
        dact[:, 0:SSD_WIDTH] = dy * dskx_ref[...] + dxdtbuf[...] * dtx[...]
        dlast = _dot_hi(jnp.concatenate(lastrows, axis=1), expt)
        rows = lax.broadcasted_iota(jnp.int32, (CH, LANES), 0)
        dcs = dcs + _dot_tn_hi(dcst, jnp.eye(LANES, dtype=F32)) + jnp.where(rows == CH - 1, dlast, 0.0)
        dda = _dot_hi(_tri(False), dcs)
        dalog_ref[...] += jnp.sum(dda * dt, axis=0, keepdims=True) * a
        ddt = ddt + dda * a
        ddtr = ddt * _sigmoid(dtr_ref[...] + dtb_ref[...])
        ddtb_ref[...] += jnp.sum(ddtr, axis=0, keepdims=True)
        ddt_ref[...] = ddtr.astype(BF)
        du = dact[...] * _dsilu(u, sg)
        dcb_ref[...] += jnp.sum(du, axis=0, keepdims=True)
        ext = jnp.concatenate([du, ducarry[...]], axis=0)
        ducarry[...] = du[0:8]
        dx = jnp.zeros((CH, SSD_XBC), F32)
        for j in range(SSD_CONV):
            sj = _rows_after(ext, SSD_CONV - 1 - j, CH)
            dx = dx + cw_ref[j:j + 1, :] * sj
            dcw_ref[j:j + 1, :] += jnp.sum(sj * xc, axis=0, keepdims=True)
        dxbc_ref[...] = dx.astype(BF)

    return pl.pallas_call(
        body, name="ssd_bwd", grid=(nc,),
        in_specs=[rspec(SSD_WIDTH), rspec(SSD_WIDTH), rspec(SSD_XBC), halo_spec, rspec(LANES), rspec(SSD_WIDTH),
                  pl.BlockSpec((1, SSD_GROUPS, SSD_STATE, GW), lambda i: (nc - 1 - i, 0, 0, 0)),
                  _const_spec((8, SSD_XBC)), _const_spec((1, SSD_XBC)), _const_spec((1, LANES)),
                  _const_spec((1, LANES)), _const_spec((1, SSD_WIDTH)), _const_spec((1, SSD_WIDTH))],
        out_specs=[rspec(SSD_WIDTH), rspec(SSD_XBC), rspec(LANES),
                   _const_spec((8, SSD_XBC)), _const_spec((1, SSD_XBC)), _const_spec((1, LANES)),
                   _const_spec((1, LANES)), _const_spec((1, LANES)), _const_spec((1, SSD_WIDTH))],
        out_shape=[S((T, SSD_WIDTH), BF), S((T, SSD_XBC), BF), S((T, LANES), BF),
                   S((8, SSD_XBC), F32), S((1, SSD_XBC), F32), S((1, LANES), F32),
                   S((1, LANES), F32), S((1, LANES), F32), S((1, SSD_WIDTH), F32)],
        scratch_shapes=[pltpu.VMEM((SSD_GROUPS, SSD_STATE, GW), F32), pltpu.VMEM((8, SSD_XBC), F32),
                        pltpu.VMEM((CH, SSD_WIDTH), F32), pltpu.VMEM((CH, SSD_WIDTH), F32),
                        pltpu.VMEM((CH, SSD_WIDTH), F32), pltpu.VMEM((CH, SSD_XBC), F32)],
        compiler_params=_cparams("arbitrary"),
    )(dout, y, xbc, xbc, dtr, z, states, convw, convb, dtb, alog, dskx, nw)


def _log_gamma(h):
    return float(np.log1p(-np.exp2(np.float32(-5.0 - h)), dtype=np.float32))


def _swap_halves(t):
    n = t.shape[1]
    lane = lax.broadcasted_iota(jnp.int32, t.shape, 1)
    return jnp.where((lane & (RET_QK - 1)) < RET_QK // 2, pltpu.roll(t, n - RET_QK // 2, 1), pltpu.roll(t, RET_QK // 2, 1))


def _rot(t, cos, sin):
    return t * cos + _swap_halves(t) * sin


def _rot_t(d, cos, sin):
    return d * cos + _swap_halves(d * sin)


def _ret_consts(h):
    lg = _log_gamma(h)
    r = lax.broadcasted_iota(jnp.int32, (CH, CH), 0)
    c = lax.broadcasted_iota(jnp.int32, (CH, CH), 1)
    rel = (r - c).astype(F32)
    dmask = jnp.where(rel >= 0, jnp.exp(lg * jnp.maximum(rel, 0.0)), 0.0)
    dmask_t = jnp.where(rel <= 0, jnp.exp(lg * jnp.maximum(-rel, 0.0)), 0.0)
    pos = lax.broadcasted_iota(jnp.int32, (CH, 1), 0).astype(F32)
    kdec = jnp.exp(lg * (CH - 1.0 - pos))
    qdec = jnp.exp(lg * (pos + 1.0))
    pos_row = lax.broadcasted_iota(jnp.int32, (1, CH), 1).astype(F32)
    kdec_row = jnp.exp(lg * (CH - 1.0 - pos_row))
    qdec_row = jnp.exp(lg * (pos_row + 1.0))
    return dmask, dmask_t, kdec, qdec, kdec_row, qdec_row, math.exp(lg * CH)


def _ret_fwd(q, k, v, g, cos, sin, nw):
    T = q.shape[0]
    nc = T // CH

    def body(q_ref, k_ref, v_ref, g_ref, cos_ref, sin_ref, nw_ref, out_ref, st_ref, state):
        i = pl.program_id(0)

        @pl.when(i == 0)
        def _():
            state[...] = jnp.zeros_like(state)

        cosf = jnp.tile(cos_ref[...], (1, RET_QK_W // LANES))
        sinf = jnp.tile(sin_ref[...], (1, RET_QK_W // LANES))
        qr = _rot(q_ref[...].astype(F32), cosf, sinf)
        kr = _rot(k_ref[...].astype(F32), cosf, sinf) * (RET_QK ** -0.5)
        krt = kr.T
        st_ref[0] = state[...]
        o_all = []
        for h in range(RET_HEADS):
            dmask, dmask_t, kdec, qdec, kdec_row, qdec_row, gam = _ret_consts(h)
            qs = slice(h * RET_QK, (h + 1) * RET_QK)
            v_h = v_ref[:, h * RET_V:(h + 1) * RET_V]
            rprev = state[h]
            scores = _dot_nt(qr[:, qs], kr[:, qs]) * dmask
            o_all.append(_dot(scores, v_h) + _dot(qr[:, qs] * qdec, rprev))
            state[h] = rprev * gam + _dot(krt[qs, :] * kdec_row, v_h)
        for h in range(RET_HEADS):
            sl = slice(h * RET_V, (h + 1) * RET_V)
            o = o_all[h]
            gf = g_ref[:, sl].astype(F32)
            out_ref[:, sl] = (o * _rstd(o) * nw_ref[:, sl] * (gf * _sigmoid(gf))).astype(BF)

    return pl.pallas_call(
        body, name="ret_fwd", grid=(nc,),
        in_specs=[_row_spec(CH, RET_QK_W), _row_spec(CH, RET_QK_W), _row_spec(CH, RET_V_W), _row_spec(CH, RET_V_W),
                  _row_spec(CH, LANES), _row_spec(CH, LANES), _const_spec((1, RET_V_W))],
        out_specs=[_row_spec(CH, RET_V_W), pl.BlockSpec((1, RET_HEADS, RET_QK, RET_V), lambda i: (i, 0, 0, 0))],
        out_shape=[S((T, RET_V_W), BF), S((nc, RET_HEADS, RET_QK, RET_V), F32)],
        scratch_shapes=[pltpu.VMEM((RET_HEADS, RET_QK, RET_V), F32)],
        compiler_params=_cparams("arbitrary"),
    )(q, k, v, g, cos, sin, nw)


def _ret_bwd(dout, q, k, v, g, states, cos, sin, nw):
    T = q.shape[0]
    nc = T // CH
    rev = lambda i: (nc - 1 - i, 0)
    rspec = lambda w: pl.BlockSpec((CH, w), rev)

    def body(do_ref, q_ref, k_ref, v_ref, g_ref, st_ref, cos_ref, sin_ref, nw_ref,
             dq_ref, dk_ref, dv_ref, dg_ref, dnw_ref, dstate, dqbuf, dkbuf):
        i = pl.program_id(0)

        @pl.when(i == 0)
        def _():
            dstate[...] = jnp.zeros_like(dstate)
            dnw_ref[...] = jnp.zeros_like(dnw_ref)

        cosf = jnp.tile(cos_ref[...], (1, RET_QK_W // LANES))
        sinf = jnp.tile(sin_ref[...], (1, RET_QK_W // LANES))
        qr = _rot(q_ref[...].astype(F32), cosf, sinf)
        kr = _rot(k_ref[...].astype(F32), cosf, sinf) * (RET_QK ** -0.5)
        qrt = qr.T
        heads = range(RET_HEADS)
        qsl = [slice(h * RET_QK, (h + 1) * RET_QK) for h in heads]
        vsl = [slice(h * RET_V, (h + 1) * RET_V) for h in heads]
        scores_t, o_all, do_all = [], [], []
        for h in heads:
            dmask, dmask_t, kdec, qdec, kdec_row, qdec_row, gam = _ret_consts(h)
            q_h, k_h = qr[:, qsl[h]], kr[:, qsl[h]]
            scores = _dot_nt(q_h, k_h) * dmask
            scores_t.append(_dot_nt(k_h, q_h) * dmask_t)
            o_all.append(_dot(scores, v_ref[:, vsl[h]]) + _dot(q_h * qdec, st_ref[0, h]))
        for h in heads:
            o = o_all[h]
            rr = _rstd(o)
            of = o * rr
            gf = g_ref[:, vsl[h]].astype(F32)
            sgg = _sigmoid(gf)
            d_h = do_ref[:, vsl[h]].astype(F32)
            nw_h = nw_ref[:, vsl[h]]
            dg_ref[:, vsl[h]] = (d_h * of * nw_h * _dsilu(gf, sgg)).astype(BF)
            dt_ = d_h * (gf * sgg)
            dnw_ref[:, vsl[h]] += jnp.sum(dt_ * of, axis=0, keepdims=True)
            dof = dt_ * nw_h
            do_all.append(rr * dof - o * (rr * rr * rr) * jnp.mean(dof * o, axis=-1, keepdims=True))
        for h in heads:
            dmask, dmask_t, kdec, qdec, kdec_row, qdec_row, gam = _ret_consts(h)
            q_h, k_h, v_h, do = qr[:, qsl[h]], kr[:, qsl[h]], v_ref[:, vsl[h]], do_all[h]
            gnext = dstate[h]
            dsc = _dot_nt(do, v_h) * dmask
            dsc_t = _dot_nt(v_h, do) * dmask_t
            dv_ref[:, vsl[h]] = (_dot(scores_t[h], do) + _dot(k_h * kdec, gnext)).astype(BF)
            dqbuf[:, qsl[h]] = _dot(dsc, k_h) + _dot_nt(do, st_ref[0, h]) * qdec
            dkbuf[:, qsl[h]] = _dot(dsc_t, q_h) + _dot_nt(v_h, gnext) * kdec
            dstate[h] = gnext * gam + _dot(qrt[qsl[h], :] * qdec_row, do)
        dq_ref[...] = _rot_t(dqbuf[...], cosf, sinf).astype(BF)
        dk_ref[...] = (_rot_t(dkbuf[...], cosf, sinf) * (RET_QK ** -0.5)).astype(BF)

    return pl.pallas_call(
        body, name="ret_bwd", grid=(nc,),
        in_specs=[rspec(RET_V_W), rspec(RET_QK_W), rspec(RET_QK_W), rspec(RET_V_W), rspec(RET_V_W),
                  pl.BlockSpec((1, RET_HEADS, RET_QK, RET_V), lambda i: (nc - 1 - i, 0, 0, 0)),
                  rspec(LANES), rspec(LANES), _const_spec((1, RET_V_W))],
        out_specs=[rspec(RET_QK_W), rspec(RET_QK_W), rspec(RET_V_W), rspec(RET_V_W), _const_spec((1, RET_V_W))],
        out_shape=[S((T, RET_QK_W), BF), S((T, RET_QK_W), BF), S((T, RET_V_W), BF), S((T, RET_V_W), BF),
                   S((1, RET_V_W), F32)],
        scratch_shapes=[pltpu.VMEM((RET_HEADS, RET_QK, RET_V), F32), pltpu.VMEM((CH, RET_QK_W), F32),
                        pltpu.VMEM((CH, RET_QK_W), F32)],
        compiler_params=_cparams("arbitrary"),
    )(dout, q, k, v, g, states, cos, sin, nw)


def _exchange(bufs, name, same):
    nb = len(bufs)
    slabs = [tuple(b.shape if same else b.shape[1:]) for b in bufs]

    def body(*refs):
        buf_refs, out_refs = refs[:nb], refs[nb:2 * nb]
        send_sems, recv_sems, local_sems = refs[2 * nb:]
        x, y, c = lax.axis_index("x"), lax.axis_index("y"), lax.axis_index("c")
        me = 4 * x + 2 * y + c

        def src(b, d):
            return buf_refs[b] if same else buf_refs[b].at[d]

        def remote(b, k, to_me):
            px = 1 - x if k & 4 else x
            py = 1 - y if k & 2 else y
            pc = 1 - c if k & 1 else c
            p = 4 * px + 2 * py + pc
            s = b * (N_DEV - 1) + k - 1
            return pltpu.make_async_remote_copy(
                src_ref=src(b, p), dst_ref=out_refs[b].at[me if to_me else p], send_sem=send_sems.at[s],
                recv_sem=recv_sems.at[s], device_id=(px, py, pc), device_id_type=pl.DeviceIdType.MESH)

        local = [pltpu.make_async_copy(src(b, me), out_refs[b].at[me], local_sems.at[b]) for b in range(nb)]
        for cp in local:
            cp.start()
        sends = [remote(b, k, True) for k in range(1, N_DEV) for b in range(nb)]
        for cp in sends:
            cp.start()
        for k in range(1, N_DEV):
            for b in range(nb):
                remote(b, k, False).wait_recv()
        for cp in sends:
            cp.wait_send()
        for cp in local:
            cp.wait()

    any_spec = pl.BlockSpec(memory_space=pl.ANY)
    return pl.pallas_call(
        body, name=name,
        in_specs=[any_spec] * nb, out_specs=[any_spec] * nb,
        out_shape=[S((N_DEV,) + s, b.dtype) for s, b in zip(slabs, bufs)],
        scratch_shapes=[pltpu.SemaphoreType.DMA((nb * (N_DEV - 1),)), pltpu.SemaphoreType.DMA((nb * (N_DEV - 1),)),
                        pltpu.SemaphoreType.DMA((nb,))],
    )(*bufs)


_HBM = pl.BlockSpec(memory_space=pltpu.HBM)
_SEM = pl.BlockSpec(memory_space=pltpu.SEMAPHORE)
_EFFECT = pltpu.SideEffectType.DATAFLOW_SIDE_EFFECTING


def _split_copies(buf_refs, land_refs, send_sems, recv_sems, same, to_me):
    x, y, c = lax.axis_index("x"), lax.axis_index("y"), lax.axis_index("c")
    me = 4 * x + 2 * y + c
    cps = []
    for k in range(1, N_DEV):
        px = 1 - x if k & 4 else x
        py = 1 - y if k & 2 else y
        pc = 1 - c if k & 1 else c
        p = 4 * px + 2 * py + pc
        for b in range(len(buf_refs)):
            s = b * (N_DEV - 1) + k - 1
            cps.append(pltpu.make_async_remote_copy(
                src_ref=buf_refs[b] if same else buf_refs[b].at[p], dst_ref=land_refs[b].at[me if to_me else p],
                send_sem=send_sems.at[s], recv_sem=recv_sems.at[s], device_id=(px, py, pc), device_id_type=pl.DeviceIdType.MESH))
    return cps


def _exchange_start(bufs, name, same):
    nb = len(bufs)
    ns = nb * (N_DEV - 1)
    lands = [lax.empty((N_DEV,) + tuple(b.shape if same else b.shape[1:]), b.dtype) for b in bufs]

    def body(*refs):
        buf_refs, land_refs = refs[:nb], refs[nb:2 * nb]
        send_sems, recv_sems = refs[2 * nb], refs[2 * nb + 1]
        token = refs[-1]
        for cp in _split_copies(buf_refs, land_refs, send_sems, recv_sems, same, True):
            cp.start()
        token[...] = jnp.zeros_like(token)

    hbm = lambda a: pltpu.with_memory_space_constraint(a, pltpu.HBM)
    out = pl.pallas_call(
        body, name=name,
        out_shape=(pltpu.SemaphoreType.DMA((ns,)), pltpu.SemaphoreType.DMA((ns,)),
                   *[pltpu.HBM(a.shape, a.dtype) for a in list(bufs) + lands], S((8, LANES), F32)),
        in_specs=[_HBM] * (2 * nb), out_specs=(_SEM, _SEM, *[_HBM] * (2 * nb), pl.BlockSpec(memory_space=pltpu.VMEM)),
        input_output_aliases={i: 2 + i for i in range(2 * nb)},
        compiler_params=pltpu.CompilerParams(has_side_effects=_EFFECT),
    )(*[hbm(a) for a in list(bufs) + lands])
    return out[0], out[1], list(out[2:2 + nb]), list(out[2 + nb:2 + 2 * nb]), out[-1]


def _exchange_wait(started, after, name, same):
    send_sems, recv_sems, bufs, lands, _ = started
    nb = len(bufs)

    def body(*refs):
        buf_refs, land_refs = refs[:nb], refs[nb:2 * nb]
        s_sems, r_sems = refs[2 * nb], refs[2 * nb + 1]
        for cp in _split_copies(buf_refs, land_refs, s_sems, r_sems, same, False):
            cp.wait_send()
            cp.wait_recv()

    out = pl.pallas_call(
        body, name=name,
        out_shape=tuple(pltpu.HBM(a.shape, a.dtype) for a in bufs + lands),
        in_specs=[_HBM] * (2 * nb) + [_SEM, _SEM, pl.BlockSpec(memory_space=pl.ANY)], out_specs=tuple([_HBM] * (2 * nb)),
        input_output_aliases={i: i for i in range(2 * nb)},
        compiler_params=pltpu.CompilerParams(has_side_effects=_EFFECT),
    )(*bufs, *lands, send_sems, recv_sems, after)
    return list(out[:nb]), list(out[nb:])


def _sum_slabs(recv, name):
    n, R, _ = recv.shape

    def body(r_ref, o_ref):
        g = r_ref[0].astype(F32)
        for s in range(1, n):
            g = g + r_ref[s].astype(F32)
        o_ref[...] = g

    return pl.pallas_call(body, name=name, out_shape=S((R, LANES), F32))(recv)


def _adamw(recv, w, m, v, name, tr):
    n, R, C = recv.shape
    c1 = 1.0 - ADAM_B1 ** ADAM_STEP
    c2 = 1.0 - ADAM_B2 ** ADAM_STEP

    def body(r_ref, w_ref, m_ref, v_ref, g_out, d_out, m_out, v_out):
        g = r_ref[0].astype(F32)
        for s in range(1, n):
            g = g + r_ref[s].astype(F32)
        mm = ADAM_B1 * m_ref[...] + (1.0 - ADAM_B1) * g
        vv = ADAM_B2 * v_ref[...] + (1.0 - ADAM_B2) * (g * g)
        g_out[...] = g
        m_out[...] = mm
        v_out[...] = vv
        d_out[...] = -ADAM_LR * ((mm / c1) / (jnp.sqrt(vv / c2) + ADAM_EPS) + ADAM_WD * w_ref[...])

    spec = pl.BlockSpec((tr, C), lambda i: (i, 0))
    return pl.pallas_call(
        body, name=name, grid=(R // tr,),
        in_specs=[pl.BlockSpec((n, tr, C), lambda i: (0, i, 0)), spec, spec, spec],
        out_specs=[spec] * 4, out_shape=[S((R, C), F32)] * 4,
        compiler_params=_cparams("parallel"),
    )(recv, w, m, v)


def _pack(parts, rows):
    cols = []
    for p in parts:
        f = p.reshape(-1)
        cols.append(jnp.pad(f, (0, (-f.shape[0]) % LANES)))
    flat = jnp.concatenate(cols)
    return jnp.pad(flat, (0, rows * LANES - flat.shape[0])).reshape(rows, LANES)


def _unpack(buf, shapes):
    flat = buf.reshape(-1)
    out, o = [], 0
    for shp in shapes:
        n = int(np.prod(shp))
        out.append(flat[o:o + n].reshape(shp))
        o += n + (-n) % LANES
    return out


SMALL_ROWS = 200
CONV_ROWS = 16


def kernel(x, pre_mix_norm_w, w_in, ssd_conv_w, ssd_conv_b, ssd_dt_bias, ssd_a_log, ssd_d, ssd_norm_w, ret_norm_w, w_out, post_mix_norm_w, pre_ffn_norm_w, w_up, ffn_conv_w, ffn_conv_b, w_down, post_ffn_norm_w, loss_target, m_pre_mix_norm_w, m_w_in, m_ssd_conv_w, m_ssd_conv_b, m_ssd_dt_bias, m_ssd_a_log, m_ssd_d, m_ssd_norm_w, m_ret_norm_w, m_w_out, m_post_mix_norm_w, m_pre_ffn_norm_w, m_w_up, m_ffn_conv_w, m_ffn_conv_b, m_w_down, m_post_ffn_norm_w, v_pre_mix_norm_w, v_w_in, v_ssd_conv_w, v_ssd_conv_b, v_ssd_dt_bias, v_ssd_a_log, v_ssd_d, v_ssd_norm_w, v_ret_norm_w, v_w_out, v_post_mix_norm_w, v_pre_ffn_norm_w, v_w_up, v_ffn_conv_w, v_ffn_conv_b, v_w_down, v_post_ffn_norm_w):
    T = x.shape[1]
    xi, tgt = x[0], loss_target[0]
    me = 4 * lax.axis_index("x") + 2 * lax.axis_index("y") + lax.axis_index("c")
    n_in, n_up = w_in.shape[2], w_up.shape[2]
    n_out, n_down = w_out.shape[1], w_down.shape[1]
    n_sc, n_fc = ssd_conv_w.shape[2], ffn_conv_w.shape[2]

    def after(token, value):
        return lax.optimization_barrier((token, value))[1]

    def finish(started, after_value, name, same):
        bufs, lands = _exchange_wait(started, after_value, name, same)
        own = [b if same else lax.dynamic_index_in_dim(b, me, 0, keepdims=False) for b in bufs]
        return [lax.dynamic_update_index_in_dim(l, o, me, 0) for l, o in zip(lands, own)]

    gat_in = _exchange_start([w_in[0].astype(BF)], "gather_in_start", True)
    gat_rest = _exchange_start([after(gat_in[4], w[0]).astype(BF) for w in (w_out, w_up, w_down)], "gather_rest_start", True)
    gconv, = _exchange([after(gat_rest[4], _pack([ssd_conv_w, ffn_conv_w], CONV_ROWS))], "gather_conv", True)
    g_in, = finish(gat_in, gconv, "gather_in_wait", True)
    win = jnp.transpose(g_in, (1, 0, 2)).reshape(D_MODEL, N_DEV * n_in)
    convs = [_unpack(gconv[d], [(SSD_CONV, n_sc), (FFN_CONV, n_fc)]) for d in range(N_DEV)]
    scw = jnp.pad(jnp.concatenate([c[0] for c in convs], axis=1), ((0, 8 - SSD_CONV), (0, 0)))
    fcw = jnp.pad(jnp.concatenate([c[1] for c in convs], axis=1), ((0, 8 - FFN_CONV), (0, 0)))
    wp = jnp.concatenate([win[:, O_Z:O_XBC], win[:, O_XBC:O_DT], win[:, O_Q:O_K], win[:, O_K:O_V], win[:, O_V:O_G],
                          win[:, O_G:], win[:, O_DT:O_Q], jnp.zeros((D_MODEL, P_END - P_DT - SSD_HEADS), win.dtype)], axis=1)

    pad_h = lambda p: jnp.pad(p, ((0, 0), (0, LANES - SSD_HEADS)))
    dtb, alog = pad_h(ssd_dt_bias), pad_h(ssd_a_log)
    dskx = jnp.repeat(ssd_d, SSD_HEAD_DIM, axis=1)
    inv = ROPE_BASE ** (-jnp.arange(0, RET_QK, 2, dtype=F32) / RET_QK)
    ang = jnp.arange(T, dtype=F32)[:, None] * inv[None, :]
    cs_, sn_ = jnp.cos(ang), jnp.sin(ang)
    cos = jnp.concatenate([cs_, cs_, cs_, cs_], axis=1)
    sin = jnp.concatenate([-sn_, sn_, -sn_, sn_], axis=1)

    h, z, xbc, q, k, v, g, dtr = _fwd_in(xi, pre_mix_norm_w, wp)
    ys, ypre, sst = _ssd_fwd(xbc, dtr, z, scw, ssd_conv_b, dtb, alog, dskx, ssd_norm_w)
    yr, rst = _ret_fwd(q, k, v, g, cos, sin, ret_norm_w)
    g_out, g_up, g_down = finish(gat_rest, yr, "gather_rest_wait", True)
    wout = g_out.reshape(N_DEV * n_out, D_MODEL)
    wup = jnp.transpose(g_up, (1, 0, 2)).reshape(D_MODEL, N_DEV * n_up)
    wdown = g_down.reshape(N_DEV * n_down, D_MODEL)
    y, x1, h2, graw, val = _fwd_mid(ys, yr, xi, wout, post_mix_norm_w, pre_ffn_norm_w, wup)
    a, dfb, dval, dgate, dx2, lossb, d_pff, d_fcb = _ffn_tail(graw, val, x1, tgt, fcw, ffn_conv_b, wdown, post_ffn_norm_w)
    gdown = _matmul_tn(a, dfb, "dw_down")
    sc_down = _exchange_start([gdown.reshape(N_DEV, n_down, D_MODEL).astype(BF)], "scatter_down_start", False)
    dgraw, dx1, dyb, dys, dyr, d_fcw, d_pf, d_pm = _ffn_bwd(dgate, dval, graw, x1, dx2, y, after(sc_down[4], fcw), wup,
                                                         pre_ffn_norm_w, post_mix_norm_w, wout)
    gup = jnp.concatenate([_matmul_tn(h2, dgraw, "dw_up_g"), _matmul_tn(h2, dval, "dw_up_v")], axis=1)
    gout = jnp.concatenate([_matmul_tn(ys, dyb, "dw_out_s"), _matmul_tn(yr, dyb, "dw_out_r")], axis=0)
    sc_mid = _exchange_start([jnp.transpose(gup.reshape(D_MODEL, N_DEV, n_up), (1, 0, 2)).astype(BF),
                              gout.reshape(N_DEV, n_out, D_MODEL).astype(BF)], "scatter_mid_start", False)
    dz, dxbc, ddt, d_scw, d_scb, d_dtb, d_alog, d_dsk, d_snw = _ssd_bwd(dys, ypre, xbc, dtr, z, sst, after(sc_mid[4], scw),
                                                                      ssd_conv_b, dtb, alog, dskx, ssd_norm_w)
    dq, dk, dv, dg, d_rnw = _ret_bwd(dyr, q, k, v, g, rst, cos, sin, ret_norm_w)
    gin = jnp.concatenate([_matmul_tn(h, dz, "dw_z"), _matmul_tn(h, dxbc, "dw_xbc"),
                           _matmul_tn(h, ddt, "dw_dt")[:, :SSD_HEADS], _matmul_tn(h, dq, "dw_q"), _matmul_tn(h, dk, "dw_k"),
                           _matmul_tn(h, dv, "dw_v"), _matmul_tn(h, dg, "dw_g")], axis=1)
    sc_in = _exchange_start([jnp.transpose(gin.reshape(D_MODEL, N_DEV, n_in), (1, 0, 2)).astype(BF)], "scatter_in_start", False)
    gx, d_w0 = _in_bwd(dz, dxbc, dq, dk, dv, dg, ddt, xi, dx1, after(sc_in[4], pre_mix_norm_w), wp)
    r_down, = finish(sc_down, gx, "scatter_down_wait", False)
    r_up, r_out = finish(sc_mid, r_down, "scatter_mid_wait", False)
    per_w = [None] * 4
    per_w[3] = _adamw(r_down, w_down[0], m_w_down[0], v_w_down[0], "adamw_down", n_down)
    per_w[2] = _adamw(r_up, w_up[0], m_w_up[0], v_w_up[0], "adamw_up", 256)
    per_w[1] = _adamw(r_out, w_out[0], m_w_out[0], v_w_out[0], "adamw_out", n_out)
    r_in, = finish(sc_in, per_w[1][0], "scatter_in_wait", False)
    per_w[0] = _adamw(r_in, w_in[0], m_w_in[0], v_w_in[0], "adamw_in", 256)
    big = [[per_w[i][kind][None] for i in range(4)] for kind in range(4)]

    small_full = [d_w0, d_scw[:SSD_CONV], d_scb, d_dtb[:, :SSD_HEADS], d_alog[:, :SSD_HEADS], d_dsk[:, :SSD_HEADS], d_snw, d_rnw,
                  d_pm, d_pf, d_fcw[:FFN_CONV], d_fcb, d_pff]
    full_shapes = [t.shape for t in small_full]
    gs = _sum_slabs(_exchange([_pack(small_full, SMALL_ROWS)], "gather_small", True)[0], "sum_small")
    gfull = _unpack(gs, full_shapes)
    gfull[1] = lax.dynamic_slice_in_dim(gfull[1], me * n_sc, n_sc, axis=1)
    gfull[10] = lax.dynamic_slice_in_dim(gfull[10], me * n_fc, n_fc, axis=1)
    ws = [pre_mix_norm_w, ssd_conv_w, ssd_conv_b, ssd_dt_bias, ssd_a_log, ssd_d, ssd_norm_w, ret_norm_w, post_mix_norm_w,
          pre_ffn_norm_w, ffn_conv_w, ffn_conv_b, post_ffn_norm_w]
    ms = [m_pre_mix_norm_w, m_ssd_conv_w, m_ssd_conv_b, m_ssd_dt_bias, m_ssd_a_log, m_ssd_d, m_ssd_norm_w, m_ret_norm_w,
          m_post_mix_norm_w, m_pre_ffn_norm_w, m_ffn_conv_w, m_ffn_conv_b, m_post_ffn_norm_w]
    vs = [v_pre_mix_norm_w, v_ssd_conv_w, v_ssd_conv_b, v_ssd_dt_bias, v_ssd_a_log, v_ssd_d, v_ssd_norm_w, v_ret_norm_w,
          v_post_mix_norm_w, v_pre_ffn_norm_w, v_ffn_conv_w, v_ffn_conv_b, v_post_ffn_norm_w]
    out_shapes = [t.shape for t in ws]
    small = _adamw(_pack(gfull, SMALL_ROWS)[None], _pack(ws, SMALL_ROWS), _pack(ms, SMALL_ROWS), _pack(vs, SMALL_ROWS),
                   "adamw_small", SMALL_ROWS)
    small = [_unpack(b, out_shapes) for b in small]

    order = {"pre_mix_norm_w": ("s", 0), "w_in": ("b", 0), "ssd_conv_w": ("s", 1), "ssd_conv_b": ("s", 2),
             "ssd_dt_bias": ("s", 3), "ssd_a_log": ("s", 4), "ssd_d": ("s", 5), "ssd_norm_w": ("s", 6), "ret_norm_w": ("s", 7),
             "w_out": ("b", 1), "post_mix_norm_w": ("s", 8), "pre_ffn_norm_w": ("s", 9), "w_up": ("b", 2),
             "ffn_conv_w": ("s", 10), "ffn_conv_b": ("s", 11), "w_down": ("b", 3), "post_ffn_norm_w": ("s", 12)}
    loss = lax.psum(lossb[0, 0], ("x", "y", "c"))
    outs = [loss, gx[None]]
    for kind in range(4):
        for name, (grp, idx) in order.items():
            outs.append(big[kind][idx] if grp == "b" else small[kind][idx])
    return tuple(outs)
```

```python
import functools
import math

import numpy as np
import jax
import jax.numpy as jnp
from jax import lax
from jax.experimental import pallas as pl
from jax.experimental.pallas import tpu as pltpu

F32 = jnp.float32
BF = jnp.bfloat16
HI = lax.Precision.HIGHEST
S = jax.ShapeDtypeStruct

D_MODEL = 1024
SSD_HEADS = 16
SSD_HEAD_DIM = 64
SSD_GROUPS = 2
SSD_STATE = 128
SSD_WIDTH = 1024
SSD_XBC = 1536
SSD_CONV = 4
RET_HEADS = 8
RET_QK = 64
RET_V = 128
RET_QK_W = 512
RET_V_W = 1024
ROPE_BASE = 10000.0
CH = 128
D_FF = 2816
FFN_CONV = 3
EPS = 1e-6
IN_WIDTH = 5648
N_DEV = 8

ADAM_LR = 0.001
ADAM_B1 = 0.9
ADAM_B2 = 0.999
ADAM_EPS = 1e-08
ADAM_WD = 0.01
ADAM_STEP = 10

LANES = 128
HALO = 16
VMEM_LIMIT = 48 * 1024 * 1024

P_Z, P_XBC, P_Q, P_K, P_V, P_G, P_DT, P_END = 0, 1024, 2560, 3072, 3584, 4608, 5632, 5760
O_Z, O_XBC, O_DT, O_Q, O_K, O_V, O_G = 0, 1024, 2560, 2576, 3088, 3600, 4624


def _cparams(*sem):
    return pltpu.CompilerParams(dimension_semantics=sem, vmem_limit_bytes=VMEM_LIMIT)


def _dot(a, b):
    return jnp.dot(a.astype(BF), b.astype(BF), preferred_element_type=F32)


def _dot_nt(a, b):
    return lax.dot_general(a.astype(BF), b.astype(BF), (((1,), (1,)), ((), ())), preferred_element_type=F32)


def _dot_tn(a, b):
    return lax.dot_general(a.astype(BF), b.astype(BF), (((0,), (0,)), ((), ())), preferred_element_type=F32)


def _dot_hi(a, b):
    return jnp.dot(a, b, preferred_element_type=F32, precision=HI)


def _dot_tn_hi(a, b):
    return lax.dot_general(a, b, (((0,), (0,)), ((), ())), preferred_element_type=F32, precision=HI)


def _sigmoid(x):
    return jax.nn.sigmoid(x)


def _dsilu(x, s):
    return s * (1.0 + x * (1.0 - s))


def _softplus(x):
    return jnp.maximum(x, 0.0) + jnp.log1p(jnp.exp(-jnp.abs(x)))


def _rstd(x):
    return lax.rsqrt(jnp.mean(x * x, axis=-1, keepdims=True) + EPS)


def _rms_bwd(dy, x, r, w):
    gn = dy * w
    dx = r * gn - x * (r * r * r) * jnp.mean(gn * x, axis=-1, keepdims=True)
    dw = jnp.sum(dy * x * r, axis=0, keepdims=True)
    return dx, dw


def _rows_before(ext, s, head, n):
    if s == 0:
        return ext[head:head + n]
    return pltpu.roll(ext, s, 0)[head:head + n]


def _rows_after(ext, s, n):
    if s == 0:
        return ext[0:n]
    return pltpu.roll(ext, ext.shape[0] - s, 0)[0:n]


def _row_spec(tm, width):
    return pl.BlockSpec((tm, width), lambda i: (i, 0))


def _const_spec(shape):
    return pl.BlockSpec(shape, lambda i: (0,) * len(shape))


_VMEM_WHOLE = pl.BlockSpec(memory_space=pltpu.VMEM)


def _fwd_in(x, w0, wp, tm=256):
    T = x.shape[0]

    def body(x_ref, w0_ref, wp_ref, h_ref, z_ref, xbc_ref, q_ref, k_ref, v_ref, g_ref, dt_ref):
        xf = x_ref[...]
        h = (xf * _rstd(xf) * w0_ref[...]).astype(BF)
        h_ref[...] = h
        for ref, lo, hi in ((z_ref, P_Z, P_XBC), (xbc_ref, P_XBC, P_Q), (q_ref, P_Q, P_K), (k_ref, P_K, P_V),
                            (v_ref, P_V, P_G), (g_ref, P_G, P_DT), (dt_ref, P_DT, P_END)):
            ref[...] = jnp.dot(h, wp_ref[:, lo:hi], preferred_element_type=F32).astype(ref.dtype)

    widths = (D_MODEL, SSD_WIDTH, SSD_XBC, RET_QK_W, RET_QK_W, RET_V_W, RET_V_W)
    return pl.pallas_call(
        body, name="fwd_in", grid=(T // tm,),
        in_specs=[_row_spec(tm, D_MODEL), _const_spec((1, D_MODEL)), _VMEM_WHOLE],
        out_specs=[_row_spec(tm, w) for w in widths] + [_row_spec(tm, LANES)],
        out_shape=[S((T, w), BF) for w in widths] + [S((T, LANES), F32)],
        compiler_params=_cparams("parallel"),
    )(x, w0, wp)


def _fwd_mid(ys, yr, x, wout, wpm, wpf, wup, tm=256):
    T = x.shape[0]

    def body(ys_ref, yr_ref, x_ref, wout_ref, wpm_ref, wpf_ref, wup_ref, y_ref, x1_ref, h2_ref, graw_ref, val_ref):
        y = (jnp.dot(ys_ref[...], wout_ref[0:SSD_WIDTH, :], preferred_element_type=F32)
             + jnp.dot(yr_ref[...], wout_ref[SSD_WIDTH:, :], preferred_element_type=F32))
        y_ref[...] = y
        x1 = x_ref[...] + y * _rstd(y) * wpm_ref[...]
        x1_ref[...] = x1
        h2 = (x1 * _rstd(x1) * wpf_ref[...]).astype(BF)
        h2_ref[...] = h2
        graw_ref[...] = jnp.dot(h2, wup_ref[:, 0:D_FF], preferred_element_type=F32).astype(BF)
        val_ref[...] = jnp.dot(h2, wup_ref[:, D_FF:], preferred_element_type=F32).astype(BF)

    return pl.pallas_call(
        body, name="fwd_mid", grid=(T // tm,),
        in_specs=[_row_spec(tm, SSD_WIDTH), _row_spec(tm, RET_V_W), _row_spec(tm, D_MODEL), _VMEM_WHOLE,
                  _const_spec((1, D_MODEL)), _const_spec((1, D_MODEL)), _VMEM_WHOLE],
        out_specs=[_row_spec(tm, D_MODEL), _row_spec(tm, D_MODEL), _row_spec(tm, D_MODEL), _row_spec(tm, D_FF),
                   _row_spec(tm, D_FF)],
        out_shape=[S((T, D_MODEL), F32), S((T, D_MODEL), F32), S((T, D_MODEL), BF), S((T, D_FF), BF), S((T, D_FF), BF)],
        compiler_params=_cparams("parallel"),
    )(ys, yr, x, wout, wpm, wpf, wup)


def _ffn_tail(graw, val, x1, tgt, convw, convb, wdown, wpff, tm=256):
    T = x1.shape[0]

    def body(graw_ref, val_ref, x1_ref, tgt_ref, cw_ref, cb_ref, wd_ref, wpff_ref,
             a_ref, df_ref, dval_ref, dgate_ref, dx2_ref, loss_ref, dwpff_ref, dcb_ref, carry):
        i = pl.program_id(0)

        @pl.when(i == 0)
        def _():
            carry[...] = jnp.zeros_like(carry)
            loss_ref[...] = jnp.zeros_like(loss_ref)
            dwpff_ref[...] = jnp.zeros_like(dwpff_ref)
            dcb_ref[...] = jnp.zeros_like(dcb_ref)

        g = graw_ref[...].astype(F32)
        ext = jnp.concatenate([carry[...], g], axis=0)
        carry[...] = g[tm - 8:tm]
        gate = cb_ref[...] + sum(cw_ref[j:j + 1, :] * _rows_before(ext, FFN_CONV - 1 - j, 8, tm) for j in range(FFN_CONV))
        sg = _sigmoid(gate)
        silu = gate * sg
        v = val_ref[...].astype(F32)
        a = (silu * v).astype(BF)
        a_ref[...] = a
        f = jnp.dot(a, wd_ref[...], preferred_element_type=F32)
        r = _rstd(f)
        w = wpff_ref[...]
        e = x1_ref[...] + f * r * w - tgt_ref[...]
        loss_ref[...] += jnp.sum(e * e) * (0.5 / D_MODEL)
        dx2 = e * (1.0 / D_MODEL)
        dx2_ref[...] = dx2
        df, dw = _rms_bwd(dx2, f, r, w)
        dwpff_ref[...] += dw
        dfb = df.astype(BF)
        df_ref[...] = dfb
        da = _dot_nt(dfb, wd_ref[...])
        dval_ref[...] = (da * silu).astype(BF)
        dgate = da * v * _dsilu(gate, sg)
        dcb_ref[...] += jnp.sum(dgate, axis=0, keepdims=True)
        dgate_ref[...] = dgate.astype(BF)

    return pl.pallas_call(
        body, name="ffn_tail", grid=(T // tm,),
        in_specs=[_row_spec(tm, D_FF), _row_spec(tm, D_FF), _row_spec(tm, D_MODEL), _row_spec(tm, D_MODEL),
                  _const_spec((8, D_FF)), _const_spec((1, D_FF)), _VMEM_WHOLE, _const_spec((1, D_MODEL))],
        out_specs=[_row_spec(tm, D_FF), _row_spec(tm, D_MODEL), _row_spec(tm, D_FF), _row_spec(tm, D_FF),
                   _row_spec(tm, D_MODEL), _const_spec((8, LANES)), _const_spec((1, D_MODEL)), _const_spec((1, D_FF))],
        out_shape=[S((T, D_FF), BF), S((T, D_MODEL), BF), S((T, D_FF), BF), S((T, D_FF), BF), S((T, D_MODEL), F32),
                   S((8, LANES), F32), S((1, D_MODEL), F32), S((1, D_FF), F32)],
        scratch_shapes=[pltpu.VMEM((8, D_FF), F32)],
        compiler_params=_cparams("arbitrary"),
    )(graw, val, x1, tgt, convw, convb, wdown, wpff)


def _ffn_bwd(dgate, dval, graw, x1, dx2, y, convw, wup, wpf, wpm, wout, tm=256):
    T = x1.shape[0]
    nt = T // tm
    rev = lambda i: (nt - 1 - i, 0)
    rspec = lambda w: pl.BlockSpec((tm, w), rev)

    def body(dgate_ref, dval_ref, graw_ref, x1_ref, dx2_ref, y_ref, cw_ref, wup_ref, wpf_ref, wpm_ref, wout_ref,
             dgraw_ref, dx1_ref, dy_ref, dys_ref, dyr_ref, dcw_ref, dwpf_ref, dwpm_ref, carry):
        i = pl.program_id(0)

        @pl.when(i == 0)
        def _():
            carry[...] = jnp.zeros_like(carry)
            dcw_ref[...] = jnp.zeros_like(dcw_ref)
            dwpf_ref[...] = jnp.zeros_like(dwpf_ref)
            dwpm_ref[...] = jnp.zeros_like(dwpm_ref)

        dg = dgate_ref[...].astype(F32)
        ext = jnp.concatenate([dg, carry[...]], axis=0)
        carry[...] = dg[0:8]
        g = graw_ref[...].astype(F32)
        dgraw = jnp.zeros((tm, D_FF), F32)
        for j in range(FFN_CONV):
            sj = _rows_after(ext, FFN_CONV - 1 - j, tm)
            dgraw = dgraw + cw_ref[j:j + 1, :] * sj
            dcw_ref[j:j + 1, :] += jnp.sum(sj * g, axis=0, keepdims=True)
        dgrawb = dgraw.astype(BF)
        dgraw_ref[...] = dgrawb
        dh2 = _dot_nt(dgrawb, wup_ref[:, 0:D_FF]) + _dot_nt(dval_ref[...], wup_ref[:, D_FF:])
        x1 = x1_ref[...]
        dxa, dw = _rms_bwd(dh2, x1, _rstd(x1), wpf_ref[...])
        dwpf_ref[...] += dw
        dx1 = dx2_ref[...] + dxa
        dx1_ref[...] = dx1
        yv = y_ref[...]
        dy, dw = _rms_bwd(dx1, yv, _rstd(yv), wpm_ref[...])
        dwpm_ref[...] += dw
        dyb = dy.astype(BF)
        dy_ref[...] = dyb
        dys_ref[...] = _dot_nt(dyb, wout_ref[0:SSD_WIDTH, :]).astype(BF)
        dyr_ref[...] = _dot_nt(dyb, wout_ref[SSD_WIDTH:, :]).astype(BF)

    return pl.pallas_call(
        body, name="ffn_bwd", grid=(nt,),
        in_specs=[rspec(D_FF), rspec(D_FF), rspec(D_FF), rspec(D_MODEL), rspec(D_MODEL), rspec(D_MODEL),
                  _const_spec((8, D_FF)), _VMEM_WHOLE, _const_spec((1, D_MODEL)), _const_spec((1, D_MODEL)), _VMEM_WHOLE],
        out_specs=[rspec(D_FF), rspec(D_MODEL), rspec(D_MODEL), rspec(SSD_WIDTH), rspec(RET_V_W),
                   _const_spec((8, D_FF)), _const_spec((1, D_MODEL)), _const_spec((1, D_MODEL))],
        out_shape=[S((T, D_FF), BF), S((T, D_MODEL), F32), S((T, D_MODEL), BF), S((T, SSD_WIDTH), BF), S((T, RET_V_W), BF),
                   S((8, D_FF), F32), S((1, D_MODEL), F32), S((1, D_MODEL), F32)],
        scratch_shapes=[pltpu.VMEM((8, D_FF), F32)],
        compiler_params=_cparams("arbitrary"),
    )(dgate, dval, graw, x1, dx2, y, convw, wup, wpf, wpm, wout)


def _in_bwd(dz, dxbc, dq, dk, dv, dg, ddt, x, dx1, w0, wp, tm=256):
    T = x.shape[0]

    def body(dz_ref, dxbc_ref, dq_ref, dk_ref, dv_ref, dg_ref, ddt_ref, x_ref, dx1_ref, w0_ref, wp_ref, gx_ref, dw0_ref):
        @pl.when(pl.program_id(0) == 0)
        def _():
            dw0_ref[...] = jnp.zeros_like(dw0_ref)

        dh = jnp.zeros((tm, D_MODEL), F32)
        for ref, lo, hi in ((dz_ref, P_Z, P_XBC), (dxbc_ref, P_XBC, P_Q), (dq_ref, P_Q, P_K), (dk_ref, P_K, P_V),
                            (dv_ref, P_V, P_G), (dg_ref, P_G, P_DT), (ddt_ref, P_DT, P_END)):
            dh = dh + _dot_nt(ref[...], wp_ref[:, lo:hi])
        xf = x_ref[...]
        dx, dw = _rms_bwd(dh, xf, _rstd(xf), w0_ref[...])
        dw0_ref[...] += dw
        gx_ref[...] = dx1_ref[...] + dx

    widths = (SSD_WIDTH, SSD_XBC, RET_QK_W, RET_QK_W, RET_V_W, RET_V_W, LANES)
    return pl.pallas_call(
        body, name="in_bwd", grid=(T // tm,),
        in_specs=[_row_spec(tm, w) for w in widths] + [_row_spec(tm, D_MODEL), _row_spec(tm, D_MODEL),
                                                       _const_spec((1, D_MODEL)), _VMEM_WHOLE],
        out_specs=[_row_spec(tm, D_MODEL), _const_spec((1, D_MODEL))],
        out_shape=[S((T, D_MODEL), F32), S((1, D_MODEL), F32)],
        compiler_params=_cparams("arbitrary"),
    )(dz, dxbc, dq, dk, dv, dg, ddt, x, dx1, w0, wp)


def _matmul_tn(a, b, name, tk=512):
    T, M = a.shape
    N = b.shape[1]
    tn = N
    while M * tn * 4 > (4 << 20) and tn % 256 == 0:
        tn //= 2
    nk = T // tk

    def body(a_ref, b_ref, o_ref):
        @pl.when(pl.program_id(1) == 0)
        def _():
            o_ref[...] = jnp.zeros_like(o_ref)

        o_ref[...] += _dot_tn(a_ref[...], b_ref[...])

    return pl.pallas_call(
        body, name=name, grid=(N // tn, nk),
        in_specs=[pl.BlockSpec((tk, M), lambda n, k: (k, 0)), pl.BlockSpec((tk, tn), lambda n, k: (k, n))],
        out_specs=pl.BlockSpec((M, tn), lambda n, k: (0, n)),
        out_shape=S((M, N), F32),
        compiler_params=_cparams("parallel", "arbitrary"),
    )(a, b)


def _tri(lower):
    r = lax.broadcasted_iota(jnp.int32, (CH, CH), 0)
    c = lax.broadcasted_iota(jnp.int32, (CH, CH), 1)
    return ((c <= r) if lower else (r <= c)).astype(F32)


def _onehot_row(h):
    return (lax.broadcasted_iota(jnp.int32, (1, LANES), 1) == h).astype(F32)


def _onehot_col(h):
    return (lax.broadcasted_iota(jnp.int32, (LANES, 1), 0) == h).astype(F32)


def _ssd_pre(xc_ref, xh_ref, dtr_ref, cw_ref, cb_ref, dtb_ref, alog_ref, first):
    xc = xc_ref[...].astype(F32)
    xh = jnp.where(first, 0.0, xh_ref[...].astype(F32))
    ext = jnp.concatenate([xh, xc], axis=0)
    u = cb_ref[...] + sum(cw_ref[j:j + 1, :] * _rows_before(ext, SSD_CONV - 1 - j, HALO, CH) for j in range(SSD_CONV))
    sg = _sigmoid(u)
    act = u * sg
    dt = _softplus(dtr_ref[...] + dtb_ref[...])
    a = -jnp.exp(alog_ref[...])
    da = dt * a
    cs = _dot_hi(_tri(True), da)
    cst = _dot_tn_hi(da, _tri(False))
    return xc, u, sg, act, dt, a, cs, cst


HPG = SSD_HEADS // SSD_GROUPS
GW = HPG * SSD_HEAD_DIM


def _expand_heads(src, buf):
    for h in range(SSD_HEADS):
        buf[:, h * SSD_HEAD_DIM:(h + 1) * SSD_HEAD_DIM] = jnp.broadcast_to(src[:, h:h + 1], (CH, SSD_HEAD_DIM))


def _ssd_expanded(act, dt, cs, dtx, csx):
    _expand_heads(dt, dtx)
    _expand_heads(cs, csx)
    csv = csx[...]
    last = csv[CH - 1:CH, :]
    e_exp = jnp.exp(csv)
    dec_exp = jnp.exp(last - csv)
    el_exp = jnp.exp(last)
    xs = act[:, 0:SSD_WIDTH]
    xdt = xs * dtx[...]
    return xs, xdt, xdt * dec_exp, e_exp, dec_exp, el_exp


def _decay_mats(h, cs, cst, transposed):
    r = lax.broadcasted_iota(jnp.int32, (CH, CH), 0)
    c = lax.broadcasted_iota(jnp.int32, (CH, CH), 1)
    c_col = cs[:, h:h + 1]
    c_row = cst[h:h + 1, :]
    if transposed:
        return jnp.exp(jnp.where(r <= c, c_row - c_col, -1e30))
    return jnp.exp(jnp.where(r >= c, c_col - c_row, -1e30))


def _ssd_specs(T):
    nc = T // CH
    return nc, [
        _row_spec(CH, SSD_XBC),
        pl.BlockSpec((HALO, SSD_XBC), lambda i: (jnp.maximum(i * (CH // HALO) - 1, 0), 0)),
        _row_spec(CH, LANES),
        _row_spec(CH, SSD_WIDTH),
    ]


def _groups(act):
    bm = [act[:, SSD_WIDTH + g * SSD_STATE:SSD_WIDTH + (g + 1) * SSD_STATE] for g in range(SSD_GROUPS)]
    o = SSD_WIDTH + SSD_GROUPS * SSD_STATE
    cm = [act[:, o + g * SSD_STATE:o + (g + 1) * SSD_STATE] for g in range(SSD_GROUPS)]
    return bm, cm


def _ssd_fwd(xbc, dtr, z, convw, convb, dtb, alog, dskx, nw):
    T = xbc.shape[0]
    nc, specs = _ssd_specs(T)

    def body(xc_ref, xh_ref, dtr_ref, z_ref, cw_ref, cb_ref, dtb_ref, alog_ref, dskx_ref, nw_ref,
             out_ref, y_ref, st_ref, state, ybuf, dtx, csx):
        i = pl.program_id(0)

        @pl.when(i == 0)
        def _():
            state[...] = jnp.zeros_like(state)

        xc, u, sg, act, dt, a, cs, cst = _ssd_pre(xc_ref, xh_ref, dtr_ref, cw_ref, cb_ref, dtb_ref, alog_ref, i == 0)
        xs, xdt, w, e_exp, dec_exp, el_exp = _ssd_expanded(act, dt, cs, dtx, csx)
        bm, cm = _groups(act)
        for g in range(SSD_GROUPS):
            gs = slice(g * GW, (g + 1) * GW)
            st = state[g]
            st_ref[0, g] = st
            cb = _dot_nt(cm[g], bm[g])
            ybuf[:, gs] = _dot(cm[g], st) * e_exp[:, gs] + xs[:, gs] * dskx_ref[:, gs]
            state[g] = st * el_exp[:, gs] + _dot_tn(bm[g], w[:, gs])
            for h in range(g * HPG, (g + 1) * HPG):
                sl = slice(h * SSD_HEAD_DIM, (h + 1) * SSD_HEAD_DIM)
                ybuf[:, sl] += _dot(cb * _decay_mats(h, cs, cst, False), xdt[:, sl])
        yv = ybuf[...]
        y_ref[...] = yv.astype(BF)
        zf = z_ref[...].astype(F32)
        gated = yv * (zf * _sigmoid(zf))
        out_ref[...] = (gated * _rstd(gated) * nw_ref[...]).astype(BF)

    st_spec = pl.BlockSpec((1, SSD_GROUPS, SSD_STATE, GW), lambda i: (i, 0, 0, 0))
    return pl.pallas_call(
        body, name="ssd_fwd", grid=(nc,),
        in_specs=specs + [_const_spec((8, SSD_XBC)), _const_spec((1, SSD_XBC)), _const_spec((1, LANES)),
                          _const_spec((1, LANES)), _const_spec((1, SSD_WIDTH)), _const_spec((1, SSD_WIDTH))],
        out_specs=[_row_spec(CH, SSD_WIDTH), _row_spec(CH, SSD_WIDTH), st_spec],
        out_shape=[S((T, SSD_WIDTH), BF), S((T, SSD_WIDTH), BF), S((nc, SSD_GROUPS, SSD_STATE, GW), F32)],
        scratch_shapes=[pltpu.VMEM((SSD_GROUPS, SSD_STATE, GW), F32), pltpu.VMEM((CH, SSD_WIDTH), F32),
                        pltpu.VMEM((CH, SSD_WIDTH), F32), pltpu.VMEM((CH, SSD_WIDTH), F32)],
        compiler_params=_cparams("arbitrary"),
    )(xbc, xbc, dtr, z, convw, convb, dtb, alog, dskx, nw)


def _ssd_bwd(dout, y, xbc, dtr, z, states, convw, convb, dtb, alog, dskx, nw):
    T = xbc.shape[0]
    nc = T // CH
    rev = lambda i: (nc - 1 - i, 0)
    rspec = lambda w: pl.BlockSpec((CH, w), rev)
    halo_spec = pl.BlockSpec((HALO, SSD_XBC), lambda i: (jnp.maximum((nc - 1 - i) * (CH // HALO) - 1, 0), 0))
    NB = SSD_WIDTH
    NC_ = SSD_WIDTH + SSD_GROUPS * SSD_STATE

    def body(do_ref, y_ref, xc_ref, xh_ref, dtr_ref, z_ref, st_ref, cw_ref, cb_ref, dtb_ref, alog_ref, dskx_ref, nw_ref,
             dz_ref, dxbc_ref, ddt_ref, dcw_ref, dcb_ref, ddtb_ref, dalog_ref, ddsk_ref, dnw_ref,
             dstate, ducarry, dtx, csx, dxdtbuf, dact):
        i = pl.program_id(0)

        @pl.when(i == 0)
        def _():
            dstate[...] = jnp.zeros_like(dstate)
            ducarry[...] = jnp.zeros_like(ducarry)
            for ref in (dcw_ref, dcb_ref, ddtb_ref, dalog_ref, ddsk_ref, dnw_ref):
                ref[...] = jnp.zeros_like(ref)

        xc, u, sg, act, dt, a, cs, cst = _ssd_pre(xc_ref, xh_ref, dtr_ref, cw_ref, cb_ref, dtb_ref, alog_ref, i == nc - 1)
        xs, xdt, w, e_exp, dec_exp, el_exp = _ssd_expanded(act, dt, cs, dtx, csx)
        bm, cm = _groups(act)
        yv = y_ref[...].astype(F32)
        zf = z_ref[...].astype(F32)
        sz = _sigmoid(zf)
        gated = yv * (zf * sz)
        dgated, dnw = _rms_bwd(do_ref[...].astype(F32), gated, _rstd(gated), nw_ref[...])
        dnw_ref[...] += dnw
        dz_ref[...] = (dgated * yv * _dsilu(zf, sz)).astype(BF)
        dy = dgated * (zf * sz)
        lane_of = lax.broadcasted_iota(jnp.int32, (SSD_WIDTH, LANES), 0) - SSD_HEAD_DIM * lax.broadcasted_iota(jnp.int32, (SSD_WIDTH, LANES), 1)
        expt = ((lane_of >= 0) & (lane_of < SSD_HEAD_DIM)).astype(F32)
        ddsk_ref[...] += _dot_hi(jnp.sum(dy * xs, axis=0, keepdims=True), expt)
        dcs = jnp.zeros((CH, LANES), F32)
        dcst = jnp.zeros((LANES, CH), F32)
        ddt = jnp.zeros((CH, LANES), F32)
        lastrows = []
        for g in range(SSD_GROUPS):
            gs = slice(g * GW, (g + 1) * GW)
            st = st_ref[0, g]
            dsn = dstate[g]
            cbm = _dot_nt(cm[g], bm[g])
            cbt = _dot_nt(bm[g], cm[g])
            dy_g = dy[:, gs]
            yoff = _dot(cm[g], st) * e_exp[:, gs]
            dq = dy_g * e_exp[:, gs]
            dcm_g = _dot_nt(dq, st)
            dstate[g] = _dot_tn(cm[g], dq) + dsn * el_exp[:, gs]
            dw = _dot(bm[g], dsn)
            w_g = w[:, gs]
            dbm_g = _dot_nt(w_g, dsn)
            dww = dw * w_g
            red = dy_g * yoff - dww
            lastrows.append(jnp.sum(dsn * st, axis=0, keepdims=True) * el_exp[:, gs] + jnp.sum(dww, axis=0, keepdims=True))
            dxdtbuf[:, gs] = dw * dec_exp[:, gs]
            dcb = jnp.zeros((CH, CH), F32)
            for h in range(g * HPG, (g + 1) * HPG):
                sl = slice(h * SSD_HEAD_DIM, (h + 1) * SSD_HEAD_DIM)
                rl = slice((h - g * HPG) * SSD_HEAD_DIM, (h - g * HPG + 1) * SSD_HEAD_DIM)
                oh = _onehot_row(h)
                lmat = _decay_mats(h, cs, cst, False)
                mmat = cbm * lmat
                dy_h = dy[:, sl]
                dm = _dot_nt(dy_h, xdt[:, sl])
                dxdt_h = dxdtbuf[:, sl] + _dot(cbt * _decay_mats(h, cs, cst, True), dy_h)
                dxdtbuf[:, sl] = dxdt_h
                dseg = dm * mmat
                dcb = dcb + dm * lmat
                col = jnp.sum(dseg, axis=1, keepdims=True) + jnp.sum(red[:, rl], axis=1, keepdims=True)
                dcs = dcs + col * oh
                dcst = dcst - _onehot_col(h) * jnp.sum(dseg, axis=0, keepdims=True)
                ddt = ddt + jnp.sum(dxdt_h * xs[:, sl], axis=1, keepdims=True) * oh
            dact[:, NB + g * SSD_STATE:NB + (g + 1) * SSD_STATE] = dbm_g + _dot_tn(dcb, cm[g])
            dact[:, NC_ + g * SSD_STATE:NC_ + (g + 1) * SSD_STATE] = dcm_g + _dot(dcb, bm[g])
        dact[:, 0:SSD_WIDTH] = dy * dskx_ref[...] + dxdtbuf[...] * dtx[...]
        dlast = _dot_hi(jnp.concatenate(lastrows, axis=1), expt)
        rows = lax.broadcasted_iota(jnp.int32, (CH, LANES), 0)
        dcs = dcs + _dot_tn_hi(dcst, jnp.eye(LANES, dtype=F32)) + jnp.where(rows == CH - 1, dlast, 0.0)
        dda = _dot_hi(_tri(False), dcs)
        dalog_ref[...] += jnp.sum(dda * dt, axis=0, keepdims=True) * a
        ddt = ddt + dda * a
        ddtr = ddt * _sigmoid(dtr_ref[...] + dtb_ref[...])
        ddtb_ref[...] += jnp.sum(ddtr, axis=0, keepdims=True)
        ddt_ref[...] = ddtr.astype(BF)
        du = dact[...] * _dsilu(u, sg)
        dcb_ref[...] += jnp.sum(du, axis=0, keepdims=True)
        ext = jnp.concatenate([du, ducarry[...]], axis=0)
        ducarry[...] = du[0:8]
        dx = jnp.zeros((CH, SSD_XBC), F32)
        for j in range(SSD_CONV):
            sj = _rows_after(ext, SSD_CONV - 1 - j, CH)
            dx = dx + cw_ref[j:j + 1, :] * sj
            dcw_ref[j:j + 1, :] += jnp.sum(sj * xc, axis=0, keepdims=True)
        dxbc_ref[...] = dx.astype(BF)

    return pl.pallas_call(
        body, name="ssd_bwd", grid=(nc,),
        in_specs=[rspec(SSD_WIDTH), rspec(SSD_WIDTH), rspec(SSD_XBC), halo_spec, rspec(LANES), rspec(SSD_WIDTH),
                  pl.BlockSpec((1, SSD_GROUPS, SSD_STATE, GW), lambda i: (nc - 1 - i, 0, 0, 0)),
                  _const_spec((8, SSD_XBC)), _const_spec((1, SSD_XBC)), _const_spec((1, LANES)),
                  _const_spec((1, LANES)), _const_spec((1, SSD_WIDTH)), _const_spec((1, SSD_WIDTH))],
        out_specs=[rspec(SSD_WIDTH), rspec(SSD_XBC), rspec(LANES),
                   _const_spec((8, SSD_XBC)), _const_spec((1, SSD_XBC)), _const_spec((1, LANES)),
                   _const_spec((1, LANES)), _const_spec((1, LANES)), _const_spec((1, SSD_WIDTH))],
        out_shape=[S((T, SSD_WIDTH), BF), S((T, SSD_XBC), BF), S((T, LANES), BF),
                   S((8, SSD_XBC), F32), S((1, SSD_XBC), F32), S((1, LANES), F32),
                   S((1, LANES), F32), S((1, LANES), F32), S((1, SSD_WIDTH), F32)],
        scratch_shapes=[pltpu.VMEM((SSD_GROUPS, SSD_STATE, GW), F32), pltpu.VMEM((8, SSD_XBC), F32),
                        pltpu.VMEM((CH, SSD_WIDTH), F32), pltpu.VMEM((CH, SSD_WIDTH), F32),
                        pltpu.VMEM((CH, SSD_WIDTH), F32), pltpu.VMEM((CH, SSD_XBC), F32)],
        compiler_params=_cparams("arbitrary"),
    )(dout, y, xbc, xbc, dtr, z, states, convw, convb, dtb, alog, dskx, nw)


def _log_gamma(h):
    return float(np.log1p(-np.exp2(np.float32(-5.0 - h)), dtype=np.float32))


def _swap_halves(t):
    n = t.shape[1]
    lane = lax.broadcasted_iota(jnp.int32, t.shape, 1)
    return jnp.where((lane & (RET_QK - 1)) < RET_QK // 2, pltpu.roll(t, n - RET_QK // 2, 1), pltpu.roll(t, RET_QK // 2, 1))


def _rot(t, cos, sin):
    return t * cos + _swap_halves(t) * sin


def _rot_t(d, cos, sin):
    return d * cos + _swap_halves(d * sin)


def _ret_consts(h):
    lg = _log_gamma(h)
    r = lax.broadcasted_iota(jnp.int32, (CH, CH), 0)
    c = lax.broadcasted_iota(jnp.int32, (CH, CH), 1)
    rel = (r - c).astype(F32)
    dmask = jnp.where(rel >= 0, jnp.exp(lg * jnp.maximum(rel, 0.0)), 0.0)
    dmask_t = jnp.where(rel <= 0, jnp.exp(lg * jnp.maximum(-rel, 0.0)), 0.0)
    pos = lax.broadcasted_iota(jnp.int32, (CH, 1), 0).astype(F32)
    kdec = jnp.exp(lg * (CH - 1.0 - pos))
    qdec = jnp.exp(lg * (pos + 1.0))
    pos_row = lax.broadcasted_iota(jnp.int32, (1, CH), 1).astype(F32)
    kdec_row = jnp.exp(lg * (CH - 1.0 - pos_row))
    qdec_row = jnp.exp(lg * (pos_row + 1.0))
    return dmask, dmask_t, kdec, qdec, kdec_row, qdec_row, math.exp(lg * CH)


def _ret_fwd(q, k, v, g, cos, sin, nw):
    T = q.shape[0]
    nc = T // CH

    def body(q_ref, k_ref, v_ref, g_ref, cos_ref, sin_ref, nw_ref, out_ref, st_ref, state):
        i = pl.program_id(0)

        @pl.when(i == 0)
        def _():
            state[...] = jnp.zeros_like(state)

        cosf = jnp.tile(cos_ref[...], (1, RET_QK_W // LANES))
        sinf = jnp.tile(sin_ref[...], (1, RET_QK_W // LANES))
        qr = _rot(q_ref[...].astype(F32), cosf, sinf)
        kr = _rot(k_ref[...].astype(F32), cosf, sinf) * (RET_QK ** -0.5)
        krt = kr.T
        st_ref[0] = state[...]
        o_all = []
        for h in range(RET_HEADS):
            dmask, dmask_t, kdec, qdec, kdec_row, qdec_row, gam = _ret_consts(h)
            qs = slice(h * RET_QK, (h + 1) * RET_QK)
            v_h = v_ref[:, h * RET_V:(h + 1) * RET_V]
            rprev = state[h]
            scores = _dot_nt(qr[:, qs], kr[:, qs]) * dmask
            o_all.append(_dot(scores, v_h) + _dot(qr[:, qs] * qdec, rprev))
            state[h] = rprev * gam + _dot(krt[qs, :] * kdec_row, v_h)
        for h in range(RET_HEADS):
            sl = slice(h * RET_V, (h + 1) * RET_V)
            o = o_all[h]
            gf = g_ref[:, sl].astype(F32)
            out_ref[:, sl] = (o * _rstd(o) * nw_ref[:, sl] * (gf * _sigmoid(gf))).astype(BF)

    return pl.pallas_call(
        body, name="ret_fwd", grid=(nc,),
        in_specs=[_row_spec(CH, RET_QK_W), _row_spec(CH, RET_QK_W), _row_spec(CH, RET_V_W), _row_spec(CH, RET_V_W),
                  _row_spec(CH, LANES), _row_spec(CH, LANES), _const_spec((1, RET_V_W))],
        out_specs=[_row_spec(CH, RET_V_W), pl.BlockSpec((1, RET_HEADS, RET_QK, RET_V), lambda i: (i, 0, 0, 0))],
        out_shape=[S((T, RET_V_W), BF), S((nc, RET_HEADS, RET_QK, RET_V), F32)],
        scratch_shapes=[pltpu.VMEM((RET_HEADS, RET_QK, RET_V), F32)],
        compiler_params=_cparams("arbitrary"),
    )(q, k, v, g, cos, sin, nw)


def _ret_bwd(dout, q, k, v, g, states, cos, sin, nw):
    T = q.shape[0]
    nc = T // CH
    rev = lambda i: (nc - 1 - i, 0)
    rspec = lambda w: pl.BlockSpec((CH, w), rev)

    def body(do_ref, q_ref, k_ref, v_ref, g_ref, st_ref, cos_ref, sin_ref, nw_ref,
             dq_ref, dk_ref, dv_ref, dg_ref, dnw_ref, dstate, dqbuf, dkbuf):
        i = pl.program_id(0)

        @pl.when(i == 0)
        def _():
            dstate[...] = jnp.zeros_like(dstate)
            dnw_ref[...] = jnp.zeros_like(dnw_ref)

        cosf = jnp.tile(cos_ref[...], (1, RET_QK_W // LANES))
        sinf = jnp.tile(sin_ref[...], (1, RET_QK_W // LANES))
        qr = _rot(q_ref[...].astype(F32), cosf, sinf)
        kr = _rot(k_ref[...].astype(F32), cosf, sinf) * (RET_QK ** -0.5)
        qrt = qr.T
        heads = range(RET_HEADS)
        qsl = [slice(h * RET_QK, (h + 1) * RET_QK) for h in heads]
        vsl = [slice(h * RET_V, (h + 1) * RET_V) for h in heads]
        scores_t, o_all, do_all = [], [], []
        for h in heads:
            dmask, dmask_t, kdec, qdec, kdec_row, qdec_row, gam = _ret_consts(h)
            q_h, k_h = qr[:, qsl[h]], kr[:, qsl[h]]
            scores = _dot_nt(q_h, k_h) * dmask
            scores_t.append(_dot_nt(k_h, q_h) * dmask_t)
            o_all.append(_dot(scores, v_ref[:, vsl[h]]) + _dot(q_h * qdec, st_ref[0, h]))
        for h in heads:
            o = o_all[h]
            rr = _rstd(o)
            of = o * rr
            gf = g_ref[:, vsl[h]].astype(F32)
            sgg = _sigmoid(gf)
            d_h = do_ref[:, vsl[h]].astype(F32)
            nw_h = nw_ref[:, vsl[h]]
            dg_ref[:, vsl[h]] = (d_h * of * nw_h * _dsilu(gf, sgg)).astype(BF)
            dt_ = d_h * (gf * sgg)
            dnw_ref[:, vsl[h]] += jnp.sum(dt_ * of, axis=0, keepdims=True)
            dof = dt_ * nw_h
            do_all.append(rr * dof - o * (rr * rr * rr) * jnp.mean(dof * o, axis=-1, keepdims=True))
        for h in heads:
            dmask, dmask_t, kdec, qdec, kdec_row, qdec_row, gam = _ret_consts(h)
            q_h, k_h, v_h, do = qr[:, qsl[h]], kr[:, qsl[h]], v_ref[:, vsl[h]], do_all[h]
            gnext = dstate[h]
            dsc = _dot_nt(do, v_h) * dmask
            dsc_t = _dot_nt(v_h, do) * dmask_t
            dv_ref[:, vsl[h]] = (_dot(scores_t[h], do) + _dot(k_h * kdec, gnext)).astype(BF)
            dqbuf[:, qsl[h]] = _dot(dsc, k_h) + _dot_nt(do, st_ref[0, h]) * qdec
            dkbuf[:, qsl[h]] = _dot(dsc_t, q_h) + _dot_nt(v_h, gnext) * kdec
            dstate[h] = gnext * gam + _dot(qrt[qsl[h], :] * qdec_row, do)
        dq_ref[...] = _rot_t(dqbuf[...], cosf, sinf).astype(BF)
        dk_ref[...] = (_rot_t(dkbuf[...], cosf, sinf) * (RET_QK ** -0.5)).astype(BF)

    return pl.pallas_call(
        body, name="ret_bwd", grid=(nc,),
        in_specs=[rspec(RET_V_W), rspec(RET_QK_W), rspec(RET_QK_W), rspec(RET_V_W), rspec(RET_V_W),
                  pl.BlockSpec((1, RET_HEADS, RET_QK, RET_V), lambda i: (nc - 1 - i, 0, 0, 0)),
                  rspec(LANES), rspec(LANES), _const_spec((1, RET_V_W))],
        out_specs=[rspec(RET_QK_W), rspec(RET_QK_W), rspec(RET_V_W), rspec(RET_V_W), _const_spec((1, RET_V_W))],
        out_shape=[S((T, RET_QK_W), BF), S((T, RET_QK_W), BF), S((T, RET_V_W), BF), S((T, RET_V_W), BF),
                   S((1, RET_V_W), F32)],
        scratch_shapes=[pltpu.VMEM((RET_HEADS, RET_QK, RET_V), F32), pltpu.VMEM((CH, RET_QK_W), F32),
                        pltpu.VMEM((CH, RET_QK_W), F32)],
        compiler_params=_cparams("arbitrary"),
    )(dout, q, k, v, g, states, cos, sin, nw)


def _exchange(bufs, name, same):
    nb = len(bufs)
    slabs = [tuple(b.shape if same else b.shape[1:]) for b in bufs]

    def body(*refs):
        buf_refs, out_refs = refs[:nb], refs[nb:2 * nb]
        send_sems, recv_sems, local_sems = refs[2 * nb:]
        x, y, c = lax.axis_index("x"), lax.axis_index("y"), lax.axis_index("c")
        me = 4 * x + 2 * y + c

        def src(b, d):
            return buf_refs[b] if same else buf_refs[b].at[d]

        def remote(b, k, to_me):
            px = 1 - x if k & 4 else x
            py = 1 - y if k & 2 else y
            pc = 1 - c if k & 1 else c
            p = 4 * px + 2 * py + pc
            s = b * (N_DEV - 1) + k - 1
            return pltpu.make_async_remote_copy(
                src_ref=src(b, p), dst_ref=out_refs[b].at[me if to_me else p], send_sem=send_sems.at[s],
                recv_sem=recv_sems.at[s], device_id=(px, py, pc), device_id_type=pl.DeviceIdType.MESH)

        local = [pltpu.make_async_copy(src(b, me), out_refs[b].at[me], local_sems.at[b]) for b in range(nb)]
        for cp in local:
            cp.start()
        sends = [remote(b, k, True) for k in range(1, N_DEV) for b in range(nb)]
        for cp in sends:
            cp.start()
        for k in range(1, N_DEV):
            for b in range(nb):
                remote(b, k, False).wait_recv()
        for cp in sends:
            cp.wait_send()
        for cp in local:
            cp.wait()

    any_spec = pl.BlockSpec(memory_space=pl.ANY)
    return pl.pallas_call(
        body, name=name,
        in_specs=[any_spec] * nb, out_specs=[any_spec] * nb,
        out_shape=[S((N_DEV,) + s, b.dtype) for s, b in zip(slabs, bufs)],
        scratch_shapes=[pltpu.SemaphoreType.DMA((nb * (N_DEV - 1),)), pltpu.SemaphoreType.DMA((nb * (N_DEV - 1),)),
                        pltpu.SemaphoreType.DMA((nb,))],
    )(*bufs)


_HBM = pl.BlockSpec(memory_space=pltpu.HBM)
_SEM = pl.BlockSpec(memory_space=pltpu.SEMAPHORE)
_EFFECT = pltpu.SideEffectType.DATAFLOW_SIDE_EFFECTING


def _split_copies(buf_refs, land_refs, send_sems, recv_sems, same, to_me):
    x, y, c = lax.axis_index("x"), lax.axis_index("y"), lax.axis_index("c")
    me = 4 * x + 2 * y + c
    cps = []
    for k in range(1, N_DEV):
        px = 1 - x if k & 4 else x
        py = 1 - y if k & 2 else y
        pc = 1 - c if k & 1 else c
        p = 4 * px + 2 * py + pc
        for b in range(len(buf_refs)):
            s = b * (N_DEV - 1) + k - 1
            cps.append(pltpu.make_async_remote_copy(
                src_ref=buf_refs[b] if same else buf_refs[b].at[p], dst_ref=land_refs[b].at[me if to_me else p],
                send_sem=send_sems.at[s], recv_sem=recv_sems.at[s], device_id=(px, py, pc), device_id_type=pl.DeviceIdType.MESH))
    return cps


def _exchange_start(bufs, name, same):
    nb = len(bufs)
    ns = nb * (N_DEV - 1)
    lands = [lax.empty((N_DEV,) + tuple(b.shape if same else b.shape[1:]), b.dtype) for b in bufs]

    def body(*refs):
        buf_refs, land_refs = refs[:nb], refs[nb:2 * nb]
        send_sems, recv_sems = refs[2 * nb], refs[2 * nb + 1]
        token = refs[-1]
        for cp in _split_copies(buf_refs, land_refs, send_sems, recv_sems, same, True):
            cp.start()
        token[...] = jnp.zeros_like(token)

    hbm = lambda a: pltpu.with_memory_space_constraint(a, pltpu.HBM)
    out = pl.pallas_call(
        body, name=name,
        out_shape=(pltpu.SemaphoreType.DMA((ns,)), pltpu.SemaphoreType.DMA((ns,)),
                   *[pltpu.HBM(a.shape, a.dtype) for a in list(bufs) + lands], S((8, LANES), F32)),
        in_specs=[_HBM] * (2 * nb), out_specs=(_SEM, _SEM, *[_HBM] * (2 * nb), pl.BlockSpec(memory_space=pltpu.VMEM)),
        input_output_aliases={i: 2 + i for i in range(2 * nb)},
        compiler_params=pltpu.CompilerParams(has_side_effects=_EFFECT),
    )(*[hbm(a) for a in list(bufs) + lands])
    return out[0], out[1], list(out[2:2 + nb]), list(out[2 + nb:2 + 2 * nb]), out[-1]


def _exchange_wait(started, after, name, same):
    send_sems, recv_sems, bufs, lands, _ = started
    nb = len(bufs)
    after = list(after) if isinstance(after, (list, tuple)) else [after]

    def body(*refs):
        buf_refs, land_refs = refs[:nb], refs[nb:2 * nb]
        s_sems, r_sems = refs[2 * nb], refs[2 * nb + 1]
        for cp in _split_copies(buf_refs, land_refs, s_sems, r_sems, same, False):
            cp.wait_send()
            cp.wait_recv()

    out = pl.pallas_call(
        body, name=name,
        out_shape=tuple(pltpu.HBM(a.shape, a.dtype) for a in bufs + lands),
        in_specs=[_HBM] * (2 * nb) + [_SEM, _SEM] + [pl.BlockSpec(memory_space=pl.ANY)] * len(after),
        out_specs=tuple([_HBM] * (2 * nb)),
        input_output_aliases={i: i for i in range(2 * nb)},
        compiler_params=pltpu.CompilerParams(has_side_effects=_EFFECT),
    )(*bufs, *lands, send_sems, recv_sems, *after)
    return list(out[:nb]), list(out[nb:])


def _sum_slabs(recv, name):
    n, R, _ = recv.shape

    def body(r_ref, o_ref):
        g = r_ref[0].astype(F32)
        for s in range(1, n):
            g = g + r_ref[s].astype(F32)
        o_ref[...] = g

    return pl.pallas_call(body, name=name, out_shape=S((R, LANES), F32))(recv)


def _adamw(recv, w, m, v, name, tr):
    n, R, C = recv.shape
    c1 = 1.0 - ADAM_B1 ** ADAM_STEP
    c2 = 1.0 - ADAM_B2 ** ADAM_STEP

    def body(r_ref, w_ref, m_ref, v_ref, g_out, d_out, m_out, v_out):
        g = r_ref[0].astype(F32)
        for s in range(1, n):
            g = g + r_ref[s].astype(F32)
        mm = ADAM_B1 * m_ref[...] + (1.0 - ADAM_B1) * g
        vv = ADAM_B2 * v_ref[...] + (1.0 - ADAM_B2) * (g * g)
        g_out[...] = g
        m_out[...] = mm
        v_out[...] = vv
        d_out[...] = -ADAM_LR * ((mm / c1) / (jnp.sqrt(vv / c2) + ADAM_EPS) + ADAM_WD * w_ref[...])

    spec = pl.BlockSpec((tr, C), lambda i: (i, 0))
    return pl.pallas_call(
        body, name=name, grid=(R // tr,),
        in_specs=[pl.BlockSpec((n, tr, C), lambda i: (0, i, 0)), spec, spec, spec],
        out_specs=[spec] * 4, out_shape=[S((R, C), F32)] * 4,
        compiler_params=_cparams("parallel"),
    )(recv, w, m, v)


def _pack(parts, rows):
    cols = []
    for p in parts:
        f = p.reshape(-1)
        cols.append(jnp.pad(f, (0, (-f.shape[0]) % LANES)))
    flat = jnp.concatenate(cols)
    return jnp.pad(flat, (0, rows * LANES - flat.shape[0])).reshape(rows, LANES)


def _unpack(buf, shapes):
    flat = buf.reshape(-1)
    out, o = [], 0
    for shp in shapes:
        n = int(np.prod(shp))
        out.append(flat[o:o + n].reshape(shp))
        o += n + (-n) % LANES
    return out


SMALL_ROWS = 200
CONV_ROWS = 16


def kernel(x, pre_mix_norm_w, w_in, ssd_conv_w, ssd_conv_b, ssd_dt_bias, ssd_a_log, ssd_d, ssd_norm_w, ret_norm_w, w_out, post_mix_norm_w, pre_ffn_norm_w, w_up, ffn_conv_w, ffn_conv_b, w_down, post_ffn_norm_w, loss_target, m_pre_mix_norm_w, m_w_in, m_ssd_conv_w, m_ssd_conv_b, m_ssd_dt_bias, m_ssd_a_log, m_ssd_d, m_ssd_norm_w, m_ret_norm_w, m_w_out, m_post_mix_norm_w, m_pre_ffn_norm_w, m_w_up, m_ffn_conv_w, m_ffn_conv_b, m_w_down, m_post_ffn_norm_w, v_pre_mix_norm_w, v_w_in, v_ssd_conv_w, v_ssd_conv_b, v_ssd_dt_bias, v_ssd_a_log, v_ssd_d, v_ssd_norm_w, v_ret_norm_w, v_w_out, v_post_mix_norm_w, v_pre_ffn_norm_w, v_w_up, v_ffn_conv_w, v_ffn_conv_b, v_w_down, v_post_ffn_norm_w):
    T = x.shape[1]
    xi, tgt = x[0], loss_target[0]
    me = 4 * lax.axis_index("x") + 2 * lax.axis_index("y") + lax.axis_index("c")
    n_in, n_up = w_in.shape[2], w_up.shape[2]
    n_out, n_down = w_out.shape[1], w_down.shape[1]
    n_sc, n_fc = ssd_conv_w.shape[2], ffn_conv_w.shape[2]

    def after(token, value):
        return value * (1.0 + token[0, 0])

    def finish(started, after_value, name, same):
        bufs, lands = _exchange_wait(started, after_value, name, same)
        own = [b if same else lax.dynamic_index_in_dim(b, me, 0, keepdims=False) for b in bufs]
        return [lax.dynamic_update_index_in_dim(l, o, me, 0) for l, o in zip(lands, own)]

    gat_in = _exchange_start([w_in[0].astype(BF)], "gather_in_start", True)
    gat_rest = _exchange_start([after(gat_in[4], w[0]).astype(BF) for w in (w_out, w_up, w_down)], "gather_rest_start", True)
    gconv, = _exchange([after(gat_rest[4], _pack([ssd_conv_w, ffn_conv_w], CONV_ROWS))], "gather_conv", True)
    convs = [_unpack(gconv[d], [(SSD_CONV, n_sc), (FFN_CONV, n_fc)]) for d in range(N_DEV)]
    scw = jnp.pad(jnp.concatenate([c[0] for c in convs], axis=1), ((0, 8 - SSD_CONV), (0, 0)))
    fcw = jnp.pad(jnp.concatenate([c[1] for c in convs], axis=1), ((0, 8 - FFN_CONV), (0, 0)))
    pad_h = lambda p: jnp.pad(p, ((0, 0), (0, LANES - SSD_HEADS)))
    dtb, alog = pad_h(ssd_dt_bias), pad_h(ssd_a_log)
    dskx = jnp.repeat(ssd_d, SSD_HEAD_DIM, axis=1)
    inv = ROPE_BASE ** (-jnp.arange(0, RET_QK, 2, dtype=F32) / RET_QK)
    ang = jnp.arange(T, dtype=F32)[:, None] * inv[None, :]
    cs_, sn_ = jnp.cos(ang), jnp.sin(ang)
    cos = jnp.concatenate([cs_, cs_, cs_, cs_], axis=1)
    sin = jnp.concatenate([-sn_, sn_, -sn_, sn_], axis=1)
    g_in, = finish(gat_in, [cos, sin, scw, fcw], "gather_in_wait", True)
    win = jnp.transpose(g_in, (1, 0, 2)).reshape(D_MODEL, N_DEV * n_in)
    wp = jnp.concatenate([win[:, O_Z:O_XBC], win[:, O_XBC:O_DT], win[:, O_Q:O_K], win[:, O_K:O_V], win[:, O_V:O_G],
                          win[:, O_G:], win[:, O_DT:O_Q], jnp.zeros((D_MODEL, P_END - P_DT - SSD_HEADS), win.dtype)], axis=1)

    h, z, xbc, q, k, v, g, dtr = _fwd_in(xi, pre_mix_norm_w, wp)
    ys, ypre, sst = _ssd_fwd(xbc, dtr, z, scw, ssd_conv_b, dtb, alog, dskx, ssd_norm_w)
    yr, rst = _ret_fwd(q, k, v, g, cos, sin, ret_norm_w)
    g_out, g_up, g_down = finish(gat_rest, yr, "gather_rest_wait", True)
    wout = g_out.reshape(N_DEV * n_out, D_MODEL)
    wup = jnp.transpose(g_up, (1, 0, 2)).reshape(D_MODEL, N_DEV * n_up)
    wdown = g_down.reshape(N_DEV * n_down, D_MODEL)
    y, x1, h2, graw, val = _fwd_mid(ys, yr, xi, wout, post_mix_norm_w, pre_ffn_norm_w, wup)
    a, dfb, dval, dgate, dx2, lossb, d_pff, d_fcb = _ffn_tail(graw, val, x1, tgt, fcw, ffn_conv_b, wdown, post_ffn_norm_w)
    gdown = _matmul_tn(a, dfb, "dw_down")
    sc_down = _exchange_start([gdown.reshape(N_DEV, n_down, D_MODEL).astype(BF)], "scatter_down_start", False)
    dgraw, dx1, dyb, dys, dyr, d_fcw, d_pf, d_pm = _ffn_bwd(dgate, dval, graw, x1, dx2, y, after(sc_down[4], fcw), wup,
                                                         pre_ffn_norm_w, post_mix_norm_w, wout)
    gup = jnp.concatenate([_matmul_tn(h2, dgraw, "dw_up_g"), _matmul_tn(h2, dval, "dw_up_v")], axis=1)
    gout = jnp.concatenate([_matmul_tn(ys, dyb, "dw_out_s"), _matmul_tn(yr, dyb, "dw_out_r")], axis=0)
    sc_mid = _exchange_start([jnp.transpose(gup.reshape(D_MODEL, N_DEV, n_up), (1, 0, 2)).astype(BF),
                              gout.reshape(N_DEV, n_out, D_MODEL).astype(BF)], "scatter_mid_start", False)
    dz, dxbc, ddt, d_scw, d_scb, d_dtb, d_alog, d_dsk, d_snw = _ssd_bwd(dys, ypre, xbc, dtr, z, sst, after(sc_mid[4], scw),
                                                                      ssd_conv_b, dtb, alog, dskx, ssd_norm_w)
    dq, dk, dv, dg, d_rnw = _ret_bwd(dyr, q, k, v, g, rst, cos, sin, ret_norm_w)
    gin = jnp.concatenate([_matmul_tn(h, dz, "dw_z"), _matmul_tn(h, dxbc, "dw_xbc"),
                           _matmul_tn(h, ddt, "dw_dt")[:, :SSD_HEADS], _matmul_tn(h, dq, "dw_q"), _matmul_tn(h, dk, "dw_k"),
                           _matmul_tn(h, dv, "dw_v"), _matmul_tn(h, dg, "dw_g")], axis=1)
    sc_in = _exchange_start([jnp.transpose(gin.reshape(D_MODEL, N_DEV, n_in), (1, 0, 2)).astype(BF)], "scatter_in_start", False)
    gx, d_w0 = _in_bwd(dz, dxbc, dq, dk, dv, dg, ddt, xi, dx1, after(sc_in[4], pre_mix_norm_w), wp)
    r_down, = finish(sc_down, gx, "scatter_down_wait", False)
    r_up, r_out = finish(sc_mid, r_down, "scatter_mid_wait", False)
    per_w = [None] * 4
    per_w[3] = _adamw(r_down, w_down[0], m_w_down[0], v_w_down[0], "adamw_down", n_down)
    per_w[2] = _adamw(r_up, w_up[0], m_w_up[0], v_w_up[0], "adamw_up", 256)
    per_w[1] = _adamw(r_out, w_out[0], m_w_out[0], v_w_out[0], "adamw_out", n_out)
    r_in, = finish(sc_in, per_w[1][0], "scatter_in_wait", False)
    per_w[0] = _adamw(r_in, w_in[0], m_w_in[0], v_w_in[0], "adamw_in", 256)
    big = [[per_w[i][kind][None] for i in range(4)] for kind in range(4)]

    small_full = [d_w0, d_scw[:SSD_CONV], d_scb, d_dtb[:, :SSD_HEADS], d_alog[:, :SSD_HEADS], d_dsk[:, :SSD_HEADS], d_snw, d_rnw,
                  d_pm, d_pf, d_fcw[:FFN_CONV], d_fcb, d_pff]
    full_shapes = [t.shape for t in small_full]
    gs = _sum_slabs(_exchange([_pack(small_full, SMALL_ROWS)], "gather_small", True)[0], "sum_small")
    gfull = _unpack(gs, full_shapes)
    gfull[1] = lax.dynamic_slice_in_dim(gfull[1], me * n_sc, n_sc, axis=1)
    gfull[10] = lax.dynamic_slice_in_dim(gfull[10], me * n_fc, n_fc, axis=1)
    ws = [pre_mix_norm_w, ssd_conv_w, ssd_conv_b, ssd_dt_bias, ssd_a_log, ssd_d, ssd_norm_w, ret_norm_w, post_mix_norm_w,
          pre_ffn_norm_w, ffn_conv_w, ffn_conv_b, post_ffn_norm_w]
    ms = [m_pre_mix_norm_w, m_ssd_conv_w, m_ssd_conv_b, m_ssd_dt_bias, m_ssd_a_log, m_ssd_d, m_ssd_norm_w, m_ret_norm_w,
          m_post_mix_norm_w, m_pre_ffn_norm_w, m_ffn_conv_w, m_ffn_conv_b, m_post_ffn_norm_w]
    vs = [v_pre_mix_norm_w, v_ssd_conv_w, v_ssd_conv_b, v_ssd_dt_bias, v_ssd_a_log, v_ssd_d, v_ssd_norm_w, v_ret_norm_w,
          v_post_mix_norm_w, v_pre_ffn_norm_w, v_ffn_conv_w, v_ffn_conv_b, v_post_ffn_norm_w]
    out_shapes = [t.shape for t in ws]
    small = _adamw(_pack(gfull, SMALL_ROWS)[None], _pack(ws, SMALL_ROWS), _pack(ms, SMALL_ROWS), _pack(vs, SMALL_ROWS),
                   "adamw_small", SMALL_ROWS)
    small = [_unpack(b, out_shapes) for b in small]

    order = {"pre_mix_norm_w": ("s", 0), "w_in": ("b", 0), "ssd_conv_w": ("s", 1), "ssd_conv_b": ("s", 2),
             "ssd_dt_bias": ("s", 3), "ssd_a_log": ("s", 4), "ssd_d": ("s", 5), "ssd_norm_w": ("s", 6), "ret_norm_w": ("s", 7),
             "w_out": ("b", 1), "post_mix_norm_w": ("s", 8), "pre_ffn_norm_w": ("s", 9), "w_up": ("b", 2),
             "ffn_conv_w": ("s", 10), "ffn_conv_b": ("s", 11), "w_down": ("b", 3), "post_ffn_norm_w": ("s", 12)}
    loss = lax.psum(lossb[0, 0], ("x", "y", "c"))
    outs = [loss, gx[None]]
    for kind in range(4):
        for name, (grp, idx) in order.items():
            outs.append(big[kind][idx] if grp == "b" else small[kind][idx])
    return tuple(outs)
```

```python
import functools
import math

import numpy as np
import jax
import jax.numpy as jnp
from jax import lax
from jax.experimental import pallas as pl
from jax.experimental.pallas import tpu as pltpu

F32 = jnp.float32
BF = jnp.bfloat16
HI = lax.Precision.HIGHEST
S = jax.ShapeDtypeStruct

D_MODEL = 1024
SSD_HEADS = 16
SSD_HEAD_DIM = 64
SSD_GROUPS = 2
SSD_STATE = 128
SSD_WIDTH = 1024
SSD_XBC = 1536
SSD_CONV = 4
RET_HEADS = 8
RET_QK = 64
RET_V = 128
RET_QK_W = 512
RET_V_W = 1024
ROPE_BASE = 10000.0
CH = 128
D_FF = 2816
FFN_CONV = 3
EPS = 1e-6
IN_WIDTH = 5648
N_DEV = 8

ADAM_LR = 0.001
ADAM_B1 = 0.9
ADAM_B2 = 0.999
ADAM_EPS = 1e-08
ADAM_WD = 0.01
ADAM_STEP = 10

LANES = 128
HALO = 16
VMEM_LIMIT = 48 * 1024 * 1024

P_Z, P_XBC, P_Q, P_K, P_V, P_G, P_DT, P_END = 0, 1024, 2560, 3072, 3584, 4608, 5632, 5760
O_Z, O_XBC, O_DT, O_Q, O_K, O_V, O_G = 0, 1024, 2560, 2576, 3088, 3600, 4624


def _cparams(*sem):
    return pltpu.CompilerParams(dimension_semantics=sem, vmem_limit_bytes=VMEM_LIMIT)


def _dot(a, b):
    return jnp.dot(a.astype(BF), b.astype(BF), preferred_element_type=F32)


def _dot_nt(a, b):
    return lax.dot_general(a.astype(BF), b.astype(BF), (((1,), (1,)), ((), ())), preferred_element_type=F32)


def _dot_tn(a, b):
    return lax.dot_general(a.astype(BF), b.astype(BF), (((0,), (0,)), ((), ())), preferred_element_type=F32)


def _dot_hi(a, b):
    return jnp.dot(a, b, preferred_element_type=F32, precision=HI)


def _dot_tn_hi(a, b):
    return lax.dot_general(a, b, (((0,), (0,)), ((), ())), preferred_element_type=F32, precision=HI)


def _sigmoid(x):
    return jax.nn.sigmoid(x)


def _dsilu(x, s):
    return s * (1.0 + x * (1.0 - s))


def _softplus(x):
    return jnp.maximum(x, 0.0) + jnp.log1p(jnp.exp(-jnp.abs(x)))


def _rstd(x):
    return lax.rsqrt(jnp.mean(x * x, axis=-1, keepdims=True) + EPS)


def _rms_bwd(dy, x, r, w):
    gn = dy * w
    dx = r * gn - x * (r * r * r) * jnp.mean(gn * x, axis=-1, keepdims=True)
    dw = jnp.sum(dy * x * r, axis=0, keepdims=True)
    return dx, dw


def _rows_before(ext, s, head, n):
    if s == 0:
        return ext[head:head + n]
    return pltpu.roll(ext, s, 0)[head:head + n]


def _rows_after(ext, s, n):
    if s == 0:
        return ext[0:n]
    return pltpu.roll(ext, ext.shape[0] - s, 0)[0:n]


def _row_spec(tm, width):
    return pl.BlockSpec((tm, width), lambda i: (i, 0))


def _const_spec(shape):
    return pl.BlockSpec(shape, lambda i: (0,) * len(shape))


_VMEM_WHOLE = pl.BlockSpec(memory_space=pltpu.VMEM)


def _fwd_in(x, w0, wp, tm=256):
    T = x.shape[0]

    def body(x_ref, w0_ref, wp_ref, h_ref, z_ref, xbc_ref, q_ref, k_ref, v_ref, g_ref, dt_ref):
        xf = x_ref[...]
        h = (xf * _rstd(xf) * w0_ref[...]).astype(BF)
        h_ref[...] = h
        for ref, lo, hi in ((z_ref, P_Z, P_XBC), (xbc_ref, P_XBC, P_Q), (q_ref, P_Q, P_K), (k_ref, P_K, P_V),
                            (v_ref, P_V, P_G), (g_ref, P_G, P_DT), (dt_ref, P_DT, P_END)):
            ref[...] = jnp.dot(h, wp_ref[:, lo:hi], preferred_element_type=F32).astype(ref.dtype)

    widths = (D_MODEL, SSD_WIDTH, SSD_XBC, RET_QK_W, RET_QK_W, RET_V_W, RET_V_W)
    return pl.pallas_call(
        body, name="fwd_in", grid=(T // tm,),
        in_specs=[_row_spec(tm, D_MODEL), _const_spec((1, D_MODEL)), _VMEM_WHOLE],
        out_specs=[_row_spec(tm, w) for w in widths] + [_row_spec(tm, LANES)],
        out_shape=[S((T, w), BF) for w in widths] + [S((T, LANES), F32)],
        compiler_params=_cparams("parallel"),
    )(x, w0, wp)


def _fwd_mid(ys, yr, x, wout, wpm, wpf, wup, tm=256):
    T = x.shape[0]

    def body(ys_ref, yr_ref, x_ref, wout_ref, wpm_ref, wpf_ref, wup_ref, y_ref, x1_ref, h2_ref, graw_ref, val_ref):
        y = (jnp.dot(ys_ref[...], wout_ref[0:SSD_WIDTH, :], preferred_element_type=F32)
             + jnp.dot(yr_ref[...], wout_ref[SSD_WIDTH:, :], preferred_element_type=F32))
        y_ref[...] = y
        x1 = x_ref[...] + y * _rstd(y) * wpm_ref[...]
        x1_ref[...] = x1
        h2 = (x1 * _rstd(x1) * wpf_ref[...]).astype(BF)
        h2_ref[...] = h2
        graw_ref[...] = jnp.dot(h2, wup_ref[:, 0:D_FF], preferred_element_type=F32).astype(BF)
        val_ref[...] = jnp.dot(h2, wup_ref[:, D_FF:], preferred_element_type=F32).astype(BF)

    return pl.pallas_call(
        body, name="fwd_mid", grid=(T // tm,),
        in_specs=[_row_spec(tm, SSD_WIDTH), _row_spec(tm, RET_V_W), _row_spec(tm, D_MODEL), _VMEM_WHOLE,
                  _const_spec((1, D_MODEL)), _const_spec((1, D_MODEL)), _VMEM_WHOLE],
        out_specs=[_row_spec(tm, D_MODEL), _row_spec(tm, D_MODEL), _row_spec(tm, D_MODEL), _row_spec(tm, D_FF),
                   _row_spec(tm, D_FF)],
        out_shape=[S((T, D_MODEL), F32), S((T, D_MODEL), F32), S((T, D_MODEL), BF), S((T, D_FF), BF), S((T, D_FF), BF)],
        compiler_params=_cparams("parallel"),
    )(ys, yr, x, wout, wpm, wpf, wup)


def _ffn_tail(graw, val, x1, tgt, convw, convb, wdown, wpff, tm=256):
    T = x1.shape[0]

    def body(graw_ref, val_ref, x1_ref, tgt_ref, cw_ref, cb_ref, wd_ref, wpff_ref,
             a_ref, df_ref, dval_ref, dgate_ref, dx2_ref, loss_ref, dwpff_ref, dcb_ref, carry):
        i = pl.program_id(0)

        @pl.when(i == 0)
        def _():
            carry[...] = jnp.zeros_like(carry)
            loss_ref[...] = jnp.zeros_like(loss_ref)
            dwpff_ref[...] = jnp.zeros_like(dwpff_ref)
            dcb_ref[...] = jnp.zeros_like(dcb_ref)

        g = graw_ref[...].astype(F32)
        ext = jnp.concatenate([carry[...], g], axis=0)
        carry[...] = g[tm - 8:tm]
        gate = cb_ref[...] + sum(cw_ref[j:j + 1, :] * _rows_before(ext, FFN_CONV - 1 - j, 8, tm) for j in range(FFN_CONV))
        sg = _sigmoid(gate)
        silu = gate * sg
        v = val_ref[...].astype(F32)
        a = (silu * v).astype(BF)
        a_ref[...] = a
        f = jnp.dot(a, wd_ref[...], preferred_element_type=F32)
        r = _rstd(f)
        w = wpff_ref[...]
        e = x1_ref[...] + f * r * w - tgt_ref[...]
        loss_ref[...] += jnp.sum(e * e) * (0.5 / D_MODEL)
        dx2 = e * (1.0 / D_MODEL)
        dx2_ref[...] = dx2
        df, dw = _rms_bwd(dx2, f, r, w)
        dwpff_ref[...] += dw
        dfb = df.astype(BF)
        df_ref[...] = dfb
        da = _dot_nt(dfb, wd_ref[...])
        dval_ref[...] = (da * silu).astype(BF)
        dgate = da * v * _dsilu(gate, sg)
        dcb_ref[...] += jnp.sum(dgate, axis=0, keepdims=True)
        dgate_ref[...] = dgate.astype(BF)

    return pl.pallas_call(
        body, name="ffn_tail", grid=(T // tm,),
        in_specs=[_row_spec(tm, D_FF), _row_spec(tm, D_FF), _row_spec(tm, D_MODEL), _row_spec(tm, D_MODEL),
                  _const_spec((8, D_FF)), _const_spec((1, D_FF)), _VMEM_WHOLE, _const_spec((1, D_MODEL))],
        out_specs=[_row_spec(tm, D_FF), _row_spec(tm, D_MODEL), _row_spec(tm, D_FF), _row_spec(tm, D_FF),
                   _row_spec(tm, D_MODEL), _const_spec((8, LANES)), _const_spec((1, D_MODEL)), _const_spec((1, D_FF))],
        out_shape=[S((T, D_FF), BF), S((T, D_MODEL), BF), S((T, D_FF), BF), S((T, D_FF), BF), S((T, D_MODEL), F32),
                   S((8, LANES), F32), S((1, D_MODEL), F32), S((1, D_FF), F32)],
        scratch_shapes=[pltpu.VMEM((8, D_FF), F32)],
        compiler_params=_cparams("arbitrary"),
    )(graw, val, x1, tgt, convw, convb, wdown, wpff)


def _ffn_bwd(dgate, dval, graw, x1, dx2, y, convw, wup, wpf, wpm, wout, tm=256):
    T = x1.shape[0]
    nt = T // tm
    rev = lambda i: (nt - 1 - i, 0)
    rspec = lambda w: pl.BlockSpec((tm, w), rev)

    def body(dgate_ref, dval_ref, graw_ref, x1_ref, dx2_ref, y_ref, cw_ref, wup_ref, wpf_ref, wpm_ref, wout_ref,
             dgraw_ref, dx1_ref, dy_ref, dys_ref, dyr_ref, dcw_ref, dwpf_ref, dwpm_ref, carry):
        i = pl.program_id(0)

        @pl.when(i == 0)
        def _():
            carry[...] = jnp.zeros_like(carry)
            dcw_ref[...] = jnp.zeros_like(dcw_ref)
            dwpf_ref[...] = jnp.zeros_like(dwpf_ref)
            dwpm_ref[...] = jnp.zeros_like(dwpm_ref)

        dg = dgate_ref[...].astype(F32)
        ext = jnp.concatenate([dg, carry[...]], axis=0)
        carry[...] = dg[0:8]
        g = graw_ref[...].astype(F32)
        dgraw = jnp.zeros((tm, D_FF), F32)
        for j in range(FFN_CONV):
            sj = _rows_after(ext, FFN_CONV - 1 - j, tm)
            dgraw = dgraw + cw_ref[j:j + 1, :] * sj
            dcw_ref[j:j + 1, :] += jnp.sum(sj * g, axis=0, keepdims=True)
        dgrawb = dgraw.astype(BF)
        dgraw_ref[...] = dgrawb
        dh2 = _dot_nt(dgrawb, wup_ref[:, 0:D_FF]) + _dot_nt(dval_ref[...], wup_ref[:, D_FF:])
        x1 = x1_ref[...]
        dxa, dw = _rms_bwd(dh2, x1, _rstd(x1), wpf_ref[...])
        dwpf_ref[...] += dw
        dx1 = dx2_ref[...] + dxa
        dx1_ref[...] = dx1
        yv = y_ref[...]
        dy, dw = _rms_bwd(dx1, yv, _rstd(yv), wpm_ref[...])
        dwpm_ref[...] += dw
        dyb = dy.astype(BF)
        dy_ref[...] = dyb
        dys_ref[...] = _dot_nt(dyb, wout_ref[0:SSD_WIDTH, :]).astype(BF)
        dyr_ref[...] = _dot_nt(dyb, wout_ref[SSD_WIDTH:, :]).astype(BF)

    return pl.pallas_call(
        body, name="ffn_bwd", grid=(nt,),
        in_specs=[rspec(D_FF), rspec(D_FF), rspec(D_FF), rspec(D_MODEL), rspec(D_MODEL), rspec(D_MODEL),
                  _const_spec((8, D_FF)), _VMEM_WHOLE, _const_spec((1, D_MODEL)), _const_spec((1, D_MODEL)), _VMEM_WHOLE],
        out_specs=[rspec(D_FF), rspec(D_MODEL), rspec(D_MODEL), rspec(SSD_WIDTH), rspec(RET_V_W),
                   _const_spec((8, D_FF)), _const_spec((1, D_MODEL)), _const_spec((1, D_MODEL))],
        out_shape=[S((T, D_FF), BF), S((T, D_MODEL), F32), S((T, D_MODEL), BF), S((T, SSD_WIDTH), BF), S((T, RET_V_W), BF),
                   S((8, D_FF), F32), S((1, D_MODEL), F32), S((1, D_MODEL), F32)],
        scratch_shapes=[pltpu.VMEM((8, D_FF), F32)],
        compiler_params=_cparams("arbitrary"),
    )(dgate, dval, graw, x1, dx2, y, convw, wup, wpf, wpm, wout)


def _in_bwd(dz, dxbc, dq, dk, dv, dg, ddt, x, dx1, w0, wp, tm=256):
    T = x.shape[0]

    def body(dz_ref, dxbc_ref, dq_ref, dk_ref, dv_ref, dg_ref, ddt_ref, x_ref, dx1_ref, w0_ref, wp_ref, gx_ref, dw0_ref):
        @pl.when(pl.program_id(0) == 0)
        def _():
            dw0_ref[...] = jnp.zeros_like(dw0_ref)

        dh = jnp.zeros((tm, D_MODEL), F32)
        for ref, lo, hi in ((dz_ref, P_Z, P_XBC), (dxbc_ref, P_XBC, P_Q), (dq_ref, P_Q, P_K), (dk_ref, P_K, P_V),
                            (dv_ref, P_V, P_G), (dg_ref, P_G, P_DT), (ddt_ref, P_DT, P_END)):
            dh = dh + _dot_nt(ref[...], wp_ref[:, lo:hi])
        xf = x_ref[...]
        dx, dw = _rms_bwd(dh, xf, _rstd(xf), w0_ref[...])
        dw0_ref[...] += dw
        gx_ref[...] = dx1_ref[...] + dx

    widths = (SSD_WIDTH, SSD_XBC, RET_QK_W, RET_QK_W, RET_V_W, RET_V_W, LANES)
    return pl.pallas_call(
        body, name="in_bwd", grid=(T // tm,),
        in_specs=[_row_spec(tm, w) for w in widths] + [_row_spec(tm, D_MODEL), _row_spec(tm, D_MODEL),
                                                       _const_spec((1, D_MODEL)), _VMEM_WHOLE],
        out_specs=[_row_spec(tm, D_MODEL), _const_spec((1, D_MODEL))],
        out_shape=[S((T, D_MODEL), F32), S((1, D_MODEL), F32)],
        compiler_params=_cparams("arbitrary"),
    )(dz, dxbc, dq, dk, dv, dg, ddt, x, dx1, w0, wp)


def _matmul_tn(a, b, name, tk=512):
    T, M = a.shape
    N = b.shape[1]
    tn = N
    while M * tn * 4 > (4 << 20) and tn % 256 == 0:
        tn //= 2
    nk = T // tk

    def body(a_ref, b_ref, o_ref):
        @pl.when(pl.program_id(1) == 0)
        def _():
            o_ref[...] = jnp.zeros_like(o_ref)

        o_ref[...] += _dot_tn(a_ref[...], b_ref[...])

    return pl.pallas_call(
        body, name=name, grid=(N // tn, nk),
        in_specs=[pl.BlockSpec((tk, M), lambda n, k: (k, 0)), pl.BlockSpec((tk, tn), lambda n, k: (k, n))],
        out_specs=pl.BlockSpec((M, tn), lambda n, k: (0, n)),
        out_shape=S((M, N), F32),
        compiler_params=_cparams("parallel", "arbitrary"),
    )(a, b)


def _tri(lower):
    r = lax.broadcasted_iota(jnp.int32, (CH, CH), 0)
    c = lax.broadcasted_iota(jnp.int32, (CH, CH), 1)
    return ((c <= r) if lower else (r <= c)).astype(F32)


def _onehot_row(h):
    return (lax.broadcasted_iota(jnp.int32, (1, LANES), 1) == h).astype(F32)


def _onehot_col(h):
    return (lax.broadcasted_iota(jnp.int32, (LANES, 1), 0) == h).astype(F32)


def _ssd_pre(xc_ref, xh_ref, dtr_ref, cw_ref, cb_ref, dtb_ref, alog_ref, first):
    xc = xc_ref[...].astype(F32)
    xh = jnp.where(first, 0.0, xh_ref[...].astype(F32))
    ext = jnp.concatenate([xh, xc], axis=0)
    u = cb_ref[...] + sum(cw_ref[j:j + 1, :] * _rows_before(ext, SSD_CONV - 1 - j, HALO, CH) for j in range(SSD_CONV))
    sg = _sigmoid(u)
    act = u * sg
    dt = _softplus(dtr_ref[...] + dtb_ref[...])
    a = -jnp.exp(alog_ref[...])
    da = dt * a
    cs = _dot_hi(_tri(True), da)
    cst = _dot_tn_hi(da, _tri(False))
    return xc, u, sg, act, dt, a, cs, cst


HPG = SSD_HEADS // SSD_GROUPS
GW = HPG * SSD_HEAD_DIM


def _expand_heads(src, buf):
    for h in range(SSD_HEADS):
        buf[:, h * SSD_HEAD_DIM:(h + 1) * SSD_HEAD_DIM] = jnp.broadcast_to(src[:, h:h + 1], (CH, SSD_HEAD_DIM))


def _ssd_expanded(act, dt, cs, dtx, csx):
    _expand_heads(dt, dtx)
    _expand_heads(cs, csx)
    csv = csx[...]
    last = csv[CH - 1:CH, :]
    e_exp = jnp.exp(csv)
    dec_exp = jnp.exp(last - csv)
    el_exp = jnp.exp(last)
    xs = act[:, 0:SSD_WIDTH]
    xdt = xs * dtx[...]
    return xs, xdt, xdt * dec_exp, e_exp, dec_exp, el_exp


def _decay_mats(h, cs, cst, transposed):
    r = lax.broadcasted_iota(jnp.int32, (CH, CH), 0)
    c = lax.broadcasted_iota(jnp.int32, (CH, CH), 1)
    c_col = cs[:, h:h + 1]
    c_row = cst[h:h + 1, :]
    if transposed:
        return jnp.exp(jnp.where(r <= c, c_row - c_col, -1e30))
    return jnp.exp(jnp.where(r >= c, c_col - c_row, -1e30))


def _ssd_specs(T):
    nc = T // CH
    return nc, [
        _row_spec(CH, SSD_XBC),
        pl.BlockSpec((HALO, SSD_XBC), lambda i: (jnp.maximum(i * (CH // HALO) - 1, 0), 0)),
        _row_spec(CH, LANES),
        _row_spec(CH, SSD_WIDTH),
    ]


def _groups(act):
    bm = [act[:, SSD_WIDTH + g * SSD_STATE:SSD_WIDTH + (g + 1) * SSD_STATE] for g in range(SSD_GROUPS)]
    o = SSD_WIDTH + SSD_GROUPS * SSD_STATE
    cm = [act[:, o + g * SSD_STATE:o + (g + 1) * SSD_STATE] for g in range(SSD_GROUPS)]
    return bm, cm


def _ssd_fwd(xbc, dtr, z, convw, convb, dtb, alog, dskx, nw):
    T = xbc.shape[0]
    nc, specs = _ssd_specs(T)

    def body(xc_ref, xh_ref, dtr_ref, z_ref, cw_ref, cb_ref, dtb_ref, alog_ref, dskx_ref, nw_ref,
             out_ref, y_ref, st_ref, state, ybuf, dtx, csx):
        i = pl.program_id(0)

        @pl.when(i == 0)
        def _():
            state[...] = jnp.zeros_like(state)

        xc, u, sg, act, dt, a, cs, cst = _ssd_pre(xc_ref, xh_ref, dtr_ref, cw_ref, cb_ref, dtb_ref, alog_ref, i == 0)
        xs, xdt, w, e_exp, dec_exp, el_exp = _ssd_expanded(act, dt, cs, dtx, csx)
        bm, cm = _groups(act)
        for g in range(SSD_GROUPS):
            gs = slice(g * GW, (g + 1) * GW)
            st = state[g]
            st_ref[0, g] = st
            cb = _dot_nt(cm[g], bm[g])
            ybuf[:, gs] = _dot(cm[g], st) * e_exp[:, gs] + xs[:, gs] * dskx_ref[:, gs]
            state[g] = st * el_exp[:, gs] + _dot_tn(bm[g], w[:, gs])
            for h in range(g * HPG, (g + 1) * HPG):
                sl = slice(h * SSD_HEAD_DIM, (h + 1) * SSD_HEAD_DIM)
                ybuf[:, sl] += _dot(cb * _decay_mats(h, cs, cst, False), xdt[:, sl])
        yv = ybuf[...]
        y_ref[...] = yv.astype(BF)
        zf = z_ref[...].astype(F32)
        gated = yv * (zf * _sigmoid(zf))
        out_ref[...] = (gated * _rstd(gated) * nw_ref[...]).astype(BF)

    st_spec = pl.BlockSpec((1, SSD_GROUPS, SSD_STATE, GW), lambda i: (i, 0, 0, 0))
    return pl.pallas_call(
        body, name="ssd_fwd", grid=(nc,),
        in_specs=specs + [_const_spec((8, SSD_XBC)), _const_spec((1, SSD_XBC)), _const_spec((1, LANES)),
                          _const_spec((1, LANES)), _const_spec((1, SSD_WIDTH)), _const_spec((1, SSD_WIDTH))],
        out_specs=[_row_spec(CH, SSD_WIDTH), _row_spec(CH, SSD_WIDTH), st_spec],
        out_shape=[S((T, SSD_WIDTH), BF), S((T, SSD_WIDTH), BF), S((nc, SSD_GROUPS, SSD_STATE, GW), F32)],
        scratch_shapes=[pltpu.VMEM((SSD_GROUPS, SSD_STATE, GW), F32), pltpu.VMEM((CH, SSD_WIDTH), F32),
                        pltpu.VMEM((CH, SSD_WIDTH), F32), pltpu.VMEM((CH, SSD_WIDTH), F32)],
        compiler_params=_cparams("arbitrary"),
    )(xbc, xbc, dtr, z, convw, convb, dtb, alog, dskx, nw)


def _ssd_bwd(dout, y, xbc, dtr, z, states, convw, convb, dtb, alog, dskx, nw):
    T = xbc.shape[0]
    nc = T // CH
    rev = lambda i: (nc - 1 - i, 0)
    rspec = lambda w: pl.BlockSpec((CH, w), rev)
    halo_spec = pl.BlockSpec((HALO, SSD_XBC), lambda i: (jnp.maximum((nc - 1 - i) * (CH // HALO) - 1, 0), 0))
    NB = SSD_WIDTH
    NC_ = SSD_WIDTH + SSD_GROUPS * SSD_STATE

    def body(do_ref, y_ref, xc_ref, xh_ref, dtr_ref, z_ref, st_ref, cw_ref, cb_ref, dtb_ref, alog_ref, dskx_ref, nw_ref,
             dz_ref, dxbc_ref, ddt_ref, dcw_ref, dcb_ref, ddtb_ref, dalog_ref, ddsk_ref, dnw_ref,
             dstate, ducarry, dtx, csx, dxdtbuf, dact):
        i = pl.program_id(0)

        @pl.when(i == 0)
        def _():
            dstate[...] = jnp.zeros_like(dstate)
            ducarry[...] = jnp.zeros_like(ducarry)
            for ref in (dcw_ref, dcb_ref, ddtb_ref, dalog_ref, ddsk_ref, dnw_ref):
                ref[...] = jnp.zeros_like(ref)

        xc, u, sg, act, dt, a, cs, cst = _ssd_pre(xc_ref, xh_ref, dtr_ref, cw_ref, cb_ref, dtb_ref, alog_ref, i == nc - 1)
        xs, xdt, w, e_exp, dec_exp, el_exp = _ssd_expanded(act, dt, cs, dtx, csx)
        bm, cm = _groups(act)
        yv = y_ref[...].astype(F32)
        zf = z_ref[...].astype(F32)
        sz = _sigmoid(zf)
        gated = yv * (zf * sz)
        dgated, dnw = _rms_bwd(do_ref[...].astype(F32), gated, _rstd(gated), nw_ref[...])
        dnw_ref[...] += dnw
        dz_ref[...] = (dgated * yv * _dsilu(zf, sz)).astype(BF)
        dy = dgated * (zf * sz)
        lane_of = lax.broadcasted_iota(jnp.int32, (SSD_WIDTH, LANES), 0) - SSD_HEAD_DIM * lax.broadcasted_iota(jnp.int32, (SSD_WIDTH, LANES), 1)
        expt = ((lane_of >= 0) & (lane_of < SSD_HEAD_DIM)).astype(F32)
        ddsk_ref[...] += _dot_hi(jnp.sum(dy * xs, axis=0, keepdims=True), expt)
        dcs = jnp.zeros((CH, LANES), F32)
        dcst = jnp.zeros((LANES, CH), F32)
        ddt = jnp.zeros((CH, LANES), F32)
        lastrows = []
        for g in range(SSD_GROUPS):
            gs = slice(g * GW, (g + 1) * GW)
            st = st_ref[0, g]
            dsn = dstate[g]
            cbm = _dot_nt(cm[g], bm[g])
            cbt = _dot_nt(bm[g], cm[g])
            dy_g = dy[:, gs]
            yoff = _dot(cm[g], st) * e_exp[:, gs]
            dq = dy_g * e_exp[:, gs]
            dcm_g = _dot_nt(dq, st)
            dstate[g] = _dot_tn(cm[g], dq) + dsn * el_exp[:, gs]
            dw = _dot(bm[g], dsn)
            w_g = w[:, gs]
            dbm_g = _dot_nt(w_g, dsn)
            dww = dw * w_g
            red = dy_g * yoff - dww
            lastrows.append(jnp.sum(dsn * st, axis=0, keepdims=True) * el_exp[:, gs] + jnp.sum(dww, axis=0, keepdims=True))
            dxdtbuf[:, gs] = dw * dec_exp[:, gs]
            dcb = jnp.zeros((CH, CH), F32)
            for h in range(g * HPG, (g + 1) * HPG):
                sl = slice(h * SSD_HEAD_DIM, (h + 1) * SSD_HEAD_DIM)
                rl = slice((h - g * HPG) * SSD_HEAD_DIM, (h - g * HPG + 1) * SSD_HEAD_DIM)
                oh = _onehot_row(h)
                lmat = _decay_mats(h, cs, cst, False)
                mmat = cbm * lmat
                dy_h = dy[:, sl]
                dm = _dot_nt(dy_h, xdt[:, sl])
                dxdt_h = dxdtbuf[:, sl] + _dot(cbt * _decay_mats(h, cs, cst, True), dy_h)
                dxdtbuf[:, sl] = dxdt_h
                dseg = dm * mmat
                dcb = dcb + dm * lmat
                col = jnp.sum(dseg, axis=1, keepdims=True) + jnp.sum(red[:, rl], axis=1, keepdims=True)
                dcs = dcs + col * oh
                dcst = dcst - _onehot_col(h) * jnp.sum(dseg, axis=0, keepdims=True)
                ddt = ddt + jnp.sum(dxdt_h * xs[:, sl], axis=1, keepdims=True) * oh
            dact[:, NB + g * SSD_STATE:NB + (g + 1) * SSD_STATE] = dbm_g + _dot_tn(dcb, cm[g])
            dact[:, NC_ + g * SSD_STATE:NC_ + (g + 1) * SSD_STATE] = dcm_g + _dot(dcb, bm[g])
        dact[:, 0:SSD_WIDTH] = dy * dskx_ref[...] + dxdtbuf[...] * dtx[...]
        dlast = _dot_hi(jnp.concatenate(lastrows, axis=1), expt)
        rows = lax.broadcasted_iota(jnp.int32, (CH, LANES), 0)
        dcs = dcs + _dot_tn_hi(dcst, jnp.eye(LANES, dtype=F32)) + jnp.where(rows == CH - 1, dlast, 0.0)
        dda = _dot_hi(_tri(False), dcs)
        dalog_ref[...] += jnp.sum(dda * dt, axis=0, keepdims=True) * a
        ddt = ddt + dda * a
        ddtr = ddt * _sigmoid(dtr_ref[...] + dtb_ref[...])
        ddtb_ref[...] += jnp.sum(ddtr, axis=0, keepdims=True)
        ddt_ref[...] = ddtr.astype(BF)
        du = dact[...] * _dsilu(u, sg)
        dcb_ref[...] += jnp.sum(du, axis=0, keepdims=True)
        ext = jnp.concatenate([du, ducarry[...]], axis=0)
        ducarry[...] = du[0:8]
        dx = jnp.zeros((CH, SSD_XBC), F32)
        for j in range(SSD_CONV):
            sj = _rows_after(ext, SSD_CONV - 1 - j, CH)
            dx = dx + cw_ref[j:j + 1, :] * sj
            dcw_ref[j:j + 1, :] += jnp.sum(sj * xc, axis=0, keepdims=True)
        dxbc_ref[...] = dx.astype(BF)

    return pl.pallas_call(
        body, name="ssd_bwd", grid=(nc,),
        in_specs=[rspec(SSD_WIDTH), rspec(SSD_WIDTH), rspec(SSD_XBC), halo_spec, rspec(LANES), rspec(SSD_WIDTH),
                  pl.BlockSpec((1, SSD_GROUPS, SSD_STATE, GW), lambda i: (nc - 1 - i, 0, 0, 0)),
                  _const_spec((8, SSD_XBC)), _const_spec((1, SSD_XBC)), _const_spec((1, LANES)),
                  _const_spec((1, LANES)), _const_spec((1, SSD_WIDTH)), _const_spec((1, SSD_WIDTH))],
        out_specs=[rspec(SSD_WIDTH), rspec(SSD_XBC), rspec(LANES),
                   _const_spec((8, SSD_XBC)), _const_spec((1, SSD_XBC)), _const_spec((1, LANES)),
                   _const_spec((1, LANES)), _const_spec((1, LANES)), _const_spec((1, SSD_WIDTH))],
        out_shape=[S((T, SSD_WIDTH), BF), S((T, SSD_XBC), BF), S((T, LANES), BF),
                   S((8, SSD_XBC), F32), S((1, SSD_XBC), F32), S((1, LANES), F32),
                   S((1, LANES), F32), S((1, LANES), F32), S((1, SSD_WIDTH), F32)],
        scratch_shapes=[pltpu.VMEM((SSD_GROUPS, SSD_STATE, GW), F32), pltpu.VMEM((8, SSD_XBC), F32),
                        pltpu.VMEM((CH, SSD_WIDTH), F32), pltpu.VMEM((CH, SSD_WIDTH), F32),
                        pltpu.VMEM((CH, SSD_WIDTH), F32), pltpu.VMEM((CH, SSD_XBC), F32)],
        compiler_params=_cparams("arbitrary"),
    )(dout, y, xbc, xbc, dtr, z, states, convw, convb, dtb, alog, dskx, nw)


def _log_gamma(h):
    return float(np.log1p(-np.exp2(np.float32(-5.0 - h)), dtype=np.float32))


def _swap_halves(t):
    n = t.shape[1]
    lane = lax.broadcasted_iota(jnp.int32, t.shape, 1)
    return jnp.where((lane & (RET_QK - 1)) < RET_QK // 2, pltpu.roll(t, n - RET_QK // 2, 1), pltpu.roll(t, RET_QK // 2, 1))


def _rot(t, cos, sin):
    return t * cos + _swap_halves(t) * sin


def _rot_t(d, cos, sin):
    return d * cos + _swap_halves(d * sin)


def _ret_consts(h):
    lg = _log_gamma(h)
    r = lax.broadcasted_iota(jnp.int32, (CH, CH), 0)
    c = lax.broadcasted_iota(jnp.int32, (CH, CH), 1)
    rel = (r - c).astype(F32)
    dmask = jnp.where(rel >= 0, jnp.exp(lg * jnp.maximum(rel, 0.0)), 0.0)
    dmask_t = jnp.where(rel <= 0, jnp.exp(lg * jnp.maximum(-rel, 0.0)), 0.0)
    pos = lax.broadcasted_iota(jnp.int32, (CH, 1), 0).astype(F32)
    kdec = jnp.exp(lg * (CH - 1.0 - pos))
    qdec = jnp.exp(lg * (pos + 1.0))
    pos_row = lax.broadcasted_iota(jnp.int32, (1, CH), 1).astype(F32)
    kdec_row = jnp.exp(lg * (CH - 1.0 - pos_row))
    qdec_row = jnp.exp(lg * (pos_row + 1.0))
    return dmask, dmask_t, kdec, qdec, kdec_row, qdec_row, math.exp(lg * CH)


def _ret_fwd(q, k, v, g, cos, sin, nw):
    T = q.shape[0]
    nc = T // CH

    def body(q_ref, k_ref, v_ref, g_ref, cos_ref, sin_ref, nw_ref, out_ref, st_ref, state):
        i = pl.program_id(0)

        @pl.when(i == 0)
        def _():
            state[...] = jnp.zeros_like(state)

        cosf = jnp.tile(cos_ref[...], (1, RET_QK_W // LANES))
        sinf = jnp.tile(sin_ref[...], (1, RET_QK_W // LANES))
        qr = _rot(q_ref[...].astype(F32), cosf, sinf)
        kr = _rot(k_ref[...].astype(F32), cosf, sinf) * (RET_QK ** -0.5)
        krt = kr.T
        st_ref[0] = state[...]
        o_all = []
        for h in range(RET_HEADS):
            dmask, dmask_t, kdec, qdec, kdec_row, qdec_row, gam = _ret_consts(h)
            qs = slice(h * RET_QK, (h + 1) * RET_QK)
            v_h = v_ref[:, h * RET_V:(h + 1) * RET_V]
            rprev = state[h]
            scores = _dot_nt(qr[:, qs], kr[:, qs]) * dmask
            o_all.append(_dot(scores, v_h) + _dot(qr[:, qs] * qdec, rprev))
            state[h] = rprev * gam + _dot(krt[qs, :] * kdec_row, v_h)
        for h in range(RET_HEADS):
            sl = slice(h * RET_V, (h + 1) * RET_V)
            o = o_all[h]
            gf = g_ref[:, sl].astype(F32)
            out_ref[:, sl] = (o * _rstd(o) * nw_ref[:, sl] * (gf * _sigmoid(gf))).astype(BF)

    return pl.pallas_call(
        body, name="ret_fwd", grid=(nc,),
        in_specs=[_row_spec(CH, RET_QK_W), _row_spec(CH, RET_QK_W), _row_spec(CH, RET_V_W), _row_spec(CH, RET_V_W),
                  _row_spec(CH, LANES), _row_spec(CH, LANES), _const_spec((1, RET_V_W))],
        out_specs=[_row_spec(CH, RET_V_W), pl.BlockSpec((1, RET_HEADS, RET_QK, RET_V), lambda i: (i, 0, 0, 0))],
        out_shape=[S((T, RET_V_W), BF), S((nc, RET_HEADS, RET_QK, RET_V), F32)],
        scratch_shapes=[pltpu.VMEM((RET_HEADS, RET_QK, RET_V), F32)],
        compiler_params=_cparams("arbitrary"),
    )(q, k, v, g, cos, sin, nw)


def _ret_bwd(dout, q, k, v, g, states, cos, sin, nw):
    T = q.shape[0]
    nc = T // CH
    rev = lambda i: (nc - 1 - i, 0)
    rspec = lambda w: pl.BlockSpec((CH, w), rev)

    def body(do_ref, q_ref, k_ref, v_ref, g_ref, st_ref, cos_ref, sin_ref, nw_ref,
             dq_ref, dk_ref, dv_ref, dg_ref, dnw_ref, dstate, dqbuf, dkbuf):
        i = pl.program_id(0)

        @pl.when(i == 0)
        def _():
            dstate[...] = jnp.zeros_like(dstate)
            dnw_ref[...] = jnp.zeros_like(dnw_ref)

        cosf = jnp.tile(cos_ref[...], (1, RET_QK_W // LANES))
        sinf = jnp.tile(sin_ref[...], (1, RET_QK_W // LANES))
        qr = _rot(q_ref[...].astype(F32), cosf, sinf)
        kr = _rot(k_ref[...].astype(F32), cosf, sinf) * (RET_QK ** -0.5)
        qrt = qr.T
        heads = range(RET_HEADS)
        qsl = [slice(h * RET_QK, (h + 1) * RET_QK) for h in heads]
        vsl = [slice(h * RET_V, (h + 1) * RET_V) for h in heads]
        scores_t, o_all, do_all = [], [], []
        for h in heads:
            dmask, dmask_t, kdec, qdec, kdec_row, qdec_row, gam = _ret_consts(h)
            q_h, k_h = qr[:, qsl[h]], kr[:, qsl[h]]
            scores = _dot_nt(q_h, k_h) * dmask
            scores_t.append(_dot_nt(k_h, q_h) * dmask_t)
            o_all.append(_dot(scores, v_ref[:, vsl[h]]) + _dot(q_h * qdec, st_ref[0, h]))
        for h in heads:
            o = o_all[h]
            rr = _rstd(o)
            of = o * rr
            gf = g_ref[:, vsl[h]].astype(F32)
            sgg = _sigmoid(gf)
            d_h = do_ref[:, vsl[h]].astype(F32)
            nw_h = nw_ref[:, vsl[h]]
            dg_ref[:, vsl[h]] = (d_h * of * nw_h * _dsilu(gf, sgg)).astype(BF)
            dt_ = d_h * (gf * sgg)
            dnw_ref[:, vsl[h]] += jnp.sum(dt_ * of, axis=0, keepdims=True)
            dof = dt_ * nw_h
            do_all.append(rr * dof - o * (rr * rr * rr) * jnp.mean(dof * o, axis=-1, keepdims=True))
        for h in heads:
            dmask, dmask_t, kdec, qdec, kdec_row, qdec_row, gam = _ret_consts(h)
            q_h, k_h, v_h, do = qr[:, qsl[h]], kr[:, qsl[h]], v_ref[:, vsl[h]], do_all[h]
            gnext = dstate[h]
            dsc = _dot_nt(do, v_h) * dmask
            dsc_t = _dot_nt(v_h, do) * dmask_t
            dv_ref[:, vsl[h]] = (_dot(scores_t[h], do) + _dot(k_h * kdec, gnext)).astype(BF)
            dqbuf[:, qsl[h]] = _dot(dsc, k_h) + _dot_nt(do, st_ref[0, h]) * qdec
            dkbuf[:, qsl[h]] = _dot(dsc_t, q_h) + _dot_nt(v_h, gnext) * kdec
            dstate[h] = gnext * gam + _dot(qrt[qsl[h], :] * qdec_row, do)
        dq_ref[...] = _rot_t(dqbuf[...], cosf, sinf).astype(BF)
        dk_ref[...] = (_rot_t(dkbuf[...], cosf, sinf) * (RET_QK ** -0.5)).astype(BF)

    return pl.pallas_call(
        body, name="ret_bwd", grid=(nc,),
        in_specs=[rspec(RET_V_W), rspec(RET_QK_W), rspec(RET_QK_W), rspec(RET_V_W), rspec(RET_V_W),
                  pl.BlockSpec((1, RET_HEADS, RET_QK, RET_V), lambda i: (nc - 1 - i, 0, 0, 0)),
                  rspec(LANES), rspec(LANES), _const_spec((1, RET_V_W))],
        out_specs=[rspec(RET_QK_W), rspec(RET_QK_W), rspec(RET_V_W), rspec(RET_V_W), _const_spec((1, RET_V_W))],
        out_shape=[S((T, RET_QK_W), BF), S((T, RET_QK_W), BF), S((T, RET_V_W), BF), S((T, RET_V_W), BF),
                   S((1, RET_V_W), F32)],
        scratch_shapes=[pltpu.VMEM((RET_HEADS, RET_QK, RET_V), F32), pltpu.VMEM((CH, RET_QK_W), F32),
                        pltpu.VMEM((CH, RET_QK_W), F32)],
        compiler_params=_cparams("arbitrary"),
    )(dout, q, k, v, g, states, cos, sin, nw)


def _exchange(bufs, name, same):
    nb = len(bufs)
    slabs = [tuple(b.shape if same else b.shape[1:]) for b in bufs]

    def body(*refs):
        buf_refs, out_refs, token = refs[:nb], refs[nb:2 * nb], refs[2 * nb]
        send_sems, recv_sems, local_sems = refs[2 * nb + 1:]
        x, y, c = lax.axis_index("x"), lax.axis_index("y"), lax.axis_index("c")
        me = 4 * x + 2 * y + c
        token[...] = jnp.zeros_like(token)

        def src(b, d):
            return buf_refs[b] if same else buf_refs[b].at[d]

        def remote(b, k, to_me):
            px = 1 - x if k & 4 else x
            py = 1 - y if k & 2 else y
            pc = 1 - c if k & 1 else c
            p = 4 * px + 2 * py + pc
            s = b * (N_DEV - 1) + k - 1
            return pltpu.make_async_remote_copy(
                src_ref=src(b, p), dst_ref=out_refs[b].at[me if to_me else p], send_sem=send_sems.at[s],
                recv_sem=recv_sems.at[s], device_id=(px, py, pc), device_id_type=pl.DeviceIdType.MESH)

        local = [pltpu.make_async_copy(src(b, me), out_refs[b].at[me], local_sems.at[b]) for b in range(nb)]
        for cp in local:
            cp.start()
        sends = [remote(b, k, True) for k in range(1, N_DEV) for b in range(nb)]
        for cp in sends:
            cp.start()
        for k in range(1, N_DEV):
            for b in range(nb):
                remote(b, k, False).wait_recv()
        for cp in sends:
            cp.wait_send()
        for cp in local:
            cp.wait()

    any_spec = pl.BlockSpec(memory_space=pl.ANY)
    out = pl.pallas_call(
        body, name=name,
        in_specs=[any_spec] * nb, out_specs=[any_spec] * nb + [pl.BlockSpec(memory_space=pltpu.VMEM)],
        out_shape=[S((N_DEV,) + s, b.dtype) for s, b in zip(slabs, bufs)] + [S((8, LANES), F32)],
        scratch_shapes=[pltpu.SemaphoreType.DMA((nb * (N_DEV - 1),)), pltpu.SemaphoreType.DMA((nb * (N_DEV - 1),)),
                        pltpu.SemaphoreType.DMA((nb,))],
    )(*bufs)
    return list(out[:nb]), out[nb]


_HBM = pl.BlockSpec(memory_space=pltpu.HBM)
_SEM = pl.BlockSpec(memory_space=pltpu.SEMAPHORE)
_EFFECT = pltpu.SideEffectType.DATAFLOW_SIDE_EFFECTING


ALL_PEERS = tuple(range(1, N_DEV))
ONE_PER_CHIP = (1, 2, 4, 6)
OTHER_CHIPS = (2, 4, 6)


def _split_copies(buf_refs, land_refs, send_sems, recv_sems, same, to_me, ks):
    x, y, c = lax.axis_index("x"), lax.axis_index("y"), lax.axis_index("c")
    me = 4 * x + 2 * y + c
    cps = []
    for ki, k in enumerate(ks):
        px = 1 - x if k & 4 else x
        py = 1 - y if k & 2 else y
        pc = 1 - c if k & 1 else c
        p = 4 * px + 2 * py + pc
        for b in range(len(buf_refs)):
            s = b * len(ks) + ki
            cps.append(pltpu.make_async_remote_copy(
                src_ref=buf_refs[b] if same else buf_refs[b].at[p], dst_ref=land_refs[b].at[me if to_me else p],
                send_sem=send_sems.at[s], recv_sem=recv_sems.at[s], device_id=(px, py, pc), device_id_type=pl.DeviceIdType.MESH))
    return cps


def _exchange_start(bufs, name, same, ks=ALL_PEERS):
    nb = len(bufs)
    ns = nb * len(ks)
    lands = [lax.empty((N_DEV,) + tuple(b.shape if same else b.shape[1:]), b.dtype) for b in bufs]

    def body(*refs):
        buf_refs, land_refs = refs[:nb], refs[nb:2 * nb]
        send_sems, recv_sems = refs[2 * nb], refs[2 * nb + 1]
        token = refs[-1]
        for cp in _split_copies(buf_refs, land_refs, send_sems, recv_sems, same, True, ks):
            cp.start()
        token[...] = jnp.zeros_like(token)

    hbm = lambda a: pltpu.with_memory_space_constraint(a, pltpu.HBM)
    out = pl.pallas_call(
        body, name=name,
        out_shape=(pltpu.SemaphoreType.DMA((ns,)), pltpu.SemaphoreType.DMA((ns,)),
                   *[pltpu.HBM(a.shape, a.dtype) for a in list(bufs) + lands], S((8, LANES), F32)),
        in_specs=[_HBM] * (2 * nb), out_specs=(_SEM, _SEM, *[_HBM] * (2 * nb), pl.BlockSpec(memory_space=pltpu.VMEM)),
        input_output_aliases={i: 2 + i for i in range(2 * nb)},
        compiler_params=pltpu.CompilerParams(has_side_effects=_EFFECT),
    )(*[hbm(a) for a in list(bufs) + lands])
    return out[0], out[1], list(out[2:2 + nb]), list(out[2 + nb:2 + 2 * nb]), out[-1]


def _exchange_wait(started, after, name, same, ks=ALL_PEERS):
    send_sems, recv_sems, bufs, lands, _ = started
    nb = len(bufs)
    after = list(after) if isinstance(after, (list, tuple)) else [after]

    def body(*refs):
        buf_refs, land_refs = refs[:nb], refs[nb:2 * nb]
        s_sems, r_sems = refs[2 * nb], refs[2 * nb + 1]
        for cp in _split_copies(buf_refs, land_refs, s_sems, r_sems, same, False, ks):
            cp.wait_send()
            cp.wait_recv()

    out = pl.pallas_call(
        body, name=name,
        out_shape=tuple(pltpu.HBM(a.shape, a.dtype) for a in bufs + lands),
        in_specs=[_HBM] * (2 * nb) + [_SEM, _SEM] + [pl.BlockSpec(memory_space=pl.ANY)] * len(after),
        out_specs=tuple([_HBM] * (2 * nb)),
        input_output_aliases={i: i for i in range(2 * nb)},
        compiler_params=pltpu.CompilerParams(has_side_effects=_EFFECT),
    )(*bufs, *lands, send_sems, recv_sems, *after)
    return list(out[:nb]), list(out[nb:])


def _forward_copies(land_ref, send_sems, recv_sems, sending):
    x, y, c = lax.axis_index("x"), lax.axis_index("y"), lax.axis_index("c")
    cps = []
    for ki, k in enumerate(OTHER_CHIPS):
        px = 1 - x if k & 4 else x
        py = 1 - y if k & 2 else y
        q = 4 * px + 2 * py + (c if sending else 1 - c)
        cps.append(pltpu.make_async_remote_copy(
            src_ref=land_ref.at[q], dst_ref=land_ref.at[q], send_sem=send_sems.at[ki], recv_sem=recv_sems.at[ki],
            device_id=(x, y, 1 - c), device_id_type=pl.DeviceIdType.MESH))
    return cps


def _forward_start(land, name):
    def body(land_ref, send_sems, recv_sems, land_thru, token):
        for cp in _forward_copies(land_ref, send_sems, recv_sems, True):
            cp.start()
        token[...] = jnp.zeros_like(token)

    n = len(OTHER_CHIPS)
    out = pl.pallas_call(
        body, name=name,
        out_shape=(pltpu.SemaphoreType.DMA((n,)), pltpu.SemaphoreType.DMA((n,)), pltpu.HBM(land.shape, land.dtype),
                   S((8, LANES), F32)),
        in_specs=[_HBM], out_specs=(_SEM, _SEM, _HBM, pl.BlockSpec(memory_space=pltpu.VMEM)),
        input_output_aliases={0: 2},
        compiler_params=pltpu.CompilerParams(has_side_effects=_EFFECT),
    )(pltpu.with_memory_space_constraint(land, pltpu.HBM))
    return out


def _forward_wait(started, after, name):
    send_sems, recv_sems, land, _ = started
    after = list(after) if isinstance(after, (list, tuple)) else [after]

    def body(land_ref, s_sems, r_sems, *rest):
        for cp in _forward_copies(land_ref, s_sems, r_sems, False):
            cp.wait_send()
            cp.wait_recv()

    return pl.pallas_call(
        body, name=name, out_shape=pltpu.HBM(land.shape, land.dtype),
        in_specs=[_HBM, _SEM, _SEM] + [pl.BlockSpec(memory_space=pl.ANY)] * len(after), out_specs=_HBM,
        input_output_aliases={0: 0},
        compiler_params=pltpu.CompilerParams(has_side_effects=_EFFECT),
    )(land, send_sems, recv_sems, *after)


def _sum_slabs(recv, name):
    n, R, _ = recv.shape

    def body(r_ref, o_ref):
        g = r_ref[0].astype(F32)
        for s in range(1, n):
            g = g + r_ref[s].astype(F32)
        o_ref[...] = g

    return pl.pallas_call(body, name=name, out_shape=S((R, LANES), F32))(recv)


def _adamw(recv, w, m, v, name, tr):
    n, R, C = recv.shape
    c1 = 1.0 - ADAM_B1 ** ADAM_STEP
    c2 = 1.0 - ADAM_B2 ** ADAM_STEP

    def body(r_ref, w_ref, m_ref, v_ref, g_out, d_out, m_out, v_out):
        g = r_ref[0].astype(F32)
        for s in range(1, n):
            g = g + r_ref[s].astype(F32)
        mm = ADAM_B1 * m_ref[...] + (1.0 - ADAM_B1) * g
        vv = ADAM_B2 * v_ref[...] + (1.0 - ADAM_B2) * (g * g)
        g_out[...] = g
        m_out[...] = mm
        v_out[...] = vv
        d_out[...] = -ADAM_LR * ((mm / c1) / (jnp.sqrt(vv / c2) + ADAM_EPS) + ADAM_WD * w_ref[...])

    spec = pl.BlockSpec((tr, C), lambda i: (i, 0))
    return pl.pallas_call(
        body, name=name, grid=(R // tr,),
        in_specs=[pl.BlockSpec((n, tr, C), lambda i: (0, i, 0)), spec, spec, spec],
        out_specs=[spec] * 4, out_shape=[S((R, C), F32)] * 4,
        compiler_params=_cparams("parallel"),
    )(recv, w, m, v)


def _pack(parts, rows):
    cols = []
    for p in parts:
        f = p.reshape(-1)
        cols.append(jnp.pad(f, (0, (-f.shape[0]) % LANES)))
    flat = jnp.concatenate(cols)
    return jnp.pad(flat, (0, rows * LANES - flat.shape[0])).reshape(rows, LANES)


def _unpack(buf, shapes):
    flat = buf.reshape(-1)
    out, o = [], 0
    for shp in shapes:
        n = int(np.prod(shp))
        out.append(flat[o:o + n].reshape(shp))
        o += n + (-n) % LANES
    return out


SMALL_ROWS = 200
CONV_ROWS = 16


def kernel(x, pre_mix_norm_w, w_in, ssd_conv_w, ssd_conv_b, ssd_dt_bias, ssd_a_log, ssd_d, ssd_norm_w, ret_norm_w, w_out, post_mix_norm_w, pre_ffn_norm_w, w_up, ffn_conv_w, ffn_conv_b, w_down, post_ffn_norm_w, loss_target, m_pre_mix_norm_w, m_w_in, m_ssd_conv_w, m_ssd_conv_b, m_ssd_dt_bias, m_ssd_a_log, m_ssd_d, m_ssd_norm_w, m_ret_norm_w, m_w_out, m_post_mix_norm_w, m_pre_ffn_norm_w, m_w_up, m_ffn_conv_w, m_ffn_conv_b, m_w_down, m_post_ffn_norm_w, v_pre_mix_norm_w, v_w_in, v_ssd_conv_w, v_ssd_conv_b, v_ssd_dt_bias, v_ssd_a_log, v_ssd_d, v_ssd_norm_w, v_ret_norm_w, v_w_out, v_post_mix_norm_w, v_pre_ffn_norm_w, v_w_up, v_ffn_conv_w, v_ffn_conv_b, v_w_down, v_post_ffn_norm_w):
    T = x.shape[1]
    xi, tgt = x[0], loss_target[0]
    me = 4 * lax.axis_index("x") + 2 * lax.axis_index("y") + lax.axis_index("c")
    n_in, n_up = w_in.shape[2], w_up.shape[2]
    n_out, n_down = w_out.shape[1], w_down.shape[1]
    n_sc, n_fc = ssd_conv_w.shape[2], ffn_conv_w.shape[2]

    def after(token, value):
        return value * (1.0 + token[0, 0])

    def finish(started, after_value, name, same):
        bufs, lands = _exchange_wait(started, after_value, name, same)
        own = [b if same else lax.dynamic_index_in_dim(b, me, 0, keepdims=False) for b in bufs]
        return [lax.dynamic_update_index_in_dim(l, o, me, 0) for l, o in zip(lands, own)]

    (gconv,), tok_conv = _exchange([_pack([ssd_conv_w, ffn_conv_w], CONV_ROWS)], "gather_conv", True)
    gat_in = _exchange_start([after(tok_conv, w_in[0]).astype(BF)], "gather_in_start", True, ONE_PER_CHIP)
    convs = [_unpack(gconv[d], [(SSD_CONV, n_sc), (FFN_CONV, n_fc)]) for d in range(N_DEV)]
    scw = jnp.pad(jnp.concatenate([c[0] for c in convs], axis=1), ((0, 8 - SSD_CONV), (0, 0)))
    fcw = jnp.pad(jnp.concatenate([c[1] for c in convs], axis=1), ((0, 8 - FFN_CONV), (0, 0)))
    pad_h = lambda p: jnp.pad(p, ((0, 0), (0, LANES - SSD_HEADS)))
    dtb, alog = pad_h(ssd_dt_bias), pad_h(ssd_a_log)
    dskx = jnp.repeat(ssd_d, SSD_HEAD_DIM, axis=1)
    inv = ROPE_BASE ** (-jnp.arange(0, RET_QK, 2, dtype=F32) / RET_QK)
    ang = jnp.arange(T, dtype=F32)[:, None] * inv[None, :]
    cs_, sn_ = jnp.cos(ang), jnp.sin(ang)
    cos = jnp.concatenate([cs_, cs_, cs_, cs_], axis=1)
    sin = jnp.concatenate([-sn_, sn_, -sn_, sn_], axis=1)
    shard_in, land_in = _exchange_wait(gat_in, [cos, sin, scw, fcw], "gather_in_wait", True, ONE_PER_CHIP)
    fwd_in_ = _forward_start(land_in[0], "gather_in_forward")
    gat_rest = _exchange_start([after(fwd_in_[3], w[0]).astype(BF) for w in (w_out, w_up, w_down)], "gather_rest_start", True)
    g_in = lax.dynamic_update_index_in_dim(_forward_wait(fwd_in_, gat_rest[4], "gather_in_forward_wait"), shard_in[0], me, 0)
    win =jnp.transpose(g_in, (1, 0, 2)).reshape(D_MODEL, N_DEV * n_in)
    wp = jnp.concatenate([win[:, O_Z:O_XBC], win[:, O_XBC:O_DT], win[:, O_Q:O_K], win[:, O_K:O_V], win[:, O_V:O_G],
                          win[:, O_G:], win[:, O_DT:O_Q], jnp.zeros((D_MODEL, P_END - P_DT - SSD_HEADS), win.dtype)], axis=1)

    h, z, xbc, q, k, v, g, dtr = _fwd_in(xi, pre_mix_norm_w, wp)
    ys, ypre, sst = _ssd_fwd(xbc, dtr, z, scw, ssd_conv_b, dtb, alog, dskx, ssd_norm_w)
    yr, rst = _ret_fwd(q, k, v, g, cos, sin, ret_norm_w)
    g_out, g_up, g_down = finish(gat_rest, yr, "gather_rest_wait", True)
    wout = g_out.reshape(N_DEV * n_out, D_MODEL)
    wup = jnp.transpose(g_up, (1, 0, 2)).reshape(D_MODEL, N_DEV * n_up)
    wdown = g_down.reshape(N_DEV * n_down, D_MODEL)
    y, x1, h2, graw, val = _fwd_mid(ys, yr, xi, wout, post_mix_norm_w, pre_ffn_norm_w, wup)
    a, dfb, dval, dgate, dx2, lossb, d_pff, d_fcb = _ffn_tail(graw, val, x1, tgt, fcw, ffn_conv_b, wdown, post_ffn_norm_w)
    gdown = _matmul_tn(a, dfb, "dw_down")
    sc_down = _exchange_start([gdown.reshape(N_DEV, n_down, D_MODEL).astype(BF)], "scatter_down_start", False)
    dgraw, dx1, dyb, dys, dyr, d_fcw, d_pf, d_pm = _ffn_bwd(dgate, dval, graw, x1, dx2, y, after(sc_down[4], fcw), wup,
                                                         pre_ffn_norm_w, post_mix_norm_w, wout)
    gup = jnp.concatenate([_matmul_tn(h2, dgraw, "dw_up_g"), _matmul_tn(h2, dval, "dw_up_v")], axis=1)
    gout = jnp.concatenate([_matmul_tn(ys, dyb, "dw_out_s"), _matmul_tn(yr, dyb, "dw_out_r")], axis=0)
    sc_mid = _exchange_start([jnp.transpose(gup.reshape(D_MODEL, N_DEV, n_up), (1, 0, 2)).astype(BF),
                              gout.reshape(N_DEV, n_out, D_MODEL).astype(BF)], "scatter_mid_start", False)
    dz, dxbc, ddt, d_scw, d_scb, d_dtb, d_alog, d_dsk, d_snw = _ssd_bwd(dys, ypre, xbc, dtr, z, sst, after(sc_mid[4], scw),
                                                                      ssd_conv_b, dtb, alog, dskx, ssd_norm_w)
    dq, dk, dv, dg, d_rnw = _ret_bwd(dyr, q, k, v, g, rst, cos, sin, ret_norm_w)
    gin = jnp.concatenate([_matmul_tn(h, dz, "dw_z"), _matmul_tn(h, dxbc, "dw_xbc"),
                           _matmul_tn(h, ddt, "dw_dt")[:, :SSD_HEADS], _matmul_tn(h, dq, "dw_q"), _matmul_tn(h, dk, "dw_k"),
                           _matmul_tn(h, dv, "dw_v"), _matmul_tn(h, dg, "dw_g")], axis=1)
    sc_in = _exchange_start([jnp.transpose(gin.reshape(D_MODEL, N_DEV, n_in), (1, 0, 2)).astype(BF)], "scatter_in_start", False)
    gx, d_w0 = _in_bwd(dz, dxbc, dq, dk, dv, dg, ddt, xi, dx1, after(sc_in[4], pre_mix_norm_w), wp)
    r_down, = finish(sc_down, gx, "scatter_down_wait", False)
    r_up, r_out = finish(sc_mid, r_down, "scatter_mid_wait", False)
    per_w = [None] * 4
    per_w[3] = _adamw(r_down, w_down[0], m_w_down[0], v_w_down[0], "adamw_down", n_down)
    per_w[2] = _adamw(r_up, w_up[0], m_w_up[0], v_w_up[0], "adamw_up", 256)
    per_w[1] = _adamw(r_out, w_out[0], m_w_out[0], v_w_out[0], "adamw_out", n_out)
    r_in, = finish(sc_in, per_w[1][0], "scatter_in_wait", False)
    per_w[0] = _adamw(r_in, w_in[0], m_w_in[0], v_w_in[0], "adamw_in", 256)
    big = [[per_w[i][kind][None] for i in range(4)] for kind in range(4)]

    small_full = [d_w0, d_scw[:SSD_CONV], d_scb, d_dtb[:, :SSD_HEADS], d_alog[:, :SSD_HEADS], d_dsk[:, :SSD_HEADS], d_snw, d_rnw,
                  d_pm, d_pf, d_fcw[:FFN_CONV], d_fcb, d_pff]
    full_shapes = [t.shape for t in small_full]
    gs = _sum_slabs(_exchange([_pack(small_full, SMALL_ROWS)], "gather_small", True)[0][0], "sum_small")
    gfull = _unpack(gs, full_shapes)
    gfull[1] = lax.dynamic_slice_in_dim(gfull[1], me * n_sc, n_sc, axis=1)
    gfull[10] = lax.dynamic_slice_in_dim(gfull[10], me * n_fc, n_fc, axis=1)
    ws = [pre_mix_norm_w, ssd_conv_w, ssd_conv_b, ssd_dt_bias, ssd_a_log, ssd_d, ssd_norm_w, ret_norm_w, post_mix_norm_w,
          pre_ffn_norm_w, ffn_conv_w, ffn_conv_b, post_ffn_norm_w]
    ms = [m_pre_mix_norm_w, m_ssd_conv_w, m_ssd_conv_b, m_ssd_dt_bias, m_ssd_a_log, m_ssd_d, m_ssd_norm_w, m_ret_norm_w,
          m_post_mix_norm_w, m_pre_ffn_norm_w, m_ffn_conv_w, m_ffn_conv_b, m_post_ffn_norm_w]
    vs = [v_pre_mix_norm_w, v_ssd_conv_w, v_ssd_conv_b, v_ssd_dt_bias, v_ssd_a_log, v_ssd_d, v_ssd_norm_w, v_ret_norm_w,
          v_post_mix_norm_w, v_pre_ffn_norm_w, v_ffn_conv_w, v_ffn_conv_b, v_post_ffn_norm_w]
    out_shapes = [t.shape for t in ws]
    small = _adamw(_pack(gfull, SMALL_ROWS)[None], _pack(ws, SMALL_ROWS), _pack(ms, SMALL_ROWS), _pack(vs, SMALL_ROWS),
                   "adamw_small", SMALL_ROWS)
    small = [_unpack(b, out_shapes) for b in small]

    order = {"pre_mix_norm_w": ("s", 0), "w_in": ("b", 0), "ssd_conv_w": ("s", 1), "ssd_conv_b": ("s", 2),
             "ssd_dt_bias": ("s", 3), "ssd_a_log": ("s", 4), "ssd_d": ("s", 5), "ssd_norm_w": ("s", 6), "ret_norm_w": ("s", 7),
             "w_out": ("b", 1), "post_mix_norm_w": ("s", 8), "pre_ffn_norm_w": ("s", 9), "w_up": ("b", 2),
             "ffn_conv_w": ("s", 10), "ffn_conv_b": ("s", 11), "w_down": ("b", 3), "post_ffn_norm_w": ("s", 12)}
    loss = lax.psum(lossb[0, 0], ("x", "y", "c"))
    outs = [loss, gx[None]]
    for kind in range(4):
        for name, (grp, idx) in order.items():
            outs.append(big[kind][idx] if grp == "b" else small[kind][idx])
    return tuple(outs)
```

```python
import functools
import math

import numpy as np
import jax
import jax.numpy as jnp
from jax import lax
from jax.experimental import pallas as pl
from jax.experimental.pallas import tpu as pltpu

F32 = jnp.float32
BF = jnp.bfloat16
HI = lax.Precision.HIGHEST
S = jax.ShapeDtypeStruct

D_MODEL = 1024
SSD_HEADS = 16
SSD_HEAD_DIM = 64
SSD_GROUPS = 2
SSD_STATE = 128
SSD_WIDTH = 1024
SSD_XBC = 1536
SSD_CONV = 4
RET_HEADS = 8
RET_QK = 64
RET_V = 128
RET_QK_W = 512
RET_V_W = 1024
ROPE_BASE = 10000.0
CH = 128
D_FF = 2816
FFN_CONV = 3
EPS = 1e-6
IN_WIDTH = 5648
N_DEV = 8

ADAM_LR = 0.001
ADAM_B1 = 0.9
ADAM_B2 = 0.999
ADAM_EPS = 1e-08
ADAM_WD = 0.01
ADAM_STEP = 10

LANES = 128
HALO = 16
VMEM_LIMIT = 48 * 1024 * 1024

O_Z, O_XBC, O_DT, O_Q, O_K, O_V, O_G, O_END = 0, 1024, 2560, 2576, 3088, 3600, 4624, 5648
IN_SEGMENTS = ((O_Z, O_XBC), (O_XBC, O_DT), (O_Q, O_K), (O_K, O_V), (O_V, O_G), (O_G, O_END))


def _cparams(*sem):
    return pltpu.CompilerParams(dimension_semantics=sem, vmem_limit_bytes=VMEM_LIMIT)


def _dot(a, b):
    return jnp.dot(a.astype(BF), b.astype(BF), preferred_element_type=F32)


def _dot_nt(a, b):
    return lax.dot_general(a.astype(BF), b.astype(BF), (((1,), (1,)), ((), ())), preferred_element_type=F32)


def _dot_tn(a, b):
    return lax.dot_general(a.astype(BF), b.astype(BF), (((0,), (0,)), ((), ())), preferred_element_type=F32)


def _dot_hi(a, b):
    return jnp.dot(a, b, preferred_element_type=F32, precision=HI)


def _dot_tn_hi(a, b):
    return lax.dot_general(a, b, (((0,), (0,)), ((), ())), preferred_element_type=F32, precision=HI)


def _sigmoid(x):
    return jax.nn.sigmoid(x)


def _dsilu(x, s):
    return s * (1.0 + x * (1.0 - s))


def _softplus(x):
    return jnp.maximum(x, 0.0) + jnp.log1p(jnp.exp(-jnp.abs(x)))


def _rstd(x):
    return lax.rsqrt(jnp.mean(x * x, axis=-1, keepdims=True) + EPS)


def _rms_bwd(dy, x, r, w):
    gn = dy * w
    dx = r * gn - x * (r * r * r) * jnp.mean(gn * x, axis=-1, keepdims=True)
    dw = jnp.sum(dy * x * r, axis=0, keepdims=True)
    return dx, dw


def _rows_before(ext, s, head, n):
    if s == 0:
        return ext[head:head + n]
    return pltpu.roll(ext, s, 0)[head:head + n]


def _rows_after(ext, s, n):
    if s == 0:
        return ext[0:n]
    return pltpu.roll(ext, ext.shape[0] - s, 0)[0:n]


def _row_spec(tm, width):
    return pl.BlockSpec((tm, width), lambda i: (i, 0))


def _const_spec(shape):
    return pl.BlockSpec(shape, lambda i: (0,) * len(shape))


_VMEM_WHOLE = pl.BlockSpec(memory_space=pltpu.VMEM)


def _fwd_in(x, w0, wt, wdt, tm=256):
    T = x.shape[0]

    def body(x_ref, w0_ref, wt_ref, wdt_ref, h_ref, z_ref, xbc_ref, q_ref, k_ref, v_ref, g_ref, dt_ref):
        xf = x_ref[...]
        h = (xf * _rstd(xf) * w0_ref[...]).astype(BF)
        h_ref[...] = h
        for ref, (lo, hi) in zip((z_ref, xbc_ref, q_ref, k_ref, v_ref, g_ref), IN_SEGMENTS):
            ref[...] = _dot_nt(h, wt_ref[lo:hi, :]).astype(ref.dtype)
        dt_ref[...] = _dot_nt(h, wdt_ref[...])

    widths = (D_MODEL, SSD_WIDTH, SSD_XBC, RET_QK_W, RET_QK_W, RET_V_W, RET_V_W)
    return pl.pallas_call(
        body, name="fwd_in", grid=(T // tm,),
        in_specs=[_row_spec(tm, D_MODEL), _const_spec((1, D_MODEL)), _VMEM_WHOLE, _VMEM_WHOLE],
        out_specs=[_row_spec(tm, w) for w in widths] + [_row_spec(tm, LANES)],
        out_shape=[S((T, w), BF) for w in widths] + [S((T, LANES), F32)],
        compiler_params=_cparams("parallel"),
    )(x, w0, wt, wdt)


def _fwd_mid(ys, yr, x, wout, wpm, wpf, wup, tm=256):
    T = x.shape[0]

    def body(ys_ref, yr_ref, x_ref, wout_ref, wpm_ref, wpf_ref, wup_ref, y_ref, x1_ref, h2_ref, graw_ref, val_ref):
        y = (jnp.dot(ys_ref[...], wout_ref[0:SSD_WIDTH, :], preferred_element_type=F32)
             + jnp.dot(yr_ref[...], wout_ref[SSD_WIDTH:, :], preferred_element_type=F32))
        y_ref[...] = y
        x1 = x_ref[...] + y * _rstd(y) * wpm_ref[...]
        x1_ref[...] = x1
        h2 = (x1 * _rstd(x1) * wpf_ref[...]).astype(BF)
        h2_ref[...] = h2
        graw_ref[...] = _dot_nt(h2, wup_ref[0:D_FF, :]).astype(BF)
        val_ref[...] = _dot_nt(h2, wup_ref[D_FF:, :]).astype(BF)

    return pl.pallas_call(
        body, name="fwd_mid", grid=(T // tm,),
        in_specs=[_row_spec(tm, SSD_WIDTH), _row_spec(tm, RET_V_W), _row_spec(tm, D_MODEL), _VMEM_WHOLE,
                  _const_spec((1, D_MODEL)), _const_spec((1, D_MODEL)), _VMEM_WHOLE],
        out_specs=[_row_spec(tm, D_MODEL), _row_spec(tm, D_MODEL), _row_spec(tm, D_MODEL), _row_spec(tm, D_FF),
                   _row_spec(tm, D_FF)],
        out_shape=[S((T, D_MODEL), F32), S((T, D_MODEL), F32), S((T, D_MODEL), BF), S((T, D_FF), BF), S((T, D_FF), BF)],
        compiler_params=_cparams("parallel"),
    )(ys, yr, x, wout, wpm, wpf, wup)


def _ffn_tail(graw, val, x1, tgt, convw, convb, wdown, wpff, tm=256):
    T = x1.shape[0]

    def body(graw_ref, val_ref, x1_ref, tgt_ref, cw_ref, cb_ref, wd_ref, wpff_ref,
             a_ref, df_ref, dval_ref, dgate_ref, dx2_ref, loss_ref, dwpff_ref, dcb_ref, carry):
        i = pl.program_id(0)

        @pl.when(i == 0)
        def _():
            carry[...] = jnp.zeros_like(carry)
            loss_ref[...] = jnp.zeros_like(loss_ref)
            dwpff_ref[...] = jnp.zeros_like(dwpff_ref)
            dcb_ref[...] = jnp.zeros_like(dcb_ref)

        g = graw_ref[...].astype(F32)
        ext = jnp.concatenate([carry[...], g], axis=0)
        carry[...] = g[tm - 8:tm]
        gate = cb_ref[...] + sum(cw_ref[j:j + 1, :] * _rows_before(ext, FFN_CONV - 1 - j, 8, tm) for j in range(FFN_CONV))
        sg = _sigmoid(gate)
        silu = gate * sg
        v = val_ref[...].astype(F32)
        a = (silu * v).astype(BF)
        a_ref[...] = a
        f = jnp.dot(a, wd_ref[...], preferred_element_type=F32)
        r = _rstd(f)
        w = wpff_ref[...]
        e = x1_ref[...] + f * r * w - tgt_ref[...]
        loss_ref[...] += jnp.sum(e * e) * (0.5 / D_MODEL)
        dx2 = e * (1.0 / D_MODEL)
        dx2_ref[...] = dx2
        df, dw = _rms_bwd(dx2, f, r, w)
        dwpff_ref[...] += dw
        dfb = df.astype(BF)
        df_ref[...] = dfb
        da = _dot_nt(dfb, wd_ref[...])
        dval_ref[...] = (da * silu).astype(BF)
        dgate = da * v * _dsilu(gate, sg)
        dcb_ref[...] += jnp.sum(dgate, axis=0, keepdims=True)
        dgate_ref[...] = dgate.astype(BF)

    return pl.pallas_call(
        body, name="ffn_tail", grid=(T // tm,),
        in_specs=[_row_spec(tm, D_FF), _row_spec(tm, D_FF), _row_spec(tm, D_MODEL), _row_spec(tm, D_MODEL),
                  _const_spec((8, D_FF)), _const_spec((1, D_FF)), _VMEM_WHOLE, _const_spec((1, D_MODEL))],
        out_specs=[_row_spec(tm, D_FF), _row_spec(tm, D_MODEL), _row_spec(tm, D_FF), _row_spec(tm, D_FF),
                   _row_spec(tm, D_MODEL), _const_spec((8, LANES)), _const_spec((1, D_MODEL)), _const_spec((1, D_FF))],
        out_shape=[S((T, D_FF), BF), S((T, D_MODEL), BF), S((T, D_FF), BF), S((T, D_FF), BF), S((T, D_MODEL), F32),
                   S((8, LANES), F32), S((1, D_MODEL), F32), S((1, D_FF), F32)],
        scratch_shapes=[pltpu.VMEM((8, D_FF), F32)],
        compiler_params=_cparams("arbitrary"),
    )(graw, val, x1, tgt, convw, convb, wdown, wpff)


def _ffn_bwd(dgate, dval, graw, x1, dx2, y, convw, wup, wpf, wpm, wout, tm=256):
    T = x1.shape[0]
    nt = T // tm
    rev = lambda i: (nt - 1 - i, 0)
    rspec = lambda w: pl.BlockSpec((tm, w), rev)

    def body(dgate_ref, dval_ref, graw_ref, x1_ref, dx2_ref, y_ref, cw_ref, wup_ref, wpf_ref, wpm_ref, wout_ref,
             dgraw_ref, dx1_ref, dy_ref, dys_ref, dyr_ref, dcw_ref, dwpf_ref, dwpm_ref, carry):
        i = pl.program_id(0)

        @pl.when(i == 0)
        def _():
            carry[...] = jnp.zeros_like(carry)
            dcw_ref[...] = jnp.zeros_like(dcw_ref)
            dwpf_ref[...] = jnp.zeros_like(dwpf_ref)
            dwpm_ref[...] = jnp.zeros_like(dwpm_ref)

        dg = dgate_ref[...].astype(F32)
        ext = jnp.concatenate([dg, carry[...]], axis=0)
        carry[...] = dg[0:8]
        g = graw_ref[...].astype(F32)
        dgraw = jnp.zeros((tm, D_FF), F32)
        for j in range(FFN_CONV):
            sj = _rows_after(ext, FFN_CONV - 1 - j, tm)
            dgraw = dgraw + cw_ref[j:j + 1, :] * sj
            dcw_ref[j:j + 1, :] += jnp.sum(sj * g, axis=0, keepdims=True)
        dgrawb = dgraw.astype(BF)
        dgraw_ref[...] = dgrawb
        dh2 = _dot(dgrawb, wup_ref[0:D_FF, :]) + _dot(dval_ref[...], wup_ref[D_FF:, :])
        x1 = x1_ref[...]
        dxa, dw = _rms_bwd(dh2, x1, _rstd(x1), wpf_ref[...])
        dwpf_ref[...] += dw
        dx1 = dx2_ref[...] + dxa
        dx1_ref[...] = dx1
        yv = y_ref[...]
        dy, dw = _rms_bwd(dx1, yv, _rstd(yv), wpm_ref[...])
        dwpm_ref[...] += dw
        dyb = dy.astype(BF)
        dy_ref[...] = dyb
        dys_ref[...] = _dot_nt(dyb, wout_ref[0:SSD_WIDTH, :]).astype(BF)
        dyr_ref[...] = _dot_nt(dyb, wout_ref[SSD_WIDTH:, :]).astype(BF)

    return pl.pallas_call(
        body, name="ffn_bwd", grid=(nt,),
        in_specs=[rspec(D_FF), rspec(D_FF), rspec(D_FF), rspec(D_MODEL), rspec(D_MODEL), rspec(D_MODEL),
                  _const_spec((8, D_FF)), _VMEM_WHOLE, _const_spec((1, D_MODEL)), _const_spec((1, D_MODEL)), _VMEM_WHOLE],
        out_specs=[rspec(D_FF), rspec(D_MODEL), rspec(D_MODEL), rspec(SSD_WIDTH), rspec(RET_V_W),
                   _const_spec((8, D_FF)), _const_spec((1, D_MODEL)), _const_spec((1, D_MODEL))],
        out_shape=[S((T, D_FF), BF), S((T, D_MODEL), F32), S((T, D_MODEL), BF), S((T, SSD_WIDTH), BF), S((T, RET_V_W), BF),
                   S((8, D_FF), F32), S((1, D_MODEL), F32), S((1, D_MODEL), F32)],
        scratch_shapes=[pltpu.VMEM((8, D_FF), F32)],
        compiler_params=_cparams("arbitrary"),
    )(dgate, dval, graw, x1, dx2, y, convw, wup, wpf, wpm, wout)


def _in_bwd(dz, dxbc, dq, dk, dv, dg, ddt, x, dx1, w0, wt, wdt, tm=256):
    T = x.shape[0]

    def body(dz_ref, dxbc_ref, dq_ref, dk_ref, dv_ref, dg_ref, ddt_ref, x_ref, dx1_ref, w0_ref, wt_ref, wdt_ref, gx_ref, dw0_ref):
        @pl.when(pl.program_id(0) == 0)
        def _():
            dw0_ref[...] = jnp.zeros_like(dw0_ref)

        dh = _dot(ddt_ref[...], wdt_ref[...])
        for ref, (lo, hi) in zip((dz_ref, dxbc_ref, dq_ref, dk_ref, dv_ref, dg_ref), IN_SEGMENTS):
            dh = dh + _dot(ref[...], wt_ref[lo:hi, :])
        xf = x_ref[...]
        dx, dw = _rms_bwd(dh, xf, _rstd(xf), w0_ref[...])
        dw0_ref[...] += dw
        gx_ref[...] = dx1_ref[...] + dx

    widths = (SSD_WIDTH, SSD_XBC, RET_QK_W, RET_QK_W, RET_V_W, RET_V_W, LANES)
    return pl.pallas_call(
        body, name="in_bwd", grid=(T // tm,),
        in_specs=[_row_spec(tm, w) for w in widths] + [_row_spec(tm, D_MODEL), _row_spec(tm, D_MODEL),
                                                       _const_spec((1, D_MODEL)), _VMEM_WHOLE, _VMEM_WHOLE],
        out_specs=[_row_spec(tm, D_MODEL), _const_spec((1, D_MODEL))],
        out_shape=[S((T, D_MODEL), F32), S((1, D_MODEL), F32)],
        compiler_params=_cparams("arbitrary"),
    )(dz, dxbc, dq, dk, dv, dg, ddt, x, dx1, w0, wt, wdt)


DW_TILE_BYTES = 6 << 20


def _matmul_tn(a, b, name, tk=512):
    T, M = a.shape
    N = b.shape[1]
    tm_, tn = M, N
    while tm_ * tn * 4 > DW_TILE_BYTES:
        if tm_ >= tn and tm_ % 256 == 0:
            tm_ //= 2
        elif tn % 256 == 0:
            tn //= 2
        else:
            break
    nk = T // tk

    def body(a_ref, b_ref, o_ref):
        @pl.when(pl.program_id(2) == 0)
        def _():
            o_ref[...] = jnp.zeros_like(o_ref)

        o_ref[...] += _dot_tn(a_ref[...], b_ref[...])

    return pl.pallas_call(
        body, name=name, grid=(M // tm_, N // tn, nk),
        in_specs=[pl.BlockSpec((tk, tm_), lambda m, n, k: (k, m)), pl.BlockSpec((tk, tn), lambda m, n, k: (k, n))],
        out_specs=pl.BlockSpec((tm_, tn), lambda m, n, k: (m, n)),
        out_shape=S((M, N), F32),
        compiler_params=_cparams("parallel", "parallel", "arbitrary"),
    )(a, b)


def _tri(lower):
    r = lax.broadcasted_iota(jnp.int32, (CH, CH), 0)
    c = lax.broadcasted_iota(jnp.int32, (CH, CH), 1)
    return ((c <= r) if lower else (r <= c)).astype(F32)


def _onehot_row(h):
    return (lax.broadcasted_iota(jnp.int32, (1, LANES), 1) == h).astype(F32)


def _onehot_col(h):
    return (lax.broadcasted_iota(jnp.int32, (LANES, 1), 0) == h).astype(F32)


def _ssd_pre(xc_ref, xh_ref, dtr_ref, cw_ref, cb_ref, dtb_ref, alog_ref, first):
    xc = xc_ref[...].astype(F32)
    xh = jnp.where(first, 0.0, xh_ref[...].astype(F32))
    ext = jnp.concatenate([xh, xc], axis=0)
    u = cb_ref[...] + sum(cw_ref[j:j + 1, :] * _rows_before(ext, SSD_CONV - 1 - j, HALO, CH) for j in range(SSD_CONV))
    sg = _sigmoid(u)
    act = u * sg
    dt = _softplus(dtr_ref[...] + dtb_ref[...])
    a = -jnp.exp(alog_ref[...])
    da = dt * a
    cs = _dot_hi(_tri(True), da)
    cst = _dot_tn_hi(da, _tri(False))
    return xc, u, sg, act, dt, a, cs, cst


HPG = SSD_HEADS // SSD_GROUPS
GW = HPG * SSD_HEAD_DIM


def _expand_heads(src, buf):
    for h in range(SSD_HEADS):
        buf[:, h * SSD_HEAD_DIM:(h + 1) * SSD_HEAD_DIM] = jnp.broadcast_to(src[:, h:h + 1], (CH, SSD_HEAD_DIM))


def _ssd_expanded(act, dt, cs, dtx, csx):
    _expand_heads(dt, dtx)
    _expand_heads(cs, csx)
    csv = csx[...]
    last = csv[CH - 1:CH, :]
    e_exp = jnp.exp(csv)
    dec_exp = jnp.exp(last - csv)
    el_exp = jnp.exp(last)
    xs = act[:, 0:SSD_WIDTH]
    xdt = xs * dtx[...]
    return xs, xdt, xdt * dec_exp, e_exp, dec_exp, el_exp


def _decay_mats(h, cs, cst, transposed):
    r = lax.broadcasted_iota(jnp.int32, (CH, CH), 0)
    c = lax.broadcasted_iota(jnp.int32, (CH, CH), 1)
    c_col = cs[:, h:h + 1]
    c_row = cst[h:h + 1, :]
    if transposed:
        return jnp.exp(jnp.where(r <= c, c_row - c_col, -1e30))
    return jnp.exp(jnp.where(r >= c, c_col - c_row, -1e30))


def _ssd_specs(T):
    nc = T // CH
    return nc, [
        _row_spec(CH, SSD_XBC),
        pl.BlockSpec((HALO, SSD_XBC), lambda i: (jnp.maximum(i * (CH // HALO) - 1, 0), 0)),
        _row_spec(CH, LANES),
        _row_spec(CH, SSD_WIDTH),
    ]


def _groups(act):
    bm = [act[:, SSD_WIDTH + g * SSD_STATE:SSD_WIDTH + (g + 1) * SSD_STATE] for g in range(SSD_GROUPS)]
    o = SSD_WIDTH + SSD_GROUPS * SSD_STATE
    cm = [act[:, o + g * SSD_STATE:o + (g + 1) * SSD_STATE] for g in range(SSD_GROUPS)]
    return bm, cm


def _ssd_fwd(xbc, dtr, z, convw, convb, dtb, alog, dskx, nw):
    T = xbc.shape[0]
    nc, specs = _ssd_specs(T)

    def body(xc_ref, xh_ref, dtr_ref, z_ref, cw_ref, cb_ref, dtb_ref, alog_ref, dskx_ref, nw_ref,
             out_ref, y_ref, st_ref, state, ybuf, dtx, csx):
        i = pl.program_id(0)

        @pl.when(i == 0)
        def _():
            state[...] = jnp.zeros_like(state)

        xc, u, sg, act, dt, a, cs, cst = _ssd_pre(xc_ref, xh_ref, dtr_ref, cw_ref, cb_ref, dtb_ref, alog_ref, i == 0)
        xs, xdt, w, e_exp, dec_exp, el_exp = _ssd_expanded(act, dt, cs, dtx, csx)
        bm, cm = _groups(act)
        for g in range(SSD_GROUPS):
            gs = slice(g * GW, (g + 1) * GW)
            st = state[g]
            st_ref[0, g] = st
            cb = _dot_nt(cm[g], bm[g])
            ybuf[:, gs] = _dot(cm[g], st) * e_exp[:, gs] + xs[:, gs] * dskx_ref[:, gs]
            state[g] = st * el_exp[:, gs] + _dot_tn(bm[g], w[:, gs])
            for h in range(g * HPG, (g + 1) * HPG):
                sl = slice(h * SSD_HEAD_DIM, (h + 1) * SSD_HEAD_DIM)
                ybuf[:, sl] += _dot(cb * _decay_mats(h, cs, cst, False), xdt[:, sl])
        yv = ybuf[...]
        y_ref[...] = yv.astype(BF)
        zf = z_ref[...].astype(F32)
        gated = yv * (zf * _sigmoid(zf))
        out_ref[...] = (gated * _rstd(gated) * nw_ref[...]).astype(BF)

    st_spec = pl.BlockSpec((1, SSD_GROUPS, SSD_STATE, GW), lambda i: (i, 0, 0, 0))
    return pl.pallas_call(
        body, name="ssd_fwd", grid=(nc,),
        in_specs=specs + [_const_spec((8, SSD_XBC)), _const_spec((1, SSD_XBC)), _const_spec((1, LANES)),
                          _const_spec((1, LANES)), _const_spec((1, SSD_WIDTH)), _const_spec((1, SSD_WIDTH))],
        out_specs=[_row_spec(CH, SSD_WIDTH), _row_spec(CH, SSD_WIDTH), st_spec],
        out_shape=[S((T, SSD_WIDTH), BF), S((T, SSD_WIDTH), BF), S((nc, SSD_GROUPS, SSD_STATE, GW), F32)],
        scratch_shapes=[pltpu.VMEM((SSD_GROUPS, SSD_STATE, GW), F32), pltpu.VMEM((CH, SSD_WIDTH), F32),
                        pltpu.VMEM((CH, SSD_WIDTH), F32), pltpu.VMEM((CH, SSD_WIDTH), F32)],
        compiler_params=_cparams("arbitrary"),
    )(xbc, xbc, dtr, z, convw, convb, dtb, alog, dskx, nw)


def _ssd_bwd(dout, y, xbc, dtr, z, states, convw, convb, dtb, alog, dskx, nw):
    T = xbc.shape[0]
    nc = T // CH
    rev = lambda i: (nc - 1 - i, 0)
    rspec = lambda w: pl.BlockSpec((CH, w), rev)
    halo_spec = pl.BlockSpec((HALO, SSD_XBC), lambda i: (jnp.maximum((nc - 1 - i) * (CH // HALO) - 1, 0), 0))
    NB = SSD_WIDTH
    NC_ = SSD_WIDTH + SSD_GROUPS * SSD_STATE

    def body(do_ref, y_ref, xc_ref, xh_ref, dtr_ref, z_ref, st_ref, cw_ref, cb_ref, dtb_ref, alog_ref, dskx_ref, nw_ref,
             dz_ref, dxbc_ref, ddt_ref, dcw_ref, dcb_ref, ddtb_ref, dalog_ref, ddsk_ref, dnw_ref,
             dstate, ducarry, dtx, csx, dxdtbuf, dact):
        i = pl.program_id(0)

        @pl.when(i == 0)
        def _():
            dstate[...] = jnp.zeros_like(dstate)
            ducarry[...] = jnp.zeros_like(ducarry)
            for ref in (dcw_ref, dcb_ref, ddtb_ref, dalog_ref, ddsk_ref, dnw_ref):
                ref[...] = jnp.zeros_like(ref)

        xc, u, sg, act, dt, a, cs, cst = _ssd_pre(xc_ref, xh_ref, dtr_ref, cw_ref, cb_ref, dtb_ref, alog_ref, i == nc - 1)
        xs, xdt, w, e_exp, dec_exp, el_exp = _ssd_expanded(act, dt, cs, dtx, csx)
        bm, cm = _groups(act)
        yv = y_ref[...].astype(F32)
        zf = z_ref[...].astype(F32)
        sz = _sigmoid(zf)
        gated = yv * (zf * sz)
        dgated, dnw = _rms_bwd(do_ref[...].astype(F32), gated, _rstd(gated), nw_ref[...])
        dnw_ref[...] += dnw
        dz_ref[...] = (dgated * yv * _dsilu(zf, sz)).astype(BF)
        dy = dgated * (zf * sz)
        lane_of = lax.broadcasted_iota(jnp.int32, (SSD_WIDTH, LANES), 0) - SSD_HEAD_DIM * lax.broadcasted_iota(jnp.int32, (SSD_WIDTH, LANES), 1)
        expt = ((lane_of >= 0) & (lane_of < SSD_HEAD_DIM)).astype(F32)
        ddsk_ref[...] += _dot_hi(jnp.sum(dy * xs, axis=0, keepdims=True), expt)
        dcs = jnp.zeros((CH, LANES), F32)
        dcst = jnp.zeros((LANES, CH), F32)
        ddt = jnp.zeros((CH, LANES), F32)
        lastrows = []
        for g in range(SSD_GROUPS):
            gs = slice(g * GW, (g + 1) * GW)
            st = st_ref[0, g]
            dsn = dstate[g]
            cbm = _dot_nt(cm[g], bm[g])
            cbt = _dot_nt(bm[g], cm[g])
            dy_g = dy[:, gs]
            yoff = _dot(cm[g], st) * e_exp[:, gs]
            dq = dy_g * e_exp[:, gs]
            dcm_g = _dot_nt(dq, st)
            dstate[g] = _dot_tn(cm[g], dq) + dsn * el_exp[:, gs]
            dw = _dot(bm[g], dsn)
            w_g = w[:, gs]
            dbm_g = _dot_nt(w_g, dsn)
            dww = dw * w_g
            red = dy_g * yoff - dww
            lastrows.append(jnp.sum(dsn * st, axis=0, keepdims=True) * el_exp[:, gs] + jnp.sum(dww, axis=0, keepdims=True))
            dxdtbuf[:, gs] = dw * dec_exp[:, gs]
            dcb = jnp.zeros((CH, CH), F32)
            for h in range(g * HPG, (g + 1) * HPG):
                sl = slice(h * SSD_HEAD_DIM, (h + 1) * SSD_HEAD_DIM)
                rl = slice((h - g * HPG) * SSD_HEAD_DIM, (h - g * HPG + 1) * SSD_HEAD_DIM)
                oh = _onehot_row(h)
                lmat = _decay_mats(h, cs, cst, False)
                mmat = cbm * lmat
                dy_h = dy[:, sl]
                dm = _dot_nt(dy_h, xdt[:, sl])
                dxdt_h = dxdtbuf[:, sl] + _dot(cbt * _decay_mats(h, cs, cst, True), dy_h)
                dxdtbuf[:, sl] = dxdt_h
                dseg = dm * mmat
                dcb = dcb + dm * lmat
                col = jnp.sum(dseg, axis=1, keepdims=True) + jnp.sum(red[:, rl], axis=1, keepdims=True)
                dcs = dcs + col * oh
                dcst = dcst - _onehot_col(h) * jnp.sum(dseg, axis=0, keepdims=True)
                ddt = ddt + jnp.sum(dxdt_h * xs[:, sl], axis=1, keepdims=True) * oh
            dact[:, NB + g * SSD_STATE:NB + (g + 1) * SSD_STATE] = dbm_g + _dot_tn(dcb, cm[g])
            dact[:, NC_ + g * SSD_STATE:NC_ + (g + 1) * SSD_STATE] = dcm_g + _dot(dcb, bm[g])
        dact[:, 0:SSD_WIDTH] = dy * dskx_ref[...] + dxdtbuf[...] * dtx[...]
        dlast = _dot_hi(jnp.concatenate(lastrows, axis=1), expt)
        rows = lax.broadcasted_iota(jnp.int32, (CH, LANES), 0)
        dcs = dcs + _dot_tn_hi(dcst, jnp.eye(LANES, dtype=F32)) + jnp.where(rows == CH - 1, dlast, 0.0)
        dda = _dot_hi(_tri(False), dcs)
        dalog_ref[...] += jnp.sum(dda * dt, axis=0, keepdims=True) * a
        ddt = ddt + dda * a
        ddtr = ddt * _sigmoid(dtr_ref[...] + dtb_ref[...])
        ddtb_ref[...] += jnp.sum(ddtr, axis=0, keepdims=True)
        ddt_ref[...] = ddtr.astype(BF)
        du = dact[...] * _dsilu(u, sg)
        dcb_ref[...] += jnp.sum(du, axis=0, keepdims=True)
        ext = jnp.concatenate([du, ducarry[...]], axis=0)
        ducarry[...] = du[0:8]
        dx = jnp.zeros((CH, SSD_XBC), F32)
        for j in range(SSD_CONV):
            sj = _rows_after(ext, SSD_CONV - 1 - j, CH)
            dx = dx + cw_ref[j:j + 1, :] * sj
            dcw_ref[j:j + 1, :] += jnp.sum(sj * xc, axis=0, keepdims=True)
        dxbc_ref[...] = dx.astype(BF)

    return pl.pallas_call(
        body, name="ssd_bwd", grid=(nc,),
        in_specs=[rspec(SSD_WIDTH), rspec(SSD_WIDTH), rspec(SSD_XBC), halo_spec, rspec(LANES), rspec(SSD_WIDTH),
                  pl.BlockSpec((1, SSD_GROUPS, SSD_STATE, GW), lambda i: (nc - 1 - i, 0, 0, 0)),
                  _const_spec((8, SSD_XBC)), _const_spec((1, SSD_XBC)), _const_spec((1, LANES)),
                  _const_spec((1, LANES)), _const_spec((1, SSD_WIDTH)), _const_spec((1, SSD_WIDTH))],
        out_specs=[rspec(SSD_WIDTH), rspec(SSD_XBC), rspec(LANES),
                   _const_spec((8, SSD_XBC)), _const_spec((1, SSD_XBC)), _const_spec((1, LANES)),
                   _const_spec((1, LANES)), _const_spec((1, LANES)), _const_spec((1, SSD_WIDTH))],
        out_shape=[S((T, SSD_WIDTH), BF), S((T, SSD_XBC), BF), S((T, LANES), BF),
                   S((8, SSD_XBC), F32), S((1, SSD_XBC), F32), S((1, LANES), F32),
                   S((1, LANES), F32), S((1, LANES), F32), S((1, SSD_WIDTH), F32)],
        scratch_shapes=[pltpu.VMEM((SSD_GROUPS, SSD_STATE, GW), F32), pltpu.VMEM((8, SSD_XBC), F32),
                        pltpu.VMEM((CH, SSD_WIDTH), F32), pltpu.VMEM((CH, SSD_WIDTH), F32),
                        pltpu.VMEM((CH, SSD_WIDTH), F32), pltpu.VMEM((CH, SSD_XBC), F32)],
        compiler_params=_cparams("arbitrary"),
    )(dout, y, xbc, xbc, dtr, z, states, convw, convb, dtb, alog, dskx, nw)


def _log_gamma(h):
    return float(np.log1p(-np.exp2(np.float32(-5.0 - h)), dtype=np.float32))


def _swap_halves(t):
    n = t.shape[1]
    lane = lax.broadcasted_iota(jnp.int32, t.shape, 1)
    return jnp.where((lane & (RET_QK - 1)) < RET_QK // 2, pltpu.roll(t, n - RET_QK // 2, 1), pltpu.roll(t, RET_QK // 2, 1))


def _rot(t, cos, sin):
    return t * cos + _swap_halves(t) * sin


def _rot_t(d, cos, sin):
    return d * cos + _swap_halves(d * sin)


def _ret_consts(h):
    lg = _log_gamma(h)
    r = lax.broadcasted_iota(jnp.int32, (CH, CH), 0)
    c = lax.broadcasted_iota(jnp.int32, (CH, CH), 1)
    rel = (r - c).astype(F32)
    dmask = jnp.where(rel >= 0, jnp.exp(lg * jnp.maximum(rel, 0.0)), 0.0)
    dmask_t = jnp.where(rel <= 0, jnp.exp(lg * jnp.maximum(-rel, 0.0)), 0.0)
    pos = lax.broadcasted_iota(jnp.int32, (CH, 1), 0).astype(F32)
    kdec = jnp.exp(lg * (CH - 1.0 - pos))
    qdec = jnp.exp(lg * (pos + 1.0))
    pos_row = lax.broadcasted_iota(jnp.int32, (1, CH), 1).astype(F32)
    kdec_row = jnp.exp(lg * (CH - 1.0 - pos_row))
    qdec_row = jnp.exp(lg * (pos_row + 1.0))
    return dmask, dmask_t, kdec, qdec, kdec_row, qdec_row, math.exp(lg * CH)


def _ret_fwd(q, k, v, g, cos, sin, nw):
    T = q.shape[0]
    nc = T // CH

    def body(q_ref, k_ref, v_ref, g_ref, cos_ref, sin_ref, nw_ref, out_ref, st_ref, state):
        i = pl.program_id(0)

        @pl.when(i == 0)
        def _():
            state[...] = jnp.zeros_like(state)

        cosf = jnp.tile(cos_ref[...], (1, RET_QK_W // LANES))
        sinf = jnp.tile(sin_ref[...], (1, RET_QK_W // LANES))
        qr = _rot(q_ref[...].astype(F32), cosf, sinf)
        kr = _rot(k_ref[...].astype(F32), cosf, sinf) * (RET_QK ** -0.5)
        krt = kr.T
        st_ref[0] = state[...]
        o_all = []
        for h in range(RET_HEADS):
            dmask, dmask_t, kdec, qdec, kdec_row, qdec_row, gam = _ret_consts(h)
            qs = slice(h * RET_QK, (h + 1) * RET_QK)
            v_h = v_ref[:, h * RET_V:(h + 1) * RET_V]
            rprev = state[h]
            scores = _dot_nt(qr[:, qs], kr[:, qs]) * dmask
            o_all.append(_dot(scores, v_h) + _dot(qr[:, qs] * qdec, rprev))
            state[h] = rprev * gam + _dot(krt[qs, :] * kdec_row, v_h)
        for h in range(RET_HEADS):
            sl = slice(h * RET_V, (h + 1) * RET_V)
            o = o_all[h]
            gf = g_ref[:, sl].astype(F32)
            out_ref[:, sl] = (o * _rstd(o) * nw_ref[:, sl] * (gf * _sigmoid(gf))).astype(BF)

    return pl.pallas_call(
        body, name="ret_fwd", grid=(nc,),
        in_specs=[_row_spec(CH, RET_QK_W), _row_spec(CH, RET_QK_W), _row_spec(CH, RET_V_W), _row_spec(CH, RET_V_W),
                  _row_spec(CH, LANES), _row_spec(CH, LANES), _const_spec((1, RET_V_W))],
        out_specs=[_row_spec(CH, RET_V_W), pl.BlockSpec((1, RET_HEADS, RET_QK, RET_V), lambda i: (i, 0, 0, 0))],
        out_shape=[S((T, RET_V_W), BF), S((nc, RET_HEADS, RET_QK, RET_V), F32)],
        scratch_shapes=[pltpu.VMEM((RET_HEADS, RET_QK, RET_V), F32)],
        compiler_params=_cparams("arbitrary"),
    )(q, k, v, g, cos, sin, nw)


def _ret_bwd(dout, q, k, v, g, states, cos, sin, nw):
    T = q.shape[0]
    nc = T // CH
    rev = lambda i: (nc - 1 - i, 0)
    rspec = lambda w: pl.BlockSpec((CH, w), rev)

    def body(do_ref, q_ref, k_ref, v_ref, g_ref, st_ref, cos_ref, sin_ref, nw_ref,
             dq_ref, dk_ref, dv_ref, dg_ref, dnw_ref, dstate, dqbuf, dkbuf):
        i = pl.program_id(0)

        @pl.when(i == 0)
        def _():
            dstate[...] = jnp.zeros_like(dstate)
            dnw_ref[...] = jnp.zeros_like(dnw_ref)

        cosf = jnp.tile(cos_ref[...], (1, RET_QK_W // LANES))
        sinf = jnp.tile(sin_ref[...], (1, RET_QK_W // LANES))
        qr = _rot(q_ref[...].astype(F32), cosf, sinf)
        kr = _rot(k_ref[...].astype(F32), cosf, sinf) * (RET_QK ** -0.5)
        qrt = qr.T
        heads = range(RET_HEADS)
        qsl = [slice(h * RET_QK, (h + 1) * RET_QK) for h in heads]
        vsl = [slice(h * RET_V, (h + 1) * RET_V) for h in heads]
        scores_t, o_all, do_all = [], [], []
        for h in heads:
            dmask, dmask_t, kdec, qdec, kdec_row, qdec_row, gam = _ret_consts(h)
            q_h, k_h = qr[:, qsl[h]], kr[:, qsl[h]]
            scores = _dot_nt(q_h, k_h) * dmask
            scores_t.append(_dot_nt(k_h, q_h) * dmask_t)
            o_all.append(_dot(scores, v_ref[:, vsl[h]]) + _dot(q_h * qdec, st_ref[0, h]))
        for h in heads:
            o = o_all[h]
            rr = _rstd(o)
            of = o * rr
            gf = g_ref[:, vsl[h]].astype(F32)
            sgg = _sigmoid(gf)
            d_h = do_ref[:, vsl[h]].astype(F32)
            nw_h = nw_ref[:, vsl[h]]
            dg_ref[:, vsl[h]] = (d_h * of * nw_h * _dsilu(gf, sgg)).astype(BF)
            dt_ = d_h * (gf * sgg)
            dnw_ref[:, vsl[h]] += jnp.sum(dt_ * of, axis=0, keepdims=True)
            dof = dt_ * nw_h
            do_all.append(rr * dof - o * (rr * rr * rr) * jnp.mean(dof * o, axis=-1, keepdims=True))
        for h in heads:
            dmask, dmask_t, kdec, qdec, kdec_row, qdec_row, gam = _ret_consts(h)
            q_h, k_h, v_h, do = qr[:, qsl[h]], kr[:, qsl[h]], v_ref[:, vsl[h]], do_all[h]
            gnext = dstate[h]
            dsc = _dot_nt(do, v_h) * dmask
            dsc_t = _dot_nt(v_h, do) * dmask_t
            dv_ref[:, vsl[h]] = (_dot(scores_t[h], do) + _dot(k_h * kdec, gnext)).astype(BF)
            dqbuf[:, qsl[h]] = _dot(dsc, k_h) + _dot_nt(do, st_ref[0, h]) * qdec
            dkbuf[:, qsl[h]] = _dot(dsc_t, q_h) + _dot_nt(v_h, gnext) * kdec
            dstate[h] = gnext * gam + _dot(qrt[qsl[h], :] * qdec_row, do)
        dq_ref[...] = _rot_t(dqbuf[...], cosf, sinf).astype(BF)
        dk_ref[...] = (_rot_t(dkbuf[...], cosf, sinf) * (RET_QK ** -0.5)).astype(BF)

    return pl.pallas_call(
        body, name="ret_bwd", grid=(nc,),
        in_specs=[rspec(RET_V_W), rspec(RET_QK_W), rspec(RET_QK_W), rspec(RET_V_W), rspec(RET_V_W),
                  pl.BlockSpec((1, RET_HEADS, RET_QK, RET_V), lambda i: (nc - 1 - i, 0, 0, 0)),
                  rspec(LANES), rspec(LANES), _const_spec((1, RET_V_W))],
        out_specs=[rspec(RET_QK_W), rspec(RET_QK_W), rspec(RET_V_W), rspec(RET_V_W), _const_spec((1, RET_V_W))],
        out_shape=[S((T, RET_QK_W), BF), S((T, RET_QK_W), BF), S((T, RET_V_W), BF), S((T, RET_V_W), BF),
                   S((1, RET_V_W), F32)],
        scratch_shapes=[pltpu.VMEM((RET_HEADS, RET_QK, RET_V), F32), pltpu.VMEM((CH, RET_QK_W), F32),
                        pltpu.VMEM((CH, RET_QK_W), F32)],
        compiler_params=_cparams("arbitrary"),
    )(dout, q, k, v, g, states, cos, sin, nw)


def _exchange(bufs, name, same):
    nb = len(bufs)
    slabs = [tuple(b.shape if same else b.shape[1:]) for b in bufs]

    def body(*refs):
        buf_refs, out_refs, token = refs[:nb], refs[nb:2 * nb], refs[2 * nb]
        send_sems, recv_sems, local_sems = refs[2 * nb + 1:]
        x, y, c = lax.axis_index("x"), lax.axis_index("y"), lax.axis_index("c")
        me = 4 * x + 2 * y + c
        token[...] = jnp.zeros_like(token)

        def src(b, d):
            return buf_refs[b] if same else buf_refs[b].at[d]

        def remote(b, k, to_me):
            px = 1 - x if k & 4 else x
            py = 1 - y if k & 2 else y
            pc = 1 - c if k & 1 else c
            p = 4 * px + 2 * py + pc
            s = b * (N_DEV - 1) + k - 1
            return pltpu.make_async_remote_copy(
                src_ref=src(b, p), dst_ref=out_refs[b].at[me if to_me else p], send_sem=send_sems.at[s],
                recv_sem=recv_sems.at[s], device_id=(px, py, pc), device_id_type=pl.DeviceIdType.MESH)

        local = [pltpu.make_async_copy(src(b, me), out_refs[b].at[me], local_sems.at[b]) for b in range(nb)]
        for cp in local:
            cp.start()
        sends = [remote(b, k, True) for k in range(1, N_DEV) for b in range(nb)]
        for cp in sends:
            cp.start()
        for k in range(1, N_DEV):
            for b in range(nb):
                remote(b, k, False).wait_recv()
        for cp in sends:
            cp.wait_send()
        for cp in local:
            cp.wait()

    any_spec = pl.BlockSpec(memory_space=pl.ANY)
    out = pl.pallas_call(
        body, name=name,
        in_specs=[any_spec] * nb, out_specs=[any_spec] * nb + [pl.BlockSpec(memory_space=pltpu.VMEM)],
        out_shape=[S((N_DEV,) + s, b.dtype) for s, b in zip(slabs, bufs)] + [S((8, LANES), F32)],
        scratch_shapes=[pltpu.SemaphoreType.DMA((nb * (N_DEV - 1),)), pltpu.SemaphoreType.DMA((nb * (N_DEV - 1),)),
                        pltpu.SemaphoreType.DMA((nb,))],
    )(*bufs)
    return list(out[:nb]), out[nb]


_HBM = pl.BlockSpec(memory_space=pltpu.HBM)
_SEM = pl.BlockSpec(memory_space=pltpu.SEMAPHORE)
_EFFECT = pltpu.SideEffectType.DATAFLOW_SIDE_EFFECTING


ALL_PEERS = tuple(range(1, N_DEV))
ONE_PER_CHIP = (1, 2, 4, 6)
OTHER_CHIPS = (2, 4, 6)


def _split_copies(buf_refs, land_refs, send_sems, recv_sems, same, to_me, ks):
    x, y, c = lax.axis_index("x"), lax.axis_index("y"), lax.axis_index("c")
    me = 4 * x + 2 * y + c
    cps = []
    for ki, k in enumerate(ks):
        px = 1 - x if k & 4 else x
        py = 1 - y if k & 2 else y
        pc = 1 - c if k & 1 else c
        p = 4 * px + 2 * py + pc
        for b in range(len(buf_refs)):
            s = b * len(ks) + ki
            cps.append(pltpu.make_async_remote_copy(
                src_ref=buf_refs[b] if same else buf_refs[b].at[p], dst_ref=land_refs[b].at[me if to_me else p],
                send_sem=send_sems.at[s], recv_sem=recv_sems.at[s], device_id=(px, py, pc), device_id_type=pl.DeviceIdType.MESH))
    return cps


def _exchange_start(bufs, name, same, ks=ALL_PEERS):
    nb = len(bufs)
    ns = nb * len(ks)
    lands = [lax.empty((N_DEV,) + tuple(b.shape if same else b.shape[1:]), b.dtype) for b in bufs]

    def body(*refs):
        buf_refs, land_refs = refs[:nb], refs[nb:2 * nb]
        send_sems, recv_sems = refs[2 * nb], refs[2 * nb + 1]
        token = refs[-1]
        for cp in _split_copies(buf_refs, land_refs, send_sems, recv_sems, same, True, ks):
            cp.start()
        token[...] = jnp.zeros_like(token)

    hbm = lambda a: pltpu.with_memory_space_constraint(a, pltpu.HBM)
    out = pl.pallas_call(
        body, name=name,
        out_shape=(pltpu.SemaphoreType.DMA((ns,)), pltpu.SemaphoreType.DMA((ns,)),
                   *[pltpu.HBM(a.shape, a.dtype) for a in list(bufs) + lands], S((8, LANES), F32)),
        in_specs=[_HBM] * (2 * nb), out_specs=(_SEM, _SEM, *[_HBM] * (2 * nb), pl.BlockSpec(memory_space=pltpu.VMEM)),
        input_output_aliases={i: 2 + i for i in range(2 * nb)},
        compiler_params=pltpu.CompilerParams(has_side_effects=_EFFECT),
    )(*[hbm(a) for a in list(bufs) + lands])
    return out[0], out[1], list(out[2:2 + nb]), list(out[2 + nb:2 + 2 * nb]), out[-1]


def _exchange_wait(started, after, name, same, ks=ALL_PEERS):
    send_sems, recv_sems, bufs, lands, _ = started
    nb = len(bufs)
    after = list(after) if isinstance(after, (list, tuple)) else [after]

    def body(*refs):
        buf_refs, land_refs = refs[:nb], refs[nb:2 * nb]
        s_sems, r_sems = refs[2 * nb], refs[2 * nb + 1]
        for cp in _split_copies(buf_refs, land_refs, s_sems, r_sems, same, False, ks):
            cp.wait_send()
            cp.wait_recv()

    out = pl.pallas_call(
        body, name=name,
        out_shape=tuple(pltpu.HBM(a.shape, a.dtype) for a in bufs + lands),
        in_specs=[_HBM] * (2 * nb) + [_SEM, _SEM] + [pl.BlockSpec(memory_space=pl.ANY)] * len(after),
        out_specs=tuple([_HBM] * (2 * nb)),
        input_output_aliases={i: i for i in range(2 * nb)},
        compiler_params=pltpu.CompilerParams(has_side_effects=_EFFECT),
    )(*bufs, *lands, send_sems, recv_sems, *after)
    return list(out[:nb]), list(out[nb:])


def _forward_copies(land_ref, send_sems, recv_sems, sending):
    x, y, c = lax.axis_index("x"), lax.axis_index("y"), lax.axis_index("c")
    cps = []
    for ki, k in enumerate(OTHER_CHIPS):
        px = 1 - x if k & 4 else x
        py = 1 - y if k & 2 else y
        q = 4 * px + 2 * py + (c if sending else 1 - c)
        cps.append(pltpu.make_async_remote_copy(
            src_ref=land_ref.at[q], dst_ref=land_ref.at[q], send_sem=send_sems.at[ki], recv_sem=recv_sems.at[ki],
            device_id=(x, y, 1 - c), device_id_type=pl.DeviceIdType.MESH))
    return cps


def _forward_start(land, name):
    def body(land_ref, send_sems, recv_sems, land_thru, token):
        for cp in _forward_copies(land_ref, send_sems, recv_sems, True):
            cp.start()
        token[...] = jnp.zeros_like(token)

    n = len(OTHER_CHIPS)
    out = pl.pallas_call(
        body, name=name,
        out_shape=(pltpu.SemaphoreType.DMA((n,)), pltpu.SemaphoreType.DMA((n,)), pltpu.HBM(land.shape, land.dtype),
                   S((8, LANES), F32)),
        in_specs=[_HBM], out_specs=(_SEM, _SEM, _HBM, pl.BlockSpec(memory_space=pltpu.VMEM)),
        input_output_aliases={0: 2},
        compiler_params=pltpu.CompilerParams(has_side_effects=_EFFECT),
    )(pltpu.with_memory_space_constraint(land, pltpu.HBM))
    return out


def _forward_wait(started, after, name):
    send_sems, recv_sems, land, _ = started
    after = list(after) if isinstance(after, (list, tuple)) else [after]

    def body(land_ref, s_sems, r_sems, *rest):
        for cp in _forward_copies(land_ref, s_sems, r_sems, False):
            cp.wait_send()
            cp.wait_recv()

    return pl.pallas_call(
        body, name=name, out_shape=pltpu.HBM(land.shape, land.dtype),
        in_specs=[_HBM, _SEM, _SEM] + [pl.BlockSpec(memory_space=pl.ANY)] * len(after), out_specs=_HBM,
        input_output_aliases={0: 0},
        compiler_params=pltpu.CompilerParams(has_side_effects=_EFFECT),
    )(land, send_sems, recv_sems, *after)


def _sum_slabs(recv, name):
    n, R, _ = recv.shape

    def body(r_ref, o_ref):
        g = r_ref[0].astype(F32)
        for s in range(1, n):
            g = g + r_ref[s].astype(F32)
        o_ref[...] = g

    return pl.pallas_call(body, name=name, out_shape=S((R, LANES), F32))(recv)


def _adamw(recv, w, m, v, name, tr, tc=None):
    n, R, C = recv.shape
    c1 = 1.0 - ADAM_B1 ** ADAM_STEP
    c2 = 1.0 - ADAM_B2 ** ADAM_STEP

    def body(r_ref, w_ref, m_ref, v_ref, g_out, d_out, m_out, v_out):
        g = r_ref[0].astype(F32)
        for s in range(1, n):
            g = g + r_ref[s].astype(F32)
        mm = ADAM_B1 * m_ref[...] + (1.0 - ADAM_B1) * g
        vv = ADAM_B2 * v_ref[...] + (1.0 - ADAM_B2) * (g * g)
        g_out[...] = g
        m_out[...] = mm
        v_out[...] = vv
        d_out[...] = -ADAM_LR * ((mm / c1) / (jnp.sqrt(vv / c2) + ADAM_EPS) + ADAM_WD * w_ref[...])

    tc = C if tc is None else tc
    spec = pl.BlockSpec((tr, tc), lambda i, j: (i, j))
    return pl.pallas_call(
        body, name=name, grid=(R // tr, C // tc),
        in_specs=[pl.BlockSpec((n, tr, tc), lambda i, j: (0, i, j)), spec, spec, spec],
        out_specs=[spec] * 4, out_shape=[S((R, C), F32)] * 4,
        compiler_params=_cparams("parallel", "parallel"),
    )(recv, w, m, v)


def _pack(parts, rows):
    cols = []
    for p in parts:
        f = p.reshape(-1)
        cols.append(jnp.pad(f, (0, (-f.shape[0]) % LANES)))
    flat = jnp.concatenate(cols)
    return jnp.pad(flat, (0, rows * LANES - flat.shape[0])).reshape(rows, LANES)


def _unpack(buf, shapes):
    flat = buf.reshape(-1)
    out, o = [], 0
    for shp in shapes:
        n = int(np.prod(shp))
        out.append(flat[o:o + n].reshape(shp))
        o += n + (-n) % LANES
    return out


SMALL_ROWS = 200
CONV_ROWS = 16


def kernel(x, pre_mix_norm_w, w_in, ssd_conv_w, ssd_conv_b, ssd_dt_bias, ssd_a_log, ssd_d, ssd_norm_w, ret_norm_w, w_out, post_mix_norm_w, pre_ffn_norm_w, w_up, ffn_conv_w, ffn_conv_b, w_down, post_ffn_norm_w, loss_target, m_pre_mix_norm_w, m_w_in, m_ssd_conv_w, m_ssd_conv_b, m_ssd_dt_bias, m_ssd_a_log, m_ssd_d, m_ssd_norm_w, m_ret_norm_w, m_w_out, m_post_mix_norm_w, m_pre_ffn_norm_w, m_w_up, m_ffn_conv_w, m_ffn_conv_b, m_w_down, m_post_ffn_norm_w, v_pre_mix_norm_w, v_w_in, v_ssd_conv_w, v_ssd_conv_b, v_ssd_dt_bias, v_ssd_a_log, v_ssd_d, v_ssd_norm_w, v_ret_norm_w, v_w_out, v_post_mix_norm_w, v_pre_ffn_norm_w, v_w_up, v_ffn_conv_w, v_ffn_conv_b, v_w_down, v_post_ffn_norm_w):
    T = x.shape[1]
    xi, tgt = x[0], loss_target[0]
    me = 4 * lax.axis_index("x") + 2 * lax.axis_index("y") + lax.axis_index("c")
    n_in, n_up = w_in.shape[2], w_up.shape[2]
    n_out, n_down = w_out.shape[1], w_down.shape[1]
    n_sc, n_fc = ssd_conv_w.shape[2], ffn_conv_w.shape[2]

    def after(token, value):
        return value * (1.0 + token[0, 0])

    def finish(started, after_value, name, same):
        bufs, lands = _exchange_wait(started, after_value, name, same)
        own = [b if same else lax.dynamic_index_in_dim(b, me, 0, keepdims=False) for b in bufs]
        return [lax.dynamic_update_index_in_dim(l, o, me, 0) for l, o in zip(lands, own)]

    (gconv,), tok_conv = _exchange([_pack([ssd_conv_w, ffn_conv_w], CONV_ROWS)], "gather_conv", True)
    tr_ = lambda w: jnp.transpose(w[0])
    gat_in = _exchange_start([after(tok_conv, tr_(w_in)).astype(BF)], "gather_in_start", True, ONE_PER_CHIP)
    convs = [_unpack(gconv[d], [(SSD_CONV, n_sc), (FFN_CONV, n_fc)]) for d in range(N_DEV)]
    scw = jnp.pad(jnp.concatenate([c[0] for c in convs], axis=1), ((0, 8 - SSD_CONV), (0, 0)))
    fcw = jnp.pad(jnp.concatenate([c[1] for c in convs], axis=1), ((0, 8 - FFN_CONV), (0, 0)))
    pad_h = lambda p: jnp.pad(p, ((0, 0), (0, LANES - SSD_HEADS)))
    dtb, alog = pad_h(ssd_dt_bias), pad_h(ssd_a_log)
    dskx = jnp.repeat(ssd_d, SSD_HEAD_DIM, axis=1)
    inv = ROPE_BASE ** (-jnp.arange(0, RET_QK, 2, dtype=F32) / RET_QK)
    ang = jnp.arange(T, dtype=F32)[:, None] * inv[None, :]
    cs_, sn_ = jnp.cos(ang), jnp.sin(ang)
    cos = jnp.concatenate([cs_, cs_, cs_, cs_], axis=1)
    sin = jnp.concatenate([-sn_, sn_, -sn_, sn_], axis=1)
    shard_in, land_in = _exchange_wait(gat_in, [cos, sin, scw, fcw], "gather_in_wait", True, ONE_PER_CHIP)
    fwd_in_ = _forward_start(land_in[0], "gather_in_forward")
    gat_rest = _exchange_start([after(fwd_in_[3], w).astype(BF) for w in (w_out[0], tr_(w_up), w_down[0])], "gather_rest_start", True)
    g_in = lax.dynamic_update_index_in_dim(_forward_wait(fwd_in_, gat_rest[4], "gather_in_forward_wait"), shard_in[0], me, 0)
    wt = g_in.reshape(N_DEV * n_in, D_MODEL)
    wdt = jnp.pad(wt[O_DT:O_Q], ((0, LANES - SSD_HEADS), (0, 0)))

    h, z, xbc, q, k, v, g, dtr = _fwd_in(xi, pre_mix_norm_w, wt, wdt)
    ys, ypre, sst = _ssd_fwd(xbc, dtr, z, scw, ssd_conv_b, dtb, alog, dskx, ssd_norm_w)
    yr, rst = _ret_fwd(q, k, v, g, cos, sin, ret_norm_w)
    g_out, g_up, g_down = finish(gat_rest, yr, "gather_rest_wait", True)
    wout = g_out.reshape(N_DEV * n_out, D_MODEL)
    wup = g_up.reshape(N_DEV * n_up, D_MODEL)
    wdown = g_down.reshape(N_DEV * n_down, D_MODEL)
    y, x1, h2, graw, val = _fwd_mid(ys, yr, xi, wout, post_mix_norm_w, pre_ffn_norm_w, wup)
    a, dfb, dval, dgate, dx2, lossb, d_pff, d_fcb = _ffn_tail(graw, val, x1, tgt, fcw, ffn_conv_b, wdown, post_ffn_norm_w)
    gdown = _matmul_tn(a, dfb, "dw_down")
    sc_down = _exchange_start([gdown.reshape(N_DEV, n_down, D_MODEL).astype(BF)], "scatter_down_start", False)
    dgraw, dx1, dyb, dys, dyr, d_fcw, d_pf, d_pm = _ffn_bwd(dgate, dval, graw, x1, dx2, y, after(sc_down[4], fcw), wup,
                                                         pre_ffn_norm_w, post_mix_norm_w, wout)
    gup = jnp.concatenate([_matmul_tn(dgraw, h2, "dw_up_g"), _matmul_tn(dval, h2, "dw_up_v")], axis=0)
    gout = jnp.concatenate([_matmul_tn(ys, dyb, "dw_out_s"), _matmul_tn(yr, dyb, "dw_out_r")], axis=0)
    sc_mid = _exchange_start([gup.reshape(N_DEV, n_up, D_MODEL).astype(BF), gout.reshape(N_DEV, n_out, D_MODEL).astype(BF)],
                             "scatter_mid_start", False)
    dz, dxbc, ddt, d_scw, d_scb, d_dtb, d_alog, d_dsk, d_snw = _ssd_bwd(dys, ypre, xbc, dtr, z, sst, after(sc_mid[4], scw),
                                                                      ssd_conv_b, dtb, alog, dskx, ssd_norm_w)
    dq, dk, dv, dg, d_rnw = _ret_bwd(dyr, q, k, v, g, rst, cos, sin, ret_norm_w)
    gin = jnp.concatenate([_matmul_tn(dz, h, "dw_z"), _matmul_tn(dxbc, h, "dw_xbc"),
                           _matmul_tn(ddt, h, "dw_dt")[:SSD_HEADS], _matmul_tn(dq, h, "dw_q"), _matmul_tn(dk, h, "dw_k"),
                           _matmul_tn(dv, h, "dw_v"), _matmul_tn(dg, h, "dw_g")], axis=0)
    sc_in = _exchange_start([gin.reshape(N_DEV, n_in, D_MODEL).astype(BF)], "scatter_in_start", False)
    gx, d_w0 = _in_bwd(dz, dxbc, dq, dk, dv, dg, ddt, xi, dx1, after(sc_in[4], pre_mix_norm_w), wt, wdt)
    r_down, = finish(sc_down, gx, "scatter_down_wait", False)
    r_up, r_out = finish(sc_mid, r_down, "scatter_mid_wait", False)
    per_w = [None] * 4
    per_w[3] = _adamw(r_down, w_down[0], m_w_down[0], v_w_down[0], "adamw_down", n_down)
    per_w[2] = [jnp.transpose(t) for t in _adamw(r_up, tr_(w_up), tr_(m_w_up), tr_(v_w_up), "adamw_up", n_up, 256)]
    per_w[1] = _adamw(r_out, w_out[0], m_w_out[0], v_w_out[0], "adamw_out", n_out)
    r_in, = finish(sc_in, per_w[1][0], "scatter_in_wait", False)
    per_w[0] = [jnp.transpose(t) for t in _adamw(r_in, tr_(w_in), tr_(m_w_in), tr_(v_w_in), "adamw_in", n_in, 256)]
    big = [[per_w[i][kind][None] for i in range(4)] for kind in range(4)]

    small_full = [d_w0, d_scw[:SSD_CONV], d_scb, d_dtb[:, :SSD_HEADS], d_alog[:, :SSD_HEADS], d_dsk[:, :SSD_HEADS], d_snw, d_rnw,
                  d_pm, d_pf, d_fcw[:FFN_CONV], d_fcb, d_pff, lossb[0:1, 0:1]]
    full_shapes = [t.shape for t in small_full]
    gs = _sum_slabs(_exchange([_pack(small_full, SMALL_ROWS)], "gather_small", True)[0][0], "sum_small")
    gfull = _unpack(gs, full_shapes)
    gfull[1] = lax.dynamic_slice_in_dim(gfull[1], me * n_sc, n_sc, axis=1)
    gfull[10] = lax.dynamic_slice_in_dim(gfull[10], me * n_fc, n_fc, axis=1)
    ws = [pre_mix_norm_w, ssd_conv_w, ssd_conv_b, ssd_dt_bias, ssd_a_log, ssd_d, ssd_norm_w, ret_norm_w, post_mix_norm_w,
          pre_ffn_norm_w, ffn_conv_w, ffn_conv_b, post_ffn_norm_w]
    ms = [m_pre_mix_norm_w, m_ssd_conv_w, m_ssd_conv_b, m_ssd_dt_bias, m_ssd_a_log, m_ssd_d, m_ssd_norm_w, m_ret_norm_w,
          m_post_mix_norm_w, m_pre_ffn_norm_w, m_ffn_conv_w, m_ffn_conv_b, m_post_ffn_norm_w]
    vs = [v_pre_mix_norm_w, v_ssd_conv_w, v_ssd_conv_b, v_ssd_dt_bias, v_ssd_a_log, v_ssd_d, v_ssd_norm_w, v_ret_norm_w,
          v_post_mix_norm_w, v_pre_ffn_norm_w, v_ffn_conv_w, v_ffn_conv_b, v_post_ffn_norm_w]
    out_shapes = [t.shape for t in ws]
    loss = gfull.pop()[0, 0]
    small = _adamw(_pack(gfull, SMALL_ROWS)[None], _pack(ws, SMALL_ROWS), _pack(ms, SMALL_ROWS), _pack(vs, SMALL_ROWS),
                   "adamw_small", SMALL_ROWS)
    small = [_unpack(b, out_shapes) for b in small]

    order = {"pre_mix_norm_w": ("s", 0), "w_in": ("b", 0), "ssd_conv_w": ("s", 1), "ssd_conv_b": ("s", 2),
             "ssd_dt_bias": ("s", 3), "ssd_a_log": ("s", 4), "ssd_d": ("s", 5), "ssd_norm_w": ("s", 6), "ret_norm_w": ("s", 7),
             "w_out": ("b", 1), "post_mix_norm_w": ("s", 8), "pre_ffn_norm_w": ("s", 9), "w_up": ("b", 2),
             "ffn_conv_w": ("s", 10), "ffn_conv_b": ("s", 11), "w_down": ("b", 3), "post_ffn_norm_w": ("s", 12)}
    outs = [loss, gx[None]]
    for kind in range(4):
        for name, (grp, idx) in order.items():
            outs.append(big[kind][idx] if grp == "b" else small[kind][idx])
    return tuple(outs)
```

```python
import functools
import math

import numpy as np
import jax
import jax.numpy as jnp
from jax import lax
from jax.experimental import pallas as pl
from jax.experimental.pallas import tpu as pltpu

F32 = jnp.float32
BF = jnp.bfloat16
HI = lax.Precision.HIGHEST
S = jax.ShapeDtypeStruct

D_MODEL = 1024
SSD_HEADS = 16
SSD_HEAD_DIM = 64
SSD_GROUPS = 2
SSD_STATE = 128
SSD_WIDTH = 1024
SSD_XBC = 1536
SSD_CONV = 4
RET_HEADS = 8
RET_QK = 64
RET_V = 128
RET_QK_W = 512
RET_V_W = 1024
ROPE_BASE = 10000.0
CH = 128
D_FF = 2816
FFN_CONV = 3
EPS = 1e-6
IN_WIDTH = 5648
N_DEV = 8

ADAM_LR = 0.001
ADAM_B1 = 0.9
ADAM_B2 = 0.999
ADAM_EPS = 1e-08
ADAM_WD = 0.01
ADAM_STEP = 10

LANES = 128
HALO = 16
VMEM_LIMIT = 48 * 1024 * 1024

O_Z, O_XBC, O_DT, O_Q, O_K, O_V, O_G, O_END = 0, 1024, 2560, 2576, 3088, 3600, 4624, 5648
IN_SEGMENTS = ((O_Z, O_XBC), (O_XBC, O_DT), (O_Q, O_K), (O_K, O_V), (O_V, O_G), (O_G, O_END))


def _cparams(*sem):
    return pltpu.CompilerParams(dimension_semantics=sem, vmem_limit_bytes=VMEM_LIMIT)


def _dot(a, b):
    return jnp.dot(a.astype(BF), b.astype(BF), preferred_element_type=F32)


def _dot_nt(a, b):
    return lax.dot_general(a.astype(BF), b.astype(BF), (((1,), (1,)), ((), ())), preferred_element_type=F32)


def _dot_tn(a, b):
    return lax.dot_general(a.astype(BF), b.astype(BF), (((0,), (0,)), ((), ())), preferred_element_type=F32)


def _dot_hi(a, b):
    return jnp.dot(a, b, preferred_element_type=F32, precision=HI)


def _dot_tn_hi(a, b):
    return lax.dot_general(a, b, (((0,), (0,)), ((), ())), preferred_element_type=F32, precision=HI)


def _sigmoid(x):
    return jax.nn.sigmoid(x)


def _dsilu(x, s):
    return s * (1.0 + x * (1.0 - s))


def _softplus(x):
    return jnp.maximum(x, 0.0) + jnp.log1p(jnp.exp(-jnp.abs(x)))


def _rstd(x):
    return lax.rsqrt(jnp.mean(x * x, axis=-1, keepdims=True) + EPS)


def _rms_bwd(dy, x, r, w):
    gn = dy * w
    dx = r * gn - x * (r * r * r) * jnp.mean(gn * x, axis=-1, keepdims=True)
    dw = jnp.sum(dy * x * r, axis=0, keepdims=True)
    return dx, dw


def _rows_before(ext, s, head, n):
    if s == 0:
        return ext[head:head + n]
    return pltpu.roll(ext, s, 0)[head:head + n]


def _rows_after(ext, s, n):
    if s == 0:
        return ext[0:n]
    return pltpu.roll(ext, ext.shape[0] - s, 0)[0:n]


def _row_spec(tm, width):
    return pl.BlockSpec((tm, width), lambda i: (i, 0))


def _const_spec(shape):
    return pl.BlockSpec(shape, lambda i: (0,) * len(shape))


_VMEM_WHOLE = pl.BlockSpec(memory_space=pltpu.VMEM)


def _fwd_in(x, w0, wt, wdt, tm=256):
    T = x.shape[0]

    def body(x_ref, w0_ref, wt_ref, wdt_ref, h_ref, z_ref, xbc_ref, q_ref, k_ref, v_ref, g_ref, dt_ref):
        xf = x_ref[...]
        h = (xf * _rstd(xf) * w0_ref[...]).astype(BF)
        h_ref[...] = h
        for ref, (lo, hi) in zip((z_ref, xbc_ref, q_ref, k_ref, v_ref, g_ref), IN_SEGMENTS):
            ref[...] = _dot_nt(h, wt_ref[lo:hi, :]).astype(ref.dtype)
        dt_ref[...] = _dot_nt(h, wdt_ref[...])

    widths = (D_MODEL, SSD_WIDTH, SSD_XBC, RET_QK_W, RET_QK_W, RET_V_W, RET_V_W)
    return pl.pallas_call(
        body, name="fwd_in", grid=(T // tm,),
        in_specs=[_row_spec(tm, D_MODEL), _const_spec((1, D_MODEL)), _VMEM_WHOLE, _VMEM_WHOLE],
        out_specs=[_row_spec(tm, w) for w in widths] + [_row_spec(tm, LANES)],
        out_shape=[S((T, w), BF) for w in widths] + [S((T, LANES), F32)],
        compiler_params=_cparams("parallel"),
    )(x, w0, wt, wdt)


def _fwd_mid(ys, yr, x, wout, wpm, wpf, wup, tm=256):
    T = x.shape[0]

    def body(ys_ref, yr_ref, x_ref, wout_ref, wpm_ref, wpf_ref, wup_ref, y_ref, x1_ref, h2_ref, graw_ref, val_ref):
        y = (jnp.dot(ys_ref[...], wout_ref[0:SSD_WIDTH, :], preferred_element_type=F32)
             + jnp.dot(yr_ref[...], wout_ref[SSD_WIDTH:, :], preferred_element_type=F32))
        y_ref[...] = y
        x1 = x_ref[...] + y * _rstd(y) * wpm_ref[...]
        x1_ref[...] = x1
        h2 = (x1 * _rstd(x1) * wpf_ref[...]).astype(BF)
        h2_ref[...] = h2
        graw_ref[...] = _dot_nt(h2, wup_ref[0:D_FF, :]).astype(BF)
        val_ref[...] = _dot_nt(h2, wup_ref[D_FF:, :]).astype(BF)

    return pl.pallas_call(
        body, name="fwd_mid", grid=(T // tm,),
        in_specs=[_row_spec(tm, SSD_WIDTH), _row_spec(tm, RET_V_W), _row_spec(tm, D_MODEL), _VMEM_WHOLE,
                  _const_spec((1, D_MODEL)), _const_spec((1, D_MODEL)), _VMEM_WHOLE],
        out_specs=[_row_spec(tm, D_MODEL), _row_spec(tm, D_MODEL), _row_spec(tm, D_MODEL), _row_spec(tm, D_FF),
                   _row_spec(tm, D_FF)],
        out_shape=[S((T, D_MODEL), F32), S((T, D_MODEL), F32), S((T, D_MODEL), BF), S((T, D_FF), BF), S((T, D_FF), BF)],
        compiler_params=_cparams("parallel"),
    )(ys, yr, x, wout, wpm, wpf, wup)


def _ffn_tail(graw, val, x1, tgt, convw, convb, wdown, wpff, tm=256):
    T = x1.shape[0]

    def body(graw_ref, val_ref, x1_ref, tgt_ref, cw_ref, cb_ref, wd_ref, wpff_ref,
             a_ref, df_ref, dval_ref, dgate_ref, dx2_ref, loss_ref, dwpff_ref, dcb_ref, carry):
        i = pl.program_id(0)

        @pl.when(i == 0)
        def _():
            carry[...] = jnp.zeros_like(carry)
            loss_ref[...] = jnp.zeros_like(loss_ref)
            dwpff_ref[...] = jnp.zeros_like(dwpff_ref)
            dcb_ref[...] = jnp.zeros_like(dcb_ref)

        g = graw_ref[...].astype(F32)
        ext = jnp.concatenate([carry[...], g], axis=0)
        carry[...] = g[tm - 8:tm]
        gate = cb_ref[...] + sum(cw_ref[j:j + 1, :] * _rows_before(ext, FFN_CONV - 1 - j, 8, tm) for j in range(FFN_CONV))
        sg = _sigmoid(gate)
        silu = gate * sg
        v = val_ref[...].astype(F32)
        a = (silu * v).astype(BF)
        a_ref[...] = a
        f = jnp.dot(a, wd_ref[...], preferred_element_type=F32)
        r = _rstd(f)
        w = wpff_ref[...]
        e = x1_ref[...] + f * r * w - tgt_ref[...]
        loss_ref[...] += jnp.sum(e * e) * (0.5 / D_MODEL)
        dx2 = e * (1.0 / D_MODEL)
        dx2_ref[...] = dx2
        df, dw = _rms_bwd(dx2, f, r, w)
        dwpff_ref[...] += dw
        dfb = df.astype(BF)
        df_ref[...] = dfb
        da = _dot_nt(dfb, wd_ref[...])
        dval_ref[...] = (da * silu).astype(BF)
        dgate = da * v * _dsilu(gate, sg)
        dcb_ref[...] += jnp.sum(dgate, axis=0, keepdims=True)
        dgate_ref[...] = dgate.astype(BF)

    return pl.pallas_call(
        body, name="ffn_tail", grid=(T // tm,),
        in_specs=[_row_spec(tm, D_FF), _row_spec(tm, D_FF), _row_spec(tm, D_MODEL), _row_spec(tm, D_MODEL),
                  _const_spec((8, D_FF)), _const_spec((1, D_FF)), _VMEM_WHOLE, _const_spec((1, D_MODEL))],
        out_specs=[_row_spec(tm, D_FF), _row_spec(tm, D_MODEL), _row_spec(tm, D_FF), _row_spec(tm, D_FF),
                   _row_spec(tm, D_MODEL), _const_spec((8, LANES)), _const_spec((1, D_MODEL)), _const_spec((1, D_FF))],
        out_shape=[S((T, D_FF), BF), S((T, D_MODEL), BF), S((T, D_FF), BF), S((T, D_FF), BF), S((T, D_MODEL), F32),
                   S((8, LANES), F32), S((1, D_MODEL), F32), S((1, D_FF), F32)],
        scratch_shapes=[pltpu.VMEM((8, D_FF), F32)],
        compiler_params=_cparams("arbitrary"),
    )(graw, val, x1, tgt, convw, convb, wdown, wpff)


def _ffn_bwd(dgate, dval, graw, x1, dx2, y, convw, wup, wpf, wpm, wout, tm=256):
    T = x1.shape[0]
    nt = T // tm
    rev = lambda i: (nt - 1 - i, 0)
    rspec = lambda w: pl.BlockSpec((tm, w), rev)

    def body(dgate_ref, dval_ref, graw_ref, x1_ref, dx2_ref, y_ref, cw_ref, wup_ref, wpf_ref, wpm_ref, wout_ref,
             dgraw_ref, dx1_ref, dy_ref, dys_ref, dyr_ref, dcw_ref, dwpf_ref, dwpm_ref, carry):
        i = pl.program_id(0)

        @pl.when(i == 0)
        def _():
            carry[...] = jnp.zeros_like(carry)
            dcw_ref[...] = jnp.zeros_like(dcw_ref)
            dwpf_ref[...] = jnp.zeros_like(dwpf_ref)
            dwpm_ref[...] = jnp.zeros_like(dwpm_ref)

        dg = dgate_ref[...].astype(F32)
        ext = jnp.concatenate([dg, carry[...]], axis=0)
        carry[...] = dg[0:8]
        g = graw_ref[...].astype(F32)
        dgraw = jnp.zeros((tm, D_FF), F32)
        for j in range(FFN_CONV):
            sj = _rows_after(ext, FFN_CONV - 1 - j, tm)
            dgraw = dgraw + cw_ref[j:j + 1, :] * sj
            dcw_ref[j:j + 1, :] += jnp.sum(sj * g, axis=0, keepdims=True)
        dgrawb = dgraw.astype(BF)
        dgraw_ref[...] = dgrawb
        dh2 = _dot(dgrawb, wup_ref[0:D_FF, :]) + _dot(dval_ref[...], wup_ref[D_FF:, :])
        x1 = x1_ref[...]
        dxa, dw = _rms_bwd(dh2, x1, _rstd(x1), wpf_ref[...])
        dwpf_ref[...] += dw
        dx1 = dx2_ref[...] + dxa
        dx1_ref[...] = dx1
        yv = y_ref[...]
        dy, dw = _rms_bwd(dx1, yv, _rstd(yv), wpm_ref[...])
        dwpm_ref[...] += dw
        dyb = dy.astype(BF)
        dy_ref[...] = dyb
        dys_ref[...] = _dot_nt(dyb, wout_ref[0:SSD_WIDTH, :]).astype(BF)
        dyr_ref[...] = _dot_nt(dyb, wout_ref[SSD_WIDTH:, :]).astype(BF)

    return pl.pallas_call(
        body, name="ffn_bwd", grid=(nt,),
        in_specs=[rspec(D_FF), rspec(D_FF), rspec(D_FF), rspec(D_MODEL), rspec(D_MODEL), rspec(D_MODEL),
                  _const_spec((8, D_FF)), _VMEM_WHOLE, _const_spec((1, D_MODEL)), _const_spec((1, D_MODEL)), _VMEM_WHOLE],
        out_specs=[rspec(D_FF), rspec(D_MODEL), rspec(D_MODEL), rspec(SSD_WIDTH), rspec(RET_V_W),
                   _const_spec((8, D_FF)), _const_spec((1, D_MODEL)), _const_spec((1, D_MODEL))],
        out_shape=[S((T, D_FF), BF), S((T, D_MODEL), F32), S((T, D_MODEL), BF), S((T, SSD_WIDTH), BF), S((T, RET_V_W), BF),
                   S((8, D_FF), F32), S((1, D_MODEL), F32), S((1, D_MODEL), F32)],
        scratch_shapes=[pltpu.VMEM((8, D_FF), F32)],
        compiler_params=_cparams("arbitrary"),
    )(dgate, dval, graw, x1, dx2, y, convw, wup, wpf, wpm, wout)


def _in_bwd(dz, dxbc, dq, dk, dv, dg, ddt, x, dx1, w0, wt, wdt, tm=256):
    T = x.shape[0]

    def body(dz_ref, dxbc_ref, dq_ref, dk_ref, dv_ref, dg_ref, ddt_ref, x_ref, dx1_ref, w0_ref, wt_ref, wdt_ref, gx_ref, dw0_ref):
        @pl.when(pl.program_id(0) == 0)
        def _():
            dw0_ref[...] = jnp.zeros_like(dw0_ref)

        dh = _dot(ddt_ref[...], wdt_ref[...])
        for ref, (lo, hi) in zip((dz_ref, dxbc_ref, dq_ref, dk_ref, dv_ref, dg_ref), IN_SEGMENTS):
            dh = dh + _dot(ref[...], wt_ref[lo:hi, :])
        xf = x_ref[...]
        dx, dw = _rms_bwd(dh, xf, _rstd(xf), w0_ref[...])
        dw0_ref[...] += dw
        gx_ref[...] = dx1_ref[...] + dx

    widths = (SSD_WIDTH, SSD_XBC, RET_QK_W, RET_QK_W, RET_V_W, RET_V_W, LANES)
    return pl.pallas_call(
        body, name="in_bwd", grid=(T // tm,),
        in_specs=[_row_spec(tm, w) for w in widths] + [_row_spec(tm, D_MODEL), _row_spec(tm, D_MODEL),
                                                       _const_spec((1, D_MODEL)), _VMEM_WHOLE, _VMEM_WHOLE],
        out_specs=[_row_spec(tm, D_MODEL), _const_spec((1, D_MODEL))],
        out_shape=[S((T, D_MODEL), F32), S((1, D_MODEL), F32)],
        compiler_params=_cparams("arbitrary"),
    )(dz, dxbc, dq, dk, dv, dg, ddt, x, dx1, w0, wt, wdt)


DW_TILE_BYTES = 6 << 20


def _matmul_tn(a, b, name, tk=1024):
    T, M = a.shape
    N = b.shape[1]
    tm_, tn = M, N
    while tm_ * tn * 4 > DW_TILE_BYTES:
        if tm_ >= tn and tm_ % 256 == 0:
            tm_ //= 2
        elif tn % 256 == 0:
            tn //= 2
        else:
            break
    nk = T // tk

    def body(a_ref, b_ref, o_ref):
        @pl.when(pl.program_id(2) == 0)
        def _():
            o_ref[...] = jnp.zeros_like(o_ref)

        o_ref[...] += _dot_tn(a_ref[...], b_ref[...])

    return pl.pallas_call(
        body, name=name, grid=(M // tm_, N // tn, nk),
        in_specs=[pl.BlockSpec((tk, tm_), lambda m, n, k: (k, m)), pl.BlockSpec((tk, tn), lambda m, n, k: (k, n))],
        out_specs=pl.BlockSpec((tm_, tn), lambda m, n, k: (m, n)),
        out_shape=S((M, N), F32),
        compiler_params=_cparams("parallel", "parallel", "arbitrary"),
    )(a, b)


def _matmul_tn_group(as_, b, name, tk=1024):
    T, N = b.shape
    na = len(as_)

    def body(*refs):
        a_refs, b_ref, o_refs = refs[:na], refs[na], refs[na + 1:]

        @pl.when(pl.program_id(0) == 0)
        def _():
            for o_ref in o_refs:
                o_ref[...] = jnp.zeros_like(o_ref)

        bt = b_ref[...]
        for a_ref, o_ref in zip(a_refs, o_refs):
            o_ref[...] += _dot_tn(a_ref[...], bt)

    return pl.pallas_call(
        body, name=name, grid=(T // tk,),
        in_specs=[_row_spec(tk, a.shape[1]) for a in as_] + [_row_spec(tk, N)],
        out_specs=[_const_spec((a.shape[1], N)) for a in as_],
        out_shape=[S((a.shape[1], N), F32) for a in as_],
        compiler_params=_cparams("arbitrary"),
    )(*as_, b)


def _tri(lower):
    r = lax.broadcasted_iota(jnp.int32, (CH, CH), 0)
    c = lax.broadcasted_iota(jnp.int32, (CH, CH), 1)
    return ((c <= r) if lower else (r <= c)).astype(F32)


def _onehot_row(h):
    return (lax.broadcasted_iota(jnp.int32, (1, LANES), 1) == h).astype(F32)


def _onehot_col(h):
    return (lax.broadcasted_iota(jnp.int32, (LANES, 1), 0) == h).astype(F32)


def _ssd_pre(xc_ref, xh_ref, dtr_ref, cw_ref, cb_ref, dtb_ref, alog_ref, first):
    xc = xc_ref[...].astype(F32)
    xh = jnp.where(first, 0.0, xh_ref[...].astype(F32))
    ext = jnp.concatenate([xh, xc], axis=0)
    u = cb_ref[...] + sum(cw_ref[j:j + 1, :] * _rows_before(ext, SSD_CONV - 1 - j, HALO, CH) for j in range(SSD_CONV))
    sg = _sigmoid(u)
    act = u * sg
    dt = _softplus(dtr_ref[...] + dtb_ref[...])
    a = -jnp.exp(alog_ref[...])
    da = dt * a
    cs = _dot_hi(_tri(True), da)
    cst = _dot_tn_hi(da, _tri(False))
    return xc, u, sg, act, dt, a, cs, cst


HPG = SSD_HEADS // SSD_GROUPS
GW = HPG * SSD_HEAD_DIM


def _expand_heads(src, buf):
    for h in range(SSD_HEADS):
        buf[:, h * SSD_HEAD_DIM:(h + 1) * SSD_HEAD_DIM] = jnp.broadcast_to(src[:, h:h + 1], (CH, SSD_HEAD_DIM))


def _ssd_expanded(act, dt, cs, dtx, csx):
    _expand_heads(dt, dtx)
    _expand_heads(cs, csx)
    csv = csx[...]
    last = csv[CH - 1:CH, :]
    e_exp = jnp.exp(csv)
    dec_exp = jnp.exp(last - csv)
    el_exp = jnp.exp(last)
    xs = act[:, 0:SSD_WIDTH]
    xdt = xs * dtx[...]
    return xs, xdt, xdt * dec_exp, e_exp, dec_exp, el_exp


def _decay_mats(h, cs, cst, transposed):
    r = lax.broadcasted_iota(jnp.int32, (CH, CH), 0)
    c = lax.broadcasted_iota(jnp.int32, (CH, CH), 1)
    c_col = cs[:, h:h + 1]
    c_row = cst[h:h + 1, :]
    if transposed:
        return jnp.exp(jnp.where(r <= c, c_row - c_col, -1e30))
    return jnp.exp(jnp.where(r >= c, c_col - c_row, -1e30))


def _ssd_specs(T):
    nc = T // CH
    return nc, [
        _row_spec(CH, SSD_XBC),
        pl.BlockSpec((HALO, SSD_XBC), lambda i: (jnp.maximum(i * (CH // HALO) - 1, 0), 0)),
        _row_spec(CH, LANES),
        _row_spec(CH, SSD_WIDTH),
    ]


def _groups(act):
    bm = [act[:, SSD_WIDTH + g * SSD_STATE:SSD_WIDTH + (g + 1) * SSD_STATE] for g in range(SSD_GROUPS)]
    o = SSD_WIDTH + SSD_GROUPS * SSD_STATE
    cm = [act[:, o + g * SSD_STATE:o + (g + 1) * SSD_STATE] for g in range(SSD_GROUPS)]
    return bm, cm


def _ssd_fwd(xbc, dtr, z, convw, convb, dtb, alog, dskx, nw):
    T = xbc.shape[0]
    nc, specs = _ssd_specs(T)

    def body(xc_ref, xh_ref, dtr_ref, z_ref, cw_ref, cb_ref, dtb_ref, alog_ref, dskx_ref, nw_ref,
             out_ref, y_ref, st_ref, state, ybuf, dtx, csx):
        i = pl.program_id(0)

        @pl.when(i == 0)
        def _():
            state[...] = jnp.zeros_like(state)

        xc, u, sg, act, dt, a, cs, cst = _ssd_pre(xc_ref, xh_ref, dtr_ref, cw_ref, cb_ref, dtb_ref, alog_ref, i == 0)
        xs, xdt, w, e_exp, dec_exp, el_exp = _ssd_expanded(act, dt, cs, dtx, csx)
        bm, cm = _groups(act)
        for g in range(SSD_GROUPS):
            gs = slice(g * GW, (g + 1) * GW)
            st = state[g]
            st_ref[0, g] = st
            cb = _dot_nt(cm[g], bm[g])
            ybuf[:, gs] = _dot(cm[g], st) * e_exp[:, gs] + xs[:, gs] * dskx_ref[:, gs]
            state[g] = st * el_exp[:, gs] + _dot_tn(bm[g], w[:, gs])
            for h in range(g * HPG, (g + 1) * HPG):
                sl = slice(h * SSD_HEAD_DIM, (h + 1) * SSD_HEAD_DIM)
                ybuf[:, sl] += _dot(cb * _decay_mats(h, cs, cst, False), xdt[:, sl])
        yv = ybuf[...]
        y_ref[...] = yv.astype(BF)
        zf = z_ref[...].astype(F32)
        gated = yv * (zf * _sigmoid(zf))
        out_ref[...] = (gated * _rstd(gated) * nw_ref[...]).astype(BF)

    st_spec = pl.BlockSpec((1, SSD_GROUPS, SSD_STATE, GW), lambda i: (i, 0, 0, 0))
    return pl.pallas_call(
        body, name="ssd_fwd", grid=(nc,),
        in_specs=specs + [_const_spec((8, SSD_XBC)), _const_spec((1, SSD_XBC)), _const_spec((1, LANES)),
                          _const_spec((1, LANES)), _const_spec((1, SSD_WIDTH)), _const_spec((1, SSD_WIDTH))],
        out_specs=[_row_spec(CH, SSD_WIDTH), _row_spec(CH, SSD_WIDTH), st_spec],
        out_shape=[S((T, SSD_WIDTH), BF), S((T, SSD_WIDTH), BF), S((nc, SSD_GROUPS, SSD_STATE, GW), F32)],
        scratch_shapes=[pltpu.VMEM((SSD_GROUPS, SSD_STATE, GW), F32), pltpu.VMEM((CH, SSD_WIDTH), F32),
                        pltpu.VMEM((CH, SSD_WIDTH), F32), pltpu.VMEM((CH, SSD_WIDTH), F32)],
        compiler_params=_cparams("arbitrary"),
    )(xbc, xbc, dtr, z, convw, convb, dtb, alog, dskx, nw)


def _ssd_bwd(dout, y, xbc, dtr, z, states, convw, convb, dtb, alog, dskx, nw):
    T = xbc.shape[0]
    nc = T // CH
    rev = lambda i: (nc - 1 - i, 0)
    rspec = lambda w: pl.BlockSpec((CH, w), rev)
    halo_spec = pl.BlockSpec((HALO, SSD_XBC), lambda i: (jnp.maximum((nc - 1 - i) * (CH // HALO) - 1, 0), 0))
    NB = SSD_WIDTH
    NC_ = SSD_WIDTH + SSD_GROUPS * SSD_STATE

    def body(do_ref, y_ref, xc_ref, xh_ref, dtr_ref, z_ref, st_ref, cw_ref, cb_ref, dtb_ref, alog_ref, dskx_ref, nw_ref,
             dz_ref, dxbc_ref, ddt_ref, dcw_ref, dcb_ref, ddtb_ref, dalog_ref, ddsk_ref, dnw_ref,
             dstate, ducarry, dtx, csx, dxdtbuf, dact):
        i = pl.program_id(0)

        @pl.when(i == 0)
        def _():
            dstate[...] = jnp.zeros_like(dstate)
            ducarry[...] = jnp.zeros_like(ducarry)
            for ref in (dcw_ref, dcb_ref, ddtb_ref, dalog_ref, ddsk_ref, dnw_ref):
                ref[...] = jnp.zeros_like(ref)

        xc, u, sg, act, dt, a, cs, cst = _ssd_pre(xc_ref, xh_ref, dtr_ref, cw_ref, cb_ref, dtb_ref, alog_ref, i == nc - 1)
        xs, xdt, w, e_exp, dec_exp, el_exp = _ssd_expanded(act, dt, cs, dtx, csx)
        bm, cm = _groups(act)
        yv = y_ref[...].astype(F32)
        zf = z_ref[...].astype(F32)
        sz = _sigmoid(zf)
        gated = yv * (zf * sz)
        dgated, dnw = _rms_bwd(do_ref[...].astype(F32), gated, _rstd(gated), nw_ref[...])
        dnw_ref[...] += dnw
        dz_ref[...] = (dgated * yv * _dsilu(zf, sz)).astype(BF)
        dy = dgated * (zf * sz)
        lane_of = lax.broadcasted_iota(jnp.int32, (SSD_WIDTH, LANES), 0) - SSD_HEAD_DIM * lax.broadcasted_iota(jnp.int32, (SSD_WIDTH, LANES), 1)
        expt = ((lane_of >= 0) & (lane_of < SSD_HEAD_DIM)).astype(F32)
        ddsk_ref[...] += _dot_hi(jnp.sum(dy * xs, axis=0, keepdims=True), expt)
        dcs = jnp.zeros((CH, LANES), F32)
        dcst = jnp.zeros((LANES, CH), F32)
        ddt = jnp.zeros((CH, LANES), F32)
        lastrows = []
        for g in range(SSD_GROUPS):
            gs = slice(g * GW, (g + 1) * GW)
            st = st_ref[0, g]
            dsn = dstate[g]
            cbm = _dot_nt(cm[g], bm[g])
            cbt = _dot_nt(bm[g], cm[g])
            dy_g = dy[:, gs]
            yoff = _dot(cm[g], st) * e_exp[:, gs]
            dq = dy_g * e_exp[:, gs]
            dcm_g = _dot_nt(dq, st)
            dstate[g] = _dot_tn(cm[g], dq) + dsn * el_exp[:, gs]
            dw = _dot(bm[g], dsn)
            w_g = w[:, gs]
            dbm_g = _dot_nt(w_g, dsn)
            dww = dw * w_g
            red = dy_g * yoff - dww
            lastrows.append(jnp.sum(dsn * st, axis=0, keepdims=True) * el_exp[:, gs] + jnp.sum(dww, axis=0, keepdims=True))
            dxdtbuf[:, gs] = dw * dec_exp[:, gs]
            dcb = jnp.zeros((CH, CH), F32)
            for h in range(g * HPG, (g + 1) * HPG):
                sl = slice(h * SSD_HEAD_DIM, (h + 1) * SSD_HEAD_DIM)
                rl = slice((h - g * HPG) * SSD_HEAD_DIM, (h - g * HPG + 1) * SSD_HEAD_DIM)
                oh = _onehot_row(h)
                lmat = _decay_mats(h, cs, cst, False)
                mmat = cbm * lmat
                dy_h = dy[:, sl]
                dm = _dot_nt(dy_h, xdt[:, sl])
                dxdt_h = dxdtbuf[:, sl] + _dot(cbt * _decay_mats(h, cs, cst, True), dy_h)
                dxdtbuf[:, sl] = dxdt_h
                dseg = dm * mmat
                dcb = dcb + dm * lmat
                col = jnp.sum(dseg, axis=1, keepdims=True) + jnp.sum(red[:, rl], axis=1, keepdims=True)
                dcs = dcs + col * oh
                dcst = dcst - _onehot_col(h) * jnp.sum(dseg, axis=0, keepdims=True)
                ddt = ddt + jnp.sum(dxdt_h * xs[:, sl], axis=1, keepdims=True) * oh
            dact[:, NB + g * SSD_STATE:NB + (g + 1) * SSD_STATE] = dbm_g + _dot_tn(dcb, cm[g])
            dact[:, NC_ + g * SSD_STATE:NC_ + (g + 1) * SSD_STATE] = dcm_g + _dot(dcb, bm[g])
        dact[:, 0:SSD_WIDTH] = dy * dskx_ref[...] + dxdtbuf[...] * dtx[...]
        dlast = _dot_hi(jnp.concatenate(lastrows, axis=1), expt)
        rows = lax.broadcasted_iota(jnp.int32, (CH, LANES), 0)
        dcs = dcs + _dot_tn_hi(dcst, jnp.eye(LANES, dtype=F32)) + jnp.where(rows == CH - 1, dlast, 0.0)
        dda = _dot_hi(_tri(False), dcs)
        dalog_ref[...] += jnp.sum(dda * dt, axis=0, keepdims=True) * a
        ddt = ddt + dda * a
        ddtr = ddt * _sigmoid(dtr_ref[...] + dtb_ref[...])
        ddtb_ref[...] += jnp.sum(ddtr, axis=0, keepdims=True)
        ddt_ref[...] = ddtr.astype(BF)
        du = dact[...] * _dsilu(u, sg)
        dcb_ref[...] += jnp.sum(du, axis=0, keepdims=True)
        ext = jnp.concatenate([du, ducarry[...]], axis=0)
        ducarry[...] = du[0:8]
        dx = jnp.zeros((CH, SSD_XBC), F32)
        for j in range(SSD_CONV):
            sj = _rows_after(ext, SSD_CONV - 1 - j, CH)
            dx = dx + cw_ref[j:j + 1, :] * sj
            dcw_ref[j:j + 1, :] += jnp.sum(sj * xc, axis=0, keepdims=True)
        dxbc_ref[...] = dx.astype(BF)

    return pl.pallas_call(
        body, name="ssd_bwd", grid=(nc,),
        in_specs=[rspec(SSD_WIDTH), rspec(SSD_WIDTH), rspec(SSD_XBC), halo_spec, rspec(LANES), rspec(SSD_WIDTH),
                  pl.BlockSpec((1, SSD_GROUPS, SSD_STATE, GW), lambda i: (nc - 1 - i, 0, 0, 0)),
                  _const_spec((8, SSD_XBC)), _const_spec((1, SSD_XBC)), _const_spec((1, LANES)),
                  _const_spec((1, LANES)), _const_spec((1, SSD_WIDTH)), _const_spec((1, SSD_WIDTH))],
        out_specs=[rspec(SSD_WIDTH), rspec(SSD_XBC), rspec(LANES),
                   _const_spec((8, SSD_XBC)), _const_spec((1, SSD_XBC)), _const_spec((1, LANES)),
                   _const_spec((1, LANES)), _const_spec((1, LANES)), _const_spec((1, SSD_WIDTH))],
        out_shape=[S((T, SSD_WIDTH), BF), S((T, SSD_XBC), BF), S((T, LANES), BF),
                   S((8, SSD_XBC), F32), S((1, SSD_XBC), F32), S((1, LANES), F32),
                   S((1, LANES), F32), S((1, LANES), F32), S((1, SSD_WIDTH), F32)],
        scratch_shapes=[pltpu.VMEM((SSD_GROUPS, SSD_STATE, GW), F32), pltpu.VMEM((8, SSD_XBC), F32),
                        pltpu.VMEM((CH, SSD_WIDTH), F32), pltpu.VMEM((CH, SSD_WIDTH), F32),
                        pltpu.VMEM((CH, SSD_WIDTH), F32), pltpu.VMEM((CH, SSD_XBC), F32)],
        compiler_params=_cparams("arbitrary"),
    )(dout, y, xbc, xbc, dtr, z, states, convw, convb, dtb, alog, dskx, nw)


def _log_gamma(h):
    return float(np.log1p(-np.exp2(np.float32(-5.0 - h)), dtype=np.float32))


def _swap_halves(t):
    n = t.shape[1]
    lane = lax.broadcasted_iota(jnp.int32, t.shape, 1)
    return jnp.where((lane & (RET_QK - 1)) < RET_QK // 2, pltpu.roll(t, n - RET_QK // 2, 1), pltpu.roll(t, RET_QK // 2, 1))


def _rot(t, cos, sin):
    return t * cos + _swap_halves(t) * sin


def _rot_t(d, cos, sin):
    return d * cos + _swap_halves(d * sin)


def _ret_consts(h):
    lg = _log_gamma(h)
    r = lax.broadcasted_iota(jnp.int32, (CH, CH), 0)
    c = lax.broadcasted_iota(jnp.int32, (CH, CH), 1)
    rel = (r - c).astype(F32)
    dmask = jnp.where(rel >= 0, jnp.exp(lg * jnp.maximum(rel, 0.0)), 0.0)
    dmask_t = jnp.where(rel <= 0, jnp.exp(lg * jnp.maximum(-rel, 0.0)), 0.0)
    pos = lax.broadcasted_iota(jnp.int32, (CH, 1), 0).astype(F32)
    kdec = jnp.exp(lg * (CH - 1.0 - pos))
    qdec = jnp.exp(lg * (pos + 1.0))
    pos_row = lax.broadcasted_iota(jnp.int32, (1, CH), 1).astype(F32)
    kdec_row = jnp.exp(lg * (CH - 1.0 - pos_row))
    qdec_row = jnp.exp(lg * (pos_row + 1.0))
    return dmask, dmask_t, kdec, qdec, kdec_row, qdec_row, math.exp(lg * CH)


def _ret_fwd(q, k, v, g, cos, sin, nw):
    T = q.shape[0]
    nc = T // CH

    def body(q_ref, k_ref, v_ref, g_ref, cos_ref, sin_ref, nw_ref, out_ref, st_ref, state):
        i = pl.program_id(0)

        @pl.when(i == 0)
        def _():
            state[...] = jnp.zeros_like(state)

        cosf = jnp.tile(cos_ref[...], (1, RET_QK_W // LANES))
        sinf = jnp.tile(sin_ref[...], (1, RET_QK_W // LANES))
        qr = _rot(q_ref[...].astype(F32), cosf, sinf)
        kr = _rot(k_ref[...].astype(F32), cosf, sinf) * (RET_QK ** -0.5)
        krt = kr.T
        st_ref[0] = state[...]
        o_all = []
        for h in range(RET_HEADS):
            dmask, dmask_t, kdec, qdec, kdec_row, qdec_row, gam = _ret_consts(h)
            qs = slice(h * RET_QK, (h + 1) * RET_QK)
            v_h = v_ref[:, h * RET_V:(h + 1) * RET_V]
            rprev = state[h]
            scores = _dot_nt(qr[:, qs], kr[:, qs]) * dmask
            o_all.append(_dot(scores, v_h) + _dot(qr[:, qs] * qdec, rprev))
            state[h] = rprev * gam + _dot(krt[qs, :] * kdec_row, v_h)
        for h in range(RET_HEADS):
            sl = slice(h * RET_V, (h + 1) * RET_V)
            o = o_all[h]
            gf = g_ref[:, sl].astype(F32)
            out_ref[:, sl] = (o * _rstd(o) * nw_ref[:, sl] * (gf * _sigmoid(gf))).astype(BF)

    return pl.pallas_call(
        body, name="ret_fwd", grid=(nc,),
        in_specs=[_row_spec(CH, RET_QK_W), _row_spec(CH, RET_QK_W), _row_spec(CH, RET_V_W), _row_spec(CH, RET_V_W),
                  _row_spec(CH, LANES), _row_spec(CH, LANES), _const_spec((1, RET_V_W))],
        out_specs=[_row_spec(CH, RET_V_W), pl.BlockSpec((1, RET_HEADS, RET_QK, RET_V), lambda i: (i, 0, 0, 0))],
        out_shape=[S((T, RET_V_W), BF), S((nc, RET_HEADS, RET_QK, RET_V), F32)],
        scratch_shapes=[pltpu.VMEM((RET_HEADS, RET_QK, RET_V), F32)],
        compiler_params=_cparams("arbitrary"),
    )(q, k, v, g, cos, sin, nw)


def _ret_bwd(dout, q, k, v, g, states, cos, sin, nw):
    T = q.shape[0]
    nc = T // CH
    rev = lambda i: (nc - 1 - i, 0)
    rspec = lambda w: pl.BlockSpec((CH, w), rev)

    def body(do_ref, q_ref, k_ref, v_ref, g_ref, st_ref, cos_ref, sin_ref, nw_ref,
             dq_ref, dk_ref, dv_ref, dg_ref, dnw_ref, dstate, dqbuf, dkbuf):
        i = pl.program_id(0)

        @pl.when(i == 0)
        def _():
            dstate[...] = jnp.zeros_like(dstate)
            dnw_ref[...] = jnp.zeros_like(dnw_ref)

        cosf = jnp.tile(cos_ref[...], (1, RET_QK_W // LANES))
        sinf = jnp.tile(sin_ref[...], (1, RET_QK_W // LANES))
        qr = _rot(q_ref[...].astype(F32), cosf, sinf)
        kr = _rot(k_ref[...].astype(F32), cosf, sinf) * (RET_QK ** -0.5)
        qrt = qr.T
        heads = range(RET_HEADS)
        qsl = [slice(h * RET_QK, (h + 1) * RET_QK) for h in heads]
        vsl = [slice(h * RET_V, (h + 1) * RET_V) for h in heads]
        scores_t, o_all, do_all = [], [], []
        for h in heads:
            dmask, dmask_t, kdec, qdec, kdec_row, qdec_row, gam = _ret_consts(h)
            q_h, k_h = qr[:, qsl[h]], kr[:, qsl[h]]
            scores = _dot_nt(q_h, k_h) * dmask
            scores_t.append(_dot_nt(k_h, q_h) * dmask_t)
            o_all.append(_dot(scores, v_ref[:, vsl[h]]) + _dot(q_h * qdec, st_ref[0, h]))
        for h in heads:
            o = o_all[h]
            rr = _rstd(o)
            of = o * rr
            gf = g_ref[:, vsl[h]].astype(F32)
            sgg = _sigmoid(gf)
            d_h = do_ref[:, vsl[h]].astype(F32)
            nw_h = nw_ref[:, vsl[h]]
            dg_ref[:, vsl[h]] = (d_h * of * nw_h * _dsilu(gf, sgg)).astype(BF)
            dt_ = d_h * (gf * sgg)
            dnw_ref[:, vsl[h]] += jnp.sum(dt_ * of, axis=0, keepdims=True)
            dof = dt_ * nw_h
            do_all.append(rr * dof - o * (rr * rr * rr) * jnp.mean(dof * o, axis=-1, keepdims=True))
        for h in heads:
            dmask, dmask_t, kdec, qdec, kdec_row, qdec_row, gam = _ret_consts(h)
            q_h, k_h, v_h, do = qr[:, qsl[h]], kr[:, qsl[h]], v_ref[:, vsl[h]], do_all[h]
            gnext = dstate[h]
            dsc = _dot_nt(do, v_h) * dmask
            dsc_t = _dot_nt(v_h, do) * dmask_t
            dv_ref[:, vsl[h]] = (_dot(scores_t[h], do) + _dot(k_h * kdec, gnext)).astype(BF)
            dqbuf[:, qsl[h]] = _dot(dsc, k_h) + _dot_nt(do, st_ref[0, h]) * qdec
            dkbuf[:, qsl[h]] = _dot(dsc_t, q_h) + _dot_nt(v_h, gnext) * kdec
            dstate[h] = gnext * gam + _dot(qrt[qsl[h], :] * qdec_row, do)
        dq_ref[...] = _rot_t(dqbuf[...], cosf, sinf).astype(BF)
        dk_ref[...] = (_rot_t(dkbuf[...], cosf, sinf) * (RET_QK ** -0.5)).astype(BF)

    return pl.pallas_call(
        body, name="ret_bwd", grid=(nc,),
        in_specs=[rspec(RET_V_W), rspec(RET_QK_W), rspec(RET_QK_W), rspec(RET_V_W), rspec(RET_V_W),
                  pl.BlockSpec((1, RET_HEADS, RET_QK, RET_V), lambda i: (nc - 1 - i, 0, 0, 0)),
                  rspec(LANES), rspec(LANES), _const_spec((1, RET_V_W))],
        out_specs=[rspec(RET_QK_W), rspec(RET_QK_W), rspec(RET_V_W), rspec(RET_V_W), _const_spec((1, RET_V_W))],
        out_shape=[S((T, RET_QK_W), BF), S((T, RET_QK_W), BF), S((T, RET_V_W), BF), S((T, RET_V_W), BF),
                   S((1, RET_V_W), F32)],
        scratch_shapes=[pltpu.VMEM((RET_HEADS, RET_QK, RET_V), F32), pltpu.VMEM((CH, RET_QK_W), F32),
                        pltpu.VMEM((CH, RET_QK_W), F32)],
        compiler_params=_cparams("arbitrary"),
    )(dout, q, k, v, g, states, cos, sin, nw)


def _exchange(bufs, name, same):
    nb = len(bufs)
    slabs = [tuple(b.shape if same else b.shape[1:]) for b in bufs]

    def body(*refs):
        buf_refs, out_refs, token = refs[:nb], refs[nb:2 * nb], refs[2 * nb]
        send_sems, recv_sems, local_sems = refs[2 * nb + 1:]
        x, y, c = lax.axis_index("x"), lax.axis_index("y"), lax.axis_index("c")
        me = 4 * x + 2 * y + c
        token[...] = jnp.zeros_like(token)

        def src(b, d):
            return buf_refs[b] if same else buf_refs[b].at[d]

        def remote(b, k, to_me):
            px = 1 - x if k & 4 else x
            py = 1 - y if k & 2 else y
            pc = 1 - c if k & 1 else c
            p = 4 * px + 2 * py + pc
            s = b * (N_DEV - 1) + k - 1
            return pltpu.make_async_remote_copy(
                src_ref=src(b, p), dst_ref=out_refs[b].at[me if to_me else p], send_sem=send_sems.at[s],
                recv_sem=recv_sems.at[s], device_id=(px, py, pc), device_id_type=pl.DeviceIdType.MESH)

        local = [pltpu.make_async_copy(src(b, me), out_refs[b].at[me], local_sems.at[b]) for b in range(nb)]
        for cp in local:
            cp.start()
        sends = [remote(b, k, True) for k in range(1, N_DEV) for b in range(nb)]
        for cp in sends:
            cp.start()
        for k in range(1, N_DEV):
            for b in range(nb):
                remote(b, k, False).wait_recv()
        for cp in sends:
            cp.wait_send()
        for cp in local:
            cp.wait()

    any_spec = pl.BlockSpec(memory_space=pl.ANY)
    out = pl.pallas_call(
        body, name=name,
        in_specs=[any_spec] * nb, out_specs=[any_spec] * nb + [pl.BlockSpec(memory_space=pltpu.VMEM)],
        out_shape=[S((N_DEV,) + s, b.dtype) for s, b in zip(slabs, bufs)] + [S((8, LANES), F32)],
        scratch_shapes=[pltpu.SemaphoreType.DMA((nb * (N_DEV - 1),)), pltpu.SemaphoreType.DMA((nb * (N_DEV - 1),)),
                        pltpu.SemaphoreType.DMA((nb,))],
    )(*bufs)
    return list(out[:nb]), out[nb]


_HBM = pl.BlockSpec(memory_space=pltpu.HBM)
_SEM = pl.BlockSpec(memory_space=pltpu.SEMAPHORE)
_EFFECT = pltpu.SideEffectType.DATAFLOW_SIDE_EFFECTING


ALL_PEERS = tuple(range(1, N_DEV))
ONE_PER_CHIP = (1, 2, 4, 6)
OTHER_CHIPS = (2, 4, 6)


def _split_copies(buf_refs, land_refs, send_sems, recv_sems, same, to_me, ks):
    x, y, c = lax.axis_index("x"), lax.axis_index("y"), lax.axis_index("c")
    me = 4 * x + 2 * y + c
    cps = []
    for ki, k in enumerate(ks):
        px = 1 - x if k & 4 else x
        py = 1 - y if k & 2 else y
        pc = 1 - c if k & 1 else c
        p = 4 * px + 2 * py + pc
        for b in range(len(buf_refs)):
            s = b * len(ks) + ki
            cps.append(pltpu.make_async_remote_copy(
                src_ref=buf_refs[b] if same else buf_refs[b].at[p], dst_ref=land_refs[b].at[me if to_me else p],
                send_sem=send_sems.at[s], recv_sem=recv_sems.at[s], device_id=(px, py, pc), device_id_type=pl.DeviceIdType.MESH))
    return cps


def _exchange_start(bufs, name, same, ks=ALL_PEERS):
    nb = len(bufs)
    ns = nb * len(ks)
    lands = [lax.empty((N_DEV,) + tuple(b.shape if same else b.shape[1:]), b.dtype) for b in bufs]

    def body(*refs):
        buf_refs, land_refs = refs[:nb], refs[nb:2 * nb]
        send_sems, recv_sems = refs[2 * nb], refs[2 * nb + 1]
        token = refs[-1]
        for cp in _split_copies(buf_refs, land_refs, send_sems, recv_sems, same, True, ks):
            cp.start()
        token[...] = jnp.zeros_like(token)

    hbm = lambda a: pltpu.with_memory_space_constraint(a, pltpu.HBM)
    out = pl.pallas_call(
        body, name=name,
        out_shape=(pltpu.SemaphoreType.DMA((ns,)), pltpu.SemaphoreType.DMA((ns,)),
                   *[pltpu.HBM(a.shape, a.dtype) for a in list(bufs) + lands], S((8, LANES), F32)),
        in_specs=[_HBM] * (2 * nb), out_specs=(_SEM, _SEM, *[_HBM] * (2 * nb), pl.BlockSpec(memory_space=pltpu.VMEM)),
        input_output_aliases={i: 2 + i for i in range(2 * nb)},
        compiler_params=pltpu.CompilerParams(has_side_effects=_EFFECT),
    )(*[hbm(a) for a in list(bufs) + lands])
    return out[0], out[1], list(out[2:2 + nb]), list(out[2 + nb:2 + 2 * nb]), out[-1]


def _exchange_wait(started, after, name, same, ks=ALL_PEERS):
    send_sems, recv_sems, bufs, lands, _ = started
    nb = len(bufs)
    after = list(after) if isinstance(after, (list, tuple)) else [after]

    def body(*refs):
        buf_refs, land_refs = refs[:nb], refs[nb:2 * nb]
        s_sems, r_sems = refs[2 * nb], refs[2 * nb + 1]
        for cp in _split_copies(buf_refs, land_refs, s_sems, r_sems, same, False, ks):
            cp.wait_send()
            cp.wait_recv()

    out = pl.pallas_call(
        body, name=name,
        out_shape=tuple(pltpu.HBM(a.shape, a.dtype) for a in bufs + lands),
        in_specs=[_HBM] * (2 * nb) + [_SEM, _SEM] + [pl.BlockSpec(memory_space=pl.ANY)] * len(after),
        out_specs=tuple([_HBM] * (2 * nb)),
        input_output_aliases={i: i for i in range(2 * nb)},
        compiler_params=pltpu.CompilerParams(has_side_effects=_EFFECT),
    )(*bufs, *lands, send_sems, recv_sems, *after)
    return list(out[:nb]), list(out[nb:])


def _forward_copies(land_ref, send_sems, recv_sems, sending):
    x, y, c = lax.axis_index("x"), lax.axis_index("y"), lax.axis_index("c")
    cps = []
    for ki, k in enumerate(OTHER_CHIPS):
        px = 1 - x if k & 4 else x
        py = 1 - y if k & 2 else y
        q = 4 * px + 2 * py + (c if sending else 1 - c)
        cps.append(pltpu.make_async_remote_copy(
            src_ref=land_ref.at[q], dst_ref=land_ref.at[q], send_sem=send_sems.at[ki], recv_sem=recv_sems.at[ki],
            device_id=(x, y, 1 - c), device_id_type=pl.DeviceIdType.MESH))
    return cps


def _forward_start(land, name):
    def body(land_ref, send_sems, recv_sems, land_thru, token):
        for cp in _forward_copies(land_ref, send_sems, recv_sems, True):
            cp.start()
        token[...] = jnp.zeros_like(token)

    n = len(OTHER_CHIPS)
    out = pl.pallas_call(
        body, name=name,
        out_shape=(pltpu.SemaphoreType.DMA((n,)), pltpu.SemaphoreType.DMA((n,)), pltpu.HBM(land.shape, land.dtype),
                   S((8, LANES), F32)),
        in_specs=[_HBM], out_specs=(_SEM, _SEM, _HBM, pl.BlockSpec(memory_space=pltpu.VMEM)),
        input_output_aliases={0: 2},
        compiler_params=pltpu.CompilerParams(has_side_effects=_EFFECT),
    )(pltpu.with_memory_space_constraint(land, pltpu.HBM))
    return out


def _forward_wait(started, after, name):
    send_sems, recv_sems, land, _ = started
    after = list(after) if isinstance(after, (list, tuple)) else [after]

    def body(land_ref, s_sems, r_sems, *rest):
        for cp in _forward_copies(land_ref, s_sems, r_sems, False):
            cp.wait_send()
            cp.wait_recv()

    return pl.pallas_call(
        body, name=name, out_shape=pltpu.HBM(land.shape, land.dtype),
        in_specs=[_HBM, _SEM, _SEM] + [pl.BlockSpec(memory_space=pl.ANY)] * len(after), out_specs=_HBM,
        input_output_aliases={0: 0},
        compiler_params=pltpu.CompilerParams(has_side_effects=_EFFECT),
    )(land, send_sems, recv_sems, *after)


def _sum_slabs(recv, name):
    n, R, _ = recv.shape

    def body(r_ref, o_ref):
        g = r_ref[0].astype(F32)
        for s in range(1, n):
            g = g + r_ref[s].astype(F32)
        o_ref[...] = g

    return pl.pallas_call(body, name=name, out_shape=S((R, LANES), F32))(recv)


def _adamw(recv, w, m, v, name, tr, tc=None):
    n, R, C = recv.shape
    c1 = 1.0 - ADAM_B1 ** ADAM_STEP
    c2 = 1.0 - ADAM_B2 ** ADAM_STEP

    def body(r_ref, w_ref, m_ref, v_ref, g_out, d_out, m_out, v_out):
        g = r_ref[0].astype(F32)
        for s in range(1, n):
            g = g + r_ref[s].astype(F32)
        mm = ADAM_B1 * m_ref[...] + (1.0 - ADAM_B1) * g
        vv = ADAM_B2 * v_ref[...] + (1.0 - ADAM_B2) * (g * g)
        g_out[...] = g
        m_out[...] = mm
        v_out[...] = vv
        d_out[...] = -ADAM_LR * ((mm / c1) / (jnp.sqrt(vv / c2) + ADAM_EPS) + ADAM_WD * w_ref[...])

    tc = C if tc is None else tc
    spec = pl.BlockSpec((tr, tc), lambda i, j: (i, j))
    return pl.pallas_call(
        body, name=name, grid=(R // tr, C // tc),
        in_specs=[pl.BlockSpec((n, tr, tc), lambda i, j: (0, i, j)), spec, spec, spec],
        out_specs=[spec] * 4, out_shape=[S((R, C), F32)] * 4,
        compiler_params=_cparams("parallel", "parallel"),
    )(recv, w, m, v)


def _pack(parts, rows):
    cols = []
    for p in parts:
        f = p.reshape(-1)
        cols.append(jnp.pad(f, (0, (-f.shape[0]) % LANES)))
    flat = jnp.concatenate(cols)
    return jnp.pad(flat, (0, rows * LANES - flat.shape[0])).reshape(rows, LANES)


def _unpack(buf, shapes):
    flat = buf.reshape(-1)
    out, o = [], 0
    for shp in shapes:
        n = int(np.prod(shp))
        out.append(flat[o:o + n].reshape(shp))
        o += n + (-n) % LANES
    return out


SMALL_ROWS = 200
CONV_ROWS = 16


def kernel(x, pre_mix_norm_w, w_in, ssd_conv_w, ssd_conv_b, ssd_dt_bias, ssd_a_log, ssd_d, ssd_norm_w, ret_norm_w, w_out, post_mix_norm_w, pre_ffn_norm_w, w_up, ffn_conv_w, ffn_conv_b, w_down, post_ffn_norm_w, loss_target, m_pre_mix_norm_w, m_w_in, m_ssd_conv_w, m_ssd_conv_b, m_ssd_dt_bias, m_ssd_a_log, m_ssd_d, m_ssd_norm_w, m_ret_norm_w, m_w_out, m_post_mix_norm_w, m_pre_ffn_norm_w, m_w_up, m_ffn_conv_w, m_ffn_conv_b, m_w_down, m_post_ffn_norm_w, v_pre_mix_norm_w, v_w_in, v_ssd_conv_w, v_ssd_conv_b, v_ssd_dt_bias, v_ssd_a_log, v_ssd_d, v_ssd_norm_w, v_ret_norm_w, v_w_out, v_post_mix_norm_w, v_pre_ffn_norm_w, v_w_up, v_ffn_conv_w, v_ffn_conv_b, v_w_down, v_post_ffn_norm_w):
    T = x.shape[1]
    xi, tgt = x[0], loss_target[0]
    me = 4 * lax.axis_index("x") + 2 * lax.axis_index("y") + lax.axis_index("c")
    n_in, n_up = w_in.shape[2], w_up.shape[2]
    n_out, n_down = w_out.shape[1], w_down.shape[1]
    n_sc, n_fc = ssd_conv_w.shape[2], ffn_conv_w.shape[2]

    def after(token, value):
        return value * (1.0 + token[0, 0])

    def finish(started, after_value, name, same):
        bufs, lands = _exchange_wait(started, after_value, name, same)
        own = [b if same else lax.dynamic_index_in_dim(b, me, 0, keepdims=False) for b in bufs]
        return [lax.dynamic_update_index_in_dim(l, o, me, 0) for l, o in zip(lands, own)]

    (gconv,), tok_conv = _exchange([_pack([ssd_conv_w, ffn_conv_w], CONV_ROWS)], "gather_conv", True)
    tr_ = lambda w: jnp.transpose(w[0])
    gat_in = _exchange_start([after(tok_conv, tr_(w_in)).astype(BF)], "gather_in_start", True, ONE_PER_CHIP)
    convs = [_unpack(gconv[d], [(SSD_CONV, n_sc), (FFN_CONV, n_fc)]) for d in range(N_DEV)]
    scw = jnp.pad(jnp.concatenate([c[0] for c in convs], axis=1), ((0, 8 - SSD_CONV), (0, 0)))
    fcw = jnp.pad(jnp.concatenate([c[1] for c in convs], axis=1), ((0, 8 - FFN_CONV), (0, 0)))
    pad_h = lambda p: jnp.pad(p, ((0, 0), (0, LANES - SSD_HEADS)))
    dtb, alog = pad_h(ssd_dt_bias), pad_h(ssd_a_log)
    dskx = jnp.repeat(ssd_d, SSD_HEAD_DIM, axis=1)
    inv = ROPE_BASE ** (-jnp.arange(0, RET_QK, 2, dtype=F32) / RET_QK)
    ang = jnp.arange(T, dtype=F32)[:, None] * inv[None, :]
    cs_, sn_ = jnp.cos(ang), jnp.sin(ang)
    cos = jnp.concatenate([cs_, cs_, cs_, cs_], axis=1)
    sin = jnp.concatenate([-sn_, sn_, -sn_, sn_], axis=1)
    shard_in, land_in = _exchange_wait(gat_in, [cos, sin, scw, fcw], "gather_in_wait", True, ONE_PER_CHIP)
    fwd_in_ = _forward_start(land_in[0], "gather_in_forward")
    gat_rest = _exchange_start([after(fwd_in_[3], w).astype(BF) for w in (w_out[0], tr_(w_up), w_down[0])], "gather_rest_start", True)
    g_in = lax.dynamic_update_index_in_dim(_forward_wait(fwd_in_, gat_rest[4], "gather_in_forward_wait"), shard_in[0], me, 0)
    wt = g_in.reshape(N_DEV * n_in, D_MODEL)
    wdt = jnp.pad(wt[O_DT:O_Q], ((0, LANES - SSD_HEADS), (0, 0)))

    h, z, xbc, q, k, v, g, dtr = _fwd_in(xi, pre_mix_norm_w, wt, wdt)
    ys, ypre, sst = _ssd_fwd(xbc, dtr, z, scw, ssd_conv_b, dtb, alog, dskx, ssd_norm_w)
    yr, rst = _ret_fwd(q, k, v, g, cos, sin, ret_norm_w)
    g_out, g_up, g_down = finish(gat_rest, yr, "gather_rest_wait", True)
    wout = g_out.reshape(N_DEV * n_out, D_MODEL)
    wup = g_up.reshape(N_DEV * n_up, D_MODEL)
    wdown = g_down.reshape(N_DEV * n_down, D_MODEL)
    y, x1, h2, graw, val = _fwd_mid(ys, yr, xi, wout, post_mix_norm_w, pre_ffn_norm_w, wup)
    a, dfb, dval, dgate, dx2, lossb, d_pff, d_fcb = _ffn_tail(graw, val, x1, tgt, fcw, ffn_conv_b, wdown, post_ffn_norm_w)
    gdown = _matmul_tn(a, dfb, "dw_down")
    sc_down = _exchange_start([gdown.reshape(N_DEV, n_down, D_MODEL).astype(BF)], "scatter_down_start", False)
    dgraw, dx1, dyb, dys, dyr, d_fcw, d_pf, d_pm = _ffn_bwd(dgate, dval, graw, x1, dx2, y, after(sc_down[4], fcw), wup,
                                                         pre_ffn_norm_w, post_mix_norm_w, wout)
    gup = jnp.concatenate([_matmul_tn(dgraw, h2, "dw_up_g"), _matmul_tn(dval, h2, "dw_up_v")], axis=0)
    gout = jnp.concatenate(_matmul_tn_group([ys, yr], dyb, "dw_out"), axis=0)
    sc_mid = _exchange_start([gup.reshape(N_DEV, n_up, D_MODEL).astype(BF), gout.reshape(N_DEV, n_out, D_MODEL).astype(BF)],
                             "scatter_mid_start", False)
    dz, dxbc, ddt, d_scw, d_scb, d_dtb, d_alog, d_dsk, d_snw = _ssd_bwd(dys, ypre, xbc, dtr, z, sst, after(sc_mid[4], scw),
                                                                      ssd_conv_b, dtb, alog, dskx, ssd_norm_w)
    dq, dk, dv, dg, d_rnw = _ret_bwd(dyr, q, k, v, g, rst, cos, sin, ret_norm_w)
    g_q, g_k, g_v, g_g = _matmul_tn_group([dq, dk, dv, dg], h, "dw_ret")
    g_z, g_xbc, g_dt = _matmul_tn_group([dz, dxbc, ddt], h, "dw_ssd")
    gin = jnp.concatenate([g_z, g_xbc, g_dt[:SSD_HEADS], g_q, g_k, g_v, g_g], axis=0)
    sc_in = _exchange_start([gin.reshape(N_DEV, n_in, D_MODEL).astype(BF)], "scatter_in_start", False)
    gx, d_w0 = _in_bwd(dz, dxbc, dq, dk, dv, dg, ddt, xi, dx1, after(sc_in[4], pre_mix_norm_w), wt, wdt)
    r_down, = finish(sc_down, gx, "scatter_down_wait", False)
    r_up, r_out = finish(sc_mid, r_down, "scatter_mid_wait", False)
    per_w = [None] * 4
    per_w[3] = _adamw(r_down, w_down[0], m_w_down[0], v_w_down[0], "adamw_down", n_down)
    per_w[2] = [jnp.transpose(t) for t in _adamw(r_up, tr_(w_up), tr_(m_w_up), tr_(v_w_up), "adamw_up", n_up, 256)]
    per_w[1] = _adamw(r_out, w_out[0], m_w_out[0], v_w_out[0], "adamw_out", n_out)
    r_in, = finish(sc_in, per_w[1][0], "scatter_in_wait", False)
    per_w[0] = [jnp.transpose(t) for t in _adamw(r_in, tr_(w_in), tr_(m_w_in), tr_(v_w_in), "adamw_in", n_in, 256)]
    big = [[per_w[i][kind][None] for i in range(4)] for kind in range(4)]

    small_full = [d_w0, d_scw[:SSD_CONV], d_scb, d_dtb[:, :SSD_HEADS], d_alog[:, :SSD_HEADS], d_dsk[:, :SSD_HEADS], d_snw, d_rnw,
                  d_pm, d_pf, d_fcw[:FFN_CONV], d_fcb, d_pff, lossb[0:1, 0:1]]
    full_shapes = [t.shape for t in small_full]
    gs = _sum_slabs(_exchange([_pack(small_full, SMALL_ROWS)], "gather_small", True)[0][0], "sum_small")
    gfull = _unpack(gs, full_shapes)
    gfull[1] = lax.dynamic_slice_in_dim(gfull[1], me * n_sc, n_sc, axis=1)
    gfull[10] = lax.dynamic_slice_in_dim(gfull[10], me * n_fc, n_fc, axis=1)
    ws = [pre_mix_norm_w, ssd_conv_w, ssd_conv_b, ssd_dt_bias, ssd_a_log, ssd_d, ssd_norm_w, ret_norm_w, post_mix_norm_w,
          pre_ffn_norm_w, ffn_conv_w, ffn_conv_b, post_ffn_norm_w]
    ms = [m_pre_mix_norm_w, m_ssd_conv_w, m_ssd_conv_b, m_ssd_dt_bias, m_ssd_a_log, m_ssd_d, m_ssd_norm_w, m_ret_norm_w,
          m_post_mix_norm_w, m_pre_ffn_norm_w, m_ffn_conv_w, m_ffn_conv_b, m_post_ffn_norm_w]
    vs = [v_pre_mix_norm_w, v_ssd_conv_w, v_ssd_conv_b, v_ssd_dt_bias, v_ssd_a_log, v_ssd_d, v_ssd_norm_w, v_ret_norm_w,
          v_post_mix_norm_w, v_pre_ffn_norm_w, v_ffn_conv_w, v_ffn_conv_b, v_post_ffn_norm_w]
    out_shapes = [t.shape for t in ws]
    loss = gfull.pop()[0, 0]
    small = _adamw(_pack(gfull, SMALL_ROWS)[None], _pack(ws, SMALL_ROWS), _pack(ms, SMALL_ROWS), _pack(vs, SMALL_ROWS),
                   "adamw_small", SMALL_ROWS)
    small = [_unpack(b, out_shapes) for b in small]

    order = {"pre_mix_norm_w": ("s", 0), "w_in": ("b", 0), "ssd_conv_w": ("s", 1), "ssd_conv_b": ("s", 2),
             "ssd_dt_bias": ("s", 3), "ssd_a_log": ("s", 4), "ssd_d": ("s", 5), "ssd_norm_w": ("s", 6), "ret_norm_w": ("s", 7),
             "w_out": ("b", 1), "post_mix_norm_w": ("s", 8), "pre_ffn_norm_w": ("s", 9), "w_up": ("b", 2),
             "ffn_conv_w": ("s", 10), "ffn_conv_b": ("s", 11), "w_down": ("b", 3), "post_ffn_norm_w": ("s", 12)}
    outs = [loss, gx[None]]
    for kind in range(4):
        for name, (grp, idx) in order.items():
            outs.append(big[kind][idx] if grp == "b" else small[kind][idx])
    return tuple(outs)
```

```python
import functools
import math

import numpy as np
import jax
import jax.numpy as jnp
from jax import lax
from jax.experimental import pallas as pl
from jax.experimental.pallas import tpu as pltpu

F32 = jnp.float32
BF = jnp.bfloat16
HI = lax.Precision.HIGHEST
S = jax.ShapeDtypeStruct

D_MODEL = 1024
SSD_HEADS = 16
SSD_HEAD_DIM = 64
SSD_GROUPS = 2
SSD_STATE = 128
SSD_WIDTH = 1024
SSD_XBC = 1536
SSD_CONV = 4
RET_HEADS = 8
RET_QK = 64
RET_V = 128
RET_QK_W = 512
RET_V_W = 1024
ROPE_BASE = 10000.0
CH = 128
D_FF = 2816
FFN_CONV = 3
EPS = 1e-6
IN_WIDTH = 5648
N_DEV = 8

ADAM_LR = 0.001
ADAM_B1 = 0.9
ADAM_B2 = 0.999
ADAM_EPS = 1e-08
ADAM_WD = 0.01
ADAM_STEP = 10

LANES = 128
HALO = 16
VMEM_LIMIT = 48 * 1024 * 1024

O_Z, O_XBC, O_DT, O_Q, O_K, O_V, O_G, O_END = 0, 1024, 2560, 2576, 3088, 3600, 4624, 5648
IN_SEGMENTS = ((O_Z, O_XBC), (O_XBC, O_DT), (O_Q, O_K), (O_K, O_V), (O_V, O_G), (O_G, O_END))


def _cparams(*sem):
    return pltpu.CompilerParams(dimension_semantics=sem, vmem_limit_bytes=VMEM_LIMIT)


def _dot(a, b):
    return jnp.dot(a.astype(BF), b.astype(BF), preferred_element_type=F32)


def _dot_nt(a, b):
    return lax.dot_general(a.astype(BF), b.astype(BF), (((1,), (1,)), ((), ())), preferred_element_type=F32)


def _dot_tn(a, b):
    return lax.dot_general(a.astype(BF), b.astype(BF), (((0,), (0,)), ((), ())), preferred_element_type=F32)


def _dot_hi(a, b):
    return jnp.dot(a, b, preferred_element_type=F32, precision=HI)


def _dot_tn_hi(a, b):
    return lax.dot_general(a, b, (((0,), (0,)), ((), ())), preferred_element_type=F32, precision=HI)


def _sigmoid(x):
    return jax.nn.sigmoid(x)


def _dsilu(x, s):
    return s * (1.0 + x * (1.0 - s))


def _softplus(x):
    return jnp.maximum(x, 0.0) + jnp.log1p(jnp.exp(-jnp.abs(x)))


def _rstd(x):
    return lax.rsqrt(jnp.mean(x * x, axis=-1, keepdims=True) + EPS)


def _rms_bwd(dy, x, r, w):
    gn = dy * w
    dx = r * gn - x * (r * r * r) * jnp.mean(gn * x, axis=-1, keepdims=True)
    dw = jnp.sum(dy * x * r, axis=0, keepdims=True)
    return dx, dw


def _rows_before(ext, s, head, n):
    if s == 0:
        return ext[head:head + n]
    return pltpu.roll(ext, s, 0)[head:head + n]


def _rows_after(ext, s, n):
    if s == 0:
        return ext[0:n]
    return pltpu.roll(ext, ext.shape[0] - s, 0)[0:n]


def _row_spec(tm, width):
    return pl.BlockSpec((tm, width), lambda i: (i, 0))


def _const_spec(shape):
    return pl.BlockSpec(shape, lambda i: (0,) * len(shape))


_VMEM_WHOLE = pl.BlockSpec(memory_space=pltpu.VMEM)


def _fwd_in(x, w0, wt, wdt, tm=512):
    T = x.shape[0]

    def body(x_ref, w0_ref, wt_ref, wdt_ref, h_ref, z_ref, xbc_ref, q_ref, k_ref, v_ref, g_ref, dt_ref):
        xf = x_ref[...]
        h = (xf * _rstd(xf) * w0_ref[...]).astype(BF)
        h_ref[...] = h
        for ref, (lo, hi) in zip((z_ref, xbc_ref, q_ref, k_ref, v_ref, g_ref), IN_SEGMENTS):
            ref[...] = _dot_nt(h, wt_ref[lo:hi, :]).astype(ref.dtype)
        dt_ref[...] = _dot_nt(h, wdt_ref[...])

    widths = (D_MODEL, SSD_WIDTH, SSD_XBC, RET_QK_W, RET_QK_W, RET_V_W, RET_V_W)
    return pl.pallas_call(
        body, name="fwd_in", grid=(T // tm,),
        in_specs=[_row_spec(tm, D_MODEL), _const_spec((1, D_MODEL)), _VMEM_WHOLE, _VMEM_WHOLE],
        out_specs=[_row_spec(tm, w) for w in widths] + [_row_spec(tm, LANES)],
        out_shape=[S((T, w), BF) for w in widths] + [S((T, LANES), F32)],
        compiler_params=_cparams("parallel"),
    )(x, w0, wt, wdt)


def _fwd_mid(ys, yr, x, wout, wpm, wpf, wup, tm=512):
    T = x.shape[0]

    def body(ys_ref, yr_ref, x_ref, wout_ref, wpm_ref, wpf_ref, wup_ref, y_ref, x1_ref, h2_ref, graw_ref, val_ref):
        y = (jnp.dot(ys_ref[...], wout_ref[0:SSD_WIDTH, :], preferred_element_type=F32)
             + jnp.dot(yr_ref[...], wout_ref[SSD_WIDTH:, :], preferred_element_type=F32))
        y_ref[...] = y
        x1 = x_ref[...] + y * _rstd(y) * wpm_ref[...]
        x1_ref[...] = x1
        h2 = (x1 * _rstd(x1) * wpf_ref[...]).astype(BF)
        h2_ref[...] = h2
        graw_ref[...] = _dot_nt(h2, wup_ref[0:D_FF, :]).astype(BF)
        val_ref[...] = _dot_nt(h2, wup_ref[D_FF:, :]).astype(BF)

    return pl.pallas_call(
        body, name="fwd_mid", grid=(T // tm,),
        in_specs=[_row_spec(tm, SSD_WIDTH), _row_spec(tm, RET_V_W), _row_spec(tm, D_MODEL), _VMEM_WHOLE,
                  _const_spec((1, D_MODEL)), _const_spec((1, D_MODEL)), _VMEM_WHOLE],
        out_specs=[_row_spec(tm, D_MODEL), _row_spec(tm, D_MODEL), _row_spec(tm, D_MODEL), _row_spec(tm, D_FF),
                   _row_spec(tm, D_FF)],
        out_shape=[S((T, D_MODEL), F32), S((T, D_MODEL), F32), S((T, D_MODEL), BF), S((T, D_FF), BF), S((T, D_FF), BF)],
        compiler_params=_cparams("parallel"),
    )(ys, yr, x, wout, wpm, wpf, wup)


def _ffn_tail(graw, val, x1, tgt, convw, convb, wdown, wpff, tm=256):
    T = x1.shape[0]

    def body(graw_ref, val_ref, x1_ref, tgt_ref, cw_ref, cb_ref, wd_ref, wpff_ref,
             a_ref, df_ref, dval_ref, dgate_ref, dx2_ref, loss_ref, dwpff_ref, dcb_ref, carry):
        i = pl.program_id(0)

        @pl.when(i == 0)
        def _():
            carry[...] = jnp.zeros_like(carry)
            loss_ref[...] = jnp.zeros_like(loss_ref)
            dwpff_ref[...] = jnp.zeros_like(dwpff_ref)
            dcb_ref[...] = jnp.zeros_like(dcb_ref)

        g = graw_ref[...].astype(F32)
        ext = jnp.concatenate([carry[...], g], axis=0)
        carry[...] = g[tm - 8:tm]
        gate = cb_ref[...] + sum(cw_ref[j:j + 1, :] * _rows_before(ext, FFN_CONV - 1 - j, 8, tm) for j in range(FFN_CONV))
        sg = _sigmoid(gate)
        silu = gate * sg
        v = val_ref[...].astype(F32)
        a = (silu * v).astype(BF)
        a_ref[...] = a
        f = jnp.dot(a, wd_ref[...], preferred_element_type=F32)
        r = _rstd(f)
        w = wpff_ref[...]
        e = x1_ref[...] + f * r * w - tgt_ref[...]
        loss_ref[...] += jnp.sum(e * e) * (0.5 / D_MODEL)
        dx2 = e * (1.0 / D_MODEL)
        dx2_ref[...] = dx2
        df, dw = _rms_bwd(dx2, f, r, w)
        dwpff_ref[...] += dw
        dfb = df.astype(BF)
        df_ref[...] = dfb
        da = _dot_nt(dfb, wd_ref[...])
        dval_ref[...] = (da * silu).astype(BF)
        dgate = da * v * _dsilu(gate, sg)
        dcb_ref[...] += jnp.sum(dgate, axis=0, keepdims=True)
        dgate_ref[...] = dgate.astype(BF)

    return pl.pallas_call(
        body, name="ffn_tail", grid=(T // tm,),
        in_specs=[_row_spec(tm, D_FF), _row_spec(tm, D_FF), _row_spec(tm, D_MODEL), _row_spec(tm, D_MODEL),
                  _const_spec((8, D_FF)), _const_spec((1, D_FF)), _VMEM_WHOLE, _const_spec((1, D_MODEL))],
        out_specs=[_row_spec(tm, D_FF), _row_spec(tm, D_MODEL), _row_spec(tm, D_FF), _row_spec(tm, D_FF),
                   _row_spec(tm, D_MODEL), _const_spec((8, LANES)), _const_spec((1, D_MODEL)), _const_spec((1, D_FF))],
        out_shape=[S((T, D_FF), BF), S((T, D_MODEL), BF), S((T, D_FF), BF), S((T, D_FF), BF), S((T, D_MODEL), F32),
                   S((8, LANES), F32), S((1, D_MODEL), F32), S((1, D_FF), F32)],
        scratch_shapes=[pltpu.VMEM((8, D_FF), F32)],
        compiler_params=_cparams("arbitrary"),
    )(graw, val, x1, tgt, convw, convb, wdown, wpff)


def _ffn_bwd(dgate, dval, graw, x1, dx2, y, convw, wup, wpf, wpm, wout, tm=256):
    T = x1.shape[0]
    nt = T // tm
    rev = lambda i: (nt - 1 - i, 0)
    rspec = lambda w: pl.BlockSpec((tm, w), rev)

    def body(dgate_ref, dval_ref, graw_ref, x1_ref, dx2_ref, y_ref, cw_ref, wup_ref, wpf_ref, wpm_ref, wout_ref,
             dgraw_ref, dx1_ref, dy_ref, dys_ref, dyr_ref, dcw_ref, dwpf_ref, dwpm_ref, carry):
        i = pl.program_id(0)

        @pl.when(i == 0)
        def _():
            carry[...] = jnp.zeros_like(carry)
            dcw_ref[...] = jnp.zeros_like(dcw_ref)
            dwpf_ref[...] = jnp.zeros_like(dwpf_ref)
            dwpm_ref[...] = jnp.zeros_like(dwpm_ref)

        dg = dgate_ref[...].astype(F32)
        ext = jnp.concatenate([dg, carry[...]], axis=0)
        carry[...] = dg[0:8]
        g = graw_ref[...].astype(F32)
        dgraw = jnp.zeros((tm, D_FF), F32)
        for j in range(FFN_CONV):
            sj = _rows_after(ext, FFN_CONV - 1 - j, tm)
            dgraw = dgraw + cw_ref[j:j + 1, :] * sj
            dcw_ref[j:j + 1, :] += jnp.sum(sj * g, axis=0, keepdims=True)
        dgrawb = dgraw.astype(BF)
        dgraw_ref[...] = dgrawb
        dh2 = _dot(dgrawb, wup_ref[0:D_FF, :]) + _dot(dval_ref[...], wup_ref[D_FF:, :])
        x1 = x1_ref[...]
        dxa, dw = _rms_bwd(dh2, x1, _rstd(x1), wpf_ref[...])
        dwpf_ref[...] += dw
        dx1 = dx2_ref[...] + dxa
        dx1_ref[...] = dx1
        yv = y_ref[...]
        dy, dw = _rms_bwd(dx1, yv, _rstd(yv), wpm_ref[...])
        dwpm_ref[...] += dw
        dyb = dy.astype(BF)
        dy_ref[...] = dyb
        dys_ref[...] = _dot_nt(dyb, wout_ref[0:SSD_WIDTH, :]).astype(BF)
        dyr_ref[...] = _dot_nt(dyb, wout_ref[SSD_WIDTH:, :]).astype(BF)

    return pl.pallas_call(
        body, name="ffn_bwd", grid=(nt,),
        in_specs=[rspec(D_FF), rspec(D_FF), rspec(D_FF), rspec(D_MODEL), rspec(D_MODEL), rspec(D_MODEL),
                  _const_spec((8, D_FF)), _VMEM_WHOLE, _const_spec((1, D_MODEL)), _const_spec((1, D_MODEL)), _VMEM_WHOLE],
        out_specs=[rspec(D_FF), rspec(D_MODEL), rspec(D_MODEL), rspec(SSD_WIDTH), rspec(RET_V_W),
                   _const_spec((8, D_FF)), _const_spec((1, D_MODEL)), _const_spec((1, D_MODEL))],
        out_shape=[S((T, D_FF), BF), S((T, D_MODEL), F32), S((T, D_MODEL), BF), S((T, SSD_WIDTH), BF), S((T, RET_V_W), BF),
                   S((8, D_FF), F32), S((1, D_MODEL), F32), S((1, D_MODEL), F32)],
        scratch_shapes=[pltpu.VMEM((8, D_FF), F32)],
        compiler_params=_cparams("arbitrary"),
    )(dgate, dval, graw, x1, dx2, y, convw, wup, wpf, wpm, wout)


def _in_bwd(dz, dxbc, dq, dk, dv, dg, ddt, x, dx1, w0, wt, wdt, tm=512):
    T = x.shape[0]

    def body(dz_ref, dxbc_ref, dq_ref, dk_ref, dv_ref, dg_ref, ddt_ref, x_ref, dx1_ref, w0_ref, wt_ref, wdt_ref, gx_ref, dw0_ref):
        @pl.when(pl.program_id(0) == 0)
        def _():
            dw0_ref[...] = jnp.zeros_like(dw0_ref)

        dh = _dot(ddt_ref[...], wdt_ref[...])
        for ref, (lo, hi) in zip((dz_ref, dxbc_ref, dq_ref, dk_ref, dv_ref, dg_ref), IN_SEGMENTS):
            dh = dh + _dot(ref[...], wt_ref[lo:hi, :])
        xf = x_ref[...]
        dx, dw = _rms_bwd(dh, xf, _rstd(xf), w0_ref[...])
        dw0_ref[...] += dw
        gx_ref[...] = dx1_ref[...] + dx

    widths = (SSD_WIDTH, SSD_XBC, RET_QK_W, RET_QK_W, RET_V_W, RET_V_W, LANES)
    return pl.pallas_call(
        body, name="in_bwd", grid=(T // tm,),
        in_specs=[_row_spec(tm, w) for w in widths] + [_row_spec(tm, D_MODEL), _row_spec(tm, D_MODEL),
                                                       _const_spec((1, D_MODEL)), _VMEM_WHOLE, _VMEM_WHOLE],
        out_specs=[_row_spec(tm, D_MODEL), _const_spec((1, D_MODEL))],
        out_shape=[S((T, D_MODEL), F32), S((1, D_MODEL), F32)],
        compiler_params=_cparams("arbitrary"),
    )(dz, dxbc, dq, dk, dv, dg, ddt, x, dx1, w0, wt, wdt)


DW_TILE_BYTES = 6 << 20


def _matmul_tn(a, b, name, tk=1024):
    T, M = a.shape
    N = b.shape[1]
    tm_, tn = M, N
    while tm_ * tn * 4 > DW_TILE_BYTES:
        if tm_ >= tn and tm_ % 256 == 0:
            tm_ //= 2
        elif tn % 256 == 0:
            tn //= 2
        else:
            break
    nk = T // tk

    def body(a_ref, b_ref, o_ref):
        @pl.when(pl.program_id(2) == 0)
        def _():
            o_ref[...] = jnp.zeros_like(o_ref)

        o_ref[...] += _dot_tn(a_ref[...], b_ref[...])

    return pl.pallas_call(
        body, name=name, grid=(M // tm_, N // tn, nk),
        in_specs=[pl.BlockSpec((tk, tm_), lambda m, n, k: (k, m)), pl.BlockSpec((tk, tn), lambda m, n, k: (k, n))],
        out_specs=pl.BlockSpec((tm_, tn), lambda m, n, k: (m, n)),
        out_shape=S((M, N), F32),
        compiler_params=_cparams("parallel", "parallel", "arbitrary"),
    )(a, b)


def _matmul_tn_group(as_, b, name, tk=1024):
    T, N = b.shape
    na = len(as_)

    def body(*refs):
        a_refs, b_ref, o_refs = refs[:na], refs[na], refs[na + 1:]

        @pl.when(pl.program_id(0) == 0)
        def _():
            for o_ref in o_refs:
                o_ref[...] = jnp.zeros_like(o_ref)

        bt = b_ref[...]
        for a_ref, o_ref in zip(a_refs, o_refs):
            o_ref[...] += _dot_tn(a_ref[...], bt)

    return pl.pallas_call(
        body, name=name, grid=(T // tk,),
        in_specs=[_row_spec(tk, a.shape[1]) for a in as_] + [_row_spec(tk, N)],
        out_specs=[_const_spec((a.shape[1], N)) for a in as_],
        out_shape=[S((a.shape[1], N), F32) for a in as_],
        compiler_params=_cparams("arbitrary"),
    )(*as_, b)


def _tri(lower):
    r = lax.broadcasted_iota(jnp.int32, (CH, CH), 0)
    c = lax.broadcasted_iota(jnp.int32, (CH, CH), 1)
    return ((c <= r) if lower else (r <= c)).astype(F32)


def _onehot_row(h):
    return (lax.broadcasted_iota(jnp.int32, (1, LANES), 1) == h).astype(F32)


def _onehot_col(h):
    return (lax.broadcasted_iota(jnp.int32, (LANES, 1), 0) == h).astype(F32)


def _ssd_pre(xc_ref, xh_ref, dtr_ref, cw_ref, cb_ref, dtb_ref, alog_ref, first):
    xc = xc_ref[...].astype(F32)
    xh = jnp.where(first, 0.0, xh_ref[...].astype(F32))
    ext = jnp.concatenate([xh, xc], axis=0)
    u = cb_ref[...] + sum(cw_ref[j:j + 1, :] * _rows_before(ext, SSD_CONV - 1 - j, HALO, CH) for j in range(SSD_CONV))
    sg = _sigmoid(u)
    act = u * sg
    dt = _softplus(dtr_ref[...] + dtb_ref[...])
    a = -jnp.exp(alog_ref[...])
    da = dt * a
    cs = _dot_hi(_tri(True), da)
    cst = _dot_tn_hi(da, _tri(False))
    return xc, u, sg, act, dt, a, cs, cst


HPG = SSD_HEADS // SSD_GROUPS
GW = HPG * SSD_HEAD_DIM


def _expand_heads(src, buf):
    for h in range(SSD_HEADS):
        buf[:, h * SSD_HEAD_DIM:(h + 1) * SSD_HEAD_DIM] = jnp.broadcast_to(src[:, h:h + 1], (CH, SSD_HEAD_DIM))


def _ssd_expanded(act, dt, cs, dtx, csx):
    _expand_heads(dt, dtx)
    _expand_heads(cs, csx)
    csv = csx[...]
    last = csv[CH - 1:CH, :]
    e_exp = jnp.exp(csv)
    dec_exp = jnp.exp(last - csv)
    el_exp = jnp.exp(last)
    xs = act[:, 0:SSD_WIDTH]
    xdt = xs * dtx[...]
    return xs, xdt, xdt * dec_exp, e_exp, dec_exp, el_exp


def _decay_mats(h, cs, cst, transposed):
    r = lax.broadcasted_iota(jnp.int32, (CH, CH), 0)
    c = lax.broadcasted_iota(jnp.int32, (CH, CH), 1)
    c_col = cs[:, h:h + 1]
    c_row = cst[h:h + 1, :]
    if transposed:
        return jnp.exp(jnp.where(r <= c, c_row - c_col, -1e30))
    return jnp.exp(jnp.where(r >= c, c_col - c_row, -1e30))


def _ssd_specs(T):
    nc = T // CH
    return nc, [
        _row_spec(CH, SSD_XBC),
        pl.BlockSpec((HALO, SSD_XBC), lambda i: (jnp.maximum(i * (CH // HALO) - 1, 0), 0)),
        _row_spec(CH, LANES),
        _row_spec(CH, SSD_WIDTH),
    ]


def _groups(act):
    bm = [act[:, SSD_WIDTH + g * SSD_STATE:SSD_WIDTH + (g + 1) * SSD_STATE] for g in range(SSD_GROUPS)]
    o = SSD_WIDTH + SSD_GROUPS * SSD_STATE
    cm = [act[:, o + g * SSD_STATE:o + (g + 1) * SSD_STATE] for g in range(SSD_GROUPS)]
    return bm, cm


def _ssd_fwd(xbc, dtr, z, convw, convb, dtb, alog, dskx, nw):
    T = xbc.shape[0]
    nc, specs = _ssd_specs(T)

    def body(xc_ref, xh_ref, dtr_ref, z_ref, cw_ref, cb_ref, dtb_ref, alog_ref, dskx_ref, nw_ref,
             out_ref, y_ref, st_ref, state, ybuf, dtx, csx):
        i = pl.program_id(0)

        @pl.when(i == 0)
        def _():
            state[...] = jnp.zeros_like(state)

        xc, u, sg, act, dt, a, cs, cst = _ssd_pre(xc_ref, xh_ref, dtr_ref, cw_ref, cb_ref, dtb_ref, alog_ref, i == 0)
        xs, xdt, w, e_exp, dec_exp, el_exp = _ssd_expanded(act, dt, cs, dtx, csx)
        bm, cm = _groups(act)
        groups, heads = range(SSD_GROUPS), range(SSD_HEADS)
        gsl = [slice(g * GW, (g + 1) * GW) for g in groups]
        hsl = [slice(h * SSD_HEAD_DIM, (h + 1) * SSD_HEAD_DIM) for h in heads]
        cb = [_dot_nt(cm[g], bm[g]) for g in groups]
        yoff = [_dot(cm[g], state[g]) for g in groups]
        sloc = [_dot_tn(bm[g], w[:, gsl[g]]) for g in groups]
        lm = [_decay_mats(h, cs, cst, False) for h in heads]
        ydiag = [_dot(cb[h // HPG] * lm[h], xdt[:, hsl[h]]) for h in heads]
        for g in groups:
            st_ref[0, g] = state[g]
            ybuf[:, gsl[g]] = yoff[g] * e_exp[:, gsl[g]] + xs[:, gsl[g]] * dskx_ref[:, gsl[g]]
            state[g] = state[g] * el_exp[:, gsl[g]] + sloc[g]
        for h in heads:
            ybuf[:, hsl[h]] += ydiag[h]
        yv = ybuf[...]
        y_ref[...] = yv.astype(BF)
        zf = z_ref[...].astype(F32)
        gated = yv * (zf * _sigmoid(zf))
        out_ref[...] = (gated * _rstd(gated) * nw_ref[...]).astype(BF)

    st_spec = pl.BlockSpec((1, SSD_GROUPS, SSD_STATE, GW), lambda i: (i, 0, 0, 0))
    return pl.pallas_call(
        body, name="ssd_fwd", grid=(nc,),
        in_specs=specs + [_const_spec((8, SSD_XBC)), _const_spec((1, SSD_XBC)), _const_spec((1, LANES)),
                          _const_spec((1, LANES)), _const_spec((1, SSD_WIDTH)), _const_spec((1, SSD_WIDTH))],
        out_specs=[_row_spec(CH, SSD_WIDTH), _row_spec(CH, SSD_WIDTH), st_spec],
        out_shape=[S((T, SSD_WIDTH), BF), S((T, SSD_WIDTH), BF), S((nc, SSD_GROUPS, SSD_STATE, GW), F32)],
        scratch_shapes=[pltpu.VMEM((SSD_GROUPS, SSD_STATE, GW), F32), pltpu.VMEM((CH, SSD_WIDTH), F32),
                        pltpu.VMEM((CH, SSD_WIDTH), F32), pltpu.VMEM((CH, SSD_WIDTH), F32)],
        compiler_params=_cparams("arbitrary"),
    )(xbc, xbc, dtr, z, convw, convb, dtb, alog, dskx, nw)


def _ssd_bwd(dout, y, xbc, dtr, z, states, convw, convb, dtb, alog, dskx, nw):
    T = xbc.shape[0]
    nc = T // CH
    rev = lambda i: (nc - 1 - i, 0)
    rspec = lambda w: pl.BlockSpec((CH, w), rev)
    halo_spec = pl.BlockSpec((HALO, SSD_XBC), lambda i: (jnp.maximum((nc - 1 - i) * (CH // HALO) - 1, 0), 0))
    NB = SSD_WIDTH
    NC_ = SSD_WIDTH + SSD_GROUPS * SSD_STATE

    def body(do_ref, y_ref, xc_ref, xh_ref, dtr_ref, z_ref, st_ref, cw_ref, cb_ref, dtb_ref, alog_ref, dskx_ref, nw_ref,
             dz_ref, dxbc_ref, ddt_ref, dcw_ref, dcb_ref, ddtb_ref, dalog_ref, ddsk_ref, dnw_ref,
             dstate, ducarry, dtx, csx, dxdtbuf, dact):
        i = pl.program_id(0)

        @pl.when(i == 0)
        def _():
            dstate[...] = jnp.zeros_like(dstate)
            ducarry[...] = jnp.zeros_like(ducarry)
            for ref in (dcw_ref, dcb_ref, ddtb_ref, dalog_ref, ddsk_ref, dnw_ref):
                ref[...] = jnp.zeros_like(ref)

        xc, u, sg, act, dt, a, cs, cst = _ssd_pre(xc_ref, xh_ref, dtr_ref, cw_ref, cb_ref, dtb_ref, alog_ref, i == nc - 1)
        xs, xdt, w, e_exp, dec_exp, el_exp = _ssd_expanded(act, dt, cs, dtx, csx)
        bm, cm = _groups(act)
        yv = y_ref[...].astype(F32)
        zf = z_ref[...].astype(F32)
        sz = _sigmoid(zf)
        gated = yv * (zf * sz)
        dgated, dnw = _rms_bwd(do_ref[...].astype(F32), gated, _rstd(gated), nw_ref[...])
        dnw_ref[...] += dnw
        dz_ref[...] = (dgated * yv * _dsilu(zf, sz)).astype(BF)
        dy = dgated * (zf * sz)
        lane_of = lax.broadcasted_iota(jnp.int32, (SSD_WIDTH, LANES), 0) - SSD_HEAD_DIM * lax.broadcasted_iota(jnp.int32, (SSD_WIDTH, LANES), 1)
        expt = ((lane_of >= 0) & (lane_of < SSD_HEAD_DIM)).astype(F32)
        ddsk_ref[...] += _dot_hi(jnp.sum(dy * xs, axis=0, keepdims=True), expt)
        dcs = jnp.zeros((CH, LANES), F32)
        dcst = jnp.zeros((LANES, CH), F32)
        ddt = jnp.zeros((CH, LANES), F32)
        lastrows = []
        for g in range(SSD_GROUPS):
            gs = slice(g * GW, (g + 1) * GW)
            st = st_ref[0, g]
            dsn = dstate[g]
            cbm = _dot_nt(cm[g], bm[g])
            cbt = _dot_nt(bm[g], cm[g])
            dy_g = dy[:, gs]
            yoff = _dot(cm[g], st) * e_exp[:, gs]
            dq = dy_g * e_exp[:, gs]
            dcm_g = _dot_nt(dq, st)
            dstate[g] = _dot_tn(cm[g], dq) + dsn * el_exp[:, gs]
            dw = _dot(bm[g], dsn)
            w_g = w[:, gs]
            dbm_g = _dot_nt(w_g, dsn)
            dww = dw * w_g
            red = dy_g * yoff - dww
            lastrows.append(jnp.sum(dsn * st, axis=0, keepdims=True) * el_exp[:, gs] + jnp.sum(dww, axis=0, keepdims=True))
            dxdtbuf[:, gs] = dw * dec_exp[:, gs]
            dcb = jnp.zeros((CH, CH), F32)
            hs = range(g * HPG, (g + 1) * HPG)
            hsl = {h: slice(h * SSD_HEAD_DIM, (h + 1) * SSD_HEAD_DIM) for h in hs}
            lm = {h: _decay_mats(h, cs, cst, False) for h in hs}
            dm = {h: _dot_nt(dy[:, hsl[h]], xdt[:, hsl[h]]) for h in hs}
            dxd = {h: _dot(cbt * _decay_mats(h, cs, cst, True), dy[:, hsl[h]]) for h in hs}
            for h in hs:
                sl = hsl[h]
                rl = slice((h - g * HPG) * SSD_HEAD_DIM, (h - g * HPG + 1) * SSD_HEAD_DIM)
                oh = _onehot_row(h)
                dxdt_h = dxdtbuf[:, sl] + dxd[h]
                dxdtbuf[:, sl] = dxdt_h
                dseg = dm[h] * (cbm * lm[h])
                dcb = dcb + dm[h] * lm[h]
                col = jnp.sum(dseg, axis=1, keepdims=True) + jnp.sum(red[:, rl], axis=1, keepdims=True)
                dcs = dcs + col * oh
                dcst = dcst - _onehot_col(h) * jnp.sum(dseg, axis=0, keepdims=True)
                ddt = ddt + jnp.sum(dxdt_h * xs[:, sl], axis=1, keepdims=True) * oh
            dact[:, NB + g * SSD_STATE:NB + (g + 1) * SSD_STATE] = dbm_g + _dot_tn(dcb, cm[g])
            dact[:, NC_ + g * SSD_STATE:NC_ + (g + 1) * SSD_STATE] = dcm_g + _dot(dcb, bm[g])
        dact[:, 0:SSD_WIDTH] = dy * dskx_ref[...] + dxdtbuf[...] * dtx[...]
        dlast = _dot_hi(jnp.concatenate(lastrows, axis=1), expt)
        rows = lax.broadcasted_iota(jnp.int32, (CH, LANES), 0)
        dcs = dcs + _dot_tn_hi(dcst, jnp.eye(LANES, dtype=F32)) + jnp.where(rows == CH - 1, dlast, 0.0)
        dda = _dot_hi(_tri(False), dcs)
        dalog_ref[...] += jnp.sum(dda * dt, axis=0, keepdims=True) * a
        ddt = ddt + dda * a
        ddtr = ddt * _sigmoid(dtr_ref[...] + dtb_ref[...])
        ddtb_ref[...] += jnp.sum(ddtr, axis=0, keepdims=True)
        ddt_ref[...] = ddtr.astype(BF)
        du = dact[...] * _dsilu(u, sg)
        dcb_ref[...] += jnp.sum(du, axis=0, keepdims=True)
        ext = jnp.concatenate([du, ducarry[...]], axis=0)
        ducarry[...] = du[0:8]
        dx = jnp.zeros((CH, SSD_XBC), F32)
        for j in range(SSD_CONV):
            sj = _rows_after(ext, SSD_CONV - 1 - j, CH)
            dx = dx + cw_ref[j:j + 1, :] * sj
            dcw_ref[j:j + 1, :] += jnp.sum(sj * xc, axis=0, keepdims=True)
        dxbc_ref[...] = dx.astype(BF)

    return pl.pallas_call(
        body, name="ssd_bwd", grid=(nc,),
        in_specs=[rspec(SSD_WIDTH), rspec(SSD_WIDTH), rspec(SSD_XBC), halo_spec, rspec(LANES), rspec(SSD_WIDTH),
                  pl.BlockSpec((1, SSD_GROUPS, SSD_STATE, GW), lambda i: (nc - 1 - i, 0, 0, 0)),
                  _const_spec((8, SSD_XBC)), _const_spec((1, SSD_XBC)), _const_spec((1, LANES)),
                  _const_spec((1, LANES)), _const_spec((1, SSD_WIDTH)), _const_spec((1, SSD_WIDTH))],
        out_specs=[rspec(SSD_WIDTH), rspec(SSD_XBC), rspec(LANES),
                   _const_spec((8, SSD_XBC)), _const_spec((1, SSD_XBC)), _const_spec((1, LANES)),
                   _const_spec((1, LANES)), _const_spec((1, LANES)), _const_spec((1, SSD_WIDTH))],
        out_shape=[S((T, SSD_WIDTH), BF), S((T, SSD_XBC), BF), S((T, LANES), BF),
                   S((8, SSD_XBC), F32), S((1, SSD_XBC), F32), S((1, LANES), F32),
                   S((1, LANES), F32), S((1, LANES), F32), S((1, SSD_WIDTH), F32)],
        scratch_shapes=[pltpu.VMEM((SSD_GROUPS, SSD_STATE, GW), F32), pltpu.VMEM((8, SSD_XBC), F32),
                        pltpu.VMEM((CH, SSD_WIDTH), F32), pltpu.VMEM((CH, SSD_WIDTH), F32),
                        pltpu.VMEM((CH, SSD_WIDTH), F32), pltpu.VMEM((CH, SSD_XBC), F32)],
        compiler_params=_cparams("arbitrary"),
    )(dout, y, xbc, xbc, dtr, z, states, convw, convb, dtb, alog, dskx, nw)


def _log_gamma(h):
    return float(np.log1p(-np.exp2(np.float32(-5.0 - h)), dtype=np.float32))


def _swap_halves(t):
    n = t.shape[1]
    lane = lax.broadcasted_iota(jnp.int32, t.shape, 1)
    return jnp.where((lane & (RET_QK - 1)) < RET_QK // 2, pltpu.roll(t, n - RET_QK // 2, 1), pltpu.roll(t, RET_QK // 2, 1))


def _rot(t, cos, sin):
    return t * cos + _swap_halves(t) * sin


def _rot_t(d, cos, sin):
    return d * cos + _swap_halves(d * sin)


def _ret_tables():
    lg = jnp.asarray([_log_gamma(h) for h in range(RET_HEADS)], F32)[:, None, None]
    pos = jnp.arange(CH, dtype=F32)
    rel = pos[:, None] - pos[None, :]
    dmask = jnp.where(rel >= 0, jnp.exp(lg * jnp.maximum(rel, 0.0)), 0.0)
    kdec = jnp.exp(lg * (CH - 1.0 - pos)[None, :, None])
    qdec = jnp.exp(lg * (pos + 1.0)[None, :, None])
    rows = jnp.concatenate([jnp.swapaxes(kdec, 1, 2), jnp.swapaxes(qdec, 1, 2), jnp.zeros((RET_HEADS, CH - 2, CH), F32)], axis=1)
    full = lambda t: jnp.broadcast_to(t, (RET_HEADS, CH, CH))
    return jnp.stack([dmask, jnp.swapaxes(dmask, 1, 2), full(kdec), full(qdec), full(rows)], axis=1)


def _ret_consts(h, rt_ref):
    kdec = rt_ref[h, 2][:, 0:RET_QK]
    qdec = rt_ref[h, 3][:, 0:RET_QK]
    return rt_ref[h, 0], rt_ref[h, 1], kdec, qdec, rt_ref[h, 4, 0:1, :], rt_ref[h, 4, 1:2, :], math.exp(_log_gamma(h) * CH)


_RT_SPEC = pl.BlockSpec((RET_HEADS, 5, CH, CH), lambda i: (0, 0, 0, 0))


def _ret_fwd(q, k, v, g, cos, sin, nw, rt):
    T = q.shape[0]
    nc = T // CH

    def body(q_ref, k_ref, v_ref, g_ref, cos_ref, sin_ref, nw_ref, rt_ref, out_ref, st_ref, state):
        i = pl.program_id(0)

        @pl.when(i == 0)
        def _():
            state[...] = jnp.zeros_like(state)

        cosf = jnp.tile(cos_ref[...], (1, RET_QK_W // LANES))
        sinf = jnp.tile(sin_ref[...], (1, RET_QK_W // LANES))
        qr = _rot(q_ref[...].astype(F32), cosf, sinf)
        kr = _rot(k_ref[...].astype(F32), cosf, sinf) * (RET_QK ** -0.5)
        krt = kr.T
        st_ref[0] = state[...]
        qsl = [slice(h * RET_QK, (h + 1) * RET_QK) for h in range(RET_HEADS)]
        vsl = [slice(h * RET_V, (h + 1) * RET_V) for h in range(RET_HEADS)]
        consts = [_ret_consts(h, rt_ref) for h in range(RET_HEADS)]
        scores = [_dot_nt(qr[:, qsl[h]], kr[:, qsl[h]]) * consts[h][0] for h in range(RET_HEADS)]
        cross = [_dot(qr[:, qsl[h]] * consts[h][3], state[h]) for h in range(RET_HEADS)]
        kv = [_dot(krt[qsl[h], :] * consts[h][4], v_ref[:, vsl[h]]) for h in range(RET_HEADS)]
        o_all = [_dot(scores[h], v_ref[:, vsl[h]]) + cross[h] for h in range(RET_HEADS)]
        for h in range(RET_HEADS):
            state[h] = state[h] * consts[h][6] + kv[h]
        for h in range(RET_HEADS):
            sl = slice(h * RET_V, (h + 1) * RET_V)
            o = o_all[h]
            gf = g_ref[:, sl].astype(F32)
            out_ref[:, sl] = (o * _rstd(o) * nw_ref[:, sl] * (gf * _sigmoid(gf))).astype(BF)

    return pl.pallas_call(
        body, name="ret_fwd", grid=(nc,),
        in_specs=[_row_spec(CH, RET_QK_W), _row_spec(CH, RET_QK_W), _row_spec(CH, RET_V_W), _row_spec(CH, RET_V_W),
                  _row_spec(CH, LANES), _row_spec(CH, LANES), _const_spec((1, RET_V_W)), _RT_SPEC],
        out_specs=[_row_spec(CH, RET_V_W), pl.BlockSpec((1, RET_HEADS, RET_QK, RET_V), lambda i: (i, 0, 0, 0))],
        out_shape=[S((T, RET_V_W), BF), S((nc, RET_HEADS, RET_QK, RET_V), F32)],
        scratch_shapes=[pltpu.VMEM((RET_HEADS, RET_QK, RET_V), F32)],
        compiler_params=_cparams("arbitrary"),
    )(q, k, v, g, cos, sin, nw, rt)


def _ret_bwd(dout, q, k, v, g, states, cos, sin, nw, rt):
    T = q.shape[0]
    nc = T // CH
    rev = lambda i: (nc - 1 - i, 0)
    rspec = lambda w: pl.BlockSpec((CH, w), rev)

    def body(do_ref, q_ref, k_ref, v_ref, g_ref, st_ref, cos_ref, sin_ref, nw_ref, rt_ref,
             dq_ref, dk_ref, dv_ref, dg_ref, dnw_ref, dstate, dqbuf, dkbuf):
        i = pl.program_id(0)

        @pl.when(i == 0)
        def _():
            dstate[...] = jnp.zeros_like(dstate)
            dnw_ref[...] = jnp.zeros_like(dnw_ref)

        cosf = jnp.tile(cos_ref[...], (1, RET_QK_W // LANES))
        sinf = jnp.tile(sin_ref[...], (1, RET_QK_W // LANES))
        qr = _rot(q_ref[...].astype(F32), cosf, sinf)
        kr = _rot(k_ref[...].astype(F32), cosf, sinf) * (RET_QK ** -0.5)
        qrt = qr.T
        heads = range(RET_HEADS)
        qsl = [slice(h * RET_QK, (h + 1) * RET_QK) for h in heads]
        vsl = [slice(h * RET_V, (h + 1) * RET_V) for h in heads]
        do_all = []
        consts = [_ret_consts(h, rt_ref) for h in heads]
        scores = [_dot_nt(qr[:, qsl[h]], kr[:, qsl[h]]) * consts[h][0] for h in heads]
        scores_t = [_dot_nt(kr[:, qsl[h]], qr[:, qsl[h]]) * consts[h][1] for h in heads]
        cross = [_dot(qr[:, qsl[h]] * consts[h][3], st_ref[0, h]) for h in heads]
        o_all = [_dot(scores[h], v_ref[:, vsl[h]]) + cross[h] for h in heads]
        for h in heads:
            o = o_all[h]
            rr = _rstd(o)
            of = o * rr
            gf = g_ref[:, vsl[h]].astype(F32)
            sgg = _sigmoid(gf)
            d_h = do_ref[:, vsl[h]].astype(F32)
            nw_h = nw_ref[:, vsl[h]]
            dg_ref[:, vsl[h]] = (d_h * of * nw_h * _dsilu(gf, sgg)).astype(BF)
            dt_ = d_h * (gf * sgg)
            dnw_ref[:, vsl[h]] += jnp.sum(dt_ * of, axis=0, keepdims=True)
            dof = dt_ * nw_h
            do_all.append(rr * dof - o * (rr * rr * rr) * jnp.mean(dof * o, axis=-1, keepdims=True))
        dsc = [_dot_nt(do_all[h], v_ref[:, vsl[h]]) * consts[h][0] for h in heads]
        dsc_t = [_dot_nt(v_ref[:, vsl[h]], do_all[h]) * consts[h][1] for h in heads]
        dv_a = [_dot(scores_t[h], do_all[h]) for h in heads]
        dv_b = [_dot(kr[:, qsl[h]] * consts[h][2], dstate[h]) for h in heads]
        dq_a = [_dot(dsc[h], kr[:, qsl[h]]) for h in heads]
        dq_b = [_dot_nt(do_all[h], st_ref[0, h]) * consts[h][3] for h in heads]
        dk_a = [_dot(dsc_t[h], qr[:, qsl[h]]) for h in heads]
        dk_b = [_dot_nt(v_ref[:, vsl[h]], dstate[h]) * consts[h][2] for h in heads]
        dst = [_dot(qrt[qsl[h], :] * consts[h][5], do_all[h]) for h in heads]
        for h in heads:
            dv_ref[:, vsl[h]] = (dv_a[h] + dv_b[h]).astype(BF)
            dqbuf[:, qsl[h]] = dq_a[h] + dq_b[h]
            dkbuf[:, qsl[h]] = dk_a[h] + dk_b[h]
            dstate[h] = dstate[h] * consts[h][6] + dst[h]
        dq_ref[...] = _rot_t(dqbuf[...], cosf, sinf).astype(BF)
        dk_ref[...] = (_rot_t(dkbuf[...], cosf, sinf) * (RET_QK ** -0.5)).astype(BF)

    return pl.pallas_call(
        body, name="ret_bwd", grid=(nc,),
        in_specs=[rspec(RET_V_W), rspec(RET_QK_W), rspec(RET_QK_W), rspec(RET_V_W), rspec(RET_V_W),
                  pl.BlockSpec((1, RET_HEADS, RET_QK, RET_V), lambda i: (nc - 1 - i, 0, 0, 0)),
                  rspec(LANES), rspec(LANES), _const_spec((1, RET_V_W)), _RT_SPEC],
        out_specs=[rspec(RET_QK_W), rspec(RET_QK_W), rspec(RET_V_W), rspec(RET_V_W), _const_spec((1, RET_V_W))],
        out_shape=[S((T, RET_QK_W), BF), S((T, RET_QK_W), BF), S((T, RET_V_W), BF), S((T, RET_V_W), BF),
                   S((1, RET_V_W), F32)],
        scratch_shapes=[pltpu.VMEM((RET_HEADS, RET_QK, RET_V), F32), pltpu.VMEM((CH, RET_QK_W), F32),
                        pltpu.VMEM((CH, RET_QK_W), F32)],
        compiler_params=_cparams("arbitrary"),
    )(dout, q, k, v, g, states, cos, sin, nw, rt)


def _exchange(bufs, name, same):
    nb = len(bufs)
    slabs = [tuple(b.shape if same else b.shape[1:]) for b in bufs]

    def body(*refs):
        buf_refs, out_refs, token = refs[:nb], refs[nb:2 * nb], refs[2 * nb]
        send_sems, recv_sems, local_sems = refs[2 * nb + 1:]
        x, y, c = lax.axis_index("x"), lax.axis_index("y"), lax.axis_index("c")
        me = 4 * x + 2 * y + c
        token[...] = jnp.zeros_like(token)

        def src(b, d):
            return buf_refs[b] if same else buf_refs[b].at[d]

        def remote(b, k, to_me):
            px = 1 - x if k & 4 else x
            py = 1 - y if k & 2 else y
            pc = 1 - c if k & 1 else c
            p = 4 * px + 2 * py + pc
            s = b * (N_DEV - 1) + k - 1
            return pltpu.make_async_remote_copy(
                src_ref=src(b, p), dst_ref=out_refs[b].at[me if to_me else p], send_sem=send_sems.at[s],
                recv_sem=recv_sems.at[s], device_id=(px, py, pc), device_id_type=pl.DeviceIdType.MESH)

        local = [pltpu.make_async_copy(src(b, me), out_refs[b].at[me], local_sems.at[b]) for b in range(nb)]
        for cp in local:
            cp.start()
        sends = [remote(b, k, True) for k in range(1, N_DEV) for b in range(nb)]
        for cp in sends:
            cp.start()
        for k in range(1, N_DEV):
            for b in range(nb):
                remote(b, k, False).wait_recv()
        for cp in sends:
            cp.wait_send()
        for cp in local:
            cp.wait()

    any_spec = pl.BlockSpec(memory_space=pl.ANY)
    out = pl.pallas_call(
        body, name=name,
        in_specs=[any_spec] * nb, out_specs=[any_spec] * nb + [pl.BlockSpec(memory_space=pltpu.VMEM)],
        out_shape=[S((N_DEV,) + s, b.dtype) for s, b in zip(slabs, bufs)] + [S((8, LANES), F32)],
        scratch_shapes=[pltpu.SemaphoreType.DMA((nb * (N_DEV - 1),)), pltpu.SemaphoreType.DMA((nb * (N_DEV - 1),)),
                        pltpu.SemaphoreType.DMA((nb,))],
    )(*bufs)
    return list(out[:nb]), out[nb]


_HBM = pl.BlockSpec(memory_space=pltpu.HBM)
_SEM = pl.BlockSpec(memory_space=pltpu.SEMAPHORE)
_EFFECT = pltpu.SideEffectType.DATAFLOW_SIDE_EFFECTING


ALL_PEERS = tuple(range(1, N_DEV))
ONE_PER_CHIP = (1, 2, 4, 6)
OTHER_CHIPS = (2, 4, 6)


def _split_copies(buf_refs, land_refs, send_sems, recv_sems, same, to_me, ks):
    x, y, c = lax.axis_index("x"), lax.axis_index("y"), lax.axis_index("c")
    me = 4 * x + 2 * y + c
    cps = []
    for ki, k in enumerate(ks):
        px = 1 - x if k & 4 else x
        py = 1 - y if k & 2 else y
        pc = 1 - c if k & 1 else c
        p = 4 * px + 2 * py + pc
        for b in range(len(buf_refs)):
            s = b * len(ks) + ki
            cps.append(pltpu.make_async_remote_copy(
                src_ref=buf_refs[b] if same else buf_refs[b].at[p], dst_ref=land_refs[b].at[me if to_me else p],
                send_sem=send_sems.at[s], recv_sem=recv_sems.at[s], device_id=(px, py, pc), device_id_type=pl.DeviceIdType.MESH))
    return cps


def _exchange_start(bufs, name, same, ks=ALL_PEERS):
    nb = len(bufs)
    ns = nb * len(ks)
    lands = [lax.empty((N_DEV,) + tuple(b.shape if same else b.shape[1:]), b.dtype) for b in bufs]

    def body(*refs):
        buf_refs, land_refs = refs[:nb], refs[nb:2 * nb]
        send_sems, recv_sems = refs[2 * nb], refs[2 * nb + 1]
        token = refs[-1]
        for cp in _split_copies(buf_refs, land_refs, send_sems, recv_sems, same, True, ks):
            cp.start()
        token[...] = jnp.zeros_like(token)

    hbm = lambda a: pltpu.with_memory_space_constraint(a, pltpu.HBM)
    out = pl.pallas_call(
        body, name=name,
        out_shape=(pltpu.SemaphoreType.DMA((ns,)), pltpu.SemaphoreType.DMA((ns,)),
                   *[pltpu.HBM(a.shape, a.dtype) for a in list(bufs) + lands], S((8, LANES), F32)),
        in_specs=[_HBM] * (2 * nb), out_specs=(_SEM, _SEM, *[_HBM] * (2 * nb), pl.BlockSpec(memory_space=pltpu.VMEM)),
        input_output_aliases={i: 2 + i for i in range(2 * nb)},
        compiler_params=pltpu.CompilerParams(has_side_effects=_EFFECT),
    )(*[hbm(a) for a in list(bufs) + lands])
    return out[0], out[1], list(out[2:2 + nb]), list(out[2 + nb:2 + 2 * nb]), out[-1]


def _exchange_wait(started, after, name, same, ks=ALL_PEERS):
    send_sems, recv_sems, bufs, lands, _ = started
    nb = len(bufs)
    after = list(after) if isinstance(after, (list, tuple)) else [after]

    def body(*refs):
        buf_refs, land_refs = refs[:nb], refs[nb:2 * nb]
        s_sems, r_sems = refs[2 * nb], refs[2 * nb + 1]
        for cp in _split_copies(buf_refs, land_refs, s_sems, r_sems, same, False, ks):
            cp.wait_send()
            cp.wait_recv()

    out = pl.pallas_call(
        body, name=name,
        out_shape=tuple(pltpu.HBM(a.shape, a.dtype) for a in bufs + lands),
        in_specs=[_HBM] * (2 * nb) + [_SEM, _SEM] + [pl.BlockSpec(memory_space=pl.ANY)] * len(after),
        out_specs=tuple([_HBM] * (2 * nb)),
        input_output_aliases={i: i for i in range(2 * nb)},
        compiler_params=pltpu.CompilerParams(has_side_effects=_EFFECT),
    )(*bufs, *lands, send_sems, recv_sems, *after)
    return list(out[:nb]), list(out[nb:])


def _forward_copies(land_ref, send_sems, recv_sems, sending):
    x, y, c = lax.axis_index("x"), lax.axis_index("y"), lax.axis_index("c")
    cps = []
    for ki, k in enumerate(OTHER_CHIPS):
        px = 1 - x if k & 4 else x
        py = 1 - y if k & 2 else y
        q = 4 * px + 2 * py + (c if sending else 1 - c)
        cps.append(pltpu.make_async_remote_copy(
            src_ref=land_ref.at[q], dst_ref=land_ref.at[q], send_sem=send_sems.at[ki], recv_sem=recv_sems.at[ki],
            device_id=(x, y, 1 - c), device_id_type=pl.DeviceIdType.MESH))
    return cps


def _forward_start(land, name):
    def body(land_ref, send_sems, recv_sems, land_thru, token):
        for cp in _forward_copies(land_ref, send_sems, recv_sems, True):
            cp.start()
        token[...] = jnp.zeros_like(token)

    n = len(OTHER_CHIPS)
    out = pl.pallas_call(
        body, name=name,
        out_shape=(pltpu.SemaphoreType.DMA((n,)), pltpu.SemaphoreType.DMA((n,)), pltpu.HBM(land.shape, land.dtype),
                   S((8, LANES), F32)),
        in_specs=[_HBM], out_specs=(_SEM, _SEM, _HBM, pl.BlockSpec(memory_space=pltpu.VMEM)),
        input_output_aliases={0: 2},
        compiler_params=pltpu.CompilerParams(has_side_effects=_EFFECT),
    )(pltpu.with_memory_space_constraint(land, pltpu.HBM))
    return out


def _forward_wait(started, after, name):
    send_sems, recv_sems, land, _ = started
    after = list(after) if isinstance(after, (list, tuple)) else [after]

    def body(land_ref, s_sems, r_sems, *rest):
        for cp in _forward_copies(land_ref, s_sems, r_sems, False):
            cp.wait_send()
            cp.wait_recv()

    return pl.pallas_call(
        body, name=name, out_shape=pltpu.HBM(land.shape, land.dtype),
        in_specs=[_HBM, _SEM, _SEM] + [pl.BlockSpec(memory_space=pl.ANY)] * len(after), out_specs=_HBM,
        input_output_aliases={0: 0},
        compiler_params=pltpu.CompilerParams(has_side_effects=_EFFECT),
    )(land, send_sems, recv_sems, *after)


def _sum_slabs(recv, name):
    n, R, _ = recv.shape

    def body(r_ref, o_ref):
        g = r_ref[0].astype(F32)
        for s in range(1, n):
            g = g + r_ref[s].astype(F32)
        o_ref[...] = g

    return pl.pallas_call(body, name=name, out_shape=S((R, LANES), F32))(recv)


def _adamw(recv, w, m, v, name, tr, tc=None):
    n, R, C = recv.shape
    c1 = 1.0 - ADAM_B1 ** ADAM_STEP
    c2 = 1.0 - ADAM_B2 ** ADAM_STEP

    def body(r_ref, w_ref, m_ref, v_ref, g_out, d_out, m_out, v_out):
        g = r_ref[0].astype(F32)
        for s in range(1, n):
            g = g + r_ref[s].astype(F32)
        mm = ADAM_B1 * m_ref[...] + (1.0 - ADAM_B1) * g
        vv = ADAM_B2 * v_ref[...] + (1.0 - ADAM_B2) * (g * g)
        g_out[...] = g
        m_out[...] = mm
        v_out[...] = vv
        d_out[...] = -ADAM_LR * ((mm / c1) / (jnp.sqrt(vv / c2) + ADAM_EPS) + ADAM_WD * w_ref[...])

    tc = C if tc is None else tc
    spec = pl.BlockSpec((tr, tc), lambda i, j: (i, j))
    return pl.pallas_call(
        body, name=name, grid=(R // tr, C // tc),
        in_specs=[pl.BlockSpec((n, tr, tc), lambda i, j: (0, i, j)), spec, spec, spec],
        out_specs=[spec] * 4, out_shape=[S((R, C), F32)] * 4,
        compiler_params=_cparams("parallel", "parallel"),
    )(recv, w, m, v)


def _pack(parts, rows):
    cols = []
    for p in parts:
        f = p.reshape(-1)
        cols.append(jnp.pad(f, (0, (-f.shape[0]) % LANES)))
    flat = jnp.concatenate(cols)
    return jnp.pad(flat, (0, rows * LANES - flat.shape[0])).reshape(rows, LANES)


def _unpack(buf, shapes):
    flat = buf.reshape(-1)
    out, o = [], 0
    for shp in shapes:
        n = int(np.prod(shp))
        out.append(flat[o:o + n].reshape(shp))
        o += n + (-n) % LANES
    return out


SMALL_ROWS = 200
CONV_ROWS = 16


def kernel(x, pre_mix_norm_w, w_in, ssd_conv_w, ssd_conv_b, ssd_dt_bias, ssd_a_log, ssd_d, ssd_norm_w, ret_norm_w, w_out, post_mix_norm_w, pre_ffn_norm_w, w_up, ffn_conv_w, ffn_conv_b, w_down, post_ffn_norm_w, loss_target, m_pre_mix_norm_w, m_w_in, m_ssd_conv_w, m_ssd_conv_b, m_ssd_dt_bias, m_ssd_a_log, m_ssd_d, m_ssd_norm_w, m_ret_norm_w, m_w_out, m_post_mix_norm_w, m_pre_ffn_norm_w, m_w_up, m_ffn_conv_w, m_ffn_conv_b, m_w_down, m_post_ffn_norm_w, v_pre_mix_norm_w, v_w_in, v_ssd_conv_w, v_ssd_conv_b, v_ssd_dt_bias, v_ssd_a_log, v_ssd_d, v_ssd_norm_w, v_ret_norm_w, v_w_out, v_post_mix_norm_w, v_pre_ffn_norm_w, v_w_up, v_ffn_conv_w, v_ffn_conv_b, v_w_down, v_post_ffn_norm_w):
    T = x.shape[1]
    xi, tgt = x[0], loss_target[0]
    me = 4 * lax.axis_index("x") + 2 * lax.axis_index("y") + lax.axis_index("c")
    n_in, n_up = w_in.shape[2], w_up.shape[2]
    n_out, n_down = w_out.shape[1], w_down.shape[1]
    n_sc, n_fc = ssd_conv_w.shape[2], ffn_conv_w.shape[2]

    def after(token, value):
        return value * (1.0 + token[0, 0])

    def finish(started, after_value, name, same):
        bufs, lands = _exchange_wait(started, after_value, name, same)
        own = [b if same else lax.dynamic_index_in_dim(b, me, 0, keepdims=False) for b in bufs]
        return [lax.dynamic_update_index_in_dim(l, o, me, 0) for l, o in zip(lands, own)]

    (gconv,), tok_conv = _exchange([_pack([ssd_conv_w, ffn_conv_w], CONV_ROWS)], "gather_conv", True)
    tr_ = lambda w: jnp.transpose(w[0])
    gat_in = _exchange_start([after(tok_conv, tr_(w_in)).astype(BF)], "gather_in_start", True, ONE_PER_CHIP)
    convs = [_unpack(gconv[d], [(SSD_CONV, n_sc), (FFN_CONV, n_fc)]) for d in range(N_DEV)]
    scw = jnp.pad(jnp.concatenate([c[0] for c in convs], axis=1), ((0, 8 - SSD_CONV), (0, 0)))
    fcw = jnp.pad(jnp.concatenate([c[1] for c in convs], axis=1), ((0, 8 - FFN_CONV), (0, 0)))
    pad_h = lambda p: jnp.pad(p, ((0, 0), (0, LANES - SSD_HEADS)))
    dtb, alog = pad_h(ssd_dt_bias), pad_h(ssd_a_log)
    dskx = jnp.repeat(ssd_d, SSD_HEAD_DIM, axis=1)
    inv = ROPE_BASE ** (-jnp.arange(0, RET_QK, 2, dtype=F32) / RET_QK)
    ang = jnp.arange(T, dtype=F32)[:, None] * inv[None, :]
    cs_, sn_ = jnp.cos(ang), jnp.sin(ang)
    cos = jnp.concatenate([cs_, cs_, cs_, cs_], axis=1)
    sin = jnp.concatenate([-sn_, sn_, -sn_, sn_], axis=1)
    rtab = _ret_tables()
    shard_in, land_in = _exchange_wait(gat_in, [cos, sin, rtab, scw, fcw], "gather_in_wait", True, ONE_PER_CHIP)
    fwd_in_ = _forward_start(land_in[0], "gather_in_forward")
    gat_rest = _exchange_start([after(fwd_in_[3], w).astype(BF) for w in (w_out[0], tr_(w_up), w_down[0])], "gather_rest_start", True)
    g_in = lax.dynamic_update_index_in_dim(_forward_wait(fwd_in_, gat_rest[4], "gather_in_forward_wait"), shard_in[0], me, 0)
    wt = g_in.reshape(N_DEV * n_in, D_MODEL)
    wdt = jnp.pad(wt[O_DT:O_Q], ((0, LANES - SSD_HEADS), (0, 0)))

    h, z, xbc, q, k, v, g, dtr = _fwd_in(xi, pre_mix_norm_w, wt, wdt)
    ys, ypre, sst = _ssd_fwd(xbc, dtr, z, scw, ssd_conv_b, dtb, alog, dskx, ssd_norm_w)
    yr, rst = _ret_fwd(q, k, v, g, cos, sin, ret_norm_w, rtab)
    g_out, g_up, g_down = finish(gat_rest, yr, "gather_rest_wait", True)
    wout = g_out.reshape(N_DEV * n_out, D_MODEL)
    wup = g_up.reshape(N_DEV * n_up, D_MODEL)
    wdown = g_down.reshape(N_DEV * n_down, D_MODEL)
    y, x1, h2, graw, val = _fwd_mid(ys, yr, xi, wout, post_mix_norm_w, pre_ffn_norm_w, wup)
    a, dfb, dval, dgate, dx2, lossb, d_pff, d_fcb = _ffn_tail(graw, val, x1, tgt, fcw, ffn_conv_b, wdown, post_ffn_norm_w)
    gdown = _matmul_tn(a, dfb, "dw_down")
    sc_down = _exchange_start([gdown.reshape(N_DEV, n_down, D_MODEL).astype(BF)], "scatter_down_start", False)
    dgraw, dx1, dyb, dys, dyr, d_fcw, d_pf, d_pm = _ffn_bwd(dgate, dval, graw, x1, dx2, y, after(sc_down[4], fcw), wup,
                                                         pre_ffn_norm_w, post_mix_norm_w, wout)
    gup = jnp.concatenate([_matmul_tn(dgraw, h2, "dw_up_g"), _matmul_tn(dval, h2, "dw_up_v")], axis=0)
    gout = jnp.concatenate(_matmul_tn_group([ys, yr], dyb, "dw_out"), axis=0)
    sc_mid = _exchange_start([gup.reshape(N_DEV, n_up, D_MODEL).astype(BF), gout.reshape(N_DEV, n_out, D_MODEL).astype(BF)],
                             "scatter_mid_start", False)
    dz, dxbc, ddt, d_scw, d_scb, d_dtb, d_alog, d_dsk, d_snw = _ssd_bwd(dys, ypre, xbc, dtr, z, sst, after(sc_mid[4], scw),
                                                                      ssd_conv_b, dtb, alog, dskx, ssd_norm_w)
    dq, dk, dv, dg, d_rnw = _ret_bwd(dyr, q, k, v, g, rst, cos, sin, ret_norm_w, rtab)
    g_q, g_k, g_v, g_g = _matmul_tn_group([dq, dk, dv, dg], h, "dw_ret")
    g_z, g_xbc, g_dt = _matmul_tn_group([dz, dxbc, ddt], h, "dw_ssd")
    gin = jnp.concatenate([g_z, g_xbc, g_dt[:SSD_HEADS], g_q, g_k, g_v, g_g], axis=0)
    sc_in = _exchange_start([gin.reshape(N_DEV, n_in, D_MODEL).astype(BF)], "scatter_in_start", False)
    gx, d_w0 = _in_bwd(dz, dxbc, dq, dk, dv, dg, ddt, xi, dx1, after(sc_in[4], pre_mix_norm_w), wt, wdt)
    r_down, = finish(sc_down, gx, "scatter_down_wait", False)
    r_up, r_out = finish(sc_mid, r_down, "scatter_mid_wait", False)
    per_w = [None] * 4
    per_w[3] = _adamw(r_down, w_down[0], m_w_down[0], v_w_down[0], "adamw_down", n_down)
    per_w[2] = [jnp.transpose(t) for t in _adamw(r_up, tr_(w_up), tr_(m_w_up), tr_(v_w_up), "adamw_up", n_up, 256)]
    per_w[1] = _adamw(r_out, w_out[0], m_w_out[0], v_w_out[0], "adamw_out", n_out)
    r_in, = finish(sc_in, per_w[1][0], "scatter_in_wait", False)
    per_w[0] = [jnp.transpose(t) for t in _adamw(r_in, tr_(w_in), tr_(m_w_in), tr_(v_w_in), "adamw_in", n_in, 256)]
    big = [[per_w[i][kind][None] for i in range(4)] for kind in range(4)]

    small_full = [d_w0, d_scw[:SSD_CONV], d_scb, d_dtb[:, :SSD_HEADS], d_alog[:, :SSD_HEADS], d_dsk[:, :SSD_HEADS], d_snw, d_rnw,
                  d_pm, d_pf, d_fcw[:FFN_CONV], d_fcb, d_pff, lossb[0:1, 0:1]]
    full_shapes = [t.shape for t in small_full]
    gs = _sum_slabs(_exchange([_pack(small_full, SMALL_ROWS)], "gather_small", True)[0][0], "sum_small")
    gfull = _unpack(gs, full_shapes)
    gfull[1] = lax.dynamic_slice_in_dim(gfull[1], me * n_sc, n_sc, axis=1)
    gfull[10] = lax.dynamic_slice_in_dim(gfull[10], me * n_fc, n_fc, axis=1)
    ws = [pre_mix_norm_w, ssd_conv_w, ssd_conv_b, ssd_dt_bias, ssd_a_log, ssd_d, ssd_norm_w, ret_norm_w, post_mix_norm_w,
          pre_ffn_norm_w, ffn_conv_w, ffn_conv_b, post_ffn_norm_w]
    ms = [m_pre_mix_norm_w, m_ssd_conv_w, m_ssd_conv_b, m_ssd_dt_bias, m_ssd_a_log, m_ssd_d, m_ssd_norm_w, m_ret_norm_w,
          m_post_mix_norm_w, m_pre_ffn_norm_w, m_ffn_conv_w, m_ffn_conv_b, m_post_ffn_norm_w]
    vs = [v_pre_mix_norm_w, v_ssd_conv_w, v_ssd_conv_b, v_ssd_dt_bias, v_ssd_a_log, v_ssd_d, v_ssd_norm_w, v_ret_norm_w,
          v_post_mix_norm_w, v_pre_ffn_norm_w, v_ffn_conv_w, v_ffn_conv_b, v_post_ffn_norm_w]
    out_shapes = [t.shape for t in ws]
    loss = gfull.pop()[0, 0]
    small = _adamw(_pack(gfull, SMALL_ROWS)[None], _pack(ws, SMALL_ROWS), _pack(ms, SMALL_ROWS), _pack(vs, SMALL_ROWS),
                   "adamw_small", SMALL_ROWS)
    small = [_unpack(b, out_shapes) for b in small]

    order = {"pre_mix_norm_w": ("s", 0), "w_in": ("b", 0), "ssd_conv_w": ("s", 1), "ssd_conv_b": ("s", 2),
             "ssd_dt_bias": ("s", 3), "ssd_a_log": ("s", 4), "ssd_d": ("s", 5), "ssd_norm_w": ("s", 6), "ret_norm_w": ("s", 7),
             "w_out": ("b", 1), "post_mix_norm_w": ("s", 8), "pre_ffn_norm_w": ("s", 9), "w_up": ("b", 2),
             "ffn_conv_w": ("s", 10), "ffn_conv_b": ("s", 11), "w_down": ("b", 3), "post_ffn_norm_w": ("s", 12)}
    outs = [loss, gx[None]]
    for kind in range(4):
        for name, (grp, idx) in order.items():
            outs.append(big[kind][idx] if grp == "b" else small[kind][idx])
    return tuple(outs)
```

```python
import functools
import math

import numpy as np
import jax
import jax.numpy as jnp
from jax import lax
from jax.experimental import pallas as pl
from jax.experimental.pallas import tpu as pltpu

F32 = jnp.float32
BF = jnp.bfloat16
HI = lax.Precision.HIGHEST
S = jax.ShapeDtypeStruct

D_MODEL = 1024
SSD_HEADS = 16
SSD_HEAD_DIM = 64
SSD_GROUPS = 2
SSD_STATE = 128
SSD_WIDTH = 1024
SSD_XBC = 1536
SSD_CONV = 4
RET_HEADS = 8
RET_QK = 64
RET_V = 128
RET_QK_W = 512
RET_V_W = 1024
ROPE_BASE = 10000.0
CH = 128
D_FF = 2816
FFN_CONV = 3
EPS = 1e-6
IN_WIDTH = 5648
N_DEV = 8

ADAM_LR = 0.001
ADAM_B1 = 0.9
ADAM_B2 = 0.999
ADAM_EPS = 1e-08
ADAM_WD = 0.01
ADAM_STEP = 10

LANES = 128
HALO = 16
VMEM_LIMIT = 48 * 1024 * 1024

O_Z, O_XBC, O_DT, O_Q, O_K, O_V, O_G, O_END = 0, 1024, 2560, 2576, 3088, 3600, 4624, 5648
IN_SEGMENTS = ((O_Z, O_XBC), (O_XBC, O_DT), (O_Q, O_K), (O_K, O_V), (O_V, O_G), (O_G, O_END))


def _cparams(*sem):
    return pltpu.CompilerParams(dimension_semantics=sem, vmem_limit_bytes=VMEM_LIMIT)


def _dot(a, b):
    return jnp.dot(a.astype(BF), b.astype(BF), preferred_element_type=F32)


def _dot_nt(a, b):
    return lax.dot_general(a.astype(BF), b.astype(BF), (((1,), (1,)), ((), ())), preferred_element_type=F32)


def _dot_tn(a, b):
    return lax.dot_general(a.astype(BF), b.astype(BF), (((0,), (0,)), ((), ())), preferred_element_type=F32)


def _dot_hi(a, b):
    return jnp.dot(a, b, preferred_element_type=F32, precision=HI)


def _dot_tn_hi(a, b):
    return lax.dot_general(a, b, (((0,), (0,)), ((), ())), preferred_element_type=F32, precision=HI)


def _sigmoid(x):
    return jax.nn.sigmoid(x)


def _dsilu(x, s):
    return s * (1.0 + x * (1.0 - s))


def _softplus(x):
    return jnp.maximum(x, 0.0) + jnp.log1p(jnp.exp(-jnp.abs(x)))


def _rstd(x):
    return lax.rsqrt(jnp.mean(x * x, axis=-1, keepdims=True) + EPS)


def _rms_bwd(dy, x, r, w):
    gn = dy * w
    dx = r * gn - x * (r * r * r) * jnp.mean(gn * x, axis=-1, keepdims=True)
    dw = jnp.sum(dy * x * r, axis=0, keepdims=True)
    return dx, dw


def _rows_before(ext, s, head, n):
    if s == 0:
        return ext[head:head + n]
    return pltpu.roll(ext, s, 0)[head:head + n]


def _rows_after(ext, s, n):
    if s == 0:
        return ext[0:n]
    return pltpu.roll(ext, ext.shape[0] - s, 0)[0:n]


def _row_spec(tm, width):
    return pl.BlockSpec((tm, width), lambda i: (i, 0))


def _const_spec(shape):
    return pl.BlockSpec(shape, lambda i: (0,) * len(shape))


_VMEM_WHOLE = pl.BlockSpec(memory_space=pltpu.VMEM)


def _fwd_in(x, w0, wt, wdt, tm=512):
    T = x.shape[0]

    def body(x_ref, w0_ref, wt_ref, wdt_ref, h_ref, z_ref, xbc_ref, q_ref, k_ref, v_ref, g_ref, dt_ref):
        xf = x_ref[...]
        h = (xf * _rstd(xf) * w0_ref[...]).astype(BF)
        h_ref[...] = h
        for ref, (lo, hi) in zip((z_ref, xbc_ref, q_ref, k_ref, v_ref, g_ref), IN_SEGMENTS):
            ref[...] = _dot_nt(h, wt_ref[lo:hi, :]).astype(ref.dtype)
        dt_ref[...] = _dot_nt(h, wdt_ref[...])

    widths = (D_MODEL, SSD_WIDTH, SSD_XBC, RET_QK_W, RET_QK_W, RET_V_W, RET_V_W)
    return pl.pallas_call(
        body, name="fwd_in", grid=(T // tm,),
        in_specs=[_row_spec(tm, D_MODEL), _const_spec((1, D_MODEL)), _VMEM_WHOLE, _VMEM_WHOLE],
        out_specs=[_row_spec(tm, w) for w in widths] + [_row_spec(tm, LANES)],
        out_shape=[S((T, w), BF) for w in widths] + [S((T, LANES), F32)],
        compiler_params=_cparams("parallel"),
    )(x, w0, wt, wdt)


def _fwd_mid(ys, yr, x, wout, wpm, wpf, wup, tm=512):
    T = x.shape[0]

    def body(ys_ref, yr_ref, x_ref, wout_ref, wpm_ref, wpf_ref, wup_ref, y_ref, x1_ref, h2_ref, graw_ref, val_ref):
        y = (jnp.dot(ys_ref[...], wout_ref[0:SSD_WIDTH, :], preferred_element_type=F32)
             + jnp.dot(yr_ref[...], wout_ref[SSD_WIDTH:, :], preferred_element_type=F32))
        y_ref[...] = y
        x1 = x_ref[...] + y * _rstd(y) * wpm_ref[...]
        x1_ref[...] = x1
        h2 = (x1 * _rstd(x1) * wpf_ref[...]).astype(BF)
        h2_ref[...] = h2
        graw_ref[...] = _dot_nt(h2, wup_ref[0:D_FF, :]).astype(BF)
        val_ref[...] = _dot_nt(h2, wup_ref[D_FF:, :]).astype(BF)

    return pl.pallas_call(
        body, name="fwd_mid", grid=(T // tm,),
        in_specs=[_row_spec(tm, SSD_WIDTH), _row_spec(tm, RET_V_W), _row_spec(tm, D_MODEL), _VMEM_WHOLE,
                  _const_spec((1, D_MODEL)), _const_spec((1, D_MODEL)), _VMEM_WHOLE],
        out_specs=[_row_spec(tm, D_MODEL), _row_spec(tm, D_MODEL), _row_spec(tm, D_MODEL), _row_spec(tm, D_FF),
                   _row_spec(tm, D_FF)],
        out_shape=[S((T, D_MODEL), F32), S((T, D_MODEL), F32), S((T, D_MODEL), BF), S((T, D_FF), BF), S((T, D_FF), BF)],
        compiler_params=_cparams("parallel"),
    )(ys, yr, x, wout, wpm, wpf, wup)


def _ffn_tail(graw, val, x1, tgt, convw, convb, wdown, wpff, tm=256):
    T = x1.shape[0]

    def body(graw_ref, val_ref, x1_ref, tgt_ref, cw_ref, cb_ref, wd_ref, wpff_ref,
             a_ref, df_ref, dval_ref, dgate_ref, dx2_ref, loss_ref, dwpff_ref, dcb_ref, carry):
        i = pl.program_id(0)

        @pl.when(i == 0)
        def _():
            carry[...] = jnp.zeros_like(carry)
            loss_ref[...] = jnp.zeros_like(loss_ref)
            dwpff_ref[...] = jnp.zeros_like(dwpff_ref)
            dcb_ref[...] = jnp.zeros_like(dcb_ref)

        g = graw_ref[...].astype(F32)
        ext = jnp.concatenate([carry[...], g], axis=0)
        carry[...] = g[tm - 8:tm]
        gate = cb_ref[...] + sum(cw_ref[j:j + 1, :] * _rows_before(ext, FFN_CONV - 1 - j, 8, tm) for j in range(FFN_CONV))
        sg = _sigmoid(gate)
        silu = gate * sg
        v = val_ref[...].astype(F32)
        a = (silu * v).astype(BF)
        a_ref[...] = a
        f = jnp.dot(a, wd_ref[...], preferred_element_type=F32)
        r = _rstd(f)
        w = wpff_ref[...]
        e = x1_ref[...] + f * r * w - tgt_ref[...]
        loss_ref[...] += jnp.sum(e * e) * (0.5 / D_MODEL)
        dx2 = e * (1.0 / D_MODEL)
        dx2_ref[...] = dx2
        df, dw = _rms_bwd(dx2, f, r, w)
        dwpff_ref[...] += dw
        dfb = df.astype(BF)
        df_ref[...] = dfb
        da = _dot_nt(dfb, wd_ref[...])
        dval_ref[...] = (da * silu).astype(BF)
        dgate = da * v * _dsilu(gate, sg)
        dcb_ref[...] += jnp.sum(dgate, axis=0, keepdims=True)
        dgate_ref[...] = dgate.astype(BF)

    return pl.pallas_call(
        body, name="ffn_tail", grid=(T // tm,),
        in_specs=[_row_spec(tm, D_FF), _row_spec(tm, D_FF), _row_spec(tm, D_MODEL), _row_spec(tm, D_MODEL),
                  _const_spec((8, D_FF)), _const_spec((1, D_FF)), _VMEM_WHOLE, _const_spec((1, D_MODEL))],
        out_specs=[_row_spec(tm, D_FF), _row_spec(tm, D_MODEL), _row_spec(tm, D_FF), _row_spec(tm, D_FF),
                   _row_spec(tm, D_MODEL), _const_spec((8, LANES)), _const_spec((1, D_MODEL)), _const_spec((1, D_FF))],
        out_shape=[S((T, D_FF), BF), S((T, D_MODEL), BF), S((T, D_FF), BF), S((T, D_FF), BF), S((T, D_MODEL), F32),
                   S((8, LANES), F32), S((1, D_MODEL), F32), S((1, D_FF), F32)],
        scratch_shapes=[pltpu.VMEM((8, D_FF), F32)],
        compiler_params=_cparams("arbitrary"),
    )(graw, val, x1, tgt, convw, convb, wdown, wpff)


def _ffn_bwd(dgate, dval, graw, x1, dx2, y, convw, wup, wpf, wpm, wout, tm=256):
    T = x1.shape[0]
    nt = T // tm
    rev = lambda i: (nt - 1 - i, 0)
    rspec = lambda w: pl.BlockSpec((tm, w), rev)

    def body(dgate_ref, dval_ref, graw_ref, x1_ref, dx2_ref, y_ref, cw_ref, wup_ref, wpf_ref, wpm_ref, wout_ref,
             dgraw_ref, dx1_ref, dy_ref, dys_ref, dyr_ref, dcw_ref, dwpf_ref, dwpm_ref, carry):
        i = pl.program_id(0)

        @pl.when(i == 0)
        def _():
            carry[...] = jnp.zeros_like(carry)
            dcw_ref[...] = jnp.zeros_like(dcw_ref)
            dwpf_ref[...] = jnp.zeros_like(dwpf_ref)
            dwpm_ref[...] = jnp.zeros_like(dwpm_ref)

        dg = dgate_ref[...].astype(F32)
        ext = jnp.concatenate([dg, carry[...]], axis=0)
        carry[...] = dg[0:8]
        g = graw_ref[...].astype(F32)
        dgraw = jnp.zeros((tm, D_FF), F32)
        for j in range(FFN_CONV):
            sj = _rows_after(ext, FFN_CONV - 1 - j, tm)
            dgraw = dgraw + cw_ref[j:j + 1, :] * sj
            dcw_ref[j:j + 1, :] += jnp.sum(sj * g, axis=0, keepdims=True)
        dgrawb = dgraw.astype(BF)
        dgraw_ref[...] = dgrawb
        dh2 = _dot(dgrawb, wup_ref[0:D_FF, :]) + _dot(dval_ref[...], wup_ref[D_FF:, :])
        x1 = x1_ref[...]
        dxa, dw = _rms_bwd(dh2, x1, _rstd(x1), wpf_ref[...])
        dwpf_ref[...] += dw
        dx1 = dx2_ref[...] + dxa
        dx1_ref[...] = dx1
        yv = y_ref[...]
        dy, dw = _rms_bwd(dx1, yv, _rstd(yv), wpm_ref[...])
        dwpm_ref[...] += dw
        dyb = dy.astype(BF)
        dy_ref[...] = dyb
        dys_ref[...] = _dot_nt(dyb, wout_ref[0:SSD_WIDTH, :]).astype(BF)
        dyr_ref[...] = _dot_nt(dyb, wout_ref[SSD_WIDTH:, :]).astype(BF)

    return pl.pallas_call(
        body, name="ffn_bwd", grid=(nt,),
        in_specs=[rspec(D_FF), rspec(D_FF), rspec(D_FF), rspec(D_MODEL), rspec(D_MODEL), rspec(D_MODEL),
                  _const_spec((8, D_FF)), _VMEM_WHOLE, _const_spec((1, D_MODEL)), _const_spec((1, D_MODEL)), _VMEM_WHOLE],
        out_specs=[rspec(D_FF), rspec(D_MODEL), rspec(D_MODEL), rspec(SSD_WIDTH), rspec(RET_V_W),
                   _const_spec((8, D_FF)), _const_spec((1, D_MODEL)), _const_spec((1, D_MODEL))],
        out_shape=[S((T, D_FF), BF), S((T, D_MODEL), F32), S((T, D_MODEL), BF), S((T, SSD_WIDTH), BF), S((T, RET_V_W), BF),
                   S((8, D_FF), F32), S((1, D_MODEL), F32), S((1, D_MODEL), F32)],
        scratch_shapes=[pltpu.VMEM((8, D_FF), F32)],
        compiler_params=_cparams("arbitrary"),
    )(dgate, dval, graw, x1, dx2, y, convw, wup, wpf, wpm, wout)


def _in_bwd(dz, dxbc, dq, dk, dv, dg, ddt, x, dx1, w0, wt, wdt, tm=512):
    T = x.shape[0]

    def body(dz_ref, dxbc_ref, dq_ref, dk_ref, dv_ref, dg_ref, ddt_ref, x_ref, dx1_ref, w0_ref, wt_ref, wdt_ref, gx_ref, dw0_ref):
        @pl.when(pl.program_id(0) == 0)
        def _():
            dw0_ref[...] = jnp.zeros_like(dw0_ref)

        dh = _dot(ddt_ref[...], wdt_ref[...])
        for ref, (lo, hi) in zip((dz_ref, dxbc_ref, dq_ref, dk_ref, dv_ref, dg_ref), IN_SEGMENTS):
            dh = dh + _dot(ref[...], wt_ref[lo:hi, :])
        xf = x_ref[...]
        dx, dw = _rms_bwd(dh, xf, _rstd(xf), w0_ref[...])
        dw0_ref[...] += dw
        gx_ref[...] = dx1_ref[...] + dx

    widths = (SSD_WIDTH, SSD_XBC, RET_QK_W, RET_QK_W, RET_V_W, RET_V_W, LANES)
    return pl.pallas_call(
        body, name="in_bwd", grid=(T // tm,),
        in_specs=[_row_spec(tm, w) for w in widths] + [_row_spec(tm, D_MODEL), _row_spec(tm, D_MODEL),
                                                       _const_spec((1, D_MODEL)), _VMEM_WHOLE, _VMEM_WHOLE],
        out_specs=[_row_spec(tm, D_MODEL), _const_spec((1, D_MODEL))],
        out_shape=[S((T, D_MODEL), F32), S((1, D_MODEL), F32)],
        compiler_params=_cparams("arbitrary"),
    )(dz, dxbc, dq, dk, dv, dg, ddt, x, dx1, w0, wt, wdt)


DW_TILE_BYTES = 6 << 20


def _matmul_tn(a, b, name, tk=1024):
    T, M = a.shape
    N = b.shape[1]
    tm_, tn = M, N
    while tm_ * tn * 4 > DW_TILE_BYTES:
        if tm_ >= tn and tm_ % 256 == 0:
            tm_ //= 2
        elif tn % 256 == 0:
            tn //= 2
        else:
            break
    nk = T // tk

    def body(a_ref, b_ref, o_ref):
        @pl.when(pl.program_id(2) == 0)
        def _():
            o_ref[...] = jnp.zeros_like(o_ref)

        o_ref[...] += _dot_tn(a_ref[...], b_ref[...])

    return pl.pallas_call(
        body, name=name, grid=(M // tm_, N // tn, nk),
        in_specs=[pl.BlockSpec((tk, tm_), lambda m, n, k: (k, m)), pl.BlockSpec((tk, tn), lambda m, n, k: (k, n))],
        out_specs=pl.BlockSpec((tm_, tn), lambda m, n, k: (m, n)),
        out_shape=S((M, N), F32),
        compiler_params=_cparams("parallel", "parallel", "arbitrary"),
    )(a, b)


def _matmul_tn_group(as_, b, name, tk=1024):
    T, N = b.shape
    na = len(as_)

    def body(*refs):
        a_refs, b_ref, o_refs = refs[:na], refs[na], refs[na + 1:]

        @pl.when(pl.program_id(0) == 0)
        def _():
            for o_ref in o_refs:
                o_ref[...] = jnp.zeros_like(o_ref)

        bt = b_ref[...]
        for a_ref, o_ref in zip(a_refs, o_refs):
            o_ref[...] += _dot_tn(a_ref[...], bt)

    return pl.pallas_call(
        body, name=name, grid=(T // tk,),
        in_specs=[_row_spec(tk, a.shape[1]) for a in as_] + [_row_spec(tk, N)],
        out_specs=[_const_spec((a.shape[1], N)) for a in as_],
        out_shape=[S((a.shape[1], N), F32) for a in as_],
        compiler_params=_cparams("arbitrary"),
    )(*as_, b)


def _tri(lower):
    r = lax.broadcasted_iota(jnp.int32, (CH, CH), 0)
    c = lax.broadcasted_iota(jnp.int32, (CH, CH), 1)
    return ((c <= r) if lower else (r <= c)).astype(F32)


def _onehot_row(h):
    return (lax.broadcasted_iota(jnp.int32, (1, LANES), 1) == h).astype(F32)


def _onehot_col(h):
    return (lax.broadcasted_iota(jnp.int32, (LANES, 1), 0) == h).astype(F32)


def _ssd_pre(xc_ref, xh_ref, dtr_ref, cw_ref, cb_ref, dtb_ref, alog_ref, first):
    xc = xc_ref[...].astype(F32)
    xh = jnp.where(first, 0.0, xh_ref[...].astype(F32))
    ext = jnp.concatenate([xh, xc], axis=0)
    u = cb_ref[...] + sum(cw_ref[j:j + 1, :] * _rows_before(ext, SSD_CONV - 1 - j, HALO, CH) for j in range(SSD_CONV))
    sg = _sigmoid(u)
    act = u * sg
    dt = _softplus(dtr_ref[...] + dtb_ref[...])
    a = -jnp.exp(alog_ref[...])
    da = dt * a
    cs = _dot_hi(_tri(True), da)
    cst = _dot_tn_hi(da, _tri(False))
    return xc, u, sg, act, dt, a, cs, cst


HPG = SSD_HEADS // SSD_GROUPS
GW = HPG * SSD_HEAD_DIM


def _expand_heads(src, buf):
    for h in range(SSD_HEADS):
        buf[:, h * SSD_HEAD_DIM:(h + 1) * SSD_HEAD_DIM] = jnp.broadcast_to(src[:, h:h + 1], (CH, SSD_HEAD_DIM))


def _ssd_expanded(act, dt, cs, dtx, csx):
    _expand_heads(dt, dtx)
    _expand_heads(cs, csx)
    csv = csx[...]
    last = csv[CH - 1:CH, :]
    e_exp = jnp.exp(csv)
    dec_exp = jnp.exp(last - csv)
    el_exp = jnp.exp(last)
    xs = act[:, 0:SSD_WIDTH]
    xdt = xs * dtx[...]
    return xs, xdt, xdt * dec_exp, e_exp, dec_exp, el_exp


def _decay_mats(h, cs, cst, transposed):
    r = lax.broadcasted_iota(jnp.int32, (CH, CH), 0)
    c = lax.broadcasted_iota(jnp.int32, (CH, CH), 1)
    c_col = cs[:, h:h + 1]
    c_row = cst[h:h + 1, :]
    if transposed:
        return jnp.exp(jnp.where(r <= c, c_row - c_col, -1e30))
    return jnp.exp(jnp.where(r >= c, c_col - c_row, -1e30))


def _ssd_specs(T):
    nc = T // CH
    return nc, [
        _row_spec(CH, SSD_XBC),
        pl.BlockSpec((HALO, SSD_XBC), lambda i: (jnp.maximum(i * (CH // HALO) - 1, 0), 0)),
        _row_spec(CH, LANES),
        _row_spec(CH, SSD_WIDTH),
    ]


def _groups(act):
    bm = [act[:, SSD_WIDTH + g * SSD_STATE:SSD_WIDTH + (g + 1) * SSD_STATE] for g in range(SSD_GROUPS)]
    o = SSD_WIDTH + SSD_GROUPS * SSD_STATE
    cm = [act[:, o + g * SSD_STATE:o + (g + 1) * SSD_STATE] for g in range(SSD_GROUPS)]
    return bm, cm


def _ssd_fwd(xbc, dtr, z, convw, convb, dtb, alog, dskx, nw):
    T = xbc.shape[0]
    nc, specs = _ssd_specs(T)

    def body(xc_ref, xh_ref, dtr_ref, z_ref, cw_ref, cb_ref, dtb_ref, alog_ref, dskx_ref, nw_ref,
             out_ref, y_ref, st_ref, state, ybuf, dtx, csx):
        i = pl.program_id(0)

        @pl.when(i == 0)
        def _():
            state[...] = jnp.zeros_like(state)

        xc, u, sg, act, dt, a, cs, cst = _ssd_pre(xc_ref, xh_ref, dtr_ref, cw_ref, cb_ref, dtb_ref, alog_ref, i == 0)
        xs, xdt, w, e_exp, dec_exp, el_exp = _ssd_expanded(act, dt, cs, dtx, csx)
        bm, cm = _groups(act)
        groups, heads = range(SSD_GROUPS), range(SSD_HEADS)
        gsl = [slice(g * GW, (g + 1) * GW) for g in groups]
        hsl = [slice(h * SSD_HEAD_DIM, (h + 1) * SSD_HEAD_DIM) for h in heads]
        cb = [_dot_nt(cm[g], bm[g]) for g in groups]
        yoff = [_dot(cm[g], state[g]) for g in groups]
        sloc = [_dot_tn(bm[g], w[:, gsl[g]]) for g in groups]
        lm = [_decay_mats(h, cs, cst, False) for h in heads]
        ydiag = [_dot(cb[h // HPG] * lm[h], xdt[:, hsl[h]]) for h in heads]
        for g in groups:
            st_ref[0, g] = state[g]
            ybuf[:, gsl[g]] = yoff[g] * e_exp[:, gsl[g]] + xs[:, gsl[g]] * dskx_ref[:, gsl[g]]
            state[g] = state[g] * el_exp[:, gsl[g]] + sloc[g]
        for h in heads:
            ybuf[:, hsl[h]] += ydiag[h]
        yv = ybuf[...]
        y_ref[...] = yv.astype(BF)
        zf = z_ref[...].astype(F32)
        gated = yv * (zf * _sigmoid(zf))
        out_ref[...] = (gated * _rstd(gated) * nw_ref[...]).astype(BF)

    st_spec = pl.BlockSpec((1, SSD_GROUPS, SSD_STATE, GW), lambda i: (i, 0, 0, 0))
    return pl.pallas_call(
        body, name="ssd_fwd", grid=(nc,),
        in_specs=specs + [_const_spec((8, SSD_XBC)), _const_spec((1, SSD_XBC)), _const_spec((1, LANES)),
                          _const_spec((1, LANES)), _const_spec((1, SSD_WIDTH)), _const_spec((1, SSD_WIDTH))],
        out_specs=[_row_spec(CH, SSD_WIDTH), _row_spec(CH, SSD_WIDTH), st_spec],
        out_shape=[S((T, SSD_WIDTH), BF), S((T, SSD_WIDTH), BF), S((nc, SSD_GROUPS, SSD_STATE, GW), F32)],
        scratch_shapes=[pltpu.VMEM((SSD_GROUPS, SSD_STATE, GW), F32), pltpu.VMEM((CH, SSD_WIDTH), F32),
                        pltpu.VMEM((CH, SSD_WIDTH), F32), pltpu.VMEM((CH, SSD_WIDTH), F32)],
        compiler_params=_cparams("arbitrary"),
    )(xbc, xbc, dtr, z, convw, convb, dtb, alog, dskx, nw)


def _ssd_bwd(dout, y, xbc, dtr, z, states, convw, convb, dtb, alog, dskx, nw):
    T = xbc.shape[0]
    nc = T // CH
    rev = lambda i: (nc - 1 - i, 0)
    rspec = lambda w: pl.BlockSpec((CH, w), rev)
    halo_spec = pl.BlockSpec((HALO, SSD_XBC), lambda i: (jnp.maximum((nc - 1 - i) * (CH // HALO) - 1, 0), 0))
    NB = SSD_WIDTH
    NC_ = SSD_WIDTH + SSD_GROUPS * SSD_STATE

    def body(do_ref, y_ref, xc_ref, xh_ref, dtr_ref, z_ref, st_ref, cw_ref, cb_ref, dtb_ref, alog_ref, dskx_ref, nw_ref,
             dz_ref, dxbc_ref, ddt_ref, dcw_ref, dcb_ref, ddtb_ref, dalog_ref, ddsk_ref, dnw_ref,
             dstate, ducarry, dtx, csx, dxdtbuf, dact):
        i = pl.program_id(0)

        @pl.when(i == 0)
        def _():
            dstate[...] = jnp.zeros_like(dstate)
            ducarry[...] = jnp.zeros_like(ducarry)
            for ref in (dcw_ref, dcb_ref, ddtb_ref, dalog_ref, ddsk_ref, dnw_ref):
                ref[...] = jnp.zeros_like(ref)

        xc, u, sg, act, dt, a, cs, cst = _ssd_pre(xc_ref, xh_ref, dtr_ref, cw_ref, cb_ref, dtb_ref, alog_ref, i == nc - 1)
        xs, xdt, w, e_exp, dec_exp, el_exp = _ssd_expanded(act, dt, cs, dtx, csx)
        bm, cm = _groups(act)
        yv = y_ref[...].astype(F32)
        zf = z_ref[...].astype(F32)
        sz = _sigmoid(zf)
        gated = yv * (zf * sz)
        dgated, dnw = _rms_bwd(do_ref[...].astype(F32), gated, _rstd(gated), nw_ref[...])
        dnw_ref[...] += dnw
        dz_ref[...] = (dgated * yv * _dsilu(zf, sz)).astype(BF)
        dy = dgated * (zf * sz)
        lane_of = lax.broadcasted_iota(jnp.int32, (SSD_WIDTH, LANES), 0) - SSD_HEAD_DIM * lax.broadcasted_iota(jnp.int32, (SSD_WIDTH, LANES), 1)
        expt = ((lane_of >= 0) & (lane_of < SSD_HEAD_DIM)).astype(F32)
        ddsk_ref[...] += _dot_hi(jnp.sum(dy * xs, axis=0, keepdims=True), expt)
        dcs = jnp.zeros((CH, LANES), F32)
        dcst = jnp.zeros((LANES, CH), F32)
        ddt = jnp.zeros((CH, LANES), F32)
        lastrows = []
        for g in range(SSD_GROUPS):
            gs = slice(g * GW, (g + 1) * GW)
            st = st_ref[0, g]
            dsn = dstate[g]
            cbm = _dot_nt(cm[g], bm[g])
            cbt = _dot_nt(bm[g], cm[g])
            dy_g = dy[:, gs]
            yoff = _dot(cm[g], st) * e_exp[:, gs]
            dq = dy_g * e_exp[:, gs]
            dcm_g = _dot_nt(dq, st)
            dstate[g] = _dot_tn(cm[g], dq) + dsn * el_exp[:, gs]
            dw = _dot(bm[g], dsn)
            w_g = w[:, gs]
            dbm_g = _dot_nt(w_g, dsn)
            dww = dw * w_g
            red = dy_g * yoff - dww
            lastrows.append(jnp.sum(dsn * st, axis=0, keepdims=True) * el_exp[:, gs] + jnp.sum(dww, axis=0, keepdims=True))
            dxdtbuf[:, gs] = dw * dec_exp[:, gs]
            dcb = jnp.zeros((CH, CH), F32)
            hs = range(g * HPG, (g + 1) * HPG)
            hsl = {h: slice(h * SSD_HEAD_DIM, (h + 1) * SSD_HEAD_DIM) for h in hs}
            lm = {h: _decay_mats(h, cs, cst, False) for h in hs}
            dm = {h: _dot_nt(dy[:, hsl[h]], xdt[:, hsl[h]]) for h in hs}
            dxd = {h: _dot(cbt * _decay_mats(h, cs, cst, True), dy[:, hsl[h]]) for h in hs}
            for h in hs:
                sl = hsl[h]
                rl = slice((h - g * HPG) * SSD_HEAD_DIM, (h - g * HPG + 1) * SSD_HEAD_DIM)
                oh = _onehot_row(h)
                dxdt_h = dxdtbuf[:, sl] + dxd[h]
                dxdtbuf[:, sl] = dxdt_h
                dseg = dm[h] * (cbm * lm[h])
                dcb = dcb + dm[h] * lm[h]
                col = jnp.sum(dseg, axis=1, keepdims=True) + jnp.sum(red[:, rl], axis=1, keepdims=True)
                dcs = dcs + col * oh
                dcst = dcst - _onehot_col(h) * jnp.sum(dseg, axis=0, keepdims=True)
                ddt = ddt + jnp.sum(dxdt_h * xs[:, sl], axis=1, keepdims=True) * oh
            dact[:, NB + g * SSD_STATE:NB + (g + 1) * SSD_STATE] = dbm_g + _dot_tn(dcb, cm[g])
            dact[:, NC_ + g * SSD_STATE:NC_ + (g + 1) * SSD_STATE] = dcm_g + _dot(dcb, bm[g])
        dact[:, 0:SSD_WIDTH] = dy * dskx_ref[...] + dxdtbuf[...] * dtx[...]
        dlast = _dot_hi(jnp.concatenate(lastrows, axis=1), expt)
        rows = lax.broadcasted_iota(jnp.int32, (CH, LANES), 0)
        dcs = dcs + _dot_tn_hi(dcst, jnp.eye(LANES, dtype=F32)) + jnp.where(rows == CH - 1, dlast, 0.0)
        dda = _dot_hi(_tri(False), dcs)
        dalog_ref[...] += jnp.sum(dda * dt, axis=0, keepdims=True) * a
        ddt = ddt + dda * a
        ddtr = ddt * _sigmoid(dtr_ref[...] + dtb_ref[...])
        ddtb_ref[...] += jnp.sum(ddtr, axis=0, keepdims=True)
        ddt_ref[...] = ddtr.astype(BF)
        du = dact[...] * _dsilu(u, sg)
        dcb_ref[...] += jnp.sum(du, axis=0, keepdims=True)
        ext = jnp.concatenate([du, ducarry[...]], axis=0)
        ducarry[...] = du[0:8]
        dx = jnp.zeros((CH, SSD_XBC), F32)
        for j in range(SSD_CONV):
            sj = _rows_after(ext, SSD_CONV - 1 - j, CH)
            dx = dx + cw_ref[j:j + 1, :] * sj
            dcw_ref[j:j + 1, :] += jnp.sum(sj * xc, axis=0, keepdims=True)
        dxbc_ref[...] = dx.astype(BF)

    return pl.pallas_call(
        body, name="ssd_bwd", grid=(nc,),
        in_specs=[rspec(SSD_WIDTH), rspec(SSD_WIDTH), rspec(SSD_XBC), halo_spec, rspec(LANES), rspec(SSD_WIDTH),
                  pl.BlockSpec((1, SSD_GROUPS, SSD_STATE, GW), lambda i: (nc - 1 - i, 0, 0, 0)),
                  _const_spec((8, SSD_XBC)), _const_spec((1, SSD_XBC)), _const_spec((1, LANES)),
                  _const_spec((1, LANES)), _const_spec((1, SSD_WIDTH)), _const_spec((1, SSD_WIDTH))],
        out_specs=[rspec(SSD_WIDTH), rspec(SSD_XBC), rspec(LANES),
                   _const_spec((8, SSD_XBC)), _const_spec((1, SSD_XBC)), _const_spec((1, LANES)),
                   _const_spec((1, LANES)), _const_spec((1, LANES)), _const_spec((1, SSD_WIDTH))],
        out_shape=[S((T, SSD_WIDTH), BF), S((T, SSD_XBC), BF), S((T, LANES), BF),
                   S((8, SSD_XBC), F32), S((1, SSD_XBC), F32), S((1, LANES), F32),
                   S((1, LANES), F32), S((1, LANES), F32), S((1, SSD_WIDTH), F32)],
        scratch_shapes=[pltpu.VMEM((SSD_GROUPS, SSD_STATE, GW), F32), pltpu.VMEM((8, SSD_XBC), F32),
                        pltpu.VMEM((CH, SSD_WIDTH), F32), pltpu.VMEM((CH, SSD_WIDTH), F32),
                        pltpu.VMEM((CH, SSD_WIDTH), F32), pltpu.VMEM((CH, SSD_XBC), F32)],
        compiler_params=_cparams("arbitrary"),
    )(dout, y, xbc, xbc, dtr, z, states, convw, convb, dtb, alog, dskx, nw)


def _log_gamma(h):
    return float(np.log1p(-np.exp2(np.float32(-5.0 - h)), dtype=np.float32))


def _swap_halves(t):
    n = t.shape[1]
    lane = lax.broadcasted_iota(jnp.int32, t.shape, 1)
    return jnp.where((lane & (RET_QK - 1)) < RET_QK // 2, pltpu.roll(t, n - RET_QK // 2, 1), pltpu.roll(t, RET_QK // 2, 1))


def _rot(t, cos, sin):
    return t * cos + _swap_halves(t) * sin


def _rot_t(d, cos, sin):
    return d * cos + _swap_halves(d * sin)


def _ret_tables():
    lg = jnp.asarray([_log_gamma(h) for h in range(RET_HEADS)], F32)[:, None, None]
    pos = jnp.arange(CH, dtype=F32)
    rel = pos[:, None] - pos[None, :]
    dmask = jnp.where(rel >= 0, jnp.exp(lg * jnp.maximum(rel, 0.0)), 0.0)
    kdec = jnp.exp(lg * (CH - 1.0 - pos)[None, :, None])
    qdec = jnp.exp(lg * (pos + 1.0)[None, :, None])
    rows = jnp.concatenate([jnp.swapaxes(kdec, 1, 2), jnp.swapaxes(qdec, 1, 2), jnp.zeros((RET_HEADS, CH - 2, CH), F32)], axis=1)
    full = lambda t: jnp.broadcast_to(t, (RET_HEADS, CH, CH))
    return jnp.stack([dmask, jnp.swapaxes(dmask, 1, 2), full(kdec), full(qdec), full(rows)], axis=1)


def _ret_consts(h, rt_ref):
    kdec = rt_ref[h, 2][:, 0:RET_QK]
    qdec = rt_ref[h, 3][:, 0:RET_QK]
    return rt_ref[h, 0], rt_ref[h, 1], kdec, qdec, rt_ref[h, 4, 0:1, :], rt_ref[h, 4, 1:2, :], math.exp(_log_gamma(h) * CH)


_RT_SPEC = pl.BlockSpec((RET_HEADS, 5, CH, CH), lambda i: (0, 0, 0, 0))


def _ret_fwd(q, k, v, g, cos, sin, nw, rt):
    T = q.shape[0]
    nc = T // CH

    def body(q_ref, k_ref, v_ref, g_ref, cos_ref, sin_ref, nw_ref, rt_ref, out_ref, st_ref, state):
        i = pl.program_id(0)

        @pl.when(i == 0)
        def _():
            state[...] = jnp.zeros_like(state)

        cosf = jnp.tile(cos_ref[...], (1, RET_QK_W // LANES))
        sinf = jnp.tile(sin_ref[...], (1, RET_QK_W // LANES))
        qr = _rot(q_ref[...].astype(F32), cosf, sinf)
        kr = _rot(k_ref[...].astype(F32), cosf, sinf) * (RET_QK ** -0.5)
        krt = kr.T
        st_ref[0] = state[...]
        qsl = [slice(h * RET_QK, (h + 1) * RET_QK) for h in range(RET_HEADS)]
        vsl = [slice(h * RET_V, (h + 1) * RET_V) for h in range(RET_HEADS)]
        consts = [_ret_consts(h, rt_ref) for h in range(RET_HEADS)]
        scores = [_dot_nt(qr[:, qsl[h]], kr[:, qsl[h]]) * consts[h][0] for h in range(RET_HEADS)]
        cross = [_dot(qr[:, qsl[h]] * consts[h][3], state[h]) for h in range(RET_HEADS)]
        kv = [_dot(krt[qsl[h], :] * consts[h][4], v_ref[:, vsl[h]]) for h in range(RET_HEADS)]
        o_all = [_dot(scores[h], v_ref[:, vsl[h]]) + cross[h] for h in range(RET_HEADS)]
        for h in range(RET_HEADS):
            state[h] = state[h] * consts[h][6] + kv[h]
        for h in range(RET_HEADS):
            sl = slice(h * RET_V, (h + 1) * RET_V)
            o = o_all[h]
            gf = g_ref[:, sl].astype(F32)
            out_ref[:, sl] = (o * _rstd(o) * nw_ref[:, sl] * (gf * _sigmoid(gf))).astype(BF)

    return pl.pallas_call(
        body, name="ret_fwd", grid=(nc,),
        in_specs=[_row_spec(CH, RET_QK_W), _row_spec(CH, RET_QK_W), _row_spec(CH, RET_V_W), _row_spec(CH, RET_V_W),
                  _row_spec(CH, LANES), _row_spec(CH, LANES), _const_spec((1, RET_V_W)), _RT_SPEC],
        out_specs=[_row_spec(CH, RET_V_W), pl.BlockSpec((1, RET_HEADS, RET_QK, RET_V), lambda i: (i, 0, 0, 0))],
        out_shape=[S((T, RET_V_W), BF), S((nc, RET_HEADS, RET_QK, RET_V), F32)],
        scratch_shapes=[pltpu.VMEM((RET_HEADS, RET_QK, RET_V), F32)],
        compiler_params=_cparams("arbitrary"),
    )(q, k, v, g, cos, sin, nw, rt)


def _ret_bwd(dout, q, k, v, g, states, cos, sin, nw, rt):
    T = q.shape[0]
    nc = T // CH
    rev = lambda i: (nc - 1 - i, 0)
    rspec = lambda w: pl.BlockSpec((CH, w), rev)

    def body(do_ref, q_ref, k_ref, v_ref, g_ref, st_ref, cos_ref, sin_ref, nw_ref, rt_ref,
             dq_ref, dk_ref, dv_ref, dg_ref, dnw_ref, dstate, dqbuf, dkbuf):
        i = pl.program_id(0)

        @pl.when(i == 0)
        def _():
            dstate[...] = jnp.zeros_like(dstate)
            dnw_ref[...] = jnp.zeros_like(dnw_ref)

        cosf = jnp.tile(cos_ref[...], (1, RET_QK_W // LANES))
        sinf = jnp.tile(sin_ref[...], (1, RET_QK_W // LANES))
        qr = _rot(q_ref[...].astype(F32), cosf, sinf)
        kr = _rot(k_ref[...].astype(F32), cosf, sinf) * (RET_QK ** -0.5)
        qrt = qr.T
        heads = range(RET_HEADS)
        qsl = [slice(h * RET_QK, (h + 1) * RET_QK) for h in heads]
        vsl = [slice(h * RET_V, (h + 1) * RET_V) for h in heads]
        do_all = []
        consts = [_ret_consts(h, rt_ref) for h in heads]
        scores = [_dot_nt(qr[:, qsl[h]], kr[:, qsl[h]]) * consts[h][0] for h in heads]
        scores_t = [_dot_nt(kr[:, qsl[h]], qr[:, qsl[h]]) * consts[h][1] for h in heads]
        cross = [_dot(qr[:, qsl[h]] * consts[h][3], st_ref[0, h]) for h in heads]
        o_all = [_dot(scores[h], v_ref[:, vsl[h]]) + cross[h] for h in heads]
        for h in heads:
            o = o_all[h]
            rr = _rstd(o)
            of = o * rr
            gf = g_ref[:, vsl[h]].astype(F32)
            sgg = _sigmoid(gf)
            d_h = do_ref[:, vsl[h]].astype(F32)
            nw_h = nw_ref[:, vsl[h]]
            dg_ref[:, vsl[h]] = (d_h * of * nw_h * _dsilu(gf, sgg)).astype(BF)
            dt_ = d_h * (gf * sgg)
            dnw_ref[:, vsl[h]] += jnp.sum(dt_ * of, axis=0, keepdims=True)
            dof = dt_ * nw_h
            do_all.append(rr * dof - o * (rr * rr * rr) * jnp.mean(dof * o, axis=-1, keepdims=True))
        dsc = [_dot_nt(do_all[h], v_ref[:, vsl[h]]) * consts[h][0] for h in heads]
        dsc_t = [_dot_nt(v_ref[:, vsl[h]], do_all[h]) * consts[h][1] for h in heads]
        dv_a = [_dot(scores_t[h], do_all[h]) for h in heads]
        dv_b = [_dot(kr[:, qsl[h]] * consts[h][2], dstate[h]) for h in heads]
        dq_a = [_dot(dsc[h], kr[:, qsl[h]]) for h in heads]
        dq_b = [_dot_nt(do_all[h], st_ref[0, h]) * consts[h][3] for h in heads]
        dk_a = [_dot(dsc_t[h], qr[:, qsl[h]]) for h in heads]
        dk_b = [_dot_nt(v_ref[:, vsl[h]], dstate[h]) * consts[h][2] for h in heads]
        dst = [_dot(qrt[qsl[h], :] * consts[h][5], do_all[h]) for h in heads]
        for h in heads:
            dv_ref[:, vsl[h]] = (dv_a[h] + dv_b[h]).astype(BF)
            dqbuf[:, qsl[h]] = dq_a[h] + dq_b[h]
            dkbuf[:, qsl[h]] = dk_a[h] + dk_b[h]
            dstate[h] = dstate[h] * consts[h][6] + dst[h]
        dq_ref[...] = _rot_t(dqbuf[...], cosf, sinf).astype(BF)
        dk_ref[...] = (_rot_t(dkbuf[...], cosf, sinf) * (RET_QK ** -0.5)).astype(BF)

    return pl.pallas_call(
        body, name="ret_bwd", grid=(nc,),
        in_specs=[rspec(RET_V_W), rspec(RET_QK_W), rspec(RET_QK_W), rspec(RET_V_W), rspec(RET_V_W),
                  pl.BlockSpec((1, RET_HEADS, RET_QK, RET_V), lambda i: (nc - 1 - i, 0, 0, 0)),
                  rspec(LANES), rspec(LANES), _const_spec((1, RET_V_W)), _RT_SPEC],
        out_specs=[rspec(RET_QK_W), rspec(RET_QK_W), rspec(RET_V_W), rspec(RET_V_W), _const_spec((1, RET_V_W))],
        out_shape=[S((T, RET_QK_W), BF), S((T, RET_QK_W), BF), S((T, RET_V_W), BF), S((T, RET_V_W), BF),
                   S((1, RET_V_W), F32)],
        scratch_shapes=[pltpu.VMEM((RET_HEADS, RET_QK, RET_V), F32), pltpu.VMEM((CH, RET_QK_W), F32),
                        pltpu.VMEM((CH, RET_QK_W), F32)],
        compiler_params=_cparams("arbitrary"),
    )(dout, q, k, v, g, states, cos, sin, nw, rt)


_HBM = pl.BlockSpec(memory_space=pltpu.HBM)
_SEM = pl.BlockSpec(memory_space=pltpu.SEMAPHORE)
_EFFECT = pltpu.SideEffectType.DATAFLOW_SIDE_EFFECTING


ALL_PEERS = tuple(range(1, N_DEV))
ONE_PER_CHIP = (1, 2, 4, 6)
OTHER_CHIPS = (2, 4, 6)


def _split_copies(buf_refs, land_refs, send_sems, recv_sems, same, to_me, ks):
    x, y, c = lax.axis_index("x"), lax.axis_index("y"), lax.axis_index("c")
    me = 4 * x + 2 * y + c
    cps = []
    for ki, k in enumerate(ks):
        px = 1 - x if k & 4 else x
        py = 1 - y if k & 2 else y
        pc = 1 - c if k & 1 else c
        p = 4 * px + 2 * py + pc
        for b in range(len(buf_refs)):
            s = b * len(ks) + ki
            cps.append(pltpu.make_async_remote_copy(
                src_ref=buf_refs[b] if same else buf_refs[b].at[p], dst_ref=land_refs[b].at[me if to_me else p],
                send_sem=send_sems.at[s], recv_sem=recv_sems.at[s], device_id=(px, py, pc), device_id_type=pl.DeviceIdType.MESH))
    return cps


def _exchange_start(bufs, name, same, ks=ALL_PEERS):
    nb = len(bufs)
    ns = nb * len(ks)
    lands = [lax.empty((N_DEV,) + tuple(b.shape if same else b.shape[1:]), b.dtype) for b in bufs]

    def body(*refs):
        buf_refs, land_refs = refs[:nb], refs[nb:2 * nb]
        send_sems, recv_sems = refs[2 * nb], refs[2 * nb + 1]
        token = refs[-1]
        for cp in _split_copies(buf_refs, land_refs, send_sems, recv_sems, same, True, ks):
            cp.start()
        token[...] = jnp.zeros_like(token)

    hbm = lambda a: pltpu.with_memory_space_constraint(a, pltpu.HBM)
    out = pl.pallas_call(
        body, name=name,
        out_shape=(pltpu.SemaphoreType.DMA((ns,)), pltpu.SemaphoreType.DMA((ns,)),
                   *[pltpu.HBM(a.shape, a.dtype) for a in list(bufs) + lands], S((8, LANES), F32)),
        in_specs=[_HBM] * (2 * nb), out_specs=(_SEM, _SEM, *[_HBM] * (2 * nb), pl.BlockSpec(memory_space=pltpu.VMEM)),
        input_output_aliases={i: 2 + i for i in range(2 * nb)},
        compiler_params=pltpu.CompilerParams(has_side_effects=_EFFECT),
    )(*[hbm(a) for a in list(bufs) + lands])
    return out[0], out[1], list(out[2:2 + nb]), list(out[2 + nb:2 + 2 * nb]), out[-1]


def _exchange_wait(started, after, name, same, ks=ALL_PEERS):
    send_sems, recv_sems, bufs, lands, _ = started
    nb = len(bufs)
    after = list(after) if isinstance(after, (list, tuple)) else [after]

    def body(*refs):
        buf_refs, land_refs = refs[:nb], refs[nb:2 * nb]
        s_sems, r_sems = refs[2 * nb], refs[2 * nb + 1]
        for cp in _split_copies(buf_refs, land_refs, s_sems, r_sems, same, False, ks):
            cp.wait_send()
            cp.wait_recv()

    out = pl.pallas_call(
        body, name=name,
        out_shape=tuple(pltpu.HBM(a.shape, a.dtype) for a in bufs + lands),
        in_specs=[_HBM] * (2 * nb) + [_SEM, _SEM] + [pl.BlockSpec(memory_space=pl.ANY)] * len(after),
        out_specs=tuple([_HBM] * (2 * nb)),
        input_output_aliases={i: i for i in range(2 * nb)},
        compiler_params=pltpu.CompilerParams(has_side_effects=_EFFECT),
    )(*bufs, *lands, send_sems, recv_sems, *after)
    return list(out[:nb]), list(out[nb:])


def _forward_copies(land_ref, send_sems, recv_sems, sending):
    x, y, c = lax.axis_index("x"), lax.axis_index("y"), lax.axis_index("c")
    cps = []
    for ki, k in enumerate(OTHER_CHIPS):
        px = 1 - x if k & 4 else x
        py = 1 - y if k & 2 else y
        q = 4 * px + 2 * py + (c if sending else 1 - c)
        cps.append(pltpu.make_async_remote_copy(
            src_ref=land_ref.at[q], dst_ref=land_ref.at[q], send_sem=send_sems.at[ki], recv_sem=recv_sems.at[ki],
            device_id=(x, y, 1 - c), device_id_type=pl.DeviceIdType.MESH))
    return cps


def _forward_start(land, name):
    def body(land_ref, send_sems, recv_sems, land_thru, token):
        for cp in _forward_copies(land_ref, send_sems, recv_sems, True):
            cp.start()
        token[...] = jnp.zeros_like(token)

    n = len(OTHER_CHIPS)
    out = pl.pallas_call(
        body, name=name,
        out_shape=(pltpu.SemaphoreType.DMA((n,)), pltpu.SemaphoreType.DMA((n,)), pltpu.HBM(land.shape, land.dtype),
                   S((8, LANES), F32)),
        in_specs=[_HBM], out_specs=(_SEM, _SEM, _HBM, pl.BlockSpec(memory_space=pltpu.VMEM)),
        input_output_aliases={0: 2},
        compiler_params=pltpu.CompilerParams(has_side_effects=_EFFECT),
    )(pltpu.with_memory_space_constraint(land, pltpu.HBM))
    return out


def _forward_wait(started, after, name):
    send_sems, recv_sems, land, _ = started
    after = list(after) if isinstance(after, (list, tuple)) else [after]

    def body(land_ref, s_sems, r_sems, *rest):
        for cp in _forward_copies(land_ref, s_sems, r_sems, False):
            cp.wait_send()
            cp.wait_recv()

    return pl.pallas_call(
        body, name=name, out_shape=pltpu.HBM(land.shape, land.dtype),
        in_specs=[_HBM, _SEM, _SEM] + [pl.BlockSpec(memory_space=pl.ANY)] * len(after), out_specs=_HBM,
        input_output_aliases={0: 0},
        compiler_params=pltpu.CompilerParams(has_side_effects=_EFFECT),
    )(land, send_sems, recv_sems, *after)


def _sum_slabs(recv, name):
    n, R, _ = recv.shape

    def body(r_ref, o_ref):
        g = r_ref[0].astype(F32)
        for s in range(1, n):
            g = g + r_ref[s].astype(F32)
        o_ref[...] = g

    return pl.pallas_call(body, name=name, out_shape=S((R, LANES), F32))(recv)


def _adamw(recv, w, m, v, name, tr, tc=None):
    n, R, C = recv.shape
    c1 = 1.0 - ADAM_B1 ** ADAM_STEP
    c2 = 1.0 - ADAM_B2 ** ADAM_STEP

    def body(r_ref, w_ref, m_ref, v_ref, g_out, d_out, m_out, v_out):
        g = r_ref[0].astype(F32)
        for s in range(1, n):
            g = g + r_ref[s].astype(F32)
        mm = ADAM_B1 * m_ref[...] + (1.0 - ADAM_B1) * g
        vv = ADAM_B2 * v_ref[...] + (1.0 - ADAM_B2) * (g * g)
        g_out[...] = g
        m_out[...] = mm
        v_out[...] = vv
        d_out[...] = -ADAM_LR * ((mm / c1) / (jnp.sqrt(vv / c2) + ADAM_EPS) + ADAM_WD * w_ref[...])

    tc = C if tc is None else tc
    spec = pl.BlockSpec((tr, tc), lambda i, j: (i, j))
    return pl.pallas_call(
        body, name=name, grid=(R // tr, C // tc),
        in_specs=[pl.BlockSpec((n, tr, tc), lambda i, j: (0, i, j)), spec, spec, spec],
        out_specs=[spec] * 4, out_shape=[S((R, C), F32)] * 4,
        compiler_params=_cparams("parallel", "parallel"),
    )(recv, w, m, v)


def _pack(parts, rows):
    cols = []
    for p in parts:
        f = p.reshape(-1)
        cols.append(jnp.pad(f, (0, (-f.shape[0]) % LANES)))
    flat = jnp.concatenate(cols)
    return jnp.pad(flat, (0, rows * LANES - flat.shape[0])).reshape(rows, LANES)


def _unpack(buf, shapes):
    flat = buf.reshape(-1)
    out, o = [], 0
    for shp in shapes:
        n = int(np.prod(shp))
        out.append(flat[o:o + n].reshape(shp))
        o += n + (-n) % LANES
    return out


SMALL_ROWS = 200
CONV_ROWS = 16


def kernel(x, pre_mix_norm_w, w_in, ssd_conv_w, ssd_conv_b, ssd_dt_bias, ssd_a_log, ssd_d, ssd_norm_w, ret_norm_w, w_out, post_mix_norm_w, pre_ffn_norm_w, w_up, ffn_conv_w, ffn_conv_b, w_down, post_ffn_norm_w, loss_target, m_pre_mix_norm_w, m_w_in, m_ssd_conv_w, m_ssd_conv_b, m_ssd_dt_bias, m_ssd_a_log, m_ssd_d, m_ssd_norm_w, m_ret_norm_w, m_w_out, m_post_mix_norm_w, m_pre_ffn_norm_w, m_w_up, m_ffn_conv_w, m_ffn_conv_b, m_w_down, m_post_ffn_norm_w, v_pre_mix_norm_w, v_w_in, v_ssd_conv_w, v_ssd_conv_b, v_ssd_dt_bias, v_ssd_a_log, v_ssd_d, v_ssd_norm_w, v_ret_norm_w, v_w_out, v_post_mix_norm_w, v_pre_ffn_norm_w, v_w_up, v_ffn_conv_w, v_ffn_conv_b, v_w_down, v_post_ffn_norm_w):
    T = x.shape[1]
    xi, tgt = x[0], loss_target[0]
    me = 4 * lax.axis_index("x") + 2 * lax.axis_index("y") + lax.axis_index("c")
    n_in, n_up = w_in.shape[2], w_up.shape[2]
    n_out, n_down = w_out.shape[1], w_down.shape[1]
    n_sc, n_fc = ssd_conv_w.shape[2], ffn_conv_w.shape[2]

    def after(token, value):
        return value * (1.0 + token[0, 0])

    def finish(started, after_value, name, same):
        bufs, lands = _exchange_wait(started, after_value, name, same)
        own = [b if same else lax.dynamic_index_in_dim(b, me, 0, keepdims=False) for b in bufs]
        return [lax.dynamic_update_index_in_dim(l, o, me, 0) for l, o in zip(lands, own)]

    tr_ = lambda w: jnp.transpose(w[0])
    gat_in = _exchange_start([tr_(w_in).astype(BF)], "gather_in_start", True, ONE_PER_CHIP)
    gat_conv = _exchange_start([after(gat_in[4], _pack([ssd_conv_w, ffn_conv_w], CONV_ROWS))], "gather_conv_start", True)
    pad_h = lambda p: jnp.pad(p, ((0, 0), (0, LANES - SSD_HEADS)))
    dtb, alog = pad_h(ssd_dt_bias), pad_h(ssd_a_log)
    dskx = jnp.repeat(ssd_d, SSD_HEAD_DIM, axis=1)
    inv = ROPE_BASE ** (-jnp.arange(0, RET_QK, 2, dtype=F32) / RET_QK)
    ang = after(gat_conv[4], jnp.arange(T, dtype=F32)[:, None]) * inv[None, :]
    cs_, sn_ = jnp.cos(ang), jnp.sin(ang)
    cos = jnp.concatenate([cs_, cs_, cs_, cs_], axis=1)
    sin = jnp.concatenate([-sn_, sn_, -sn_, sn_], axis=1)
    rtab = _ret_tables()
    shard_in, land_in = _exchange_wait(gat_in, [cos, sin, rtab], "gather_in_wait", True, ONE_PER_CHIP)
    fwd_in_ = _forward_start(land_in[0], "gather_in_forward")
    gat_rest = _exchange_start([after(fwd_in_[3], w).astype(BF) for w in (w_out[0], tr_(w_up), w_down[0])], "gather_rest_start", True)
    g_in = lax.dynamic_update_index_in_dim(_forward_wait(fwd_in_, gat_rest[4], "gather_in_forward_wait"), shard_in[0], me, 0)
    wt = g_in.reshape(N_DEV * n_in, D_MODEL)
    wdt = jnp.pad(wt[O_DT:O_Q], ((0, LANES - SSD_HEADS), (0, 0)))

    h, z, xbc, q, k, v, g, dtr = _fwd_in(xi, pre_mix_norm_w, wt, wdt)
    yr, rst = _ret_fwd(q, k, v, g, cos, sin, ret_norm_w, rtab)
    gconv, = finish(gat_conv, yr, "gather_conv_wait", True)
    convs = [_unpack(gconv[d], [(SSD_CONV, n_sc), (FFN_CONV, n_fc)]) for d in range(N_DEV)]
    scw = jnp.pad(jnp.concatenate([c[0] for c in convs], axis=1), ((0, 8 - SSD_CONV), (0, 0)))
    fcw = jnp.pad(jnp.concatenate([c[1] for c in convs], axis=1), ((0, 8 - FFN_CONV), (0, 0)))
    ys, ypre, sst = _ssd_fwd(xbc, dtr, z, scw, ssd_conv_b, dtb, alog, dskx, ssd_norm_w)
    g_out, g_up, g_down = finish(gat_rest, [yr, ys], "gather_rest_wait", True)
    wout = g_out.reshape(N_DEV * n_out, D_MODEL)
    wup = g_up.reshape(N_DEV * n_up, D_MODEL)
    wdown = g_down.reshape(N_DEV * n_down, D_MODEL)
    y, x1, h2, graw, val = _fwd_mid(ys, yr, xi, wout, post_mix_norm_w, pre_ffn_norm_w, wup)
    a, dfb, dval, dgate, dx2, lossb, d_pff, d_fcb = _ffn_tail(graw, val, x1, tgt, fcw, ffn_conv_b, wdown, post_ffn_norm_w)
    gdown = _matmul_tn(a, dfb, "dw_down")
    sc_down = _exchange_start([gdown.reshape(N_DEV, n_down, D_MODEL).astype(BF)], "scatter_down_start", False)
    dgraw, dx1, dyb, dys, dyr, d_fcw, d_pf, d_pm = _ffn_bwd(dgate, dval, graw, x1, dx2, y, after(sc_down[4], fcw), wup,
                                                         pre_ffn_norm_w, post_mix_norm_w, wout)
    gup = jnp.concatenate([_matmul_tn(dgraw, h2, "dw_up_g"), _matmul_tn(dval, h2, "dw_up_v")], axis=0)
    gout = jnp.concatenate(_matmul_tn_group([ys, yr], dyb, "dw_out"), axis=0)
    sc_mid = _exchange_start([gup.reshape(N_DEV, n_up, D_MODEL).astype(BF), gout.reshape(N_DEV, n_out, D_MODEL).astype(BF)],
                             "scatter_mid_start", False)
    dz, dxbc, ddt, d_scw, d_scb, d_dtb, d_alog, d_dsk, d_snw = _ssd_bwd(dys, ypre, xbc, dtr, z, sst, after(sc_mid[4], scw),
                                                                      ssd_conv_b, dtb, alog, dskx, ssd_norm_w)
    dq, dk, dv, dg, d_rnw = _ret_bwd(dyr, q, k, v, g, rst, cos, sin, ret_norm_w, rtab)
    g_q, g_k, g_v, g_g = _matmul_tn_group([dq, dk, dv, dg], h, "dw_ret")
    g_z, g_xbc, g_dt = _matmul_tn_group([dz, dxbc, ddt], h, "dw_ssd")
    gin = jnp.concatenate([g_z, g_xbc, g_dt[:SSD_HEADS], g_q, g_k, g_v, g_g], axis=0)
    sc_in = _exchange_start([gin.reshape(N_DEV, n_in, D_MODEL).astype(BF)], "scatter_in_start", False)
    gx, d_w0 = _in_bwd(dz, dxbc, dq, dk, dv, dg, ddt, xi, dx1, after(sc_in[4], pre_mix_norm_w), wt, wdt)
    small_full = [d_w0, d_scw[:SSD_CONV], d_scb, d_dtb[:, :SSD_HEADS], d_alog[:, :SSD_HEADS], d_dsk[:, :SSD_HEADS], d_snw, d_rnw,
                  d_pm, d_pf, d_fcw[:FFN_CONV], d_fcb, d_pff, lossb[0:1, 0:1]]
    gat_small = _exchange_start([_pack(small_full, SMALL_ROWS)], "gather_small_start", True)
    r_down, = finish(sc_down, [gx, gat_small[4]], "scatter_down_wait", False)
    r_up, r_out = finish(sc_mid, r_down, "scatter_mid_wait", False)
    per_w = [None] * 4
    per_w[3] = _adamw(r_down, w_down[0], m_w_down[0], v_w_down[0], "adamw_down", n_down)
    per_w[2] = [jnp.transpose(t) for t in _adamw(r_up, tr_(w_up), tr_(m_w_up), tr_(v_w_up), "adamw_up", n_up, 256)]
    per_w[1] = _adamw(r_out, w_out[0], m_w_out[0], v_w_out[0], "adamw_out", n_out)
    r_in, = finish(sc_in, per_w[1][0], "scatter_in_wait", False)
    per_w[0] = [jnp.transpose(t) for t in _adamw(r_in, tr_(w_in), tr_(m_w_in), tr_(v_w_in), "adamw_in", n_in, 256)]
    big = [[per_w[i][kind][None] for i in range(4)] for kind in range(4)]

    full_shapes = [t.shape for t in small_full]
    gs = _sum_slabs(finish(gat_small, per_w[0][0], "gather_small_wait", True)[0], "sum_small")
    gfull = _unpack(gs, full_shapes)
    gfull[1] = lax.dynamic_slice_in_dim(gfull[1], me * n_sc, n_sc, axis=1)
    gfull[10] = lax.dynamic_slice_in_dim(gfull[10], me * n_fc, n_fc, axis=1)
    ws = [pre_mix_norm_w, ssd_conv_w, ssd_conv_b, ssd_dt_bias, ssd_a_log, ssd_d, ssd_norm_w, ret_norm_w, post_mix_norm_w,
          pre_ffn_norm_w, ffn_conv_w, ffn_conv_b, post_ffn_norm_w]
    ms = [m_pre_mix_norm_w, m_ssd_conv_w, m_ssd_conv_b, m_ssd_dt_bias, m_ssd_a_log, m_ssd_d, m_ssd_norm_w, m_ret_norm_w,
          m_post_mix_norm_w, m_pre_ffn_norm_w, m_ffn_conv_w, m_ffn_conv_b, m_post_ffn_norm_w]
    vs = [v_pre_mix_norm_w, v_ssd_conv_w, v_ssd_conv_b, v_ssd_dt_bias, v_ssd_a_log, v_ssd_d, v_ssd_norm_w, v_ret_norm_w,
          v_post_mix_norm_w, v_pre_ffn_norm_w, v_ffn_conv_w, v_ffn_conv_b, v_post_ffn_norm_w]
    out_shapes = [t.shape for t in ws]
    loss = gfull.pop()[0, 0]
    small = _adamw(_pack(gfull, SMALL_ROWS)[None], _pack(ws, SMALL_ROWS), _pack(ms, SMALL_ROWS), _pack(vs, SMALL_ROWS),
                   "adamw_small", SMALL_ROWS)
    small = [_unpack(b, out_shapes) for b in small]

    order = {"pre_mix_norm_w": ("s", 0), "w_in": ("b", 0), "ssd_conv_w": ("s", 1), "ssd_conv_b": ("s", 2),
             "ssd_dt_bias": ("s", 3), "ssd_a_log": ("s", 4), "ssd_d": ("s", 5), "ssd_norm_w": ("s", 6), "ret_norm_w": ("s", 7),
             "w_out": ("b", 1), "post_mix_norm_w": ("s", 8), "pre_ffn_norm_w": ("s", 9), "w_up": ("b", 2),
             "ffn_conv_w": ("s", 10), "ffn_conv_b": ("s", 11), "w_down": ("b", 3), "post_ffn_norm_w": ("s", 12)}
    outs = [loss, gx[None]]
    for kind in range(4):
        for name, (grp, idx) in order.items():
            outs.append(big[kind][idx] if grp == "b" else small[kind][idx])
    return tuple(outs)
```

```python
import functools
import math

import numpy as np
import jax
import jax.numpy as jnp
from jax import lax
from jax.experimental import pallas as pl
from jax.experimental.pallas import tpu as pltpu

F32 = jnp.float32
BF = jnp.bfloat16
HI = lax.Precision.HIGHEST
S = jax.ShapeDtypeStruct

D_MODEL = 1024
SSD_HEADS = 16
SSD_HEAD_DIM = 64
SSD_GROUPS = 2
SSD_STATE = 128
SSD_WIDTH = 1024
SSD_XBC = 1536
SSD_CONV = 4
RET_HEADS = 8
RET_QK = 64
RET_V = 128
RET_QK_W = 512
RET_V_W = 1024
ROPE_BASE = 10000.0
CH = 128
D_FF = 2816
FFN_CONV = 3
EPS = 1e-6
IN_WIDTH = 5648
N_DEV = 8

ADAM_LR = 0.001
ADAM_B1 = 0.9
ADAM_B2 = 0.999
ADAM_EPS = 1e-08
ADAM_WD = 0.01
ADAM_STEP = 10

LANES = 128
HALO = 16
VMEM_LIMIT = 48 * 1024 * 1024

O_Z, O_XBC, O_DT, O_Q, O_K, O_V, O_G, O_END = 0, 1024, 2560, 2576, 3088, 3600, 4624, 5648
IN_SEGMENTS = ((O_Z, O_XBC), (O_XBC, O_DT), (O_Q, O_K), (O_K, O_V), (O_V, O_G), (O_G, O_END))


def _cparams(*sem):
    return pltpu.CompilerParams(dimension_semantics=sem, vmem_limit_bytes=VMEM_LIMIT)


def _dot(a, b):
    return jnp.dot(a.astype(BF), b.astype(BF), preferred_element_type=F32)


def _dot_nt(a, b):
    return lax.dot_general(a.astype(BF), b.astype(BF), (((1,), (1,)), ((), ())), preferred_element_type=F32)


def _dot_tn(a, b):
    return lax.dot_general(a.astype(BF), b.astype(BF), (((0,), (0,)), ((), ())), preferred_element_type=F32)


def _dot_hi(a, b):
    return jnp.dot(a, b, preferred_element_type=F32, precision=HI)


def _dot_tn_hi(a, b):
    return lax.dot_general(a, b, (((0,), (0,)), ((), ())), preferred_element_type=F32, precision=HI)


def _sigmoid(x):
    return jax.nn.sigmoid(x)


def _dsilu(x, s):
    return s * (1.0 + x * (1.0 - s))


def _softplus(x):
    return jnp.maximum(x, 0.0) + jnp.log1p(jnp.exp(-jnp.abs(x)))


def _rstd(x):
    return lax.rsqrt(jnp.mean(x * x, axis=-1, keepdims=True) + EPS)


def _rms_bwd(dy, x, r, w):
    gn = dy * w
    dx = r * gn - x * (r * r * r) * jnp.mean(gn * x, axis=-1, keepdims=True)
    dw = jnp.sum(dy * x * r, axis=0, keepdims=True)
    return dx, dw


def _rows_before(ext, s, head, n):
    if s == 0:
        return ext[head:head + n]
    return pltpu.roll(ext, s, 0)[head:head + n]


def _rows_after(ext, s, n):
    if s == 0:
        return ext[0:n]
    return pltpu.roll(ext, ext.shape[0] - s, 0)[0:n]


def _row_spec(tm, width):
    return pl.BlockSpec((tm, width), lambda i: (i, 0))


def _const_spec(shape):
    return pl.BlockSpec(shape, lambda i: (0,) * len(shape))


_VMEM_WHOLE = pl.BlockSpec(memory_space=pltpu.VMEM)


def _fwd_in(x, w0, wt, wdt, tm=512):
    T = x.shape[0]

    def body(x_ref, w0_ref, wt_ref, wdt_ref, h_ref, z_ref, xbc_ref, q_ref, k_ref, v_ref, g_ref, dt_ref):
        xf = x_ref[...]
        h = (xf * _rstd(xf) * w0_ref[...]).astype(BF)
        h_ref[...] = h
        for ref, (lo, hi) in zip((z_ref, xbc_ref, q_ref, k_ref, v_ref, g_ref), IN_SEGMENTS):
            ref[...] = _dot_nt(h, wt_ref[lo:hi, :]).astype(ref.dtype)
        dt_ref[...] = _dot_nt(h, wdt_ref[...])

    widths = (D_MODEL, SSD_WIDTH, SSD_XBC, RET_QK_W, RET_QK_W, RET_V_W, RET_V_W)
    return pl.pallas_call(
        body, name="fwd_in", grid=(T // tm,),
        in_specs=[_row_spec(tm, D_MODEL), _const_spec((1, D_MODEL)), _VMEM_WHOLE, _VMEM_WHOLE],
        out_specs=[_row_spec(tm, w) for w in widths] + [_row_spec(tm, LANES)],
        out_shape=[S((T, w), BF) for w in widths] + [S((T, LANES), F32)],
        compiler_params=_cparams("parallel"),
    )(x, w0, wt, wdt)


def _fwd_mid(ys, yr, x, wout, wpm, wpf, wup, tm=512):
    T = x.shape[0]

    def body(ys_ref, yr_ref, x_ref, wout_ref, wpm_ref, wpf_ref, wup_ref, y_ref, x1_ref, h2_ref, graw_ref, val_ref):
        y = (jnp.dot(ys_ref[...], wout_ref[0:SSD_WIDTH, :], preferred_element_type=F32)
             + jnp.dot(yr_ref[...], wout_ref[SSD_WIDTH:, :], preferred_element_type=F32))
        y_ref[...] = y
        x1 = x_ref[...] + y * _rstd(y) * wpm_ref[...]
        x1_ref[...] = x1
        h2 = (x1 * _rstd(x1) * wpf_ref[...]).astype(BF)
        h2_ref[...] = h2
        graw_ref[...] = _dot_nt(h2, wup_ref[0:D_FF, :]).astype(BF)
        val_ref[...] = _dot_nt(h2, wup_ref[D_FF:, :]).astype(BF)

    return pl.pallas_call(
        body, name="fwd_mid", grid=(T // tm,),
        in_specs=[_row_spec(tm, SSD_WIDTH), _row_spec(tm, RET_V_W), _row_spec(tm, D_MODEL), _VMEM_WHOLE,
                  _const_spec((1, D_MODEL)), _const_spec((1, D_MODEL)), _VMEM_WHOLE],
        out_specs=[_row_spec(tm, D_MODEL), _row_spec(tm, D_MODEL), _row_spec(tm, D_MODEL), _row_spec(tm, D_FF),
                   _row_spec(tm, D_FF)],
        out_shape=[S((T, D_MODEL), F32), S((T, D_MODEL), F32), S((T, D_MODEL), BF), S((T, D_FF), BF), S((T, D_FF), BF)],
        compiler_params=_cparams("parallel"),
    )(ys, yr, x, wout, wpm, wpf, wup)


def _ffn_tail(graw, val, x1, tgt, convw, convb, wdown, wpff, tm=256):
    T = x1.shape[0]

    def body(graw_ref, val_ref, x1_ref, tgt_ref, cw_ref, cb_ref, wd_ref, wpff_ref,
             a_ref, df_ref, dval_ref, dgate_ref, dx2_ref, loss_ref, dwpff_ref, dcb_ref, carry):
        i = pl.program_id(0)

        @pl.when(i == 0)
        def _():
            carry[...] = jnp.zeros_like(carry)
            loss_ref[...] = jnp.zeros_like(loss_ref)
            dwpff_ref[...] = jnp.zeros_like(dwpff_ref)
            dcb_ref[...] = jnp.zeros_like(dcb_ref)

        g = graw_ref[...].astype(F32)
        ext = jnp.concatenate([carry[...], g], axis=0)
        carry[...] = g[tm - 8:tm]
        gate = cb_ref[...] + sum(cw_ref[j:j + 1, :] * _rows_before(ext, FFN_CONV - 1 - j, 8, tm) for j in range(FFN_CONV))
        sg = _sigmoid(gate)
        silu = gate * sg
        v = val_ref[...].astype(F32)
        a = (silu * v).astype(BF)
        a_ref[...] = a
        f = jnp.dot(a, wd_ref[...], preferred_element_type=F32)
        r = _rstd(f)
        w = wpff_ref[...]
        e = x1_ref[...] + f * r * w - tgt_ref[...]
        loss_ref[...] += jnp.sum(e * e) * (0.5 / D_MODEL)
        dx2 = e * (1.0 / D_MODEL)
        dx2_ref[...] = dx2
        df, dw = _rms_bwd(dx2, f, r, w)
        dwpff_ref[...] += dw
        dfb = df.astype(BF)
        df_ref[...] = dfb
        da = _dot_nt(dfb, wd_ref[...])
        dval_ref[...] = (da * silu).astype(BF)
        dgate = da * v * _dsilu(gate, sg)
        dcb_ref[...] += jnp.sum(dgate, axis=0, keepdims=True)
        dgate_ref[...] = dgate.astype(BF)

    return pl.pallas_call(
        body, name="ffn_tail", grid=(T // tm,),
        in_specs=[_row_spec(tm, D_FF), _row_spec(tm, D_FF), _row_spec(tm, D_MODEL), _row_spec(tm, D_MODEL),
                  _const_spec((8, D_FF)), _const_spec((1, D_FF)), _VMEM_WHOLE, _const_spec((1, D_MODEL))],
        out_specs=[_row_spec(tm, D_FF), _row_spec(tm, D_MODEL), _row_spec(tm, D_FF), _row_spec(tm, D_FF),
                   _row_spec(tm, D_MODEL), _const_spec((8, LANES)), _const_spec((1, D_MODEL)), _const_spec((1, D_FF))],
        out_shape=[S((T, D_FF), BF), S((T, D_MODEL), BF), S((T, D_FF), BF), S((T, D_FF), BF), S((T, D_MODEL), F32),
                   S((8, LANES), F32), S((1, D_MODEL), F32), S((1, D_FF), F32)],
        scratch_shapes=[pltpu.VMEM((8, D_FF), F32)],
        compiler_params=_cparams("arbitrary"),
    )(graw, val, x1, tgt, convw, convb, wdown, wpff)


def _ffn_bwd(dgate, dval, graw, x1, dx2, y, convw, wup, wpf, wpm, wout, tm=256):
    T = x1.shape[0]
    nt = T // tm
    rev = lambda i: (nt - 1 - i, 0)
    rspec = lambda w: pl.BlockSpec((tm, w), rev)

    def body(dgate_ref, dval_ref, graw_ref, x1_ref, dx2_ref, y_ref, cw_ref, wup_ref, wpf_ref, wpm_ref, wout_ref,
             dgraw_ref, dx1_ref, dy_ref, dys_ref, dyr_ref, dcw_ref, dwpf_ref, dwpm_ref, carry):
        i = pl.program_id(0)

        @pl.when(i == 0)
        def _():
            carry[...] = jnp.zeros_like(carry)
            dcw_ref[...] = jnp.zeros_like(dcw_ref)
            dwpf_ref[...] = jnp.zeros_like(dwpf_ref)
            dwpm_ref[...] = jnp.zeros_like(dwpm_ref)

        dg = dgate_ref[...].astype(F32)
        ext = jnp.concatenate([dg, carry[...]], axis=0)
        carry[...] = dg[0:8]
        g = graw_ref[...].astype(F32)
        dgraw = jnp.zeros((tm, D_FF), F32)
        for j in range(FFN_CONV):
            sj = _rows_after(ext, FFN_CONV - 1 - j, tm)
            dgraw = dgraw + cw_ref[j:j + 1, :] * sj
            dcw_ref[j:j + 1, :] += jnp.sum(sj * g, axis=0, keepdims=True)
        dgrawb = dgraw.astype(BF)
        dgraw_ref[...] = dgrawb
        dh2 = _dot(dgrawb, wup_ref[0:D_FF, :]) + _dot(dval_ref[...], wup_ref[D_FF:, :])
        x1 = x1_ref[...]
        dxa, dw = _rms_bwd(dh2, x1, _rstd(x1), wpf_ref[...])
        dwpf_ref[...] += dw
        dx1 = dx2_ref[...] + dxa
        dx1_ref[...] = dx1
        yv = y_ref[...]
        dy, dw = _rms_bwd(dx1, yv, _rstd(yv), wpm_ref[...])
        dwpm_ref[...] += dw
        dyb = dy.astype(BF)
        dy_ref[...] = dyb
        dys_ref[...] = _dot_nt(dyb, wout_ref[0:SSD_WIDTH, :]).astype(BF)
        dyr_ref[...] = _dot_nt(dyb, wout_ref[SSD_WIDTH:, :]).astype(BF)

    return pl.pallas_call(
        body, name="ffn_bwd", grid=(nt,),
        in_specs=[rspec(D_FF), rspec(D_FF), rspec(D_FF), rspec(D_MODEL), rspec(D_MODEL), rspec(D_MODEL),
                  _const_spec((8, D_FF)), _VMEM_WHOLE, _const_spec((1, D_MODEL)), _const_spec((1, D_MODEL)), _VMEM_WHOLE],
        out_specs=[rspec(D_FF), rspec(D_MODEL), rspec(D_MODEL), rspec(SSD_WIDTH), rspec(RET_V_W),
                   _const_spec((8, D_FF)), _const_spec((1, D_MODEL)), _const_spec((1, D_MODEL))],
        out_shape=[S((T, D_FF), BF), S((T, D_MODEL), F32), S((T, D_MODEL), BF), S((T, SSD_WIDTH), BF), S((T, RET_V_W), BF),
                   S((8, D_FF), F32), S((1, D_MODEL), F32), S((1, D_MODEL), F32)],
        scratch_shapes=[pltpu.VMEM((8, D_FF), F32)],
        compiler_params=_cparams("arbitrary"),
    )(dgate, dval, graw, x1, dx2, y, convw, wup, wpf, wpm, wout)


def _in_bwd(dz, dxbc, dq, dk, dv, dg, ddt, x, dx1, w0, wt, wdt, tm=512):
    T = x.shape[0]

    def body(dz_ref, dxbc_ref, dq_ref, dk_ref, dv_ref, dg_ref, ddt_ref, x_ref, dx1_ref, w0_ref, wt_ref, wdt_ref, gx_ref, dw0_ref):
        @pl.when(pl.program_id(0) == 0)
        def _():
            dw0_ref[...] = jnp.zeros_like(dw0_ref)

        dh = _dot(ddt_ref[...], wdt_ref[...])
        for ref, (lo, hi) in zip((dz_ref, dxbc_ref, dq_ref, dk_ref, dv_ref, dg_ref), IN_SEGMENTS):
            dh = dh + _dot(ref[...], wt_ref[lo:hi, :])
        xf = x_ref[...]
        dx, dw = _rms_bwd(dh, xf, _rstd(xf), w0_ref[...])
        dw0_ref[...] += dw
        gx_ref[...] = dx1_ref[...] + dx

    widths = (SSD_WIDTH, SSD_XBC, RET_QK_W, RET_QK_W, RET_V_W, RET_V_W, LANES)
    return pl.pallas_call(
        body, name="in_bwd", grid=(T // tm,),
        in_specs=[_row_spec(tm, w) for w in widths] + [_row_spec(tm, D_MODEL), _row_spec(tm, D_MODEL),
                                                       _const_spec((1, D_MODEL)), _VMEM_WHOLE, _VMEM_WHOLE],
        out_specs=[_row_spec(tm, D_MODEL), _const_spec((1, D_MODEL))],
        out_shape=[S((T, D_MODEL), F32), S((1, D_MODEL), F32)],
        compiler_params=_cparams("arbitrary"),
    )(dz, dxbc, dq, dk, dv, dg, ddt, x, dx1, w0, wt, wdt)


DW_TILE_BYTES = 6 << 20


def _matmul_tn(a, b, name, tk=1024):
    T, M = a.shape
    N = b.shape[1]
    tm_, tn = M, N
    while tm_ * tn * 4 > DW_TILE_BYTES:
        if tm_ >= tn and tm_ % 256 == 0:
            tm_ //= 2
        elif tn % 256 == 0:
            tn //= 2
        else:
            break
    nk = T // tk

    def body(a_ref, b_ref, o_ref):
        @pl.when(pl.program_id(2) == 0)
        def _():
            o_ref[...] = jnp.zeros_like(o_ref)

        o_ref[...] += _dot_tn(a_ref[...], b_ref[...])

    return pl.pallas_call(
        body, name=name, grid=(M // tm_, N // tn, nk),
        in_specs=[pl.BlockSpec((tk, tm_), lambda m, n, k: (k, m)), pl.BlockSpec((tk, tn), lambda m, n, k: (k, n))],
        out_specs=pl.BlockSpec((tm_, tn), lambda m, n, k: (m, n)),
        out_shape=S((M, N), F32),
        compiler_params=_cparams("parallel", "parallel", "arbitrary"),
    )(a, b)


def _matmul_tn_group(as_, b, name, tk=1024):
    T, N = b.shape
    na = len(as_)

    def body(*refs):
        a_refs, b_ref, o_refs = refs[:na], refs[na], refs[na + 1:]

        @pl.when(pl.program_id(0) == 0)
        def _():
            for o_ref in o_refs:
                o_ref[...] = jnp.zeros_like(o_ref)

        bt = b_ref[...]
        for a_ref, o_ref in zip(a_refs, o_refs):
            o_ref[...] += _dot_tn(a_ref[...], bt)

    return pl.pallas_call(
        body, name=name, grid=(T // tk,),
        in_specs=[_row_spec(tk, a.shape[1]) for a in as_] + [_row_spec(tk, N)],
        out_specs=[_const_spec((a.shape[1], N)) for a in as_],
        out_shape=[S((a.shape[1], N), F32) for a in as_],
        compiler_params=_cparams("arbitrary"),
    )(*as_, b)


def _tri(lower):
    r = lax.broadcasted_iota(jnp.int32, (CH, CH), 0)
    c = lax.broadcasted_iota(jnp.int32, (CH, CH), 1)
    return ((c <= r) if lower else (r <= c)).astype(F32)


def _ssd_conv(xc_ref, xh_ref, cw_ref, cb_ref, first):
    xc = xc_ref[...].astype(F32)
    xh = jnp.where(first, 0.0, xh_ref[...].astype(F32))
    ext = jnp.concatenate([xh, xc], axis=0)
    return cb_ref[...] + sum(cw_ref[j:j + 1, :] * _rows_before(ext, SSD_CONV - 1 - j, HALO, CH) for j in range(SSD_CONV))


def _ssd_decay(dtr_ref, dtb_ref, alog_ref):
    dt = _softplus(dtr_ref[...] + dtb_ref[...])
    a = -jnp.exp(alog_ref[...])
    da = dt * a
    cs = _dot_hi(_tri(True), da)
    cst = _dot_tn_hi(da, _tri(False))
    return dt, a, cs, cst


HPG = SSD_HEADS // SSD_GROUPS
GW = HPG * SSD_HEAD_DIM


def _expand_heads(src, buf):
    for h in range(SSD_HEADS):
        buf[:, h * SSD_HEAD_DIM:(h + 1) * SSD_HEAD_DIM] = jnp.broadcast_to(src[:, h:h + 1], (CH, SSD_HEAD_DIM))


def _ssd_expanded(act, dt, cs, dtx, csx):
    _expand_heads(dt, dtx)
    _expand_heads(cs, csx)
    csv = csx[...]
    last = csv[CH - 1:CH, :]
    e_exp = jnp.exp(csv)
    dec_exp = jnp.exp(last - csv)
    el_exp = jnp.exp(last)
    xs = act[:, 0:SSD_WIDTH]
    xdt = xs * dtx[...]
    return xs, xdt, xdt * dec_exp, e_exp, dec_exp, el_exp


def _decay_mats(h, cs, cst, transposed):
    r = lax.broadcasted_iota(jnp.int32, (CH, CH), 0)
    c = lax.broadcasted_iota(jnp.int32, (CH, CH), 1)
    c_col = cs[:, h:h + 1]
    c_row = cst[h:h + 1, :]
    if transposed:
        return jnp.exp(jnp.where(r <= c, c_row - c_col, -1e30))
    return jnp.exp(jnp.where(r >= c, c_col - c_row, -1e30))


def _ssd_specs(T):
    nc = T // CH
    return nc, [
        _row_spec(CH, SSD_XBC),
        pl.BlockSpec((HALO, SSD_XBC), lambda i: (jnp.maximum(i * (CH // HALO) - 1, 0), 0)),
        _row_spec(CH, LANES),
        _row_spec(CH, SSD_WIDTH),
    ]


def _groups(act):
    bm = [act[:, SSD_WIDTH + g * SSD_STATE:SSD_WIDTH + (g + 1) * SSD_STATE] for g in range(SSD_GROUPS)]
    o = SSD_WIDTH + SSD_GROUPS * SSD_STATE
    cm = [act[:, o + g * SSD_STATE:o + (g + 1) * SSD_STATE] for g in range(SSD_GROUPS)]
    return bm, cm


def _ssd_fwd(xbc, dtr, z, convw, convb, dtb, alog, dskx, nw):
    T = xbc.shape[0]
    nc, specs = _ssd_specs(T)

    def body(xc_ref, xh_ref, dtr_ref, z_ref, cw_ref, cb_ref, dtb_ref, alog_ref, dskx_ref, nw_ref,
             out_ref, y_ref, u_ref, st_ref, state, ybuf, dtx, csx):
        i = pl.program_id(0)

        @pl.when(i == 0)
        def _():
            state[...] = jnp.zeros_like(state)

        u = _ssd_conv(xc_ref, xh_ref, cw_ref, cb_ref, i == 0)
        u_ref[...] = u.astype(BF)
        act = u * _sigmoid(u)
        dt, a, cs, cst = _ssd_decay(dtr_ref, dtb_ref, alog_ref)
        xs, xdt, w, e_exp, dec_exp, el_exp = _ssd_expanded(act, dt, cs, dtx, csx)
        bm, cm = _groups(act)
        groups, heads = range(SSD_GROUPS), range(SSD_HEADS)
        gsl = [slice(g * GW, (g + 1) * GW) for g in groups]
        hsl = [slice(h * SSD_HEAD_DIM, (h + 1) * SSD_HEAD_DIM) for h in heads]
        cb = [_dot_nt(cm[g], bm[g]) for g in groups]
        yoff = [_dot(cm[g], state[g]) for g in groups]
        sloc = [_dot_tn(bm[g], w[:, gsl[g]]) for g in groups]
        lm = [_decay_mats(h, cs, cst, False) for h in heads]
        ydiag = [_dot(cb[h // HPG] * lm[h], xdt[:, hsl[h]]) for h in heads]
        for g in groups:
            st_ref[0, g] = state[g]
            ybuf[:, gsl[g]] = yoff[g] * e_exp[:, gsl[g]] + xs[:, gsl[g]] * dskx_ref[:, gsl[g]]
            state[g] = state[g] * el_exp[:, gsl[g]] + sloc[g]
        for h in heads:
            ybuf[:, hsl[h]] += ydiag[h]
        yv = ybuf[...]
        y_ref[...] = yv.astype(BF)
        zf = z_ref[...].astype(F32)
        gated = yv * (zf * _sigmoid(zf))
        out_ref[...] = (gated * _rstd(gated) * nw_ref[...]).astype(BF)

    st_spec = pl.BlockSpec((1, SSD_GROUPS, SSD_STATE, GW), lambda i: (i, 0, 0, 0))
    return pl.pallas_call(
        body, name="ssd_fwd", grid=(nc,),
        in_specs=specs + [_const_spec((8, SSD_XBC)), _const_spec((1, SSD_XBC)), _const_spec((1, LANES)),
                          _const_spec((1, LANES)), _const_spec((1, SSD_WIDTH)), _const_spec((1, SSD_WIDTH))],
        out_specs=[_row_spec(CH, SSD_WIDTH), _row_spec(CH, SSD_WIDTH), _row_spec(CH, SSD_XBC), st_spec],
        out_shape=[S((T, SSD_WIDTH), BF), S((T, SSD_WIDTH), BF), S((T, SSD_XBC), BF), S((nc, SSD_GROUPS, SSD_STATE, GW), F32)],
        scratch_shapes=[pltpu.VMEM((SSD_GROUPS, SSD_STATE, GW), F32), pltpu.VMEM((CH, SSD_WIDTH), F32),
                        pltpu.VMEM((CH, SSD_WIDTH), F32), pltpu.VMEM((CH, SSD_WIDTH), F32)],
        compiler_params=_cparams("arbitrary"),
    )(xbc, xbc, dtr, z, convw, convb, dtb, alog, dskx, nw)


def _ssd_bwd(dout, y, u, xbc, dtr, z, states, convw, dtb, alog, dskx, nw):
    T = xbc.shape[0]
    nc = T // CH
    rev = lambda i: (nc - 1 - i, 0)
    rspec = lambda w: pl.BlockSpec((CH, w), rev)
    NB = SSD_WIDTH
    NC_ = SSD_WIDTH + SSD_GROUPS * SSD_STATE

    def body(do_ref, y_ref, u_ref, xc_ref, dtr_ref, z_ref, st_ref, cw_ref, dtb_ref, alog_ref, dskx_ref, nw_ref,
             dz_ref, dxbc_ref, ddt_ref, dcw_ref, dcb_ref, ddtb_ref, dalog_ref, ddsk_ref, dnw_ref,
             dstate, ducarry, dtx, csx, dxdtbuf, dact):
        i = pl.program_id(0)

        @pl.when(i == 0)
        def _():
            dstate[...] = jnp.zeros_like(dstate)
            ducarry[...] = jnp.zeros_like(ducarry)
            for ref in (dcw_ref, dcb_ref, ddtb_ref, dalog_ref, ddsk_ref, dnw_ref):
                ref[...] = jnp.zeros_like(ref)

        xc = xc_ref[...].astype(F32)
        u = u_ref[...].astype(F32)
        sg = _sigmoid(u)
        act = u * sg
        dt, a, cs, cst = _ssd_decay(dtr_ref, dtb_ref, alog_ref)
        xs, xdt, w, e_exp, dec_exp, el_exp = _ssd_expanded(act, dt, cs, dtx, csx)
        bm, cm = _groups(act)
        yv = y_ref[...].astype(F32)
        zf = z_ref[...].astype(F32)
        sz = _sigmoid(zf)
        gated = yv * (zf * sz)
        dgated, dnw = _rms_bwd(do_ref[...].astype(F32), gated, _rstd(gated), nw_ref[...])
        dnw_ref[...] += dnw
        dz_ref[...] = (dgated * yv * _dsilu(zf, sz)).astype(BF)
        dy = dgated * (zf * sz)
        lane_of = lax.broadcasted_iota(jnp.int32, (SSD_WIDTH, LANES), 0) - SSD_HEAD_DIM * lax.broadcasted_iota(jnp.int32, (SSD_WIDTH, LANES), 1)
        expt = ((lane_of >= 0) & (lane_of < SSD_HEAD_DIM)).astype(F32)
        ddsk_ref[...] += _dot_hi(jnp.sum(dy * xs, axis=0, keepdims=True), expt)
        dcs = jnp.zeros((CH, LANES), F32)
        dcst = jnp.zeros((LANES, CH), F32)
        ddt = jnp.zeros((CH, LANES), F32)
        lane_id = lax.broadcasted_iota(jnp.int32, (CH, LANES), 1)
        row_id = lax.broadcasted_iota(jnp.int32, (LANES, CH), 0)
        lastrows = []
        for g in range(SSD_GROUPS):
            gs = slice(g * GW, (g + 1) * GW)
            st = st_ref[0, g]
            dsn = dstate[g]
            cbm = _dot_nt(cm[g], bm[g])
            cbt = _dot_nt(bm[g], cm[g])
            dy_g = dy[:, gs]
            yoff = _dot(cm[g], st) * e_exp[:, gs]
            dq = dy_g * e_exp[:, gs]
            dcm_g = _dot_nt(dq, st)
            dstate[g] = _dot_tn(cm[g], dq) + dsn * el_exp[:, gs]
            dw = _dot(bm[g], dsn)
            w_g = w[:, gs]
            dbm_g = _dot_nt(w_g, dsn)
            dww = dw * w_g
            red = dy_g * yoff - dww
            lastrows.append(jnp.sum(dsn * st, axis=0, keepdims=True) * el_exp[:, gs] + jnp.sum(dww, axis=0, keepdims=True))
            dxdtbuf[:, gs] = dw * dec_exp[:, gs]
            dcb = jnp.zeros((CH, CH), F32)
            hs = range(g * HPG, (g + 1) * HPG)
            hsl = {h: slice(h * SSD_HEAD_DIM, (h + 1) * SSD_HEAD_DIM) for h in hs}
            lm = {h: _decay_mats(h, cs, cst, False) for h in hs}
            dm = {h: _dot_nt(dy[:, hsl[h]], xdt[:, hsl[h]]) for h in hs}
            dxd = {h: _dot(cbt * _decay_mats(h, cs, cst, True), dy[:, hsl[h]]) for h in hs}
            for h in hs:
                sl = hsl[h]
                rl = slice((h - g * HPG) * SSD_HEAD_DIM, (h - g * HPG + 1) * SSD_HEAD_DIM)
                dxdt_h = dxdtbuf[:, sl] + dxd[h]
                dxdtbuf[:, sl] = dxdt_h
                dseg = dm[h] * (cbm * lm[h])
                dcb = dcb + dm[h] * lm[h]
                col = jnp.sum(dseg, axis=1, keepdims=True) + jnp.sum(red[:, rl], axis=1, keepdims=True)
                dcs = jnp.where(lane_id == h, col, dcs)
                dcst = jnp.where(row_id == h, -jnp.sum(dseg, axis=0, keepdims=True), dcst)
                ddt = jnp.where(lane_id == h, jnp.sum(dxdt_h * xs[:, sl], axis=1, keepdims=True), ddt)
            dact[:, NB + g * SSD_STATE:NB + (g + 1) * SSD_STATE] = dbm_g + _dot_tn(dcb, cm[g])
            dact[:, NC_ + g * SSD_STATE:NC_ + (g + 1) * SSD_STATE] = dcm_g + _dot(dcb, bm[g])
        dact[:, 0:SSD_WIDTH] = dy * dskx_ref[...] + dxdtbuf[...] * dtx[...]
        dlast = _dot_hi(jnp.concatenate(lastrows, axis=1), expt)
        rows = lax.broadcasted_iota(jnp.int32, (CH, LANES), 0)
        dcs = dcs + _dot_tn_hi(dcst, jnp.eye(LANES, dtype=F32)) + jnp.where(rows == CH - 1, dlast, 0.0)
        dda = _dot_hi(_tri(False), dcs)
        dalog_ref[...] += jnp.sum(dda * dt, axis=0, keepdims=True) * a
        ddt = ddt + dda * a
        ddtr = ddt * _sigmoid(dtr_ref[...] + dtb_ref[...])
        ddtb_ref[...] += jnp.sum(ddtr, axis=0, keepdims=True)
        ddt_ref[...] = ddtr.astype(BF)
        du = dact[...] * _dsilu(u, sg)
        dcb_ref[...] += jnp.sum(du, axis=0, keepdims=True)
        ext = jnp.concatenate([du, ducarry[...]], axis=0)
        ducarry[...] = du[0:8]
        dx = jnp.zeros((CH, SSD_XBC), F32)
        for j in range(SSD_CONV):
            sj = _rows_after(ext, SSD_CONV - 1 - j, CH)
            dx = dx + cw_ref[j:j + 1, :] * sj
            dcw_ref[j:j + 1, :] += jnp.sum(sj * xc, axis=0, keepdims=True)
        dxbc_ref[...] = dx.astype(BF)

    return pl.pallas_call(
        body, name="ssd_bwd", grid=(nc,),
        in_specs=[rspec(SSD_WIDTH), rspec(SSD_WIDTH), rspec(SSD_XBC), rspec(SSD_XBC), rspec(LANES), rspec(SSD_WIDTH),
                  pl.BlockSpec((1, SSD_GROUPS, SSD_STATE, GW), lambda i: (nc - 1 - i, 0, 0, 0)),
                  _const_spec((8, SSD_XBC)), _const_spec((1, LANES)),
                  _const_spec((1, LANES)), _const_spec((1, SSD_WIDTH)), _const_spec((1, SSD_WIDTH))],
        out_specs=[rspec(SSD_WIDTH), rspec(SSD_XBC), rspec(LANES),
                   _const_spec((8, SSD_XBC)), _const_spec((1, SSD_XBC)), _const_spec((1, LANES)),
                   _const_spec((1, LANES)), _const_spec((1, LANES)), _const_spec((1, SSD_WIDTH))],
        out_shape=[S((T, SSD_WIDTH), BF), S((T, SSD_XBC), BF), S((T, LANES), BF),
                   S((8, SSD_XBC), F32), S((1, SSD_XBC), F32), S((1, LANES), F32),
                   S((1, LANES), F32), S((1, LANES), F32), S((1, SSD_WIDTH), F32)],
        scratch_shapes=[pltpu.VMEM((SSD_GROUPS, SSD_STATE, GW), F32), pltpu.VMEM((8, SSD_XBC), F32),
                        pltpu.VMEM((CH, SSD_WIDTH), F32), pltpu.VMEM((CH, SSD_WIDTH), F32),
                        pltpu.VMEM((CH, SSD_WIDTH), F32), pltpu.VMEM((CH, SSD_XBC), F32)],
        compiler_params=_cparams("arbitrary"),
    )(dout, y, u, xbc, dtr, z, states, convw, dtb, alog, dskx, nw)


def _log_gamma(h):
    return float(np.log1p(-np.exp2(np.float32(-5.0 - h)), dtype=np.float32))


def _swap_halves(t):
    n = t.shape[1]
    lane = lax.broadcasted_iota(jnp.int32, t.shape, 1)
    return jnp.where((lane & (RET_QK - 1)) < RET_QK // 2, pltpu.roll(t, n - RET_QK // 2, 1), pltpu.roll(t, RET_QK // 2, 1))


def _rot(t, cos, sin):
    return t * cos + _swap_halves(t) * sin


def _rot_t(d, cos, sin):
    return d * cos + _swap_halves(d * sin)


def _ret_tables():
    lg = jnp.asarray([_log_gamma(h) for h in range(RET_HEADS)], F32)[:, None, None]
    pos = jnp.arange(CH, dtype=F32)
    rel = pos[:, None] - pos[None, :]
    dmask = jnp.where(rel >= 0, jnp.exp(lg * jnp.maximum(rel, 0.0)), 0.0)
    kdec = jnp.exp(lg * (CH - 1.0 - pos)[None, :, None])
    qdec = jnp.exp(lg * (pos + 1.0)[None, :, None])
    rows = jnp.concatenate([jnp.swapaxes(kdec, 1, 2), jnp.swapaxes(qdec, 1, 2), jnp.zeros((RET_HEADS, CH - 2, CH), F32)], axis=1)
    full = lambda t: jnp.broadcast_to(t, (RET_HEADS, CH, CH))
    return jnp.stack([dmask, jnp.swapaxes(dmask, 1, 2), full(kdec), full(qdec), full(rows)], axis=1)


def _ret_consts(h, rt_ref):
    kdec = rt_ref[h, 2][:, 0:RET_QK]
    qdec = rt_ref[h, 3][:, 0:RET_QK]
    return rt_ref[h, 0], rt_ref[h, 1], kdec, qdec, rt_ref[h, 4, 0:1, :], rt_ref[h, 4, 1:2, :], math.exp(_log_gamma(h) * CH)


_RT_SPEC = pl.BlockSpec((RET_HEADS, 5, CH, CH), lambda i: (0, 0, 0, 0))


def _ret_fwd(q, k, v, g, cos, sin, nw, rt):
    T = q.shape[0]
    nc = T // CH

    def body(q_ref, k_ref, v_ref, g_ref, cos_ref, sin_ref, nw_ref, rt_ref, out_ref, st_ref, state):
        i = pl.program_id(0)

        @pl.when(i == 0)
        def _():
            state[...] = jnp.zeros_like(state)

        cosf = jnp.tile(cos_ref[...], (1, RET_QK_W // LANES))
        sinf = jnp.tile(sin_ref[...], (1, RET_QK_W // LANES))
        qr = _rot(q_ref[...].astype(F32), cosf, sinf)
        kr = _rot(k_ref[...].astype(F32), cosf, sinf) * (RET_QK ** -0.5)
        krt = kr.T
        st_ref[0] = state[...]
        qsl = [slice(h * RET_QK, (h + 1) * RET_QK) for h in range(RET_HEADS)]
        vsl = [slice(h * RET_V, (h + 1) * RET_V) for h in range(RET_HEADS)]
        consts = [_ret_consts(h, rt_ref) for h in range(RET_HEADS)]
        scores = [_dot_nt(qr[:, qsl[h]], kr[:, qsl[h]]) * consts[h][0] for h in range(RET_HEADS)]
        cross = [_dot(qr[:, qsl[h]] * consts[h][3], state[h]) for h in range(RET_HEADS)]
        kv = [_dot(krt[qsl[h], :] * consts[h][4], v_ref[:, vsl[h]]) for h in range(RET_HEADS)]
        o_all = [_dot(scores[h], v_ref[:, vsl[h]]) + cross[h] for h in range(RET_HEADS)]
        for h in range(RET_HEADS):
            state[h] = state[h] * consts[h][6] + kv[h]
        for h in range(RET_HEADS):
            sl = slice(h * RET_V, (h + 1) * RET_V)
            o = o_all[h]
            gf = g_ref[:, sl].astype(F32)
            out_ref[:, sl] = (o * _rstd(o) * nw_ref[:, sl] * (gf * _sigmoid(gf))).astype(BF)

    return pl.pallas_call(
        body, name="ret_fwd", grid=(nc,),
        in_specs=[_row_spec(CH, RET_QK_W), _row_spec(CH, RET_QK_W), _row_spec(CH, RET_V_W), _row_spec(CH, RET_V_W),
                  _row_spec(CH, LANES), _row_spec(CH, LANES), _const_spec((1, RET_V_W)), _RT_SPEC],
        out_specs=[_row_spec(CH, RET_V_W), pl.BlockSpec((1, RET_HEADS, RET_QK, RET_V), lambda i: (i, 0, 0, 0))],
        out_shape=[S((T, RET_V_W), BF), S((nc, RET_HEADS, RET_QK, RET_V), F32)],
        scratch_shapes=[pltpu.VMEM((RET_HEADS, RET_QK, RET_V), F32)],
        compiler_params=_cparams("arbitrary"),
    )(q, k, v, g, cos, sin, nw, rt)


def _ret_bwd(dout, q, k, v, g, states, cos, sin, nw, rt):
    T = q.shape[0]
    nc = T // CH
    rev = lambda i: (nc - 1 - i, 0)
    rspec = lambda w: pl.BlockSpec((CH, w), rev)

    def body(do_ref, q_ref, k_ref, v_ref, g_ref, st_ref, cos_ref, sin_ref, nw_ref, rt_ref,
             dq_ref, dk_ref, dv_ref, dg_ref, dnw_ref, dstate, dqbuf, dkbuf):
        i = pl.program_id(0)

        @pl.when(i == 0)
        def _():
            dstate[...] = jnp.zeros_like(dstate)
            dnw_ref[...] = jnp.zeros_like(dnw_ref)

        cosf = jnp.tile(cos_ref[...], (1, RET_QK_W // LANES))
        sinf = jnp.tile(sin_ref[...], (1, RET_QK_W // LANES))
        qr = _rot(q_ref[...].astype(F32), cosf, sinf)
        kr = _rot(k_ref[...].astype(F32), cosf, sinf) * (RET_QK ** -0.5)
        qrt = qr.T
        heads = range(RET_HEADS)
        qsl = [slice(h * RET_QK, (h + 1) * RET_QK) for h in heads]
        vsl = [slice(h * RET_V, (h + 1) * RET_V) for h in heads]
        do_all = []
        consts = [_ret_consts(h, rt_ref) for h in heads]
        scores = [_dot_nt(qr[:, qsl[h]], kr[:, qsl[h]]) * consts[h][0] for h in heads]
        scores_t = [_dot_nt(kr[:, qsl[h]], qr[:, qsl[h]]) * consts[h][1] for h in heads]
        cross = [_dot(qr[:, qsl[h]] * consts[h][3], st_ref[0, h]) for h in heads]
        o_all = [_dot(scores[h], v_ref[:, vsl[h]]) + cross[h] for h in heads]
        for h in heads:
            o = o_all[h]
            rr = _rstd(o)
            of = o * rr
            gf = g_ref[:, vsl[h]].astype(F32)
            sgg = _sigmoid(gf)
            d_h = do_ref[:, vsl[h]].astype(F32)
            nw_h = nw_ref[:, vsl[h]]
            dg_ref[:, vsl[h]] = (d_h * of * nw_h * _dsilu(gf, sgg)).astype(BF)
            dt_ = d_h * (gf * sgg)
            dnw_ref[:, vsl[h]] += jnp.sum(dt_ * of, axis=0, keepdims=True)
            dof = dt_ * nw_h
            do_all.append(rr * dof - o * (rr * rr * rr) * jnp.mean(dof * o, axis=-1, keepdims=True))
        dsc = [_dot_nt(do_all[h], v_ref[:, vsl[h]]) * consts[h][0] for h in heads]
        dsc_t = [_dot_nt(v_ref[:, vsl[h]], do_all[h]) * consts[h][1] for h in heads]
        dv_a = [_dot(scores_t[h], do_all[h]) for h in heads]
        dv_b = [_dot(kr[:, qsl[h]] * consts[h][2], dstate[h]) for h in heads]
        dq_a = [_dot(dsc[h], kr[:, qsl[h]]) for h in heads]
        dq_b = [_dot_nt(do_all[h], st_ref[0, h]) * consts[h][3] for h in heads]
        dk_a = [_dot(dsc_t[h], qr[:, qsl[h]]) for h in heads]
        dk_b = [_dot_nt(v_ref[:, vsl[h]], dstate[h]) * consts[h][2] for h in heads]
        dst = [_dot(qrt[qsl[h], :] * consts[h][5], do_all[h]) for h in heads]
        for h in heads:
            dv_ref[:, vsl[h]] = (dv_a[h] + dv_b[h]).astype(BF)
            dqbuf[:, qsl[h]] = dq_a[h] + dq_b[h]
            dkbuf[:, qsl[h]] = dk_a[h] + dk_b[h]
            dstate[h] = dstate[h] * consts[h][6] + dst[h]
        dq_ref[...] = _rot_t(dqbuf[...], cosf, sinf).astype(BF)
        dk_ref[...] = (_rot_t(dkbuf[...], cosf, sinf) * (RET_QK ** -0.5)).astype(BF)

    return pl.pallas_call(
        body, name="ret_bwd", grid=(nc,),
        in_specs=[rspec(RET_V_W), rspec(RET_QK_W), rspec(RET_QK_W), rspec(RET_V_W), rspec(RET_V_W),
                  pl.BlockSpec((1, RET_HEADS, RET_QK, RET_V), lambda i: (nc - 1 - i, 0, 0, 0)),
                  rspec(LANES), rspec(LANES), _const_spec((1, RET_V_W)), _RT_SPEC],
        out_specs=[rspec(RET_QK_W), rspec(RET_QK_W), rspec(RET_V_W), rspec(RET_V_W), _const_spec((1, RET_V_W))],
        out_shape=[S((T, RET_QK_W), BF), S((T, RET_QK_W), BF), S((T, RET_V_W), BF), S((T, RET_V_W), BF),
                   S((1, RET_V_W), F32)],
        scratch_shapes=[pltpu.VMEM((RET_HEADS, RET_QK, RET_V), F32), pltpu.VMEM((CH, RET_QK_W), F32),
                        pltpu.VMEM((CH, RET_QK_W), F32)],
        compiler_params=_cparams("arbitrary"),
    )(dout, q, k, v, g, states, cos, sin, nw, rt)


_HBM = pl.BlockSpec(memory_space=pltpu.HBM)
_SEM = pl.BlockSpec(memory_space=pltpu.SEMAPHORE)
_EFFECT = pltpu.SideEffectType.DATAFLOW_SIDE_EFFECTING


ALL_PEERS = tuple(range(1, N_DEV))
ONE_PER_CHIP = (1, 2, 4, 6)
OTHER_CHIPS = (2, 4, 6)


def _split_copies(buf_refs, land_refs, send_sems, recv_sems, same, to_me, ks):
    x, y, c = lax.axis_index("x"), lax.axis_index("y"), lax.axis_index("c")
    me = 4 * x + 2 * y + c
    cps = []
    for ki, k in enumerate(ks):
        px = 1 - x if k & 4 else x
        py = 1 - y if k & 2 else y
        pc = 1 - c if k & 1 else c
        p = 4 * px + 2 * py + pc
        for b in range(len(buf_refs)):
            s = b * len(ks) + ki
            cps.append(pltpu.make_async_remote_copy(
                src_ref=buf_refs[b] if same else buf_refs[b].at[p], dst_ref=land_refs[b].at[me if to_me else p],
                send_sem=send_sems.at[s], recv_sem=recv_sems.at[s], device_id=(px, py, pc), device_id_type=pl.DeviceIdType.MESH))
    return cps


def _exchange_start(bufs, name, same, ks=ALL_PEERS):
    nb = len(bufs)
    ns = nb * len(ks)
    lands = [lax.empty((N_DEV,) + tuple(b.shape if same else b.shape[1:]), b.dtype) for b in bufs]

    def body(*refs):
        buf_refs, land_refs = refs[:nb], refs[nb:2 * nb]
        send_sems, recv_sems = refs[2 * nb], refs[2 * nb + 1]
        token = refs[-1]
        for cp in _split_copies(buf_refs, land_refs, send_sems, recv_sems, same, True, ks):
            cp.start()
        token[...] = jnp.zeros_like(token)

    hbm = lambda a: pltpu.with_memory_space_constraint(a, pltpu.HBM)
    out = pl.pallas_call(
        body, name=name,
        out_shape=(pltpu.SemaphoreType.DMA((ns,)), pltpu.SemaphoreType.DMA((ns,)),
                   *[pltpu.HBM(a.shape, a.dtype) for a in list(bufs) + lands], S((8, LANES), F32)),
        in_specs=[_HBM] * (2 * nb), out_specs=(_SEM, _SEM, *[_HBM] * (2 * nb), pl.BlockSpec(memory_space=pltpu.VMEM)),
        input_output_aliases={i: 2 + i for i in range(2 * nb)},
        compiler_params=pltpu.CompilerParams(has_side_effects=_EFFECT),
    )(*[hbm(a) for a in list(bufs) + lands])
    return out[0], out[1], list(out[2:2 + nb]), list(out[2 + nb:2 + 2 * nb]), out[-1]


def _exchange_wait(started, after, name, same, ks=ALL_PEERS):
    send_sems, recv_sems, bufs, lands, _ = started
    nb = len(bufs)
    after = list(after) if isinstance(after, (list, tuple)) else [after]

    def body(*refs):
        buf_refs, land_refs = refs[:nb], refs[nb:2 * nb]
        s_sems, r_sems = refs[2 * nb], refs[2 * nb + 1]
        for cp in _split_copies(buf_refs, land_refs, s_sems, r_sems, same, False, ks):
            cp.wait_send()
            cp.wait_recv()

    out = pl.pallas_call(
        body, name=name,
        out_shape=tuple(pltpu.HBM(a.shape, a.dtype) for a in bufs + lands),
        in_specs=[_HBM] * (2 * nb) + [_SEM, _SEM] + [pl.BlockSpec(memory_space=pl.ANY)] * len(after),
        out_specs=tuple([_HBM] * (2 * nb)),
        input_output_aliases={i: i for i in range(2 * nb)},
        compiler_params=pltpu.CompilerParams(has_side_effects=_EFFECT),
    )(*bufs, *lands, send_sems, recv_sems, *after)
    return list(out[:nb]), list(out[nb:])


def _forward_copies(land_ref, send_sems, recv_sems, sending):
    x, y, c = lax.axis_index("x"), lax.axis_index("y"), lax.axis_index("c")
    cps = []
    for ki, k in enumerate(OTHER_CHIPS):
        px = 1 - x if k & 4 else x
        py = 1 - y if k & 2 else y
        q = 4 * px + 2 * py + (c if sending else 1 - c)
        cps.append(pltpu.make_async_remote_copy(
            src_ref=land_ref.at[q], dst_ref=land_ref.at[q], send_sem=send_sems.at[ki], recv_sem=recv_sems.at[ki],
            device_id=(x, y, 1 - c), device_id_type=pl.DeviceIdType.MESH))
    return cps


def _forward_start(land, name):
    def body(land_ref, send_sems, recv_sems, land_thru, token):
        for cp in _forward_copies(land_ref, send_sems, recv_sems, True):
            cp.start()
        token[...] = jnp.zeros_like(token)

    n = len(OTHER_CHIPS)
    out = pl.pallas_call(
        body, name=name,
        out_shape=(pltpu.SemaphoreType.DMA((n,)), pltpu.SemaphoreType.DMA((n,)), pltpu.HBM(land.shape, land.dtype),
                   S((8, LANES), F32)),
        in_specs=[_HBM], out_specs=(_SEM, _SEM, _HBM, pl.BlockSpec(memory_space=pltpu.VMEM)),
        input_output_aliases={0: 2},
        compiler_params=pltpu.CompilerParams(has_side_effects=_EFFECT),
    )(pltpu.with_memory_space_constraint(land, pltpu.HBM))
    return out


def _forward_wait(started, after, name):
    send_sems, recv_sems, land, _ = started
    after = list(after) if isinstance(after, (list, tuple)) else [after]

    def body(land_ref, s_sems, r_sems, *rest):
        for cp in _forward_copies(land_ref, s_sems, r_sems, False):
            cp.wait_send()
            cp.wait_recv()

    return pl.pallas_call(
        body, name=name, out_shape=pltpu.HBM(land.shape, land.dtype),
        in_specs=[_HBM, _SEM, _SEM] + [pl.BlockSpec(memory_space=pl.ANY)] * len(after), out_specs=_HBM,
        input_output_aliases={0: 0},
        compiler_params=pltpu.CompilerParams(has_side_effects=_EFFECT),
    )(land, send_sems, recv_sems, *after)


def _sum_slabs(recv, name):
    n, R, _ = recv.shape

    def body(r_ref, o_ref):
        g = r_ref[0].astype(F32)
        for s in range(1, n):
            g = g + r_ref[s].astype(F32)
        o_ref[...] = g

    return pl.pallas_call(body, name=name, out_shape=S((R, LANES), F32))(recv)


def _adamw(recv, w, m, v, name, tr, tc=None):
    n, R, C = recv.shape
    c1 = 1.0 - ADAM_B1 ** ADAM_STEP
    c2 = 1.0 - ADAM_B2 ** ADAM_STEP

    def body(r_ref, w_ref, m_ref, v_ref, g_out, d_out, m_out, v_out):
        g = r_ref[0].astype(F32)
        for s in range(1, n):
            g = g + r_ref[s].astype(F32)
        mm = ADAM_B1 * m_ref[...] + (1.0 - ADAM_B1) * g
        vv = ADAM_B2 * v_ref[...] + (1.0 - ADAM_B2) * (g * g)
        g_out[...] = g
        m_out[...] = mm
        v_out[...] = vv
        d_out[...] = -ADAM_LR * ((mm / c1) / (jnp.sqrt(vv / c2) + ADAM_EPS) + ADAM_WD * w_ref[...])

    tc = C if tc is None else tc
    spec = pl.BlockSpec((tr, tc), lambda i, j: (i, j))
    return pl.pallas_call(
        body, name=name, grid=(R // tr, C // tc),
        in_specs=[pl.BlockSpec((n, tr, tc), lambda i, j: (0, i, j)), spec, spec, spec],
        out_specs=[spec] * 4, out_shape=[S((R, C), F32)] * 4,
        compiler_params=_cparams("parallel", "parallel"),
    )(recv, w, m, v)


def _pack(parts, rows):
    cols = []
    for p in parts:
        f = p.reshape(-1)
        cols.append(jnp.pad(f, (0, (-f.shape[0]) % LANES)))
    flat = jnp.concatenate(cols)
    return jnp.pad(flat, (0, rows * LANES - flat.shape[0])).reshape(rows, LANES)


def _unpack(buf, shapes):
    flat = buf.reshape(-1)
    out, o = [], 0
    for shp in shapes:
        n = int(np.prod(shp))
        out.append(flat[o:o + n].reshape(shp))
        o += n + (-n) % LANES
    return out


SMALL_ROWS = 200
CONV_ROWS = 16


def kernel(x, pre_mix_norm_w, w_in, ssd_conv_w, ssd_conv_b, ssd_dt_bias, ssd_a_log, ssd_d, ssd_norm_w, ret_norm_w, w_out, post_mix_norm_w, pre_ffn_norm_w, w_up, ffn_conv_w, ffn_conv_b, w_down, post_ffn_norm_w, loss_target, m_pre_mix_norm_w, m_w_in, m_ssd_conv_w, m_ssd_conv_b, m_ssd_dt_bias, m_ssd_a_log, m_ssd_d, m_ssd_norm_w, m_ret_norm_w, m_w_out, m_post_mix_norm_w, m_pre_ffn_norm_w, m_w_up, m_ffn_conv_w, m_ffn_conv_b, m_w_down, m_post_ffn_norm_w, v_pre_mix_norm_w, v_w_in, v_ssd_conv_w, v_ssd_conv_b, v_ssd_dt_bias, v_ssd_a_log, v_ssd_d, v_ssd_norm_w, v_ret_norm_w, v_w_out, v_post_mix_norm_w, v_pre_ffn_norm_w, v_w_up, v_ffn_conv_w, v_ffn_conv_b, v_w_down, v_post_ffn_norm_w):
    T = x.shape[1]
    xi, tgt = x[0], loss_target[0]
    me = 4 * lax.axis_index("x") + 2 * lax.axis_index("y") + lax.axis_index("c")
    n_in, n_up = w_in.shape[2], w_up.shape[2]
    n_out, n_down = w_out.shape[1], w_down.shape[1]
    n_sc, n_fc = ssd_conv_w.shape[2], ffn_conv_w.shape[2]

    def after(token, value):
        return value * (1.0 + token[0, 0])

    def finish(started, after_value, name, same):
        bufs, lands = _exchange_wait(started, after_value, name, same)
        own = [b if same else lax.dynamic_index_in_dim(b, me, 0, keepdims=False) for b in bufs]
        return [lax.dynamic_update_index_in_dim(l, o, me, 0) for l, o in zip(lands, own)]

    tr_ = lambda w: jnp.transpose(w[0])
    gat_in = _exchange_start([tr_(w_in).astype(BF)], "gather_in_start", True, ONE_PER_CHIP)
    gat_conv = _exchange_start([after(gat_in[4], _pack([ssd_conv_w, ffn_conv_w], CONV_ROWS))], "gather_conv_start", True)
    pad_h = lambda p: jnp.pad(p, ((0, 0), (0, LANES - SSD_HEADS)))
    dtb, alog = pad_h(ssd_dt_bias), pad_h(ssd_a_log)
    dskx = jnp.repeat(ssd_d, SSD_HEAD_DIM, axis=1)
    inv = ROPE_BASE ** (-jnp.arange(0, RET_QK, 2, dtype=F32) / RET_QK)
    ang = after(gat_conv[4], jnp.arange(T, dtype=F32)[:, None]) * inv[None, :]
    cs_, sn_ = jnp.cos(ang), jnp.sin(ang)
    cos = jnp.concatenate([cs_, cs_, cs_, cs_], axis=1)
    sin = jnp.concatenate([-sn_, sn_, -sn_, sn_], axis=1)
    rtab = _ret_tables()
    ws = [pre_mix_norm_w, ssd_conv_w, ssd_conv_b, ssd_dt_bias, ssd_a_log, ssd_d, ssd_norm_w, ret_norm_w, post_mix_norm_w,
          pre_ffn_norm_w, ffn_conv_w, ffn_conv_b, post_ffn_norm_w]
    ms = [m_pre_mix_norm_w, m_ssd_conv_w, m_ssd_conv_b, m_ssd_dt_bias, m_ssd_a_log, m_ssd_d, m_ssd_norm_w, m_ret_norm_w,
          m_post_mix_norm_w, m_pre_ffn_norm_w, m_ffn_conv_w, m_ffn_conv_b, m_post_ffn_norm_w]
    vs = [v_pre_mix_norm_w, v_ssd_conv_w, v_ssd_conv_b, v_ssd_dt_bias, v_ssd_a_log, v_ssd_d, v_ssd_norm_w, v_ret_norm_w,
          v_post_mix_norm_w, v_pre_ffn_norm_w, v_ffn_conv_w, v_ffn_conv_b, v_post_ffn_norm_w]
    out_shapes = [t.shape for t in ws]
    small_wmv = [_pack(t, SMALL_ROWS) for t in (ws, ms, vs)]
    shard_in, land_in = _exchange_wait(gat_in, [cos, sin, rtab] + small_wmv, "gather_in_wait", True, ONE_PER_CHIP)
    fwd_in_ = _forward_start(land_in[0], "gather_in_forward")
    gat_rest = _exchange_start([after(fwd_in_[3], w).astype(BF) for w in (w_out[0], tr_(w_up), w_down[0])], "gather_rest_start", True)
    g_in = lax.dynamic_update_index_in_dim(_forward_wait(fwd_in_, gat_rest[4], "gather_in_forward_wait"), shard_in[0], me, 0)
    wt = g_in.reshape(N_DEV * n_in, D_MODEL)
    wdt = jnp.pad(wt[O_DT:O_Q], ((0, LANES - SSD_HEADS), (0, 0)))

    h, z, xbc, q, k, v, g, dtr = _fwd_in(xi, pre_mix_norm_w, wt, wdt)
    yr, rst = _ret_fwd(q, k, v, g, cos, sin, ret_norm_w, rtab)
    gconv, = finish(gat_conv, yr, "gather_conv_wait", True)
    convs = [_unpack(gconv[d], [(SSD_CONV, n_sc), (FFN_CONV, n_fc)]) for d in range(N_DEV)]
    scw = jnp.pad(jnp.concatenate([c[0] for c in convs], axis=1), ((0, 8 - SSD_CONV), (0, 0)))
    fcw = jnp.pad(jnp.concatenate([c[1] for c in convs], axis=1), ((0, 8 - FFN_CONV), (0, 0)))
    ys, ypre, uconv, sst = _ssd_fwd(xbc, dtr, z, scw, ssd_conv_b, dtb, alog, dskx, ssd_norm_w)
    g_out, g_up, g_down = finish(gat_rest, [yr, ys], "gather_rest_wait", True)
    wout = g_out.reshape(N_DEV * n_out, D_MODEL)
    wup = g_up.reshape(N_DEV * n_up, D_MODEL)
    wdown = g_down.reshape(N_DEV * n_down, D_MODEL)
    y, x1, h2, graw, val = _fwd_mid(ys, yr, xi, wout, post_mix_norm_w, pre_ffn_norm_w, wup)
    a, dfb, dval, dgate, dx2, lossb, d_pff, d_fcb = _ffn_tail(graw, val, x1, tgt, fcw, ffn_conv_b, wdown, post_ffn_norm_w)
    gdown = _matmul_tn(a, dfb, "dw_down")
    sc_down = _exchange_start([gdown.reshape(N_DEV, n_down, D_MODEL).astype(BF)], "scatter_down_start", False)
    dgraw, dx1, dyb, dys, dyr, d_fcw, d_pf, d_pm = _ffn_bwd(dgate, dval, graw, x1, dx2, y, after(sc_down[4], fcw), wup,
                                                         pre_ffn_norm_w, post_mix_norm_w, wout)
    gup = jnp.concatenate([_matmul_tn(dgraw, h2, "dw_up_g"), _matmul_tn(dval, h2, "dw_up_v")], axis=0)
    gout = jnp.concatenate(_matmul_tn_group([ys, yr], dyb, "dw_out"), axis=0)
    sc_mid = _exchange_start([gup.reshape(N_DEV, n_up, D_MODEL).astype(BF), gout.reshape(N_DEV, n_out, D_MODEL).astype(BF)],
                             "scatter_mid_start", False)
    dz, dxbc, ddt, d_scw, d_scb, d_dtb, d_alog, d_dsk, d_snw = _ssd_bwd(dys, ypre, uconv, xbc, dtr, z, sst, after(sc_mid[4], scw),
                                                                      dtb, alog, dskx, ssd_norm_w)
    dq, dk, dv, dg, d_rnw = _ret_bwd(dyr, q, k, v, g, rst, cos, sin, ret_norm_w, rtab)
    g_q, g_k, g_v, g_g = _matmul_tn_group([dq, dk, dv, dg], h, "dw_ret")
    g_z, g_xbc, g_dt = _matmul_tn_group([dz, dxbc, ddt], h, "dw_ssd")
    gin = jnp.concatenate([g_z, g_xbc, g_dt[:SSD_HEADS], g_q, g_k, g_v, g_g], axis=0)
    sc_in = _exchange_start([gin.reshape(N_DEV, n_in, D_MODEL).astype(BF)], "scatter_in_start", False)
    gx, d_w0 = _in_bwd(dz, dxbc, dq, dk, dv, dg, ddt, xi, dx1, after(sc_in[4], pre_mix_norm_w), wt, wdt)
    small_full = [d_w0, d_scw[:SSD_CONV], d_scb, d_dtb[:, :SSD_HEADS], d_alog[:, :SSD_HEADS], d_dsk[:, :SSD_HEADS], d_snw, d_rnw,
                  d_pm, d_pf, d_fcw[:FFN_CONV], d_fcb, d_pff, lossb[0:1, 0:1]]
    gat_small = _exchange_start([_pack(small_full, SMALL_ROWS)], "gather_small_start", True)
    r_down, = finish(sc_down, [gx, gat_small[4]], "scatter_down_wait", False)
    r_up, r_out = finish(sc_mid, r_down, "scatter_mid_wait", False)
    per_w = [None] * 4
    per_w[3] = _adamw(r_down, w_down[0], m_w_down[0], v_w_down[0], "adamw_down", n_down)
    per_w[2] = [jnp.transpose(t) for t in _adamw(r_up, tr_(w_up), tr_(m_w_up), tr_(v_w_up), "adamw_up", n_up, 256)]
    per_w[1] = _adamw(r_out, w_out[0], m_w_out[0], v_w_out[0], "adamw_out", n_out)
    r_in, = finish(sc_in, per_w[1][0], "scatter_in_wait", False)
    per_w[0] = [jnp.transpose(t) for t in _adamw(r_in, tr_(w_in), tr_(m_w_in), tr_(v_w_in), "adamw_in", n_in, 256)]
    big = [[per_w[i][kind][None] for i in range(4)] for kind in range(4)]

    full_shapes = [t.shape for t in small_full]
    gs = _sum_slabs(finish(gat_small, per_w[0][0], "gather_small_wait", True)[0], "sum_small")
    gfull = _unpack(gs, full_shapes)
    gfull[1] = lax.dynamic_slice_in_dim(gfull[1], me * n_sc, n_sc, axis=1)
    gfull[10] = lax.dynamic_slice_in_dim(gfull[10], me * n_fc, n_fc, axis=1)
    loss = gfull.pop()[0, 0]
    small = _adamw(_pack(gfull, SMALL_ROWS)[None], *small_wmv, "adamw_small", SMALL_ROWS)
    small = [_unpack(b, out_shapes) for b in small]

    order = {"pre_mix_norm_w": ("s", 0), "w_in": ("b", 0), "ssd_conv_w": ("s", 1), "ssd_conv_b": ("s", 2),
             "ssd_dt_bias": ("s", 3), "ssd_a_log": ("s", 4), "ssd_d": ("s", 5), "ssd_norm_w": ("s", 6), "ret_norm_w": ("s", 7),
             "w_out": ("b", 1), "post_mix_norm_w": ("s", 8), "pre_ffn_norm_w": ("s", 9), "w_up": ("b", 2),
             "ffn_conv_w": ("s", 10), "ffn_conv_b": ("s", 11), "w_down": ("b", 3), "post_ffn_norm_w": ("s", 12)}
    outs = [loss, gx[None]]
    for kind in range(4):
        for name, (grp, idx) in order.items():
            outs.append(big[kind][idx] if grp == "b" else small[kind][idx])
    return tuple(outs)
```

```python
import functools
import math

import numpy as np
import jax
import jax.numpy as jnp
from jax import lax
from jax.experimental import pallas as pl
from jax.experimental.pallas import tpu as pltpu

F32 = jnp.float32
BF = jnp.bfloat16
HI = lax.Precision.HIGHEST
S = jax.ShapeDtypeStruct

D_MODEL = 1024
SSD_HEADS = 16
SSD_HEAD_DIM = 64
SSD_GROUPS = 2
SSD_STATE = 128
SSD_WIDTH = 1024
SSD_XBC = 1536
SSD_CONV = 4
RET_HEADS = 8
RET_QK = 64
RET_V = 128
RET_QK_W = 512
RET_V_W = 1024
ROPE_BASE = 10000.0
CH = 128
D_FF = 2816
FFN_CONV = 3
EPS = 1e-6
IN_WIDTH = 5648
N_DEV = 8

ADAM_LR = 0.001
ADAM_B1 = 0.9
ADAM_B2 = 0.999
ADAM_EPS = 1e-08
ADAM_WD = 0.01
ADAM_STEP = 10

LANES = 128
HALO = 16
VMEM_LIMIT = 48 * 1024 * 1024

O_Z, O_XBC, O_DT, O_Q, O_K, O_V, O_G, O_END = 0, 1024, 2560, 2576, 3088, 3600, 4624, 5648
IN_SEGMENTS = ((O_Z, O_XBC), (O_XBC, O_DT), (O_Q, O_K), (O_K, O_V), (O_V, O_G), (O_G, O_END))


def _cparams(*sem):
    return pltpu.CompilerParams(dimension_semantics=sem, vmem_limit_bytes=VMEM_LIMIT)


def _dot(a, b):
    return jnp.dot(a.astype(BF), b.astype(BF), preferred_element_type=F32)


def _dot_nt(a, b):
    return lax.dot_general(a.astype(BF), b.astype(BF), (((1,), (1,)), ((), ())), preferred_element_type=F32)


def _dot_tn(a, b):
    return lax.dot_general(a.astype(BF), b.astype(BF), (((0,), (0,)), ((), ())), preferred_element_type=F32)


def _dot_hi(a, b):
    return jnp.dot(a, b, preferred_element_type=F32, precision=HI)


def _dot_tn_hi(a, b):
    return lax.dot_general(a, b, (((0,), (0,)), ((), ())), preferred_element_type=F32, precision=HI)


def _sigmoid(x):
    return jax.nn.sigmoid(x)


def _dsilu(x, s):
    return s * (1.0 + x * (1.0 - s))


def _softplus(x):
    return jnp.maximum(x, 0.0) + jnp.log1p(jnp.exp(-jnp.abs(x)))


def _rstd(x):
    return lax.rsqrt(jnp.mean(x * x, axis=-1, keepdims=True) + EPS)


def _rms_bwd(dy, x, r, w):
    gn = dy * w
    dx = r * gn - x * (r * r * r) * jnp.mean(gn * x, axis=-1, keepdims=True)
    dw = jnp.sum(dy * x * r, axis=0, keepdims=True)
    return dx, dw


def _rows_before(ext, s, head, n):
    if s == 0:
        return ext[head:head + n]
    return pltpu.roll(ext, s, 0)[head:head + n]


def _rows_after(ext, s, n):
    if s == 0:
        return ext[0:n]
    return pltpu.roll(ext, ext.shape[0] - s, 0)[0:n]


def _row_spec(tm, width):
    return pl.BlockSpec((tm, width), lambda i: (i, 0))


def _const_spec(shape):
    return pl.BlockSpec(shape, lambda i: (0,) * len(shape))


_VMEM_WHOLE = pl.BlockSpec(memory_space=pltpu.VMEM)


def _fwd_in(x, w0, wt, wdt, tm=512):
    T = x.shape[0]

    def body(x_ref, w0_ref, wt_ref, wdt_ref, h_ref, z_ref, xbc_ref, q_ref, k_ref, v_ref, g_ref, dt_ref):
        xf = x_ref[...]
        h = (xf * _rstd(xf) * w0_ref[...]).astype(BF)
        h_ref[...] = h
        for ref, (lo, hi) in zip((z_ref, xbc_ref, q_ref, k_ref, v_ref, g_ref), IN_SEGMENTS):
            ref[...] = _dot_nt(h, wt_ref[lo:hi, :]).astype(ref.dtype)
        dt_ref[...] = _dot_nt(h, wdt_ref[...])

    widths = (D_MODEL, SSD_WIDTH, SSD_XBC, RET_QK_W, RET_QK_W, RET_V_W, RET_V_W)
    return pl.pallas_call(
        body, name="fwd_in", grid=(T // tm,),
        in_specs=[_row_spec(tm, D_MODEL), _const_spec((1, D_MODEL)), _VMEM_WHOLE, _VMEM_WHOLE],
        out_specs=[_row_spec(tm, w) for w in widths] + [_row_spec(tm, LANES)],
        out_shape=[S((T, w), BF) for w in widths] + [S((T, LANES), F32)],
        compiler_params=_cparams("parallel"),
    )(x, w0, wt, wdt)


def _fwd_mid(ys, yr, x, wout, wpm, wpf, wup, tm=512):
    T = x.shape[0]

    def body(ys_ref, yr_ref, x_ref, wout_ref, wpm_ref, wpf_ref, wup_ref, y_ref, x1_ref, h2_ref, graw_ref, val_ref):
        y = (jnp.dot(ys_ref[...], wout_ref[0:SSD_WIDTH, :], preferred_element_type=F32)
             + jnp.dot(yr_ref[...], wout_ref[SSD_WIDTH:, :], preferred_element_type=F32))
        y_ref[...] = y
        x1 = x_ref[...] + y * _rstd(y) * wpm_ref[...]
        x1_ref[...] = x1
        h2 = (x1 * _rstd(x1) * wpf_ref[...]).astype(BF)
        h2_ref[...] = h2
        graw_ref[...] = _dot_nt(h2, wup_ref[0:D_FF, :]).astype(BF)
        val_ref[...] = _dot_nt(h2, wup_ref[D_FF:, :]).astype(BF)

    return pl.pallas_call(
        body, name="fwd_mid", grid=(T // tm,),
        in_specs=[_row_spec(tm, SSD_WIDTH), _row_spec(tm, RET_V_W), _row_spec(tm, D_MODEL), _VMEM_WHOLE,
                  _const_spec((1, D_MODEL)), _const_spec((1, D_MODEL)), _VMEM_WHOLE],
        out_specs=[_row_spec(tm, D_MODEL), _row_spec(tm, D_MODEL), _row_spec(tm, D_MODEL), _row_spec(tm, D_FF),
                   _row_spec(tm, D_FF)],
        out_shape=[S((T, D_MODEL), F32), S((T, D_MODEL), F32), S((T, D_MODEL), BF), S((T, D_FF), BF), S((T, D_FF), BF)],
        compiler_params=_cparams("parallel"),
    )(ys, yr, x, wout, wpm, wpf, wup)


def _ffn_tail(graw, val, x1, tgt, convw, convb, wdown, wpff, tm=256):
    T = x1.shape[0]

    def body(graw_ref, val_ref, x1_ref, tgt_ref, cw_ref, cb_ref, wd_ref, wpff_ref,
             a_ref, df_ref, dval_ref, dgate_ref, dx2_ref, loss_ref, dwpff_ref, dcb_ref, carry):
        i = pl.program_id(0)

        @pl.when(i == 0)
        def _():
            carry[...] = jnp.zeros_like(carry)
            loss_ref[...] = jnp.zeros_like(loss_ref)
            dwpff_ref[...] = jnp.zeros_like(dwpff_ref)
            dcb_ref[...] = jnp.zeros_like(dcb_ref)

        g = graw_ref[...].astype(F32)
        ext = jnp.concatenate([carry[...], g], axis=0)
        carry[...] = g[tm - 8:tm]
        gate = cb_ref[...] + sum(cw_ref[j:j + 1, :] * _rows_before(ext, FFN_CONV - 1 - j, 8, tm) for j in range(FFN_CONV))
        sg = _sigmoid(gate)
        silu = gate * sg
        v = val_ref[...].astype(F32)
        a = (silu * v).astype(BF)
        a_ref[...] = a
        f = jnp.dot(a, wd_ref[...], preferred_element_type=F32)
        r = _rstd(f)
        w = wpff_ref[...]
        e = x1_ref[...] + f * r * w - tgt_ref[...]
        loss_ref[...] += jnp.sum(e * e) * (0.5 / D_MODEL)
        dx2 = e * (1.0 / D_MODEL)
        dx2_ref[...] = dx2
        df, dw = _rms_bwd(dx2, f, r, w)
        dwpff_ref[...] += dw
        dfb = df.astype(BF)
        df_ref[...] = dfb
        da = _dot_nt(dfb, wd_ref[...])
        dval_ref[...] = (da * silu).astype(BF)
        dgate = da * v * _dsilu(gate, sg)
        dcb_ref[...] += jnp.sum(dgate, axis=0, keepdims=True)
        dgate_ref[...] = dgate.astype(BF)

    return pl.pallas_call(
        body, name="ffn_tail", grid=(T // tm,),
        in_specs=[_row_spec(tm, D_FF), _row_spec(tm, D_FF), _row_spec(tm, D_MODEL), _row_spec(tm, D_MODEL),
                  _const_spec((8, D_FF)), _const_spec((1, D_FF)), _VMEM_WHOLE, _const_spec((1, D_MODEL))],
        out_specs=[_row_spec(tm, D_FF), _row_spec(tm, D_MODEL), _row_spec(tm, D_FF), _row_spec(tm, D_FF),
                   _row_spec(tm, D_MODEL), _const_spec((8, LANES)), _const_spec((1, D_MODEL)), _const_spec((1, D_FF))],
        out_shape=[S((T, D_FF), BF), S((T, D_MODEL), BF), S((T, D_FF), BF), S((T, D_FF), BF), S((T, D_MODEL), F32),
                   S((8, LANES), F32), S((1, D_MODEL), F32), S((1, D_FF), F32)],
        scratch_shapes=[pltpu.VMEM((8, D_FF), F32)],
        compiler_params=_cparams("arbitrary"),
    )(graw, val, x1, tgt, convw, convb, wdown, wpff)


def _ffn_bwd(dgate, dval, graw, x1, dx2, y, convw, wup, wpf, wpm, wout, tm=256):
    T = x1.shape[0]
    nt = T // tm
    rev = lambda i: (nt - 1 - i, 0)
    rspec = lambda w: pl.BlockSpec((tm, w), rev)

    def body(dgate_ref, dval_ref, graw_ref, x1_ref, dx2_ref, y_ref, cw_ref, wup_ref, wpf_ref, wpm_ref, wout_ref,
             dgraw_ref, dx1_ref, dy_ref, dys_ref, dyr_ref, dcw_ref, dwpf_ref, dwpm_ref, carry):
        i = pl.program_id(0)

        @pl.when(i == 0)
        def _():
            carry[...] = jnp.zeros_like(carry)
            dcw_ref[...] = jnp.zeros_like(dcw_ref)
            dwpf_ref[...] = jnp.zeros_like(dwpf_ref)
            dwpm_ref[...] = jnp.zeros_like(dwpm_ref)

        dg = dgate_ref[...].astype(F32)
        ext = jnp.concatenate([dg, carry[...]], axis=0)
        carry[...] = dg[0:8]
        g = graw_ref[...].astype(F32)
        dgraw = jnp.zeros((tm, D_FF), F32)
        for j in range(FFN_CONV):
            sj = _rows_after(ext, FFN_CONV - 1 - j, tm)
            dgraw = dgraw + cw_ref[j:j + 1, :] * sj
            dcw_ref[j:j + 1, :] += jnp.sum(sj * g, axis=0, keepdims=True)
        dgrawb = dgraw.astype(BF)
        dgraw_ref[...] = dgrawb
        dh2 = _dot(dgrawb, wup_ref[0:D_FF, :]) + _dot(dval_ref[...], wup_ref[D_FF:, :])
        x1 = x1_ref[...]
        dxa, dw = _rms_bwd(dh2, x1, _rstd(x1), wpf_ref[...])
        dwpf_ref[...] += dw
        dx1 = dx2_ref[...] + dxa
        dx1_ref[...] = dx1
        yv = y_ref[...]
        dy, dw = _rms_bwd(dx1, yv, _rstd(yv), wpm_ref[...])
        dwpm_ref[...] += dw
        dyb = dy.astype(BF)
        dy_ref[...] = dyb
        dys_ref[...] = _dot_nt(dyb, wout_ref[0:SSD_WIDTH, :]).astype(BF)
        dyr_ref[...] = _dot_nt(dyb, wout_ref[SSD_WIDTH:, :]).astype(BF)

    return pl.pallas_call(
        body, name="ffn_bwd", grid=(nt,),
        in_specs=[rspec(D_FF), rspec(D_FF), rspec(D_FF), rspec(D_MODEL), rspec(D_MODEL), rspec(D_MODEL),
                  _const_spec((8, D_FF)), _VMEM_WHOLE, _const_spec((1, D_MODEL)), _const_spec((1, D_MODEL)), _VMEM_WHOLE],
        out_specs=[rspec(D_FF), rspec(D_MODEL), rspec(D_MODEL), rspec(SSD_WIDTH), rspec(RET_V_W),
                   _const_spec((8, D_FF)), _const_spec((1, D_MODEL)), _const_spec((1, D_MODEL))],
        out_shape=[S((T, D_FF), BF), S((T, D_MODEL), F32), S((T, D_MODEL), BF), S((T, SSD_WIDTH), BF), S((T, RET_V_W), BF),
                   S((8, D_FF), F32), S((1, D_MODEL), F32), S((1, D_MODEL), F32)],
        scratch_shapes=[pltpu.VMEM((8, D_FF), F32)],
        compiler_params=_cparams("arbitrary"),
    )(dgate, dval, graw, x1, dx2, y, convw, wup, wpf, wpm, wout)


def _in_bwd(dz, dxbc, dq, dk, dv, dg, ddt, x, dx1, w0, wt, wdt, tm=512):
    T = x.shape[0]

    def body(dz_ref, dxbc_ref, dq_ref, dk_ref, dv_ref, dg_ref, ddt_ref, x_ref, dx1_ref, w0_ref, wt_ref, wdt_ref, gx_ref, dw0_ref):
        @pl.when(pl.program_id(0) == 0)
        def _():
            dw0_ref[...] = jnp.zeros_like(dw0_ref)

        dh = _dot(ddt_ref[...], wdt_ref[...])
        for ref, (lo, hi) in zip((dz_ref, dxbc_ref, dq_ref, dk_ref, dv_ref, dg_ref), IN_SEGMENTS):
            dh = dh + _dot(ref[...], wt_ref[lo:hi, :])
        xf = x_ref[...]
        dx, dw = _rms_bwd(dh, xf, _rstd(xf), w0_ref[...])
        dw0_ref[...] += dw
        gx_ref[...] = dx1_ref[...] + dx

    widths = (SSD_WIDTH, SSD_XBC, RET_QK_W, RET_QK_W, RET_V_W, RET_V_W, LANES)
    return pl.pallas_call(
        body, name="in_bwd", grid=(T // tm,),
        in_specs=[_row_spec(tm, w) for w in widths] + [_row_spec(tm, D_MODEL), _row_spec(tm, D_MODEL),
                                                       _const_spec((1, D_MODEL)), _VMEM_WHOLE, _VMEM_WHOLE],
        out_specs=[_row_spec(tm, D_MODEL), _const_spec((1, D_MODEL))],
        out_shape=[S((T, D_MODEL), F32), S((1, D_MODEL), F32)],
        compiler_params=_cparams("arbitrary"),
    )(dz, dxbc, dq, dk, dv, dg, ddt, x, dx1, w0, wt, wdt)


DW_TILE_BYTES = 6 << 20


def _matmul_tn(a, b, name, tk=1024):
    T, M = a.shape
    N = b.shape[1]
    tm_, tn = M, N
    while tm_ * tn * 4 > DW_TILE_BYTES:
        if tm_ >= tn and tm_ % 256 == 0:
            tm_ //= 2
        elif tn % 256 == 0:
            tn //= 2
        else:
            break
    nk = T // tk

    def body(a_ref, b_ref, o_ref):
        @pl.when(pl.program_id(2) == 0)
        def _():
            o_ref[...] = jnp.zeros_like(o_ref)

        o_ref[...] += _dot_tn(a_ref[...], b_ref[...])

    return pl.pallas_call(
        body, name=name, grid=(M // tm_, N // tn, nk),
        in_specs=[pl.BlockSpec((tk, tm_), lambda m, n, k: (k, m)), pl.BlockSpec((tk, tn), lambda m, n, k: (k, n))],
        out_specs=pl.BlockSpec((tm_, tn), lambda m, n, k: (m, n)),
        out_shape=S((M, N), F32),
        compiler_params=_cparams("parallel", "parallel", "arbitrary"),
    )(a, b)


def _matmul_tn_group(as_, b, name, tk=1024):
    T, N = b.shape
    na = len(as_)

    def body(*refs):
        a_refs, b_ref, o_refs = refs[:na], refs[na], refs[na + 1:]

        @pl.when(pl.program_id(0) == 0)
        def _():
            for o_ref in o_refs:
                o_ref[...] = jnp.zeros_like(o_ref)

        bt = b_ref[...]
        for a_ref, o_ref in zip(a_refs, o_refs):
            o_ref[...] += _dot_tn(a_ref[...], bt)

    return pl.pallas_call(
        body, name=name, grid=(T // tk,),
        in_specs=[_row_spec(tk, a.shape[1]) for a in as_] + [_row_spec(tk, N)],
        out_specs=[_const_spec((a.shape[1], N)) for a in as_],
        out_shape=[S((a.shape[1], N), F32) for a in as_],
        compiler_params=_cparams("arbitrary"),
    )(*as_, b)


def _tri(lower):
    r = lax.broadcasted_iota(jnp.int32, (CH, CH), 0)
    c = lax.broadcasted_iota(jnp.int32, (CH, CH), 1)
    return ((c <= r) if lower else (r <= c)).astype(F32)


def _ssd_conv(xc_ref, xh_ref, cw_ref, cb_ref, first):
    xc = xc_ref[...].astype(F32)
    xh = jnp.where(first, 0.0, xh_ref[...].astype(F32))
    ext = jnp.concatenate([xh, xc], axis=0)
    return cb_ref[...] + sum(cw_ref[j:j + 1, :] * _rows_before(ext, SSD_CONV - 1 - j, HALO, CH) for j in range(SSD_CONV))


def _ssd_decay(dtr_ref, dtb_ref, alog_ref):
    dt = _softplus(dtr_ref[...] + dtb_ref[...])
    a = -jnp.exp(alog_ref[...])
    da = dt * a
    cs = _dot_hi(_tri(True), da)
    cst = _dot_tn_hi(da, _tri(False))
    return dt, a, cs, cst


HPG = SSD_HEADS // SSD_GROUPS
GW = HPG * SSD_HEAD_DIM


def _expand_heads(src, buf):
    for h in range(SSD_HEADS):
        buf[:, h * SSD_HEAD_DIM:(h + 1) * SSD_HEAD_DIM] = jnp.broadcast_to(src[:, h:h + 1], (CH, SSD_HEAD_DIM))


def _ssd_expanded(act, dt, cs, dtx, csx):
    _expand_heads(dt, dtx)
    _expand_heads(cs, csx)
    csv = csx[...]
    last = csv[CH - 1:CH, :]
    e_exp = jnp.exp(csv)
    dec_exp = jnp.exp(last - csv)
    el_exp = jnp.exp(last)
    xs = act[:, 0:SSD_WIDTH]
    xdt = xs * dtx[...]
    return xs, xdt, xdt * dec_exp, e_exp, dec_exp, el_exp


def _decay_mats(h, cs, cst, transposed):
    r = lax.broadcasted_iota(jnp.int32, (CH, CH), 0)
    c = lax.broadcasted_iota(jnp.int32, (CH, CH), 1)
    c_col = cs[:, h:h + 1]
    c_row = cst[h:h + 1, :]
    if transposed:
        return jnp.exp(jnp.where(r <= c, c_row - c_col, -1e30))
    return jnp.exp(jnp.where(r >= c, c_col - c_row, -1e30))


def _ssd_specs(T):
    nc = T // CH
    return nc, [
        _row_spec(CH, SSD_XBC),
        pl.BlockSpec((HALO, SSD_XBC), lambda i: (jnp.maximum(i * (CH // HALO) - 1, 0), 0)),
        _row_spec(CH, LANES),
        _row_spec(CH, SSD_WIDTH),
    ]


def _groups(act):
    bm = [act[:, SSD_WIDTH + g * SSD_STATE:SSD_WIDTH + (g + 1) * SSD_STATE] for g in range(SSD_GROUPS)]
    o = SSD_WIDTH + SSD_GROUPS * SSD_STATE
    cm = [act[:, o + g * SSD_STATE:o + (g + 1) * SSD_STATE] for g in range(SSD_GROUPS)]
    return bm, cm


def _ssd_fwd(xbc, dtr, z, convw, convb, dtb, alog, dskx, nw):
    T = xbc.shape[0]
    nc, specs = _ssd_specs(T)

    def body(xc_ref, xh_ref, dtr_ref, z_ref, cw_ref, cb_ref, dtb_ref, alog_ref, dskx_ref, nw_ref,
             out_ref, y_ref, u_ref, st_ref, state, ybuf, dtx, csx):
        i = pl.program_id(0)

        @pl.when(i == 0)
        def _():
            state[...] = jnp.zeros_like(state)

        u = _ssd_conv(xc_ref, xh_ref, cw_ref, cb_ref, i == 0)
        u_ref[...] = u.astype(BF)
        act = u * _sigmoid(u)
        dt, a, cs, cst = _ssd_decay(dtr_ref, dtb_ref, alog_ref)
        xs, xdt, w, e_exp, dec_exp, el_exp = _ssd_expanded(act, dt, cs, dtx, csx)
        bm, cm = _groups(act)
        groups, heads = range(SSD_GROUPS), range(SSD_HEADS)
        gsl = [slice(g * GW, (g + 1) * GW) for g in groups]
        hsl = [slice(h * SSD_HEAD_DIM, (h + 1) * SSD_HEAD_DIM) for h in heads]
        cb = [_dot_nt(cm[g], bm[g]) for g in groups]
        yoff = [_dot(cm[g], state[g]) for g in groups]
        sloc = [_dot_tn(bm[g], w[:, gsl[g]]) for g in groups]
        lm = [_decay_mats(h, cs, cst, False) for h in heads]
        ydiag = [_dot(cb[h // HPG] * lm[h], xdt[:, hsl[h]]) for h in heads]
        for g in groups:
            st_ref[0, g] = state[g]
            ybuf[:, gsl[g]] = yoff[g] * e_exp[:, gsl[g]] + xs[:, gsl[g]] * dskx_ref[:, gsl[g]]
            state[g] = state[g] * el_exp[:, gsl[g]] + sloc[g]
        for h in heads:
            ybuf[:, hsl[h]] += ydiag[h]
        yv = ybuf[...]
        y_ref[...] = yv.astype(BF)
        zf = z_ref[...].astype(F32)
        gated = yv * (zf * _sigmoid(zf))
        out_ref[...] = (gated * _rstd(gated) * nw_ref[...]).astype(BF)

    st_spec = pl.BlockSpec((1, SSD_GROUPS, SSD_STATE, GW), lambda i: (i, 0, 0, 0))
    return pl.pallas_call(
        body, name="ssd_fwd", grid=(nc,),
        in_specs=specs + [_const_spec((8, SSD_XBC)), _const_spec((1, SSD_XBC)), _const_spec((1, LANES)),
                          _const_spec((1, LANES)), _const_spec((1, SSD_WIDTH)), _const_spec((1, SSD_WIDTH))],
        out_specs=[_row_spec(CH, SSD_WIDTH), _row_spec(CH, SSD_WIDTH), _row_spec(CH, SSD_XBC), st_spec],
        out_shape=[S((T, SSD_WIDTH), BF), S((T, SSD_WIDTH), BF), S((T, SSD_XBC), BF), S((nc, SSD_GROUPS, SSD_STATE, GW), F32)],
        scratch_shapes=[pltpu.VMEM((SSD_GROUPS, SSD_STATE, GW), F32), pltpu.VMEM((CH, SSD_WIDTH), F32),
                        pltpu.VMEM((CH, SSD_WIDTH), F32), pltpu.VMEM((CH, SSD_WIDTH), F32)],
        compiler_params=_cparams("arbitrary"),
    )(xbc, xbc, dtr, z, convw, convb, dtb, alog, dskx, nw)


def _ssd_bwd(dout, y, u, xbc, dtr, z, states, convw, dtb, alog, dskx, nw):
    T = xbc.shape[0]
    nc = T // CH
    rev = lambda i: (nc - 1 - i, 0)
    rspec = lambda w: pl.BlockSpec((CH, w), rev)
    NB = SSD_WIDTH
    NC_ = SSD_WIDTH + SSD_GROUPS * SSD_STATE

    def body(do_ref, y_ref, u_ref, xc_ref, dtr_ref, z_ref, st_ref, cw_ref, dtb_ref, alog_ref, dskx_ref, nw_ref,
             dz_ref, dxbc_ref, ddt_ref, dcw_ref, dcb_ref, ddtb_ref, dalog_ref, ddsk_ref, dnw_ref,
             dstate, ducarry, dtx, csx, dxdtbuf, dact):
        i = pl.program_id(0)

        @pl.when(i == 0)
        def _():
            dstate[...] = jnp.zeros_like(dstate)
            ducarry[...] = jnp.zeros_like(ducarry)
            for ref in (dcw_ref, dcb_ref, ddtb_ref, dalog_ref, ddsk_ref, dnw_ref):
                ref[...] = jnp.zeros_like(ref)

        xc = xc_ref[...].astype(F32)
        u = u_ref[...].astype(F32)
        sg = _sigmoid(u)
        act = u * sg
        dt, a, cs, cst = _ssd_decay(dtr_ref, dtb_ref, alog_ref)
        xs, xdt, w, e_exp, dec_exp, el_exp = _ssd_expanded(act, dt, cs, dtx, csx)
        bm, cm = _groups(act)
        yv = y_ref[...].astype(F32)
        zf = z_ref[...].astype(F32)
        sz = _sigmoid(zf)
        gated = yv * (zf * sz)
        dgated, dnw = _rms_bwd(do_ref[...].astype(F32), gated, _rstd(gated), nw_ref[...])
        dnw_ref[...] += dnw
        dz_ref[...] = (dgated * yv * _dsilu(zf, sz)).astype(BF)
        dy = dgated * (zf * sz)
        lane_of = lax.broadcasted_iota(jnp.int32, (SSD_WIDTH, LANES), 0) - SSD_HEAD_DIM * lax.broadcasted_iota(jnp.int32, (SSD_WIDTH, LANES), 1)
        expt = ((lane_of >= 0) & (lane_of < SSD_HEAD_DIM)).astype(F32)
        ddsk_ref[...] += _dot_hi(jnp.sum(dy * xs, axis=0, keepdims=True), expt)
        dcs = jnp.zeros((CH, LANES), F32)
        dcst = jnp.zeros((LANES, CH), F32)
        ddt = jnp.zeros((CH, LANES), F32)
        lane_id = lax.broadcasted_iota(jnp.int32, (CH, LANES), 1)
        row_id = lax.broadcasted_iota(jnp.int32, (LANES, CH), 0)
        lastrows = []
        for g in range(SSD_GROUPS):
            gs = slice(g * GW, (g + 1) * GW)
            st = st_ref[0, g]
            dsn = dstate[g]
            cbm = _dot_nt(cm[g], bm[g])
            cbt = _dot_nt(bm[g], cm[g])
            dy_g = dy[:, gs]
            yoff = _dot(cm[g], st) * e_exp[:, gs]
            dq = dy_g * e_exp[:, gs]
            dcm_g = _dot_nt(dq, st)
            dstate[g] = _dot_tn(cm[g], dq) + dsn * el_exp[:, gs]
            dw = _dot(bm[g], dsn)
            w_g = w[:, gs]
            dbm_g = _dot_nt(w_g, dsn)
            dww = dw * w_g
            red = dy_g * yoff - dww
            lastrows.append(jnp.sum(dsn * st, axis=0, keepdims=True) * el_exp[:, gs] + jnp.sum(dww, axis=0, keepdims=True))
            dxdtbuf[:, gs] = dw * dec_exp[:, gs]
            dcb = jnp.zeros((CH, CH), F32)
            hs = range(g * HPG, (g + 1) * HPG)
            hsl = {h: slice(h * SSD_HEAD_DIM, (h + 1) * SSD_HEAD_DIM) for h in hs}
            lm = {h: _decay_mats(h, cs, cst, False) for h in hs}
            dm = {h: _dot_nt(dy[:, hsl[h]], xdt[:, hsl[h]]) for h in hs}
            dxd = {h: _dot(cbt * _decay_mats(h, cs, cst, True), dy[:, hsl[h]]) for h in hs}
            for h in hs:
                sl = hsl[h]
                rl = slice((h - g * HPG) * SSD_HEAD_DIM, (h - g * HPG + 1) * SSD_HEAD_DIM)
                dxdt_h = dxdtbuf[:, sl] + dxd[h]
                dxdtbuf[:, sl] = dxdt_h
                dseg = dm[h] * (cbm * lm[h])
                dcb = dcb + dm[h] * lm[h]
                col = jnp.sum(dseg, axis=1, keepdims=True) + jnp.sum(red[:, rl], axis=1, keepdims=True)
                dcs = jnp.where(lane_id == h, col, dcs)
                dcst = jnp.where(row_id == h, -jnp.sum(dseg, axis=0, keepdims=True), dcst)
                ddt = jnp.where(lane_id == h, jnp.sum(dxdt_h * xs[:, sl], axis=1, keepdims=True), ddt)
            dact[:, NB + g * SSD_STATE:NB + (g + 1) * SSD_STATE] = dbm_g + _dot_tn(dcb, cm[g])
            dact[:, NC_ + g * SSD_STATE:NC_ + (g + 1) * SSD_STATE] = dcm_g + _dot(dcb, bm[g])
        dact[:, 0:SSD_WIDTH] = dy * dskx_ref[...] + dxdtbuf[...] * dtx[...]
        dlast = _dot_hi(jnp.concatenate(lastrows, axis=1), expt)
        rows = lax.broadcasted_iota(jnp.int32, (CH, LANES), 0)
        dcs = dcs + _dot_tn_hi(dcst, jnp.eye(LANES, dtype=F32)) + jnp.where(rows == CH - 1, dlast, 0.0)
        dda = _dot_hi(_tri(False), dcs)
        dalog_ref[...] += jnp.sum(dda * dt, axis=0, keepdims=True) * a
        ddt = ddt + dda * a
        ddtr = ddt * _sigmoid(dtr_ref[...] + dtb_ref[...])
        ddtb_ref[...] += jnp.sum(ddtr, axis=0, keepdims=True)
        ddt_ref[...] = ddtr.astype(BF)
        du = dact[...] * _dsilu(u, sg)
        dcb_ref[...] += jnp.sum(du, axis=0, keepdims=True)
        ext = jnp.concatenate([du, ducarry[...]], axis=0)
        ducarry[...] = du[0:8]
        dx = jnp.zeros((CH, SSD_XBC), F32)
        for j in range(SSD_CONV):
            sj = _rows_after(ext, SSD_CONV - 1 - j, CH)
            dx = dx + cw_ref[j:j + 1, :] * sj
            dcw_ref[j:j + 1, :] += jnp.sum(sj * xc, axis=0, keepdims=True)
        dxbc_ref[...] = dx.astype(BF)

    return pl.pallas_call(
        body, name="ssd_bwd", grid=(nc,),
        in_specs=[rspec(SSD_WIDTH), rspec(SSD_WIDTH), rspec(SSD_XBC), rspec(SSD_XBC), rspec(LANES), rspec(SSD_WIDTH),
                  pl.BlockSpec((1, SSD_GROUPS, SSD_STATE, GW), lambda i: (nc - 1 - i, 0, 0, 0)),
                  _const_spec((8, SSD_XBC)), _const_spec((1, LANES)),
                  _const_spec((1, LANES)), _const_spec((1, SSD_WIDTH)), _const_spec((1, SSD_WIDTH))],
        out_specs=[rspec(SSD_WIDTH), rspec(SSD_XBC), rspec(LANES),
                   _const_spec((8, SSD_XBC)), _const_spec((1, SSD_XBC)), _const_spec((1, LANES)),
                   _const_spec((1, LANES)), _const_spec((1, LANES)), _const_spec((1, SSD_WIDTH))],
        out_shape=[S((T, SSD_WIDTH), BF), S((T, SSD_XBC), BF), S((T, LANES), BF),
                   S((8, SSD_XBC), F32), S((1, SSD_XBC), F32), S((1, LANES), F32),
                   S((1, LANES), F32), S((1, LANES), F32), S((1, SSD_WIDTH), F32)],
        scratch_shapes=[pltpu.VMEM((SSD_GROUPS, SSD_STATE, GW), F32), pltpu.VMEM((8, SSD_XBC), F32),
                        pltpu.VMEM((CH, SSD_WIDTH), F32), pltpu.VMEM((CH, SSD_WIDTH), F32),
                        pltpu.VMEM((CH, SSD_WIDTH), F32), pltpu.VMEM((CH, SSD_XBC), F32)],
        compiler_params=_cparams("arbitrary"),
    )(dout, y, u, xbc, dtr, z, states, convw, dtb, alog, dskx, nw)


def _log_gamma(h):
    return float(np.log1p(-np.exp2(np.float32(-5.0 - h)), dtype=np.float32))


def _swap_halves(t):
    n = t.shape[1]
    lane = lax.broadcasted_iota(jnp.int32, t.shape, 1)
    return jnp.where((lane & (RET_QK - 1)) < RET_QK // 2, pltpu.roll(t, n - RET_QK // 2, 1), pltpu.roll(t, RET_QK // 2, 1))


def _rot(t, cos, sin):
    return t * cos + _swap_halves(t) * sin


def _rot_t(d, cos, sin):
    return d * cos + _swap_halves(d * sin)


def _ret_tables():
    lg = jnp.asarray([_log_gamma(h) for h in range(RET_HEADS)], F32)[:, None, None]
    pos = jnp.arange(CH, dtype=F32)
    rel = pos[:, None] - pos[None, :]
    dmask = jnp.where(rel >= 0, jnp.exp(lg * jnp.maximum(rel, 0.0)), 0.0)
    kdec = jnp.exp(lg * (CH - 1.0 - pos)[None, :, None])
    qdec = jnp.exp(lg * (pos + 1.0)[None, :, None])
    rows = jnp.concatenate([jnp.swapaxes(kdec, 1, 2), jnp.swapaxes(qdec, 1, 2), jnp.zeros((RET_HEADS, CH - 2, CH), F32)], axis=1)
    full = lambda t: jnp.broadcast_to(t, (RET_HEADS, CH, CH))
    return jnp.stack([dmask, jnp.swapaxes(dmask, 1, 2), full(kdec), full(qdec), full(rows)], axis=1)


def _ret_consts(h, rt_ref):
    kdec = rt_ref[h, 2][:, 0:RET_QK]
    qdec = rt_ref[h, 3][:, 0:RET_QK]
    return rt_ref[h, 0], rt_ref[h, 1], kdec, qdec, rt_ref[h, 4, 0:1, :], rt_ref[h, 4, 1:2, :], math.exp(_log_gamma(h) * CH)


_RT_SPEC = pl.BlockSpec((RET_HEADS, 5, CH, CH), lambda i: (0, 0, 0, 0))


def _ret_fwd(q, k, v, g, cos, sin, nw, rt):
    T = q.shape[0]
    nc = T // CH

    def body(q_ref, k_ref, v_ref, g_ref, cos_ref, sin_ref, nw_ref, rt_ref, out_ref, st_ref, state):
        i = pl.program_id(0)

        @pl.when(i == 0)
        def _():
            state[...] = jnp.zeros_like(state)

        cosf = jnp.tile(cos_ref[...], (1, RET_QK_W // LANES))
        sinf = jnp.tile(sin_ref[...], (1, RET_QK_W // LANES))
        qr = _rot(q_ref[...].astype(F32), cosf, sinf)
        kr = _rot(k_ref[...].astype(F32), cosf, sinf) * (RET_QK ** -0.5)
        krt = kr.T
        st_ref[0] = state[...]
        qsl = [slice(h * RET_QK, (h + 1) * RET_QK) for h in range(RET_HEADS)]
        vsl = [slice(h * RET_V, (h + 1) * RET_V) for h in range(RET_HEADS)]
        consts = [_ret_consts(h, rt_ref) for h in range(RET_HEADS)]
        scores = [_dot_nt(qr[:, qsl[h]], kr[:, qsl[h]]) * consts[h][0] for h in range(RET_HEADS)]
        cross = [_dot(qr[:, qsl[h]] * consts[h][3], state[h]) for h in range(RET_HEADS)]
        kv = [_dot(krt[qsl[h], :] * consts[h][4], v_ref[:, vsl[h]]) for h in range(RET_HEADS)]
        o_all = [_dot(scores[h], v_ref[:, vsl[h]]) + cross[h] for h in range(RET_HEADS)]
        for h in range(RET_HEADS):
            state[h] = state[h] * consts[h][6] + kv[h]
        for h in range(RET_HEADS):
            sl = slice(h * RET_V, (h + 1) * RET_V)
            o = o_all[h]
            gf = g_ref[:, sl].astype(F32)
            out_ref[:, sl] = (o * _rstd(o) * nw_ref[:, sl] * (gf * _sigmoid(gf))).astype(BF)

    return pl.pallas_call(
        body, name="ret_fwd", grid=(nc,),
        in_specs=[_row_spec(CH, RET_QK_W), _row_spec(CH, RET_QK_W), _row_spec(CH, RET_V_W), _row_spec(CH, RET_V_W),
                  _row_spec(CH, LANES), _row_spec(CH, LANES), _const_spec((1, RET_V_W)), _RT_SPEC],
        out_specs=[_row_spec(CH, RET_V_W), pl.BlockSpec((1, RET_HEADS, RET_QK, RET_V), lambda i: (i, 0, 0, 0))],
        out_shape=[S((T, RET_V_W), BF), S((nc, RET_HEADS, RET_QK, RET_V), F32)],
        scratch_shapes=[pltpu.VMEM((RET_HEADS, RET_QK, RET_V), F32)],
        compiler_params=_cparams("arbitrary"),
    )(q, k, v, g, cos, sin, nw, rt)


def _ret_bwd(dout, q, k, v, g, states, cos, sin, nw, rt):
    T = q.shape[0]
    nc = T // CH
    rev = lambda i: (nc - 1 - i, 0)
    rspec = lambda w: pl.BlockSpec((CH, w), rev)

    def body(do_ref, q_ref, k_ref, v_ref, g_ref, st_ref, cos_ref, sin_ref, nw_ref, rt_ref,
             dq_ref, dk_ref, dv_ref, dg_ref, dnw_ref, dstate, dqbuf, dkbuf):
        i = pl.program_id(0)

        @pl.when(i == 0)
        def _():
            dstate[...] = jnp.zeros_like(dstate)
            dnw_ref[...] = jnp.zeros_like(dnw_ref)

        cosf = jnp.tile(cos_ref[...], (1, RET_QK_W // LANES))
        sinf = jnp.tile(sin_ref[...], (1, RET_QK_W // LANES))
        qr = _rot(q_ref[...].astype(F32), cosf, sinf)
        kr = _rot(k_ref[...].astype(F32), cosf, sinf) * (RET_QK ** -0.5)
        qrt = qr.T
        heads = range(RET_HEADS)
        qsl = [slice(h * RET_QK, (h + 1) * RET_QK) for h in heads]
        vsl = [slice(h * RET_V, (h + 1) * RET_V) for h in heads]
        do_all = []
        consts = [_ret_consts(h, rt_ref) for h in heads]
        scores = [_dot_nt(qr[:, qsl[h]], kr[:, qsl[h]]) * consts[h][0] for h in heads]
        scores_t = [_dot_nt(kr[:, qsl[h]], qr[:, qsl[h]]) * consts[h][1] for h in heads]
        cross = [_dot(qr[:, qsl[h]] * consts[h][3], st_ref[0, h]) for h in heads]
        o_all = [_dot(scores[h], v_ref[:, vsl[h]]) + cross[h] for h in heads]
        for h in heads:
            o = o_all[h]
            rr = _rstd(o)
            of = o * rr
            gf = g_ref[:, vsl[h]].astype(F32)
            sgg = _sigmoid(gf)
            d_h = do_ref[:, vsl[h]].astype(F32)
            nw_h = nw_ref[:, vsl[h]]
            dg_ref[:, vsl[h]] = (d_h * of * nw_h * _dsilu(gf, sgg)).astype(BF)
            dt_ = d_h * (gf * sgg)
            dnw_ref[:, vsl[h]] += jnp.sum(dt_ * of, axis=0, keepdims=True)
            dof = dt_ * nw_h
            do_all.append(rr * dof - o * (rr * rr * rr) * jnp.mean(dof * o, axis=-1, keepdims=True))
        dsc = [_dot_nt(do_all[h], v_ref[:, vsl[h]]) * consts[h][0] for h in heads]
        dsc_t = [_dot_nt(v_ref[:, vsl[h]], do_all[h]) * consts[h][1] for h in heads]
        dv_a = [_dot(scores_t[h], do_all[h]) for h in heads]
        dv_b = [_dot(kr[:, qsl[h]] * consts[h][2], dstate[h]) for h in heads]
        dq_a = [_dot(dsc[h], kr[:, qsl[h]]) for h in heads]
        dq_b = [_dot_nt(do_all[h], st_ref[0, h]) * consts[h][3] for h in heads]
        dk_a = [_dot(dsc_t[h], qr[:, qsl[h]]) for h in heads]
        dk_b = [_dot_nt(v_ref[:, vsl[h]], dstate[h]) * consts[h][2] for h in heads]
        dst = [_dot(qrt[qsl[h], :] * consts[h][5], do_all[h]) for h in heads]
        for h in heads:
            dv_ref[:, vsl[h]] = (dv_a[h] + dv_b[h]).astype(BF)
            dqbuf[:, qsl[h]] = dq_a[h] + dq_b[h]
            dkbuf[:, qsl[h]] = dk_a[h] + dk_b[h]
            dstate[h] = dstate[h] * consts[h][6] + dst[h]
        dq_ref[...] = _rot_t(dqbuf[...], cosf, sinf).astype(BF)
        dk_ref[...] = (_rot_t(dkbuf[...], cosf, sinf) * (RET_QK ** -0.5)).astype(BF)

    return pl.pallas_call(
        body, name="ret_bwd", grid=(nc,),
        in_specs=[rspec(RET_V_W), rspec(RET_QK_W), rspec(RET_QK_W), rspec(RET_V_W), rspec(RET_V_W),
                  pl.BlockSpec((1, RET_HEADS, RET_QK, RET_V), lambda i: (nc - 1 - i, 0, 0, 0)),
                  rspec(LANES), rspec(LANES), _const_spec((1, RET_V_W)), _RT_SPEC],
        out_specs=[rspec(RET_QK_W), rspec(RET_QK_W), rspec(RET_V_W), rspec(RET_V_W), _const_spec((1, RET_V_W))],
        out_shape=[S((T, RET_QK_W), BF), S((T, RET_QK_W), BF), S((T, RET_V_W), BF), S((T, RET_V_W), BF),
                   S((1, RET_V_W), F32)],
        scratch_shapes=[pltpu.VMEM((RET_HEADS, RET_QK, RET_V), F32), pltpu.VMEM((CH, RET_QK_W), F32),
                        pltpu.VMEM((CH, RET_QK_W), F32)],
        compiler_params=_cparams("arbitrary"),
    )(dout, q, k, v, g, states, cos, sin, nw, rt)


_HBM = pl.BlockSpec(memory_space=pltpu.HBM)
_SEM = pl.BlockSpec(memory_space=pltpu.SEMAPHORE)
_EFFECT = pltpu.SideEffectType.DATAFLOW_SIDE_EFFECTING


ALL_PEERS = tuple(range(1, N_DEV))
ONE_PER_CHIP = (1, 2, 4, 6)
OTHER_CHIPS = (2, 4, 6)


def _split_copies(buf_refs, land_refs, send_sems, recv_sems, same, to_me, ks):
    x, y, c = lax.axis_index("x"), lax.axis_index("y"), lax.axis_index("c")
    me = 4 * x + 2 * y + c
    cps = []
    for ki, k in enumerate(ks):
        px = 1 - x if k & 4 else x
        py = 1 - y if k & 2 else y
        pc = 1 - c if k & 1 else c
        p = 4 * px + 2 * py + pc
        for b in range(len(buf_refs)):
            s = b * len(ks) + ki
            cps.append(pltpu.make_async_remote_copy(
                src_ref=buf_refs[b] if same else buf_refs[b].at[p], dst_ref=land_refs[b].at[me if to_me else p],
                send_sem=send_sems.at[s], recv_sem=recv_sems.at[s], device_id=(px, py, pc), device_id_type=pl.DeviceIdType.MESH))
    return cps


def _exchange_start(bufs, name, same, ks=ALL_PEERS):
    nb = len(bufs)
    ns = nb * len(ks)
    lands = [lax.empty((N_DEV,) + tuple(b.shape if same else b.shape[1:]), b.dtype) for b in bufs]

    def body(*refs):
        buf_refs, land_refs = refs[:nb], refs[nb:2 * nb]
        send_sems, recv_sems = refs[2 * nb], refs[2 * nb + 1]
        token = refs[-1]
        for cp in _split_copies(buf_refs, land_refs, send_sems, recv_sems, same, True, ks):
            cp.start()
        token[...] = jnp.zeros_like(token)

    hbm = lambda a: pltpu.with_memory_space_constraint(a, pltpu.HBM)
    out = pl.pallas_call(
        body, name=name,
        out_shape=(pltpu.SemaphoreType.DMA((ns,)), pltpu.SemaphoreType.DMA((ns,)),
                   *[pltpu.HBM(a.shape, a.dtype) for a in list(bufs) + lands], S((8, LANES), F32)),
        in_specs=[_HBM] * (2 * nb), out_specs=(_SEM, _SEM, *[_HBM] * (2 * nb), pl.BlockSpec(memory_space=pltpu.VMEM)),
        input_output_aliases={i: 2 + i for i in range(2 * nb)},
        compiler_params=pltpu.CompilerParams(has_side_effects=_EFFECT),
    )(*[hbm(a) for a in list(bufs) + lands])
    return out[0], out[1], list(out[2:2 + nb]), list(out[2 + nb:2 + 2 * nb]), out[-1]


def _exchange_wait(started, after, name, same, ks=ALL_PEERS):
    send_sems, recv_sems, bufs, lands, _ = started
    nb = len(bufs)
    after = list(after) if isinstance(after, (list, tuple)) else [after]

    def body(*refs):
        buf_refs, land_refs = refs[:nb], refs[nb:2 * nb]
        s_sems, r_sems = refs[2 * nb], refs[2 * nb + 1]
        for cp in _split_copies(buf_refs, land_refs, s_sems, r_sems, same, False, ks):
            cp.wait_send()
            cp.wait_recv()

    out = pl.pallas_call(
        body, name=name,
        out_shape=tuple(pltpu.HBM(a.shape, a.dtype) for a in bufs + lands),
        in_specs=[_HBM] * (2 * nb) + [_SEM, _SEM] + [pl.BlockSpec(memory_space=pl.ANY)] * len(after),
        out_specs=tuple([_HBM] * (2 * nb)),
        input_output_aliases={i: i for i in range(2 * nb)},
        compiler_params=pltpu.CompilerParams(has_side_effects=_EFFECT),
    )(*bufs, *lands, send_sems, recv_sems, *after)
    return list(out[:nb]), list(out[nb:])


def _forward_copies(land_ref, send_sems, recv_sems, sending):
    x, y, c = lax.axis_index("x"), lax.axis_index("y"), lax.axis_index("c")
    cps = []
    for ki, k in enumerate(OTHER_CHIPS):
        px = 1 - x if k & 4 else x
        py = 1 - y if k & 2 else y
        q = 4 * px + 2 * py + (c if sending else 1 - c)
        cps.append(pltpu.make_async_remote_copy(
            src_ref=land_ref.at[q], dst_ref=land_ref.at[q], send_sem=send_sems.at[ki], recv_sem=recv_sems.at[ki],
            device_id=(x, y, 1 - c), device_id_type=pl.DeviceIdType.MESH))
    return cps


def _forward_start(land, name):
    def body(land_ref, send_sems, recv_sems, land_thru, token):
        for cp in _forward_copies(land_ref, send_sems, recv_sems, True):
            cp.start()
        token[...] = jnp.zeros_like(token)

    n = len(OTHER_CHIPS)
    out = pl.pallas_call(
        body, name=name,
        out_shape=(pltpu.SemaphoreType.DMA((n,)), pltpu.SemaphoreType.DMA((n,)), pltpu.HBM(land.shape, land.dtype),
                   S((8, LANES), F32)),
        in_specs=[_HBM], out_specs=(_SEM, _SEM, _HBM, pl.BlockSpec(memory_space=pltpu.VMEM)),
        input_output_aliases={0: 2},
        compiler_params=pltpu.CompilerParams(has_side_effects=_EFFECT),
    )(pltpu.with_memory_space_constraint(land, pltpu.HBM))
    return out


def _forward_wait(started, after, name):
    send_sems, recv_sems, land, _ = started
    after = list(after) if isinstance(after, (list, tuple)) else [after]

    def body(land_ref, s_sems, r_sems, *rest):
        for cp in _forward_copies(land_ref, s_sems, r_sems, False):
            cp.wait_send()
            cp.wait_recv()

    return pl.pallas_call(
        body, name=name, out_shape=pltpu.HBM(land.shape, land.dtype),
        in_specs=[_HBM, _SEM, _SEM] + [pl.BlockSpec(memory_space=pl.ANY)] * len(after), out_specs=_HBM,
        input_output_aliases={0: 0},
        compiler_params=pltpu.CompilerParams(has_side_effects=_EFFECT),
    )(land, send_sems, recv_sems, *after)


def _adamw(recv, w, m, v, name, tr, tc=None):
    n, R, C = recv.shape
    c1 = 1.0 - ADAM_B1 ** ADAM_STEP
    c2 = 1.0 - ADAM_B2 ** ADAM_STEP

    def body(r_ref, w_ref, m_ref, v_ref, g_out, d_out, m_out, v_out):
        g = r_ref[0].astype(F32)
        for s in range(1, n):
            g = g + r_ref[s].astype(F32)
        mm = ADAM_B1 * m_ref[...] + (1.0 - ADAM_B1) * g
        vv = ADAM_B2 * v_ref[...] + (1.0 - ADAM_B2) * (g * g)
        g_out[...] = g
        m_out[...] = mm
        v_out[...] = vv
        d_out[...] = -ADAM_LR * ((mm / c1) / (jnp.sqrt(vv / c2) + ADAM_EPS) + ADAM_WD * w_ref[...])

    tc = C if tc is None else tc
    spec = pl.BlockSpec((tr, tc), lambda i, j: (i, j))
    return pl.pallas_call(
        body, name=name, grid=(R // tr, C // tc),
        in_specs=[pl.BlockSpec((n, tr, tc), lambda i, j: (0, i, j)), spec, spec, spec],
        out_specs=[spec] * 4, out_shape=[S((R, C), F32)] * 4,
        compiler_params=_cparams("parallel", "parallel"),
    )(recv, w, m, v)


def _pack(parts, rows):
    cols = []
    for p in parts:
        f = p.reshape(-1)
        cols.append(jnp.pad(f, (0, (-f.shape[0]) % LANES)))
    flat = jnp.concatenate(cols)
    return jnp.pad(flat, (0, rows * LANES - flat.shape[0])).reshape(rows, LANES)


def _unpack(buf, shapes):
    flat = buf.reshape(-1)
    out, o = [], 0
    for shp in shapes:
        n = int(np.prod(shp))
        out.append(flat[o:o + n].reshape(shp))
        o += n + (-n) % LANES
    return out


SMALL_ROWS = 200
CONV_ROWS = 16


def kernel(x, pre_mix_norm_w, w_in, ssd_conv_w, ssd_conv_b, ssd_dt_bias, ssd_a_log, ssd_d, ssd_norm_w, ret_norm_w, w_out, post_mix_norm_w, pre_ffn_norm_w, w_up, ffn_conv_w, ffn_conv_b, w_down, post_ffn_norm_w, loss_target, m_pre_mix_norm_w, m_w_in, m_ssd_conv_w, m_ssd_conv_b, m_ssd_dt_bias, m_ssd_a_log, m_ssd_d, m_ssd_norm_w, m_ret_norm_w, m_w_out, m_post_mix_norm_w, m_pre_ffn_norm_w, m_w_up, m_ffn_conv_w, m_ffn_conv_b, m_w_down, m_post_ffn_norm_w, v_pre_mix_norm_w, v_w_in, v_ssd_conv_w, v_ssd_conv_b, v_ssd_dt_bias, v_ssd_a_log, v_ssd_d, v_ssd_norm_w, v_ret_norm_w, v_w_out, v_post_mix_norm_w, v_pre_ffn_norm_w, v_w_up, v_ffn_conv_w, v_ffn_conv_b, v_w_down, v_post_ffn_norm_w):
    T = x.shape[1]
    xi, tgt = x[0], loss_target[0]
    me = 4 * lax.axis_index("x") + 2 * lax.axis_index("y") + lax.axis_index("c")
    n_in, n_up = w_in.shape[2], w_up.shape[2]
    n_out, n_down = w_out.shape[1], w_down.shape[1]
    n_sc, n_fc = ssd_conv_w.shape[2], ffn_conv_w.shape[2]

    def after(token, value):
        return value * (1.0 + token[0, 0])

    def finish(started, after_value, name, same):
        bufs, lands = _exchange_wait(started, after_value, name, same)
        own = [b if same else lax.dynamic_index_in_dim(b, me, 0, keepdims=False) for b in bufs]
        return [lax.dynamic_update_index_in_dim(l, o, me, 0) for l, o in zip(lands, own)]

    tr_ = lambda w: jnp.transpose(w[0])
    gat_in = _exchange_start([tr_(w_in).astype(BF)], "gather_in_start", True, ONE_PER_CHIP)
    gat_conv = _exchange_start([after(gat_in[4], _pack([ssd_conv_w, ffn_conv_w], CONV_ROWS))], "gather_conv_start", True)
    pad_h = lambda p: jnp.pad(p, ((0, 0), (0, LANES - SSD_HEADS)))
    dtb, alog = pad_h(ssd_dt_bias), pad_h(ssd_a_log)
    dskx = jnp.repeat(ssd_d, SSD_HEAD_DIM, axis=1)
    inv = ROPE_BASE ** (-jnp.arange(0, RET_QK, 2, dtype=F32) / RET_QK)
    ang = after(gat_conv[4], jnp.arange(T, dtype=F32)[:, None]) * inv[None, :]
    cs_, sn_ = jnp.cos(ang), jnp.sin(ang)
    cos = jnp.concatenate([cs_, cs_, cs_, cs_], axis=1)
    sin = jnp.concatenate([-sn_, sn_, -sn_, sn_], axis=1)
    rtab = _ret_tables()
    def exchange_layout(params):
        embed = lambda t, width: lax.dynamic_update_slice_in_dim(jnp.zeros((t.shape[1], width), F32), t[0], me * t.shape[2], axis=1)
        params = list(params)
        params[1], params[10] = embed(params[1], SSD_XBC), embed(params[10], D_FF)
        return _pack(params + [jnp.zeros((1, 1), F32)], SMALL_ROWS)

    ws = [pre_mix_norm_w, ssd_conv_w, ssd_conv_b, ssd_dt_bias, ssd_a_log, ssd_d, ssd_norm_w, ret_norm_w, post_mix_norm_w,
          pre_ffn_norm_w, ffn_conv_w, ffn_conv_b, post_ffn_norm_w]
    ms = [m_pre_mix_norm_w, m_ssd_conv_w, m_ssd_conv_b, m_ssd_dt_bias, m_ssd_a_log, m_ssd_d, m_ssd_norm_w, m_ret_norm_w,
          m_post_mix_norm_w, m_pre_ffn_norm_w, m_ffn_conv_w, m_ffn_conv_b, m_post_ffn_norm_w]
    vs = [v_pre_mix_norm_w, v_ssd_conv_w, v_ssd_conv_b, v_ssd_dt_bias, v_ssd_a_log, v_ssd_d, v_ssd_norm_w, v_ret_norm_w,
          v_post_mix_norm_w, v_pre_ffn_norm_w, v_ffn_conv_w, v_ffn_conv_b, v_post_ffn_norm_w]
    small_wmv = [exchange_layout(t) for t in (ws, ms, vs)]
    shard_in, land_in = _exchange_wait(gat_in, [cos, sin, rtab] + small_wmv, "gather_in_wait", True, ONE_PER_CHIP)
    fwd_in_ = _forward_start(land_in[0], "gather_in_forward")
    gat_rest = _exchange_start([after(fwd_in_[3], w).astype(BF) for w in (w_out[0], tr_(w_up), w_down[0])], "gather_rest_start", True)
    g_in = lax.dynamic_update_index_in_dim(_forward_wait(fwd_in_, gat_rest[4], "gather_in_forward_wait"), shard_in[0], me, 0)
    wt = g_in.reshape(N_DEV * n_in, D_MODEL)
    wdt = jnp.pad(wt[O_DT:O_Q], ((0, LANES - SSD_HEADS), (0, 0)))

    h, z, xbc, q, k, v, g, dtr = _fwd_in(xi, pre_mix_norm_w, wt, wdt)
    yr, rst = _ret_fwd(q, k, v, g, cos, sin, ret_norm_w, rtab)
    gconv, = finish(gat_conv, yr, "gather_conv_wait", True)
    convs = [_unpack(gconv[d], [(SSD_CONV, n_sc), (FFN_CONV, n_fc)]) for d in range(N_DEV)]
    scw = jnp.pad(jnp.concatenate([c[0] for c in convs], axis=1), ((0, 8 - SSD_CONV), (0, 0)))
    fcw = jnp.pad(jnp.concatenate([c[1] for c in convs], axis=1), ((0, 8 - FFN_CONV), (0, 0)))
    ys, ypre, uconv, sst = _ssd_fwd(xbc, dtr, z, scw, ssd_conv_b, dtb, alog, dskx, ssd_norm_w)
    g_out, g_up, g_down = finish(gat_rest, [yr, ys], "gather_rest_wait", True)
    wout = g_out.reshape(N_DEV * n_out, D_MODEL)
    wup = g_up.reshape(N_DEV * n_up, D_MODEL)
    wdown = g_down.reshape(N_DEV * n_down, D_MODEL)
    y, x1, h2, graw, val = _fwd_mid(ys, yr, xi, wout, post_mix_norm_w, pre_ffn_norm_w, wup)
    a, dfb, dval, dgate, dx2, lossb, d_pff, d_fcb = _ffn_tail(graw, val, x1, tgt, fcw, ffn_conv_b, wdown, post_ffn_norm_w)
    gdown = _matmul_tn(a, dfb, "dw_down")
    sc_down = _exchange_start([gdown.reshape(N_DEV, n_down, D_MODEL).astype(BF)], "scatter_down_start", False)
    dgraw, dx1, dyb, dys, dyr, d_fcw, d_pf, d_pm = _ffn_bwd(dgate, dval, graw, x1, dx2, y, after(sc_down[4], fcw), wup,
                                                         pre_ffn_norm_w, post_mix_norm_w, wout)
    gup = jnp.concatenate([_matmul_tn(dgraw, h2, "dw_up_g"), _matmul_tn(dval, h2, "dw_up_v")], axis=0)
    gout = jnp.concatenate(_matmul_tn_group([ys, yr], dyb, "dw_out"), axis=0)
    sc_mid = _exchange_start([gup.reshape(N_DEV, n_up, D_MODEL).astype(BF), gout.reshape(N_DEV, n_out, D_MODEL).astype(BF)],
                             "scatter_mid_start", False)
    dz, dxbc, ddt, d_scw, d_scb, d_dtb, d_alog, d_dsk, d_snw = _ssd_bwd(dys, ypre, uconv, xbc, dtr, z, sst, after(sc_mid[4], scw),
                                                                      dtb, alog, dskx, ssd_norm_w)
    dq, dk, dv, dg, d_rnw = _ret_bwd(dyr, q, k, v, g, rst, cos, sin, ret_norm_w, rtab)
    g_q, g_k, g_v, g_g = _matmul_tn_group([dq, dk, dv, dg], h, "dw_ret")
    g_z, g_xbc, g_dt = _matmul_tn_group([dz, dxbc, ddt], h, "dw_ssd")
    gin = jnp.concatenate([g_z, g_xbc, g_dt[:SSD_HEADS], g_q, g_k, g_v, g_g], axis=0)
    sc_in = _exchange_start([gin.reshape(N_DEV, n_in, D_MODEL).astype(BF)], "scatter_in_start", False)
    gx, d_w0 = _in_bwd(dz, dxbc, dq, dk, dv, dg, ddt, xi, dx1, after(sc_in[4], pre_mix_norm_w), wt, wdt)
    small_full = [d_w0, d_scw[:SSD_CONV], d_scb, d_dtb[:, :SSD_HEADS], d_alog[:, :SSD_HEADS], d_dsk[:, :SSD_HEADS], d_snw, d_rnw,
                  d_pm, d_pf, d_fcw[:FFN_CONV], d_fcb, d_pff, lossb[0:1, 0:1]]
    gat_small = _exchange_start([_pack(small_full, SMALL_ROWS)], "gather_small_start", True)
    r_down, = finish(sc_down, [gx, gat_small[4]], "scatter_down_wait", False)
    r_up, r_out = finish(sc_mid, r_down, "scatter_mid_wait", False)
    per_w = [None] * 4
    per_w[3] = _adamw(r_down, w_down[0], m_w_down[0], v_w_down[0], "adamw_down", n_down)
    per_w[2] = [jnp.transpose(t) for t in _adamw(r_up, tr_(w_up), tr_(m_w_up), tr_(v_w_up), "adamw_up", n_up, 256)]
    per_w[1] = _adamw(r_out, w_out[0], m_w_out[0], v_w_out[0], "adamw_out", n_out)
    r_in, = finish(sc_in, per_w[1][0], "scatter_in_wait", False)
    per_w[0] = [jnp.transpose(t) for t in _adamw(r_in, tr_(w_in), tr_(m_w_in), tr_(v_w_in), "adamw_in", n_in, 256)]
    big = [[per_w[i][kind][None] for i in range(4)] for kind in range(4)]

    recv_small, = finish(gat_small, per_w[0][0], "gather_small_wait", True)
    small = [_unpack(b, [t.shape for t in small_full]) for b in _adamw(recv_small, *small_wmv, "adamw_small", SMALL_ROWS)]
    loss = small[0][13][0, 0]
    for kind in range(4):
        small[kind][1] = lax.dynamic_slice_in_dim(small[kind][1], me * n_sc, n_sc, axis=1)[None]
        small[kind][10] = lax.dynamic_slice_in_dim(small[kind][10], me * n_fc, n_fc, axis=1)[None]

    order = {"pre_mix_norm_w": ("s", 0), "w_in": ("b", 0), "ssd_conv_w": ("s", 1), "ssd_conv_b": ("s", 2),
             "ssd_dt_bias": ("s", 3), "ssd_a_log": ("s", 4), "ssd_d": ("s", 5), "ssd_norm_w": ("s", 6), "ret_norm_w": ("s", 7),
             "w_out": ("b", 1), "post_mix_norm_w": ("s", 8), "pre_ffn_norm_w": ("s", 9), "w_up": ("b", 2),
             "ffn_conv_w": ("s", 10), "ffn_conv_b": ("s", 11), "w_down": ("b", 3), "post_ffn_norm_w": ("s", 12)}
    outs = [loss, gx[None]]
    for kind in range(4):
        for name, (grp, idx) in order.items():
            outs.append(big[kind][idx] if grp == "b" else small[kind][idx])
    return tuple(outs)
```

```python
import functools
import math

import numpy as np
import jax
import jax.numpy as jnp
from jax import lax
from jax.experimental import pallas as pl
from jax.experimental.pallas import tpu as pltpu

F32 = jnp.float32
BF = jnp.bfloat16
HI = lax.Precision.HIGHEST
S = jax.ShapeDtypeStruct

D_MODEL = 1024
SSD_HEADS = 16
SSD_HEAD_DIM = 64
SSD_GROUPS = 2
SSD_STATE = 128
SSD_WIDTH = 1024
SSD_XBC = 1536
SSD_CONV = 4
RET_HEADS = 8
RET_QK = 64
RET_V = 128
RET_QK_W = 512
RET_V_W = 1024
ROPE_BASE = 10000.0
CH = 128
D_FF = 2816
FFN_CONV = 3
EPS = 1e-6
IN_WIDTH = 5648
N_DEV = 8

ADAM_LR = 0.001
ADAM_B1 = 0.9
ADAM_B2 = 0.999
ADAM_EPS = 1e-08
ADAM_WD = 0.01
ADAM_STEP = 10

LANES = 128
HALO = 16
VMEM_LIMIT = 48 * 1024 * 1024

O_Z, O_XBC, O_DT, O_Q, O_K, O_V, O_G, O_END = 0, 1024, 2560, 2576, 3088, 3600, 4624, 5648
IN_SEGMENTS = ((O_Z, O_XBC), (O_XBC, O_DT), (O_Q, O_K), (O_K, O_V), (O_V, O_G), (O_G, O_END))


def _cparams(*sem):
    return pltpu.CompilerParams(dimension_semantics=sem, vmem_limit_bytes=VMEM_LIMIT)


def _dot(a, b):
    return jnp.dot(a.astype(BF), b.astype(BF), preferred_element_type=F32)


def _dot_nt(a, b):
    return lax.dot_general(a.astype(BF), b.astype(BF), (((1,), (1,)), ((), ())), preferred_element_type=F32)


def _dot_tn(a, b):
    return lax.dot_general(a.astype(BF), b.astype(BF), (((0,), (0,)), ((), ())), preferred_element_type=F32)


def _dot_hi(a, b):
    return jnp.dot(a, b, preferred_element_type=F32, precision=HI)


def _dot_tn_hi(a, b):
    return lax.dot_general(a, b, (((0,), (0,)), ((), ())), preferred_element_type=F32, precision=HI)


def _sigmoid(x):
    return jax.nn.sigmoid(x)


def _dsilu(x, s):
    return s * (1.0 + x * (1.0 - s))


def _softplus(x):
    return jnp.maximum(x, 0.0) + jnp.log1p(jnp.exp(-jnp.abs(x)))


def _rstd(x):
    return lax.rsqrt(jnp.mean(x * x, axis=-1, keepdims=True) + EPS)


def _rms_bwd(dy, x, r, w):
    gn = dy * w
    dx = r * gn - x * (r * r * r) * jnp.mean(gn * x, axis=-1, keepdims=True)
    dw = jnp.sum(dy * x * r, axis=0, keepdims=True)
    return dx, dw


def _rows_before(ext, s, head, n):
    if s == 0:
        return ext[head:head + n]
    return pltpu.roll(ext, s, 0)[head:head + n]


def _rows_after(ext, s, n):
    if s == 0:
        return ext[0:n]
    return pltpu.roll(ext, ext.shape[0] - s, 0)[0:n]


def _row_spec(tm, width):
    return pl.BlockSpec((tm, width), lambda i: (i, 0))


def _const_spec(shape):
    return pl.BlockSpec(shape, lambda i: (0,) * len(shape))


_VMEM_WHOLE = pl.BlockSpec(memory_space=pltpu.VMEM)


def _fwd_in(x, w0, wt, wdt, tm=512):
    T = x.shape[0]

    def body(x_ref, w0_ref, wt_ref, wdt_ref, h_ref, z_ref, xbc_ref, q_ref, k_ref, v_ref, g_ref, dt_ref):
        xf = x_ref[...]
        h = (xf * _rstd(xf) * w0_ref[...]).astype(BF)
        h_ref[...] = h
        for ref, (lo, hi) in zip((z_ref, xbc_ref, q_ref, k_ref, v_ref, g_ref), IN_SEGMENTS):
            ref[...] = _dot_nt(h, wt_ref[lo:hi, :]).astype(ref.dtype)
        dt_ref[...] = _dot_nt(h, wdt_ref[...])

    widths = (D_MODEL, SSD_WIDTH, SSD_XBC, RET_QK_W, RET_QK_W, RET_V_W, RET_V_W)
    return pl.pallas_call(
        body, name="fwd_in", grid=(T // tm,),
        in_specs=[_row_spec(tm, D_MODEL), _const_spec((1, D_MODEL)), _VMEM_WHOLE, _VMEM_WHOLE],
        out_specs=[_row_spec(tm, w) for w in widths] + [_row_spec(tm, LANES)],
        out_shape=[S((T, w), BF) for w in widths] + [S((T, LANES), F32)],
        compiler_params=_cparams("parallel"),
    )(x, w0, wt, wdt)


def _fwd_mid(ys, yr, x, wout, wpm, wpf, wup, tm=512):
    T = x.shape[0]

    def body(ys_ref, yr_ref, x_ref, wout_ref, wpm_ref, wpf_ref, wup_ref, y_ref, x1_ref, h2_ref, graw_ref, val_ref):
        y = (jnp.dot(ys_ref[...], wout_ref[0:SSD_WIDTH, :], preferred_element_type=F32)
             + jnp.dot(yr_ref[...], wout_ref[SSD_WIDTH:, :], preferred_element_type=F32))
        y_ref[...] = y
        x1 = x_ref[...] + y * _rstd(y) * wpm_ref[...]
        x1_ref[...] = x1
        h2 = (x1 * _rstd(x1) * wpf_ref[...]).astype(BF)
        h2_ref[...] = h2
        graw_ref[...] = _dot_nt(h2, wup_ref[0:D_FF, :]).astype(BF)
        val_ref[...] = _dot_nt(h2, wup_ref[D_FF:, :]).astype(BF)

    return pl.pallas_call(
        body, name="fwd_mid", grid=(T // tm,),
        in_specs=[_row_spec(tm, SSD_WIDTH), _row_spec(tm, RET_V_W), _row_spec(tm, D_MODEL), _VMEM_WHOLE,
                  _const_spec((1, D_MODEL)), _const_spec((1, D_MODEL)), _VMEM_WHOLE],
        out_specs=[_row_spec(tm, D_MODEL), _row_spec(tm, D_MODEL), _row_spec(tm, D_MODEL), _row_spec(tm, D_FF),
                   _row_spec(tm, D_FF)],
        out_shape=[S((T, D_MODEL), F32), S((T, D_MODEL), F32), S((T, D_MODEL), BF), S((T, D_FF), BF), S((T, D_FF), BF)],
        compiler_params=_cparams("parallel"),
    )(ys, yr, x, wout, wpm, wpf, wup)


def _ffn_tail(graw, val, x1, tgt, convw, convb, wdown, wpff, tm=256):
    T = x1.shape[0]

    def body(graw_ref, val_ref, x1_ref, tgt_ref, cw_ref, cb_ref, wd_ref, wpff_ref,
             a_ref, df_ref, dval_ref, dgate_ref, dx2_ref, loss_ref, dwpff_ref, dcb_ref, carry):
        i = pl.program_id(0)

        @pl.when(i == 0)
        def _():
            carry[...] = jnp.zeros_like(carry)
            loss_ref[...] = jnp.zeros_like(loss_ref)
            dwpff_ref[...] = jnp.zeros_like(dwpff_ref)
            dcb_ref[...] = jnp.zeros_like(dcb_ref)

        g = graw_ref[...].astype(F32)
        ext = jnp.concatenate([carry[...], g], axis=0)
        carry[...] = g[tm - 8:tm]
        gate = cb_ref[...] + sum(cw_ref[j:j + 1, :] * _rows_before(ext, FFN_CONV - 1 - j, 8, tm) for j in range(FFN_CONV))
        sg = _sigmoid(gate)
        silu = gate * sg
        v = val_ref[...].astype(F32)
        a = (silu * v).astype(BF)
        a_ref[...] = a
        f = jnp.dot(a, wd_ref[...], preferred_element_type=F32)
        r = _rstd(f)
        w = wpff_ref[...]
        e = x1_ref[...] + f * r * w - tgt_ref[...]
        loss_ref[...] += jnp.sum(e * e) * (0.5 / D_MODEL)
        dx2 = e * (1.0 / D_MODEL)
        dx2_ref[...] = dx2
        df, dw = _rms_bwd(dx2, f, r, w)
        dwpff_ref[...] += dw
        dfb = df.astype(BF)
        df_ref[...] = dfb
        da = _dot_nt(dfb, wd_ref[...])
        dval_ref[...] = (da * silu).astype(BF)
        dgate = da * v * _dsilu(gate, sg)
        dcb_ref[...] += jnp.sum(dgate, axis=0, keepdims=True)
        dgate_ref[...] = dgate.astype(BF)

    return pl.pallas_call(
        body, name="ffn_tail", grid=(T // tm,),
        in_specs=[_row_spec(tm, D_FF), _row_spec(tm, D_FF), _row_spec(tm, D_MODEL), _row_spec(tm, D_MODEL),
                  _const_spec((8, D_FF)), _const_spec((1, D_FF)), _VMEM_WHOLE, _const_spec((1, D_MODEL))],
        out_specs=[_row_spec(tm, D_FF), _row_spec(tm, D_MODEL), _row_spec(tm, D_FF), _row_spec(tm, D_FF),
                   _row_spec(tm, D_MODEL), _const_spec((8, LANES)), _const_spec((1, D_MODEL)), _const_spec((1, D_FF))],
        out_shape=[S((T, D_FF), BF), S((T, D_MODEL), BF), S((T, D_FF), BF), S((T, D_FF), BF), S((T, D_MODEL), F32),
                   S((8, LANES), F32), S((1, D_MODEL), F32), S((1, D_FF), F32)],
        scratch_shapes=[pltpu.VMEM((8, D_FF), F32)],
        compiler_params=_cparams("arbitrary"),
    )(graw, val, x1, tgt, convw, convb, wdown, wpff)


def _ffn_bwd(dgate, dval, graw, x1, dx2, y, convw, wup, wpf, wpm, wout, tm=256):
    T = x1.shape[0]
    nt = T // tm
    rev = lambda i: (nt - 1 - i, 0)
    rspec = lambda w: pl.BlockSpec((tm, w), rev)

    def body(dgate_ref, dval_ref, graw_ref, x1_ref, dx2_ref, y_ref, cw_ref, wup_ref, wpf_ref, wpm_ref, wout_ref,
             dgraw_ref, dx1_ref, dy_ref, dys_ref, dyr_ref, dcw_ref, dwpf_ref, dwpm_ref, carry):
        i = pl.program_id(0)

        @pl.when(i == 0)
        def _():
            carry[...] = jnp.zeros_like(carry)
            dcw_ref[...] = jnp.zeros_like(dcw_ref)
            dwpf_ref[...] = jnp.zeros_like(dwpf_ref)
            dwpm_ref[...] = jnp.zeros_like(dwpm_ref)

        dg = dgate_ref[...].astype(F32)
        ext = jnp.concatenate([dg, carry[...]], axis=0)
        carry[...] = dg[0:8]
        g = graw_ref[...].astype(F32)
        dgraw = jnp.zeros((tm, D_FF), F32)
        for j in range(FFN_CONV):
            sj = _rows_after(ext, FFN_CONV - 1 - j, tm)
            dgraw = dgraw + cw_ref[j:j + 1, :] * sj
            dcw_ref[j:j + 1, :] += jnp.sum(sj * g, axis=0, keepdims=True)
        dgrawb = dgraw.astype(BF)
        dgraw_ref[...] = dgrawb
        dh2 = _dot(dgrawb, wup_ref[0:D_FF, :]) + _dot(dval_ref[...], wup_ref[D_FF:, :])
        x1 = x1_ref[...]
        dxa, dw = _rms_bwd(dh2, x1, _rstd(x1), wpf_ref[...])
        dwpf_ref[...] += dw
        dx1 = dx2_ref[...] + dxa
        dx1_ref[...] = dx1
        yv = y_ref[...]
        dy, dw = _rms_bwd(dx1, yv, _rstd(yv), wpm_ref[...])
        dwpm_ref[...] += dw
        dyb = dy.astype(BF)
        dy_ref[...] = dyb
        dys_ref[...] = _dot_nt(dyb, wout_ref[0:SSD_WIDTH, :]).astype(BF)
        dyr_ref[...] = _dot_nt(dyb, wout_ref[SSD_WIDTH:, :]).astype(BF)

    return pl.pallas_call(
        body, name="ffn_bwd", grid=(nt,),
        in_specs=[rspec(D_FF), rspec(D_FF), rspec(D_FF), rspec(D_MODEL), rspec(D_MODEL), rspec(D_MODEL),
                  _const_spec((8, D_FF)), _VMEM_WHOLE, _const_spec((1, D_MODEL)), _const_spec((1, D_MODEL)), _VMEM_WHOLE],
        out_specs=[rspec(D_FF), rspec(D_MODEL), rspec(D_MODEL), rspec(SSD_WIDTH), rspec(RET_V_W),
                   _const_spec((8, D_FF)), _const_spec((1, D_MODEL)), _const_spec((1, D_MODEL))],
        out_shape=[S((T, D_FF), BF), S((T, D_MODEL), F32), S((T, D_MODEL), BF), S((T, SSD_WIDTH), BF), S((T, RET_V_W), BF),
                   S((8, D_FF), F32), S((1, D_MODEL), F32), S((1, D_MODEL), F32)],
        scratch_shapes=[pltpu.VMEM((8, D_FF), F32)],
        compiler_params=_cparams("arbitrary"),
    )(dgate, dval, graw, x1, dx2, y, convw, wup, wpf, wpm, wout)


def _in_bwd(dz, dxbc, dq, dk, dv, dg, ddt, x, dx1, w0, wt, wdt, tm=512):
    T = x.shape[0]

    def body(dz_ref, dxbc_ref, dq_ref, dk_ref, dv_ref, dg_ref, ddt_ref, x_ref, dx1_ref, w0_ref, wt_ref, wdt_ref, gx_ref, dw0_ref):
        @pl.when(pl.program_id(0) == 0)
        def _():
            dw0_ref[...] = jnp.zeros_like(dw0_ref)

        dh = _dot(ddt_ref[...], wdt_ref[...])
        for ref, (lo, hi) in zip((dz_ref, dxbc_ref, dq_ref, dk_ref, dv_ref, dg_ref), IN_SEGMENTS):
            dh = dh + _dot(ref[...], wt_ref[lo:hi, :])
        xf = x_ref[...]
        dx, dw = _rms_bwd(dh, xf, _rstd(xf), w0_ref[...])
        dw0_ref[...] += dw
        gx_ref[...] = dx1_ref[...] + dx

    widths = (SSD_WIDTH, SSD_XBC, RET_QK_W, RET_QK_W, RET_V_W, RET_V_W, LANES)
    return pl.pallas_call(
        body, name="in_bwd", grid=(T // tm,),
        in_specs=[_row_spec(tm, w) for w in widths] + [_row_spec(tm, D_MODEL), _row_spec(tm, D_MODEL),
                                                       _const_spec((1, D_MODEL)), _VMEM_WHOLE, _VMEM_WHOLE],
        out_specs=[_row_spec(tm, D_MODEL), _const_spec((1, D_MODEL))],
        out_shape=[S((T, D_MODEL), F32), S((1, D_MODEL), F32)],
        compiler_params=_cparams("arbitrary"),
    )(dz, dxbc, dq, dk, dv, dg, ddt, x, dx1, w0, wt, wdt)


DW_TILE_BYTES = 6 << 20


def _matmul_tn(a, b, name, tk=1024):
    T, M = a.shape
    N = b.shape[1]
    tm_, tn = M, N
    while tm_ * tn * 4 > DW_TILE_BYTES:
        if tm_ >= tn and tm_ % 256 == 0:
            tm_ //= 2
        elif tn % 256 == 0:
            tn //= 2
        else:
            break
    nk = T // tk

    def body(a_ref, b_ref, o_ref, acc):
        k = pl.program_id(2)

        @pl.when(k == 0)
        def _():
            acc[...] = jnp.zeros_like(acc)

        acc[...] += _dot_tn(a_ref[...], b_ref[...])

        @pl.when(k == nk - 1)
        def _():
            o_ref[...] = acc[...].astype(o_ref.dtype)

    return pl.pallas_call(
        body, name=name, grid=(M // tm_, N // tn, nk),
        in_specs=[pl.BlockSpec((tk, tm_), lambda m, n, k: (k, m)), pl.BlockSpec((tk, tn), lambda m, n, k: (k, n))],
        out_specs=pl.BlockSpec((tm_, tn), lambda m, n, k: (m, n)),
        out_shape=S((M, N), BF),
        scratch_shapes=[pltpu.VMEM((tm_, tn), F32)],
        compiler_params=_cparams("parallel", "parallel", "arbitrary"),
    )(a, b)


def _matmul_tn_group(as_, b, name, tk=1024):
    T, N = b.shape
    na = len(as_)
    nk = T // tk

    def body(*refs):
        a_refs, b_ref, o_refs, accs = refs[:na], refs[na], refs[na + 1:2 * na + 1], refs[2 * na + 1:]
        k = pl.program_id(0)

        @pl.when(k == 0)
        def _():
            for acc in accs:
                acc[...] = jnp.zeros_like(acc)

        bt = b_ref[...]
        for a_ref, acc in zip(a_refs, accs):
            acc[...] += _dot_tn(a_ref[...], bt)

        @pl.when(k == nk - 1)
        def _():
            for o_ref, acc in zip(o_refs, accs):
                o_ref[...] = acc[...].astype(o_ref.dtype)

    return pl.pallas_call(
        body, name=name, grid=(nk,),
        in_specs=[_row_spec(tk, a.shape[1]) for a in as_] + [_row_spec(tk, N)],
        out_specs=[_const_spec((a.shape[1], N)) for a in as_],
        out_shape=[S((a.shape[1], N), BF) for a in as_],
        scratch_shapes=[pltpu.VMEM((a.shape[1], N), F32) for a in as_],
        compiler_params=_cparams("arbitrary"),
    )(*as_, b)


def _tri(lower):
    r = lax.broadcasted_iota(jnp.int32, (CH, CH), 0)
    c = lax.broadcasted_iota(jnp.int32, (CH, CH), 1)
    return ((c <= r) if lower else (r <= c)).astype(F32)


def _ssd_conv(xc_ref, xh_ref, cw_ref, cb_ref, first):
    xc = xc_ref[...].astype(F32)
    xh = jnp.where(first, 0.0, xh_ref[...].astype(F32))
    ext = jnp.concatenate([xh, xc], axis=0)
    return cb_ref[...] + sum(cw_ref[j:j + 1, :] * _rows_before(ext, SSD_CONV - 1 - j, HALO, CH) for j in range(SSD_CONV))


def _ssd_decay(dtr_ref, dtb_ref, alog_ref):
    dt = _softplus(dtr_ref[...] + dtb_ref[...])
    a = -jnp.exp(alog_ref[...])
    da = dt * a
    cs = _dot_hi(_tri(True), da)
    cst = _dot_tn_hi(da, _tri(False))
    return dt, a, cs, cst


HPG = SSD_HEADS // SSD_GROUPS
GW = HPG * SSD_HEAD_DIM


def _expand_heads(src, buf):
    for h in range(SSD_HEADS):
        buf[:, h * SSD_HEAD_DIM:(h + 1) * SSD_HEAD_DIM] = jnp.broadcast_to(src[:, h:h + 1], (CH, SSD_HEAD_DIM))


def _ssd_expanded(act, dt, cs, dtx, csx):
    _expand_heads(dt, dtx)
    _expand_heads(cs, csx)
    csv = csx[...]
    last = csv[CH - 1:CH, :]
    e_exp = jnp.exp(csv)
    dec_exp = jnp.exp(last - csv)
    el_exp = jnp.exp(last)
    xs = act[:, 0:SSD_WIDTH]
    xdt = xs * dtx[...]
    return xs, xdt, xdt * dec_exp, e_exp, dec_exp, el_exp


def _decay_mats(h, cs, cst, transposed):
    r = lax.broadcasted_iota(jnp.int32, (CH, CH), 0)
    c = lax.broadcasted_iota(jnp.int32, (CH, CH), 1)
    c_col = cs[:, h:h + 1]
    c_row = cst[h:h + 1, :]
    if transposed:
        return jnp.exp(jnp.where(r <= c, c_row - c_col, -1e30))
    return jnp.exp(jnp.where(r >= c, c_col - c_row, -1e30))


def _ssd_specs(T):
    nc = T // CH
    return nc, [
        _row_spec(CH, SSD_XBC),
        pl.BlockSpec((HALO, SSD_XBC), lambda i: (jnp.maximum(i * (CH // HALO) - 1, 0), 0)),
        _row_spec(CH, LANES),
        _row_spec(CH, SSD_WIDTH),
    ]


def _groups(act):
    bm = [act[:, SSD_WIDTH + g * SSD_STATE:SSD_WIDTH + (g + 1) * SSD_STATE] for g in range(SSD_GROUPS)]
    o = SSD_WIDTH + SSD_GROUPS * SSD_STATE
    cm = [act[:, o + g * SSD_STATE:o + (g + 1) * SSD_STATE] for g in range(SSD_GROUPS)]
    return bm, cm


def _ssd_fwd(xbc, dtr, z, convw, convb, dtb, alog, dskx, nw):
    T = xbc.shape[0]
    nc, specs = _ssd_specs(T)

    def body(xc_ref, xh_ref, dtr_ref, z_ref, cw_ref, cb_ref, dtb_ref, alog_ref, dskx_ref, nw_ref,
             out_ref, y_ref, u_ref, st_ref, state, ybuf, dtx, csx):
        i = pl.program_id(0)

        @pl.when(i == 0)
        def _():
            state[...] = jnp.zeros_like(state)

        u = _ssd_conv(xc_ref, xh_ref, cw_ref, cb_ref, i == 0)
        u_ref[...] = u.astype(BF)
        act = u * _sigmoid(u)
        dt, a, cs, cst = _ssd_decay(dtr_ref, dtb_ref, alog_ref)
        xs, xdt, w, e_exp, dec_exp, el_exp = _ssd_expanded(act, dt, cs, dtx, csx)
        bm, cm = _groups(act)
        groups, heads = range(SSD_GROUPS), range(SSD_HEADS)
        gsl = [slice(g * GW, (g + 1) * GW) for g in groups]
        hsl = [slice(h * SSD_HEAD_DIM, (h + 1) * SSD_HEAD_DIM) for h in heads]
        cb = [_dot_nt(cm[g], bm[g]) for g in groups]
        yoff = [_dot(cm[g], state[g]) for g in groups]
        sloc = [_dot_tn(bm[g], w[:, gsl[g]]) for g in groups]
        lm = [_decay_mats(h, cs, cst, False) for h in heads]
        ydiag = [_dot(cb[h // HPG] * lm[h], xdt[:, hsl[h]]) for h in heads]
        for g in groups:
            st_ref[0, g] = state[g]
            ybuf[:, gsl[g]] = yoff[g] * e_exp[:, gsl[g]] + xs[:, gsl[g]] * dskx_ref[:, gsl[g]]
            state[g] = state[g] * el_exp[:, gsl[g]] + sloc[g]
        for h in heads:
            ybuf[:, hsl[h]] += ydiag[h]
        yv = ybuf[...]
        y_ref[...] = yv.astype(BF)
        zf = z_ref[...].astype(F32)
        gated = yv * (zf * _sigmoid(zf))
        out_ref[...] = (gated * _rstd(gated) * nw_ref[...]).astype(BF)

    st_spec = pl.BlockSpec((1, SSD_GROUPS, SSD_STATE, GW), lambda i: (i, 0, 0, 0))
    return pl.pallas_call(
        body, name="ssd_fwd", grid=(nc,),
        in_specs=specs + [_const_spec((8, SSD_XBC)), _const_spec((1, SSD_XBC)), _const_spec((1, LANES)),
                          _const_spec((1, LANES)), _const_spec((1, SSD_WIDTH)), _const_spec((1, SSD_WIDTH))],
        out_specs=[_row_spec(CH, SSD_WIDTH), _row_spec(CH, SSD_WIDTH), _row_spec(CH, SSD_XBC), st_spec],
        out_shape=[S((T, SSD_WIDTH), BF), S((T, SSD_WIDTH), BF), S((T, SSD_XBC), BF), S((nc, SSD_GROUPS, SSD_STATE, GW), F32)],
        scratch_shapes=[pltpu.VMEM((SSD_GROUPS, SSD_STATE, GW), F32), pltpu.VMEM((CH, SSD_WIDTH), F32),
                        pltpu.VMEM((CH, SSD_WIDTH), F32), pltpu.VMEM((CH, SSD_WIDTH), F32)],
        compiler_params=_cparams("arbitrary"),
    )(xbc, xbc, dtr, z, convw, convb, dtb, alog, dskx, nw)


def _ssd_bwd(dout, y, u, xbc, dtr, z, states, convw, dtb, alog, dskx, nw):
    T = xbc.shape[0]
    nc = T // CH
    rev = lambda i: (nc - 1 - i, 0)
    rspec = lambda w: pl.BlockSpec((CH, w), rev)
    NB = SSD_WIDTH
    NC_ = SSD_WIDTH + SSD_GROUPS * SSD_STATE

    def body(do_ref, y_ref, u_ref, xc_ref, dtr_ref, z_ref, st_ref, cw_ref, dtb_ref, alog_ref, dskx_ref, nw_ref,
             dz_ref, dxbc_ref, ddt_ref, dcw_ref, dcb_ref, ddtb_ref, dalog_ref, ddsk_ref, dnw_ref,
             dstate, ducarry, dtx, csx, dxdtbuf, dact):
        i = pl.program_id(0)

        @pl.when(i == 0)
        def _():
            dstate[...] = jnp.zeros_like(dstate)
            ducarry[...] = jnp.zeros_like(ducarry)
            for ref in (dcw_ref, dcb_ref, ddtb_ref, dalog_ref, ddsk_ref, dnw_ref):
                ref[...] = jnp.zeros_like(ref)

        xc = xc_ref[...].astype(F32)
        u = u_ref[...].astype(F32)
        sg = _sigmoid(u)
        act = u * sg
        dt, a, cs, cst = _ssd_decay(dtr_ref, dtb_ref, alog_ref)
        xs, xdt, w, e_exp, dec_exp, el_exp = _ssd_expanded(act, dt, cs, dtx, csx)
        bm, cm = _groups(act)
        yv = y_ref[...].astype(F32)
        zf = z_ref[...].astype(F32)
        sz = _sigmoid(zf)
        gated = yv * (zf * sz)
        dgated, dnw = _rms_bwd(do_ref[...].astype(F32), gated, _rstd(gated), nw_ref[...])
        dnw_ref[...] += dnw
        dz_ref[...] = (dgated * yv * _dsilu(zf, sz)).astype(BF)
        dy = dgated * (zf * sz)
        lane_of = lax.broadcasted_iota(jnp.int32, (SSD_WIDTH, LANES), 0) - SSD_HEAD_DIM * lax.broadcasted_iota(jnp.int32, (SSD_WIDTH, LANES), 1)
        expt = ((lane_of >= 0) & (lane_of < SSD_HEAD_DIM)).astype(F32)
        ddsk_ref[...] += _dot_hi(jnp.sum(dy * xs, axis=0, keepdims=True), expt)
        dcs = jnp.zeros((CH, LANES), F32)
        dcst = jnp.zeros((LANES, CH), F32)
        ddt = jnp.zeros((CH, LANES), F32)
        lane_id = lax.broadcasted_iota(jnp.int32, (CH, LANES), 1)
        row_id = lax.broadcasted_iota(jnp.int32, (LANES, CH), 0)
        lastrows = []
        for g in range(SSD_GROUPS):
            gs = slice(g * GW, (g + 1) * GW)
            st = st_ref[0, g]
            dsn = dstate[g]
            cbm = _dot_nt(cm[g], bm[g])
            cbt = _dot_nt(bm[g], cm[g])
            dy_g = dy[:, gs]
            yoff = _dot(cm[g], st) * e_exp[:, gs]
            dq = dy_g * e_exp[:, gs]
            dcm_g = _dot_nt(dq, st)
            dstate[g] = _dot_tn(cm[g], dq) + dsn * el_exp[:, gs]
            dw = _dot(bm[g], dsn)
            w_g = w[:, gs]
            dbm_g = _dot_nt(w_g, dsn)
            dww = dw * w_g
            red = dy_g * yoff - dww
            lastrows.append(jnp.sum(dsn * st, axis=0, keepdims=True) * el_exp[:, gs] + jnp.sum(dww, axis=0, keepdims=True))
            dxdtbuf[:, gs] = dw * dec_exp[:, gs]
            dcb = jnp.zeros((CH, CH), F32)
            hs = range(g * HPG, (g + 1) * HPG)
            hsl = {h: slice(h * SSD_HEAD_DIM, (h + 1) * SSD_HEAD_DIM) for h in hs}
            lm = {h: _decay_mats(h, cs, cst, False) for h in hs}
            dm = {h: _dot_nt(dy[:, hsl[h]], xdt[:, hsl[h]]) for h in hs}
            dxd = {h: _dot(cbt * _decay_mats(h, cs, cst, True), dy[:, hsl[h]]) for h in hs}
            for h in hs:
                sl = hsl[h]
                rl = slice((h - g * HPG) * SSD_HEAD_DIM, (h - g * HPG + 1) * SSD_HEAD_DIM)
                dxdt_h = dxdtbuf[:, sl] + dxd[h]
                dxdtbuf[:, sl] = dxdt_h
                dseg = dm[h] * (cbm * lm[h])
                dcb = dcb + dm[h] * lm[h]
                col = jnp.sum(dseg, axis=1, keepdims=True) + jnp.sum(red[:, rl], axis=1, keepdims=True)
                dcs = jnp.where(lane_id == h, col, dcs)
                dcst = jnp.where(row_id == h, -jnp.sum(dseg, axis=0, keepdims=True), dcst)
                ddt = jnp.where(lane_id == h, jnp.sum(dxdt_h * xs[:, sl], axis=1, keepdims=True), ddt)
            dact[:, NB + g * SSD_STATE:NB + (g + 1) * SSD_STATE] = dbm_g + _dot_tn(dcb, cm[g])
            dact[:, NC_ + g * SSD_STATE:NC_ + (g + 1) * SSD_STATE] = dcm_g + _dot(dcb, bm[g])
        dact[:, 0:SSD_WIDTH] = dy * dskx_ref[...] + dxdtbuf[...] * dtx[...]
        dlast = _dot_hi(jnp.concatenate(lastrows, axis=1), expt)
        rows = lax.broadcasted_iota(jnp.int32, (CH, LANES), 0)
        dcs = dcs + _dot_tn_hi(dcst, jnp.eye(LANES, dtype=F32)) + jnp.where(rows == CH - 1, dlast, 0.0)
        dda = _dot_hi(_tri(False), dcs)
        dalog_ref[...] += jnp.sum(dda * dt, axis=0, keepdims=True) * a
        ddt = ddt + dda * a
        ddtr = ddt * _sigmoid(dtr_ref[...] + dtb_ref[...])
        ddtb_ref[...] += jnp.sum(ddtr, axis=0, keepdims=True)
        ddt_ref[...] = ddtr.astype(BF)
        du = dact[...] * _dsilu(u, sg)
        dcb_ref[...] += jnp.sum(du, axis=0, keepdims=True)
        ext = jnp.concatenate([du, ducarry[...]], axis=0)
        ducarry[...] = du[0:8]
        dx = jnp.zeros((CH, SSD_XBC), F32)
        for j in range(SSD_CONV):
            sj = _rows_after(ext, SSD_CONV - 1 - j, CH)
            dx = dx + cw_ref[j:j + 1, :] * sj
            dcw_ref[j:j + 1, :] += jnp.sum(sj * xc, axis=0, keepdims=True)
        dxbc_ref[...] = dx.astype(BF)

    return pl.pallas_call(
        body, name="ssd_bwd", grid=(nc,),
        in_specs=[rspec(SSD_WIDTH), rspec(SSD_WIDTH), rspec(SSD_XBC), rspec(SSD_XBC), rspec(LANES), rspec(SSD_WIDTH),
                  pl.BlockSpec((1, SSD_GROUPS, SSD_STATE, GW), lambda i: (nc - 1 - i, 0, 0, 0)),
                  _const_spec((8, SSD_XBC)), _const_spec((1, LANES)),
                  _const_spec((1, LANES)), _const_spec((1, SSD_WIDTH)), _const_spec((1, SSD_WIDTH))],
        out_specs=[rspec(SSD_WIDTH), rspec(SSD_XBC), rspec(LANES),
                   _const_spec((8, SSD_XBC)), _const_spec((1, SSD_XBC)), _const_spec((1, LANES)),
                   _const_spec((1, LANES)), _const_spec((1, LANES)), _const_spec((1, SSD_WIDTH))],
        out_shape=[S((T, SSD_WIDTH), BF), S((T, SSD_XBC), BF), S((T, LANES), BF),
                   S((8, SSD_XBC), F32), S((1, SSD_XBC), F32), S((1, LANES), F32),
                   S((1, LANES), F32), S((1, LANES), F32), S((1, SSD_WIDTH), F32)],
        scratch_shapes=[pltpu.VMEM((SSD_GROUPS, SSD_STATE, GW), F32), pltpu.VMEM((8, SSD_XBC), F32),
                        pltpu.VMEM((CH, SSD_WIDTH), F32), pltpu.VMEM((CH, SSD_WIDTH), F32),
                        pltpu.VMEM((CH, SSD_WIDTH), F32), pltpu.VMEM((CH, SSD_XBC), F32)],
        compiler_params=_cparams("arbitrary"),
    )(dout, y, u, xbc, dtr, z, states, convw, dtb, alog, dskx, nw)


def _log_gamma(h):
    return float(np.log1p(-np.exp2(np.float32(-5.0 - h)), dtype=np.float32))


def _swap_halves(t):
    n = t.shape[1]
    lane = lax.broadcasted_iota(jnp.int32, t.shape, 1)
    return jnp.where((lane & (RET_QK - 1)) < RET_QK // 2, pltpu.roll(t, n - RET_QK // 2, 1), pltpu.roll(t, RET_QK // 2, 1))


def _rot(t, cos, sin):
    return t * cos + _swap_halves(t) * sin


def _rot_t(d, cos, sin):
    return d * cos + _swap_halves(d * sin)


def _ret_tables():
    lg = jnp.asarray([_log_gamma(h) for h in range(RET_HEADS)], F32)[:, None, None]
    pos = jnp.arange(CH, dtype=F32)
    rel = pos[:, None] - pos[None, :]
    dmask = jnp.where(rel >= 0, jnp.exp(lg * jnp.maximum(rel, 0.0)), 0.0)
    kdec = jnp.exp(lg * (CH - 1.0 - pos)[None, :, None])
    qdec = jnp.exp(lg * (pos + 1.0)[None, :, None])
    rows = jnp.concatenate([jnp.swapaxes(kdec, 1, 2), jnp.swapaxes(qdec, 1, 2), jnp.zeros((RET_HEADS, CH - 2, CH), F32)], axis=1)
    full = lambda t: jnp.broadcast_to(t, (RET_HEADS, CH, CH))
    return jnp.stack([dmask, jnp.swapaxes(dmask, 1, 2), full(kdec), full(qdec), full(rows)], axis=1)


def _ret_consts(h, rt_ref):
    kdec = rt_ref[h, 2][:, 0:RET_QK]
    qdec = rt_ref[h, 3][:, 0:RET_QK]
    return rt_ref[h, 0], rt_ref[h, 1], kdec, qdec, rt_ref[h, 4, 0:1, :], rt_ref[h, 4, 1:2, :], math.exp(_log_gamma(h) * CH)


_RT_SPEC = pl.BlockSpec((RET_HEADS, 5, CH, CH), lambda i: (0, 0, 0, 0))


def _ret_fwd(q, k, v, g, cos, sin, nw, rt):
    T = q.shape[0]
    nc = T // CH

    def body(q_ref, k_ref, v_ref, g_ref, cos_ref, sin_ref, nw_ref, rt_ref, out_ref, st_ref, state):
        i = pl.program_id(0)

        @pl.when(i == 0)
        def _():
            state[...] = jnp.zeros_like(state)

        cosf = jnp.tile(cos_ref[...], (1, RET_QK_W // LANES))
        sinf = jnp.tile(sin_ref[...], (1, RET_QK_W // LANES))
        qr = _rot(q_ref[...].astype(F32), cosf, sinf)
        kr = _rot(k_ref[...].astype(F32), cosf, sinf) * (RET_QK ** -0.5)
        krt = kr.T
        st_ref[0] = state[...]
        qsl = [slice(h * RET_QK, (h + 1) * RET_QK) for h in range(RET_HEADS)]
        vsl = [slice(h * RET_V, (h + 1) * RET_V) for h in range(RET_HEADS)]
        consts = [_ret_consts(h, rt_ref) for h in range(RET_HEADS)]
        scores = [_dot_nt(qr[:, qsl[h]], kr[:, qsl[h]]) * consts[h][0] for h in range(RET_HEADS)]
        cross = [_dot(qr[:, qsl[h]] * consts[h][3], state[h]) for h in range(RET_HEADS)]
        kv = [_dot(krt[qsl[h], :] * consts[h][4], v_ref[:, vsl[h]]) for h in range(RET_HEADS)]
        o_all = [_dot(scores[h], v_ref[:, vsl[h]]) + cross[h] for h in range(RET_HEADS)]
        for h in range(RET_HEADS):
            state[h] = state[h] * consts[h][6] + kv[h]
        for h in range(RET_HEADS):
            sl = slice(h * RET_V, (h + 1) * RET_V)
            o = o_all[h]
            gf = g_ref[:, sl].astype(F32)
            out_ref[:, sl] = (o * _rstd(o) * nw_ref[:, sl] * (gf * _sigmoid(gf))).astype(BF)

    return pl.pallas_call(
        body, name="ret_fwd", grid=(nc,),
        in_specs=[_row_spec(CH, RET_QK_W), _row_spec(CH, RET_QK_W), _row_spec(CH, RET_V_W), _row_spec(CH, RET_V_W),
                  _row_spec(CH, LANES), _row_spec(CH, LANES), _const_spec((1, RET_V_W)), _RT_SPEC],
        out_specs=[_row_spec(CH, RET_V_W), pl.BlockSpec((1, RET_HEADS, RET_QK, RET_V), lambda i: (i, 0, 0, 0))],
        out_shape=[S((T, RET_V_W), BF), S((nc, RET_HEADS, RET_QK, RET_V), F32)],
        scratch_shapes=[pltpu.VMEM((RET_HEADS, RET_QK, RET_V), F32)],
        compiler_params=_cparams("arbitrary"),
    )(q, k, v, g, cos, sin, nw, rt)


def _ret_bwd(dout, q, k, v, g, states, cos, sin, nw, rt):
    T = q.shape[0]
    nc = T // CH
    rev = lambda i: (nc - 1 - i, 0)
    rspec = lambda w: pl.BlockSpec((CH, w), rev)

    def body(do_ref, q_ref, k_ref, v_ref, g_ref, st_ref, cos_ref, sin_ref, nw_ref, rt_ref,
             dq_ref, dk_ref, dv_ref, dg_ref, dnw_ref, dstate, dqbuf, dkbuf):
        i = pl.program_id(0)

        @pl.when(i == 0)
        def _():
            dstate[...] = jnp.zeros_like(dstate)
            dnw_ref[...] = jnp.zeros_like(dnw_ref)

        cosf = jnp.tile(cos_ref[...], (1, RET_QK_W // LANES))
        sinf = jnp.tile(sin_ref[...], (1, RET_QK_W // LANES))
        qr = _rot(q_ref[...].astype(F32), cosf, sinf)
        kr = _rot(k_ref[...].astype(F32), cosf, sinf) * (RET_QK ** -0.5)
        qrt = qr.T
        heads = range(RET_HEADS)
        qsl = [slice(h * RET_QK, (h + 1) * RET_QK) for h in heads]
        vsl = [slice(h * RET_V, (h + 1) * RET_V) for h in heads]
        do_all = []
        consts = [_ret_consts(h, rt_ref) for h in heads]
        scores = [_dot_nt(qr[:, qsl[h]], kr[:, qsl[h]]) * consts[h][0] for h in heads]
        scores_t = [_dot_nt(kr[:, qsl[h]], qr[:, qsl[h]]) * consts[h][1] for h in heads]
        cross = [_dot(qr[:, qsl[h]] * consts[h][3], st_ref[0, h]) for h in heads]
        o_all = [_dot(scores[h], v_ref[:, vsl[h]]) + cross[h] for h in heads]
        for h in heads:
            o = o_all[h]
            rr = _rstd(o)
            of = o * rr
            gf = g_ref[:, vsl[h]].astype(F32)
            sgg = _sigmoid(gf)
            d_h = do_ref[:, vsl[h]].astype(F32)
            nw_h = nw_ref[:, vsl[h]]
            dg_ref[:, vsl[h]] = (d_h * of * nw_h * _dsilu(gf, sgg)).astype(BF)
            dt_ = d_h * (gf * sgg)
            dnw_ref[:, vsl[h]] += jnp.sum(dt_ * of, axis=0, keepdims=True)
            dof = dt_ * nw_h
            do_all.append(rr * dof - o * (rr * rr * rr) * jnp.mean(dof * o, axis=-1, keepdims=True))
        dsc = [_dot_nt(do_all[h], v_ref[:, vsl[h]]) * consts[h][0] for h in heads]
        dsc_t = [_dot_nt(v_ref[:, vsl[h]], do_all[h]) * consts[h][1] for h in heads]
        dv_a = [_dot(scores_t[h], do_all[h]) for h in heads]
        dv_b = [_dot(kr[:, qsl[h]] * consts[h][2], dstate[h]) for h in heads]
        dq_a = [_dot(dsc[h], kr[:, qsl[h]]) for h in heads]
        dq_b = [_dot_nt(do_all[h], st_ref[0, h]) * consts[h][3] for h in heads]
        dk_a = [_dot(dsc_t[h], qr[:, qsl[h]]) for h in heads]
        dk_b = [_dot_nt(v_ref[:, vsl[h]], dstate[h]) * consts[h][2] for h in heads]
        dst = [_dot(qrt[qsl[h], :] * consts[h][5], do_all[h]) for h in heads]
        for h in heads:
            dv_ref[:, vsl[h]] = (dv_a[h] + dv_b[h]).astype(BF)
            dqbuf[:, qsl[h]] = dq_a[h] + dq_b[h]
            dkbuf[:, qsl[h]] = dk_a[h] + dk_b[h]
            dstate[h] = dstate[h] * consts[h][6] + dst[h]
        dq_ref[...] = _rot_t(dqbuf[...], cosf, sinf).astype(BF)
        dk_ref[...] = (_rot_t(dkbuf[...], cosf, sinf) * (RET_QK ** -0.5)).astype(BF)

    return pl.pallas_call(
        body, name="ret_bwd", grid=(nc,),
        in_specs=[rspec(RET_V_W), rspec(RET_QK_W), rspec(RET_QK_W), rspec(RET_V_W), rspec(RET_V_W),
                  pl.BlockSpec((1, RET_HEADS, RET_QK, RET_V), lambda i: (nc - 1 - i, 0, 0, 0)),
                  rspec(LANES), rspec(LANES), _const_spec((1, RET_V_W)), _RT_SPEC],
        out_specs=[rspec(RET_QK_W), rspec(RET_QK_W), rspec(RET_V_W), rspec(RET_V_W), _const_spec((1, RET_V_W))],
        out_shape=[S((T, RET_QK_W), BF), S((T, RET_QK_W), BF), S((T, RET_V_W), BF), S((T, RET_V_W), BF),
                   S((1, RET_V_W), F32)],
        scratch_shapes=[pltpu.VMEM((RET_HEADS, RET_QK, RET_V), F32), pltpu.VMEM((CH, RET_QK_W), F32),
                        pltpu.VMEM((CH, RET_QK_W), F32)],
        compiler_params=_cparams("arbitrary"),
    )(dout, q, k, v, g, states, cos, sin, nw, rt)


_HBM = pl.BlockSpec(memory_space=pltpu.HBM)
_SEM = pl.BlockSpec(memory_space=pltpu.SEMAPHORE)
_EFFECT = pltpu.SideEffectType.DATAFLOW_SIDE_EFFECTING


ALL_PEERS = tuple(range(1, N_DEV))
ONE_PER_CHIP = (1, 2, 4, 6)
OTHER_CHIPS = (2, 4, 6)


def _split_copies(buf_refs, land_refs, send_sems, recv_sems, same, to_me, ks):
    x, y, c = lax.axis_index("x"), lax.axis_index("y"), lax.axis_index("c")
    me = 4 * x + 2 * y + c
    cps = []
    for ki, k in enumerate(ks):
        px = 1 - x if k & 4 else x
        py = 1 - y if k & 2 else y
        pc = 1 - c if k & 1 else c
        p = 4 * px + 2 * py + pc
        for b in range(len(buf_refs)):
            s = b * len(ks) + ki
            cps.append(pltpu.make_async_remote_copy(
                src_ref=buf_refs[b] if same else buf_refs[b].at[p], dst_ref=land_refs[b].at[me if to_me else p],
                send_sem=send_sems.at[s], recv_sem=recv_sems.at[s], device_id=(px, py, pc), device_id_type=pl.DeviceIdType.MESH))
    return cps


def _exchange_start(bufs, name, same, ks=ALL_PEERS):
    nb = len(bufs)
    ns = nb * len(ks)
    lands = [lax.empty((N_DEV,) + tuple(b.shape if same else b.shape[1:]), b.dtype) for b in bufs]

    def body(*refs):
        buf_refs, land_refs = refs[:nb], refs[nb:2 * nb]
        send_sems, recv_sems = refs[2 * nb], refs[2 * nb + 1]
        token = refs[-1]
        for cp in _split_copies(buf_refs, land_refs, send_sems, recv_sems, same, True, ks):
            cp.start()
        token[...] = jnp.zeros_like(token)

    hbm = lambda a: pltpu.with_memory_space_constraint(a, pltpu.HBM)
    out = pl.pallas_call(
        body, name=name,
        out_shape=(pltpu.SemaphoreType.DMA((ns,)), pltpu.SemaphoreType.DMA((ns,)),
                   *[pltpu.HBM(a.shape, a.dtype) for a in list(bufs) + lands], S((8, LANES), F32)),
        in_specs=[_HBM] * (2 * nb), out_specs=(_SEM, _SEM, *[_HBM] * (2 * nb), pl.BlockSpec(memory_space=pltpu.VMEM)),
        input_output_aliases={i: 2 + i for i in range(2 * nb)},
        compiler_params=pltpu.CompilerParams(has_side_effects=_EFFECT),
    )(*[hbm(a) for a in list(bufs) + lands])
    return out[0], out[1], list(out[2:2 + nb]), list(out[2 + nb:2 + 2 * nb]), out[-1]


def _exchange_wait(started, after, name, same, ks=ALL_PEERS):
    send_sems, recv_sems, bufs, lands, _ = started
    nb = len(bufs)
    after = list(after) if isinstance(after, (list, tuple)) else [after]

    def body(*refs):
        buf_refs, land_refs = refs[:nb], refs[nb:2 * nb]
        s_sems, r_sems = refs[2 * nb], refs[2 * nb + 1]
        for cp in _split_copies(buf_refs, land_refs, s_sems, r_sems, same, False, ks):
            cp.wait_send()
            cp.wait_recv()

    out = pl.pallas_call(
        body, name=name,
        out_shape=tuple(pltpu.HBM(a.shape, a.dtype) for a in bufs + lands),
        in_specs=[_HBM] * (2 * nb) + [_SEM, _SEM] + [pl.BlockSpec(memory_space=pl.ANY)] * len(after),
        out_specs=tuple([_HBM] * (2 * nb)),
        input_output_aliases={i: i for i in range(2 * nb)},
        compiler_params=pltpu.CompilerParams(has_side_effects=_EFFECT),
    )(*bufs, *lands, send_sems, recv_sems, *after)
    return list(out[:nb]), list(out[nb:])


def _forward_copies(land_ref, send_sems, recv_sems, sending):
    x, y, c = lax.axis_index("x"), lax.axis_index("y"), lax.axis_index("c")
    cps = []
    for ki, k in enumerate(OTHER_CHIPS):
        px = 1 - x if k & 4 else x
        py = 1 - y if k & 2 else y
        q = 4 * px + 2 * py + (c if sending else 1 - c)
        cps.append(pltpu.make_async_remote_copy(
            src_ref=land_ref.at[q], dst_ref=land_ref.at[q], send_sem=send_sems.at[ki], recv_sem=recv_sems.at[ki],
            device_id=(x, y, 1 - c), device_id_type=pl.DeviceIdType.MESH))
    return cps


def _forward_start(land, name):
    def body(land_ref, send_sems, recv_sems, land_thru, token):
        for cp in _forward_copies(land_ref, send_sems, recv_sems, True):
            cp.start()
        token[...] = jnp.zeros_like(token)

    n = len(OTHER_CHIPS)
    out = pl.pallas_call(
        body, name=name,
        out_shape=(pltpu.SemaphoreType.DMA((n,)), pltpu.SemaphoreType.DMA((n,)), pltpu.HBM(land.shape, land.dtype),
                   S((8, LANES), F32)),
        in_specs=[_HBM], out_specs=(_SEM, _SEM, _HBM, pl.BlockSpec(memory_space=pltpu.VMEM)),
        input_output_aliases={0: 2},
        compiler_params=pltpu.CompilerParams(has_side_effects=_EFFECT),
    )(pltpu.with_memory_space_constraint(land, pltpu.HBM))
    return out


def _forward_wait(started, after, name):
    send_sems, recv_sems, land, _ = started
    after = list(after) if isinstance(after, (list, tuple)) else [after]

    def body(land_ref, s_sems, r_sems, *rest):
        for cp in _forward_copies(land_ref, s_sems, r_sems, False):
            cp.wait_send()
            cp.wait_recv()

    return pl.pallas_call(
        body, name=name, out_shape=pltpu.HBM(land.shape, land.dtype),
        in_specs=[_HBM, _SEM, _SEM] + [pl.BlockSpec(memory_space=pl.ANY)] * len(after), out_specs=_HBM,
        input_output_aliases={0: 0},
        compiler_params=pltpu.CompilerParams(has_side_effects=_EFFECT),
    )(land, send_sems, recv_sems, *after)


def _sum_slabs(recv, name):
    n, R, _ = recv.shape

    def body(r_ref, o_ref):
        g = r_ref[0].astype(F32)
        for s in range(1, n):
            g = g + r_ref[s].astype(F32)
        o_ref[...] = g

    return pl.pallas_call(body, name=name, out_shape=S((R, LANES), F32))(recv)


def _adamw(recv, w, m, v, name, tr, tc=None):
    n, R, C = recv.shape
    c1 = 1.0 - ADAM_B1 ** ADAM_STEP
    c2 = 1.0 - ADAM_B2 ** ADAM_STEP

    def body(r_ref, w_ref, m_ref, v_ref, g_out, d_out, m_out, v_out):
        g = r_ref[0].astype(F32)
        for s in range(1, n):
            g = g + r_ref[s].astype(F32)
        mm = ADAM_B1 * m_ref[...] + (1.0 - ADAM_B1) * g
        vv = ADAM_B2 * v_ref[...] + (1.0 - ADAM_B2) * (g * g)
        g_out[...] = g
        m_out[...] = mm
        v_out[...] = vv
        d_out[...] = -ADAM_LR * ((mm / c1) / (jnp.sqrt(vv / c2) + ADAM_EPS) + ADAM_WD * w_ref[...])

    tc = C if tc is None else tc
    spec = pl.BlockSpec((tr, tc), lambda i, j: (i, j))
    return pl.pallas_call(
        body, name=name, grid=(R // tr, C // tc),
        in_specs=[pl.BlockSpec((n, tr, tc), lambda i, j: (0, i, j)), spec, spec, spec],
        out_specs=[spec] * 4, out_shape=[S((R, C), F32)] * 4,
        compiler_params=_cparams("parallel", "parallel"),
    )(recv, w, m, v)


def _pack(parts, rows):
    cols = []
    for p in parts:
        f = p.reshape(-1)
        cols.append(jnp.pad(f, (0, (-f.shape[0]) % LANES)))
    flat = jnp.concatenate(cols)
    return jnp.pad(flat, (0, rows * LANES - flat.shape[0])).reshape(rows, LANES)


def _unpack(buf, shapes):
    flat = buf.reshape(-1)
    out, o = [], 0
    for shp in shapes:
        n = int(np.prod(shp))
        out.append(flat[o:o + n].reshape(shp))
        o += n + (-n) % LANES
    return out


SMALL_ROWS = 200
CONV_ROWS = 16


def kernel(x, pre_mix_norm_w, w_in, ssd_conv_w, ssd_conv_b, ssd_dt_bias, ssd_a_log, ssd_d, ssd_norm_w, ret_norm_w, w_out, post_mix_norm_w, pre_ffn_norm_w, w_up, ffn_conv_w, ffn_conv_b, w_down, post_ffn_norm_w, loss_target, m_pre_mix_norm_w, m_w_in, m_ssd_conv_w, m_ssd_conv_b, m_ssd_dt_bias, m_ssd_a_log, m_ssd_d, m_ssd_norm_w, m_ret_norm_w, m_w_out, m_post_mix_norm_w, m_pre_ffn_norm_w, m_w_up, m_ffn_conv_w, m_ffn_conv_b, m_w_down, m_post_ffn_norm_w, v_pre_mix_norm_w, v_w_in, v_ssd_conv_w, v_ssd_conv_b, v_ssd_dt_bias, v_ssd_a_log, v_ssd_d, v_ssd_norm_w, v_ret_norm_w, v_w_out, v_post_mix_norm_w, v_pre_ffn_norm_w, v_w_up, v_ffn_conv_w, v_ffn_conv_b, v_w_down, v_post_ffn_norm_w):
    T = x.shape[1]
    xi, tgt = x[0], loss_target[0]
    me = 4 * lax.axis_index("x") + 2 * lax.axis_index("y") + lax.axis_index("c")
    n_in, n_up = w_in.shape[2], w_up.shape[2]
    n_out, n_down = w_out.shape[1], w_down.shape[1]
    n_sc, n_fc = ssd_conv_w.shape[2], ffn_conv_w.shape[2]

    def after(token, value):
        return value * (1.0 + token[0, 0])

    def finish(started, after_value, name, same):
        bufs, lands = _exchange_wait(started, after_value, name, same)
        own = [b if same else lax.dynamic_index_in_dim(b, me, 0, keepdims=False) for b in bufs]
        return [lax.dynamic_update_index_in_dim(l, o, me, 0) for l, o in zip(lands, own)]

    tr_ = lambda w: jnp.transpose(w[0])
    gat_in = _exchange_start([tr_(w_in).astype(BF)], "gather_in_start", True, ONE_PER_CHIP)
    gat_conv = _exchange_start([after(gat_in[4], _pack([ssd_conv_w, ffn_conv_w], CONV_ROWS))], "gather_conv_start", True)
    pad_h = lambda p: jnp.pad(p, ((0, 0), (0, LANES - SSD_HEADS)))
    dtb, alog = pad_h(ssd_dt_bias), pad_h(ssd_a_log)
    dskx = jnp.repeat(ssd_d, SSD_HEAD_DIM, axis=1)
    inv = ROPE_BASE ** (-jnp.arange(0, RET_QK, 2, dtype=F32) / RET_QK)
    ang = after(gat_conv[4], jnp.arange(T, dtype=F32)[:, None]) * inv[None, :]
    cs_, sn_ = jnp.cos(ang), jnp.sin(ang)
    cos = jnp.concatenate([cs_, cs_, cs_, cs_], axis=1)
    sin = jnp.concatenate([-sn_, sn_, -sn_, sn_], axis=1)
    rtab = _ret_tables()
    ws = [pre_mix_norm_w, ssd_conv_w, ssd_conv_b, ssd_dt_bias, ssd_a_log, ssd_d, ssd_norm_w, ret_norm_w, post_mix_norm_w,
          pre_ffn_norm_w, ffn_conv_w, ffn_conv_b, post_ffn_norm_w]
    ms = [m_pre_mix_norm_w, m_ssd_conv_w, m_ssd_conv_b, m_ssd_dt_bias, m_ssd_a_log, m_ssd_d, m_ssd_norm_w, m_ret_norm_w,
          m_post_mix_norm_w, m_pre_ffn_norm_w, m_ffn_conv_w, m_ffn_conv_b, m_post_ffn_norm_w]
    vs = [v_pre_mix_norm_w, v_ssd_conv_w, v_ssd_conv_b, v_ssd_dt_bias, v_ssd_a_log, v_ssd_d, v_ssd_norm_w, v_ret_norm_w,
          v_post_mix_norm_w, v_pre_ffn_norm_w, v_ffn_conv_w, v_ffn_conv_b, v_post_ffn_norm_w]
    out_shapes = [t.shape for t in ws]
    small_wmv = [_pack(t, SMALL_ROWS) for t in (ws, ms, vs)]
    shard_in, land_in = _exchange_wait(gat_in, [cos, sin, rtab] + small_wmv, "gather_in_wait", True, ONE_PER_CHIP)
    fwd_in_ = _forward_start(land_in[0], "gather_in_forward")
    gat_rest = _exchange_start([after(fwd_in_[3], w).astype(BF) for w in (w_out[0], tr_(w_up), w_down[0])], "gather_rest_start", True)
    g_in = lax.dynamic_update_index_in_dim(_forward_wait(fwd_in_, gat_rest[4], "gather_in_forward_wait"), shard_in[0], me, 0)
    wt = g_in.reshape(N_DEV * n_in, D_MODEL)
    wdt = jnp.pad(wt[O_DT:O_Q], ((0, LANES - SSD_HEADS), (0, 0)))

    h, z, xbc, q, k, v, g, dtr = _fwd_in(xi, pre_mix_norm_w, wt, wdt)
    yr, rst = _ret_fwd(q, k, v, g, cos, sin, ret_norm_w, rtab)
    gconv, = finish(gat_conv, yr, "gather_conv_wait", True)
    convs = [_unpack(gconv[d], [(SSD_CONV, n_sc), (FFN_CONV, n_fc)]) for d in range(N_DEV)]
    scw = jnp.pad(jnp.concatenate([c[0] for c in convs], axis=1), ((0, 8 - SSD_CONV), (0, 0)))
    fcw = jnp.pad(jnp.concatenate([c[1] for c in convs], axis=1), ((0, 8 - FFN_CONV), (0, 0)))
    ys, ypre, uconv, sst = _ssd_fwd(xbc, dtr, z, scw, ssd_conv_b, dtb, alog, dskx, ssd_norm_w)
    g_out, g_up, g_down = finish(gat_rest, [yr, ys], "gather_rest_wait", True)
    wout = g_out.reshape(N_DEV * n_out, D_MODEL)
    wup = g_up.reshape(N_DEV * n_up, D_MODEL)
    wdown = g_down.reshape(N_DEV * n_down, D_MODEL)
    y, x1, h2, graw, val = _fwd_mid(ys, yr, xi, wout, post_mix_norm_w, pre_ffn_norm_w, wup)
    a, dfb, dval, dgate, dx2, lossb, d_pff, d_fcb = _ffn_tail(graw, val, x1, tgt, fcw, ffn_conv_b, wdown, post_ffn_norm_w)
    gdown = _matmul_tn(a, dfb, "dw_down")
    sc_down = _exchange_start([gdown.reshape(N_DEV, n_down, D_MODEL)], "scatter_down_start", False)
    dgraw, dx1, dyb, dys, dyr, d_fcw, d_pf, d_pm = _ffn_bwd(dgate, dval, graw, x1, dx2, y, after(sc_down[4], fcw), wup,
                                                         pre_ffn_norm_w, post_mix_norm_w, wout)
    gup = jnp.concatenate([_matmul_tn(dgraw, h2, "dw_up_g"), _matmul_tn(dval, h2, "dw_up_v")], axis=0)
    gout = jnp.concatenate(_matmul_tn_group([ys, yr], dyb, "dw_out"), axis=0)
    sc_mid = _exchange_start([gup.reshape(N_DEV, n_up, D_MODEL), gout.reshape(N_DEV, n_out, D_MODEL)],
                             "scatter_mid_start", False)
    dz, dxbc, ddt, d_scw, d_scb, d_dtb, d_alog, d_dsk, d_snw = _ssd_bwd(dys, ypre, uconv, xbc, dtr, z, sst, after(sc_mid[4], scw),
                                                                      dtb, alog, dskx, ssd_norm_w)
    dq, dk, dv, dg, d_rnw = _ret_bwd(dyr, q, k, v, g, rst, cos, sin, ret_norm_w, rtab)
    g_q, g_k, g_v, g_g = _matmul_tn_group([dq, dk, dv, dg], h, "dw_ret")
    g_z, g_xbc, g_dt = _matmul_tn_group([dz, dxbc, ddt], h, "dw_ssd")
    gin = jnp.concatenate([g_z, g_xbc, g_dt[:SSD_HEADS], g_q, g_k, g_v, g_g], axis=0)
    sc_in = _exchange_start([gin.reshape(N_DEV, n_in, D_MODEL)], "scatter_in_start", False)
    gx, d_w0 = _in_bwd(dz, dxbc, dq, dk, dv, dg, ddt, xi, dx1, after(sc_in[4], pre_mix_norm_w), wt, wdt)
    small_full = [d_w0, d_scw[:SSD_CONV], d_scb, d_dtb[:, :SSD_HEADS], d_alog[:, :SSD_HEADS], d_dsk[:, :SSD_HEADS], d_snw, d_rnw,
                  d_pm, d_pf, d_fcw[:FFN_CONV], d_fcb, d_pff, lossb[0:1, 0:1]]
    gat_small = _exchange_start([_pack(small_full, SMALL_ROWS)], "gather_small_start", True)
    r_down, = finish(sc_down, [gx, gat_small[4]], "scatter_down_wait", False)
    r_up, r_out = finish(sc_mid, r_down, "scatter_mid_wait", False)
    per_w = [None] * 4
    per_w[3] = _adamw(r_down, w_down[0], m_w_down[0], v_w_down[0], "adamw_down", n_down)
    per_w[2] = [jnp.transpose(t) for t in _adamw(r_up, tr_(w_up), tr_(m_w_up), tr_(v_w_up), "adamw_up", n_up, 256)]
    per_w[1] = _adamw(r_out, w_out[0], m_w_out[0], v_w_out[0], "adamw_out", n_out)
    r_in, = finish(sc_in, per_w[1][0], "scatter_in_wait", False)
    per_w[0] = [jnp.transpose(t) for t in _adamw(r_in, tr_(w_in), tr_(m_w_in), tr_(v_w_in), "adamw_in", n_in, 256)]
    big = [[per_w[i][kind][None] for i in range(4)] for kind in range(4)]

    full_shapes = [t.shape for t in small_full]
    gs = _sum_slabs(finish(gat_small, per_w[0][0], "gather_small_wait", True)[0], "sum_small")
    gfull = _unpack(gs, full_shapes)
    gfull[1] = lax.dynamic_slice_in_dim(gfull[1], me * n_sc, n_sc, axis=1)
    gfull[10] = lax.dynamic_slice_in_dim(gfull[10], me * n_fc, n_fc, axis=1)
    loss = gfull.pop()[0, 0]
    small = _adamw(_pack(gfull, SMALL_ROWS)[None], *small_wmv, "adamw_small", SMALL_ROWS)
    small = [_unpack(b, out_shapes) for b in small]

    order = {"pre_mix_norm_w": ("s", 0), "w_in": ("b", 0), "ssd_conv_w": ("s", 1), "ssd_conv_b": ("s", 2),
             "ssd_dt_bias": ("s", 3), "ssd_a_log": ("s", 4), "ssd_d": ("s", 5), "ssd_norm_w": ("s", 6), "ret_norm_w": ("s", 7),
             "w_out": ("b", 1), "post_mix_norm_w": ("s", 8), "pre_ffn_norm_w": ("s", 9), "w_up": ("b", 2),
             "ffn_conv_w": ("s", 10), "ffn_conv_b": ("s", 11), "w_down": ("b", 3), "post_ffn_norm_w": ("s", 12)}
    outs = [loss, gx[None]]
    for kind in range(4):
        for name, (grp, idx) in order.items():
            outs.append(big[kind][idx] if grp == "b" else small[kind][idx])
    return tuple(outs)
```

```python
import functools
import math

import numpy as np
import jax
import jax.numpy as jnp
from jax import lax
from jax.experimental import pallas as pl
from jax.experimental.pallas import tpu as pltpu

F32 = jnp.float32
BF = jnp.bfloat16
HI = lax.Precision.HIGHEST
S = jax.ShapeDtypeStruct

D_MODEL = 1024
SSD_HEADS = 16
SSD_HEAD_DIM = 64
SSD_GROUPS = 2
SSD_STATE = 128
SSD_WIDTH = 1024
SSD_XBC = 1536
SSD_CONV = 4
RET_HEADS = 8
RET_QK = 64
RET_V = 128
RET_QK_W = 512
RET_V_W = 1024
ROPE_BASE = 10000.0
CH = 128
D_FF = 2816
FFN_CONV = 3
EPS = 1e-6
IN_WIDTH = 5648
N_DEV = 8

ADAM_LR = 0.001
ADAM_B1 = 0.9
ADAM_B2 = 0.999
ADAM_EPS = 1e-08
ADAM_WD = 0.01
ADAM_STEP = 10

LANES = 128
HALO = 16
VMEM_LIMIT = 48 * 1024 * 1024

O_Z, O_XBC, O_DT, O_Q, O_K, O_V, O_G, O_END = 0, 1024, 2560, 2576, 3088, 3600, 4624, 5648
IN_SEGMENTS = ((O_Z, O_XBC), (O_XBC, O_DT), (O_Q, O_K), (O_K, O_V), (O_V, O_G), (O_G, O_END))


def _cparams(*sem):
    return pltpu.CompilerParams(dimension_semantics=sem, vmem_limit_bytes=VMEM_LIMIT)


def _dot(a, b):
    return jnp.dot(a.astype(BF), b.astype(BF), preferred_element_type=F32)


def _dot_nt(a, b):
    return lax.dot_general(a.astype(BF), b.astype(BF), (((1,), (1,)), ((), ())), preferred_element_type=F32)


def _dot_tn(a, b):
    return lax.dot_general(a.astype(BF), b.astype(BF), (((0,), (0,)), ((), ())), preferred_element_type=F32)


def _dot_hi(a, b):
    return jnp.dot(a, b, preferred_element_type=F32, precision=HI)


def _dot_tn_hi(a, b):
    return lax.dot_general(a, b, (((0,), (0,)), ((), ())), preferred_element_type=F32, precision=HI)


def _sigmoid(x):
    return jax.nn.sigmoid(x)


def _dsilu(x, s):
    return s * (1.0 + x * (1.0 - s))


def _softplus(x):
    return jnp.maximum(x, 0.0) + jnp.log1p(jnp.exp(-jnp.abs(x)))


def _rstd(x):
    return lax.rsqrt(jnp.mean(x * x, axis=-1, keepdims=True) + EPS)


def _rms_bwd(dy, x, r, w):
    gn = dy * w
    dx = r * gn - x * (r * r * r) * jnp.mean(gn * x, axis=-1, keepdims=True)
    dw = jnp.sum(dy * x * r, axis=0, keepdims=True)
    return dx, dw


def _rows_before(ext, s, head, n):
    if s == 0:
        return ext[head:head + n]
    return pltpu.roll(ext, s, 0)[head:head + n]


def _rows_after(ext, s, n):
    if s == 0:
        return ext[0:n]
    return pltpu.roll(ext, ext.shape[0] - s, 0)[0:n]


def _row_spec(tm, width):
    return pl.BlockSpec((tm, width), lambda i: (i, 0))


def _const_spec(shape):
    return pl.BlockSpec(shape, lambda i: (0,) * len(shape))


_VMEM_WHOLE = pl.BlockSpec(memory_space=pltpu.VMEM)


def _fwd_in(x, w0, wt, wdt, tm=512):
    T = x.shape[0]

    def body(x_ref, w0_ref, wt_ref, wdt_ref, h_ref, z_ref, xbc_ref, q_ref, k_ref, v_ref, g_ref, dt_ref):
        xf = x_ref[...]
        h = (xf * _rstd(xf) * w0_ref[...]).astype(BF)
        h_ref[...] = h
        for ref, (lo, hi) in zip((z_ref, xbc_ref, q_ref, k_ref, v_ref, g_ref), IN_SEGMENTS):
            ref[...] = _dot_nt(h, wt_ref[lo:hi, :]).astype(ref.dtype)
        dt_ref[...] = _dot_nt(h, wdt_ref[...])

    widths = (D_MODEL, SSD_WIDTH, SSD_XBC, RET_QK_W, RET_QK_W, RET_V_W, RET_V_W)
    return pl.pallas_call(
        body, name="fwd_in", grid=(T // tm,),
        in_specs=[_row_spec(tm, D_MODEL), _const_spec((1, D_MODEL)), _VMEM_WHOLE, _VMEM_WHOLE],
        out_specs=[_row_spec(tm, w) for w in widths] + [_row_spec(tm, LANES)],
        out_shape=[S((T, w), BF) for w in widths] + [S((T, LANES), F32)],
        compiler_params=_cparams("parallel"),
    )(x, w0, wt, wdt)


def _fwd_mid(ys, yr, x, wout, wpm, wpf, wup, tm=512):
    T = x.shape[0]

    def body(ys_ref, yr_ref, x_ref, wout_ref, wpm_ref, wpf_ref, wup_ref, y_ref, x1_ref, h2_ref, graw_ref, val_ref):
        y = (jnp.dot(ys_ref[...], wout_ref[0:SSD_WIDTH, :], preferred_element_type=F32)
             + jnp.dot(yr_ref[...], wout_ref[SSD_WIDTH:, :], preferred_element_type=F32))
        y_ref[...] = y
        x1 = x_ref[...] + y * _rstd(y) * wpm_ref[...]
        x1_ref[...] = x1
        h2 = (x1 * _rstd(x1) * wpf_ref[...]).astype(BF)
        h2_ref[...] = h2
        graw_ref[...] = _dot_nt(h2, wup_ref[0:D_FF, :]).astype(BF)
        val_ref[...] = _dot_nt(h2, wup_ref[D_FF:, :]).astype(BF)

    return pl.pallas_call(
        body, name="fwd_mid", grid=(T // tm,),
        in_specs=[_row_spec(tm, SSD_WIDTH), _row_spec(tm, RET_V_W), _row_spec(tm, D_MODEL), _VMEM_WHOLE,
                  _const_spec((1, D_MODEL)), _const_spec((1, D_MODEL)), _VMEM_WHOLE],
        out_specs=[_row_spec(tm, D_MODEL), _row_spec(tm, D_MODEL), _row_spec(tm, D_MODEL), _row_spec(tm, D_FF),
                   _row_spec(tm, D_FF)],
        out_shape=[S((T, D_MODEL), F32), S((T, D_MODEL), F32), S((T, D_MODEL), BF), S((T, D_FF), BF), S((T, D_FF), BF)],
        compiler_params=_cparams("parallel"),
    )(ys, yr, x, wout, wpm, wpf, wup)


def _ffn_tail(graw, val, x1, tgt, convw, convb, wdown, wpff, tm=256):
    T = x1.shape[0]

    def body(graw_ref, val_ref, x1_ref, tgt_ref, cw_ref, cb_ref, wd_ref, wpff_ref,
             a_ref, df_ref, dval_ref, dgate_ref, dx2_ref, loss_ref, dwpff_ref, dcb_ref, carry):
        i = pl.program_id(0)

        @pl.when(i == 0)
        def _():
            carry[...] = jnp.zeros_like(carry)
            loss_ref[...] = jnp.zeros_like(loss_ref)
            dwpff_ref[...] = jnp.zeros_like(dwpff_ref)
            dcb_ref[...] = jnp.zeros_like(dcb_ref)

        g = graw_ref[...].astype(F32)
        ext = jnp.concatenate([carry[...], g], axis=0)
        carry[...] = g[tm - 8:tm]
        gate = cb_ref[...] + sum(cw_ref[j:j + 1, :] * _rows_before(ext, FFN_CONV - 1 - j, 8, tm) for j in range(FFN_CONV))
        sg = _sigmoid(gate)
        silu = gate * sg
        v = val_ref[...].astype(F32)
        a = (silu * v).astype(BF)
        a_ref[...] = a
        f = jnp.dot(a, wd_ref[...], preferred_element_type=F32)
        r = _rstd(f)
        w = wpff_ref[...]
        e = x1_ref[...] + f * r * w - tgt_ref[...]
        loss_ref[...] += jnp.sum(e * e) * (0.5 / D_MODEL)
        dx2 = e * (1.0 / D_MODEL)
        dx2_ref[...] = dx2
        df, dw = _rms_bwd(dx2, f, r, w)
        dwpff_ref[...] += dw
        dfb = df.astype(BF)
        df_ref[...] = dfb
        da = _dot_nt(dfb, wd_ref[...])
        dval_ref[...] = (da * silu).astype(BF)
        dgate = da * v * _dsilu(gate, sg)
        dcb_ref[...] += jnp.sum(dgate, axis=0, keepdims=True)
        dgate_ref[...] = dgate.astype(BF)

    return pl.pallas_call(
        body, name="ffn_tail", grid=(T // tm,),
        in_specs=[_row_spec(tm, D_FF), _row_spec(tm, D_FF), _row_spec(tm, D_MODEL), _row_spec(tm, D_MODEL),
                  _const_spec((8, D_FF)), _const_spec((1, D_FF)), _VMEM_WHOLE, _const_spec((1, D_MODEL))],
        out_specs=[_row_spec(tm, D_FF), _row_spec(tm, D_MODEL), _row_spec(tm, D_FF), _row_spec(tm, D_FF),
                   _row_spec(tm, D_MODEL), _const_spec((8, LANES)), _const_spec((1, D_MODEL)), _const_spec((1, D_FF))],
        out_shape=[S((T, D_FF), BF), S((T, D_MODEL), BF), S((T, D_FF), BF), S((T, D_FF), BF), S((T, D_MODEL), F32),
                   S((8, LANES), F32), S((1, D_MODEL), F32), S((1, D_FF), F32)],
        scratch_shapes=[pltpu.VMEM((8, D_FF), F32)],
        compiler_params=_cparams("arbitrary"),
    )(graw, val, x1, tgt, convw, convb, wdown, wpff)


def _ffn_bwd(dgate, dval, graw, x1, dx2, y, convw, wup, wpf, wpm, wout, tm=256):
    T = x1.shape[0]
    nt = T // tm
    rev = lambda i: (nt - 1 - i, 0)
    rspec = lambda w: pl.BlockSpec((tm, w), rev)

    def body(dgate_ref, dval_ref, graw_ref, x1_ref, dx2_ref, y_ref, cw_ref, wup_ref, wpf_ref, wpm_ref, wout_ref,
             dgraw_ref, dx1_ref, dy_ref, dys_ref, dyr_ref, dcw_ref, dwpf_ref, dwpm_ref, carry):
        i = pl.program_id(0)

        @pl.when(i == 0)
        def _():
            carry[...] = jnp.zeros_like(carry)
            dcw_ref[...] = jnp.zeros_like(dcw_ref)
            dwpf_ref[...] = jnp.zeros_like(dwpf_ref)
            dwpm_ref[...] = jnp.zeros_like(dwpm_ref)

        dg = dgate_ref[...].astype(F32)
        ext = jnp.concatenate([dg, carry[...]], axis=0)
        carry[...] = dg[0:8]
        g = graw_ref[...].astype(F32)
        dgraw = jnp.zeros((tm, D_FF), F32)
        for j in range(FFN_CONV):
            sj = _rows_after(ext, FFN_CONV - 1 - j, tm)
            dgraw = dgraw + cw_ref[j:j + 1, :] * sj
            dcw_ref[j:j + 1, :] += jnp.sum(sj * g, axis=0, keepdims=True)
        dgrawb = dgraw.astype(BF)
        dgraw_ref[...] = dgrawb
        dh2 = _dot(dgrawb, wup_ref[0:D_FF, :]) + _dot(dval_ref[...], wup_ref[D_FF:, :])
        x1 = x1_ref[...]
        dxa, dw = _rms_bwd(dh2, x1, _rstd(x1), wpf_ref[...])
        dwpf_ref[...] += dw
        dx1 = dx2_ref[...] + dxa
        dx1_ref[...] = dx1
        yv = y_ref[...]
        dy, dw = _rms_bwd(dx1, yv, _rstd(yv), wpm_ref[...])
        dwpm_ref[...] += dw
        dyb = dy.astype(BF)
        dy_ref[...] = dyb
        dys_ref[...] = _dot_nt(dyb, wout_ref[0:SSD_WIDTH, :]).astype(BF)
        dyr_ref[...] = _dot_nt(dyb, wout_ref[SSD_WIDTH:, :]).astype(BF)

    return pl.pallas_call(
        body, name="ffn_bwd", grid=(nt,),
        in_specs=[rspec(D_FF), rspec(D_FF), rspec(D_FF), rspec(D_MODEL), rspec(D_MODEL), rspec(D_MODEL),
                  _const_spec((8, D_FF)), _VMEM_WHOLE, _const_spec((1, D_MODEL)), _const_spec((1, D_MODEL)), _VMEM_WHOLE],
        out_specs=[rspec(D_FF), rspec(D_MODEL), rspec(D_MODEL), rspec(SSD_WIDTH), rspec(RET_V_W),
                   _const_spec((8, D_FF)), _const_spec((1, D_MODEL)), _const_spec((1, D_MODEL))],
        out_shape=[S((T, D_FF), BF), S((T, D_MODEL), F32), S((T, D_MODEL), BF), S((T, SSD_WIDTH), BF), S((T, RET_V_W), BF),
                   S((8, D_FF), F32), S((1, D_MODEL), F32), S((1, D_MODEL), F32)],
        scratch_shapes=[pltpu.VMEM((8, D_FF), F32)],
        compiler_params=_cparams("arbitrary"),
    )(dgate, dval, graw, x1, dx2, y, convw, wup, wpf, wpm, wout)


def _in_bwd(dz, dxbc, dq, dk, dv, dg, ddt, x, dx1, w0, wt, wdt, tm=512):
    T = x.shape[0]

    def body(dz_ref, dxbc_ref, dq_ref, dk_ref, dv_ref, dg_ref, ddt_ref, x_ref, dx1_ref, w0_ref, wt_ref, wdt_ref, gx_ref, dw0_ref):
        @pl.when(pl.program_id(0) == 0)
        def _():
            dw0_ref[...] = jnp.zeros_like(dw0_ref)

        dh = _dot(ddt_ref[...], wdt_ref[...])
        for ref, (lo, hi) in zip((dz_ref, dxbc_ref, dq_ref, dk_ref, dv_ref, dg_ref), IN_SEGMENTS):
            dh = dh + _dot(ref[...], wt_ref[lo:hi, :])
        xf = x_ref[...]
        dx, dw = _rms_bwd(dh, xf, _rstd(xf), w0_ref[...])
        dw0_ref[...] += dw
        gx_ref[...] = dx1_ref[...] + dx

    widths = (SSD_WIDTH, SSD_XBC, RET_QK_W, RET_QK_W, RET_V_W, RET_V_W, LANES)
    return pl.pallas_call(
        body, name="in_bwd", grid=(T // tm,),
        in_specs=[_row_spec(tm, w) for w in widths] + [_row_spec(tm, D_MODEL), _row_spec(tm, D_MODEL),
                                                       _const_spec((1, D_MODEL)), _VMEM_WHOLE, _VMEM_WHOLE],
        out_specs=[_row_spec(tm, D_MODEL), _const_spec((1, D_MODEL))],
        out_shape=[S((T, D_MODEL), F32), S((1, D_MODEL), F32)],
        compiler_params=_cparams("arbitrary"),
    )(dz, dxbc, dq, dk, dv, dg, ddt, x, dx1, w0, wt, wdt)


DW_TILE_BYTES = 6 << 20


def _matmul_tn(a, b, name, tk=1024, stack=None):
    T, M = a.shape
    N = b.shape[1]
    parts, index, below = stack if stack is not None else (1, 0, None)
    tm_, tn = M, N
    while tm_ * tn * 4 > DW_TILE_BYTES:
        if tm_ >= tn and tm_ % 256 == 0:
            tm_ //= 2
        elif tn % 256 == 0:
            tn //= 2
        else:
            break
    nk = T // tk

    def body(a_ref, b_ref, *rest):
        o_ref, acc = rest[-2], rest[-1]
        k = pl.program_id(2)

        @pl.when(k == 0)
        def _():
            acc[...] = jnp.zeros_like(acc)

        acc[...] += _dot_tn(a_ref[...], b_ref[...])

        @pl.when(k == nk - 1)
        def _():
            o_ref[...] = acc[...].astype(o_ref.dtype)

    first_block = index * (M // tm_)
    extra = [] if below is None else [below]
    return pl.pallas_call(
        body, name=name, grid=(M // tm_, N // tn, nk),
        in_specs=[pl.BlockSpec((tk, tm_), lambda m, n, k: (k, m)), pl.BlockSpec((tk, tn), lambda m, n, k: (k, n))]
        + [pl.BlockSpec(memory_space=pl.ANY)] * len(extra),
        out_specs=pl.BlockSpec((tm_, tn), lambda m, n, k: (first_block + m, n)),
        out_shape=S((parts * M, N), BF),
        scratch_shapes=[pltpu.VMEM((tm_, tn), F32)],
        input_output_aliases={2: 0} if extra else {},
        compiler_params=_cparams("parallel", "parallel", "arbitrary"),
    )(a, b, *extra)


def _matmul_tn_group(as_, b, name, tk=1024):
    T, N = b.shape
    na = len(as_)
    nk = T // tk

    def body(*refs):
        a_refs, b_ref, o_refs, accs = refs[:na], refs[na], refs[na + 1:2 * na + 1], refs[2 * na + 1:]
        k = pl.program_id(0)

        @pl.when(k == 0)
        def _():
            for acc in accs:
                acc[...] = jnp.zeros_like(acc)

        bt = b_ref[...]
        for a_ref, acc in zip(a_refs, accs):
            acc[...] += _dot_tn(a_ref[...], bt)

        @pl.when(k == nk - 1)
        def _():
            for o_ref, acc in zip(o_refs, accs):
                o_ref[...] = acc[...].astype(o_ref.dtype)

    return pl.pallas_call(
        body, name=name, grid=(nk,),
        in_specs=[_row_spec(tk, a.shape[1]) for a in as_] + [_row_spec(tk, N)],
        out_specs=[_const_spec((a.shape[1], N)) for a in as_],
        out_shape=[S((a.shape[1], N), BF) for a in as_],
        scratch_shapes=[pltpu.VMEM((a.shape[1], N), F32) for a in as_],
        compiler_params=_cparams("arbitrary"),
    )(*as_, b)


def _tri(lower):
    r = lax.broadcasted_iota(jnp.int32, (CH, CH), 0)
    c = lax.broadcasted_iota(jnp.int32, (CH, CH), 1)
    return ((c <= r) if lower else (r <= c)).astype(F32)


def _ssd_conv(xc_ref, xh_ref, cw_ref, cb_ref, first):
    xc = xc_ref[...].astype(F32)
    xh = jnp.where(first, 0.0, xh_ref[...].astype(F32))
    ext = jnp.concatenate([xh, xc], axis=0)
    return cb_ref[...] + sum(cw_ref[j:j + 1, :] * _rows_before(ext, SSD_CONV - 1 - j, HALO, CH) for j in range(SSD_CONV))


def _ssd_decay(dtr_ref, dtb_ref, alog_ref):
    dt = _softplus(dtr_ref[...] + dtb_ref[...])
    a = -jnp.exp(alog_ref[...])
    da = dt * a
    cs = _dot_hi(_tri(True), da)
    cst = _dot_tn_hi(da, _tri(False))
    return dt, a, cs, cst


HPG = SSD_HEADS // SSD_GROUPS
GW = HPG * SSD_HEAD_DIM


def _expand_heads(src, buf):
    for h in range(SSD_HEADS):
        buf[:, h * SSD_HEAD_DIM:(h + 1) * SSD_HEAD_DIM] = jnp.broadcast_to(src[:, h:h + 1], (CH, SSD_HEAD_DIM))


def _ssd_expanded(act, dt, cs, dtx, csx):
    _expand_heads(dt, dtx)
    _expand_heads(cs, csx)
    csv = csx[...]
    last = csv[CH - 1:CH, :]
    e_exp = jnp.exp(csv)
    dec_exp = jnp.exp(last - csv)
    el_exp = jnp.exp(last)
    xs = act[:, 0:SSD_WIDTH]
    xdt = xs * dtx[...]
    return xs, xdt, xdt * dec_exp, e_exp, dec_exp, el_exp


def _decay_mats(h, cs, cst, transposed):
    r = lax.broadcasted_iota(jnp.int32, (CH, CH), 0)
    c = lax.broadcasted_iota(jnp.int32, (CH, CH), 1)
    c_col = cs[:, h:h + 1]
    c_row = cst[h:h + 1, :]
    if transposed:
        return jnp.exp(jnp.where(r <= c, c_row - c_col, -1e30))
    return jnp.exp(jnp.where(r >= c, c_col - c_row, -1e30))


def _ssd_specs(T):
    nc = T // CH
    return nc, [
        _row_spec(CH, SSD_XBC),
        pl.BlockSpec((HALO, SSD_XBC), lambda i: (jnp.maximum(i * (CH // HALO) - 1, 0), 0)),
        _row_spec(CH, LANES),
        _row_spec(CH, SSD_WIDTH),
    ]


def _groups(act):
    bm = [act[:, SSD_WIDTH + g * SSD_STATE:SSD_WIDTH + (g + 1) * SSD_STATE] for g in range(SSD_GROUPS)]
    o = SSD_WIDTH + SSD_GROUPS * SSD_STATE
    cm = [act[:, o + g * SSD_STATE:o + (g + 1) * SSD_STATE] for g in range(SSD_GROUPS)]
    return bm, cm


def _ssd_fwd(xbc, dtr, z, convw, convb, dtb, alog, dskx, nw):
    T = xbc.shape[0]
    nc, specs = _ssd_specs(T)

    def body(xc_ref, xh_ref, dtr_ref, z_ref, cw_ref, cb_ref, dtb_ref, alog_ref, dskx_ref, nw_ref,
             out_ref, y_ref, u_ref, st_ref, state, ybuf, dtx, csx):
        i = pl.program_id(0)

        @pl.when(i == 0)
        def _():
            state[...] = jnp.zeros_like(state)

        u = _ssd_conv(xc_ref, xh_ref, cw_ref, cb_ref, i == 0)
        u_ref[...] = u.astype(BF)
        act = u * _sigmoid(u)
        dt, a, cs, cst = _ssd_decay(dtr_ref, dtb_ref, alog_ref)
        xs, xdt, w, e_exp, dec_exp, el_exp = _ssd_expanded(act, dt, cs, dtx, csx)
        bm, cm = _groups(act)
        groups, heads = range(SSD_GROUPS), range(SSD_HEADS)
        gsl = [slice(g * GW, (g + 1) * GW) for g in groups]
        hsl = [slice(h * SSD_HEAD_DIM, (h + 1) * SSD_HEAD_DIM) for h in heads]
        cb = [_dot_nt(cm[g], bm[g]) for g in groups]
        yoff = [_dot(cm[g], state[g]) for g in groups]
        sloc = [_dot_tn(bm[g], w[:, gsl[g]]) for g in groups]
        lm = [_decay_mats(h, cs, cst, False) for h in heads]
        ydiag = [_dot(cb[h // HPG] * lm[h], xdt[:, hsl[h]]) for h in heads]
        for g in groups:
            st_ref[0, g] = state[g]
            ybuf[:, gsl[g]] = yoff[g] * e_exp[:, gsl[g]] + xs[:, gsl[g]] * dskx_ref[:, gsl[g]]
            state[g] = state[g] * el_exp[:, gsl[g]] + sloc[g]
        for h in heads:
            ybuf[:, hsl[h]] += ydiag[h]
        yv = ybuf[...]
        y_ref[...] = yv.astype(BF)
        zf = z_ref[...].astype(F32)
        gated = yv * (zf * _sigmoid(zf))
        out_ref[...] = (gated * _rstd(gated) * nw_ref[...]).astype(BF)

    st_spec = pl.BlockSpec((1, SSD_GROUPS, SSD_STATE, GW), lambda i: (i, 0, 0, 0))
    return pl.pallas_call(
        body, name="ssd_fwd", grid=(nc,),
        in_specs=specs + [_const_spec((8, SSD_XBC)), _const_spec((1, SSD_XBC)), _const_spec((1, LANES)),
                          _const_spec((1, LANES)), _const_spec((1, SSD_WIDTH)), _const_spec((1, SSD_WIDTH))],
        out_specs=[_row_spec(CH, SSD_WIDTH), _row_spec(CH, SSD_WIDTH), _row_spec(CH, SSD_XBC), st_spec],
        out_shape=[S((T, SSD_WIDTH), BF), S((T, SSD_WIDTH), BF), S((T, SSD_XBC), BF), S((nc, SSD_GROUPS, SSD_STATE, GW), F32)],
        scratch_shapes=[pltpu.VMEM((SSD_GROUPS, SSD_STATE, GW), F32), pltpu.VMEM((CH, SSD_WIDTH), F32),
                        pltpu.VMEM((CH, SSD_WIDTH), F32), pltpu.VMEM((CH, SSD_WIDTH), F32)],
        compiler_params=_cparams("arbitrary"),
    )(xbc, xbc, dtr, z, convw, convb, dtb, alog, dskx, nw)


def _ssd_bwd(dout, y, u, xbc, dtr, z, states, convw, dtb, alog, dskx, nw):
    T = xbc.shape[0]
    nc = T // CH
    rev = lambda i: (nc - 1 - i, 0)
    rspec = lambda w: pl.BlockSpec((CH, w), rev)
    NB = SSD_WIDTH
    NC_ = SSD_WIDTH + SSD_GROUPS * SSD_STATE

    def body(do_ref, y_ref, u_ref, xc_ref, dtr_ref, z_ref, st_ref, cw_ref, dtb_ref, alog_ref, dskx_ref, nw_ref,
             dz_ref, dxbc_ref, ddt_ref, dcw_ref, dcb_ref, ddtb_ref, dalog_ref, ddsk_ref, dnw_ref,
             dstate, ducarry, dtx, csx, dxdtbuf, dact):
        i = pl.program_id(0)

        @pl.when(i == 0)
        def _():
            dstate[...] = jnp.zeros_like(dstate)
            ducarry[...] = jnp.zeros_like(ducarry)
            for ref in (dcw_ref, dcb_ref, ddtb_ref, dalog_ref, ddsk_ref, dnw_ref):
                ref[...] = jnp.zeros_like(ref)

        xc = xc_ref[...].astype(F32)
        u = u_ref[...].astype(F32)
        sg = _sigmoid(u)
        act = u * sg
        dt, a, cs, cst = _ssd_decay(dtr_ref, dtb_ref, alog_ref)
        xs, xdt, w, e_exp, dec_exp, el_exp = _ssd_expanded(act, dt, cs, dtx, csx)
        bm, cm = _groups(act)
        yv = y_ref[...].astype(F32)
        zf = z_ref[...].astype(F32)
        sz = _sigmoid(zf)
        gated = yv * (zf * sz)
        dgated, dnw = _rms_bwd(do_ref[...].astype(F32), gated, _rstd(gated), nw_ref[...])
        dnw_ref[...] += dnw
        dz_ref[...] = (dgated * yv * _dsilu(zf, sz)).astype(BF)
        dy = dgated * (zf * sz)
        lane_of = lax.broadcasted_iota(jnp.int32, (SSD_WIDTH, LANES), 0) - SSD_HEAD_DIM * lax.broadcasted_iota(jnp.int32, (SSD_WIDTH, LANES), 1)
        expt = ((lane_of >= 0) & (lane_of < SSD_HEAD_DIM)).astype(F32)
        ddsk_ref[...] += _dot_hi(jnp.sum(dy * xs, axis=0, keepdims=True), expt)
        dcs = jnp.zeros((CH, LANES), F32)
        dcst = jnp.zeros((LANES, CH), F32)
        ddt = jnp.zeros((CH, LANES), F32)
        lane_id = lax.broadcasted_iota(jnp.int32, (CH, LANES), 1)
        row_id = lax.broadcasted_iota(jnp.int32, (LANES, CH), 0)
        lastrows = []
        for g in range(SSD_GROUPS):
            gs = slice(g * GW, (g + 1) * GW)
            st = st_ref[0, g]
            dsn = dstate[g]
            cbm = _dot_nt(cm[g], bm[g])
            cbt = _dot_nt(bm[g], cm[g])
            dy_g = dy[:, gs]
            yoff = _dot(cm[g], st) * e_exp[:, gs]
            dq = dy_g * e_exp[:, gs]
            dcm_g = _dot_nt(dq, st)
            dstate[g] = _dot_tn(cm[g], dq) + dsn * el_exp[:, gs]
            dw = _dot(bm[g], dsn)
            w_g = w[:, gs]
            dbm_g = _dot_nt(w_g, dsn)
            dww = dw * w_g
            red = dy_g * yoff - dww
            lastrows.append(jnp.sum(dsn * st, axis=0, keepdims=True) * el_exp[:, gs] + jnp.sum(dww, axis=0, keepdims=True))
            dxdtbuf[:, gs] = dw * dec_exp[:, gs]
            dcb = jnp.zeros((CH, CH), F32)
            hs = range(g * HPG, (g + 1) * HPG)
            hsl = {h: slice(h * SSD_HEAD_DIM, (h + 1) * SSD_HEAD_DIM) for h in hs}
            lm = {h: _decay_mats(h, cs, cst, False) for h in hs}
            dm = {h: _dot_nt(dy[:, hsl[h]], xdt[:, hsl[h]]) for h in hs}
            dxd = {h: _dot(cbt * _decay_mats(h, cs, cst, True), dy[:, hsl[h]]) for h in hs}
            for h in hs:
                sl = hsl[h]
                rl = slice((h - g * HPG) * SSD_HEAD_DIM, (h - g * HPG + 1) * SSD_HEAD_DIM)
                dxdt_h = dxdtbuf[:, sl] + dxd[h]
                dxdtbuf[:, sl] = dxdt_h
                dseg = dm[h] * (cbm * lm[h])
                dcb = dcb + dm[h] * lm[h]
                col = jnp.sum(dseg, axis=1, keepdims=True) + jnp.sum(red[:, rl], axis=1, keepdims=True)
                dcs = jnp.where(lane_id == h, col, dcs)
                dcst = jnp.where(row_id == h, -jnp.sum(dseg, axis=0, keepdims=True), dcst)
                ddt = jnp.where(lane_id == h, jnp.sum(dxdt_h * xs[:, sl], axis=1, keepdims=True), ddt)
            dact[:, NB + g * SSD_STATE:NB + (g + 1) * SSD_STATE] = dbm_g + _dot_tn(dcb, cm[g])
            dact[:, NC_ + g * SSD_STATE:NC_ + (g + 1) * SSD_STATE] = dcm_g + _dot(dcb, bm[g])
        dact[:, 0:SSD_WIDTH] = dy * dskx_ref[...] + dxdtbuf[...] * dtx[...]
        dlast = _dot_hi(jnp.concatenate(lastrows, axis=1), expt)
        rows = lax.broadcasted_iota(jnp.int32, (CH, LANES), 0)
        dcs = dcs + _dot_tn_hi(dcst, jnp.eye(LANES, dtype=F32)) + jnp.where(rows == CH - 1, dlast, 0.0)
        dda = _dot_hi(_tri(False), dcs)
        dalog_ref[...] += jnp.sum(dda * dt, axis=0, keepdims=True) * a
        ddt = ddt + dda * a
        ddtr = ddt * _sigmoid(dtr_ref[...] + dtb_ref[...])
        ddtb_ref[...] += jnp.sum(ddtr, axis=0, keepdims=True)
        ddt_ref[...] = ddtr.astype(BF)
        du = dact[...] * _dsilu(u, sg)
        dcb_ref[...] += jnp.sum(du, axis=0, keepdims=True)
        ext = jnp.concatenate([du, ducarry[...]], axis=0)
        ducarry[...] = du[0:8]
        dx = jnp.zeros((CH, SSD_XBC), F32)
        for j in range(SSD_CONV):
            sj = _rows_after(ext, SSD_CONV - 1 - j, CH)
            dx = dx + cw_ref[j:j + 1, :] * sj
            dcw_ref[j:j + 1, :] += jnp.sum(sj * xc, axis=0, keepdims=True)
        dxbc_ref[...] = dx.astype(BF)

    return pl.pallas_call(
        body, name="ssd_bwd", grid=(nc,),
        in_specs=[rspec(SSD_WIDTH), rspec(SSD_WIDTH), rspec(SSD_XBC), rspec(SSD_XBC), rspec(LANES), rspec(SSD_WIDTH),
                  pl.BlockSpec((1, SSD_GROUPS, SSD_STATE, GW), lambda i: (nc - 1 - i, 0, 0, 0)),
                  _const_spec((8, SSD_XBC)), _const_spec((1, LANES)),
                  _const_spec((1, LANES)), _const_spec((1, SSD_WIDTH)), _const_spec((1, SSD_WIDTH))],
        out_specs=[rspec(SSD_WIDTH), rspec(SSD_XBC), rspec(LANES),
                   _const_spec((8, SSD_XBC)), _const_spec((1, SSD_XBC)), _const_spec((1, LANES)),
                   _const_spec((1, LANES)), _const_spec((1, LANES)), _const_spec((1, SSD_WIDTH))],
        out_shape=[S((T, SSD_WIDTH), BF), S((T, SSD_XBC), BF), S((T, LANES), BF),
                   S((8, SSD_XBC), F32), S((1, SSD_XBC), F32), S((1, LANES), F32),
                   S((1, LANES), F32), S((1, LANES), F32), S((1, SSD_WIDTH), F32)],
        scratch_shapes=[pltpu.VMEM((SSD_GROUPS, SSD_STATE, GW), F32), pltpu.VMEM((8, SSD_XBC), F32),
                        pltpu.VMEM((CH, SSD_WIDTH), F32), pltpu.VMEM((CH, SSD_WIDTH), F32),
                        pltpu.VMEM((CH, SSD_WIDTH), F32), pltpu.VMEM((CH, SSD_XBC), F32)],
        compiler_params=_cparams("arbitrary"),
    )(dout, y, u, xbc, dtr, z, states, convw, dtb, alog, dskx, nw)


def _log_gamma(h):
    return float(np.log1p(-np.exp2(np.float32(-5.0 - h)), dtype=np.float32))


def _swap_halves(t):
    n = t.shape[1]
    lane = lax.broadcasted_iota(jnp.int32, t.shape, 1)
    return jnp.where((lane & (RET_QK - 1)) < RET_QK // 2, pltpu.roll(t, n - RET_QK // 2, 1), pltpu.roll(t, RET_QK // 2, 1))


def _rot(t, cos, sin):
    return t * cos + _swap_halves(t) * sin


def _rot_t(d, cos, sin):
    return d * cos + _swap_halves(d * sin)


def _ret_tables():
    lg = jnp.asarray([_log_gamma(h) for h in range(RET_HEADS)], F32)[:, None, None]
    pos = jnp.arange(CH, dtype=F32)
    rel = pos[:, None] - pos[None, :]
    dmask = jnp.where(rel >= 0, jnp.exp(lg * jnp.maximum(rel, 0.0)), 0.0)
    kdec = jnp.exp(lg * (CH - 1.0 - pos)[None, :, None])
    qdec = jnp.exp(lg * (pos + 1.0)[None, :, None])
    rows = jnp.concatenate([jnp.swapaxes(kdec, 1, 2), jnp.swapaxes(qdec, 1, 2), jnp.zeros((RET_HEADS, CH - 2, CH), F32)], axis=1)
    full = lambda t: jnp.broadcast_to(t, (RET_HEADS, CH, CH))
    return jnp.stack([dmask, jnp.swapaxes(dmask, 1, 2), full(kdec), full(qdec), full(rows)], axis=1)


def _ret_consts(h, rt_ref):
    kdec = rt_ref[h, 2][:, 0:RET_QK]
    qdec = rt_ref[h, 3][:, 0:RET_QK]
    return rt_ref[h, 0], rt_ref[h, 1], kdec, qdec, rt_ref[h, 4, 0:1, :], rt_ref[h, 4, 1:2, :], math.exp(_log_gamma(h) * CH)


_RT_SPEC = pl.BlockSpec((RET_HEADS, 5, CH, CH), lambda i: (0, 0, 0, 0))


def _ret_fwd(q, k, v, g, cos, sin, nw, rt):
    T = q.shape[0]
    nc = T // CH

    def body(q_ref, k_ref, v_ref, g_ref, cos_ref, sin_ref, nw_ref, rt_ref, out_ref, st_ref, state):
        i = pl.program_id(0)

        @pl.when(i == 0)
        def _():
            state[...] = jnp.zeros_like(state)

        cosf = jnp.tile(cos_ref[...], (1, RET_QK_W // LANES))
        sinf = jnp.tile(sin_ref[...], (1, RET_QK_W // LANES))
        qr = _rot(q_ref[...].astype(F32), cosf, sinf)
        kr = _rot(k_ref[...].astype(F32), cosf, sinf) * (RET_QK ** -0.5)
        krt = kr.T
        st_ref[0] = state[...]
        qsl = [slice(h * RET_QK, (h + 1) * RET_QK) for h in range(RET_HEADS)]
        vsl = [slice(h * RET_V, (h + 1) * RET_V) for h in range(RET_HEADS)]
        consts = [_ret_consts(h, rt_ref) for h in range(RET_HEADS)]
        scores = [_dot_nt(qr[:, qsl[h]], kr[:, qsl[h]]) * consts[h][0] for h in range(RET_HEADS)]
        cross = [_dot(qr[:, qsl[h]] * consts[h][3], state[h]) for h in range(RET_HEADS)]
        kv = [_dot(krt[qsl[h], :] * consts[h][4], v_ref[:, vsl[h]]) for h in range(RET_HEADS)]
        o_all = [_dot(scores[h], v_ref[:, vsl[h]]) + cross[h] for h in range(RET_HEADS)]
        for h in range(RET_HEADS):
            state[h] = state[h] * consts[h][6] + kv[h]
        for h in range(RET_HEADS):
            sl = slice(h * RET_V, (h + 1) * RET_V)
            o = o_all[h]
            gf = g_ref[:, sl].astype(F32)
            out_ref[:, sl] = (o * _rstd(o) * nw_ref[:, sl] * (gf * _sigmoid(gf))).astype(BF)

    return pl.pallas_call(
        body, name="ret_fwd", grid=(nc,),
        in_specs=[_row_spec(CH, RET_QK_W), _row_spec(CH, RET_QK_W), _row_spec(CH, RET_V_W), _row_spec(CH, RET_V_W),
                  _row_spec(CH, LANES), _row_spec(CH, LANES), _const_spec((1, RET_V_W)), _RT_SPEC],
        out_specs=[_row_spec(CH, RET_V_W), pl.BlockSpec((1, RET_HEADS, RET_QK, RET_V), lambda i: (i, 0, 0, 0))],
        out_shape=[S((T, RET_V_W), BF), S((nc, RET_HEADS, RET_QK, RET_V), F32)],
        scratch_shapes=[pltpu.VMEM((RET_HEADS, RET_QK, RET_V), F32)],
        compiler_params=_cparams("arbitrary"),
    )(q, k, v, g, cos, sin, nw, rt)


def _ret_bwd(dout, q, k, v, g, states, cos, sin, nw, rt):
    T = q.shape[0]
    nc = T // CH
    rev = lambda i: (nc - 1 - i, 0)
    rspec = lambda w: pl.BlockSpec((CH, w), rev)

    def body(do_ref, q_ref, k_ref, v_ref, g_ref, st_ref, cos_ref, sin_ref, nw_ref, rt_ref,
             dq_ref, dk_ref, dv_ref, dg_ref, dnw_ref, dstate, dqbuf, dkbuf):
        i = pl.program_id(0)

        @pl.when(i == 0)
        def _():
            dstate[...] = jnp.zeros_like(dstate)
            dnw_ref[...] = jnp.zeros_like(dnw_ref)

        cosf = jnp.tile(cos_ref[...], (1, RET_QK_W // LANES))
        sinf = jnp.tile(sin_ref[...], (1, RET_QK_W // LANES))
        qr = _rot(q_ref[...].astype(F32), cosf, sinf)
        kr = _rot(k_ref[...].astype(F32), cosf, sinf) * (RET_QK ** -0.5)
        qrt = qr.T
        heads = range(RET_HEADS)
        qsl = [slice(h * RET_QK, (h + 1) * RET_QK) for h in heads]
        vsl = [slice(h * RET_V, (h + 1) * RET_V) for h in heads]
        do_all = []
        consts = [_ret_consts(h, rt_ref) for h in heads]
        scores = [_dot_nt(qr[:, qsl[h]], kr[:, qsl[h]]) * consts[h][0] for h in heads]
        scores_t = [_dot_nt(kr[:, qsl[h]], qr[:, qsl[h]]) * consts[h][1] for h in heads]
        cross = [_dot(qr[:, qsl[h]] * consts[h][3], st_ref[0, h]) for h in heads]
        o_all = [_dot(scores[h], v_ref[:, vsl[h]]) + cross[h] for h in heads]
        for h in heads:
            o = o_all[h]
            rr = _rstd(o)
            of = o * rr
            gf = g_ref[:, vsl[h]].astype(F32)
            sgg = _sigmoid(gf)
            d_h = do_ref[:, vsl[h]].astype(F32)
            nw_h = nw_ref[:, vsl[h]]
            dg_ref[:, vsl[h]] = (d_h * of * nw_h * _dsilu(gf, sgg)).astype(BF)
            dt_ = d_h * (gf * sgg)
            dnw_ref[:, vsl[h]] += jnp.sum(dt_ * of, axis=0, keepdims=True)
            dof = dt_ * nw_h
            do_all.append(rr * dof - o * (rr * rr * rr) * jnp.mean(dof * o, axis=-1, keepdims=True))
        dsc = [_dot_nt(do_all[h], v_ref[:, vsl[h]]) * consts[h][0] for h in heads]
        dsc_t = [_dot_nt(v_ref[:, vsl[h]], do_all[h]) * consts[h][1] for h in heads]
        dv_a = [_dot(scores_t[h], do_all[h]) for h in heads]
        dv_b = [_dot(kr[:, qsl[h]] * consts[h][2], dstate[h]) for h in heads]
        dq_a = [_dot(dsc[h], kr[:, qsl[h]]) for h in heads]
        dq_b = [_dot_nt(do_all[h], st_ref[0, h]) * consts[h][3] for h in heads]
        dk_a = [_dot(dsc_t[h], qr[:, qsl[h]]) for h in heads]
        dk_b = [_dot_nt(v_ref[:, vsl[h]], dstate[h]) * consts[h][2] for h in heads]
        dst = [_dot(qrt[qsl[h], :] * consts[h][5], do_all[h]) for h in heads]
        for h in heads:
            dv_ref[:, vsl[h]] = (dv_a[h] + dv_b[h]).astype(BF)
            dqbuf[:, qsl[h]] = dq_a[h] + dq_b[h]
            dkbuf[:, qsl[h]] = dk_a[h] + dk_b[h]
            dstate[h] = dstate[h] * consts[h][6] + dst[h]
        dq_ref[...] = _rot_t(dqbuf[...], cosf, sinf).astype(BF)
        dk_ref[...] = (_rot_t(dkbuf[...], cosf, sinf) * (RET_QK ** -0.5)).astype(BF)

    return pl.pallas_call(
        body, name="ret_bwd", grid=(nc,),
        in_specs=[rspec(RET_V_W), rspec(RET_QK_W), rspec(RET_QK_W), rspec(RET_V_W), rspec(RET_V_W),
                  pl.BlockSpec((1, RET_HEADS, RET_QK, RET_V), lambda i: (nc - 1 - i, 0, 0, 0)),
                  rspec(LANES), rspec(LANES), _const_spec((1, RET_V_W)), _RT_SPEC],
        out_specs=[rspec(RET_QK_W), rspec(RET_QK_W), rspec(RET_V_W), rspec(RET_V_W), _const_spec((1, RET_V_W))],
        out_shape=[S((T, RET_QK_W), BF), S((T, RET_QK_W), BF), S((T, RET_V_W), BF), S((T, RET_V_W), BF),
                   S((1, RET_V_W), F32)],
        scratch_shapes=[pltpu.VMEM((RET_HEADS, RET_QK, RET_V), F32), pltpu.VMEM((CH, RET_QK_W), F32),
                        pltpu.VMEM((CH, RET_QK_W), F32)],
        compiler_params=_cparams("arbitrary"),
    )(dout, q, k, v, g, states, cos, sin, nw, rt)


_HBM = pl.BlockSpec(memory_space=pltpu.HBM)
_SEM = pl.BlockSpec(memory_space=pltpu.SEMAPHORE)
_EFFECT = pltpu.SideEffectType.DATAFLOW_SIDE_EFFECTING


ALL_PEERS = tuple(range(1, N_DEV))
ONE_PER_CHIP = (1, 2, 4, 6)
OTHER_CHIPS = (2, 4, 6)


def _split_copies(buf_refs, land_refs, send_sems, recv_sems, same, to_me, ks):
    x, y, c = lax.axis_index("x"), lax.axis_index("y"), lax.axis_index("c")
    me = 4 * x + 2 * y + c
    cps = []
    for ki, k in enumerate(ks):
        px = 1 - x if k & 4 else x
        py = 1 - y if k & 2 else y
        pc = 1 - c if k & 1 else c
        p = 4 * px + 2 * py + pc
        for b in range(len(buf_refs)):
            s = b * len(ks) + ki
            cps.append(pltpu.make_async_remote_copy(
                src_ref=buf_refs[b] if same else buf_refs[b].at[p], dst_ref=land_refs[b].at[me if to_me else p],
                send_sem=send_sems.at[s], recv_sem=recv_sems.at[s], device_id=(px, py, pc), device_id_type=pl.DeviceIdType.MESH))
    return cps


def _exchange_start(bufs, name, same, ks=ALL_PEERS):
    nb = len(bufs)
    ns = nb * len(ks)
    lands = [lax.empty((N_DEV,) + tuple(b.shape if same else b.shape[1:]), b.dtype) for b in bufs]

    def body(*refs):
        buf_refs, land_refs = refs[:nb], refs[nb:2 * nb]
        send_sems, recv_sems = refs[2 * nb], refs[2 * nb + 1]
        token = refs[-1]
        for cp in _split_copies(buf_refs, land_refs, send_sems, recv_sems, same, True, ks):
            cp.start()
        token[...] = jnp.zeros_like(token)

    hbm = lambda a: pltpu.with_memory_space_constraint(a, pltpu.HBM)
    out = pl.pallas_call(
        body, name=name,
        out_shape=(pltpu.SemaphoreType.DMA((ns,)), pltpu.SemaphoreType.DMA((ns,)),
                   *[pltpu.HBM(a.shape, a.dtype) for a in list(bufs) + lands], S((8, LANES), F32)),
        in_specs=[_HBM] * (2 * nb), out_specs=(_SEM, _SEM, *[_HBM] * (2 * nb), pl.BlockSpec(memory_space=pltpu.VMEM)),
        input_output_aliases={i: 2 + i for i in range(2 * nb)},
        compiler_params=pltpu.CompilerParams(has_side_effects=_EFFECT),
    )(*[hbm(a) for a in list(bufs) + lands])
    return out[0], out[1], list(out[2:2 + nb]), list(out[2 + nb:2 + 2 * nb]), out[-1]


def _exchange_wait(started, after, name, same, ks=ALL_PEERS):
    send_sems, recv_sems, bufs, lands, _ = started
    nb = len(bufs)
    after = list(after) if isinstance(after, (list, tuple)) else [after]

    def body(*refs):
        buf_refs, land_refs = refs[:nb], refs[nb:2 * nb]
        s_sems, r_sems = refs[2 * nb], refs[2 * nb + 1]
        for cp in _split_copies(buf_refs, land_refs, s_sems, r_sems, same, False, ks):
            cp.wait_send()
            cp.wait_recv()

    out = pl.pallas_call(
        body, name=name,
        out_shape=tuple(pltpu.HBM(a.shape, a.dtype) for a in bufs + lands),
        in_specs=[_HBM] * (2 * nb) + [_SEM, _SEM] + [pl.BlockSpec(memory_space=pl.ANY)] * len(after),
        out_specs=tuple([_HBM] * (2 * nb)),
        input_output_aliases={i: i for i in range(2 * nb)},
        compiler_params=pltpu.CompilerParams(has_side_effects=_EFFECT),
    )(*bufs, *lands, send_sems, recv_sems, *after)
    return list(out[:nb]), list(out[nb:])


def _forward_copies(land_ref, send_sems, recv_sems, sending):
    x, y, c = lax.axis_index("x"), lax.axis_index("y"), lax.axis_index("c")
    cps = []
    for ki, k in enumerate(OTHER_CHIPS):
        px = 1 - x if k & 4 else x
        py = 1 - y if k & 2 else y
        q = 4 * px + 2 * py + (c if sending else 1 - c)
        cps.append(pltpu.make_async_remote_copy(
            src_ref=land_ref.at[q], dst_ref=land_ref.at[q], send_sem=send_sems.at[ki], recv_sem=recv_sems.at[ki],
            device_id=(x, y, 1 - c), device_id_type=pl.DeviceIdType.MESH))
    return cps


def _forward_start(land, name):
    def body(land_ref, send_sems, recv_sems, land_thru, token):
        for cp in _forward_copies(land_ref, send_sems, recv_sems, True):
            cp.start()
        token[...] = jnp.zeros_like(token)

    n = len(OTHER_CHIPS)
    out = pl.pallas_call(
        body, name=name,
        out_shape=(pltpu.SemaphoreType.DMA((n,)), pltpu.SemaphoreType.DMA((n,)), pltpu.HBM(land.shape, land.dtype),
                   S((8, LANES), F32)),
        in_specs=[_HBM], out_specs=(_SEM, _SEM, _HBM, pl.BlockSpec(memory_space=pltpu.VMEM)),
        input_output_aliases={0: 2},
        compiler_params=pltpu.CompilerParams(has_side_effects=_EFFECT),
    )(pltpu.with_memory_space_constraint(land, pltpu.HBM))
    return out


def _forward_wait(started, after, name):
    send_sems, recv_sems, land, _ = started
    after = list(after) if isinstance(after, (list, tuple)) else [after]

    def body(land_ref, s_sems, r_sems, *rest):
        for cp in _forward_copies(land_ref, s_sems, r_sems, False):
            cp.wait_send()
            cp.wait_recv()

    return pl.pallas_call(
        body, name=name, out_shape=pltpu.HBM(land.shape, land.dtype),
        in_specs=[_HBM, _SEM, _SEM] + [pl.BlockSpec(memory_space=pl.ANY)] * len(after), out_specs=_HBM,
        input_output_aliases={0: 0},
        compiler_params=pltpu.CompilerParams(has_side_effects=_EFFECT),
    )(land, send_sems, recv_sems, *after)


def _sum_slabs(recv, name):
    n, R, _ = recv.shape

    def body(r_ref, o_ref):
        g = r_ref[0].astype(F32)
        for s in range(1, n):
            g = g + r_ref[s].astype(F32)
        o_ref[...] = g

    return pl.pallas_call(body, name=name, out_shape=S((R, LANES), F32))(recv)


def _adamw(recv, w, m, v, name, tr, tc=None):
    n, R, C = recv.shape
    c1 = 1.0 - ADAM_B1 ** ADAM_STEP
    c2 = 1.0 - ADAM_B2 ** ADAM_STEP

    def body(r_ref, w_ref, m_ref, v_ref, g_out, d_out, m_out, v_out):
        g = r_ref[0].astype(F32)
        for s in range(1, n):
            g = g + r_ref[s].astype(F32)
        mm = ADAM_B1 * m_ref[...] + (1.0 - ADAM_B1) * g
        vv = ADAM_B2 * v_ref[...] + (1.0 - ADAM_B2) * (g * g)
        g_out[...] = g
        m_out[...] = mm
        v_out[...] = vv
        d_out[...] = -ADAM_LR * ((mm / c1) / (jnp.sqrt(vv / c2) + ADAM_EPS) + ADAM_WD * w_ref[...])

    tc = C if tc is None else tc
    spec = pl.BlockSpec((tr, tc), lambda i, j: (i, j))
    return pl.pallas_call(
        body, name=name, grid=(R // tr, C // tc),
        in_specs=[pl.BlockSpec((n, tr, tc), lambda i, j: (0, i, j)), spec, spec, spec],
        out_specs=[spec] * 4, out_shape=[S((R, C), F32)] * 4,
        compiler_params=_cparams("parallel", "parallel"),
    )(recv, w, m, v)


def _pack(parts, rows):
    cols = []
    for p in parts:
        f = p.reshape(-1)
        cols.append(jnp.pad(f, (0, (-f.shape[0]) % LANES)))
    flat = jnp.concatenate(cols)
    return jnp.pad(flat, (0, rows * LANES - flat.shape[0])).reshape(rows, LANES)


def _unpack(buf, shapes):
    flat = buf.reshape(-1)
    out, o = [], 0
    for shp in shapes:
        n = int(np.prod(shp))
        out.append(flat[o:o + n].reshape(shp))
        o += n + (-n) % LANES
    return out


SMALL_ROWS = 200
CONV_ROWS = 16


def kernel(x, pre_mix_norm_w, w_in, ssd_conv_w, ssd_conv_b, ssd_dt_bias, ssd_a_log, ssd_d, ssd_norm_w, ret_norm_w, w_out, post_mix_norm_w, pre_ffn_norm_w, w_up, ffn_conv_w, ffn_conv_b, w_down, post_ffn_norm_w, loss_target, m_pre_mix_norm_w, m_w_in, m_ssd_conv_w, m_ssd_conv_b, m_ssd_dt_bias, m_ssd_a_log, m_ssd_d, m_ssd_norm_w, m_ret_norm_w, m_w_out, m_post_mix_norm_w, m_pre_ffn_norm_w, m_w_up, m_ffn_conv_w, m_ffn_conv_b, m_w_down, m_post_ffn_norm_w, v_pre_mix_norm_w, v_w_in, v_ssd_conv_w, v_ssd_conv_b, v_ssd_dt_bias, v_ssd_a_log, v_ssd_d, v_ssd_norm_w, v_ret_norm_w, v_w_out, v_post_mix_norm_w, v_pre_ffn_norm_w, v_w_up, v_ffn_conv_w, v_ffn_conv_b, v_w_down, v_post_ffn_norm_w):
    T = x.shape[1]
    xi, tgt = x[0], loss_target[0]
    me = 4 * lax.axis_index("x") + 2 * lax.axis_index("y") + lax.axis_index("c")
    n_in, n_up = w_in.shape[2], w_up.shape[2]
    n_out, n_down = w_out.shape[1], w_down.shape[1]
    n_sc, n_fc = ssd_conv_w.shape[2], ffn_conv_w.shape[2]

    def after(token, value):
        return value * (1.0 + token[0, 0])

    def finish(started, after_value, name, same):
        bufs, lands = _exchange_wait(started, after_value, name, same)
        own = [b if same else lax.dynamic_index_in_dim(b, me, 0, keepdims=False) for b in bufs]
        return [lax.dynamic_update_index_in_dim(l, o, me, 0) for l, o in zip(lands, own)]

    tr_ = lambda w: jnp.transpose(w[0])
    gat_in = _exchange_start([tr_(w_in).astype(BF)], "gather_in_start", True, ONE_PER_CHIP)
    gat_conv = _exchange_start([after(gat_in[4], _pack([ssd_conv_w, ffn_conv_w], CONV_ROWS))], "gather_conv_start", True)
    pad_h = lambda p: jnp.pad(p, ((0, 0), (0, LANES - SSD_HEADS)))
    dtb, alog = pad_h(ssd_dt_bias), pad_h(ssd_a_log)
    dskx = jnp.repeat(ssd_d, SSD_HEAD_DIM, axis=1)
    inv = ROPE_BASE ** (-jnp.arange(0, RET_QK, 2, dtype=F32) / RET_QK)
    ang = after(gat_conv[4], jnp.arange(T, dtype=F32)[:, None]) * inv[None, :]
    cs_, sn_ = jnp.cos(ang), jnp.sin(ang)
    cos = jnp.concatenate([cs_, cs_, cs_, cs_], axis=1)
    sin = jnp.concatenate([-sn_, sn_, -sn_, sn_], axis=1)
    rtab = _ret_tables()
    ws = [pre_mix_norm_w, ssd_conv_w, ssd_conv_b, ssd_dt_bias, ssd_a_log, ssd_d, ssd_norm_w, ret_norm_w, post_mix_norm_w,
          pre_ffn_norm_w, ffn_conv_w, ffn_conv_b, post_ffn_norm_w]
    ms = [m_pre_mix_norm_w, m_ssd_conv_w, m_ssd_conv_b, m_ssd_dt_bias, m_ssd_a_log, m_ssd_d, m_ssd_norm_w, m_ret_norm_w,
          m_post_mix_norm_w, m_pre_ffn_norm_w, m_ffn_conv_w, m_ffn_conv_b, m_post_ffn_norm_w]
    vs = [v_pre_mix_norm_w, v_ssd_conv_w, v_ssd_conv_b, v_ssd_dt_bias, v_ssd_a_log, v_ssd_d, v_ssd_norm_w, v_ret_norm_w,
          v_post_mix_norm_w, v_pre_ffn_norm_w, v_ffn_conv_w, v_ffn_conv_b, v_post_ffn_norm_w]
    out_shapes = [t.shape for t in ws]
    small_wmv = [_pack(t, SMALL_ROWS) for t in (ws, ms, vs)]
    shard_in, land_in = _exchange_wait(gat_in, [cos, sin, rtab] + small_wmv, "gather_in_wait", True, ONE_PER_CHIP)
    fwd_in_ = _forward_start(land_in[0], "gather_in_forward")
    gat_rest = _exchange_start([after(fwd_in_[3], w).astype(BF) for w in (w_out[0], tr_(w_up), w_down[0])], "gather_rest_start", True)
    g_in = lax.dynamic_update_index_in_dim(_forward_wait(fwd_in_, gat_rest[4], "gather_in_forward_wait"), shard_in[0], me, 0)
    wt = g_in.reshape(N_DEV * n_in, D_MODEL)
    wdt = jnp.pad(wt[O_DT:O_Q], ((0, LANES - SSD_HEADS), (0, 0)))

    h, z, xbc, q, k, v, g, dtr = _fwd_in(xi, pre_mix_norm_w, wt, wdt)
    yr, rst = _ret_fwd(q, k, v, g, cos, sin, ret_norm_w, rtab)
    gconv, = finish(gat_conv, yr, "gather_conv_wait", True)
    convs = [_unpack(gconv[d], [(SSD_CONV, n_sc), (FFN_CONV, n_fc)]) for d in range(N_DEV)]
    scw = jnp.pad(jnp.concatenate([c[0] for c in convs], axis=1), ((0, 8 - SSD_CONV), (0, 0)))
    fcw = jnp.pad(jnp.concatenate([c[1] for c in convs], axis=1), ((0, 8 - FFN_CONV), (0, 0)))
    ys, ypre, uconv, sst = _ssd_fwd(xbc, dtr, z, scw, ssd_conv_b, dtb, alog, dskx, ssd_norm_w)
    g_out, g_up, g_down = finish(gat_rest, [yr, ys], "gather_rest_wait", True)
    wout = g_out.reshape(N_DEV * n_out, D_MODEL)
    wup = g_up.reshape(N_DEV * n_up, D_MODEL)
    wdown = g_down.reshape(N_DEV * n_down, D_MODEL)
    y, x1, h2, graw, val = _fwd_mid(ys, yr, xi, wout, post_mix_norm_w, pre_ffn_norm_w, wup)
    a, dfb, dval, dgate, dx2, lossb, d_pff, d_fcb = _ffn_tail(graw, val, x1, tgt, fcw, ffn_conv_b, wdown, post_ffn_norm_w)
    gdown = _matmul_tn(a, dfb, "dw_down")
    sc_down = _exchange_start([gdown.reshape(N_DEV, n_down, D_MODEL)], "scatter_down_start", False)
    dgraw, dx1, dyb, dys, dyr, d_fcw, d_pf, d_pm = _ffn_bwd(dgate, dval, graw, x1, dx2, y, after(sc_down[4], fcw), wup,
                                                         pre_ffn_norm_w, post_mix_norm_w, wout)
    gup = _matmul_tn(dval, h2, "dw_up_v", stack=(2, 1, _matmul_tn(dgraw, h2, "dw_up_g", stack=(2, 0, None))))
    gout = jnp.concatenate(_matmul_tn_group([ys, yr], dyb, "dw_out"), axis=0)
    sc_mid = _exchange_start([gup.reshape(N_DEV, n_up, D_MODEL), gout.reshape(N_DEV, n_out, D_MODEL)],
                             "scatter_mid_start", False)
    dz, dxbc, ddt, d_scw, d_scb, d_dtb, d_alog, d_dsk, d_snw = _ssd_bwd(dys, ypre, uconv, xbc, dtr, z, sst, after(sc_mid[4], scw),
                                                                      dtb, alog, dskx, ssd_norm_w)
    dq, dk, dv, dg, d_rnw = _ret_bwd(dyr, q, k, v, g, rst, cos, sin, ret_norm_w, rtab)
    g_q, g_k, g_v, g_g = _matmul_tn_group([dq, dk, dv, dg], h, "dw_ret")
    g_z, g_xbc, g_dt = _matmul_tn_group([dz, dxbc, ddt], h, "dw_ssd")
    gin = jnp.concatenate([g_z, g_xbc, g_dt[:SSD_HEADS], g_q, g_k, g_v, g_g], axis=0)
    sc_in = _exchange_start([gin.reshape(N_DEV, n_in, D_MODEL)], "scatter_in_start", False)
    gx, d_w0 = _in_bwd(dz, dxbc, dq, dk, dv, dg, ddt, xi, dx1, after(sc_in[4], pre_mix_norm_w), wt, wdt)
    small_full = [d_w0, d_scw[:SSD_CONV], d_scb, d_dtb[:, :SSD_HEADS], d_alog[:, :SSD_HEADS], d_dsk[:, :SSD_HEADS], d_snw, d_rnw,
                  d_pm, d_pf, d_fcw[:FFN_CONV], d_fcb, d_pff, lossb[0:1, 0:1]]
    gat_small = _exchange_start([_pack(small_full, SMALL_ROWS)], "gather_small_start", True)
    r_down, = finish(sc_down, [gx, gat_small[4]], "scatter_down_wait", False)
    r_up, r_out = finish(sc_mid, r_down, "scatter_mid_wait", False)
    per_w = [None] * 4
    per_w[3] = _adamw(r_down, w_down[0], m_w_down[0], v_w_down[0], "adamw_down", n_down)
    per_w[2] = [jnp.transpose(t) for t in _adamw(r_up, tr_(w_up), tr_(m_w_up), tr_(v_w_up), "adamw_up", n_up, 256)]
    per_w[1] = _adamw(r_out, w_out[0], m_w_out[0], v_w_out[0], "adamw_out", n_out)
    r_in, = finish(sc_in, per_w[1][0], "scatter_in_wait", False)
    per_w[0] = [jnp.transpose(t) for t in _adamw(r_in, tr_(w_in), tr_(m_w_in), tr_(v_w_in), "adamw_in", n_in, 256)]
    big = [[per_w[i][kind][None] for i in range(4)] for kind in range(4)]

    full_shapes = [t.shape for t in small_full]
    gs = _sum_slabs(finish(gat_small, per_w[0][0], "gather_small_wait", True)[0], "sum_small")
    gfull = _unpack(gs, full_shapes)
    gfull[1] = lax.dynamic_slice_in_dim(gfull[1], me * n_sc, n_sc, axis=1)
    gfull[10] = lax.dynamic_slice_in_dim(gfull[10], me * n_fc, n_fc, axis=1)
    loss = gfull.pop()[0, 0]
    small = _adamw(_pack(gfull, SMALL_ROWS)[None], *small_wmv, "adamw_small", SMALL_ROWS)
    small = [_unpack(b, out_shapes) for b in small]

    order = {"pre_mix_norm_w": ("s", 0), "w_in": ("b", 0), "ssd_conv_w": ("s", 1), "ssd_conv_b": ("s", 2),
             "ssd_dt_bias": ("s", 3), "ssd_a_log": ("s", 4), "ssd_d": ("s", 5), "ssd_norm_w": ("s", 6), "ret_norm_w": ("s", 7),
             "w_out": ("b", 1), "post_mix_norm_w": ("s", 8), "pre_ffn_norm_w": ("s", 9), "w_up": ("b", 2),
             "ffn_conv_w": ("s", 10), "ffn_conv_b": ("s", 11), "w_down": ("b", 3), "post_ffn_norm_w": ("s", 12)}
    outs = [loss, gx[None]]
    for kind in range(4):
        for name, (grp, idx) in order.items():
            outs.append(big[kind][idx] if grp == "b" else small[kind][idx])
    return tuple(outs)
```

```python
import functools
import math

import numpy as np
import jax
import jax.numpy as jnp
from jax import lax
from jax.experimental import pallas as pl
from jax.experimental.pallas import tpu as pltpu

F32 = jnp.float32
BF = jnp.bfloat16
HI = lax.Precision.HIGHEST
S = jax.ShapeDtypeStruct

D_MODEL = 1024
SSD_HEADS = 16
SSD_HEAD_DIM = 64
SSD_GROUPS = 2
SSD_STATE = 128
SSD_WIDTH = 1024
SSD_XBC = 1536
SSD_CONV = 4
RET_HEADS = 8
RET_QK = 64
RET_V = 128
RET_QK_W = 512
RET_V_W = 1024
ROPE_BASE = 10000.0
CH = 128
D_FF = 2816
FFN_CONV = 3
EPS = 1e-6
IN_WIDTH = 5648
N_DEV = 8

ADAM_LR = 0.001
ADAM_B1 = 0.9
ADAM_B2 = 0.999
ADAM_EPS = 1e-08
ADAM_WD = 0.01
ADAM_STEP = 10

LANES = 128
HALO = 16
VMEM_LIMIT = 48 * 1024 * 1024

O_Z, O_XBC, O_DT, O_Q, O_K, O_V, O_G, O_END = 0, 1024, 2560, 2576, 3088, 3600, 4624, 5648
IN_SEGMENTS = ((O_Z, O_XBC), (O_XBC, O_DT), (O_Q, O_K), (O_K, O_V), (O_V, O_G), (O_G, O_END))


def _cparams(*sem):
    return pltpu.CompilerParams(dimension_semantics=sem, vmem_limit_bytes=VMEM_LIMIT)


def _dot(a, b):
    return jnp.dot(a.astype(BF), b.astype(BF), preferred_element_type=F32)


def _dot_nt(a, b):
    return lax.dot_general(a.astype(BF), b.astype(BF), (((1,), (1,)), ((), ())), preferred_element_type=F32)


def _dot_tn(a, b):
    return lax.dot_general(a.astype(BF), b.astype(BF), (((0,), (0,)), ((), ())), preferred_element_type=F32)


def _dot_hi(a, b):
    return jnp.dot(a, b, preferred_element_type=F32, precision=HI)


def _dot_tn_hi(a, b):
    return lax.dot_general(a, b, (((0,), (0,)), ((), ())), preferred_element_type=F32, precision=HI)


def _sigmoid(x):
    return jax.nn.sigmoid(x)


def _dsilu(x, s):
    return s * (1.0 + x * (1.0 - s))


def _softplus(x):
    return jnp.maximum(x, 0.0) + jnp.log1p(jnp.exp(-jnp.abs(x)))


def _rstd(x):
    return lax.rsqrt(jnp.mean(x * x, axis=-1, keepdims=True) + EPS)


def _rms_bwd(dy, x, r, w):
    gn = dy * w
    dx = r * gn - x * (r * r * r) * jnp.mean(gn * x, axis=-1, keepdims=True)
    dw = jnp.sum(dy * x * r, axis=0, keepdims=True)
    return dx, dw


def _rows_before(ext, s, head, n):
    if s == 0:
        return ext[head:head + n]
    return pltpu.roll(ext, s, 0)[head:head + n]


def _rows_after(ext, s, n):
    if s == 0:
        return ext[0:n]
    return pltpu.roll(ext, ext.shape[0] - s, 0)[0:n]


def _row_spec(tm, width):
    return pl.BlockSpec((tm, width), lambda i: (i, 0))


def _const_spec(shape):
    return pl.BlockSpec(shape, lambda i: (0,) * len(shape))


_VMEM_WHOLE = pl.BlockSpec(memory_space=pltpu.VMEM)


def _fwd_in(x, w0, wt, wdt, tm=512):
    T = x.shape[0]

    def body(x_ref, w0_ref, wt_ref, wdt_ref, h_ref, z_ref, xbc_ref, q_ref, k_ref, v_ref, g_ref, dt_ref):
        xf = x_ref[...]
        h = (xf * _rstd(xf) * w0_ref[...]).astype(BF)
        h_ref[...] = h
        for ref, (lo, hi) in zip((z_ref, xbc_ref, q_ref, k_ref, v_ref, g_ref), IN_SEGMENTS):
            ref[...] = _dot_nt(h, wt_ref[lo:hi, :]).astype(ref.dtype)
        dt_ref[...] = _dot_nt(h, wdt_ref[...])

    widths = (D_MODEL, SSD_WIDTH, SSD_XBC, RET_QK_W, RET_QK_W, RET_V_W, RET_V_W)
    return pl.pallas_call(
        body, name="fwd_in", grid=(T // tm,),
        in_specs=[_row_spec(tm, D_MODEL), _const_spec((1, D_MODEL)), _VMEM_WHOLE, _VMEM_WHOLE],
        out_specs=[_row_spec(tm, w) for w in widths] + [_row_spec(tm, LANES)],
        out_shape=[S((T, w), BF) for w in widths] + [S((T, LANES), F32)],
        compiler_params=_cparams("parallel"),
    )(x, w0, wt, wdt)


def _fwd_mid(ys, yr, x, wout, wpm, wpf, wup, tm=512):
    T = x.shape[0]

    def body(ys_ref, yr_ref, x_ref, wout_ref, wpm_ref, wpf_ref, wup_ref, y_ref, x1_ref, h2_ref, graw_ref, val_ref):
        y = (jnp.dot(ys_ref[...], wout_ref[0:SSD_WIDTH, :], preferred_element_type=F32)
             + jnp.dot(yr_ref[...], wout_ref[SSD_WIDTH:, :], preferred_element_type=F32))
        y_ref[...] = y
        x1 = x_ref[...] + y * _rstd(y) * wpm_ref[...]
        x1_ref[...] = x1
        h2 = (x1 * _rstd(x1) * wpf_ref[...]).astype(BF)
        h2_ref[...] = h2
        graw_ref[...] = _dot_nt(h2, wup_ref[0:D_FF, :]).astype(BF)
        val_ref[...] = _dot_nt(h2, wup_ref[D_FF:, :]).astype(BF)

    return pl.pallas_call(
        body, name="fwd_mid", grid=(T // tm,),
        in_specs=[_row_spec(tm, SSD_WIDTH), _row_spec(tm, RET_V_W), _row_spec(tm, D_MODEL), _VMEM_WHOLE,
                  _const_spec((1, D_MODEL)), _const_spec((1, D_MODEL)), _VMEM_WHOLE],
        out_specs=[_row_spec(tm, D_MODEL), _row_spec(tm, D_MODEL), _row_spec(tm, D_MODEL), _row_spec(tm, D_FF),
                   _row_spec(tm, D_FF)],
        out_shape=[S((T, D_MODEL), F32), S((T, D_MODEL), F32), S((T, D_MODEL), BF), S((T, D_FF), BF), S((T, D_FF), BF)],
        compiler_params=_cparams("parallel"),
    )(ys, yr, x, wout, wpm, wpf, wup)


def _ffn_tail(graw, val, x1, tgt, convw, convb, wdown, wpff, tm=256):
    T = x1.shape[0]

    def body(graw_ref, val_ref, x1_ref, tgt_ref, cw_ref, cb_ref, wd_ref, wpff_ref,
             a_ref, df_ref, dval_ref, dgate_ref, dx2_ref, loss_ref, dwpff_ref, dcb_ref, carry):
        i = pl.program_id(0)

        @pl.when(i == 0)
        def _():
            carry[...] = jnp.zeros_like(carry)
            loss_ref[...] = jnp.zeros_like(loss_ref)
            dwpff_ref[...] = jnp.zeros_like(dwpff_ref)
            dcb_ref[...] = jnp.zeros_like(dcb_ref)

        g = graw_ref[...].astype(F32)
        ext = jnp.concatenate([carry[...], g], axis=0)
        carry[...] = g[tm - 8:tm]
        gate = cb_ref[...] + sum(cw_ref[j:j + 1, :] * _rows_before(ext, FFN_CONV - 1 - j, 8, tm) for j in range(FFN_CONV))
        gate = gate.astype(BF)
        sg = _sigmoid(gate)
        silu = gate * sg
        v = val_ref[...]
        a = silu * v
        a_ref[...] = a
        f = jnp.dot(a, wd_ref[...], preferred_element_type=F32)
        r = _rstd(f)
        w = wpff_ref[...]
        e = x1_ref[...] + f * r * w - tgt_ref[...]
        loss_ref[...] += jnp.sum(e * e) * (0.5 / D_MODEL)
        dx2 = e * (1.0 / D_MODEL)
        dx2_ref[...] = dx2
        df, dw = _rms_bwd(dx2, f, r, w)
        dwpff_ref[...] += dw
        dfb = df.astype(BF)
        df_ref[...] = dfb
        da = _dot_nt(dfb, wd_ref[...]).astype(BF)
        dval_ref[...] = da * silu
        dgate = da * v * _dsilu(gate, sg)
        dcb_ref[...] += jnp.sum(dgate.astype(F32), axis=0, keepdims=True)
        dgate_ref[...] = dgate

    return pl.pallas_call(
        body, name="ffn_tail", grid=(T // tm,),
        in_specs=[_row_spec(tm, D_FF), _row_spec(tm, D_FF), _row_spec(tm, D_MODEL), _row_spec(tm, D_MODEL),
                  _const_spec((8, D_FF)), _const_spec((1, D_FF)), _VMEM_WHOLE, _const_spec((1, D_MODEL))],
        out_specs=[_row_spec(tm, D_FF), _row_spec(tm, D_MODEL), _row_spec(tm, D_FF), _row_spec(tm, D_FF),
                   _row_spec(tm, D_MODEL), _const_spec((8, LANES)), _const_spec((1, D_MODEL)), _const_spec((1, D_FF))],
        out_shape=[S((T, D_FF), BF), S((T, D_MODEL), BF), S((T, D_FF), BF), S((T, D_FF), BF), S((T, D_MODEL), F32),
                   S((8, LANES), F32), S((1, D_MODEL), F32), S((1, D_FF), F32)],
        scratch_shapes=[pltpu.VMEM((8, D_FF), F32)],
        compiler_params=_cparams("arbitrary"),
    )(graw, val, x1, tgt, convw, convb, wdown, wpff)


def _ffn_bwd(dgate, dval, graw, x1, dx2, y, convw, wup, wpf, wpm, wout, tm=256):
    T = x1.shape[0]
    nt = T // tm
    rev = lambda i: (nt - 1 - i, 0)
    rspec = lambda w: pl.BlockSpec((tm, w), rev)

    def body(dgate_ref, dval_ref, graw_ref, x1_ref, dx2_ref, y_ref, cw_ref, wup_ref, wpf_ref, wpm_ref, wout_ref,
             dgraw_ref, dx1_ref, dy_ref, dys_ref, dyr_ref, dcw_ref, dwpf_ref, dwpm_ref, carry):
        i = pl.program_id(0)

        @pl.when(i == 0)
        def _():
            carry[...] = jnp.zeros_like(carry)
            dcw_ref[...] = jnp.zeros_like(dcw_ref)
            dwpf_ref[...] = jnp.zeros_like(dwpf_ref)
            dwpm_ref[...] = jnp.zeros_like(dwpm_ref)

        dg = dgate_ref[...].astype(F32)
        ext = jnp.concatenate([dg, carry[...]], axis=0)
        carry[...] = dg[0:8]
        g = graw_ref[...].astype(F32)
        dgraw = jnp.zeros((tm, D_FF), F32)
        for j in range(FFN_CONV):
            sj = _rows_after(ext, FFN_CONV - 1 - j, tm)
            dgraw = dgraw + cw_ref[j:j + 1, :] * sj
            dcw_ref[j:j + 1, :] += jnp.sum(sj * g, axis=0, keepdims=True)
        dgrawb = dgraw.astype(BF)
        dgraw_ref[...] = dgrawb
        dh2 = _dot(dgrawb, wup_ref[0:D_FF, :]) + _dot(dval_ref[...], wup_ref[D_FF:, :])
        x1 = x1_ref[...]
        dxa, dw = _rms_bwd(dh2, x1, _rstd(x1), wpf_ref[...])
        dwpf_ref[...] += dw
        dx1 = dx2_ref[...] + dxa
        dx1_ref[...] = dx1
        yv = y_ref[...]
        dy, dw = _rms_bwd(dx1, yv, _rstd(yv), wpm_ref[...])
        dwpm_ref[...] += dw
        dyb = dy.astype(BF)
        dy_ref[...] = dyb
        dys_ref[...] = _dot_nt(dyb, wout_ref[0:SSD_WIDTH, :]).astype(BF)
        dyr_ref[...] = _dot_nt(dyb, wout_ref[SSD_WIDTH:, :]).astype(BF)

    return pl.pallas_call(
        body, name="ffn_bwd", grid=(nt,),
        in_specs=[rspec(D_FF), rspec(D_FF), rspec(D_FF), rspec(D_MODEL), rspec(D_MODEL), rspec(D_MODEL),
                  _const_spec((8, D_FF)), _VMEM_WHOLE, _const_spec((1, D_MODEL)), _const_spec((1, D_MODEL)), _VMEM_WHOLE],
        out_specs=[rspec(D_FF), rspec(D_MODEL), rspec(D_MODEL), rspec(SSD_WIDTH), rspec(RET_V_W),
                   _const_spec((8, D_FF)), _const_spec((1, D_MODEL)), _const_spec((1, D_MODEL))],
        out_shape=[S((T, D_FF), BF), S((T, D_MODEL), F32), S((T, D_MODEL), BF), S((T, SSD_WIDTH), BF), S((T, RET_V_W), BF),
                   S((8, D_FF), F32), S((1, D_MODEL), F32), S((1, D_MODEL), F32)],
        scratch_shapes=[pltpu.VMEM((8, D_FF), F32)],
        compiler_params=_cparams("arbitrary"),
    )(dgate, dval, graw, x1, dx2, y, convw, wup, wpf, wpm, wout)


def _in_bwd(dz, dxbc, dq, dk, dv, dg, ddt, x, dx1, w0, wt, wdt, tm=512):
    T = x.shape[0]

    def body(dz_ref, dxbc_ref, dq_ref, dk_ref, dv_ref, dg_ref, ddt_ref, x_ref, dx1_ref, w0_ref, wt_ref, wdt_ref, gx_ref, dw0_ref):
        @pl.when(pl.program_id(0) == 0)
        def _():
            dw0_ref[...] = jnp.zeros_like(dw0_ref)

        dh = _dot(ddt_ref[...], wdt_ref[...])
        for ref, (lo, hi) in zip((dz_ref, dxbc_ref, dq_ref, dk_ref, dv_ref, dg_ref), IN_SEGMENTS):
            dh = dh + _dot(ref[...], wt_ref[lo:hi, :])
        xf = x_ref[...]
        dx, dw = _rms_bwd(dh, xf, _rstd(xf), w0_ref[...])
        dw0_ref[...] += dw
        gx_ref[...] = dx1_ref[...] + dx

    widths = (SSD_WIDTH, SSD_XBC, RET_QK_W, RET_QK_W, RET_V_W, RET_V_W, LANES)
    return pl.pallas_call(
        body, name="in_bwd", grid=(T // tm,),
        in_specs=[_row_spec(tm, w) for w in widths] + [_row_spec(tm, D_MODEL), _row_spec(tm, D_MODEL),
                                                       _const_spec((1, D_MODEL)), _VMEM_WHOLE, _VMEM_WHOLE],
        out_specs=[_row_spec(tm, D_MODEL), _const_spec((1, D_MODEL))],
        out_shape=[S((T, D_MODEL), F32), S((1, D_MODEL), F32)],
        compiler_params=_cparams("arbitrary"),
    )(dz, dxbc, dq, dk, dv, dg, ddt, x, dx1, w0, wt, wdt)


DW_TILE_BYTES = 6 << 20


def _matmul_tn(a, b, name, tk=1024, stack=None):
    T, M = a.shape
    N = b.shape[1]
    parts, index, below = stack if stack is not None else (1, 0, None)
    tm_, tn = M, N
    while tm_ * tn * 4 > DW_TILE_BYTES:
        if tm_ >= tn and tm_ % 256 == 0:
            tm_ //= 2
        elif tn % 256 == 0:
            tn //= 2
        else:
            break
    nk = T // tk

    def body(a_ref, b_ref, *rest):
        o_ref, acc = rest[-2], rest[-1]
        k = pl.program_id(2)

        @pl.when(k == 0)
        def _():
            acc[...] = jnp.zeros_like(acc)

        acc[...] += _dot_tn(a_ref[...], b_ref[...])

        @pl.when(k == nk - 1)
        def _():
            o_ref[...] = acc[...].astype(o_ref.dtype)

    first_block = index * (M // tm_)
    extra = [] if below is None else [below]
    return pl.pallas_call(
        body, name=name, grid=(M // tm_, N // tn, nk),
        in_specs=[pl.BlockSpec((tk, tm_), lambda m, n, k: (k, m)), pl.BlockSpec((tk, tn), lambda m, n, k: (k, n))]
        + [pl.BlockSpec(memory_space=pl.ANY)] * len(extra),
        out_specs=pl.BlockSpec((tm_, tn), lambda m, n, k: (first_block + m, n)),
        out_shape=S((parts * M, N), BF),
        scratch_shapes=[pltpu.VMEM((tm_, tn), F32)],
        input_output_aliases={2: 0} if extra else {},
        compiler_params=_cparams("parallel", "parallel", "arbitrary"),
    )(a, b, *extra)


def _matmul_tn_group(as_, b, name, tk=1024):
    T, N = b.shape
    na = len(as_)
    nk = T // tk

    def body(*refs):
        a_refs, b_ref, o_refs, accs = refs[:na], refs[na], refs[na + 1:2 * na + 1], refs[2 * na + 1:]
        k = pl.program_id(0)

        @pl.when(k == 0)
        def _():
            for acc in accs:
                acc[...] = jnp.zeros_like(acc)

        bt = b_ref[...]
        for a_ref, acc in zip(a_refs, accs):
            acc[...] += _dot_tn(a_ref[...], bt)

        @pl.when(k == nk - 1)
        def _():
            for o_ref, acc in zip(o_refs, accs):
                o_ref[...] = acc[...].astype(o_ref.dtype)

    return pl.pallas_call(
        body, name=name, grid=(nk,),
        in_specs=[_row_spec(tk, a.shape[1]) for a in as_] + [_row_spec(tk, N)],
        out_specs=[_const_spec((a.shape[1], N)) for a in as_],
        out_shape=[S((a.shape[1], N), BF) for a in as_],
        scratch_shapes=[pltpu.VMEM((a.shape[1], N), F32) for a in as_],
        compiler_params=_cparams("arbitrary"),
    )(*as_, b)


def _tri(lower):
    r = lax.broadcasted_iota(jnp.int32, (CH, CH), 0)
    c = lax.broadcasted_iota(jnp.int32, (CH, CH), 1)
    return ((c <= r) if lower else (r <= c)).astype(F32)


def _ssd_conv(xc_ref, xh_ref, cw_ref, cb_ref, first):
    xc = xc_ref[...].astype(F32)
    xh = jnp.where(first, 0.0, xh_ref[...].astype(F32))
    ext = jnp.concatenate([xh, xc], axis=0)
    return cb_ref[...] + sum(cw_ref[j:j + 1, :] * _rows_before(ext, SSD_CONV - 1 - j, HALO, CH) for j in range(SSD_CONV))


def _ssd_decay(dtr_ref, dtb_ref, alog_ref):
    dt = _softplus(dtr_ref[...] + dtb_ref[...])
    a = -jnp.exp(alog_ref[...])
    da = dt * a
    cs = _dot_hi(_tri(True), da)
    cst = _dot_tn_hi(da, _tri(False))
    return dt, a, cs, cst


HPG = SSD_HEADS // SSD_GROUPS
GW = HPG * SSD_HEAD_DIM


def _expand_heads(src, buf):
    for h in range(SSD_HEADS):
        buf[:, h * SSD_HEAD_DIM:(h + 1) * SSD_HEAD_DIM] = jnp.broadcast_to(src[:, h:h + 1], (CH, SSD_HEAD_DIM))


def _ssd_expanded(act, dt, cs, dtx, csx):
    _expand_heads(dt, dtx)
    _expand_heads(cs, csx)
    csv = csx[...]
    last = csv[CH - 1:CH, :]
    e_exp = jnp.exp(csv)
    dec_exp = jnp.exp(last - csv)
    el_exp = jnp.exp(last)
    xs = act[:, 0:SSD_WIDTH]
    xdt = xs * dtx[...]
    return xs, xdt, xdt * dec_exp, e_exp, dec_exp, el_exp


def _decay_mats(h, cs, cst, transposed):
    r = lax.broadcasted_iota(jnp.int32, (CH, CH), 0)
    c = lax.broadcasted_iota(jnp.int32, (CH, CH), 1)
    c_col = cs[:, h:h + 1]
    c_row = cst[h:h + 1, :]
    if transposed:
        return jnp.exp(jnp.where(r <= c, c_row - c_col, -1e30))
    return jnp.exp(jnp.where(r >= c, c_col - c_row, -1e30))


def _ssd_specs(T):
    nc = T // CH
    return nc, [
        _row_spec(CH, SSD_XBC),
        pl.BlockSpec((HALO, SSD_XBC), lambda i: (jnp.maximum(i * (CH // HALO) - 1, 0), 0)),
        _row_spec(CH, LANES),
        _row_spec(CH, SSD_WIDTH),
    ]


def _groups(act):
    bm = [act[:, SSD_WIDTH + g * SSD_STATE:SSD_WIDTH + (g + 1) * SSD_STATE] for g in range(SSD_GROUPS)]
    o = SSD_WIDTH + SSD_GROUPS * SSD_STATE
    cm = [act[:, o + g * SSD_STATE:o + (g + 1) * SSD_STATE] for g in range(SSD_GROUPS)]
    return bm, cm


def _ssd_fwd(xbc, dtr, z, convw, convb, dtb, alog, dskx, nw):
    T = xbc.shape[0]
    nc, specs = _ssd_specs(T)

    def body(xc_ref, xh_ref, dtr_ref, z_ref, cw_ref, cb_ref, dtb_ref, alog_ref, dskx_ref, nw_ref,
             out_ref, y_ref, u_ref, st_ref, state, ybuf, dtx, csx):
        i = pl.program_id(0)

        @pl.when(i == 0)
        def _():
            state[...] = jnp.zeros_like(state)

        u = _ssd_conv(xc_ref, xh_ref, cw_ref, cb_ref, i == 0)
        u_ref[...] = u.astype(BF)
        act = u * _sigmoid(u)
        dt, a, cs, cst = _ssd_decay(dtr_ref, dtb_ref, alog_ref)
        xs, xdt, w, e_exp, dec_exp, el_exp = _ssd_expanded(act, dt, cs, dtx, csx)
        bm, cm = _groups(act)
        groups, heads = range(SSD_GROUPS), range(SSD_HEADS)
        gsl = [slice(g * GW, (g + 1) * GW) for g in groups]
        hsl = [slice(h * SSD_HEAD_DIM, (h + 1) * SSD_HEAD_DIM) for h in heads]
        cb = [_dot_nt(cm[g], bm[g]) for g in groups]
        yoff = [_dot(cm[g], state[g]) for g in groups]
        sloc = [_dot_tn(bm[g], w[:, gsl[g]]) for g in groups]
        lm = [_decay_mats(h, cs, cst, False) for h in heads]
        ydiag = [_dot(cb[h // HPG] * lm[h], xdt[:, hsl[h]]) for h in heads]
        for g in groups:
            st_ref[0, g] = state[g]
            ybuf[:, gsl[g]] = yoff[g] * e_exp[:, gsl[g]] + xs[:, gsl[g]] * dskx_ref[:, gsl[g]]
            state[g] = state[g] * el_exp[:, gsl[g]] + sloc[g]
        for h in heads:
            ybuf[:, hsl[h]] += ydiag[h]
        yv = ybuf[...]
        y_ref[...] = yv.astype(BF)
        zf = z_ref[...].astype(F32)
        gated = yv * (zf * _sigmoid(zf))
        out_ref[...] = (gated * _rstd(gated) * nw_ref[...]).astype(BF)

    st_spec = pl.BlockSpec((1, SSD_GROUPS, SSD_STATE, GW), lambda i: (i, 0, 0, 0))
    return pl.pallas_call(
        body, name="ssd_fwd", grid=(nc,),
        in_specs=specs + [_const_spec((8, SSD_XBC)), _const_spec((1, SSD_XBC)), _const_spec((1, LANES)),
                          _const_spec((1, LANES)), _const_spec((1, SSD_WIDTH)), _const_spec((1, SSD_WIDTH))],
        out_specs=[_row_spec(CH, SSD_WIDTH), _row_spec(CH, SSD_WIDTH), _row_spec(CH, SSD_XBC), st_spec],
        out_shape=[S((T, SSD_WIDTH), BF), S((T, SSD_WIDTH), BF), S((T, SSD_XBC), BF), S((nc, SSD_GROUPS, SSD_STATE, GW), F32)],
        scratch_shapes=[pltpu.VMEM((SSD_GROUPS, SSD_STATE, GW), F32), pltpu.VMEM((CH, SSD_WIDTH), F32),
                        pltpu.VMEM((CH, SSD_WIDTH), F32), pltpu.VMEM((CH, SSD_WIDTH), F32)],
        compiler_params=_cparams("arbitrary"),
    )(xbc, xbc, dtr, z, convw, convb, dtb, alog, dskx, nw)


def _ssd_bwd(dout, y, u, xbc, dtr, z, states, convw, dtb, alog, dskx, nw):
    T = xbc.shape[0]
    nc = T // CH
    rev = lambda i: (nc - 1 - i, 0)
    rspec = lambda w: pl.BlockSpec((CH, w), rev)
    NB = SSD_WIDTH
    NC_ = SSD_WIDTH + SSD_GROUPS * SSD_STATE

    def body(do_ref, y_ref, u_ref, xc_ref, dtr_ref, z_ref, st_ref, cw_ref, dtb_ref, alog_ref, dskx_ref, nw_ref,
             dz_ref, dxbc_ref, ddt_ref, dcw_ref, dcb_ref, ddtb_ref, dalog_ref, ddsk_ref, dnw_ref,
             dstate, ducarry, dtx, csx, dxdtbuf, dact):
        i = pl.program_id(0)

        @pl.when(i == 0)
        def _():
            dstate[...] = jnp.zeros_like(dstate)
            ducarry[...] = jnp.zeros_like(ducarry)
            for ref in (dcw_ref, dcb_ref, ddtb_ref, dalog_ref, ddsk_ref, dnw_ref):
                ref[...] = jnp.zeros_like(ref)

        xc = xc_ref[...].astype(F32)
        u = u_ref[...].astype(F32)
        sg = _sigmoid(u)
        act = u * sg
        dt, a, cs, cst = _ssd_decay(dtr_ref, dtb_ref, alog_ref)
        xs, xdt, w, e_exp, dec_exp, el_exp = _ssd_expanded(act, dt, cs, dtx, csx)
        bm, cm = _groups(act)
        yv = y_ref[...].astype(F32)
        zf = z_ref[...].astype(F32)
        sz = _sigmoid(zf)
        gated = yv * (zf * sz)
        dgated, dnw = _rms_bwd(do_ref[...].astype(F32), gated, _rstd(gated), nw_ref[...])
        dnw_ref[...] += dnw
        dz_ref[...] = (dgated * yv * _dsilu(zf, sz)).astype(BF)
        dy = dgated * (zf * sz)
        lane_of = lax.broadcasted_iota(jnp.int32, (SSD_WIDTH, LANES), 0) - SSD_HEAD_DIM * lax.broadcasted_iota(jnp.int32, (SSD_WIDTH, LANES), 1)
        expt = ((lane_of >= 0) & (lane_of < SSD_HEAD_DIM)).astype(F32)
        ddsk_ref[...] += _dot_hi(jnp.sum(dy * xs, axis=0, keepdims=True), expt)
        dcs = jnp.zeros((CH, LANES), F32)
        dcst = jnp.zeros((LANES, CH), F32)
        ddt = jnp.zeros((CH, LANES), F32)
        lane_id = lax.broadcasted_iota(jnp.int32, (CH, LANES), 1)
        row_id = lax.broadcasted_iota(jnp.int32, (LANES, CH), 0)
        lastrows = []
        for g in range(SSD_GROUPS):
            gs = slice(g * GW, (g + 1) * GW)
            st = st_ref[0, g]
            dsn = dstate[g]
            cbm = _dot_nt(cm[g], bm[g])
            cbt = _dot_nt(bm[g], cm[g])
            dy_g = dy[:, gs]
            yoff = _dot(cm[g], st) * e_exp[:, gs]
            dq = dy_g * e_exp[:, gs]
            dcm_g = _dot_nt(dq, st)
            dstate[g] = _dot_tn(cm[g], dq) + dsn * el_exp[:, gs]
            dw = _dot(bm[g], dsn)
            w_g = w[:, gs]
            dbm_g = _dot_nt(w_g, dsn)
            dww = dw * w_g
            red = dy_g * yoff - dww
            lastrows.append(jnp.sum(dsn * st, axis=0, keepdims=True) * el_exp[:, gs] + jnp.sum(dww, axis=0, keepdims=True))
            dxdtbuf[:, gs] = dw * dec_exp[:, gs]
            dcb = jnp.zeros((CH, CH), F32)
            hs = range(g * HPG, (g + 1) * HPG)
            hsl = {h: slice(h * SSD_HEAD_DIM, (h + 1) * SSD_HEAD_DIM) for h in hs}
            lm = {h: _decay_mats(h, cs, cst, False) for h in hs}
            dm = {h: _dot_nt(dy[:, hsl[h]], xdt[:, hsl[h]]) for h in hs}
            dxd = {h: _dot(cbt * _decay_mats(h, cs, cst, True), dy[:, hsl[h]]) for h in hs}
            for h in hs:
                sl = hsl[h]
                rl = slice((h - g * HPG) * SSD_HEAD_DIM, (h - g * HPG + 1) * SSD_HEAD_DIM)
                dxdt_h = dxdtbuf[:, sl] + dxd[h]
                dxdtbuf[:, sl] = dxdt_h
                dseg = dm[h] * (cbm * lm[h])
                dcb = dcb + dm[h] * lm[h]
                col = jnp.sum(dseg, axis=1, keepdims=True) + jnp.sum(red[:, rl], axis=1, keepdims=True)
                dcs = jnp.where(lane_id == h, col, dcs)
                dcst = jnp.where(row_id == h, -jnp.sum(dseg, axis=0, keepdims=True), dcst)
                ddt = jnp.where(lane_id == h, jnp.sum(dxdt_h * xs[:, sl], axis=1, keepdims=True), ddt)
            dact[:, NB + g * SSD_STATE:NB + (g + 1) * SSD_STATE] = dbm_g + _dot_tn(dcb, cm[g])
            dact[:, NC_ + g * SSD_STATE:NC_ + (g + 1) * SSD_STATE] = dcm_g + _dot(dcb, bm[g])
        dact[:, 0:SSD_WIDTH] = dy * dskx_ref[...] + dxdtbuf[...] * dtx[...]
        dlast = _dot_hi(jnp.concatenate(lastrows, axis=1), expt)
        rows = lax.broadcasted_iota(jnp.int32, (CH, LANES), 0)
        dcs = dcs + _dot_tn_hi(dcst, jnp.eye(LANES, dtype=F32)) + jnp.where(rows == CH - 1, dlast, 0.0)
        dda = _dot_hi(_tri(False), dcs)
        dalog_ref[...] += jnp.sum(dda * dt, axis=0, keepdims=True) * a
        ddt = ddt + dda * a
        ddtr = ddt * _sigmoid(dtr_ref[...] + dtb_ref[...])
        ddtb_ref[...] += jnp.sum(ddtr, axis=0, keepdims=True)
        ddt_ref[...] = ddtr.astype(BF)
        du = dact[...] * _dsilu(u, sg)
        dcb_ref[...] += jnp.sum(du, axis=0, keepdims=True)
        ext = jnp.concatenate([du, ducarry[...]], axis=0)
        ducarry[...] = du[0:8]
        dx = jnp.zeros((CH, SSD_XBC), F32)
        for j in range(SSD_CONV):
            sj = _rows_after(ext, SSD_CONV - 1 - j, CH)
            dx = dx + cw_ref[j:j + 1, :] * sj
            dcw_ref[j:j + 1, :] += jnp.sum(sj * xc, axis=0, keepdims=True)
        dxbc_ref[...] = dx.astype(BF)

    return pl.pallas_call(
        body, name="ssd_bwd", grid=(nc,),
        in_specs=[rspec(SSD_WIDTH), rspec(SSD_WIDTH), rspec(SSD_XBC), rspec(SSD_XBC), rspec(LANES), rspec(SSD_WIDTH),
                  pl.BlockSpec((1, SSD_GROUPS, SSD_STATE, GW), lambda i: (nc - 1 - i, 0, 0, 0)),
                  _const_spec((8, SSD_XBC)), _const_spec((1, LANES)),
                  _const_spec((1, LANES)), _const_spec((1, SSD_WIDTH)), _const_spec((1, SSD_WIDTH))],
        out_specs=[rspec(SSD_WIDTH), rspec(SSD_XBC), rspec(LANES),
                   _const_spec((8, SSD_XBC)), _const_spec((1, SSD_XBC)), _const_spec((1, LANES)),
                   _const_spec((1, LANES)), _const_spec((1, LANES)), _const_spec((1, SSD_WIDTH))],
        out_shape=[S((T, SSD_WIDTH), BF), S((T, SSD_XBC), BF), S((T, LANES), BF),
                   S((8, SSD_XBC), F32), S((1, SSD_XBC), F32), S((1, LANES), F32),
                   S((1, LANES), F32), S((1, LANES), F32), S((1, SSD_WIDTH), F32)],
        scratch_shapes=[pltpu.VMEM((SSD_GROUPS, SSD_STATE, GW), F32), pltpu.VMEM((8, SSD_XBC), F32),
                        pltpu.VMEM((CH, SSD_WIDTH), F32), pltpu.VMEM((CH, SSD_WIDTH), F32),
                        pltpu.VMEM((CH, SSD_WIDTH), F32), pltpu.VMEM((CH, SSD_XBC), F32)],
        compiler_params=_cparams("arbitrary"),
    )(dout, y, u, xbc, dtr, z, states, convw, dtb, alog, dskx, nw)


def _log_gamma(h):
    return float(np.log1p(-np.exp2(np.float32(-5.0 - h)), dtype=np.float32))


def _swap_halves(t):
    n = t.shape[1]
    lane = lax.broadcasted_iota(jnp.int32, t.shape, 1)
    return jnp.where((lane & (RET_QK - 1)) < RET_QK // 2, pltpu.roll(t, n - RET_QK // 2, 1), pltpu.roll(t, RET_QK // 2, 1))


def _rot(t, cos, sin):
    return t * cos + _swap_halves(t) * sin


def _rot_t(d, cos, sin):
    return d * cos + _swap_halves(d * sin)


def _ret_tables():
    lg = jnp.asarray([_log_gamma(h) for h in range(RET_HEADS)], F32)[:, None, None]
    pos = jnp.arange(CH, dtype=F32)
    rel = pos[:, None] - pos[None, :]
    dmask = jnp.where(rel >= 0, jnp.exp(lg * jnp.maximum(rel, 0.0)), 0.0)
    kdec = jnp.exp(lg * (CH - 1.0 - pos)[None, :, None])
    qdec = jnp.exp(lg * (pos + 1.0)[None, :, None])
    rows = jnp.concatenate([jnp.swapaxes(kdec, 1, 2), jnp.swapaxes(qdec, 1, 2), jnp.zeros((RET_HEADS, CH - 2, CH), F32)], axis=1)
    full = lambda t: jnp.broadcast_to(t, (RET_HEADS, CH, CH))
    return jnp.stack([dmask, jnp.swapaxes(dmask, 1, 2), full(kdec), full(qdec), full(rows)], axis=1)


def _ret_consts(h, rt_ref):
    kdec = rt_ref[h, 2][:, 0:RET_QK]
    qdec = rt_ref[h, 3][:, 0:RET_QK]
    return rt_ref[h, 0], rt_ref[h, 1], kdec, qdec, rt_ref[h, 4, 0:1, :], rt_ref[h, 4, 1:2, :], math.exp(_log_gamma(h) * CH)


_RT_SPEC = pl.BlockSpec((RET_HEADS, 5, CH, CH), lambda i: (0, 0, 0, 0))


def _ret_fwd(q, k, v, g, cos, sin, nw, rt):
    T = q.shape[0]
    nc = T // CH

    def body(q_ref, k_ref, v_ref, g_ref, cos_ref, sin_ref, nw_ref, rt_ref, out_ref, st_ref, state):
        i = pl.program_id(0)

        @pl.when(i == 0)
        def _():
            state[...] = jnp.zeros_like(state)

        cosf = jnp.tile(cos_ref[...], (1, RET_QK_W // LANES))
        sinf = jnp.tile(sin_ref[...], (1, RET_QK_W // LANES))
        qr = _rot(q_ref[...].astype(F32), cosf, sinf)
        kr = _rot(k_ref[...].astype(F32), cosf, sinf) * (RET_QK ** -0.5)
        krt = kr.T
        st_ref[0] = state[...]
        qsl = [slice(h * RET_QK, (h + 1) * RET_QK) for h in range(RET_HEADS)]
        vsl = [slice(h * RET_V, (h + 1) * RET_V) for h in range(RET_HEADS)]
        consts = [_ret_consts(h, rt_ref) for h in range(RET_HEADS)]
        scores = [_dot_nt(qr[:, qsl[h]], kr[:, qsl[h]]) * consts[h][0] for h in range(RET_HEADS)]
        cross = [_dot(qr[:, qsl[h]] * consts[h][3], state[h]) for h in range(RET_HEADS)]
        kv = [_dot(krt[qsl[h], :] * consts[h][4], v_ref[:, vsl[h]]) for h in range(RET_HEADS)]
        o_all = [_dot(scores[h], v_ref[:, vsl[h]]) + cross[h] for h in range(RET_HEADS)]
        for h in range(RET_HEADS):
            state[h] = state[h] * consts[h][6] + kv[h]
        for h in range(RET_HEADS):
            sl = slice(h * RET_V, (h + 1) * RET_V)
            o = o_all[h]
            gf = g_ref[:, sl].astype(F32)
            out_ref[:, sl] = (o * _rstd(o) * nw_ref[:, sl] * (gf * _sigmoid(gf))).astype(BF)

    return pl.pallas_call(
        body, name="ret_fwd", grid=(nc,),
        in_specs=[_row_spec(CH, RET_QK_W), _row_spec(CH, RET_QK_W), _row_spec(CH, RET_V_W), _row_spec(CH, RET_V_W),
                  _row_spec(CH, LANES), _row_spec(CH, LANES), _const_spec((1, RET_V_W)), _RT_SPEC],
        out_specs=[_row_spec(CH, RET_V_W), pl.BlockSpec((1, RET_HEADS, RET_QK, RET_V), lambda i: (i, 0, 0, 0))],
        out_shape=[S((T, RET_V_W), BF), S((nc, RET_HEADS, RET_QK, RET_V), F32)],
        scratch_shapes=[pltpu.VMEM((RET_HEADS, RET_QK, RET_V), F32)],
        compiler_params=_cparams("arbitrary"),
    )(q, k, v, g, cos, sin, nw, rt)


def _ret_bwd(dout, q, k, v, g, states, cos, sin, nw, rt):
    T = q.shape[0]
    nc = T // CH
    rev = lambda i: (nc - 1 - i, 0)
    rspec = lambda w: pl.BlockSpec((CH, w), rev)

    def body(do_ref, q_ref, k_ref, v_ref, g_ref, st_ref, cos_ref, sin_ref, nw_ref, rt_ref,
             dq_ref, dk_ref, dv_ref, dg_ref, dnw_ref, dstate, dqbuf, dkbuf):
        i = pl.program_id(0)

        @pl.when(i == 0)
        def _():
            dstate[...] = jnp.zeros_like(dstate)
            dnw_ref[...] = jnp.zeros_like(dnw_ref)

        cosf = jnp.tile(cos_ref[...], (1, RET_QK_W // LANES))
        sinf = jnp.tile(sin_ref[...], (1, RET_QK_W // LANES))
        qr = _rot(q_ref[...].astype(F32), cosf, sinf)
        kr = _rot(k_ref[...].astype(F32), cosf, sinf) * (RET_QK ** -0.5)
        qrt = qr.T
        heads = range(RET_HEADS)
        qsl = [slice(h * RET_QK, (h + 1) * RET_QK) for h in heads]
        vsl = [slice(h * RET_V, (h + 1) * RET_V) for h in heads]
        do_all = []
        consts = [_ret_consts(h, rt_ref) for h in heads]
        scores = [_dot_nt(qr[:, qsl[h]], kr[:, qsl[h]]) * consts[h][0] for h in heads]
        scores_t = [_dot_nt(kr[:, qsl[h]], qr[:, qsl[h]]) * consts[h][1] for h in heads]
        cross = [_dot(qr[:, qsl[h]] * consts[h][3], st_ref[0, h]) for h in heads]
        o_all = [_dot(scores[h], v_ref[:, vsl[h]]) + cross[h] for h in heads]
        for h in heads:
            o = o_all[h]
            rr = _rstd(o)
            of = o * rr
            gf = g_ref[:, vsl[h]].astype(F32)
            sgg = _sigmoid(gf)
            d_h = do_ref[:, vsl[h]].astype(F32)
            nw_h = nw_ref[:, vsl[h]]
            dg_ref[:, vsl[h]] = (d_h * of * nw_h * _dsilu(gf, sgg)).astype(BF)
            dt_ = d_h * (gf * sgg)
            dnw_ref[:, vsl[h]] += jnp.sum(dt_ * of, axis=0, keepdims=True)
            dof = dt_ * nw_h
            do_all.append(rr * dof - o * (rr * rr * rr) * jnp.mean(dof * o, axis=-1, keepdims=True))
        dsc = [_dot_nt(do_all[h], v_ref[:, vsl[h]]) * consts[h][0] for h in heads]
        dsc_t = [_dot_nt(v_ref[:, vsl[h]], do_all[h]) * consts[h][1] for h in heads]
        dv_a = [_dot(scores_t[h], do_all[h]) for h in heads]
        dv_b = [_dot(kr[:, qsl[h]] * consts[h][2], dstate[h]) for h in heads]
        dq_a = [_dot(dsc[h], kr[:, qsl[h]]) for h in heads]
        dq_b = [_dot_nt(do_all[h], st_ref[0, h]) * consts[h][3] for h in heads]
        dk_a = [_dot(dsc_t[h], qr[:, qsl[h]]) for h in heads]
        dk_b = [_dot_nt(v_ref[:, vsl[h]], dstate[h]) * consts[h][2] for h in heads]
        dst = [_dot(qrt[qsl[h], :] * consts[h][5], do_all[h]) for h in heads]
        for h in heads:
            dv_ref[:, vsl[h]] = (dv_a[h] + dv_b[h]).astype(BF)
            dqbuf[:, qsl[h]] = dq_a[h] + dq_b[h]
            dkbuf[:, qsl[h]] = dk_a[h] + dk_b[h]
            dstate[h] = dstate[h] * consts[h][6] + dst[h]
        dq_ref[...] = _rot_t(dqbuf[...], cosf, sinf).astype(BF)
        dk_ref[...] = (_rot_t(dkbuf[...], cosf, sinf) * (RET_QK ** -0.5)).astype(BF)

    return pl.pallas_call(
        body, name="ret_bwd", grid=(nc,),
        in_specs=[rspec(RET_V_W), rspec(RET_QK_W), rspec(RET_QK_W), rspec(RET_V_W), rspec(RET_V_W),
                  pl.BlockSpec((1, RET_HEADS, RET_QK, RET_V), lambda i: (nc - 1 - i, 0, 0, 0)),
                  rspec(LANES), rspec(LANES), _const_spec((1, RET_V_W)), _RT_SPEC],
        out_specs=[rspec(RET_QK_W), rspec(RET_QK_W), rspec(RET_V_W), rspec(RET_V_W), _const_spec((1, RET_V_W))],
        out_shape=[S((T, RET_QK_W), BF), S((T, RET_QK_W), BF), S((T, RET_V_W), BF), S((T, RET_V_W), BF),
                   S((1, RET_V_W), F32)],
        scratch_shapes=[pltpu.VMEM((RET_HEADS, RET_QK, RET_V), F32), pltpu.VMEM((CH, RET_QK_W), F32),
                        pltpu.VMEM((CH, RET_QK_W), F32)],
        compiler_params=_cparams("arbitrary"),
    )(dout, q, k, v, g, states, cos, sin, nw, rt)


_HBM = pl.BlockSpec(memory_space=pltpu.HBM)
_SEM = pl.BlockSpec(memory_space=pltpu.SEMAPHORE)
_EFFECT = pltpu.SideEffectType.DATAFLOW_SIDE_EFFECTING


ALL_PEERS = tuple(range(1, N_DEV))
ONE_PER_CHIP = (1, 2, 4, 6)
OTHER_CHIPS = (2, 4, 6)


def _split_copies(buf_refs, land_refs, send_sems, recv_sems, same, to_me, ks):
    x, y, c = lax.axis_index("x"), lax.axis_index("y"), lax.axis_index("c")
    me = 4 * x + 2 * y + c
    cps = []
    for ki, k in enumerate(ks):
        px = 1 - x if k & 4 else x
        py = 1 - y if k & 2 else y
        pc = 1 - c if k & 1 else c
        p = 4 * px + 2 * py + pc
        for b in range(len(buf_refs)):
            s = b * len(ks) + ki
            cps.append(pltpu.make_async_remote_copy(
                src_ref=buf_refs[b] if same else buf_refs[b].at[p], dst_ref=land_refs[b].at[me if to_me else p],
                send_sem=send_sems.at[s], recv_sem=recv_sems.at[s], device_id=(px, py, pc), device_id_type=pl.DeviceIdType.MESH))
    return cps


def _exchange_start(bufs, name, same, ks=ALL_PEERS):
    nb = len(bufs)
    ns = nb * len(ks)
    lands = [lax.empty((N_DEV,) + tuple(b.shape if same else b.shape[1:]), b.dtype) for b in bufs]

    def body(*refs):
        buf_refs, land_refs = refs[:nb], refs[nb:2 * nb]
        send_sems, recv_sems = refs[2 * nb], refs[2 * nb + 1]
        token = refs[-1]
        for cp in _split_copies(buf_refs, land_refs, send_sems, recv_sems, same, True, ks):
            cp.start()
        token[...] = jnp.zeros_like(token)

    hbm = lambda a: pltpu.with_memory_space_constraint(a, pltpu.HBM)
    out = pl.pallas_call(
        body, name=name,
        out_shape=(pltpu.SemaphoreType.DMA((ns,)), pltpu.SemaphoreType.DMA((ns,)),
                   *[pltpu.HBM(a.shape, a.dtype) for a in list(bufs) + lands], S((8, LANES), F32)),
        in_specs=[_HBM] * (2 * nb), out_specs=(_SEM, _SEM, *[_HBM] * (2 * nb), pl.BlockSpec(memory_space=pltpu.VMEM)),
        input_output_aliases={i: 2 + i for i in range(2 * nb)},
        compiler_params=pltpu.CompilerParams(has_side_effects=_EFFECT),
    )(*[hbm(a) for a in list(bufs) + lands])
    return out[0], out[1], list(out[2:2 + nb]), list(out[2 + nb:2 + 2 * nb]), out[-1]


def _exchange_wait(started, after, name, same, ks=ALL_PEERS):
    send_sems, recv_sems, bufs, lands, _ = started
    nb = len(bufs)
    after = list(after) if isinstance(after, (list, tuple)) else [after]

    def body(*refs):
        buf_refs, land_refs = refs[:nb], refs[nb:2 * nb]
        s_sems, r_sems = refs[2 * nb], refs[2 * nb + 1]
        for cp in _split_copies(buf_refs, land_refs, s_sems, r_sems, same, False, ks):
            cp.wait_send()
            cp.wait_recv()

    out = pl.pallas_call(
        body, name=name,
        out_shape=tuple(pltpu.HBM(a.shape, a.dtype) for a in bufs + lands),
        in_specs=[_HBM] * (2 * nb) + [_SEM, _SEM] + [pl.BlockSpec(memory_space=pl.ANY)] * len(after),
        out_specs=tuple([_HBM] * (2 * nb)),
        input_output_aliases={i: i for i in range(2 * nb)},
        compiler_params=pltpu.CompilerParams(has_side_effects=_EFFECT),
    )(*bufs, *lands, send_sems, recv_sems, *after)
    return list(out[:nb]), list(out[nb:])


def _forward_copies(land_ref, send_sems, recv_sems, sending):
    x, y, c = lax.axis_index("x"), lax.axis_index("y"), lax.axis_index("c")
    cps = []
    for ki, k in enumerate(OTHER_CHIPS):
        px = 1 - x if k & 4 else x
        py = 1 - y if k & 2 else y
        q = 4 * px + 2 * py + (c if sending else 1 - c)
        cps.append(pltpu.make_async_remote_copy(
            src_ref=land_ref.at[q], dst_ref=land_ref.at[q], send_sem=send_sems.at[ki], recv_sem=recv_sems.at[ki],
            device_id=(x, y, 1 - c), device_id_type=pl.DeviceIdType.MESH))
    return cps


def _forward_start(land, name):
    def body(land_ref, send_sems, recv_sems, land_thru, token):
        for cp in _forward_copies(land_ref, send_sems, recv_sems, True):
            cp.start()
        token[...] = jnp.zeros_like(token)

    n = len(OTHER_CHIPS)
    out = pl.pallas_call(
        body, name=name,
        out_shape=(pltpu.SemaphoreType.DMA((n,)), pltpu.SemaphoreType.DMA((n,)), pltpu.HBM(land.shape, land.dtype),
                   S((8, LANES), F32)),
        in_specs=[_HBM], out_specs=(_SEM, _SEM, _HBM, pl.BlockSpec(memory_space=pltpu.VMEM)),
        input_output_aliases={0: 2},
        compiler_params=pltpu.CompilerParams(has_side_effects=_EFFECT),
    )(pltpu.with_memory_space_constraint(land, pltpu.HBM))
    return out


def _forward_wait(started, after, name):
    send_sems, recv_sems, land, _ = started
    after = list(after) if isinstance(after, (list, tuple)) else [after]

    def body(land_ref, s_sems, r_sems, *rest):
        for cp in _forward_copies(land_ref, s_sems, r_sems, False):
            cp.wait_send()
            cp.wait_recv()

    return pl.pallas_call(
        body, name=name, out_shape=pltpu.HBM(land.shape, land.dtype),
        in_specs=[_HBM, _SEM, _SEM] + [pl.BlockSpec(memory_space=pl.ANY)] * len(after), out_specs=_HBM,
        input_output_aliases={0: 0},
        compiler_params=pltpu.CompilerParams(has_side_effects=_EFFECT),
    )(land, send_sems, recv_sems, *after)


def _sum_slabs(recv, name):
    n, R, _ = recv.shape

    def body(r_ref, o_ref):
        g = r_ref[0].astype(F32)
        for s in range(1, n):
            g = g + r_ref[s].astype(F32)
        o_ref[...] = g

    return pl.pallas_call(body, name=name, out_shape=S((R, LANES), F32))(recv)


def _adamw(recv, w, m, v, name, tr, tc=None):
    n, R, C = recv.shape
    c1 = 1.0 - ADAM_B1 ** ADAM_STEP
    c2 = 1.0 - ADAM_B2 ** ADAM_STEP

    def body(r_ref, w_ref, m_ref, v_ref, g_out, d_out, m_out, v_out):
        g = r_ref[0].astype(F32)
        for s in range(1, n):
            g = g + r_ref[s].astype(F32)
        mm = ADAM_B1 * m_ref[...] + (1.0 - ADAM_B1) * g
        vv = ADAM_B2 * v_ref[...] + (1.0 - ADAM_B2) * (g * g)
        g_out[...] = g
        m_out[...] = mm
        v_out[...] = vv
        d_out[...] = -ADAM_LR * ((mm / c1) / (jnp.sqrt(vv / c2) + ADAM_EPS) + ADAM_WD * w_ref[...])

    tc = C if tc is None else tc
    spec = pl.BlockSpec((tr, tc), lambda i, j: (i, j))
    return pl.pallas_call(
        body, name=name, grid=(R // tr, C // tc),
        in_specs=[pl.BlockSpec((n, tr, tc), lambda i, j: (0, i, j)), spec, spec, spec],
        out_specs=[spec] * 4, out_shape=[S((R, C), F32)] * 4,
        compiler_params=_cparams("parallel", "parallel"),
    )(recv, w, m, v)


def _pack(parts, rows):
    cols = []
    for p in parts:
        f = p.reshape(-1)
        cols.append(jnp.pad(f, (0, (-f.shape[0]) % LANES)))
    flat = jnp.concatenate(cols)
    return jnp.pad(flat, (0, rows * LANES - flat.shape[0])).reshape(rows, LANES)


def _unpack(buf, shapes):
    flat = buf.reshape(-1)
    out, o = [], 0
    for shp in shapes:
        n = int(np.prod(shp))
        out.append(flat[o:o + n].reshape(shp))
        o += n + (-n) % LANES
    return out


SMALL_ROWS = 200
CONV_ROWS = 16


def kernel(x, pre_mix_norm_w, w_in, ssd_conv_w, ssd_conv_b, ssd_dt_bias, ssd_a_log, ssd_d, ssd_norm_w, ret_norm_w, w_out, post_mix_norm_w, pre_ffn_norm_w, w_up, ffn_conv_w, ffn_conv_b, w_down, post_ffn_norm_w, loss_target, m_pre_mix_norm_w, m_w_in, m_ssd_conv_w, m_ssd_conv_b, m_ssd_dt_bias, m_ssd_a_log, m_ssd_d, m_ssd_norm_w, m_ret_norm_w, m_w_out, m_post_mix_norm_w, m_pre_ffn_norm_w, m_w_up, m_ffn_conv_w, m_ffn_conv_b, m_w_down, m_post_ffn_norm_w, v_pre_mix_norm_w, v_w_in, v_ssd_conv_w, v_ssd_conv_b, v_ssd_dt_bias, v_ssd_a_log, v_ssd_d, v_ssd_norm_w, v_ret_norm_w, v_w_out, v_post_mix_norm_w, v_pre_ffn_norm_w, v_w_up, v_ffn_conv_w, v_ffn_conv_b, v_w_down, v_post_ffn_norm_w):
    T = x.shape[1]
    xi, tgt = x[0], loss_target[0]
    me = 4 * lax.axis_index("x") + 2 * lax.axis_index("y") + lax.axis_index("c")
    n_in, n_up = w_in.shape[2], w_up.shape[2]
    n_out, n_down = w_out.shape[1], w_down.shape[1]
    n_sc, n_fc = ssd_conv_w.shape[2], ffn_conv_w.shape[2]

    def after(token, value):
        return value * (1.0 + token[0, 0])

    def finish(started, after_value, name, same):
        bufs, lands = _exchange_wait(started, after_value, name, same)
        own = [b if same else lax.dynamic_index_in_dim(b, me, 0, keepdims=False) for b in bufs]
        return [lax.dynamic_update_index_in_dim(l, o, me, 0) for l, o in zip(lands, own)]

    tr_ = lambda w: jnp.transpose(w[0])
    gat_in = _exchange_start([tr_(w_in).astype(BF)], "gather_in_start", True, ONE_PER_CHIP)
    gat_conv = _exchange_start([after(gat_in[4], _pack([ssd_conv_w, ffn_conv_w], CONV_ROWS))], "gather_conv_start", True)
    pad_h = lambda p: jnp.pad(p, ((0, 0), (0, LANES - SSD_HEADS)))
    dtb, alog = pad_h(ssd_dt_bias), pad_h(ssd_a_log)
    dskx = jnp.repeat(ssd_d, SSD_HEAD_DIM, axis=1)
    inv = ROPE_BASE ** (-jnp.arange(0, RET_QK, 2, dtype=F32) / RET_QK)
    ang = after(gat_conv[4], jnp.arange(T, dtype=F32)[:, None]) * inv[None, :]
    cs_, sn_ = jnp.cos(ang), jnp.sin(ang)
    cos = jnp.concatenate([cs_, cs_, cs_, cs_], axis=1)
    sin = jnp.concatenate([-sn_, sn_, -sn_, sn_], axis=1)
    rtab = _ret_tables()
    ws = [pre_mix_norm_w, ssd_conv_w, ssd_conv_b, ssd_dt_bias, ssd_a_log, ssd_d, ssd_norm_w, ret_norm_w, post_mix_norm_w,
          pre_ffn_norm_w, ffn_conv_w, ffn_conv_b, post_ffn_norm_w]
    ms = [m_pre_mix_norm_w, m_ssd_conv_w, m_ssd_conv_b, m_ssd_dt_bias, m_ssd_a_log, m_ssd_d, m_ssd_norm_w, m_ret_norm_w,
          m_post_mix_norm_w, m_pre_ffn_norm_w, m_ffn_conv_w, m_ffn_conv_b, m_post_ffn_norm_w]
    vs = [v_pre_mix_norm_w, v_ssd_conv_w, v_ssd_conv_b, v_ssd_dt_bias, v_ssd_a_log, v_ssd_d, v_ssd_norm_w, v_ret_norm_w,
          v_post_mix_norm_w, v_pre_ffn_norm_w, v_ffn_conv_w, v_ffn_conv_b, v_post_ffn_norm_w]
    out_shapes = [t.shape for t in ws]
    small_wmv = [_pack(t, SMALL_ROWS) for t in (ws, ms, vs)]
    shard_in, land_in = _exchange_wait(gat_in, [cos, sin, rtab] + small_wmv, "gather_in_wait", True, ONE_PER_CHIP)
    fwd_in_ = _forward_start(land_in[0], "gather_in_forward")
    gat_rest = _exchange_start([after(fwd_in_[3], w).astype(BF) for w in (w_out[0], tr_(w_up), w_down[0])], "gather_rest_start", True)
    g_in = lax.dynamic_update_index_in_dim(_forward_wait(fwd_in_, gat_rest[4], "gather_in_forward_wait"), shard_in[0], me, 0)
    wt = g_in.reshape(N_DEV * n_in, D_MODEL)
    wdt = jnp.pad(wt[O_DT:O_Q], ((0, LANES - SSD_HEADS), (0, 0)))

    h, z, xbc, q, k, v, g, dtr = _fwd_in(xi, pre_mix_norm_w, wt, wdt)
    yr, rst = _ret_fwd(q, k, v, g, cos, sin, ret_norm_w, rtab)
    gconv, = finish(gat_conv, yr, "gather_conv_wait", True)
    convs = [_unpack(gconv[d], [(SSD_CONV, n_sc), (FFN_CONV, n_fc)]) for d in range(N_DEV)]
    scw = jnp.pad(jnp.concatenate([c[0] for c in convs], axis=1), ((0, 8 - SSD_CONV), (0, 0)))
    fcw = jnp.pad(jnp.concatenate([c[1] for c in convs], axis=1), ((0, 8 - FFN_CONV), (0, 0)))
    ys, ypre, uconv, sst = _ssd_fwd(xbc, dtr, z, scw, ssd_conv_b, dtb, alog, dskx, ssd_norm_w)
    g_out, g_up, g_down = finish(gat_rest, [yr, ys], "gather_rest_wait", True)
    wout = g_out.reshape(N_DEV * n_out, D_MODEL)
    wup = g_up.reshape(N_DEV * n_up, D_MODEL)
    wdown = g_down.reshape(N_DEV * n_down, D_MODEL)
    y, x1, h2, graw, val = _fwd_mid(ys, yr, xi, wout, post_mix_norm_w, pre_ffn_norm_w, wup)
    a, dfb, dval, dgate, dx2, lossb, d_pff, d_fcb = _ffn_tail(graw, val, x1, tgt, fcw, ffn_conv_b, wdown, post_ffn_norm_w)
    gdown = _matmul_tn(a, dfb, "dw_down")
    sc_down = _exchange_start([gdown.reshape(N_DEV, n_down, D_MODEL)], "scatter_down_start", False)
    dgraw, dx1, dyb, dys, dyr, d_fcw, d_pf, d_pm = _ffn_bwd(dgate, dval, graw, x1, dx2, y, after(sc_down[4], fcw), wup,
                                                         pre_ffn_norm_w, post_mix_norm_w, wout)
    gup = _matmul_tn(dval, h2, "dw_up_v", stack=(2, 1, _matmul_tn(dgraw, h2, "dw_up_g", stack=(2, 0, None))))
    gout = jnp.concatenate(_matmul_tn_group([ys, yr], dyb, "dw_out"), axis=0)
    sc_mid = _exchange_start([gup.reshape(N_DEV, n_up, D_MODEL), gout.reshape(N_DEV, n_out, D_MODEL)],
                             "scatter_mid_start", False)
    dz, dxbc, ddt, d_scw, d_scb, d_dtb, d_alog, d_dsk, d_snw = _ssd_bwd(dys, ypre, uconv, xbc, dtr, z, sst, after(sc_mid[4], scw),
                                                                      dtb, alog, dskx, ssd_norm_w)
    dq, dk, dv, dg, d_rnw = _ret_bwd(dyr, q, k, v, g, rst, cos, sin, ret_norm_w, rtab)
    g_q, g_k, g_v, g_g = _matmul_tn_group([dq, dk, dv, dg], h, "dw_ret")
    g_z, g_xbc, g_dt = _matmul_tn_group([dz, dxbc, ddt], h, "dw_ssd")
    gin = jnp.concatenate([g_z, g_xbc, g_dt[:SSD_HEADS], g_q, g_k, g_v, g_g], axis=0)
    sc_in = _exchange_start([gin.reshape(N_DEV, n_in, D_MODEL)], "scatter_in_start", False)
    gx, d_w0 = _in_bwd(dz, dxbc, dq, dk, dv, dg, ddt, xi, dx1, after(sc_in[4], pre_mix_norm_w), wt, wdt)
    small_full = [d_w0, d_scw[:SSD_CONV], d_scb, d_dtb[:, :SSD_HEADS], d_alog[:, :SSD_HEADS], d_dsk[:, :SSD_HEADS], d_snw, d_rnw,
                  d_pm, d_pf, d_fcw[:FFN_CONV], d_fcb, d_pff, lossb[0:1, 0:1]]
    gat_small = _exchange_start([_pack(small_full, SMALL_ROWS)], "gather_small_start", True)
    r_down, = finish(sc_down, [gx, gat_small[4]], "scatter_down_wait", False)
    r_up, r_out = finish(sc_mid, r_down, "scatter_mid_wait", False)
    per_w = [None] * 4
    per_w[3] = _adamw(r_down, w_down[0], m_w_down[0], v_w_down[0], "adamw_down", n_down)
    per_w[2] = [jnp.transpose(t) for t in _adamw(r_up, tr_(w_up), tr_(m_w_up), tr_(v_w_up), "adamw_up", n_up, 256)]
    per_w[1] = _adamw(r_out, w_out[0], m_w_out[0], v_w_out[0], "adamw_out", n_out)
    r_in, = finish(sc_in, per_w[1][0], "scatter_in_wait", False)
    per_w[0] = [jnp.transpose(t) for t in _adamw(r_in, tr_(w_in), tr_(m_w_in), tr_(v_w_in), "adamw_in", n_in, 256)]
    big = [[per_w[i][kind][None] for i in range(4)] for kind in range(4)]

    full_shapes = [t.shape for t in small_full]
    gs = _sum_slabs(finish(gat_small, per_w[0][0], "gather_small_wait", True)[0], "sum_small")
    gfull = _unpack(gs, full_shapes)
    gfull[1] = lax.dynamic_slice_in_dim(gfull[1], me * n_sc, n_sc, axis=1)
    gfull[10] = lax.dynamic_slice_in_dim(gfull[10], me * n_fc, n_fc, axis=1)
    loss = gfull.pop()[0, 0]
    small = _adamw(_pack(gfull, SMALL_ROWS)[None], *small_wmv, "adamw_small", SMALL_ROWS)
    small = [_unpack(b, out_shapes) for b in small]

    order = {"pre_mix_norm_w": ("s", 0), "w_in": ("b", 0), "ssd_conv_w": ("s", 1), "ssd_conv_b": ("s", 2),
             "ssd_dt_bias": ("s", 3), "ssd_a_log": ("s", 4), "ssd_d": ("s", 5), "ssd_norm_w": ("s", 6), "ret_norm_w": ("s", 7),
             "w_out": ("b", 1), "post_mix_norm_w": ("s", 8), "pre_ffn_norm_w": ("s", 9), "w_up": ("b", 2),
             "ffn_conv_w": ("s", 10), "ffn_conv_b": ("s", 11), "w_down": ("b", 3), "post_ffn_norm_w": ("s", 12)}
    outs = [loss, gx[None]]
    for kind in range(4):
        for name, (grp, idx) in order.items():
            outs.append(big[kind][idx] if grp == "b" else small[kind][idx])
    return tuple(outs)
```
